```python
import math
import jax, jax.numpy as jnp
from jax import lax
import numpy as np

D_MODEL = 1024
BATCH = 8
SEQ = 8192
DEPTH = 2

PLE_DIM = 256
NORM_EPS = 1e-6
S5_WIDTH = 512
S5_GROUP = 16
S5_GROUPS = S5_WIDTH // S5_GROUP
S5_STATE = 64
S5_DT_MIN = 1e-3
S5_DT_MAX = 1e-1
LRU_WIDTH = 1280
LRU_HEADS = 10
LRU_HEAD_DIM = LRU_WIDTH // LRU_HEADS
LRU_C = 8.0
CONV_WIDTH = 4
IN_SPLITS = (
    S5_WIDTH,
    2 * S5_WIDTH,
    2 * S5_WIDTH + LRU_WIDTH,
    2 * S5_WIDTH + 2 * LRU_WIDTH,
    2 * S5_WIDTH + 2 * LRU_WIDTH + D_MODEL,
)
IN_COLS = 2 * S5_WIDTH + 2 * LRU_WIDTH + 2 * D_MODEL

kernel_name = 'hybrid_s5_rglru_gated_parallel'


def rms_norm(x, g):
    xf = x.astype(jnp.float32)
    y = xf * lax.rsqrt(jnp.mean(xf * xf, axis=-1, keepdims=True) + NORM_EPS)
    return (y * g.astype(jnp.float32)).astype(x.dtype)


def _linear_combine(e1, e2):
    a1, b1 = e1
    a2, b2 = e2
    return (a1 * a2, a2 * b1 + b2)


def s5_ssm(u, a_re, a_im, log_dt, b_re, b_im, c_re, c_im, d_skip):
    f32 = jnp.float32
    bsz, seqlen, _ = u.shape
    uf = u.astype(f32)
    ug = uf.reshape(bsz, seqlen, S5_GROUPS, S5_GROUP)
    lam = lax.complex(a_re.astype(f32), a_im.astype(f32))
    dt = jnp.exp(log_dt.astype(f32))[:, None]
    a_bar = jnp.exp(lam * dt)
    zoh = (a_bar - 1.0) / lam
    b = lax.complex(b_re.astype(f32), b_im.astype(f32))
    b_bar = zoh[..., None] * b
    bu = lax.complex(jnp.einsum('blgc,gnc->lbgn', ug, jnp.real(b_bar)),
                     jnp.einsum('blgc,gnc->lbgn', ug, jnp.imag(b_bar)))
    a_seq = jnp.broadcast_to(a_bar[None, None], (seqlen, 1, S5_GROUPS, S5_STATE))
    _, states = lax.associative_scan(_linear_combine, (a_seq, bu), axis=0)
    y = (jnp.einsum('lbgn,gcn->blgc', jnp.real(states), c_re.astype(f32))
         - jnp.einsum('lbgn,gcn->blgc', jnp.imag(states), c_im.astype(f32)))
    y = y.reshape(bsz, seqlen, S5_WIDTH) + d_skip.astype(f32) * uf
    return y.astype(u.dtype)


def causal_depthwise_conv(x, w, b):
    y = lax.conv_general_dilated(
        x, w[:, None, :].astype(x.dtype), window_strides=(1,),
        padding=[(CONV_WIDTH - 1, 0)],
        dimension_numbers=('NWC', 'WIO', 'NWC'),
        feature_group_count=x.shape[-1])
    return y + b


def rg_lru(x, w_a, b_a, w_x, b_x, lam):
    f32 = jnp.float32
    bsz, seqlen, _ = x.shape
    xf = x.astype(f32)
    xh = xf.reshape(bsz, seqlen, LRU_HEADS, LRU_HEAD_DIM)
    r = jax.nn.sigmoid(jnp.einsum('blhi,hij->blhj', xh, w_a.astype(f32)).reshape(bsz, seqlen, LRU_WIDTH)
                       + b_a.astype(f32))
    i = jax.nn.sigmoid(jnp.einsum('blhi,hij->blhj', xh, w_x.astype(f32)).reshape(bsz, seqlen, LRU_WIDTH)
                       + b_x.astype(f32))
    log_a = -LRU_C * r * jax.nn.softplus(-lam.astype(f32))
    a = jnp.exp(log_a)
    mult = jnp.sqrt(-jnp.expm1(2.0 * log_a))
    _, h = lax.associative_scan(_linear_combine, (a, mult * (i * xf)), axis=1)
    return h.astype(x.dtype)


def _fwd_setup_inputs(seed: int = 0) -> dict:
    key = jax.random.key(seed)
    ks = jax.random.split(key, 32)
    f32 = jnp.float32

    def nrm(k, shape, scale):
        return jax.random.normal(k, shape, f32) * scale

    L, D, G, N, C, HD = DEPTH, D_MODEL, S5_GROUPS, S5_STATE, S5_GROUP, LRU_HEAD_DIM
    x = nrm(ks[0], (BATCH, SEQ, D), 1.0)
    p = nrm(ks[1], (DEPTH, BATCH, SEQ, PLE_DIM), 1.0)
    g_pre = 1.0 + nrm(ks[2], (L, D), 0.05)
    w_in = nrm(ks[3], (L, D, IN_COLS), D ** -0.5)
    s5_a_re = -0.5 + nrm(ks[4], (L, G, N), 0.01)
    s5_a_im = math.pi * jnp.arange(N, dtype=f32)[None, None, :] + nrm(ks[5], (L, G, N), 0.01)
    s5_log_dt = jax.random.uniform(ks[6], (L, G), f32, math.log(S5_DT_MIN), math.log(S5_DT_MAX))
    s5_b_re = nrm(ks[7], (L, G, N, C), C ** -0.5)
    s5_b_im = nrm(ks[8], (L, G, N, C), C ** -0.5)
    s5_c_re = nrm(ks[9], (L, G, C, N), N ** -0.5)
    s5_c_im = nrm(ks[10], (L, G, C, N), N ** -0.5)
    s5_d = nrm(ks[11], (L, S5_WIDTH), 1.0)
    w_glu = nrm(ks[12], (L, S5_WIDTH, 2 * S5_WIDTH), S5_WIDTH ** -0.5)
    w_bs = nrm(ks[13], (L, S5_WIDTH, D), S5_WIDTH ** -0.5)
    conv_w = nrm(ks[14], (L, CONV_WIDTH, LRU_WIDTH), CONV_WIDTH ** -0.5)
    conv_b = nrm(ks[15], (L, LRU_WIDTH), 0.01)
    lru_w_a = nrm(ks[16], (L, LRU_HEADS, HD, HD), HD ** -0.5)
    lru_b_a = nrm(ks[17], (L, LRU_WIDTH), 0.01)
    lru_w_x = nrm(ks[18], (L, LRU_HEADS, HD, HD), HD ** -0.5)
    lru_b_x = nrm(ks[19], (L, LRU_WIDTH), 0.01)
    a_c = jax.random.uniform(ks[20], (L, LRU_WIDTH), f32, 0.9, 0.999)
    sig = a_c ** (1.0 / LRU_C)
    lru_lambda = jnp.log(sig) - jnp.log1p(-sig)
    w_bl = nrm(ks[21], (L, LRU_WIDTH, D), LRU_WIDTH ** -0.5)
    w_out = nrm(ks[22], (L, D, D), D ** -0.5)
    g_post = 1.0 + nrm(ks[23], (L, D), 0.05)
    w_ple = nrm(ks[24], (L, PLE_DIM, D), PLE_DIM ** -0.5)
    w_ple_gate = nrm(ks[25], (L, D, D), D ** -0.5)
    return {
        'x': x, 'p': p, 'g_pre': g_pre, 'w_in': w_in,
        's5_a_re': s5_a_re, 's5_a_im': s5_a_im, 's5_log_dt': s5_log_dt,
        's5_b_re': s5_b_re, 's5_b_im': s5_b_im, 's5_c_re': s5_c_re, 's5_c_im': s5_c_im,
        's5_d': s5_d, 'w_glu': w_glu, 'w_bs': w_bs,
        'conv_w': conv_w, 'conv_b': conv_b,
        'lru_w_a': lru_w_a, 'lru_b_a': lru_b_a, 'lru_w_x': lru_w_x, 'lru_b_x': lru_b_x,
        'lru_lambda': lru_lambda, 'w_bl': w_bl, 'w_out': w_out, 'g_post': g_post,
        'w_ple': w_ple, 'w_ple_gate': w_ple_gate,
    }


def _fwd_reference(x, p, g_pre, w_in, s5_a_re, s5_a_im, s5_log_dt, s5_b_re, s5_b_im,
              s5_c_re, s5_c_im, s5_d, w_glu, w_bs, conv_w, conv_b,
              lru_w_a, lru_b_a, lru_w_x, lru_b_x, lru_lambda, w_bl, w_out, g_post,
              w_ple, w_ple_gate):
    for i in range(DEPTH):
        h = rms_norm(x, g_pre[i])
        proj = h @ w_in[i]
        s5_x, s5_g, lru_x, lru_g, gate_s, gate_l = jnp.split(proj, IN_SPLITS, axis=-1)

        y_s = s5_ssm(s5_x, s5_a_re[i], s5_a_im[i], s5_log_dt[i], s5_b_re[i], s5_b_im[i],
                     s5_c_re[i], s5_c_im[i], s5_d[i])
        glu_a, glu_b = jnp.split(jax.nn.gelu(y_s) @ w_glu[i], 2, axis=-1)
        y_s = glu_a * jax.nn.sigmoid(glu_b) * jax.nn.silu(s5_g)
        z_s = y_s @ w_bs[i]

        c = causal_depthwise_conv(lru_x, conv_w[i], conv_b[i])
        y_l = rg_lru(c, lru_w_a[i], lru_b_a[i], lru_w_x[i], lru_b_x[i], lru_lambda[i])
        z_l = (y_l * jax.nn.silu(lru_g)) @ w_bl[i]

        merged = jax.nn.sigmoid(gate_s) * z_s + jax.nn.sigmoid(gate_l) * z_l
        x = x + rms_norm(merged @ w_out[i], g_post[i])

        x = x + (p[i] @ w_ple[i]) * jax.nn.sigmoid(x @ w_ple_gate[i])
    return x


import jax as _jax
import jax.numpy as _jnp

TWIN_FORMAT = 'train_step'
FWD_PARAMS = ['x', 'p', 'g_pre', 'w_in', 's5_a_re', 's5_a_im', 's5_log_dt', 's5_b_re', 's5_b_im', 's5_c_re', 's5_c_im', 's5_d', 'w_glu', 'w_bs', 'conv_w', 'conv_b', 'lru_w_a', 'lru_b_a', 'lru_w_x', 'lru_b_x', 'lru_lambda', 'w_bl', 'w_out', 'g_post', 'w_ple', 'w_ple_gate']
TWIN_WEIGHTS = ['g_pre', 'w_in', 's5_a_re', 's5_a_im', 's5_log_dt', 's5_b_re', 's5_b_im', 's5_c_re', 's5_c_im', 's5_d', 'w_glu', 'w_bs', 'conv_w', 'conv_b', 'lru_w_a', 'lru_b_a', 'lru_w_x', 'lru_b_x', 'lru_lambda', 'w_bl', 'w_out', 'g_post', 'w_ple', 'w_ple_gate']
TWIN_DIFF_INPUT = 'x'
TWIN_INPUTS = ['x', 'p', 'g_pre', 'w_in', 's5_a_re', 's5_a_im', 's5_log_dt', 's5_b_re', 's5_b_im', 's5_c_re', 's5_c_im', 's5_d', 'w_glu', 'w_bs', 'conv_w', 'conv_b', 'lru_w_a', 'lru_b_a', 'lru_w_x', 'lru_b_x', 'lru_lambda', 'w_bl', 'w_out', 'g_post', 'w_ple', 'w_ple_gate', 'loss_target', 'm_g_pre', 'm_w_in', 'm_s5_a_re', 'm_s5_a_im', 'm_s5_log_dt', 'm_s5_b_re', 'm_s5_b_im', 'm_s5_c_re', 'm_s5_c_im', 'm_s5_d', 'm_w_glu', 'm_w_bs', 'm_conv_w', 'm_conv_b', 'm_lru_w_a', 'm_lru_b_a', 'm_lru_w_x', 'm_lru_b_x', 'm_lru_lambda', 'm_w_bl', 'm_w_out', 'm_g_post', 'm_w_ple', 'm_w_ple_gate', 'v_g_pre', 'v_w_in', 'v_s5_a_re', 'v_s5_a_im', 'v_s5_log_dt', 'v_s5_b_re', 'v_s5_b_im', 'v_s5_c_re', 'v_s5_c_im', 'v_s5_d', 'v_w_glu', 'v_w_bs', 'v_conv_w', 'v_conv_b', 'v_lru_w_a', 'v_lru_b_a', 'v_lru_w_x', 'v_lru_b_x', 'v_lru_lambda', 'v_w_bl', 'v_w_out', 'v_g_post', 'v_w_ple', 'v_w_ple_gate']
TWIN_OUTPUTS = ['loss', 'grad_x', 'grad_g_pre', 'grad_w_in', 'grad_s5_a_re', 'grad_s5_a_im', 'grad_s5_log_dt', 'grad_s5_b_re', 'grad_s5_b_im', 'grad_s5_c_re', 'grad_s5_c_im', 'grad_s5_d', 'grad_w_glu', 'grad_w_bs', 'grad_conv_w', 'grad_conv_b', 'grad_lru_w_a', 'grad_lru_b_a', 'grad_lru_w_x', 'grad_lru_b_x', 'grad_lru_lambda', 'grad_w_bl', 'grad_w_out', 'grad_g_post', 'grad_w_ple', 'grad_w_ple_gate', 'delta_g_pre', 'delta_w_in', 'delta_s5_a_re', 'delta_s5_a_im', 'delta_s5_log_dt', 'delta_s5_b_re', 'delta_s5_b_im', 'delta_s5_c_re', 'delta_s5_c_im', 'delta_s5_d', 'delta_w_glu', 'delta_w_bs', 'delta_conv_w', 'delta_conv_b', 'delta_lru_w_a', 'delta_lru_b_a', 'delta_lru_w_x', 'delta_lru_b_x', 'delta_lru_lambda', 'delta_w_bl', 'delta_w_out', 'delta_g_post', 'delta_w_ple', 'delta_w_ple_gate', 'new_m_g_pre', 'new_m_w_in', 'new_m_s5_a_re', 'new_m_s5_a_im', 'new_m_s5_log_dt', 'new_m_s5_b_re', 'new_m_s5_b_im', 'new_m_s5_c_re', 'new_m_s5_c_im', 'new_m_s5_d', 'new_m_w_glu', 'new_m_w_bs', 'new_m_conv_w', 'new_m_conv_b', 'new_m_lru_w_a', 'new_m_lru_b_a', 'new_m_lru_w_x', 'new_m_lru_b_x', 'new_m_lru_lambda', 'new_m_w_bl', 'new_m_w_out', 'new_m_g_post', 'new_m_w_ple', 'new_m_w_ple_gate', 'new_v_g_pre', 'new_v_w_in', 'new_v_s5_a_re', 'new_v_s5_a_im', 'new_v_s5_log_dt', 'new_v_s5_b_re', 'new_v_s5_b_im', 'new_v_s5_c_re', 'new_v_s5_c_im', 'new_v_s5_d', 'new_v_w_glu', 'new_v_w_bs', 'new_v_conv_w', 'new_v_conv_b', 'new_v_lru_w_a', 'new_v_lru_b_a', 'new_v_lru_w_x', 'new_v_lru_b_x', 'new_v_lru_lambda', 'new_v_w_bl', 'new_v_w_out', 'new_v_g_post', 'new_v_w_ple', 'new_v_w_ple_gate']
TWIN_LEAF_KINDS = {'loss': 'loss', 'grad_x': 'grad_x', 'grad_g_pre': 'grad_w', 'grad_w_in': 'grad_w', 'grad_s5_a_re': 'grad_w', 'grad_s5_a_im': 'grad_w', 'grad_s5_log_dt': 'grad_w', 'grad_s5_b_re': 'grad_w', 'grad_s5_b_im': 'grad_w', 'grad_s5_c_re': 'grad_w', 'grad_s5_c_im': 'grad_w', 'grad_s5_d': 'grad_w', 'grad_w_glu': 'grad_w', 'grad_w_bs': 'grad_w', 'grad_conv_w': 'grad_w', 'grad_conv_b': 'grad_w', 'grad_lru_w_a': 'grad_w', 'grad_lru_b_a': 'grad_w', 'grad_lru_w_x': 'grad_w', 'grad_lru_b_x': 'grad_w', 'grad_lru_lambda': 'grad_w', 'grad_w_bl': 'grad_w', 'grad_w_out': 'grad_w', 'grad_g_post': 'grad_w', 'grad_w_ple': 'grad_w', 'grad_w_ple_gate': 'grad_w', 'delta_g_pre': 'delta_w', 'delta_w_in': 'delta_w', 'delta_s5_a_re': 'delta_w', 'delta_s5_a_im': 'delta_w', 'delta_s5_log_dt': 'delta_w', 'delta_s5_b_re': 'delta_w', 'delta_s5_b_im': 'delta_w', 'delta_s5_c_re': 'delta_w', 'delta_s5_c_im': 'delta_w', 'delta_s5_d': 'delta_w', 'delta_w_glu': 'delta_w', 'delta_w_bs': 'delta_w', 'delta_conv_w': 'delta_w', 'delta_conv_b': 'delta_w', 'delta_lru_w_a': 'delta_w', 'delta_lru_b_a': 'delta_w', 'delta_lru_w_x': 'delta_w', 'delta_lru_b_x': 'delta_w', 'delta_lru_lambda': 'delta_w', 'delta_w_bl': 'delta_w', 'delta_w_out': 'delta_w', 'delta_g_post': 'delta_w', 'delta_w_ple': 'delta_w', 'delta_w_ple_gate': 'delta_w', 'new_m_g_pre': 'new_m', 'new_m_w_in': 'new_m', 'new_m_s5_a_re': 'new_m', 'new_m_s5_a_im': 'new_m', 'new_m_s5_log_dt': 'new_m', 'new_m_s5_b_re': 'new_m', 'new_m_s5_b_im': 'new_m', 'new_m_s5_c_re': 'new_m', 'new_m_s5_c_im': 'new_m', 'new_m_s5_d': 'new_m', 'new_m_w_glu': 'new_m', 'new_m_w_bs': 'new_m', 'new_m_conv_w': 'new_m', 'new_m_conv_b': 'new_m', 'new_m_lru_w_a': 'new_m', 'new_m_lru_b_a': 'new_m', 'new_m_lru_w_x': 'new_m', 'new_m_lru_b_x': 'new_m', 'new_m_lru_lambda': 'new_m', 'new_m_w_bl': 'new_m', 'new_m_w_out': 'new_m', 'new_m_g_post': 'new_m', 'new_m_w_ple': 'new_m', 'new_m_w_ple_gate': 'new_m', 'new_v_g_pre': 'new_v', 'new_v_w_in': 'new_v', 'new_v_s5_a_re': 'new_v', 'new_v_s5_a_im': 'new_v', 'new_v_s5_log_dt': 'new_v', 'new_v_s5_b_re': 'new_v', 'new_v_s5_b_im': 'new_v', 'new_v_s5_c_re': 'new_v', 'new_v_s5_c_im': 'new_v', 'new_v_s5_d': 'new_v', 'new_v_w_glu': 'new_v', 'new_v_w_bs': 'new_v', 'new_v_conv_w': 'new_v', 'new_v_conv_b': 'new_v', 'new_v_lru_w_a': 'new_v', 'new_v_lru_b_a': 'new_v', 'new_v_lru_w_x': 'new_v', 'new_v_lru_b_x': 'new_v', 'new_v_lru_lambda': 'new_v', 'new_v_w_bl': 'new_v', 'new_v_w_out': 'new_v', 'new_v_g_post': 'new_v', 'new_v_w_ple': 'new_v', 'new_v_w_ple_gate': 'new_v'}


def _forward(args):
    return _fwd_reference(*[args[k] for k in FWD_PARAMS])


def _output_shape():
    def fwd():
        inp = _fwd_setup_inputs(0)
        return _fwd_reference(*[inp[k] for k in FWD_PARAMS])
    out = _jax.eval_shape(fwd)
    return out.shape, out.dtype

N_MICROBATCH = 1
ADAM_LR = 0.001
ADAM_B1 = 0.9
ADAM_B2 = 0.999
ADAM_EPS = 1e-08
ADAM_WD = 0.01
ADAM_STEP = 10
PER_EXAMPLE_BATCH_AXIS = {'x': 0, 'p': 1, 'loss_target': 0}
SHARED_INPUTS = []
_WEIGHT_DTYPES = {'g_pre': _jnp.float32, 'w_in': _jnp.float32, 's5_a_re': _jnp.float32, 's5_a_im': _jnp.float32, 's5_log_dt': _jnp.float32, 's5_b_re': _jnp.float32, 's5_b_im': _jnp.float32, 's5_c_re': _jnp.float32, 's5_c_im': _jnp.float32, 's5_d': _jnp.float32, 'w_glu': _jnp.float32, 'w_bs': _jnp.float32, 'conv_w': _jnp.float32, 'conv_b': _jnp.float32, 'lru_w_a': _jnp.float32, 'lru_b_a': _jnp.float32, 'lru_w_x': _jnp.float32, 'lru_b_x': _jnp.float32, 'lru_lambda': _jnp.float32, 'w_bl': _jnp.float32, 'w_out': _jnp.float32, 'g_post': _jnp.float32, 'w_ple': _jnp.float32, 'w_ple_gate': _jnp.float32}
MOMENT_SCALE = {'g_pre': 1.683035e+00, 'w_in': 7.215988e-01, 's5_a_re': 5.330047e-02, 's5_a_im': 5.459883e-02, 's5_log_dt': 2.420781e+01, 's5_b_re': 1.995783e-02, 's5_b_im': 1.927473e-02, 's5_c_re': 3.981057e-02, 's5_c_im': 3.644661e-02, 's5_d': 1.269136e+00, 'w_glu': 9.843101e-01, 'w_bs': 9.921015e-01, 'conv_w': 1.947785e+00, 'conv_b': 3.084178e+01, 'lru_w_a': 9.398799e-01, 'lru_b_a': 6.501836e-01, 'lru_w_x': 1.687740e+00, 'lru_b_x': 6.230453e-01, 'lru_lambda': 9.964682e-01, 'w_bl': 2.515644e+00, 'w_out': 2.764655e+00, 'g_post': 6.577398e+01, 'w_ple': 9.789919e-01, 'w_ple_gate': 9.153867e-01}


def _to_microbatches(a, axis):
    t = _jnp.moveaxis(a, axis, 0)
    t = t.reshape((N_MICROBATCH, t.shape[0] // N_MICROBATCH) + t.shape[1:])
    return _jnp.moveaxis(t, 1, axis + 1)


def setup_inputs(seed: int = 0) -> dict:
    inp = _fwd_setup_inputs(seed)
    key = _jax.random.fold_in(_jax.random.key(seed), 7919)
    shape, _ = _output_shape()
    out = dict(inp)
    out["loss_target"] = _jax.random.normal(_jax.random.fold_in(key, 0), shape, _jnp.float32)
    for i, name in enumerate(TWIN_WEIGHTS):
        w = inp[name].astype(_jnp.float32)
        if MOMENT_SCALE is None:
            s = _jnp.sqrt(_jnp.mean(_jnp.square(w)) + 1e-30)
        else:
            s = MOMENT_SCALE[name]
        km, kv = _jax.random.split(_jax.random.fold_in(key, i + 1))
        out[name] = w
        out["m_" + name] = s * _jax.random.normal(km, w.shape, _jnp.float32)
        out["v_" + name] = (s * s) * _jax.random.uniform(kv, w.shape, _jnp.float32, 0.5, 1.5)
    if N_MICROBATCH > 1:
        for name, axis in PER_EXAMPLE_BATCH_AXIS.items():
            out[name] = _to_microbatches(out[name], axis)
    return {'x': out['x'], 'p': out['p'], 'g_pre': out['g_pre'], 'w_in': out['w_in'], 's5_a_re': out['s5_a_re'], 's5_a_im': out['s5_a_im'], 's5_log_dt': out['s5_log_dt'], 's5_b_re': out['s5_b_re'], 's5_b_im': out['s5_b_im'], 's5_c_re': out['s5_c_re'], 's5_c_im': out['s5_c_im'], 's5_d': out['s5_d'], 'w_glu': out['w_glu'], 'w_bs': out['w_bs'], 'conv_w': out['conv_w'], 'conv_b': out['conv_b'], 'lru_w_a': out['lru_w_a'], 'lru_b_a': out['lru_b_a'], 'lru_w_x': out['lru_w_x'], 'lru_b_x': out['lru_b_x'], 'lru_lambda': out['lru_lambda'], 'w_bl': out['w_bl'], 'w_out': out['w_out'], 'g_post': out['g_post'], 'w_ple': out['w_ple'], 'w_ple_gate': out['w_ple_gate'], 'loss_target': out['loss_target'], 'm_g_pre': out['m_g_pre'], 'm_w_in': out['m_w_in'], 'm_s5_a_re': out['m_s5_a_re'], 'm_s5_a_im': out['m_s5_a_im'], 'm_s5_log_dt': out['m_s5_log_dt'], 'm_s5_b_re': out['m_s5_b_re'], 'm_s5_b_im': out['m_s5_b_im'], 'm_s5_c_re': out['m_s5_c_re'], 'm_s5_c_im': out['m_s5_c_im'], 'm_s5_d': out['m_s5_d'], 'm_w_glu': out['m_w_glu'], 'm_w_bs': out['m_w_bs'], 'm_conv_w': out['m_conv_w'], 'm_conv_b': out['m_conv_b'], 'm_lru_w_a': out['m_lru_w_a'], 'm_lru_b_a': out['m_lru_b_a'], 'm_lru_w_x': out['m_lru_w_x'], 'm_lru_b_x': out['m_lru_b_x'], 'm_lru_lambda': out['m_lru_lambda'], 'm_w_bl': out['m_w_bl'], 'm_w_out': out['m_w_out'], 'm_g_post': out['m_g_post'], 'm_w_ple': out['m_w_ple'], 'm_w_ple_gate': out['m_w_ple_gate'], 'v_g_pre': out['v_g_pre'], 'v_w_in': out['v_w_in'], 'v_s5_a_re': out['v_s5_a_re'], 'v_s5_a_im': out['v_s5_a_im'], 'v_s5_log_dt': out['v_s5_log_dt'], 'v_s5_b_re': out['v_s5_b_re'], 'v_s5_b_im': out['v_s5_b_im'], 'v_s5_c_re': out['v_s5_c_re'], 'v_s5_c_im': out['v_s5_c_im'], 'v_s5_d': out['v_s5_d'], 'v_w_glu': out['v_w_glu'], 'v_w_bs': out['v_w_bs'], 'v_conv_w': out['v_conv_w'], 'v_conv_b': out['v_conv_b'], 'v_lru_w_a': out['v_lru_w_a'], 'v_lru_b_a': out['v_lru_b_a'], 'v_lru_w_x': out['v_lru_w_x'], 'v_lru_b_x': out['v_lru_b_x'], 'v_lru_lambda': out['v_lru_lambda'], 'v_w_bl': out['v_w_bl'], 'v_w_out': out['v_w_out'], 'v_g_post': out['v_g_post'], 'v_w_ple': out['v_w_ple'], 'v_w_ple_gate': out['v_w_ple_gate']}


def _loss(weights, diff, rest, loss_target):
    with _jax.named_scope("forward"):
        args = {**rest, TWIN_DIFF_INPUT: diff, **{k: w.astype(_WEIGHT_DTYPES[k]) for k, w in weights.items()}}
        y = _forward(args)
    with _jax.named_scope("loss_head"):
        err = _jnp.square(y.astype(_jnp.float32) - loss_target)
        return 0.5 * _jnp.sum(_jnp.mean(err, axis=-1)) if err.ndim else 0.5 * err


def _adamw(w, g, m, v):
    m = ADAM_B1 * m + (1.0 - ADAM_B1) * g
    v = ADAM_B2 * v + (1.0 - ADAM_B2) * _jnp.square(g)
    m_hat = m / (1.0 - ADAM_B1 ** ADAM_STEP)
    v_hat = v / (1.0 - ADAM_B2 ** ADAM_STEP)
    delta = -ADAM_LR * (m_hat / (_jnp.sqrt(v_hat) + ADAM_EPS) + ADAM_WD * w)
    return delta, m, v


def reference(x, p, g_pre, w_in, s5_a_re, s5_a_im, s5_log_dt, s5_b_re, s5_b_im, s5_c_re, s5_c_im, s5_d, w_glu, w_bs, conv_w, conv_b, lru_w_a, lru_b_a, lru_w_x, lru_b_x, lru_lambda, w_bl, w_out, g_post, w_ple, w_ple_gate, loss_target, m_g_pre, m_w_in, m_s5_a_re, m_s5_a_im, m_s5_log_dt, m_s5_b_re, m_s5_b_im, m_s5_c_re, m_s5_c_im, m_s5_d, m_w_glu, m_w_bs, m_conv_w, m_conv_b, m_lru_w_a, m_lru_b_a, m_lru_w_x, m_lru_b_x, m_lru_lambda, m_w_bl, m_w_out, m_g_post, m_w_ple, m_w_ple_gate, v_g_pre, v_w_in, v_s5_a_re, v_s5_a_im, v_s5_log_dt, v_s5_b_re, v_s5_b_im, v_s5_c_re, v_s5_c_im, v_s5_d, v_w_glu, v_w_bs, v_conv_w, v_conv_b, v_lru_w_a, v_lru_b_a, v_lru_w_x, v_lru_b_x, v_lru_lambda, v_w_bl, v_w_out, v_g_post, v_w_ple, v_w_ple_gate):
    given = dict(x=x, p=p, g_pre=g_pre, w_in=w_in, s5_a_re=s5_a_re, s5_a_im=s5_a_im, s5_log_dt=s5_log_dt, s5_b_re=s5_b_re, s5_b_im=s5_b_im, s5_c_re=s5_c_re, s5_c_im=s5_c_im, s5_d=s5_d, w_glu=w_glu, w_bs=w_bs, conv_w=conv_w, conv_b=conv_b, lru_w_a=lru_w_a, lru_b_a=lru_b_a, lru_w_x=lru_w_x, lru_b_x=lru_b_x, lru_lambda=lru_lambda, w_bl=w_bl, w_out=w_out, g_post=g_post, w_ple=w_ple, w_ple_gate=w_ple_gate, loss_target=loss_target, m_g_pre=m_g_pre, m_w_in=m_w_in, m_s5_a_re=m_s5_a_re, m_s5_a_im=m_s5_a_im, m_s5_log_dt=m_s5_log_dt, m_s5_b_re=m_s5_b_re, m_s5_b_im=m_s5_b_im, m_s5_c_re=m_s5_c_re, m_s5_c_im=m_s5_c_im, m_s5_d=m_s5_d, m_w_glu=m_w_glu, m_w_bs=m_w_bs, m_conv_w=m_conv_w, m_conv_b=m_conv_b, m_lru_w_a=m_lru_w_a, m_lru_b_a=m_lru_b_a, m_lru_w_x=m_lru_w_x, m_lru_b_x=m_lru_b_x, m_lru_lambda=m_lru_lambda, m_w_bl=m_w_bl, m_w_out=m_w_out, m_g_post=m_g_post, m_w_ple=m_w_ple, m_w_ple_gate=m_w_ple_gate, v_g_pre=v_g_pre, v_w_in=v_w_in, v_s5_a_re=v_s5_a_re, v_s5_a_im=v_s5_a_im, v_s5_log_dt=v_s5_log_dt, v_s5_b_re=v_s5_b_re, v_s5_b_im=v_s5_b_im, v_s5_c_re=v_s5_c_re, v_s5_c_im=v_s5_c_im, v_s5_d=v_s5_d, v_w_glu=v_w_glu, v_w_bs=v_w_bs, v_conv_w=v_conv_w, v_conv_b=v_conv_b, v_lru_w_a=v_lru_w_a, v_lru_b_a=v_lru_b_a, v_lru_w_x=v_lru_w_x, v_lru_b_x=v_lru_b_x, v_lru_lambda=v_lru_lambda, v_w_bl=v_w_bl, v_w_out=v_w_out, v_g_post=v_g_post, v_w_ple=v_w_ple, v_w_ple_gate=v_w_ple_gate)
    weights = {n: given[n] for n in TWIN_WEIGHTS}
    shared = {n: given[n] for n in SHARED_INPUTS}
    per_example = {n: given[n] for n in ['x', 'p']}
    grad_fn = _jax.value_and_grad(_loss, argnums=(0, 1))

    def one_microbatch(ex, loss_target):
        ex = dict(ex)
        diff = ex.pop(TWIN_DIFF_INPUT)
        return grad_fn(weights, diff, {**shared, **ex}, loss_target)

    if N_MICROBATCH == 1:
        loss, (grad_w, grad_x) = one_microbatch(per_example, given["loss_target"])
    else:
        def body(carry, xs):
            loss_sum, grad_sum = carry
            l_k, (gw_k, gx_k) = one_microbatch(xs[0], xs[1])
            with _jax.named_scope("update"):
                return (loss_sum + l_k, _jax.tree.map(_jnp.add, grad_sum, gw_k)), gx_k

        init = (_jnp.zeros((), _jnp.float32), _jax.tree.map(_jnp.zeros_like, weights))
        (loss, grad_w), grad_x = _jax.lax.scan(body, init, (per_example, given["loss_target"]))
    with _jax.named_scope("update"):
        delta_w, new_m, new_v = {}, {}, {}
        for n in TWIN_WEIGHTS:
            delta_w[n], new_m[n], new_v[n] = _adamw(weights[n], grad_w[n], given["m_" + n], given["v_" + n])
    return (loss, grad_x, *[grad_w[n] for n in TWIN_WEIGHTS], *[delta_w[n] for n in TWIN_WEIGHTS],
            *[new_m[n] for n in TWIN_WEIGHTS], *[new_v[n] for n in TWIN_WEIGHTS])
```

```python
import jax
import jax.numpy as jnp
from jax import lax
from jax.experimental import pallas as pl
from jax.experimental.pallas import tpu as pltpu

F32 = jnp.float32
BF16 = jnp.bfloat16
MESH = pl.DeviceIdType.MESH

DEPTH = 2
D_MODEL = 1024
NORM_EPS = 1e-6
S5_WIDTH = 512
S5_GROUPS = 32
S5_GROUP = 16
S5_STATE = 64
LRU_WIDTH = 1280
LRU_HEADS = 10
LRU_HEAD_DIM = 128
LRU_C = 8.0
CONV_WIDTH = 4
PLE_DIM = 256
IN_WIDTHS = (S5_WIDTH, S5_WIDTH, LRU_WIDTH, LRU_WIDTH, D_MODEL, D_MODEL)
ADAM_LR = 0.001
ADAM_B1 = 0.9
ADAM_B2 = 0.999
ADAM_EPS = 1e-08
ADAM_WD = 0.01
ADAM_STEP = 10

SUBLANES = 8
LANES = 128
S5_HALF_IN = S5_WIDTH // 2
S5_CPLX = S5_GROUPS * S5_STATE
S5_HALF_CPLX = S5_CPLX // 2
S5_LANES = 2 * S5_CPLX
VMEM_LIMIT = 48 * 2 ** 20
ROW_TILE = 256


def _sigmoid(x):
    return 1.0 / (1.0 + jnp.exp(-x))


def _gelu_parts(x):
    k = 0.7978845608028654
    t = jnp.tanh(k * (x + 0.044715 * x * x * x))
    val = 0.5 * x * (1.0 + t)
    grad = 0.5 * (1.0 + t) + 0.5 * x * (1.0 - t * t) * k * (1.0 + 3.0 * 0.044715 * x * x)
    return val, grad


def _nn(a, w):
    return jnp.dot(a.astype(BF16), w.astype(BF16), preferred_element_type=F32)


def _nt(a, w):
    return lax.dot_general(a.astype(BF16), w.astype(BF16), (((1,), (1,)), ((), ())), preferred_element_type=F32)


def _tn(a, b):
    return lax.dot_general(a.astype(BF16), b.astype(BF16), (((0,), (0,)), ((), ())), preferred_element_type=F32)


def _heads(op, a, w):
    d = LRU_HEAD_DIM
    return jnp.concatenate([op(a[:, h * d:(h + 1) * d], w[h]) for h in range(LRU_HEADS)], axis=1)


def _heads_tn(a, b):
    d = LRU_HEAD_DIM
    return jnp.stack([_tn(a[:, h * d:(h + 1) * d], b[:, h * d:(h + 1) * d]) for h in range(LRU_HEADS)], axis=0)


def _rows_before(x, halo, s):
    main = pltpu.roll(x, s, 0)
    head = pltpu.roll(jnp.concatenate([halo, x[0:SUBLANES]], axis=0), s, 0)[SUBLANES:2 * SUBLANES]
    return jnp.concatenate([head, main[SUBLANES:]], axis=0)


def _rows_after(x, halo, s):
    n = x.shape[0]
    main = pltpu.roll(x, n - s, 0)
    tail = pltpu.roll(jnp.concatenate([x[n - SUBLANES:], halo], axis=0), 2 * SUBLANES - s, 0)[0:SUBLANES]
    return jnp.concatenate([main[:n - SUBLANES], tail], axis=0)


def _rows(name, fn, ins, outs, *, rows, tile):
    tile = min(tile, rows)
    n = rows // tile
    assert n * tile == rows, (name, rows, tile)
    per8 = tile // SUBLANES
    last8 = rows // SUBLANES - 1
    in_specs = []
    for arr, kind in ins:
        if kind == "row":
            in_specs.append(pl.BlockSpec((tile, arr.shape[1]), lambda i: (i, 0)))
        elif kind == "prev":
            in_specs.append(pl.BlockSpec((SUBLANES, arr.shape[1]), lambda i: (jnp.maximum(i * per8 - 1, 0), 0)))
        elif kind == "next":
            in_specs.append(pl.BlockSpec((SUBLANES, arr.shape[1]), lambda i: (jnp.minimum((i + 1) * per8, last8), 0)))
        else:
            in_specs.append(pl.BlockSpec(arr.shape, lambda i, nd=arr.ndim: (0,) * nd))
    out_shape, out_specs = [], []
    for shape, dtype, kind in outs:
        out_shape.append(jax.ShapeDtypeStruct(shape, dtype))
        if kind == "row":
            out_specs.append(pl.BlockSpec((tile, shape[1]), lambda i: (i, 0)))
        else:
            out_specs.append(pl.BlockSpec(shape, lambda i, nd=len(shape): (0,) * nd))
    n_in = len(ins)

    def body(*refs):
        i = pl.program_id(0)
        vals = fn(i, *[r[...] for r in refs[:n_in]])
        assert len(vals) == len(outs), name
        for r, v, (_, _, kind) in zip(refs[n_in:], vals, outs):
            if kind == "row":
                r[...] = v.astype(r.dtype)
            else:
                @pl.when(i == 0)
                def _():
                    r[...] = jnp.zeros_like(r)

                r[...] += v.astype(r.dtype)

    return pl.pallas_call(
        body, name=name, grid=(n,), in_specs=in_specs, out_specs=out_specs, out_shape=out_shape,
        compiler_params=pltpu.CompilerParams(dimension_semantics=("arbitrary",), vmem_limit_bytes=VMEM_LIMIT),
    )(*[a for a, _ in ins])


def _s5_discretise(are, aim, ldt, bre, bim):
    dt = jnp.exp(ldt)
    er = jnp.exp(are * dt)
    abr = er * jnp.cos(aim * dt)
    abi = er * jnp.sin(aim * dt)
    den = are * are + aim * aim
    zr = ((abr - 1.0) * are + abi * aim) / den
    zi = (abi * are - (abr - 1.0) * aim) / den
    return abr, abi, zr * bre - zi * bim, zr * bim + zi * bre


def _s5_prep(are, aim, ldt, bre, bim):
    def body(a, b, c, d, e, o0, o1, o2, o3):
        r = _s5_discretise(a[...], b[...], c[...], d[...], e[...])
        o0[...], o1[...], o2[...], o3[...] = r

    sd = jax.ShapeDtypeStruct(are.shape, F32)
    return pl.pallas_call(body, name="s5_prep", out_shape=[sd] * 4)(are, aim, ldt, bre, bim)


def _s5_prep_bwd(are, aim, ldt, bre, bim, cts):
    def body(a, b, c, d, e, c0, c1, c2, c3, o0, o1, o2, o3, o4):
        _, vjp = jax.vjp(_s5_discretise, a[...], b[...], c[...], d[...], e[...])
        r = vjp((c0[...], c1[...], c2[...], c3[...]))
        o0[...], o1[...], o2[...], o3[...], o4[...] = r

    sd = jax.ShapeDtypeStruct(are.shape, F32)
    return pl.pallas_call(body, name="s5_prep_bwd", out_shape=[sd] * 5)(are, aim, ldt, bre, bim, *cts)


def _s5_consts(abr, abi):
    shape = (SUBLANES, S5_CPLX)

    def body(ar_ref, ai_ref, f_ref, b_ref):
        ar = jnp.broadcast_to(ar_ref[...], shape)
        ai = jnp.broadcast_to(ai_ref[...], shape)
        row = lax.broadcasted_iota(jnp.int32, shape, 0)

        def cmul(p, q):
            return (p[0] * q[0] - p[1] * q[1], p[0] * q[1] + p[1] * q[0])

        a1 = (ar, ai)
        a2 = cmul(a1, a1)
        a3 = cmul(a2, a1)
        a4 = cmul(a2, a2)
        pw = [a1, a2, a3, a4, cmul(a4, a1), cmul(a4, a2), cmul(a4, a3), cmul(a4, a4)]

        def by_row(vals):
            out = vals[7]
            for r in range(6, -1, -1):
                out = jnp.where(row == r, vals[r], out)
            return out

        fwd, rev = [], []
        for a, k in ((a1, 1), (a2, 2), (a4, 4)):
            fwd += [jnp.where(row >= k, a[0], 0.0), jnp.where(row >= k, a[1], 0.0)]
            rev += [jnp.where(row <= 7 - k, a[0], 0.0), jnp.where(row <= 7 - k, -a[1], 0.0)]
        fwd += [by_row([p[0] for p in pw]), by_row([p[1] for p in pw])]
        rev += [by_row([pw[7 - r][0] for r in range(8)]), by_row([-pw[7 - r][1] for r in range(8)])]
        f_ref[...] = jnp.concatenate(fwd, axis=0)
        b_ref[...] = jnp.concatenate(rev, axis=0)

    sd = jax.ShapeDtypeStruct((8 * SUBLANES, S5_CPLX), F32)
    return pl.pallas_call(body, name="s5_consts", out_shape=[sd, sd])(abr, abi)


def _s5_lane_offsets(q):
    re = (q // 8) * 2 * S5_HALF_CPLX + (q % 8) * LANES
    return re, re + S5_HALF_CPLX


def _s5_scan(s_ref, sc_ref, carry_ref, nblk, reverse):
    group = 2
    edge = 0 if reverse else SUBLANES - 1
    for q0 in range(0, S5_CPLX // LANES, group):
        offs = [_s5_lane_offsets(q) for q in range(q0, q0 + group)]
        consts = [[sc_ref[k * SUBLANES:(k + 1) * SUBLANES, q * LANES:(q + 1) * LANES] for k in range(8)]
                  for q in range(q0, q0 + group)]
        carry0 = tuple(carry_ref[:, o:o + LANES] for pair in offs for o in pair)

        def blk(t, carry, offs=offs, consts=consts):
            b = (nblk - 1 - t) if reverse else t
            r0 = pl.multiple_of(b * SUBLANES, SUBLANES)
            new = []
            for j, ((re, im), (a1r, a1i, a2r, a2i, a4r, a4i, pr, pi)) in enumerate(zip(offs, consts)):
                xr = s_ref[pl.ds(r0, SUBLANES), re:re + LANES]
                xi = s_ref[pl.ds(r0, SUBLANES), im:im + LANES]
                for ar, ai, sh in ((a1r, a1i, 1), (a2r, a2i, 2), (a4r, a4i, 4)):
                    shift = SUBLANES - sh if reverse else sh
                    sr = pltpu.roll(xr, shift, 0)
                    si = pltpu.roll(xi, shift, 0)
                    xr, xi = xr + ar * sr - ai * si, xi + ar * si + ai * sr
                cr, ci = carry[2 * j], carry[2 * j + 1]
                xr, xi = xr + pr * cr - pi * ci, xi + pr * ci + pi * cr
                s_ref[pl.ds(r0, SUBLANES), re:re + LANES] = xr
                s_ref[pl.ds(r0, SUBLANES), im:im + LANES] = xi
                new.append(jnp.broadcast_to(xr[edge:edge + 1, :], (SUBLANES, LANES)))
                new.append(jnp.broadcast_to(xi[edge:edge + 1, :], (SUBLANES, LANES)))
            return tuple(new)

        carry = lax.fori_loop(0, nblk, blk, carry0, unroll=2)
        for k, o in enumerate(o for pair in offs for o in pair):
            carry_ref[:, o:o + LANES] = carry[k]


def _s5_fwd(u, bd, cdt, dskip, sc, *, rows, tile):
    n = rows // tile
    nblk = tile // SUBLANES
    hc = 2 * S5_HALF_CPLX

    def body(u_ref, bd_ref, cdt_ref, d_ref, sc_ref, y_ref, s_ref, carry_ref):
        @pl.when(pl.program_id(0) == 0)
        def _():
            carry_ref[...] = jnp.zeros_like(carry_ref)

        ub = u_ref[...].astype(BF16)
        for h in range(2):
            s_ref[:, h * hc:(h + 1) * hc] = jnp.dot(ub[:, h * S5_HALF_IN:(h + 1) * S5_HALF_IN], bd_ref[h],
                                                    preferred_element_type=F32)
        _s5_scan(s_ref, sc_ref, carry_ref, nblk, reverse=False)
        ys = [_nt(s_ref[:, h * hc:(h + 1) * hc], cdt_ref[h]) for h in range(2)]
        y_ref[...] = jnp.concatenate(ys, axis=1) + d_ref[...] * u_ref[...]

    full = lambda a: pl.BlockSpec(a.shape, lambda i, nd=a.ndim: (0,) * nd)
    return pl.pallas_call(
        body, name="s5_fwd", grid=(n,),
        in_specs=[pl.BlockSpec((tile, S5_WIDTH), lambda i: (i, 0)), full(bd), full(cdt), full(dskip), full(sc)],
        out_specs=[pl.BlockSpec((tile, S5_WIDTH), lambda i: (i, 0)), pl.BlockSpec((tile, S5_LANES), lambda i: (i, 0))],
        out_shape=[jax.ShapeDtypeStruct((rows, S5_WIDTH), F32), jax.ShapeDtypeStruct((rows, S5_LANES), F32)],
        scratch_shapes=[pltpu.VMEM((SUBLANES, S5_LANES), F32)],
        compiler_params=pltpu.CompilerParams(dimension_semantics=("arbitrary",), vmem_limit_bytes=VMEM_LIMIT),
    )(u, bd, cdt, dskip, sc)


def _s5_bwd(dy, s, u, bd, cdt, dskip, sc, *, rows, tile):
    n = rows // tile
    nblk = tile // SUBLANES
    hc = 2 * S5_HALF_CPLX
    per8 = tile // SUBLANES

    def body(dy_ref, s_ref, sp_ref, u_ref, bd_ref, cdt_ref, d_ref, sc_ref,
             du_ref, dbd_ref, dcdt_ref, dd_ref, da_ref, g_ref, carry_ref):
        i = pl.program_id(0)

        @pl.when(i == 0)
        def _():
            carry_ref[...] = jnp.zeros_like(carry_ref)
            dbd_ref[...] = jnp.zeros_like(dbd_ref)
            dcdt_ref[...] = jnp.zeros_like(dcdt_ref)
            dd_ref[...] = jnp.zeros_like(dd_ref)
            da_ref[...] = jnp.zeros_like(da_ref)

        dy = dy_ref[...]
        dyb = dy.astype(BF16)
        u = u_ref[...]
        ub = u.astype(BF16)
        for h in range(2):
            g_ref[:, h * hc:(h + 1) * hc] = jnp.dot(dyb[:, h * S5_HALF_IN:(h + 1) * S5_HALF_IN], cdt_ref[h],
                                                    preferred_element_type=F32)
        _s5_scan(g_ref, sc_ref, carry_ref, nblk, reverse=True)
        dus = []
        for h in range(2):
            gb = g_ref[:, h * hc:(h + 1) * hc].astype(BF16)
            sb = s_ref[:, h * hc:(h + 1) * hc].astype(BF16)
            dus.append(_nt(gb, bd_ref[h]))
            dbd_ref[h] += _tn(ub[:, h * S5_HALF_IN:(h + 1) * S5_HALF_IN], gb)
            dcdt_ref[h] += _tn(dyb[:, h * S5_HALF_IN:(h + 1) * S5_HALF_IN], sb)
        du_ref[...] = jnp.concatenate(dus, axis=1) + d_ref[...] * dy
        dd_ref[...] += jnp.sum(dy * u, axis=0, keepdims=True)

        not_first = (i < n - 1).astype(F32)
        row = lax.broadcasted_iota(jnp.int32, (SUBLANES, LANES), 0)
        for q in range(S5_CPLX // LANES):
            re, im = _s5_lane_offsets(q)
            pr0 = jnp.broadcast_to(sp_ref[SUBLANES - 1:SUBLANES, re:re + LANES], (SUBLANES, LANES)) * not_first
            pi0 = jnp.broadcast_to(sp_ref[SUBLANES - 1:SUBLANES, im:im + LANES], (SUBLANES, LANES)) * not_first
            zero = jnp.zeros((SUBLANES, LANES), F32)

            def blk(b, c, re=re, im=im):
                accr, acci, pr, pi = c
                r0 = pl.multiple_of(b * SUBLANES, SUBLANES)
                sr = s_ref[pl.ds(r0, SUBLANES), re:re + LANES]
                si = s_ref[pl.ds(r0, SUBLANES), im:im + LANES]
                gr = g_ref[pl.ds(r0, SUBLANES), re:re + LANES]
                gi = g_ref[pl.ds(r0, SUBLANES), im:im + LANES]
                ssr = jnp.where(row == 0, pr, pltpu.roll(sr, 1, 0))
                ssi = jnp.where(row == 0, pi, pltpu.roll(si, 1, 0))
                accr = accr + ssr * gr + ssi * gi
                acci = acci + ssr * gi - ssi * gr
                return (accr, acci, jnp.broadcast_to(sr[SUBLANES - 1:SUBLANES, :], (SUBLANES, LANES)),
                        jnp.broadcast_to(si[SUBLANES - 1:SUBLANES, :], (SUBLANES, LANES)))

            accr, acci, _, _ = lax.fori_loop(0, nblk, blk, (zero, zero, pr0, pi0), unroll=2)
            da_ref[0:1, q * LANES:(q + 1) * LANES] += jnp.sum(accr, axis=0, keepdims=True)
            da_ref[1:2, q * LANES:(q + 1) * LANES] += jnp.sum(acci, axis=0, keepdims=True)

    full = lambda a: pl.BlockSpec(a.shape, lambda i, nd=a.ndim: (0,) * nd)
    rev = lambda i: (n - 1 - i, 0)
    wshape = (2, S5_HALF_IN, hc)
    return pl.pallas_call(
        body, name="s5_bwd", grid=(n,),
        in_specs=[pl.BlockSpec((tile, S5_WIDTH), rev), pl.BlockSpec((tile, S5_LANES), rev),
                  pl.BlockSpec((SUBLANES, S5_LANES), lambda i: (jnp.maximum((n - 1 - i) * per8 - 1, 0), 0)),
                  pl.BlockSpec((tile, S5_WIDTH), rev), full(bd), full(cdt), full(dskip), full(sc)],
        out_specs=[pl.BlockSpec((tile, S5_WIDTH), rev),
                   pl.BlockSpec(wshape, lambda i: (0, 0, 0)), pl.BlockSpec(wshape, lambda i: (0, 0, 0)),
                   pl.BlockSpec((1, S5_WIDTH), lambda i: (0, 0)), pl.BlockSpec((SUBLANES, S5_CPLX), lambda i: (0, 0))],
        out_shape=[jax.ShapeDtypeStruct((rows, S5_WIDTH), F32), jax.ShapeDtypeStruct(wshape, F32),
                   jax.ShapeDtypeStruct(wshape, F32), jax.ShapeDtypeStruct((1, S5_WIDTH), F32),
                   jax.ShapeDtypeStruct((SUBLANES, S5_CPLX), F32)],
        scratch_shapes=[pltpu.VMEM((tile, S5_LANES), F32), pltpu.VMEM((SUBLANES, S5_LANES), F32)],
        compiler_params=pltpu.CompilerParams(dimension_semantics=("arbitrary",), vmem_limit_bytes=VMEM_LIMIT),
    )(dy, s, s, u, bd, cdt, dskip, sc)


def _s5_block_diag(parts):
    v = jnp.stack(parts, axis=2).reshape(2, 16, S5_GROUP, 2, S5_STATE)
    eye = jnp.eye(16, dtype=v.dtype)
    return jnp.einsum("hgcpn,gk->hgcpkn", v, eye).reshape(2, S5_HALF_IN, 2 * S5_HALF_CPLX)


def _s5_block_diag_extract(m):
    v = m.reshape(2, 16, S5_GROUP, 2, 16, S5_STATE)
    d = jnp.diagonal(v, axis1=1, axis2=4)
    d = jnp.transpose(d, (2, 0, 4, 1, 3)).reshape(2, S5_GROUPS, S5_GROUP, S5_STATE)
    return d[0], d[1]


def _cplx_to_lanes(v):
    return v.reshape(1, S5_CPLX)


def _lru_scan_fwd(a, b, *, rows, tile):
    n = rows // tile
    nblk = tile // SUBLANES
    group = 2

    def body(a_ref, b_ref, h_ref, carry_ref):
        @pl.when(pl.program_id(0) == 0)
        def _():
            carry_ref[...] = jnp.zeros_like(carry_ref)

        row = lax.broadcasted_iota(jnp.int32, (SUBLANES, LANES), 0)
        for q0 in range(0, LRU_WIDTH // LANES, group):
            offs = [q * LANES for q in range(q0, q0 + group)]

            def blk(t, carry, offs=offs):
                r0 = pl.multiple_of(t * SUBLANES, SUBLANES)
                new = []
                for j, o in enumerate(offs):
                    av = a_ref[pl.ds(r0, SUBLANES), o:o + LANES]
                    xv = b_ref[pl.ds(r0, SUBLANES), o:o + LANES]
                    for sh in (1, 2, 4):
                        m = row >= sh
                        xs = pltpu.roll(xv, sh, 0)
                        asft = pltpu.roll(av, sh, 0)
                        xv = xv + jnp.where(m, av * xs, 0.0)
                        av = jnp.where(m, av * asft, av)
                    hv = xv + av * carry[j]
                    h_ref[pl.ds(r0, SUBLANES), o:o + LANES] = hv
                    new.append(jnp.broadcast_to(hv[SUBLANES - 1:SUBLANES, :], (SUBLANES, LANES)))
                return tuple(new)

            carry = lax.fori_loop(0, nblk, blk, tuple(carry_ref[:, o:o + LANES] for o in offs), unroll=2)
            for j, o in enumerate(offs):
                carry_ref[:, o:o + LANES] = carry[j]

    spec = pl.BlockSpec((tile, LRU_WIDTH), lambda i: (i, 0))
    return pl.pallas_call(
        body, name="lru_scan_fwd", grid=(n,), in_specs=[spec, spec], out_specs=spec,
        out_shape=jax.ShapeDtypeStruct((rows, LRU_WIDTH), F32),
        scratch_shapes=[pltpu.VMEM((SUBLANES, LRU_WIDTH), F32)],
        compiler_params=pltpu.CompilerParams(dimension_semantics=("arbitrary",), vmem_limit_bytes=VMEM_LIMIT),
    )(a, b)


def _lru_scan_bwd(dh, a, *, rows, tile):
    n = rows // tile
    nblk = tile // SUBLANES
    group = 2

    def body(dh_ref, a_ref, g_ref, cg_ref, ca_ref):
        @pl.when(pl.program_id(0) == 0)
        def _():
            cg_ref[...] = jnp.zeros_like(cg_ref)
            ca_ref[...] = jnp.zeros_like(ca_ref)

        row = lax.broadcasted_iota(jnp.int32, (SUBLANES, LANES), 0)
        for q0 in range(0, LRU_WIDTH // LANES, group):
            offs = [q * LANES for q in range(q0, q0 + group)]

            def blk(t, carry, offs=offs):
                r0 = pl.multiple_of((nblk - 1 - t) * SUBLANES, SUBLANES)
                new = []
                for j, o in enumerate(offs):
                    cg, ca = carry[2 * j], carry[2 * j + 1]
                    araw = a_ref[pl.ds(r0, SUBLANES), o:o + LANES]
                    xv = dh_ref[pl.ds(r0, SUBLANES), o:o + LANES]
                    av = jnp.where(row == SUBLANES - 1, ca, pltpu.roll(araw, SUBLANES - 1, 0))
                    for sh in (1, 2, 4):
                        m = row <= SUBLANES - 1 - sh
                        xs = pltpu.roll(xv, SUBLANES - sh, 0)
                        asft = pltpu.roll(av, SUBLANES - sh, 0)
                        xv = xv + jnp.where(m, av * xs, 0.0)
                        av = jnp.where(m, av * asft, av)
                    gv = xv + av * cg
                    g_ref[pl.ds(r0, SUBLANES), o:o + LANES] = gv
                    new.append(jnp.broadcast_to(gv[0:1, :], (SUBLANES, LANES)))
                    new.append(jnp.broadcast_to(araw[0:1, :], (SUBLANES, LANES)))
                return tuple(new)

            carry0 = tuple(r[:, o:o + LANES] for o in offs for r in (cg_ref, ca_ref))
            carry = lax.fori_loop(0, nblk, blk, carry0, unroll=2)
            for j, o in enumerate(offs):
                cg_ref[:, o:o + LANES] = carry[2 * j]
                ca_ref[:, o:o + LANES] = carry[2 * j + 1]

    spec = pl.BlockSpec((tile, LRU_WIDTH), lambda i: (n - 1 - i, 0))
    return pl.pallas_call(
        body, name="lru_scan_bwd", grid=(n,), in_specs=[spec, spec], out_specs=spec,
        out_shape=jax.ShapeDtypeStruct((rows, LRU_WIDTH), F32),
        scratch_shapes=[pltpu.VMEM((SUBLANES, LRU_WIDTH), F32), pltpu.VMEM((SUBLANES, LRU_WIDTH), F32)],
        compiler_params=pltpu.CompilerParams(dimension_semantics=("arbitrary",), vmem_limit_bytes=VMEM_LIMIT),
    )(dh, a)


def _conv_fwd(i, x, prev, cw, cb):
    prev = prev * (i > 0).astype(F32)
    y = x * cw[3:4, :] + cb
    for s in range(1, CONV_WIDTH):
        y = y + _rows_before(x, prev, s) * cw[3 - s:4 - s, :]
    return y


def _lru_gates(c, wa, ba, wx, bx, lam):
    r = _sigmoid(_heads(_nn, c, wa) + ba)
    ig = _sigmoid(_heads(_nn, c, wx) + bx)
    z = -lam
    sp = jnp.maximum(z, 0.0) + jnp.log(1.0 + jnp.exp(-jnp.abs(z)))
    log_a = -LRU_C * r * sp
    a = jnp.exp(log_a)
    z2 = 2.0 * log_a
    series = -z2 * (1.0 + z2 * (0.5 + z2 * (1.0 / 6.0 + z2 * (1.0 / 24.0 + z2 * (1.0 / 120.0 + z2 / 720.0)))))
    one_minus = jnp.where(z2 > -0.2, series, 1.0 - jnp.exp(z2))
    mult = jnp.sqrt(one_minus)
    return r, ig, sp, a, mult


def _layer_fwd(x, p, w, rows):
    tile = ROW_TILE
    d = D_MODEL

    def f_in(i, xb, g, *ws):
        rstd = lax.rsqrt(jnp.mean(xb * xb, axis=-1, keepdims=True) + NORM_EPS)
        hb = (xb * rstd * g).astype(BF16)
        return tuple(jnp.dot(hb, wc, preferred_element_type=F32) for wc in ws) + (hb,)

    s5x, s5g, lrux, lrug, gs, gl, h = _rows(
        "f_in", f_in, [(x, "row"), (w["g_pre"], "full")] + [(wc, "full") for wc in w["w_in"]],
        [((rows, wd), F32, "row") for wd in IN_WIDTHS] + [((rows, d), BF16, "row")], rows=rows, tile=tile)

    ys, st = _s5_fwd(s5x, w["bd"], w["cdt"], w["s5_d"], w["scf"], rows=rows, tile=tile)

    def f_s5post(i, ysb, gb, wglu, wbs):
        glv, _ = _gelu_parts(ysb)
        glu = _nn(glv, wglu)
        y2 = glu[:, :S5_WIDTH] * _sigmoid(glu[:, S5_WIDTH:]) * (gb * _sigmoid(gb))
        return (_nn(y2, wbs),)

    (z_s,) = _rows("f_s5post", f_s5post, [(ys, "row"), (s5g, "row"), (w["w_glu"], "full"), (w["w_bs"], "full")],
                   [((rows, d), F32, "row")], rows=rows, tile=tile)

    def f_gates(i, xb, prev, cw, cb, wa, ba, wx, bx, lam):
        c = _conv_fwd(i, xb, prev, cw, cb)
        _, ig, _, a, mult = _lru_gates(c, wa, ba, wx, bx, lam)
        return a, mult * (ig * c)

    a, b = _rows("f_gates", f_gates,
                 [(lrux, "row"), (lrux, "prev"), (w["conv_w"], "full"), (w["conv_b"], "full"), (w["lru_w_a"], "full"),
                  (w["lru_b_a"], "full"), (w["lru_w_x"], "full"), (w["lru_b_x"], "full"), (w["lru_lambda"], "full")],
                 [((rows, LRU_WIDTH), F32, "row")] * 2, rows=rows, tile=tile)
    hl = _lru_scan_fwd(a, b, rows=rows, tile=tile)

    def f_merge(i, hb, lg, zs, gsb, glb, xb, wbl, wout, gpost):
        z_l = _nn(hb * (lg * _sigmoid(lg)), wbl)
        merged = _sigmoid(gsb) * zs + _sigmoid(glb) * z_l
        mix = _nn(merged, wout)
        rstd = lax.rsqrt(jnp.mean(mix * mix, axis=-1, keepdims=True) + NORM_EPS)
        return xb + mix * rstd * gpost, mix, z_l

    x1, mix, z_l = _rows("f_merge", f_merge,
                         [(hl, "row"), (lrug, "row"), (z_s, "row"), (gs, "row"), (gl, "row"), (x, "row"),
                          (w["w_bl"], "full"), (w["w_out"], "full"), (w["g_post"], "full")],
                         [((rows, d), F32, "row")] * 3, rows=rows, tile=tile)

    def f_ple(i, x1b, pb, wple, wpg):
        return (x1b + _nn(pb, wple) * _sigmoid(_nn(x1b, wpg)),)

    (x2,) = _rows("f_ple", f_ple, [(x1, "row"), (p, "row"), (w["w_ple"], "full"), (w["w_ple_gate"], "full")],
                  [((rows, d), F32, "row")], rows=rows, tile=tile)
    saved = dict(x=x, h=h, s5x=s5x, s5g=s5g, lrux=lrux, lrug=lrug, gs=gs, gl=gl, ys=ys, st=st, a=a, hl=hl, z_s=z_s,
                 z_l=z_l, mix=mix, x1=x1, p=p)
    return x2, saved


def _layer_bwd(dx2, sv, w, rows):
    tile = ROW_TILE
    d = D_MODEL
    g = {}

    def b_ple(i, dxb, x1b, pb, wple, wpg):
        pe = _nn(pb, wple)
        sg = _sigmoid(_nn(x1b, wpg))
        dpe = dxb * sg
        dgt = dxb * pe * sg * (1.0 - sg)
        return dxb + _nt(dgt, wpg), _tn(pb, dpe), _tn(x1b, dgt)

    dx1, g["w_ple"], g["w_ple_gate"] = _rows(
        "b_ple", b_ple, [(dx2, "row"), (sv["x1"], "row"), (sv["p"], "row"), (w["w_ple"], "full"), (w["w_ple_gate"], "full")],
        [((rows, d), F32, "row"), ((PLE_DIM, d), F32, "acc"), ((d, d), F32, "acc")], rows=rows, tile=tile)

    def b_merge(i, dxb, mixb, zs, zl, gsb, glb, wout, gpost):
        rstd = lax.rsqrt(jnp.mean(mixb * mixb, axis=-1, keepdims=True) + NORM_EPS)
        nrm = mixb * rstd
        dn = dxb * gpost
        dmix = rstd * (dn - nrm * jnp.mean(dn * nrm, axis=-1, keepdims=True))
        ss, sl = _sigmoid(gsb), _sigmoid(glb)
        merged = ss * zs + sl * zl
        dm = _nt(dmix, wout)
        return (dm * ss, dm * sl, dm * zs * ss * (1.0 - ss), dm * zl * sl * (1.0 - sl),
                _tn(merged, dmix), jnp.sum(dxb * nrm, axis=0, keepdims=True))

    dz_s, dz_l, dgs, dgl, g["w_out"], g["g_post"] = _rows(
        "b_merge", b_merge,
        [(dx1, "row"), (sv["mix"], "row"), (sv["z_s"], "row"), (sv["z_l"], "row"), (sv["gs"], "row"), (sv["gl"], "row"),
         (w["w_out"], "full"), (w["g_post"], "full")],
        [((rows, d), F32, "row")] * 4 + [((d, d), F32, "acc"), ((1, d), F32, "acc")], rows=rows, tile=tile)

    def b_bl(i, dzl, hb, lg, wbl):
        sl = _sigmoid(lg)
        silu = lg * sl
        dy3 = _nt(dzl, wbl)
        return dy3 * silu, dy3 * hb * sl * (1.0 + lg * (1.0 - sl)), _tn(hb * silu, dzl)

    dh, dlrug, g["w_bl"] = _rows(
        "b_bl", b_bl, [(dz_l, "row"), (sv["hl"], "row"), (sv["lrug"], "row"), (w["w_bl"], "full")],
        [((rows, LRU_WIDTH), F32, "row")] * 2 + [((LRU_WIDTH, d), F32, "acc")], rows=rows, tile=tile)

    gh = _lru_scan_bwd(dh, sv["a"], rows=rows, tile=tile)

    def b_gates(i, ghb, hb, hprev, xb, xprev, cw, cb, wa, ba, wx, bx, lam):
        c = _conv_fwd(i, xb, xprev, cw, cb)
        r, ig, sp, a, mult = _lru_gates(c, wa, ba, wx, bx, lam)
        h_before = _rows_before(hb, hprev * (i > 0).astype(F32), 1)
        da = ghb * h_before
        dmult = ghb * ig * c
        dlog_a = da * a - dmult * a * a / mult
        dpre_r = dlog_a * (-LRU_C) * sp * r * (1.0 - r)
        dpre_i = ghb * mult * c * ig * (1.0 - ig)
        dc = ghb * mult * ig + _heads(_nt, dpre_r, wa) + _heads(_nt, dpre_i, wx)
        dlam = jnp.sum(dlog_a * LRU_C * r, axis=0, keepdims=True) * _sigmoid(-lam)
        return (dc, _heads_tn(c, dpre_r), _heads_tn(c, dpre_i), jnp.sum(dpre_r, axis=0, keepdims=True),
                jnp.sum(dpre_i, axis=0, keepdims=True), dlam)

    hshape = (LRU_HEADS, LRU_HEAD_DIM, LRU_HEAD_DIM)
    dc, g["lru_w_a"], g["lru_w_x"], g["lru_b_a"], g["lru_b_x"], g["lru_lambda"] = _rows(
        "b_gates", b_gates,
        [(gh, "row"), (sv["hl"], "row"), (sv["hl"], "prev"), (sv["lrux"], "row"), (sv["lrux"], "prev"),
         (w["conv_w"], "full"), (w["conv_b"], "full"), (w["lru_w_a"], "full"), (w["lru_b_a"], "full"),
         (w["lru_w_x"], "full"), (w["lru_b_x"], "full"), (w["lru_lambda"], "full")],
        [((rows, LRU_WIDTH), F32, "row"), (hshape, F32, "acc"), (hshape, F32, "acc")] + [((1, LRU_WIDTH), F32, "acc")] * 3,
        rows=rows, tile=tile)

    n_tiles = rows // min(tile, rows)

    def b_conv(i, dcb, dnext, xb, xprev, cw):
        dnext = dnext * (i < n_tiles - 1).astype(F32)
        xprev = xprev * (i > 0).astype(F32)
        dx = dcb * cw[3:4, :]
        dws = [jnp.sum(dcb * xb, axis=0, keepdims=True)]
        for s in range(1, CONV_WIDTH):
            dx = dx + _rows_after(dcb, dnext, s) * cw[3 - s:4 - s, :]
            dws.append(jnp.sum(dcb * _rows_before(xb, xprev, s), axis=0, keepdims=True))
        return dx, jnp.concatenate(dws[::-1], axis=0), jnp.sum(dcb, axis=0, keepdims=True)

    dlrux, g["conv_w"], g["conv_b"] = _rows(
        "b_conv", b_conv, [(dc, "row"), (dc, "next"), (sv["lrux"], "row"), (sv["lrux"], "prev"), (w["conv_w"], "full")],
        [((rows, LRU_WIDTH), F32, "row"), ((CONV_WIDTH, LRU_WIDTH), F32, "acc"), ((1, LRU_WIDTH), F32, "acc")],
        rows=rows, tile=tile)

    def b_s5post(i, dzs, ysb, gb, wglu, wbs):
        glv, dgelu = _gelu_parts(ysb)
        glu = _nn(glv, wglu)
        ga, gb2 = glu[:, :S5_WIDTH], glu[:, S5_WIDTH:]
        sb = _sigmoid(gb2)
        sg = _sigmoid(gb)
        silu = gb * sg
        y2 = ga * sb * silu
        dy2 = _nt(dzs, wbs)
        dglu = jnp.concatenate([dy2 * sb * silu, dy2 * ga * silu * sb * (1.0 - sb)], axis=1)
        dg = dy2 * ga * sb * sg * (1.0 + gb * (1.0 - sg))
        return _nt(dglu, wglu) * dgelu, dg, _tn(y2, dzs), _tn(glv, dglu)

    dys, ds5g, g["w_bs"], g["w_glu"] = _rows(
        "b_s5post", b_s5post, [(dz_s, "row"), (sv["ys"], "row"), (sv["s5g"], "row"), (w["w_glu"], "full"), (w["w_bs"], "full")],
        [((rows, S5_WIDTH), F32, "row")] * 2 + [((S5_WIDTH, d), F32, "acc"), ((S5_WIDTH, 2 * S5_WIDTH), F32, "acc")],
        rows=rows, tile=tile)

    ds5x, g["bd"], g["cdt"], g["s5_d"], g["abar"] = _s5_bwd(dys, sv["st"], sv["s5x"], w["bd"], w["cdt"], w["s5_d"],
                                                            w["scb"], rows=rows, tile=tile)

    dcomps = [ds5x, ds5g, dlrux, dlrug, dgs, dgl]

    def b_in(i, xb, dx1b, gpre, *rest):
        dcs, ws = rest[:6], rest[6:]
        dh = _nt(dcs[0], ws[0])
        for dcv, wc in zip(dcs[1:], ws[1:]):
            dh = dh + _nt(dcv, wc)
        rstd = lax.rsqrt(jnp.mean(xb * xb, axis=-1, keepdims=True) + NORM_EPS)
        nrm = xb * rstd
        dn = dh * gpre
        dx = rstd * (dn - nrm * jnp.mean(dn * nrm, axis=-1, keepdims=True))
        return dx1b + dx, jnp.sum(dh * nrm, axis=0, keepdims=True)

    dx, g["g_pre"] = _rows(
        "b_in", b_in, [(sv["x"], "row"), (dx1, "row"), (w["g_pre"], "full")] + [(dcv, "row") for dcv in dcomps]
        + [(wc, "full") for wc in w["w_in"]],
        [((rows, d), F32, "row"), ((1, d), F32, "acc")], rows=rows, tile=tile)

    def b_win(i, hb, dcv):
        return (_tn(hb, dcv),)

    g["w_in"] = jnp.concatenate(
        [_rows("b_win", b_win, [(sv["h"], "row"), (dcv, "row")], [((d, dcv.shape[1]), F32, "acc")], rows=rows, tile=tile)[0]
         for dcv in dcomps], axis=1)
    return dx, g


SMALL = ("g_pre", "s5_a_re", "s5_a_im", "s5_log_dt", "s5_b_re", "s5_b_im", "s5_c_re", "s5_c_im", "s5_d", "conv_b",
         "lru_w_a", "lru_b_a", "lru_w_x", "lru_b_x", "lru_lambda", "g_post")
BIG = ("w_in", "w_glu", "w_bs", "conv_w", "w_bl", "w_out", "w_ple", "w_ple_gate")
BIG_SHARD_AXIS = {"w_in": 1, "w_glu": 1, "w_bs": 1, "conv_w": 1, "w_bl": 0, "w_out": 0, "w_ple": 1, "w_ple_gate": 0}


def _bcast_groups(v):
    return jnp.broadcast_to(v[:, None, :], (S5_GROUPS, S5_GROUP, S5_STATE)).reshape(S5_WIDTH, S5_STATE)


def _s5_prep_inputs(wl):
    ldt = jnp.broadcast_to(wl["s5_log_dt"][:, None], (S5_GROUPS, S5_STATE))
    gcn = lambda b: jnp.transpose(b, (0, 2, 1)).reshape(S5_WIDTH, S5_STATE)
    return (_bcast_groups(wl["s5_a_re"]), _bcast_groups(wl["s5_a_im"]), _bcast_groups(ldt), gcn(wl["s5_b_re"]),
            gcn(wl["s5_b_im"]))


def _layer_weights(wl):
    w = {}
    offs = [0]
    for wd in IN_WIDTHS:
        offs.append(offs[-1] + wd)
    w["w_in"] = [wl["w_in"][:, offs[k]:offs[k + 1]] for k in range(6)]
    for k in ("w_glu", "w_bs", "w_bl", "w_out", "w_ple", "w_ple_gate"):
        w[k] = wl[k]
    w["conv_w"] = wl["conv_w"]
    for k in ("g_pre", "g_post", "s5_d", "conv_b", "lru_b_a", "lru_b_x", "lru_lambda"):
        w[k] = wl[k].reshape(1, -1)
    w["lru_w_a"] = wl["lru_w_a"].astype(BF16)
    w["lru_w_x"] = wl["lru_w_x"].astype(BF16)
    prep_in = _s5_prep_inputs(wl)
    abr, abi, bbr, bbi = _s5_prep(*prep_in)
    w["prep_in"] = prep_in
    shape3 = (S5_GROUPS, S5_GROUP, S5_STATE)
    w["bd"] = _s5_block_diag([bbr.reshape(shape3), bbi.reshape(shape3)]).astype(BF16)
    w["cdt"] = _s5_block_diag([wl["s5_c_re"], -wl["s5_c_im"]]).astype(BF16)
    abr_s = abr.reshape(shape3)[:, 0, :]
    abi_s = abi.reshape(shape3)[:, 0, :]
    w["scf"], w["scb"] = _s5_consts(_cplx_to_lanes(abr_s), _cplx_to_lanes(abi_s))
    return w


def _layer_param_grads(g, w, wl):
    out = {}
    shape3 = (S5_GROUPS, S5_GROUP, S5_STATE)
    dbr, dbi = _s5_block_diag_extract(g["bd"])
    dcr, dci = _s5_block_diag_extract(g["cdt"])
    out["s5_c_re"], out["s5_c_im"] = dcr, -dci
    zeros = jnp.zeros(shape3, F32)
    dar = zeros.at[:, 0, :].set(g["abar"][0].reshape(S5_GROUPS, S5_STATE)).reshape(S5_WIDTH, S5_STATE)
    dai = zeros.at[:, 0, :].set(g["abar"][1].reshape(S5_GROUPS, S5_STATE)).reshape(S5_WIDTH, S5_STATE)
    cts = (dar, dai, dbr.reshape(S5_WIDTH, S5_STATE), dbi.reshape(S5_WIDTH, S5_STATE))
    d_are, d_aim, d_ldt, d_bre, d_bim = _s5_prep_bwd(*w["prep_in"], cts)
    out["s5_a_re"] = d_are.reshape(shape3).sum(axis=1)
    out["s5_a_im"] = d_aim.reshape(shape3).sum(axis=1)
    out["s5_log_dt"] = d_ldt.reshape(shape3).sum(axis=(1, 2))
    out["s5_b_re"] = jnp.transpose(d_bre.reshape(shape3), (0, 2, 1))
    out["s5_b_im"] = jnp.transpose(d_bim.reshape(shape3), (0, 2, 1))
    out["s5_d"] = g["s5_d"].reshape(-1)
    for k in ("g_pre", "g_post", "conv_b", "lru_b_a", "lru_b_x", "lru_lambda"):
        out[k] = g[k].reshape(-1)
    for k in ("lru_w_a", "lru_w_x", "conv_w", "w_in", "w_glu", "w_bs", "w_bl", "w_out", "w_ple", "w_ple_gate"):
        out[k] = g[k]
    return out


def _local_step(x, p, layers, target):
    rows = x.shape[0]
    ws = [_layer_weights(wl) for wl in layers]
    saved = []
    for i in range(DEPTH):
        x, sv = _layer_fwd(x, p[i], ws[i], rows)
        saved.append(sv)

    def f_loss(i, yb, tb):
        e = yb - tb
        return e * (1.0 / D_MODEL), jnp.sum(jnp.sum(e * e, axis=0, keepdims=True), axis=1, keepdims=True)

    dx, sq = _rows("f_loss", f_loss, [(x, "row"), (target, "row")],
                   [((rows, D_MODEL), F32, "row"), ((1, 1), F32, "acc")], rows=rows, tile=ROW_TILE)
    loss = sq[0, 0] * (0.5 / D_MODEL)
    grads = [None] * DEPTH
    for i in reversed(range(DEPTH)):
        dx, g = _layer_bwd(dx, saved[i], ws[i], rows)
        grads[i] = _layer_param_grads(g, ws[i], layers[i])
    return loss, dx, grads


def _place():
    return lax.axis_index("x"), lax.axis_index("y"), lax.axis_index("c")


def _other_chips(x, y):
    return [(1 - x, y), (x, 1 - y), (1 - x, 1 - y)]


def _any_spec():
    return pl.BlockSpec(memory_space=pl.ANY)


def _gather_chips(name, v, via_sibling):
    rows = v.shape[0]
    half = rows // 2

    def body(v_ref, out_ref, send_sems, recv_sems, local_sem):
        x, y, c = _place()
        me = 2 * x + y
        chips = _other_chips(x, y)
        slots = [2 * cx + cy for cx, cy in chips]
        mine = pltpu.make_async_copy(v_ref, out_ref.at[me], local_sem)
        mine.start()

        def part(slot, hc):
            return out_ref.at[slot, pl.ds(hc * half, half), :] if via_sibling else out_ref.at[slot]

        def copy(k, src, dst, to):
            return pltpu.make_async_remote_copy(src_ref=src, dst_ref=dst, send_sem=send_sems.at[k], recv_sem=recv_sems.at[k],
                                                device_id=to, device_id_type=MESH)

        src = v_ref.at[pl.ds(c * half, half), :] if via_sibling else v_ref
        first = [copy(k, src, part(me, c), (*chips[k], c)) for k in range(3)]
        for cp in first:
            cp.start()
        passed = []
        for k in range(3):
            copy(k, src, part(slots[k], c), (*chips[k], c)).wait_recv()
            if via_sibling:
                fwd = copy(3 + k, part(slots[k], c), part(slots[k], c), (x, y, 1 - c))
                fwd.start()
                passed.append(fwd)
        if via_sibling:
            for k in range(3):
                copy(3 + k, src, part(slots[k], 1 - c), (x, y, 1 - c)).wait_recv()
        for cp in first + passed:
            cp.wait_send()
        mine.wait()

    n_sem = 6 if via_sibling else 3
    return pl.pallas_call(
        body, name=name, out_shape=jax.ShapeDtypeStruct((4,) + v.shape, v.dtype),
        in_specs=[_any_spec()], out_specs=_any_spec(),
        scratch_shapes=[pltpu.SemaphoreType.DMA((n_sem,)), pltpu.SemaphoreType.DMA((n_sem,)), pltpu.SemaphoreType.DMA],
    )(v)


def _rs_sibling(gr):
    half = gr.shape[1] // 2

    def body(g_ref, mine_ref, got_ref, send_sem, recv_sem, local_sem):
        x, y, c = _place()
        keep = pltpu.make_async_copy(g_ref.at[:, pl.ds(c * half, half), :], mine_ref, local_sem)
        keep.start()
        give = pltpu.make_async_remote_copy(src_ref=g_ref.at[:, pl.ds((1 - c) * half, half), :], dst_ref=got_ref,
                                            send_sem=send_sem, recv_sem=recv_sem, device_id=(x, y, 1 - c), device_id_type=MESH)
        give.start()
        give.wait()
        keep.wait()

    sd = jax.ShapeDtypeStruct((4, half, LANES), F32)
    return pl.pallas_call(
        body, name="rs_sibling", out_shape=[sd, sd], in_specs=[_any_spec()], out_specs=[_any_spec(), _any_spec()],
        scratch_shapes=[pltpu.SemaphoreType.DMA, pltpu.SemaphoreType.DMA, pltpu.SemaphoreType.DMA],
    )(gr)


def _rs_chips(a16, a32):
    half = a16.shape[1]

    def body(a16_ref, a32_ref, got_ref, own_ref, send_sems, recv_sems, local_sem):
        x, y, c = _place()
        chips = _other_chips(x, y)
        keep = pltpu.make_async_copy(a32_ref.at[2 * x + y], own_ref, local_sem)
        keep.start()
        copies = [pltpu.make_async_remote_copy(src_ref=a16_ref.at[2 * cx + cy], dst_ref=got_ref.at[k], send_sem=send_sems.at[k],
                                               recv_sem=recv_sems.at[k], device_id=(cx, cy, c), device_id_type=MESH)
                  for k, (cx, cy) in enumerate(chips)]
        for cp in copies:
            cp.start()
        for cp in copies:
            cp.wait()
        keep.wait()

    return pl.pallas_call(
        body, name="rs_chips",
        out_shape=[jax.ShapeDtypeStruct((3, half, LANES), BF16), jax.ShapeDtypeStruct((half, LANES), F32)],
        in_specs=[_any_spec(), _any_spec()], out_specs=[_any_spec(), _any_spec()],
        scratch_shapes=[pltpu.SemaphoreType.DMA((3,)), pltpu.SemaphoreType.DMA((3,)), pltpu.SemaphoreType.DMA],
    )(a16, a32)


def _swap_halves(v):
    def body(v_ref, out_ref, send_sem, recv_sem, local_sem):
        x, y, c = _place()
        keep = pltpu.make_async_copy(v_ref, out_ref.at[c], local_sem)
        keep.start()
        give = pltpu.make_async_remote_copy(src_ref=v_ref, dst_ref=out_ref.at[c], send_sem=send_sem, recv_sem=recv_sem,
                                            device_id=(x, y, 1 - c), device_id_type=MESH)
        give.start()
        give.wait_send()
        pltpu.make_async_remote_copy(src_ref=v_ref, dst_ref=out_ref.at[1 - c], send_sem=send_sem, recv_sem=recv_sem,
                                     device_id=(x, y, 1 - c), device_id_type=MESH).wait_recv()
        keep.wait()

    return pl.pallas_call(
        body, name="swap_halves", out_shape=jax.ShapeDtypeStruct((2,) + v.shape, v.dtype),
        in_specs=[_any_spec()], out_specs=_any_spec(),
        scratch_shapes=[pltpu.SemaphoreType.DMA, pltpu.SemaphoreType.DMA, pltpu.SemaphoreType.DMA],
    )(v)


WIDE = 1024
PACK_TILE = 384
GRAD_ROWS_UNIT = 2 * PACK_TILE * (WIDE // LANES)


def _pack(parts, rows_unit, dtype):
    flat = jnp.concatenate([q.reshape(-1).astype(dtype) for q in parts])
    unit = rows_unit * LANES
    total = -(-flat.shape[0] // unit) * unit
    return jnp.pad(flat, (0, total - flat.shape[0])).reshape(-1, LANES)


def _unpack(flat, shapes):
    out, off = [], 0
    for s in shapes:
        n = 1
        for q in s:
            n *= q
        out.append(flat[off:off + n].reshape(s))
        off += n
    return out


def _to_slots(name, full):
    dp, r, c = full.shape
    if BIG_SHARD_AXIS[name] == 1:
        return jnp.transpose(full.reshape(dp, r, 4, c // 4), (2, 0, 1, 3)).reshape(4, -1)
    return jnp.transpose(full.reshape(dp, 4, r // 4, c), (1, 0, 2, 3)).reshape(4, -1)


def _from_slots(name, slots, shard_shape):
    dp, r, c = shard_shape
    v = slots.reshape(4, dp, r, c)
    if BIG_SHARD_AXIS[name] == 1:
        return jnp.transpose(v, (1, 2, 0, 3)).reshape(dp, r, 4 * c)
    return jnp.transpose(v, (1, 0, 2, 3)).reshape(dp, 4 * r, c)


def _adamw(name, w, g, m, v, tile):
    def fn(i, wb, gb, mb, vb):
        m2 = ADAM_B1 * mb + (1.0 - ADAM_B1) * gb
        v2 = ADAM_B2 * vb + (1.0 - ADAM_B2) * (gb * gb)
        m_hat = m2 / (1.0 - ADAM_B1 ** ADAM_STEP)
        v_hat = v2 / (1.0 - ADAM_B2 ** ADAM_STEP)
        return -ADAM_LR * (m_hat / (jnp.sqrt(v_hat) + ADAM_EPS) + ADAM_WD * wb), m2, v2

    return _rows(name, fn, [(w, "row"), (g, "row"), (m, "row"), (v, "row")], [(w.shape, F32, "row")] * 3,
                 rows=w.shape[0], tile=tile)


def _as_2d(a):
    return a.reshape(-1, a.shape[-1])


def _adam_tile(rows):
    for t in (256, 184, 128, 64, 32, 16, 8):
        if rows % t == 0:
            return t
    return rows


def kernel(x, p, g_pre, w_in, s5_a_re, s5_a_im, s5_log_dt, s5_b_re, s5_b_im, s5_c_re, s5_c_im, s5_d, w_glu, w_bs, conv_w, conv_b, lru_w_a, lru_b_a, lru_w_x, lru_b_x, lru_lambda, w_bl, w_out, g_post, w_ple, w_ple_gate, loss_target, m_g_pre, m_w_in, m_s5_a_re, m_s5_a_im, m_s5_log_dt, m_s5_b_re, m_s5_b_im, m_s5_c_re, m_s5_c_im, m_s5_d, m_w_glu, m_w_bs, m_conv_w, m_conv_b, m_lru_w_a, m_lru_b_a, m_lru_w_x, m_lru_b_x, m_lru_lambda, m_w_bl, m_w_out, m_g_post, m_w_ple, m_w_ple_gate, v_g_pre, v_w_in, v_s5_a_re, v_s5_a_im, v_s5_log_dt, v_s5_b_re, v_s5_b_im, v_s5_c_re, v_s5_c_im, v_s5_d, v_w_glu, v_w_bs, v_conv_w, v_conv_b, v_lru_w_a, v_lru_b_a, v_lru_w_x, v_lru_b_x, v_lru_lambda, v_w_bl, v_w_out, v_g_post, v_w_ple, v_w_ple_gate):
    wts = dict(g_pre=g_pre, w_in=w_in, s5_a_re=s5_a_re, s5_a_im=s5_a_im, s5_log_dt=s5_log_dt, s5_b_re=s5_b_re,
               s5_b_im=s5_b_im, s5_c_re=s5_c_re, s5_c_im=s5_c_im, s5_d=s5_d, w_glu=w_glu, w_bs=w_bs, conv_w=conv_w,
               conv_b=conv_b, lru_w_a=lru_w_a, lru_b_a=lru_b_a, lru_w_x=lru_w_x, lru_b_x=lru_b_x, lru_lambda=lru_lambda,
               w_bl=w_bl, w_out=w_out, g_post=g_post, w_ple=w_ple, w_ple_gate=w_ple_gate)
    mom1 = dict(g_pre=m_g_pre, w_in=m_w_in, s5_a_re=m_s5_a_re, s5_a_im=m_s5_a_im, s5_log_dt=m_s5_log_dt, s5_b_re=m_s5_b_re,
                s5_b_im=m_s5_b_im, s5_c_re=m_s5_c_re, s5_c_im=m_s5_c_im, s5_d=m_s5_d, w_glu=m_w_glu, w_bs=m_w_bs,
                conv_w=m_conv_w, conv_b=m_conv_b, lru_w_a=m_lru_w_a, lru_b_a=m_lru_b_a, lru_w_x=m_lru_w_x, lru_b_x=m_lru_b_x,
                lru_lambda=m_lru_lambda, w_bl=m_w_bl, w_out=m_w_out, g_post=m_g_post, w_ple=m_w_ple, w_ple_gate=m_w_ple_gate)
    mom2 = dict(g_pre=v_g_pre, w_in=v_w_in, s5_a_re=v_s5_a_re, s5_a_im=v_s5_a_im, s5_log_dt=v_s5_log_dt, s5_b_re=v_s5_b_re,
                s5_b_im=v_s5_b_im, s5_c_re=v_s5_c_re, s5_c_im=v_s5_c_im, s5_d=v_s5_d, w_glu=v_w_glu, w_bs=v_w_bs,
                conv_w=v_conv_w, conv_b=v_conv_b, lru_w_a=v_lru_w_a, lru_b_a=v_lru_b_a, lru_w_x=v_lru_w_x, lru_b_x=v_lru_b_x,
                lru_lambda=v_lru_lambda, w_bl=v_w_bl, w_out=v_w_out, g_post=v_g_post, w_ple=v_w_ple, w_ple_gate=v_w_ple_gate)
    names = list(wts)

    def wire(name):
        return lax.bitcast_convert_type(wts[name], BF16) if name == "conv_w" else wts[name].astype(BF16)

    wire_shapes = [wire(k).shape for k in BIG]
    gathered = _gather_chips("gather_weights", _pack([wire(k) for k in BIG], 32, BF16), via_sibling=True)
    per_chip = [_unpack(gathered[j].reshape(-1), wire_shapes) for j in range(4)]
    whole = {}
    for idx, k in enumerate(BIG):
        v = jnp.stack([per_chip[j][idx] for j in range(4)])
        if k == "conv_w":
            v = lax.bitcast_convert_type(v, F32)
        whole[k] = _from_slots(k, v.reshape(4, -1), wts[k].shape)
    layers = []
    for i in range(DEPTH):
        wl = {k: whole[k][i] for k in BIG}
        wl.update({k: wts[k][i] for k in SMALL})
        layers.append(wl)

    loss, grad_x, grads = _local_step(x[0], p[:, 0], layers, loss_target[0])
    loss = lax.psum(loss, ("x", "y", "c"))

    big_slots = jnp.concatenate([_to_slots(k, jnp.stack([grads[i][k] for i in range(DEPTH)])) for k in BIG], axis=1)
    small_shapes = [wts[k].shape for k in SMALL]
    small_flat = jnp.concatenate([jnp.stack([grads[i][k] for i in range(DEPTH)]).reshape(-1) for k in SMALL])
    n_small = small_flat.shape[0]
    small_q = -(-n_small // (4 * 8 * WIDE)) * 8 * WIDE
    small_slots = jnp.pad(small_flat, (0, 4 * small_q - n_small)).reshape(4, small_q)
    n_big = big_slots.shape[1]
    n_big_pad = -(-n_big // (8 * WIDE)) * 8 * WIDE
    n_slot = n_big_pad + small_q
    unit = GRAD_ROWS_UNIT * LANES
    n_slot_pad = -(-n_slot // unit) * unit
    gr = jnp.concatenate([jnp.pad(big_slots, ((0, 0), (0, n_big_pad - n_big))), small_slots,
                          jnp.zeros((4, n_slot_pad - n_slot), F32)], axis=1).reshape(4, -1, LANES)
    mine, got = _rs_sibling(gr)
    half = mine.shape[1]
    wide = lambda a: a.reshape(-1, WIDE)

    def f_add1(i, a, b):
        s = a + b
        return s, s

    a32, a16 = _rows("rs_add1", f_add1, [(wide(mine), "row"), (wide(got), "row")],
                     [(wide(mine).shape, F32, "row"), (wide(mine).shape, BF16, "row")], rows=wide(mine).shape[0], tile=384)
    got3, own = _rs_chips(a16.reshape(4, half, LANES), a32.reshape(4, half, LANES))

    def f_add2(i, o, g0, g1, g2):
        return (((o + g0.astype(F32)) + g1.astype(F32)) + g2.astype(F32),)

    (red_half,) = _rows("rs_add2", f_add2, [(wide(own), "row")] + [(wide(got3[k]), "row") for k in range(3)],
                        [(wide(own).shape, F32, "row")], rows=wide(own).shape[0], tile=384)
    red = _swap_halves(red_half.reshape(half, LANES)).reshape(-1)
    small_red = _gather_chips("gather_small", red[n_big_pad:n_big_pad + small_q].reshape(-1, LANES), via_sibling=False)
    small_red = small_red.reshape(-1)[:n_small]

    big_shapes = [wts[k].shape for k in BIG]
    grad_out = dict(zip(BIG, _unpack(red[:n_big], big_shapes)))
    grad_out.update(zip(SMALL, _unpack(small_red, small_shapes)))
    delta, new_m, new_v = {}, {}, {}
    for k in BIG:
        w2 = _as_2d(wts[k])
        res = _adamw("adamw_" + k, w2, _as_2d(grad_out[k]), _as_2d(mom1[k]), _as_2d(mom2[k]), _adam_tile(w2.shape[0]))
        delta[k], new_m[k], new_v[k] = [r.reshape(wts[k].shape) for r in res]
    pack_small = lambda d: jnp.pad(jnp.concatenate([d[k].reshape(-1) for k in SMALL]), (0, 4 * small_q - n_small)).reshape(-1, WIDE)
    res = _adamw("adamw_small", pack_small(wts), pack_small(grad_out), pack_small(mom1), pack_small(mom2),
                 _adam_tile(4 * small_q // WIDE))
    for d, r in zip((delta, new_m, new_v), res):
        d.update(zip(SMALL, _unpack(r.reshape(-1), small_shapes)))
    return (loss, grad_x[None], *[grad_out[k] for k in names], *[delta[k] for k in names],
            *[new_m[k] for k in names], *[new_v[k] for k in names])
```

```python
import jax
import jax.numpy as jnp
from jax import lax
from jax.experimental import pallas as pl
from jax.experimental.pallas import tpu as pltpu

F32 = jnp.float32
BF16 = jnp.bfloat16
MESH = pl.DeviceIdType.MESH

DEPTH = 2
D_MODEL = 1024
NORM_EPS = 1e-6
S5_WIDTH = 512
S5_GROUPS = 32
S5_GROUP = 16
S5_STATE = 64
LRU_WIDTH = 1280
LRU_HEADS = 10
LRU_HEAD_DIM = 128
LRU_C = 8.0
CONV_WIDTH = 4
PLE_DIM = 256
IN_WIDTHS = (S5_WIDTH, S5_WIDTH, LRU_WIDTH, LRU_WIDTH, D_MODEL, D_MODEL)
ADAM_LR = 0.001
ADAM_B1 = 0.9
ADAM_B2 = 0.999
ADAM_EPS = 1e-08
ADAM_WD = 0.01
ADAM_STEP = 10

SUBLANES = 8
LANES = 128
S5_HALF_IN = S5_WIDTH // 2
S5_CPLX = S5_GROUPS * S5_STATE
S5_HALF_CPLX = S5_CPLX // 2
S5_LANES = 2 * S5_CPLX
VMEM_LIMIT = 48 * 2 ** 20
ROW_TILE = 256


def _sigmoid(x):
    return 1.0 / (1.0 + jnp.exp(-x))


def _gelu_parts(x):
    k = 0.7978845608028654
    t = jnp.tanh(k * (x + 0.044715 * x * x * x))
    val = 0.5 * x * (1.0 + t)
    grad = 0.5 * (1.0 + t) + 0.5 * x * (1.0 - t * t) * k * (1.0 + 3.0 * 0.044715 * x * x)
    return val, grad


def _nn(a, w):
    return jnp.dot(a.astype(BF16), w.astype(BF16), preferred_element_type=F32)


def _nt(a, w):
    return lax.dot_general(a.astype(BF16), w.astype(BF16), (((1,), (1,)), ((), ())), preferred_element_type=F32)


def _tn(a, b):
    return lax.dot_general(a.astype(BF16), b.astype(BF16), (((0,), (0,)), ((), ())), preferred_element_type=F32)


def _heads(op, a, w):
    d = LRU_HEAD_DIM
    return jnp.concatenate([op(a[:, h * d:(h + 1) * d], w[h]) for h in range(LRU_HEADS)], axis=1)


def _heads_tn(a, b):
    d = LRU_HEAD_DIM
    return jnp.stack([_tn(a[:, h * d:(h + 1) * d], b[:, h * d:(h + 1) * d]) for h in range(LRU_HEADS)], axis=0)


def _rows_before(x, halo, s):
    main = pltpu.roll(x, s, 0)
    head = pltpu.roll(jnp.concatenate([halo, x[0:SUBLANES]], axis=0), s, 0)[SUBLANES:2 * SUBLANES]
    return jnp.concatenate([head, main[SUBLANES:]], axis=0)


def _rows_after(x, halo, s):
    n = x.shape[0]
    main = pltpu.roll(x, n - s, 0)
    tail = pltpu.roll(jnp.concatenate([x[n - SUBLANES:], halo], axis=0), 2 * SUBLANES - s, 0)[0:SUBLANES]
    return jnp.concatenate([main[:n - SUBLANES], tail], axis=0)


def _rows(name, fn, ins, outs, *, rows, tile):
    tile = min(tile, rows)
    n = rows // tile
    assert n * tile == rows, (name, rows, tile)
    per8 = tile // SUBLANES
    last8 = rows // SUBLANES - 1
    in_specs = []
    for arr, kind in ins:
        if kind == "row":
            in_specs.append(pl.BlockSpec((tile, arr.shape[1]), lambda i: (i, 0)))
        elif kind == "prev":
            in_specs.append(pl.BlockSpec((SUBLANES, arr.shape[1]), lambda i: (jnp.maximum(i * per8 - 1, 0), 0)))
        elif kind == "next":
            in_specs.append(pl.BlockSpec((SUBLANES, arr.shape[1]), lambda i: (jnp.minimum((i + 1) * per8, last8), 0)))
        else:
            in_specs.append(pl.BlockSpec(arr.shape, lambda i, nd=arr.ndim: (0,) * nd))
    out_shape, out_specs = [], []
    for shape, dtype, kind in outs:
        out_shape.append(jax.ShapeDtypeStruct(shape, dtype))
        if kind == "row":
            out_specs.append(pl.BlockSpec((tile, shape[1]), lambda i: (i, 0)))
        else:
            out_specs.append(pl.BlockSpec(shape, lambda i, nd=len(shape): (0,) * nd))
    n_in = len(ins)

    def body(*refs):
        i = pl.program_id(0)
        vals = fn(i, *[r[...] for r in refs[:n_in]])
        assert len(vals) == len(outs), name
        for r, v, (_, _, kind) in zip(refs[n_in:], vals, outs):
            if kind == "row":
                r[...] = v.astype(r.dtype)
            else:
                @pl.when(i == 0)
                def _():
                    r[...] = jnp.zeros_like(r)

                r[...] += v.astype(r.dtype)

    return pl.pallas_call(
        body, name=name, grid=(n,), in_specs=in_specs, out_specs=out_specs, out_shape=out_shape,
        compiler_params=pltpu.CompilerParams(dimension_semantics=("arbitrary",), vmem_limit_bytes=VMEM_LIMIT),
    )(*[a for a, _ in ins])


def _s5_discretise(are, aim, ldt, bre, bim):
    dt = jnp.exp(ldt)
    er = jnp.exp(are * dt)
    abr = er * jnp.cos(aim * dt)
    abi = er * jnp.sin(aim * dt)
    den = are * are + aim * aim
    zr = ((abr - 1.0) * are + abi * aim) / den
    zi = (abi * are - (abr - 1.0) * aim) / den
    return abr, abi, zr * bre - zi * bim, zr * bim + zi * bre


def _s5_prep(are, aim, ldt, bre, bim):
    def body(a, b, c, d, e, o0, o1, o2, o3):
        r = _s5_discretise(a[...], b[...], c[...], d[...], e[...])
        o0[...], o1[...], o2[...], o3[...] = r

    sd = jax.ShapeDtypeStruct(are.shape, F32)
    return pl.pallas_call(body, name="s5_prep", out_shape=[sd] * 4)(are, aim, ldt, bre, bim)


def _s5_prep_bwd(are, aim, ldt, bre, bim, cts):
    def body(a, b, c, d, e, c0, c1, c2, c3, o0, o1, o2, o3, o4):
        _, vjp = jax.vjp(_s5_discretise, a[...], b[...], c[...], d[...], e[...])
        r = vjp((c0[...], c1[...], c2[...], c3[...]))
        o0[...], o1[...], o2[...], o3[...], o4[...] = r

    sd = jax.ShapeDtypeStruct(are.shape, F32)
    return pl.pallas_call(body, name="s5_prep_bwd", out_shape=[sd] * 5)(are, aim, ldt, bre, bim, *cts)


def _s5_consts(abr, abi):
    shape = (SUBLANES, S5_CPLX)

    def body(ar_ref, ai_ref, f_ref, b_ref):
        ar = jnp.broadcast_to(ar_ref[...], shape)
        ai = jnp.broadcast_to(ai_ref[...], shape)
        row = lax.broadcasted_iota(jnp.int32, shape, 0)

        def cmul(p, q):
            return (p[0] * q[0] - p[1] * q[1], p[0] * q[1] + p[1] * q[0])

        a1 = (ar, ai)
        a2 = cmul(a1, a1)
        a3 = cmul(a2, a1)
        a4 = cmul(a2, a2)
        pw = [a1, a2, a3, a4, cmul(a4, a1), cmul(a4, a2), cmul(a4, a3), cmul(a4, a4)]

        def by_row(vals):
            out = vals[7]
            for r in range(6, -1, -1):
                out = jnp.where(row == r, vals[r], out)
            return out

        fwd, rev = [], []
        for a, k in ((a1, 1), (a2, 2), (a4, 4)):
            fwd += [jnp.where(row >= k, a[0], 0.0), jnp.where(row >= k, a[1], 0.0)]
            rev += [jnp.where(row <= 7 - k, a[0], 0.0), jnp.where(row <= 7 - k, -a[1], 0.0)]
        fwd += [by_row([p[0] for p in pw]), by_row([p[1] for p in pw])]
        rev += [by_row([pw[7 - r][0] for r in range(8)]), by_row([-pw[7 - r][1] for r in range(8)])]
        f_ref[...] = jnp.concatenate(fwd, axis=0)
        b_ref[...] = jnp.concatenate(rev, axis=0)

    sd = jax.ShapeDtypeStruct((8 * SUBLANES, S5_CPLX), F32)
    return pl.pallas_call(body, name="s5_consts", out_shape=[sd, sd])(abr, abi)


def _s5_lane_offsets(q):
    re = (q // 8) * 2 * S5_HALF_CPLX + (q % 8) * LANES
    return re, re + S5_HALF_CPLX


def _s5_scan(s_ref, sc_ref, carry_ref, nblk, reverse):
    group = 2
    edge = 0 if reverse else SUBLANES - 1
    for q0 in range(0, S5_CPLX // LANES, group):
        offs = [_s5_lane_offsets(q) for q in range(q0, q0 + group)]
        consts = [[sc_ref[k * SUBLANES:(k + 1) * SUBLANES, q * LANES:(q + 1) * LANES] for k in range(8)]
                  for q in range(q0, q0 + group)]
        carry0 = tuple(carry_ref[:, o:o + LANES] for pair in offs for o in pair)

        def blk(t, carry, offs=offs, consts=consts):
            b = (nblk - 1 - t) if reverse else t
            r0 = pl.multiple_of(b * SUBLANES, SUBLANES)
            new = []
            for j, ((re, im), (a1r, a1i, a2r, a2i, a4r, a4i, pr, pi)) in enumerate(zip(offs, consts)):
                xr = s_ref[pl.ds(r0, SUBLANES), re:re + LANES]
                xi = s_ref[pl.ds(r0, SUBLANES), im:im + LANES]
                for ar, ai, sh in ((a1r, a1i, 1), (a2r, a2i, 2), (a4r, a4i, 4)):
                    shift = SUBLANES - sh if reverse else sh
                    sr = pltpu.roll(xr, shift, 0)
                    si = pltpu.roll(xi, shift, 0)
                    xr, xi = xr + ar * sr - ai * si, xi + ar * si + ai * sr
                cr, ci = carry[2 * j], carry[2 * j + 1]
                xr, xi = xr + pr * cr - pi * ci, xi + pr * ci + pi * cr
                s_ref[pl.ds(r0, SUBLANES), re:re + LANES] = xr
                s_ref[pl.ds(r0, SUBLANES), im:im + LANES] = xi
                new.append(jnp.broadcast_to(xr[edge:edge + 1, :], (SUBLANES, LANES)))
                new.append(jnp.broadcast_to(xi[edge:edge + 1, :], (SUBLANES, LANES)))
            return tuple(new)

        carry = lax.fori_loop(0, nblk, blk, carry0, unroll=2)
        for k, o in enumerate(o for pair in offs for o in pair):
            carry_ref[:, o:o + LANES] = carry[k]


def _s5_fwd(u, bd, cdt, dskip, sc, *, rows, tile):
    n = rows // tile
    nblk = tile // SUBLANES
    hc = 2 * S5_HALF_CPLX

    def body(u_ref, bd_ref, cdt_ref, d_ref, sc_ref, y_ref, s_ref, carry_ref):
        @pl.when(pl.program_id(0) == 0)
        def _():
            carry_ref[...] = jnp.zeros_like(carry_ref)

        ub = u_ref[...].astype(BF16)
        for h in range(2):
            s_ref[:, h * hc:(h + 1) * hc] = jnp.dot(ub[:, h * S5_HALF_IN:(h + 1) * S5_HALF_IN], bd_ref[h],
                                                    preferred_element_type=F32)
        _s5_scan(s_ref, sc_ref, carry_ref, nblk, reverse=False)
        ys = [_nt(s_ref[:, h * hc:(h + 1) * hc], cdt_ref[h]) for h in range(2)]
        y_ref[...] = jnp.concatenate(ys, axis=1) + d_ref[...] * u_ref[...]

    full = lambda a: pl.BlockSpec(a.shape, lambda i, nd=a.ndim: (0,) * nd)
    return pl.pallas_call(
        body, name="s5_fwd", grid=(n,),
        in_specs=[pl.BlockSpec((tile, S5_WIDTH), lambda i: (i, 0)), full(bd), full(cdt), full(dskip), full(sc)],
        out_specs=[pl.BlockSpec((tile, S5_WIDTH), lambda i: (i, 0)), pl.BlockSpec((tile, S5_LANES), lambda i: (i, 0))],
        out_shape=[jax.ShapeDtypeStruct((rows, S5_WIDTH), F32), jax.ShapeDtypeStruct((rows, S5_LANES), F32)],
        scratch_shapes=[pltpu.VMEM((SUBLANES, S5_LANES), F32)],
        compiler_params=pltpu.CompilerParams(dimension_semantics=("arbitrary",), vmem_limit_bytes=VMEM_LIMIT),
    )(u, bd, cdt, dskip, sc)


def _s5_bwd(dy, s, u, bd, cdt, dskip, sc, *, rows, tile):
    n = rows // tile
    nblk = tile // SUBLANES
    hc = 2 * S5_HALF_CPLX
    per8 = tile // SUBLANES

    def body(dy_ref, s_ref, sp_ref, u_ref, bd_ref, cdt_ref, d_ref, sc_ref,
             du_ref, dbd_ref, dcdt_ref, dd_ref, da_ref, g_ref, carry_ref):
        i = pl.program_id(0)

        @pl.when(i == 0)
        def _():
            carry_ref[...] = jnp.zeros_like(carry_ref)
            dbd_ref[...] = jnp.zeros_like(dbd_ref)
            dcdt_ref[...] = jnp.zeros_like(dcdt_ref)
            dd_ref[...] = jnp.zeros_like(dd_ref)
            da_ref[...] = jnp.zeros_like(da_ref)

        dy = dy_ref[...]
        dyb = dy.astype(BF16)
        u = u_ref[...]
        ub = u.astype(BF16)
        for h in range(2):
            g_ref[:, h * hc:(h + 1) * hc] = jnp.dot(dyb[:, h * S5_HALF_IN:(h + 1) * S5_HALF_IN], cdt_ref[h],
                                                    preferred_element_type=F32)
        _s5_scan(g_ref, sc_ref, carry_ref, nblk, reverse=True)
        dus = []
        for h in range(2):
            gb = g_ref[:, h * hc:(h + 1) * hc].astype(BF16)
            sb = s_ref[:, h * hc:(h + 1) * hc].astype(BF16)
            dus.append(_nt(gb, bd_ref[h]))
            dbd_ref[h] += _tn(ub[:, h * S5_HALF_IN:(h + 1) * S5_HALF_IN], gb)
            dcdt_ref[h] += _tn(dyb[:, h * S5_HALF_IN:(h + 1) * S5_HALF_IN], sb)
        du_ref[...] = jnp.concatenate(dus, axis=1) + d_ref[...] * dy
        dd_ref[...] += jnp.sum(dy * u, axis=0, keepdims=True)

        not_first = (i < n - 1).astype(F32)
        row = lax.broadcasted_iota(jnp.int32, (SUBLANES, LANES), 0)
        for q in range(S5_CPLX // LANES):
            re, im = _s5_lane_offsets(q)
            pr0 = jnp.broadcast_to(sp_ref[SUBLANES - 1:SUBLANES, re:re + LANES], (SUBLANES, LANES)) * not_first
            pi0 = jnp.broadcast_to(sp_ref[SUBLANES - 1:SUBLANES, im:im + LANES], (SUBLANES, LANES)) * not_first
            zero = jnp.zeros((SUBLANES, LANES), F32)

            def blk(b, c, re=re, im=im):
                accr, acci, pr, pi = c
                r0 = pl.multiple_of(b * SUBLANES, SUBLANES)
                sr = s_ref[pl.ds(r0, SUBLANES), re:re + LANES]
                si = s_ref[pl.ds(r0, SUBLANES), im:im + LANES]
                gr = g_ref[pl.ds(r0, SUBLANES), re:re + LANES]
                gi = g_ref[pl.ds(r0, SUBLANES), im:im + LANES]
                ssr = jnp.where(row == 0, pr, pltpu.roll(sr, 1, 0))
                ssi = jnp.where(row == 0, pi, pltpu.roll(si, 1, 0))
                accr = accr + ssr * gr + ssi * gi
                acci = acci + ssr * gi - ssi * gr
                return (accr, acci, jnp.broadcast_to(sr[SUBLANES - 1:SUBLANES, :], (SUBLANES, LANES)),
                        jnp.broadcast_to(si[SUBLANES - 1:SUBLANES, :], (SUBLANES, LANES)))

            accr, acci, _, _ = lax.fori_loop(0, nblk, blk, (zero, zero, pr0, pi0), unroll=2)
            da_ref[0:1, q * LANES:(q + 1) * LANES] += jnp.sum(accr, axis=0, keepdims=True)
            da_ref[1:2, q * LANES:(q + 1) * LANES] += jnp.sum(acci, axis=0, keepdims=True)

    full = lambda a: pl.BlockSpec(a.shape, lambda i, nd=a.ndim: (0,) * nd)
    rev = lambda i: (n - 1 - i, 0)
    wshape = (2, S5_HALF_IN, hc)
    return pl.pallas_call(
        body, name="s5_bwd", grid=(n,),
        in_specs=[pl.BlockSpec((tile, S5_WIDTH), rev), pl.BlockSpec((tile, S5_LANES), rev),
                  pl.BlockSpec((SUBLANES, S5_LANES), lambda i: (jnp.maximum((n - 1 - i) * per8 - 1, 0), 0)),
                  pl.BlockSpec((tile, S5_WIDTH), rev), full(bd), full(cdt), full(dskip), full(sc)],
        out_specs=[pl.BlockSpec((tile, S5_WIDTH), rev),
                   pl.BlockSpec(wshape, lambda i: (0, 0, 0)), pl.BlockSpec(wshape, lambda i: (0, 0, 0)),
                   pl.BlockSpec((1, S5_WIDTH), lambda i: (0, 0)), pl.BlockSpec((SUBLANES, S5_CPLX), lambda i: (0, 0))],
        out_shape=[jax.ShapeDtypeStruct((rows, S5_WIDTH), F32), jax.ShapeDtypeStruct(wshape, F32),
                   jax.ShapeDtypeStruct(wshape, F32), jax.ShapeDtypeStruct((1, S5_WIDTH), F32),
                   jax.ShapeDtypeStruct((SUBLANES, S5_CPLX), F32)],
        scratch_shapes=[pltpu.VMEM((tile, S5_LANES), F32), pltpu.VMEM((SUBLANES, S5_LANES), F32)],
        compiler_params=pltpu.CompilerParams(dimension_semantics=("arbitrary",), vmem_limit_bytes=VMEM_LIMIT),
    )(dy, s, s, u, bd, cdt, dskip, sc)


def _s5_block_diag(parts):
    v = jnp.stack(parts, axis=2).reshape(2, 16, S5_GROUP, 2, S5_STATE)
    eye = jnp.eye(16, dtype=v.dtype)
    return jnp.einsum("hgcpn,gk->hgcpkn", v, eye).reshape(2, S5_HALF_IN, 2 * S5_HALF_CPLX)


def _s5_block_diag_extract(m):
    v = m.reshape(2, 16, S5_GROUP, 2, 16, S5_STATE)
    d = jnp.diagonal(v, axis1=1, axis2=4)
    d = jnp.transpose(d, (2, 0, 4, 1, 3)).reshape(2, S5_GROUPS, S5_GROUP, S5_STATE)
    return d[0], d[1]


def _cplx_to_lanes(v):
    return v.reshape(1, S5_CPLX)


def _lru_scan_fwd(a, b, *, rows, tile):
    n = rows // tile
    nblk = tile // SUBLANES
    group = 2

    def body(a_ref, b_ref, h_ref, carry_ref):
        @pl.when(pl.program_id(0) == 0)
        def _():
            carry_ref[...] = jnp.zeros_like(carry_ref)

        row = lax.broadcasted_iota(jnp.int32, (SUBLANES, LANES), 0)
        for q0 in range(0, LRU_WIDTH // LANES, group):
            offs = [q * LANES for q in range(q0, q0 + group)]

            def blk(t, carry, offs=offs):
                r0 = pl.multiple_of(t * SUBLANES, SUBLANES)
                new = []
                for j, o in enumerate(offs):
                    av = a_ref[pl.ds(r0, SUBLANES), o:o + LANES]
                    xv = b_ref[pl.ds(r0, SUBLANES), o:o + LANES]
                    for sh in (1, 2, 4):
                        m = row >= sh
                        xs = pltpu.roll(xv, sh, 0)
                        asft = pltpu.roll(av, sh, 0)
                        xv = xv + jnp.where(m, av * xs, 0.0)
                        av = jnp.where(m, av * asft, av)
                    hv = xv + av * carry[j]
                    h_ref[pl.ds(r0, SUBLANES), o:o + LANES] = hv
                    new.append(jnp.broadcast_to(hv[SUBLANES - 1:SUBLANES, :], (SUBLANES, LANES)))
                return tuple(new)

            carry = lax.fori_loop(0, nblk, blk, tuple(carry_ref[:, o:o + LANES] for o in offs), unroll=2)
            for j, o in enumerate(offs):
                carry_ref[:, o:o + LANES] = carry[j]

    spec = pl.BlockSpec((tile, LRU_WIDTH), lambda i: (i, 0))
    return pl.pallas_call(
        body, name="lru_scan_fwd", grid=(n,), in_specs=[spec, spec], out_specs=spec,
        out_shape=jax.ShapeDtypeStruct((rows, LRU_WIDTH), F32),
        scratch_shapes=[pltpu.VMEM((SUBLANES, LRU_WIDTH), F32)],
        compiler_params=pltpu.CompilerParams(dimension_semantics=("arbitrary",), vmem_limit_bytes=VMEM_LIMIT),
    )(a, b)


def _lru_scan_bwd(dh, a, *, rows, tile):
    n = rows // tile
    nblk = tile // SUBLANES
    group = 2

    def body(dh_ref, a_ref, g_ref, cg_ref, ca_ref):
        @pl.when(pl.program_id(0) == 0)
        def _():
            cg_ref[...] = jnp.zeros_like(cg_ref)
            ca_ref[...] = jnp.zeros_like(ca_ref)

        row = lax.broadcasted_iota(jnp.int32, (SUBLANES, LANES), 0)
        for q0 in range(0, LRU_WIDTH // LANES, group):
            offs = [q * LANES for q in range(q0, q0 + group)]

            def blk(t, carry, offs=offs):
                r0 = pl.multiple_of((nblk - 1 - t) * SUBLANES, SUBLANES)
                new = []
                for j, o in enumerate(offs):
                    cg, ca = carry[2 * j], carry[2 * j + 1]
                    araw = a_ref[pl.ds(r0, SUBLANES), o:o + LANES]
                    xv = dh_ref[pl.ds(r0, SUBLANES), o:o + LANES]
                    av = jnp.where(row == SUBLANES - 1, ca, pltpu.roll(araw, SUBLANES - 1, 0))
                    for sh in (1, 2, 4):
                        m = row <= SUBLANES - 1 - sh
                        xs = pltpu.roll(xv, SUBLANES - sh, 0)
                        asft = pltpu.roll(av, SUBLANES - sh, 0)
                        xv = xv + jnp.where(m, av * xs, 0.0)
                        av = jnp.where(m, av * asft, av)
                    gv = xv + av * cg
                    g_ref[pl.ds(r0, SUBLANES), o:o + LANES] = gv
                    new.append(jnp.broadcast_to(gv[0:1, :], (SUBLANES, LANES)))
                    new.append(jnp.broadcast_to(araw[0:1, :], (SUBLANES, LANES)))
                return tuple(new)

            carry0 = tuple(r[:, o:o + LANES] for o in offs for r in (cg_ref, ca_ref))
            carry = lax.fori_loop(0, nblk, blk, carry0, unroll=2)
            for j, o in enumerate(offs):
                cg_ref[:, o:o + LANES] = carry[2 * j]
                ca_ref[:, o:o + LANES] = carry[2 * j + 1]

    spec = pl.BlockSpec((tile, LRU_WIDTH), lambda i: (n - 1 - i, 0))
    return pl.pallas_call(
        body, name="lru_scan_bwd", grid=(n,), in_specs=[spec, spec], out_specs=spec,
        out_shape=jax.ShapeDtypeStruct((rows, LRU_WIDTH), F32),
        scratch_shapes=[pltpu.VMEM((SUBLANES, LRU_WIDTH), F32), pltpu.VMEM((SUBLANES, LRU_WIDTH), F32)],
        compiler_params=pltpu.CompilerParams(dimension_semantics=("arbitrary",), vmem_limit_bytes=VMEM_LIMIT),
    )(dh, a)


def _conv_fwd(i, x, prev, cw, cb):
    prev = prev * (i > 0).astype(F32)
    y = x * cw[3:4, :] + cb
    for s in range(1, CONV_WIDTH):
        y = y + _rows_before(x, prev, s) * cw[3 - s:4 - s, :]
    return y


def _lru_gates(c, wa, ba, wx, bx, lam):
    r = _sigmoid(_heads(_nn, c, wa) + ba)
    ig = _sigmoid(_heads(_nn, c, wx) + bx)
    z = -lam
    sp = jnp.maximum(z, 0.0) + jnp.log(1.0 + jnp.exp(-jnp.abs(z)))
    log_a = -LRU_C * r * sp
    a = jnp.exp(log_a)
    z2 = 2.0 * log_a
    series = -z2 * (1.0 + z2 * (0.5 + z2 * (1.0 / 6.0 + z2 * (1.0 / 24.0 + z2 * (1.0 / 120.0 + z2 / 720.0)))))
    one_minus = jnp.where(z2 > -0.2, series, 1.0 - jnp.exp(z2))
    mult = jnp.sqrt(one_minus)
    return r, ig, sp, a, mult


def _layer_fwd(x, p, w, rows):
    tile = ROW_TILE
    d = D_MODEL

    def f_in(i, xb, g, *ws):
        rstd = lax.rsqrt(jnp.mean(xb * xb, axis=-1, keepdims=True) + NORM_EPS)
        hb = (xb * rstd * g).astype(BF16)
        return tuple(jnp.dot(hb, wc, preferred_element_type=F32) for wc in ws) + (hb,)

    s5x, s5g, lrux, lrug, gs, gl, h = _rows(
        "f_in", f_in, [(x, "row"), (w["g_pre"], "full")] + [(wc, "full") for wc in w["w_in"]],
        [((rows, wd), F32, "row") for wd in IN_WIDTHS] + [((rows, d), BF16, "row")], rows=rows, tile=tile)

    ys, st = _s5_fwd(s5x, w["bd"], w["cdt"], w["s5_d"], w["scf"], rows=rows, tile=tile)

    def f_s5post(i, ysb, gb, wglu, wbs):
        glv, _ = _gelu_parts(ysb)
        glu = _nn(glv, wglu)
        y2 = glu[:, :S5_WIDTH] * _sigmoid(glu[:, S5_WIDTH:]) * (gb * _sigmoid(gb))
        return (_nn(y2, wbs),)

    (z_s,) = _rows("f_s5post", f_s5post, [(ys, "row"), (s5g, "row"), (w["w_glu"], "full"), (w["w_bs"], "full")],
                   [((rows, d), F32, "row")], rows=rows, tile=tile)

    def f_gates(i, xb, prev, cw, cb, wa, ba, wx, bx, lam):
        c = _conv_fwd(i, xb, prev, cw, cb)
        _, ig, _, a, mult = _lru_gates(c, wa, ba, wx, bx, lam)
        return a, mult * (ig * c)

    a, b = _rows("f_gates", f_gates,
                 [(lrux, "row"), (lrux, "prev"), (w["conv_w"], "full"), (w["conv_b"], "full"), (w["lru_w_a"], "full"),
                  (w["lru_b_a"], "full"), (w["lru_w_x"], "full"), (w["lru_b_x"], "full"), (w["lru_lambda"], "full")],
                 [((rows, LRU_WIDTH), F32, "row")] * 2, rows=rows, tile=tile)
    hl = _lru_scan_fwd(a, b, rows=rows, tile=tile)

    def f_merge(i, hb, lg, zs, gsb, glb, xb, wbl, wout, gpost):
        z_l = _nn(hb * (lg * _sigmoid(lg)), wbl)
        merged = _sigmoid(gsb) * zs + _sigmoid(glb) * z_l
        mix = _nn(merged, wout)
        rstd = lax.rsqrt(jnp.mean(mix * mix, axis=-1, keepdims=True) + NORM_EPS)
        return xb + mix * rstd * gpost, mix, z_l

    x1, mix, z_l = _rows("f_merge", f_merge,
                         [(hl, "row"), (lrug, "row"), (z_s, "row"), (gs, "row"), (gl, "row"), (x, "row"),
                          (w["w_bl"], "full"), (w["w_out"], "full"), (w["g_post"], "full")],
                         [((rows, d), F32, "row")] * 3, rows=rows, tile=tile)

    def f_ple(i, x1b, pb, wple, wpg):
        return (x1b + _nn(pb, wple) * _sigmoid(_nn(x1b, wpg)),)

    (x2,) = _rows("f_ple", f_ple, [(x1, "row"), (p, "row"), (w["w_ple"], "full"), (w["w_ple_gate"], "full")],
                  [((rows, d), F32, "row")], rows=rows, tile=tile)
    saved = dict(x=x, h=h, s5x=s5x, s5g=s5g, lrux=lrux, lrug=lrug, gs=gs, gl=gl, ys=ys, st=st, a=a, hl=hl, z_s=z_s,
                 z_l=z_l, mix=mix, x1=x1, p=p)
    return x2, saved


def _layer_bwd(dx2, sv, w, rows):
    tile = ROW_TILE
    d = D_MODEL
    g = {}

    def b_ple(i, dxb, x1b, pb, wple, wpg):
        pe = _nn(pb, wple)
        sg = _sigmoid(_nn(x1b, wpg))
        dpe = dxb * sg
        dgt = dxb * pe * sg * (1.0 - sg)
        return dxb + _nt(dgt, wpg), _tn(pb, dpe), _tn(x1b, dgt)

    dx1, g["w_ple"], g["w_ple_gate"] = _rows(
        "b_ple", b_ple, [(dx2, "row"), (sv["x1"], "row"), (sv["p"], "row"), (w["w_ple"], "full"), (w["w_ple_gate"], "full")],
        [((rows, d), F32, "row"), ((PLE_DIM, d), F32, "acc"), ((d, d), F32, "acc")], rows=rows, tile=tile)

    def b_merge(i, dxb, mixb, zs, zl, gsb, glb, wout, gpost):
        rstd = lax.rsqrt(jnp.mean(mixb * mixb, axis=-1, keepdims=True) + NORM_EPS)
        nrm = mixb * rstd
        dn = dxb * gpost
        dmix = rstd * (dn - nrm * jnp.mean(dn * nrm, axis=-1, keepdims=True))
        ss, sl = _sigmoid(gsb), _sigmoid(glb)
        merged = ss * zs + sl * zl
        dm = _nt(dmix, wout)
        return (dm * ss, dm * sl, dm * zs * ss * (1.0 - ss), dm * zl * sl * (1.0 - sl),
                _tn(merged, dmix), jnp.sum(dxb * nrm, axis=0, keepdims=True))

    dz_s, dz_l, dgs, dgl, g["w_out"], g["g_post"] = _rows(
        "b_merge", b_merge,
        [(dx1, "row"), (sv["mix"], "row"), (sv["z_s"], "row"), (sv["z_l"], "row"), (sv["gs"], "row"), (sv["gl"], "row"),
         (w["w_out"], "full"), (w["g_post"], "full")],
        [((rows, d), F32, "row")] * 4 + [((d, d), F32, "acc"), ((1, d), F32, "acc")], rows=rows, tile=tile)

    def b_bl(i, dzl, hb, lg, wbl):
        sl = _sigmoid(lg)
        silu = lg * sl
        dy3 = _nt(dzl, wbl)
        return dy3 * silu, dy3 * hb * sl * (1.0 + lg * (1.0 - sl)), _tn(hb * silu, dzl)

    dh, dlrug, g["w_bl"] = _rows(
        "b_bl", b_bl, [(dz_l, "row"), (sv["hl"], "row"), (sv["lrug"], "row"), (w["w_bl"], "full")],
        [((rows, LRU_WIDTH), F32, "row")] * 2 + [((LRU_WIDTH, d), F32, "acc")], rows=rows, tile=tile)

    gh = _lru_scan_bwd(dh, sv["a"], rows=rows, tile=tile)

    def b_gates(i, ghb, hb, hprev, xb, xprev, cw, cb, wa, ba, wx, bx, lam):
        c = _conv_fwd(i, xb, xprev, cw, cb)
        r, ig, sp, a, mult = _lru_gates(c, wa, ba, wx, bx, lam)
        h_before = _rows_before(hb, hprev * (i > 0).astype(F32), 1)
        da = ghb * h_before
        dmult = ghb * ig * c
        dlog_a = da * a - dmult * a * a / mult
        dpre_r = dlog_a * (-LRU_C) * sp * r * (1.0 - r)
        dpre_i = ghb * mult * c * ig * (1.0 - ig)
        dc = ghb * mult * ig + _heads(_nt, dpre_r, wa) + _heads(_nt, dpre_i, wx)
        dlam = jnp.sum(dlog_a * LRU_C * r, axis=0, keepdims=True) * _sigmoid(-lam)
        return (dc, _heads_tn(c, dpre_r), _heads_tn(c, dpre_i), jnp.sum(dpre_r, axis=0, keepdims=True),
                jnp.sum(dpre_i, axis=0, keepdims=True), dlam)

    hshape = (LRU_HEADS, LRU_HEAD_DIM, LRU_HEAD_DIM)
    dc, g["lru_w_a"], g["lru_w_x"], g["lru_b_a"], g["lru_b_x"], g["lru_lambda"] = _rows(
        "b_gates", b_gates,
        [(gh, "row"), (sv["hl"], "row"), (sv["hl"], "prev"), (sv["lrux"], "row"), (sv["lrux"], "prev"),
         (w["conv_w"], "full"), (w["conv_b"], "full"), (w["lru_w_a"], "full"), (w["lru_b_a"], "full"),
         (w["lru_w_x"], "full"), (w["lru_b_x"], "full"), (w["lru_lambda"], "full")],
        [((rows, LRU_WIDTH), F32, "row"), (hshape, F32, "acc"), (hshape, F32, "acc")] + [((1, LRU_WIDTH), F32, "acc")] * 3,
        rows=rows, tile=tile)

    n_tiles = rows // min(tile, rows)

    def b_conv(i, dcb, dnext, xb, xprev, cw):
        dnext = dnext * (i < n_tiles - 1).astype(F32)
        xprev = xprev * (i > 0).astype(F32)
        dx = dcb * cw[3:4, :]
        dws = [jnp.sum(dcb * xb, axis=0, keepdims=True)]
        for s in range(1, CONV_WIDTH):
            dx = dx + _rows_after(dcb, dnext, s) * cw[3 - s:4 - s, :]
            dws.append(jnp.sum(dcb * _rows_before(xb, xprev, s), axis=0, keepdims=True))
        return dx, jnp.concatenate(dws[::-1], axis=0), jnp.sum(dcb, axis=0, keepdims=True)

    dlrux, g["conv_w"], g["conv_b"] = _rows(
        "b_conv", b_conv, [(dc, "row"), (dc, "next"), (sv["lrux"], "row"), (sv["lrux"], "prev"), (w["conv_w"], "full")],
        [((rows, LRU_WIDTH), F32, "row"), ((CONV_WIDTH, LRU_WIDTH), F32, "acc"), ((1, LRU_WIDTH), F32, "acc")],
        rows=rows, tile=tile)

    def b_s5post(i, dzs, ysb, gb, wglu, wbs):
        glv, dgelu = _gelu_parts(ysb)
        glu = _nn(glv, wglu)
        ga, gb2 = glu[:, :S5_WIDTH], glu[:, S5_WIDTH:]
        sb = _sigmoid(gb2)
        sg = _sigmoid(gb)
        silu = gb * sg
        y2 = ga * sb * silu
        dy2 = _nt(dzs, wbs)
        dglu = jnp.concatenate([dy2 * sb * silu, dy2 * ga * silu * sb * (1.0 - sb)], axis=1)
        dg = dy2 * ga * sb * sg * (1.0 + gb * (1.0 - sg))
        return _nt(dglu, wglu) * dgelu, dg, _tn(y2, dzs), _tn(glv, dglu)

    dys, ds5g, g["w_bs"], g["w_glu"] = _rows(
        "b_s5post", b_s5post, [(dz_s, "row"), (sv["ys"], "row"), (sv["s5g"], "row"), (w["w_glu"], "full"), (w["w_bs"], "full")],
        [((rows, S5_WIDTH), F32, "row")] * 2 + [((S5_WIDTH, d), F32, "acc"), ((S5_WIDTH, 2 * S5_WIDTH), F32, "acc")],
        rows=rows, tile=tile)

    ds5x, g["bd"], g["cdt"], g["s5_d"], g["abar"] = _s5_bwd(dys, sv["st"], sv["s5x"], w["bd"], w["cdt"], w["s5_d"],
                                                            w["scb"], rows=rows, tile=tile)

    dcomps = [ds5x, ds5g, dlrux, dlrug, dgs, dgl]

    def b_in(i, xb, dx1b, gpre, *rest):
        dcs, ws = rest[:6], rest[6:]
        dh = _nt(dcs[0], ws[0])
        for dcv, wc in zip(dcs[1:], ws[1:]):
            dh = dh + _nt(dcv, wc)
        rstd = lax.rsqrt(jnp.mean(xb * xb, axis=-1, keepdims=True) + NORM_EPS)
        nrm = xb * rstd
        dn = dh * gpre
        dx = rstd * (dn - nrm * jnp.mean(dn * nrm, axis=-1, keepdims=True))
        return dx1b + dx, jnp.sum(dh * nrm, axis=0, keepdims=True)

    dx, g["g_pre"] = _rows(
        "b_in", b_in, [(sv["x"], "row"), (dx1, "row"), (w["g_pre"], "full")] + [(dcv, "row") for dcv in dcomps]
        + [(wc, "full") for wc in w["w_in"]],
        [((rows, d), F32, "row"), ((1, d), F32, "acc")], rows=rows, tile=tile)

    def b_win(i, hb, dcv):
        return (_tn(hb, dcv),)

    g["w_in"] = jnp.concatenate(
        [_rows("b_win", b_win, [(sv["h"], "row"), (dcv, "row")], [((d, dcv.shape[1]), F32, "acc")], rows=rows, tile=4 * tile)[0]
         for dcv in dcomps], axis=1)
    return dx, g


SMALL = ("g_pre", "s5_a_re", "s5_a_im", "s5_log_dt", "s5_b_re", "s5_b_im", "s5_c_re", "s5_c_im", "s5_d", "conv_b",
         "lru_w_a", "lru_b_a", "lru_w_x", "lru_b_x", "lru_lambda", "g_post")
BIG = ("w_in", "w_glu", "w_bs", "conv_w", "w_bl", "w_out", "w_ple", "w_ple_gate")
BIG_SHARD_AXIS = {"w_in": 1, "w_glu": 1, "w_bs": 1, "conv_w": 1, "w_bl": 0, "w_out": 0, "w_ple": 1, "w_ple_gate": 0}


def _bcast_groups(v):
    return jnp.broadcast_to(v[:, None, :], (S5_GROUPS, S5_GROUP, S5_STATE)).reshape(S5_WIDTH, S5_STATE)


def _s5_prep_inputs(wl):
    ldt = jnp.broadcast_to(wl["s5_log_dt"][:, None], (S5_GROUPS, S5_STATE))
    gcn = lambda b: jnp.transpose(b, (0, 2, 1)).reshape(S5_WIDTH, S5_STATE)
    return (_bcast_groups(wl["s5_a_re"]), _bcast_groups(wl["s5_a_im"]), _bcast_groups(ldt), gcn(wl["s5_b_re"]),
            gcn(wl["s5_b_im"]))


def _layer_weights(wl):
    w = {}
    offs = [0]
    for wd in IN_WIDTHS:
        offs.append(offs[-1] + wd)
    w["w_in"] = [wl["w_in"][:, offs[k]:offs[k + 1]] for k in range(6)]
    for k in ("w_glu", "w_bs", "w_bl", "w_out", "w_ple", "w_ple_gate"):
        w[k] = wl[k]
    w["conv_w"] = wl["conv_w"]
    for k in ("g_pre", "g_post", "s5_d", "conv_b", "lru_b_a", "lru_b_x", "lru_lambda"):
        w[k] = wl[k].reshape(1, -1)
    w["lru_w_a"] = wl["lru_w_a"].astype(BF16)
    w["lru_w_x"] = wl["lru_w_x"].astype(BF16)
    prep_in = _s5_prep_inputs(wl)
    abr, abi, bbr, bbi = _s5_prep(*prep_in)
    w["prep_in"] = prep_in
    shape3 = (S5_GROUPS, S5_GROUP, S5_STATE)
    w["bd"] = _s5_block_diag([bbr.reshape(shape3), bbi.reshape(shape3)]).astype(BF16)
    w["cdt"] = _s5_block_diag([wl["s5_c_re"], -wl["s5_c_im"]]).astype(BF16)
    abr_s = abr.reshape(shape3)[:, 0, :]
    abi_s = abi.reshape(shape3)[:, 0, :]
    w["scf"], w["scb"] = _s5_consts(_cplx_to_lanes(abr_s), _cplx_to_lanes(abi_s))
    return w


def _layer_param_grads(g, w, wl):
    out = {}
    shape3 = (S5_GROUPS, S5_GROUP, S5_STATE)
    dbr, dbi = _s5_block_diag_extract(g["bd"])
    dcr, dci = _s5_block_diag_extract(g["cdt"])
    out["s5_c_re"], out["s5_c_im"] = dcr, -dci
    zeros = jnp.zeros(shape3, F32)
    dar = zeros.at[:, 0, :].set(g["abar"][0].reshape(S5_GROUPS, S5_STATE)).reshape(S5_WIDTH, S5_STATE)
    dai = zeros.at[:, 0, :].set(g["abar"][1].reshape(S5_GROUPS, S5_STATE)).reshape(S5_WIDTH, S5_STATE)
    cts = (dar, dai, dbr.reshape(S5_WIDTH, S5_STATE), dbi.reshape(S5_WIDTH, S5_STATE))
    d_are, d_aim, d_ldt, d_bre, d_bim = _s5_prep_bwd(*w["prep_in"], cts)
    out["s5_a_re"] = d_are.reshape(shape3).sum(axis=1)
    out["s5_a_im"] = d_aim.reshape(shape3).sum(axis=1)
    out["s5_log_dt"] = d_ldt.reshape(shape3).sum(axis=(1, 2))
    out["s5_b_re"] = jnp.transpose(d_bre.reshape(shape3), (0, 2, 1))
    out["s5_b_im"] = jnp.transpose(d_bim.reshape(shape3), (0, 2, 1))
    out["s5_d"] = g["s5_d"].reshape(-1)
    for k in ("g_pre", "g_post", "conv_b", "lru_b_a", "lru_b_x", "lru_lambda"):
        out[k] = g[k].reshape(-1)
    for k in ("lru_w_a", "lru_w_x", "conv_w", "w_in", "w_glu", "w_bs", "w_bl", "w_out", "w_ple", "w_ple_gate"):
        out[k] = g[k]
    return out


def _local_step(x, p, layers, target):
    rows = x.shape[0]
    ws = [_layer_weights(wl) for wl in layers]
    saved = []
    for i in range(DEPTH):
        x, sv = _layer_fwd(x, p[i], ws[i], rows)
        saved.append(sv)

    def f_loss(i, yb, tb):
        e = yb - tb
        return e * (1.0 / D_MODEL), jnp.sum(jnp.sum(e * e, axis=0, keepdims=True), axis=1, keepdims=True)

    dx, sq = _rows("f_loss", f_loss, [(x, "row"), (target, "row")],
                   [((rows, D_MODEL), F32, "row"), ((1, 1), F32, "acc")], rows=rows, tile=ROW_TILE)
    loss = sq[0, 0] * (0.5 / D_MODEL)
    grads = [None] * DEPTH
    for i in reversed(range(DEPTH)):
        dx, g = _layer_bwd(dx, saved[i], ws[i], rows)
        grads[i] = _layer_param_grads(g, ws[i], layers[i])
    return loss, dx, grads


def _place():
    return lax.axis_index("x"), lax.axis_index("y"), lax.axis_index("c")


def _other_chips(x, y):
    return [(1 - x, y), (x, 1 - y), (1 - x, 1 - y)]


def _any_spec():
    return pl.BlockSpec(memory_space=pl.ANY)


ICI_PIECES = 4
D2D_PIECES = 4
D2D_SOLO_PIECES = 16


def _pieces(rows, k):
    step = rows // k
    assert step * k == rows and step % 16 == 0, (rows, k)
    return [(q * step, step) for q in range(k)]


def _gather_chips(name, v, via_sibling):
    rows = v.shape[0]
    half = rows // 2

    n_sent = half if via_sibling else rows

    def body(v_ref, out_ref, send_sems, recv_sems, local_sem):
        x, y, c = _place()
        me = 2 * x + y
        chips = _other_chips(x, y)
        slots = [2 * cx + cy for cx, cy in chips]
        mine = pltpu.make_async_copy(v_ref, out_ref.at[me], local_sem)
        mine.start()

        def part(slot, hc, o=0, s=n_sent):
            return out_ref.at[slot, pl.ds(hc * half + o, s), :] if via_sibling else out_ref.at[slot, pl.ds(o, s), :]

        def own(o=0, s=n_sent):
            return v_ref.at[pl.ds(c * half + o, s), :] if via_sibling else v_ref.at[pl.ds(o, s), :]

        def copy(k, src, dst, to):
            return pltpu.make_async_remote_copy(src_ref=src, dst_ref=dst, send_sem=send_sems.at[k], recv_sem=recv_sems.at[k],
                                                device_id=to, device_id_type=MESH)

        for k in range(3):
            for o, s in _pieces(n_sent, ICI_PIECES):
                copy(k, own(o, s), part(me, c, o, s), (*chips[k], c)).start()
        for k in range(3):
            copy(k, own(), part(slots[k], c), (*chips[k], c)).wait_recv()
            if via_sibling:
                for o, s in _pieces(n_sent, D2D_PIECES):
                    copy(3 + k, part(slots[k], c, o, s), part(slots[k], c, o, s), (x, y, 1 - c)).start()
        if via_sibling:
            for k in range(3):
                copy(3 + k, own(), part(slots[k], 1 - c), (x, y, 1 - c)).wait_recv()
        for k in range(6 if via_sibling else 3):
            copy(k, own(), part(me, c), (x, y, 1 - c)).wait_send()
        mine.wait()

    n_sem = 6 if via_sibling else 3
    return pl.pallas_call(
        body, name=name, out_shape=jax.ShapeDtypeStruct((4,) + v.shape, v.dtype),
        in_specs=[_any_spec()], out_specs=_any_spec(),
        scratch_shapes=[pltpu.SemaphoreType.DMA((n_sem,)), pltpu.SemaphoreType.DMA((n_sem,)), pltpu.SemaphoreType.DMA],
    )(v)


def _rs_sibling(gr):
    half = gr.shape[1] // 2

    def body(g_ref, mine_ref, got_ref, send_sem, recv_sem, local_sem):
        x, y, c = _place()
        keep = pltpu.make_async_copy(g_ref.at[:, pl.ds(c * half, half), :], mine_ref, local_sem)
        keep.start()

        def give(src, dst):
            return pltpu.make_async_remote_copy(src_ref=src, dst_ref=dst, send_sem=send_sem, recv_sem=recv_sem,
                                                device_id=(x, y, 1 - c), device_id_type=MESH)

        for j in range(4):
            for o, s in _pieces(half, D2D_PIECES):
                give(g_ref.at[j, pl.ds((1 - c) * half + o, s), :], got_ref.at[j, pl.ds(o, s), :]).start()
        give(g_ref.at[:, pl.ds((1 - c) * half, half), :], got_ref).wait()
        keep.wait()

    sd = jax.ShapeDtypeStruct((4, half, LANES), F32)
    return pl.pallas_call(
        body, name="rs_sibling", out_shape=[sd, sd], in_specs=[_any_spec()], out_specs=[_any_spec(), _any_spec()],
        scratch_shapes=[pltpu.SemaphoreType.DMA, pltpu.SemaphoreType.DMA, pltpu.SemaphoreType.DMA],
    )(gr)


def _rs_chips(a16, a32):
    half = a16.shape[1]

    def body(a16_ref, a32_ref, got_ref, own_ref, send_sems, recv_sems, local_sem):
        x, y, c = _place()
        chips = _other_chips(x, y)
        keep = pltpu.make_async_copy(a32_ref.at[2 * x + y], own_ref, local_sem)
        keep.start()
        def copy(k, o=0, s=half):
            cx, cy = chips[k]
            return pltpu.make_async_remote_copy(
                src_ref=a16_ref.at[2 * cx + cy, pl.ds(o, s), :], dst_ref=got_ref.at[k, pl.ds(o, s), :],
                send_sem=send_sems.at[k], recv_sem=recv_sems.at[k], device_id=(cx, cy, c), device_id_type=MESH)

        for k in range(3):
            for o, s in _pieces(half, ICI_PIECES):
                copy(k, o, s).start()
        for k in range(3):
            copy(k).wait()
        keep.wait()

    return pl.pallas_call(
        body, name="rs_chips",
        out_shape=[jax.ShapeDtypeStruct((3, half, LANES), BF16), jax.ShapeDtypeStruct((half, LANES), F32)],
        in_specs=[_any_spec(), _any_spec()], out_specs=[_any_spec(), _any_spec()],
        scratch_shapes=[pltpu.SemaphoreType.DMA((3,)), pltpu.SemaphoreType.DMA((3,)), pltpu.SemaphoreType.DMA],
    )(a16, a32)


def _swap_halves(v):
    def body(v_ref, out_ref, send_sem, recv_sem, local_sem):
        x, y, c = _place()
        keep = pltpu.make_async_copy(v_ref, out_ref.at[c], local_sem)
        keep.start()
        def give(hc, o=0, s=v.shape[0]):
            return pltpu.make_async_remote_copy(src_ref=v_ref.at[pl.ds(o, s), :], dst_ref=out_ref.at[hc, pl.ds(o, s), :],
                                                send_sem=send_sem, recv_sem=recv_sem, device_id=(x, y, 1 - c),
                                                device_id_type=MESH)

        for o, s in _pieces(v.shape[0], D2D_SOLO_PIECES):
            give(c, o, s).start()
        give(c).wait_send()
        give(1 - c).wait_recv()
        keep.wait()

    return pl.pallas_call(
        body, name="swap_halves", out_shape=jax.ShapeDtypeStruct((2,) + v.shape, v.dtype),
        in_specs=[_any_spec()], out_specs=_any_spec(),
        scratch_shapes=[pltpu.SemaphoreType.DMA, pltpu.SemaphoreType.DMA, pltpu.SemaphoreType.DMA],
    )(v)


WIDE = 1024
PACK_TILE = 3072
GRAD_ROWS_UNIT = 2 * PACK_TILE


def _pack(parts, rows_unit, dtype):
    flat = jnp.concatenate([q.reshape(-1).astype(dtype) for q in parts])
    unit = rows_unit * LANES
    total = -(-flat.shape[0] // unit) * unit
    return jnp.pad(flat, (0, total - flat.shape[0])).reshape(-1, LANES)


def _unpack(flat, shapes):
    out, off = [], 0
    for s in shapes:
        n = 1
        for q in s:
            n *= q
        out.append(flat[off:off + n].reshape(s))
        off += n
    return out


def _to_slots(name, full):
    dp, r, c = full.shape
    if BIG_SHARD_AXIS[name] == 1:
        return jnp.transpose(full.reshape(dp, r, 4, c // 4), (2, 0, 1, 3)).reshape(4, -1)
    return jnp.transpose(full.reshape(dp, 4, r // 4, c), (1, 0, 2, 3)).reshape(4, -1)


def _from_slots(name, slots, shard_shape):
    dp, r, c = shard_shape
    v = slots.reshape(4, dp, r, c)
    if BIG_SHARD_AXIS[name] == 1:
        return jnp.transpose(v, (1, 2, 0, 3)).reshape(dp, r, 4 * c)
    return jnp.transpose(v, (1, 0, 2, 3)).reshape(dp, 4 * r, c)


def _adamw(name, w, g, m, v, tile):
    def fn(i, wb, gb, mb, vb):
        m2 = ADAM_B1 * mb + (1.0 - ADAM_B1) * gb
        v2 = ADAM_B2 * vb + (1.0 - ADAM_B2) * (gb * gb)
        m_hat = m2 / (1.0 - ADAM_B1 ** ADAM_STEP)
        v_hat = v2 / (1.0 - ADAM_B2 ** ADAM_STEP)
        return -ADAM_LR * (m_hat / (jnp.sqrt(v_hat) + ADAM_EPS) + ADAM_WD * wb), m2, v2

    return _rows(name, fn, [(w, "row"), (g, "row"), (m, "row"), (v, "row")], [(w.shape, F32, "row")] * 3,
                 rows=w.shape[0], tile=tile)


def _as_2d(a):
    return a.reshape(-1, a.shape[-1])


def _adam_tile(rows):
    for t in (256, 184, 128, 64, 32, 16, 8):
        if rows % t == 0:
            return t
    return rows


def kernel(x, p, g_pre, w_in, s5_a_re, s5_a_im, s5_log_dt, s5_b_re, s5_b_im, s5_c_re, s5_c_im, s5_d, w_glu, w_bs, conv_w, conv_b, lru_w_a, lru_b_a, lru_w_x, lru_b_x, lru_lambda, w_bl, w_out, g_post, w_ple, w_ple_gate, loss_target, m_g_pre, m_w_in, m_s5_a_re, m_s5_a_im, m_s5_log_dt, m_s5_b_re, m_s5_b_im, m_s5_c_re, m_s5_c_im, m_s5_d, m_w_glu, m_w_bs, m_conv_w, m_conv_b, m_lru_w_a, m_lru_b_a, m_lru_w_x, m_lru_b_x, m_lru_lambda, m_w_bl, m_w_out, m_g_post, m_w_ple, m_w_ple_gate, v_g_pre, v_w_in, v_s5_a_re, v_s5_a_im, v_s5_log_dt, v_s5_b_re, v_s5_b_im, v_s5_c_re, v_s5_c_im, v_s5_d, v_w_glu, v_w_bs, v_conv_w, v_conv_b, v_lru_w_a, v_lru_b_a, v_lru_w_x, v_lru_b_x, v_lru_lambda, v_w_bl, v_w_out, v_g_post, v_w_ple, v_w_ple_gate):
    wts = dict(g_pre=g_pre, w_in=w_in, s5_a_re=s5_a_re, s5_a_im=s5_a_im, s5_log_dt=s5_log_dt, s5_b_re=s5_b_re,
               s5_b_im=s5_b_im, s5_c_re=s5_c_re, s5_c_im=s5_c_im, s5_d=s5_d, w_glu=w_glu, w_bs=w_bs, conv_w=conv_w,
               conv_b=conv_b, lru_w_a=lru_w_a, lru_b_a=lru_b_a, lru_w_x=lru_w_x, lru_b_x=lru_b_x, lru_lambda=lru_lambda,
               w_bl=w_bl, w_out=w_out, g_post=g_post, w_ple=w_ple, w_ple_gate=w_ple_gate)
    mom1 = dict(g_pre=m_g_pre, w_in=m_w_in, s5_a_re=m_s5_a_re, s5_a_im=m_s5_a_im, s5_log_dt=m_s5_log_dt, s5_b_re=m_s5_b_re,
                s5_b_im=m_s5_b_im, s5_c_re=m_s5_c_re, s5_c_im=m_s5_c_im, s5_d=m_s5_d, w_glu=m_w_glu, w_bs=m_w_bs,
                conv_w=m_conv_w, conv_b=m_conv_b, lru_w_a=m_lru_w_a, lru_b_a=m_lru_b_a, lru_w_x=m_lru_w_x, lru_b_x=m_lru_b_x,
                lru_lambda=m_lru_lambda, w_bl=m_w_bl, w_out=m_w_out, g_post=m_g_post, w_ple=m_w_ple, w_ple_gate=m_w_ple_gate)
    mom2 = dict(g_pre=v_g_pre, w_in=v_w_in, s5_a_re=v_s5_a_re, s5_a_im=v_s5_a_im, s5_log_dt=v_s5_log_dt, s5_b_re=v_s5_b_re,
                s5_b_im=v_s5_b_im, s5_c_re=v_s5_c_re, s5_c_im=v_s5_c_im, s5_d=v_s5_d, w_glu=v_w_glu, w_bs=v_w_bs,
                conv_w=v_conv_w, conv_b=v_conv_b, lru_w_a=v_lru_w_a, lru_b_a=v_lru_b_a, lru_w_x=v_lru_w_x, lru_b_x=v_lru_b_x,
                lru_lambda=v_lru_lambda, w_bl=v_w_bl, w_out=v_w_out, g_post=v_g_post, w_ple=v_w_ple, w_ple_gate=v_w_ple_gate)
    names = list(wts)

    def wire(name):
        return lax.bitcast_convert_type(wts[name], BF16) if name == "conv_w" else wts[name].astype(BF16)

    wire_shapes = [wire(k).shape for k in BIG]
    gathered = _gather_chips("gather_weights", _pack([wire(k) for k in BIG], 128, BF16), via_sibling=True)
    per_chip = [_unpack(gathered[j].reshape(-1), wire_shapes) for j in range(4)]
    whole = {}
    for idx, k in enumerate(BIG):
        v = jnp.stack([per_chip[j][idx] for j in range(4)])
        if k == "conv_w":
            v = lax.bitcast_convert_type(v, F32)
        whole[k] = _from_slots(k, v.reshape(4, -1), wts[k].shape)
    layers = []
    for i in range(DEPTH):
        wl = {k: whole[k][i] for k in BIG}
        wl.update({k: wts[k][i] for k in SMALL})
        layers.append(wl)

    loss, grad_x, grads = _local_step(x[0], p[:, 0], layers, loss_target[0])
    loss = lax.psum(loss, ("x", "y", "c"))

    big_slots = jnp.concatenate([_to_slots(k, jnp.stack([grads[i][k] for i in range(DEPTH)])) for k in BIG], axis=1)
    small_shapes = [wts[k].shape for k in SMALL]
    small_flat = jnp.concatenate([jnp.stack([grads[i][k] for i in range(DEPTH)]).reshape(-1) for k in SMALL])
    n_small = small_flat.shape[0]
    small_q = -(-n_small // (4 * 8 * WIDE)) * 8 * WIDE
    small_slots = jnp.pad(small_flat, (0, 4 * small_q - n_small)).reshape(4, small_q)
    n_big = big_slots.shape[1]
    n_big_pad = -(-n_big // (8 * WIDE)) * 8 * WIDE
    n_slot = n_big_pad + small_q
    unit = GRAD_ROWS_UNIT * LANES
    n_slot_pad = -(-n_slot // unit) * unit
    gr = jnp.concatenate([jnp.pad(big_slots, ((0, 0), (0, n_big_pad - n_big))), small_slots,
                          jnp.zeros((4, n_slot_pad - n_slot), F32)], axis=1).reshape(4, -1, LANES)
    mine, got = _rs_sibling(gr)
    half = mine.shape[1]
    rows2d = lambda a: a.reshape(-1, LANES)

    def f_add1(i, a, b):
        s = a + b
        return s, s

    a32, a16 = _rows("rs_add1", f_add1, [(rows2d(mine), "row"), (rows2d(got), "row")],
                     [((4 * half, LANES), F32, "row"), ((4 * half, LANES), BF16, "row")], rows=4 * half, tile=PACK_TILE)
    got3, own = _rs_chips(a16.reshape(4, half, LANES), a32.reshape(4, half, LANES))

    def f_add2(i, o, g0, g1, g2):
        return (((o + g0.astype(F32)) + g1.astype(F32)) + g2.astype(F32),)

    (red_half,) = _rows("rs_add2", f_add2, [(own, "row")] + [(got3[k], "row") for k in range(3)],
                        [((half, LANES), F32, "row")], rows=half, tile=PACK_TILE)
    red = _swap_halves(red_half).reshape(-1)
    small_red = _gather_chips("gather_small", red[n_big_pad:n_big_pad + small_q].reshape(-1, LANES), via_sibling=False)
    small_red = small_red.reshape(-1)[:n_small]

    big_shapes = [wts[k].shape for k in BIG]
    grad_out = dict(zip(BIG, _unpack(red[:n_big], big_shapes)))
    grad_out.update(zip(SMALL, _unpack(small_red, small_shapes)))
    delta, new_m, new_v = {}, {}, {}
    for k in BIG:
        w2 = _as_2d(wts[k])
        res = _adamw("adamw_" + k, w2, _as_2d(grad_out[k]), _as_2d(mom1[k]), _as_2d(mom2[k]), _adam_tile(w2.shape[0]))
        delta[k], new_m[k], new_v[k] = [r.reshape(wts[k].shape) for r in res]
    pack_small = lambda d: jnp.pad(jnp.concatenate([d[k].reshape(-1) for k in SMALL]), (0, 4 * small_q - n_small)).reshape(-1, WIDE)
    res = _adamw("adamw_small", pack_small(wts), pack_small(grad_out), pack_small(mom1), pack_small(mom2),
                 _adam_tile(4 * small_q // WIDE))
    for d, r in zip((delta, new_m, new_v), res):
        d.update(zip(SMALL, _unpack(r.reshape(-1), small_shapes)))
    return (loss, grad_x[None], *[grad_out[k] for k in names], *[delta[k] for k in names],
            *[new_m[k] for k in names], *[new_v[k] for k in names])
```

```python
import jax
import jax.numpy as jnp
from jax import lax
from jax.experimental import pallas as pl
from jax.experimental.pallas import tpu as pltpu

F32 = jnp.float32
BF16 = jnp.bfloat16
MESH = pl.DeviceIdType.MESH

DEPTH = 2
D_MODEL = 1024
NORM_EPS = 1e-6
S5_WIDTH = 512
S5_GROUPS = 32
S5_GROUP = 16
S5_STATE = 64
LRU_WIDTH = 1280
LRU_HEADS = 10
LRU_HEAD_DIM = 128
LRU_C = 8.0
CONV_WIDTH = 4
PLE_DIM = 256
IN_WIDTHS = (S5_WIDTH, S5_WIDTH, LRU_WIDTH, LRU_WIDTH, D_MODEL, D_MODEL)
ADAM_LR = 0.001
ADAM_B1 = 0.9
ADAM_B2 = 0.999
ADAM_EPS = 1e-08
ADAM_WD = 0.01
ADAM_STEP = 10

SUBLANES = 8
LANES = 128
S5_HALF_IN = S5_WIDTH // 2
S5_CPLX = S5_GROUPS * S5_STATE
S5_HALF_CPLX = S5_CPLX // 2
S5_LANES = 2 * S5_CPLX
VMEM_LIMIT = 48 * 2 ** 20
ROW_TILE = 256


def _sigmoid(x):
    return 1.0 / (1.0 + jnp.exp(-x))


def _gelu_parts(x):
    k = 0.7978845608028654
    t = jnp.tanh(k * (x + 0.044715 * x * x * x))
    val = 0.5 * x * (1.0 + t)
    grad = 0.5 * (1.0 + t) + 0.5 * x * (1.0 - t * t) * k * (1.0 + 3.0 * 0.044715 * x * x)
    return val, grad


def _nn(a, w):
    return jnp.dot(a.astype(BF16), w.astype(BF16), preferred_element_type=F32)


def _nt(a, w):
    return lax.dot_general(a.astype(BF16), w.astype(BF16), (((1,), (1,)), ((), ())), preferred_element_type=F32)


def _tn(a, b):
    return lax.dot_general(a.astype(BF16), b.astype(BF16), (((0,), (0,)), ((), ())), preferred_element_type=F32)


def _heads(op, a, w):
    d = LRU_HEAD_DIM
    return jnp.concatenate([op(a[:, h * d:(h + 1) * d], w[h]) for h in range(LRU_HEADS)], axis=1)


def _heads_tn(a, b):
    d = LRU_HEAD_DIM
    return jnp.stack([_tn(a[:, h * d:(h + 1) * d], b[:, h * d:(h + 1) * d]) for h in range(LRU_HEADS)], axis=0)


def _rows_before(x, halo, s):
    main = pltpu.roll(x, s, 0)
    head = pltpu.roll(jnp.concatenate([halo, x[0:SUBLANES]], axis=0), s, 0)[SUBLANES:2 * SUBLANES]
    return jnp.concatenate([head, main[SUBLANES:]], axis=0)


def _rows_after(x, halo, s):
    n = x.shape[0]
    main = pltpu.roll(x, n - s, 0)
    tail = pltpu.roll(jnp.concatenate([x[n - SUBLANES:], halo], axis=0), 2 * SUBLANES - s, 0)[0:SUBLANES]
    return jnp.concatenate([main[:n - SUBLANES], tail], axis=0)


def _rows(name, fn, ins, outs, *, rows, tile):
    tile = min(tile, rows)
    n = rows // tile
    assert n * tile == rows, (name, rows, tile)
    per8 = tile // SUBLANES
    last8 = rows // SUBLANES - 1
    in_specs = []
    for arr, kind in ins:
        if kind == "row":
            in_specs.append(pl.BlockSpec((tile, arr.shape[1]), lambda i: (i, 0)))
        elif kind == "prev":
            in_specs.append(pl.BlockSpec((SUBLANES, arr.shape[1]), lambda i: (jnp.maximum(i * per8 - 1, 0), 0)))
        elif kind == "next":
            in_specs.append(pl.BlockSpec((SUBLANES, arr.shape[1]), lambda i: (jnp.minimum((i + 1) * per8, last8), 0)))
        else:
            in_specs.append(pl.BlockSpec(arr.shape, lambda i, nd=arr.ndim: (0,) * nd))
    out_shape, out_specs = [], []
    for shape, dtype, kind in outs:
        out_shape.append(jax.ShapeDtypeStruct(shape, dtype))
        if kind == "row":
            out_specs.append(pl.BlockSpec((tile, shape[1]), lambda i: (i, 0)))
        else:
            out_specs.append(pl.BlockSpec(shape, lambda i, nd=len(shape): (0,) * nd))
    n_in = len(ins)

    def body(*refs):
        i = pl.program_id(0)
        vals = fn(i, *[r[...] for r in refs[:n_in]])
        assert len(vals) == len(outs), name
        for r, v, (_, _, kind) in zip(refs[n_in:], vals, outs):
            if kind == "row":
                r[...] = v.astype(r.dtype)
            else:
                @pl.when(i == 0)
                def _():
                    r[...] = jnp.zeros_like(r)

                r[...] += v.astype(r.dtype)

    return pl.pallas_call(
        body, name=name, grid=(n,), in_specs=in_specs, out_specs=out_specs, out_shape=out_shape,
        compiler_params=pltpu.CompilerParams(dimension_semantics=("arbitrary",), vmem_limit_bytes=VMEM_LIMIT),
    )(*[a for a, _ in ins])


def _s5_discretise(are, aim, ldt, bre, bim):
    dt = jnp.exp(ldt)
    er = jnp.exp(are * dt)
    abr = er * jnp.cos(aim * dt)
    abi = er * jnp.sin(aim * dt)
    den = are * are + aim * aim
    zr = ((abr - 1.0) * are + abi * aim) / den
    zi = (abi * are - (abr - 1.0) * aim) / den
    return abr, abi, zr * bre - zi * bim, zr * bim + zi * bre


def _s5_prep(are, aim, ldt, bre, bim):
    def body(a, b, c, d, e, o0, o1, o2, o3):
        r = _s5_discretise(a[...], b[...], c[...], d[...], e[...])
        o0[...], o1[...], o2[...], o3[...] = r

    sd = jax.ShapeDtypeStruct(are.shape, F32)
    return pl.pallas_call(body, name="s5_prep", out_shape=[sd] * 4)(are, aim, ldt, bre, bim)


def _s5_prep_bwd(are, aim, ldt, bre, bim, cts):
    def body(a, b, c, d, e, c0, c1, c2, c3, o0, o1, o2, o3, o4):
        _, vjp = jax.vjp(_s5_discretise, a[...], b[...], c[...], d[...], e[...])
        r = vjp((c0[...], c1[...], c2[...], c3[...]))
        o0[...], o1[...], o2[...], o3[...], o4[...] = r

    sd = jax.ShapeDtypeStruct(are.shape, F32)
    return pl.pallas_call(body, name="s5_prep_bwd", out_shape=[sd] * 5)(are, aim, ldt, bre, bim, *cts)


def _s5_consts(abr, abi):
    shape = (SUBLANES, S5_CPLX)

    def body(ar_ref, ai_ref, f_ref, b_ref):
        ar = jnp.broadcast_to(ar_ref[...], shape)
        ai = jnp.broadcast_to(ai_ref[...], shape)
        row = lax.broadcasted_iota(jnp.int32, shape, 0)

        def cmul(p, q):
            return (p[0] * q[0] - p[1] * q[1], p[0] * q[1] + p[1] * q[0])

        a1 = (ar, ai)
        a2 = cmul(a1, a1)
        a3 = cmul(a2, a1)
        a4 = cmul(a2, a2)
        pw = [a1, a2, a3, a4, cmul(a4, a1), cmul(a4, a2), cmul(a4, a3), cmul(a4, a4)]

        def by_row(vals):
            out = vals[7]
            for r in range(6, -1, -1):
                out = jnp.where(row == r, vals[r], out)
            return out

        fwd, rev = [], []
        for a, k in ((a1, 1), (a2, 2), (a4, 4)):
            fwd += [jnp.where(row >= k, a[0], 0.0), jnp.where(row >= k, a[1], 0.0)]
            rev += [jnp.where(row <= 7 - k, a[0], 0.0), jnp.where(row <= 7 - k, -a[1], 0.0)]
        fwd += [by_row([p[0] for p in pw]), by_row([p[1] for p in pw])]
        rev += [by_row([pw[7 - r][0] for r in range(8)]), by_row([-pw[7 - r][1] for r in range(8)])]
        f_ref[...] = jnp.concatenate(fwd, axis=0)
        b_ref[...] = jnp.concatenate(rev, axis=0)

    sd = jax.ShapeDtypeStruct((8 * SUBLANES, S5_CPLX), F32)
    return pl.pallas_call(body, name="s5_consts", out_shape=[sd, sd])(abr, abi)


def _s5_lane_offsets(q):
    re = (q // 8) * 2 * S5_HALF_CPLX + (q % 8) * LANES
    return re, re + S5_HALF_CPLX


def _s5_scan(s_ref, sc_ref, carry_ref, nblk, reverse):
    group = 2
    edge = 0 if reverse else SUBLANES - 1
    for q0 in range(0, S5_CPLX // LANES, group):
        offs = [_s5_lane_offsets(q) for q in range(q0, q0 + group)]
        consts = [[sc_ref[k * SUBLANES:(k + 1) * SUBLANES, q * LANES:(q + 1) * LANES] for k in range(8)]
                  for q in range(q0, q0 + group)]
        carry0 = tuple(carry_ref[:, o:o + LANES] for pair in offs for o in pair)

        def blk(t, carry, offs=offs, consts=consts):
            b = (nblk - 1 - t) if reverse else t
            r0 = pl.multiple_of(b * SUBLANES, SUBLANES)
            new = []
            for j, ((re, im), (a1r, a1i, a2r, a2i, a4r, a4i, pr, pi)) in enumerate(zip(offs, consts)):
                xr = s_ref[pl.ds(r0, SUBLANES), re:re + LANES]
                xi = s_ref[pl.ds(r0, SUBLANES), im:im + LANES]
                for ar, ai, sh in ((a1r, a1i, 1), (a2r, a2i, 2), (a4r, a4i, 4)):
                    shift = SUBLANES - sh if reverse else sh
                    sr = pltpu.roll(xr, shift, 0)
                    si = pltpu.roll(xi, shift, 0)
                    xr, xi = xr + ar * sr - ai * si, xi + ar * si + ai * sr
                cr, ci = carry[2 * j], carry[2 * j + 1]
                xr, xi = xr + pr * cr - pi * ci, xi + pr * ci + pi * cr
                s_ref[pl.ds(r0, SUBLANES), re:re + LANES] = xr
                s_ref[pl.ds(r0, SUBLANES), im:im + LANES] = xi
                new.append(jnp.broadcast_to(xr[edge:edge + 1, :], (SUBLANES, LANES)))
                new.append(jnp.broadcast_to(xi[edge:edge + 1, :], (SUBLANES, LANES)))
            return tuple(new)

        carry = lax.fori_loop(0, nblk, blk, carry0, unroll=2)
        for k, o in enumerate(o for pair in offs for o in pair):
            carry_ref[:, o:o + LANES] = carry[k]


def _s5_fwd(u, bd, cdt, dskip, sc, *, rows, tile):
    n = rows // tile
    nblk = tile // SUBLANES
    hc = 2 * S5_HALF_CPLX

    def body(u_ref, bd_ref, cdt_ref, d_ref, sc_ref, y_ref, s_ref, carry_ref):
        @pl.when(pl.program_id(0) == 0)
        def _():
            carry_ref[...] = jnp.zeros_like(carry_ref)

        ub = u_ref[...].astype(BF16)
        for h in range(2):
            s_ref[:, h * hc:(h + 1) * hc] = jnp.dot(ub[:, h * S5_HALF_IN:(h + 1) * S5_HALF_IN], bd_ref[h],
                                                    preferred_element_type=F32)
        _s5_scan(s_ref, sc_ref, carry_ref, nblk, reverse=False)
        ys = [_nt(s_ref[:, h * hc:(h + 1) * hc], cdt_ref[h]) for h in range(2)]
        y_ref[...] = jnp.concatenate(ys, axis=1) + d_ref[...] * u_ref[...]

    full = lambda a: pl.BlockSpec(a.shape, lambda i, nd=a.ndim: (0,) * nd)
    return pl.pallas_call(
        body, name="s5_fwd", grid=(n,),
        in_specs=[pl.BlockSpec((tile, S5_WIDTH), lambda i: (i, 0)), full(bd), full(cdt), full(dskip), full(sc)],
        out_specs=[pl.BlockSpec((tile, S5_WIDTH), lambda i: (i, 0)), pl.BlockSpec((tile, S5_LANES), lambda i: (i, 0))],
        out_shape=[jax.ShapeDtypeStruct((rows, S5_WIDTH), F32), jax.ShapeDtypeStruct((rows, S5_LANES), F32)],
        scratch_shapes=[pltpu.VMEM((SUBLANES, S5_LANES), F32)],
        compiler_params=pltpu.CompilerParams(dimension_semantics=("arbitrary",), vmem_limit_bytes=VMEM_LIMIT),
    )(u, bd, cdt, dskip, sc)


def _s5_bwd(dy, s, u, bd, cdt, dskip, sc, *, rows, tile):
    n = rows // tile
    nblk = tile // SUBLANES
    hc = 2 * S5_HALF_CPLX
    per8 = tile // SUBLANES

    def body(dy_ref, s_ref, sp_ref, u_ref, bd_ref, cdt_ref, d_ref, sc_ref,
             du_ref, dbd_ref, dcdt_ref, dd_ref, da_ref, g_ref, carry_ref):
        i = pl.program_id(0)

        @pl.when(i == 0)
        def _():
            carry_ref[...] = jnp.zeros_like(carry_ref)
            dbd_ref[...] = jnp.zeros_like(dbd_ref)
            dcdt_ref[...] = jnp.zeros_like(dcdt_ref)
            dd_ref[...] = jnp.zeros_like(dd_ref)
            da_ref[...] = jnp.zeros_like(da_ref)

        dy = dy_ref[...]
        dyb = dy.astype(BF16)
        u = u_ref[...]
        ub = u.astype(BF16)
        for h in range(2):
            g_ref[:, h * hc:(h + 1) * hc] = jnp.dot(dyb[:, h * S5_HALF_IN:(h + 1) * S5_HALF_IN], cdt_ref[h],
                                                    preferred_element_type=F32)
        _s5_scan(g_ref, sc_ref, carry_ref, nblk, reverse=True)
        dus = []
        for h in range(2):
            gb = g_ref[:, h * hc:(h + 1) * hc].astype(BF16)
            sb = s_ref[:, h * hc:(h + 1) * hc].astype(BF16)
            dus.append(_nt(gb, bd_ref[h]))
            dbd_ref[h] += _tn(ub[:, h * S5_HALF_IN:(h + 1) * S5_HALF_IN], gb)
            dcdt_ref[h] += _tn(dyb[:, h * S5_HALF_IN:(h + 1) * S5_HALF_IN], sb)
        du_ref[...] = jnp.concatenate(dus, axis=1) + d_ref[...] * dy
        dd_ref[...] += jnp.sum(dy * u, axis=0, keepdims=True)

        not_first = (i < n - 1).astype(F32)
        row = lax.broadcasted_iota(jnp.int32, (SUBLANES, LANES), 0)
        for q in range(S5_CPLX // LANES):
            re, im = _s5_lane_offsets(q)
            pr0 = jnp.broadcast_to(sp_ref[SUBLANES - 1:SUBLANES, re:re + LANES], (SUBLANES, LANES)) * not_first
            pi0 = jnp.broadcast_to(sp_ref[SUBLANES - 1:SUBLANES, im:im + LANES], (SUBLANES, LANES)) * not_first
            zero = jnp.zeros((SUBLANES, LANES), F32)

            def blk(b, c, re=re, im=im):
                accr, acci, pr, pi = c
                r0 = pl.multiple_of(b * SUBLANES, SUBLANES)
                sr = s_ref[pl.ds(r0, SUBLANES), re:re + LANES]
                si = s_ref[pl.ds(r0, SUBLANES), im:im + LANES]
                gr = g_ref[pl.ds(r0, SUBLANES), re:re + LANES]
                gi = g_ref[pl.ds(r0, SUBLANES), im:im + LANES]
                ssr = jnp.where(row == 0, pr, pltpu.roll(sr, 1, 0))
                ssi = jnp.where(row == 0, pi, pltpu.roll(si, 1, 0))
                accr = accr + ssr * gr + ssi * gi
                acci = acci + ssr * gi - ssi * gr
                return (accr, acci, jnp.broadcast_to(sr[SUBLANES - 1:SUBLANES, :], (SUBLANES, LANES)),
                        jnp.broadcast_to(si[SUBLANES - 1:SUBLANES, :], (SUBLANES, LANES)))

            accr, acci, _, _ = lax.fori_loop(0, nblk, blk, (zero, zero, pr0, pi0), unroll=2)
            da_ref[0:1, q * LANES:(q + 1) * LANES] += jnp.sum(accr, axis=0, keepdims=True)
            da_ref[1:2, q * LANES:(q + 1) * LANES] += jnp.sum(acci, axis=0, keepdims=True)

    full = lambda a: pl.BlockSpec(a.shape, lambda i, nd=a.ndim: (0,) * nd)
    rev = lambda i: (n - 1 - i, 0)
    wshape = (2, S5_HALF_IN, hc)
    return pl.pallas_call(
        body, name="s5_bwd", grid=(n,),
        in_specs=[pl.BlockSpec((tile, S5_WIDTH), rev), pl.BlockSpec((tile, S5_LANES), rev),
                  pl.BlockSpec((SUBLANES, S5_LANES), lambda i: (jnp.maximum((n - 1 - i) * per8 - 1, 0), 0)),
                  pl.BlockSpec((tile, S5_WIDTH), rev), full(bd), full(cdt), full(dskip), full(sc)],
        out_specs=[pl.BlockSpec((tile, S5_WIDTH), rev),
                   pl.BlockSpec(wshape, lambda i: (0, 0, 0)), pl.BlockSpec(wshape, lambda i: (0, 0, 0)),
                   pl.BlockSpec((1, S5_WIDTH), lambda i: (0, 0)), pl.BlockSpec((SUBLANES, S5_CPLX), lambda i: (0, 0))],
        out_shape=[jax.ShapeDtypeStruct((rows, S5_WIDTH), F32), jax.ShapeDtypeStruct(wshape, F32),
                   jax.ShapeDtypeStruct(wshape, F32), jax.ShapeDtypeStruct((1, S5_WIDTH), F32),
                   jax.ShapeDtypeStruct((SUBLANES, S5_CPLX), F32)],
        scratch_shapes=[pltpu.VMEM((tile, S5_LANES), F32), pltpu.VMEM((SUBLANES, S5_LANES), F32)],
        compiler_params=pltpu.CompilerParams(dimension_semantics=("arbitrary",), vmem_limit_bytes=VMEM_LIMIT),
    )(dy, s, s, u, bd, cdt, dskip, sc)


def _s5_block_diag(parts):
    v = jnp.stack(parts, axis=2).reshape(2, 16, S5_GROUP, 2, S5_STATE)
    eye = jnp.eye(16, dtype=v.dtype)
    return jnp.einsum("hgcpn,gk->hgcpkn", v, eye).reshape(2, S5_HALF_IN, 2 * S5_HALF_CPLX)


def _s5_block_diag_extract(m):
    v = m.reshape(2, 16, S5_GROUP, 2, 16, S5_STATE)
    d = jnp.diagonal(v, axis1=1, axis2=4)
    d = jnp.transpose(d, (2, 0, 4, 1, 3)).reshape(2, S5_GROUPS, S5_GROUP, S5_STATE)
    return d[0], d[1]


def _cplx_to_lanes(v):
    return v.reshape(1, S5_CPLX)


def _lru_scan_fwd(a, b, *, rows, tile):
    n = rows // tile
    nblk = tile // SUBLANES
    group = 2

    def body(a_ref, b_ref, h_ref, carry_ref):
        @pl.when(pl.program_id(0) == 0)
        def _():
            carry_ref[...] = jnp.zeros_like(carry_ref)

        row = lax.broadcasted_iota(jnp.int32, (SUBLANES, LANES), 0)
        for q0 in range(0, LRU_WIDTH // LANES, group):
            offs = [q * LANES for q in range(q0, q0 + group)]

            def blk(t, carry, offs=offs):
                r0 = pl.multiple_of(t * SUBLANES, SUBLANES)
                new = []
                for j, o in enumerate(offs):
                    av = a_ref[pl.ds(r0, SUBLANES), o:o + LANES]
                    xv = b_ref[pl.ds(r0, SUBLANES), o:o + LANES]
                    for sh in (1, 2, 4):
                        m = row >= sh
                        xs = pltpu.roll(xv, sh, 0)
                        asft = pltpu.roll(av, sh, 0)
                        xv = xv + jnp.where(m, av * xs, 0.0)
                        av = jnp.where(m, av * asft, av)
                    hv = xv + av * carry[j]
                    h_ref[pl.ds(r0, SUBLANES), o:o + LANES] = hv
                    new.append(jnp.broadcast_to(hv[SUBLANES - 1:SUBLANES, :], (SUBLANES, LANES)))
                return tuple(new)

            carry = lax.fori_loop(0, nblk, blk, tuple(carry_ref[:, o:o + LANES] for o in offs), unroll=2)
            for j, o in enumerate(offs):
                carry_ref[:, o:o + LANES] = carry[j]

    spec = pl.BlockSpec((tile, LRU_WIDTH), lambda i: (i, 0))
    return pl.pallas_call(
        body, name="lru_scan_fwd", grid=(n,), in_specs=[spec, spec], out_specs=spec,
        out_shape=jax.ShapeDtypeStruct((rows, LRU_WIDTH), F32),
        scratch_shapes=[pltpu.VMEM((SUBLANES, LRU_WIDTH), F32)],
        compiler_params=pltpu.CompilerParams(dimension_semantics=("arbitrary",), vmem_limit_bytes=VMEM_LIMIT),
    )(a, b)


def _lru_scan_bwd(dh, a, *, rows, tile):
    n = rows // tile
    nblk = tile // SUBLANES
    group = 2

    def body(dh_ref, a_ref, g_ref, cg_ref, ca_ref):
        @pl.when(pl.program_id(0) == 0)
        def _():
            cg_ref[...] = jnp.zeros_like(cg_ref)
            ca_ref[...] = jnp.zeros_like(ca_ref)

        row = lax.broadcasted_iota(jnp.int32, (SUBLANES, LANES), 0)
        for q0 in range(0, LRU_WIDTH // LANES, group):
            offs = [q * LANES for q in range(q0, q0 + group)]

            def blk(t, carry, offs=offs):
                r0 = pl.multiple_of((nblk - 1 - t) * SUBLANES, SUBLANES)
                new = []
                for j, o in enumerate(offs):
                    cg, ca = carry[2 * j], carry[2 * j + 1]
                    araw = a_ref[pl.ds(r0, SUBLANES), o:o + LANES]
                    xv = dh_ref[pl.ds(r0, SUBLANES), o:o + LANES]
                    av = jnp.where(row == SUBLANES - 1, ca, pltpu.roll(araw, SUBLANES - 1, 0))
                    for sh in (1, 2, 4):
                        m = row <= SUBLANES - 1 - sh
                        xs = pltpu.roll(xv, SUBLANES - sh, 0)
                        asft = pltpu.roll(av, SUBLANES - sh, 0)
                        xv = xv + jnp.where(m, av * xs, 0.0)
                        av = jnp.where(m, av * asft, av)
                    gv = xv + av * cg
                    g_ref[pl.ds(r0, SUBLANES), o:o + LANES] = gv
                    new.append(jnp.broadcast_to(gv[0:1, :], (SUBLANES, LANES)))
                    new.append(jnp.broadcast_to(araw[0:1, :], (SUBLANES, LANES)))
                return tuple(new)

            carry0 = tuple(r[:, o:o + LANES] for o in offs for r in (cg_ref, ca_ref))
            carry = lax.fori_loop(0, nblk, blk, carry0, unroll=2)
            for j, o in enumerate(offs):
                cg_ref[:, o:o + LANES] = carry[2 * j]
                ca_ref[:, o:o + LANES] = carry[2 * j + 1]

    spec = pl.BlockSpec((tile, LRU_WIDTH), lambda i: (n - 1 - i, 0))
    return pl.pallas_call(
        body, name="lru_scan_bwd", grid=(n,), in_specs=[spec, spec], out_specs=spec,
        out_shape=jax.ShapeDtypeStruct((rows, LRU_WIDTH), F32),
        scratch_shapes=[pltpu.VMEM((SUBLANES, LRU_WIDTH), F32), pltpu.VMEM((SUBLANES, LRU_WIDTH), F32)],
        compiler_params=pltpu.CompilerParams(dimension_semantics=("arbitrary",), vmem_limit_bytes=VMEM_LIMIT),
    )(dh, a)


def _conv_fwd(i, x, prev, cw, cb):
    prev = prev * (i > 0).astype(F32)
    y = x * cw[3:4, :] + cb
    for s in range(1, CONV_WIDTH):
        y = y + _rows_before(x, prev, s) * cw[3 - s:4 - s, :]
    return y


def _lru_gates(c, wa, ba, wx, bx, lam):
    r = _sigmoid(_heads(_nn, c, wa) + ba)
    ig = _sigmoid(_heads(_nn, c, wx) + bx)
    z = -lam
    sp = jnp.maximum(z, 0.0) + jnp.log(1.0 + jnp.exp(-jnp.abs(z)))
    log_a = -LRU_C * r * sp
    a = jnp.exp(log_a)
    z2 = 2.0 * log_a
    series = -z2 * (1.0 + z2 * (0.5 + z2 * (1.0 / 6.0 + z2 * (1.0 / 24.0 + z2 * (1.0 / 120.0 + z2 / 720.0)))))
    one_minus = jnp.where(z2 > -0.2, series, 1.0 - jnp.exp(z2))
    mult = jnp.sqrt(one_minus)
    return r, ig, sp, a, mult


def _layer_fwd(x, p, w, rows):
    tile = ROW_TILE
    d = D_MODEL

    def f_in(i, xb, g, *ws):
        rstd = lax.rsqrt(jnp.mean(xb * xb, axis=-1, keepdims=True) + NORM_EPS)
        hb = (xb * rstd * g).astype(BF16)
        return tuple(jnp.dot(hb, wc, preferred_element_type=F32) for wc in ws) + (hb,)

    s5x, s5g, lrux, lrug, gs, gl, h = _rows(
        "f_in", f_in, [(x, "row"), (w["g_pre"], "full")] + [(wc, "full") for wc in w["w_in"]],
        [((rows, wd), F32, "row") for wd in IN_WIDTHS] + [((rows, d), BF16, "row")], rows=rows, tile=tile)

    ys, st = _s5_fwd(s5x, w["bd"], w["cdt"], w["s5_d"], w["scf"], rows=rows, tile=tile)

    def f_s5post(i, ysb, gb, wglu, wbs):
        glv, _ = _gelu_parts(ysb)
        glu = _nn(glv, wglu)
        y2 = glu[:, :S5_WIDTH] * _sigmoid(glu[:, S5_WIDTH:]) * (gb * _sigmoid(gb))
        return (_nn(y2, wbs),)

    (z_s,) = _rows("f_s5post", f_s5post, [(ys, "row"), (s5g, "row"), (w["w_glu"], "full"), (w["w_bs"], "full")],
                   [((rows, d), F32, "row")], rows=rows, tile=tile)

    def f_gates(i, xb, prev, cw, cb, wa, ba, wx, bx, lam):
        c = _conv_fwd(i, xb, prev, cw, cb)
        _, ig, _, a, mult = _lru_gates(c, wa, ba, wx, bx, lam)
        return a, mult * (ig * c)

    a, b = _rows("f_gates", f_gates,
                 [(lrux, "row"), (lrux, "prev"), (w["conv_w"], "full"), (w["conv_b"], "full"), (w["lru_w_a"], "full"),
                  (w["lru_b_a"], "full"), (w["lru_w_x"], "full"), (w["lru_b_x"], "full"), (w["lru_lambda"], "full")],
                 [((rows, LRU_WIDTH), F32, "row")] * 2, rows=rows, tile=tile)
    hl = _lru_scan_fwd(a, b, rows=rows, tile=tile)

    def f_merge(i, hb, lg, zs, gsb, glb, xb, wbl, wout, gpost):
        z_l = _nn(hb * (lg * _sigmoid(lg)), wbl)
        merged = _sigmoid(gsb) * zs + _sigmoid(glb) * z_l
        mix = _nn(merged, wout)
        rstd = lax.rsqrt(jnp.mean(mix * mix, axis=-1, keepdims=True) + NORM_EPS)
        return xb + mix * rstd * gpost, mix, z_l

    x1, mix, z_l = _rows("f_merge", f_merge,
                         [(hl, "row"), (lrug, "row"), (z_s, "row"), (gs, "row"), (gl, "row"), (x, "row"),
                          (w["w_bl"], "full"), (w["w_out"], "full"), (w["g_post"], "full")],
                         [((rows, d), F32, "row")] * 3, rows=rows, tile=tile)

    def f_ple(i, x1b, pb, wple, wpg):
        return (x1b + _nn(pb, wple) * _sigmoid(_nn(x1b, wpg)),)

    (x2,) = _rows("f_ple", f_ple, [(x1, "row"), (p, "row"), (w["w_ple"], "full"), (w["w_ple_gate"], "full")],
                  [((rows, d), F32, "row")], rows=rows, tile=tile)
    saved = dict(x=x, h=h, s5x=s5x, s5g=s5g, lrux=lrux, lrug=lrug, gs=gs, gl=gl, ys=ys, st=st, a=a, hl=hl, z_s=z_s,
                 z_l=z_l, mix=mix, x1=x1, p=p)
    return x2, saved


def _layer_bwd(dx2, sv, w, rows):
    tile = ROW_TILE
    d = D_MODEL
    g = {}

    def b_ple(i, dxb, x1b, pb, wple, wpg):
        pe = _nn(pb, wple)
        sg = _sigmoid(_nn(x1b, wpg))
        dpe = dxb * sg
        dgt = dxb * pe * sg * (1.0 - sg)
        return dxb + _nt(dgt, wpg), _tn(pb, dpe), _tn(x1b, dgt)

    dx1, g["w_ple"], g["w_ple_gate"] = _rows(
        "b_ple", b_ple, [(dx2, "row"), (sv["x1"], "row"), (sv["p"], "row"), (w["w_ple"], "full"), (w["w_ple_gate"], "full")],
        [((rows, d), F32, "row"), ((PLE_DIM, d), F32, "acc"), ((d, d), F32, "acc")], rows=rows, tile=tile)

    def b_merge(i, dxb, mixb, zs, zl, gsb, glb, wout, gpost):
        rstd = lax.rsqrt(jnp.mean(mixb * mixb, axis=-1, keepdims=True) + NORM_EPS)
        nrm = mixb * rstd
        dn = dxb * gpost
        dmix = rstd * (dn - nrm * jnp.mean(dn * nrm, axis=-1, keepdims=True))
        ss, sl = _sigmoid(gsb), _sigmoid(glb)
        merged = ss * zs + sl * zl
        dm = _nt(dmix, wout)
        return (dm * ss, dm * sl, dm * zs * ss * (1.0 - ss), dm * zl * sl * (1.0 - sl),
                _tn(merged, dmix), jnp.sum(dxb * nrm, axis=0, keepdims=True))

    dz_s, dz_l, dgs, dgl, g["w_out"], g["g_post"] = _rows(
        "b_merge", b_merge,
        [(dx1, "row"), (sv["mix"], "row"), (sv["z_s"], "row"), (sv["z_l"], "row"), (sv["gs"], "row"), (sv["gl"], "row"),
         (w["w_out"], "full"), (w["g_post"], "full")],
        [((rows, d), F32, "row")] * 4 + [((d, d), F32, "acc"), ((1, d), F32, "acc")], rows=rows, tile=tile)

    def b_bl(i, dzl, hb, lg, wbl):
        sl = _sigmoid(lg)
        silu = lg * sl
        dy3 = _nt(dzl, wbl)
        return dy3 * silu, dy3 * hb * sl * (1.0 + lg * (1.0 - sl)), _tn(hb * silu, dzl)

    dh, dlrug, g["w_bl"] = _rows(
        "b_bl", b_bl, [(dz_l, "row"), (sv["hl"], "row"), (sv["lrug"], "row"), (w["w_bl"], "full")],
        [((rows, LRU_WIDTH), F32, "row")] * 2 + [((LRU_WIDTH, d), F32, "acc")], rows=rows, tile=tile)

    gh = _lru_scan_bwd(dh, sv["a"], rows=rows, tile=tile)

    def b_gates(i, ghb, hb, hprev, xb, xprev, cw, cb, wa, ba, wx, bx, lam):
        c = _conv_fwd(i, xb, xprev, cw, cb)
        r, ig, sp, a, mult = _lru_gates(c, wa, ba, wx, bx, lam)
        h_before = _rows_before(hb, hprev * (i > 0).astype(F32), 1)
        da = ghb * h_before
        dmult = ghb * ig * c
        dlog_a = da * a - dmult * a * a / mult
        dpre_r = dlog_a * (-LRU_C) * sp * r * (1.0 - r)
        dpre_i = ghb * mult * c * ig * (1.0 - ig)
        dc = ghb * mult * ig + _heads(_nt, dpre_r, wa) + _heads(_nt, dpre_i, wx)
        dlam = jnp.sum(dlog_a * LRU_C * r, axis=0, keepdims=True) * _sigmoid(-lam)
        return (dc, _heads_tn(c, dpre_r), _heads_tn(c, dpre_i), jnp.sum(dpre_r, axis=0, keepdims=True),
                jnp.sum(dpre_i, axis=0, keepdims=True), dlam)

    hshape = (LRU_HEADS, LRU_HEAD_DIM, LRU_HEAD_DIM)
    dc, g["lru_w_a"], g["lru_w_x"], g["lru_b_a"], g["lru_b_x"], g["lru_lambda"] = _rows(
        "b_gates", b_gates,
        [(gh, "row"), (sv["hl"], "row"), (sv["hl"], "prev"), (sv["lrux"], "row"), (sv["lrux"], "prev"),
         (w["conv_w"], "full"), (w["conv_b"], "full"), (w["lru_w_a"], "full"), (w["lru_b_a"], "full"),
         (w["lru_w_x"], "full"), (w["lru_b_x"], "full"), (w["lru_lambda"], "full")],
        [((rows, LRU_WIDTH), F32, "row"), (hshape, F32, "acc"), (hshape, F32, "acc")] + [((1, LRU_WIDTH), F32, "acc")] * 3,
        rows=rows, tile=tile)

    n_tiles = rows // min(tile, rows)

    def b_conv(i, dcb, dnext, xb, xprev, cw):
        dnext = dnext * (i < n_tiles - 1).astype(F32)
        xprev = xprev * (i > 0).astype(F32)
        dx = dcb * cw[3:4, :]
        dws = [jnp.sum(dcb * xb, axis=0, keepdims=True)]
        for s in range(1, CONV_WIDTH):
            dx = dx + _rows_after(dcb, dnext, s) * cw[3 - s:4 - s, :]
            dws.append(jnp.sum(dcb * _rows_before(xb, xprev, s), axis=0, keepdims=True))
        return dx, jnp.concatenate(dws[::-1], axis=0), jnp.sum(dcb, axis=0, keepdims=True)

    dlrux, g["conv_w"], g["conv_b"] = _rows(
        "b_conv", b_conv, [(dc, "row"), (dc, "next"), (sv["lrux"], "row"), (sv["lrux"], "prev"), (w["conv_w"], "full")],
        [((rows, LRU_WIDTH), F32, "row"), ((CONV_WIDTH, LRU_WIDTH), F32, "acc"), ((1, LRU_WIDTH), F32, "acc")],
        rows=rows, tile=tile)

    def b_s5post(i, dzs, ysb, gb, wglu, wbs):
        glv, dgelu = _gelu_parts(ysb)
        glu = _nn(glv, wglu)
        ga, gb2 = glu[:, :S5_WIDTH], glu[:, S5_WIDTH:]
        sb = _sigmoid(gb2)
        sg = _sigmoid(gb)
        silu = gb * sg
        y2 = ga * sb * silu
        dy2 = _nt(dzs, wbs)
        dglu = jnp.concatenate([dy2 * sb * silu, dy2 * ga * silu * sb * (1.0 - sb)], axis=1)
        dg = dy2 * ga * sb * sg * (1.0 + gb * (1.0 - sg))
        return _nt(dglu, wglu) * dgelu, dg, _tn(y2, dzs), _tn(glv, dglu)

    dys, ds5g, g["w_bs"], g["w_glu"] = _rows(
        "b_s5post", b_s5post, [(dz_s, "row"), (sv["ys"], "row"), (sv["s5g"], "row"), (w["w_glu"], "full"), (w["w_bs"], "full")],
        [((rows, S5_WIDTH), F32, "row")] * 2 + [((S5_WIDTH, d), F32, "acc"), ((S5_WIDTH, 2 * S5_WIDTH), F32, "acc")],
        rows=rows, tile=tile)

    ds5x, g["bd"], g["cdt"], g["s5_d"], g["abar"] = _s5_bwd(dys, sv["st"], sv["s5x"], w["bd"], w["cdt"], w["s5_d"],
                                                            w["scb"], rows=rows, tile=tile)

    dcomps = [ds5x, ds5g, dlrux, dlrug, dgs, dgl]

    def b_in(i, xb, dx1b, gpre, *rest):
        dcs, ws = rest[:6], rest[6:]
        dh = _nt(dcs[0], ws[0])
        for dcv, wc in zip(dcs[1:], ws[1:]):
            dh = dh + _nt(dcv, wc)
        rstd = lax.rsqrt(jnp.mean(xb * xb, axis=-1, keepdims=True) + NORM_EPS)
        nrm = xb * rstd
        dn = dh * gpre
        dx = rstd * (dn - nrm * jnp.mean(dn * nrm, axis=-1, keepdims=True))
        return dx1b + dx, jnp.sum(dh * nrm, axis=0, keepdims=True)

    dx, g["g_pre"] = _rows(
        "b_in", b_in, [(sv["x"], "row"), (dx1, "row"), (w["g_pre"], "full")] + [(dcv, "row") for dcv in dcomps]
        + [(wc, "full") for wc in w["w_in"]],
        [((rows, d), F32, "row"), ((1, d), F32, "acc")], rows=rows, tile=tile)

    def b_win(i, hb, dcv):
        return (_tn(hb, dcv),)

    g["w_in"] = jnp.concatenate(
        [_rows("b_win", b_win, [(sv["h"], "row"), (dcv, "row")], [((d, dcv.shape[1]), F32, "acc")], rows=rows, tile=4 * tile)[0]
         for dcv in dcomps], axis=1)
    return dx, g


SMALL = ("g_pre", "s5_a_re", "s5_a_im", "s5_log_dt", "s5_b_re", "s5_b_im", "s5_c_re", "s5_c_im", "s5_d", "conv_b",
         "lru_w_a", "lru_b_a", "lru_w_x", "lru_b_x", "lru_lambda", "g_post")
BIG = ("w_in", "w_glu", "w_bs", "conv_w", "w_bl", "w_out", "w_ple", "w_ple_gate")
BIG_SHARD_AXIS = {"w_in": 1, "w_glu": 1, "w_bs": 1, "conv_w": 1, "w_bl": 0, "w_out": 0, "w_ple": 1, "w_ple_gate": 0}


def _bcast_groups(v):
    return jnp.broadcast_to(v[:, None, :], (S5_GROUPS, S5_GROUP, S5_STATE)).reshape(S5_WIDTH, S5_STATE)


def _s5_prep_inputs(wl):
    ldt = jnp.broadcast_to(wl["s5_log_dt"][:, None], (S5_GROUPS, S5_STATE))
    gcn = lambda b: jnp.transpose(b, (0, 2, 1)).reshape(S5_WIDTH, S5_STATE)
    return (_bcast_groups(wl["s5_a_re"]), _bcast_groups(wl["s5_a_im"]), _bcast_groups(ldt), gcn(wl["s5_b_re"]),
            gcn(wl["s5_b_im"]))


def _layer_weights(wl):
    w = {}
    offs = [0]
    for wd in IN_WIDTHS:
        offs.append(offs[-1] + wd)
    w["w_in"] = [wl["w_in"][:, offs[k]:offs[k + 1]] for k in range(6)]
    for k in ("w_glu", "w_bs", "w_bl", "w_out", "w_ple", "w_ple_gate"):
        w[k] = wl[k]
    w["conv_w"] = wl["conv_w"]
    for k in ("g_pre", "g_post", "s5_d", "conv_b", "lru_b_a", "lru_b_x", "lru_lambda"):
        w[k] = wl[k].reshape(1, -1)
    w["lru_w_a"] = wl["lru_w_a"].astype(BF16)
    w["lru_w_x"] = wl["lru_w_x"].astype(BF16)
    prep_in = _s5_prep_inputs(wl)
    abr, abi, bbr, bbi = _s5_prep(*prep_in)
    w["prep_in"] = prep_in
    shape3 = (S5_GROUPS, S5_GROUP, S5_STATE)
    w["bd"] = _s5_block_diag([bbr.reshape(shape3), bbi.reshape(shape3)]).astype(BF16)
    w["cdt"] = _s5_block_diag([wl["s5_c_re"], -wl["s5_c_im"]]).astype(BF16)
    abr_s = abr.reshape(shape3)[:, 0, :]
    abi_s = abi.reshape(shape3)[:, 0, :]
    w["scf"], w["scb"] = _s5_consts(_cplx_to_lanes(abr_s), _cplx_to_lanes(abi_s))
    return w


def _layer_param_grads(g, w, wl):
    out = {}
    shape3 = (S5_GROUPS, S5_GROUP, S5_STATE)
    dbr, dbi = _s5_block_diag_extract(g["bd"])
    dcr, dci = _s5_block_diag_extract(g["cdt"])
    out["s5_c_re"], out["s5_c_im"] = dcr, -dci
    zeros = jnp.zeros(shape3, F32)
    dar = zeros.at[:, 0, :].set(g["abar"][0].reshape(S5_GROUPS, S5_STATE)).reshape(S5_WIDTH, S5_STATE)
    dai = zeros.at[:, 0, :].set(g["abar"][1].reshape(S5_GROUPS, S5_STATE)).reshape(S5_WIDTH, S5_STATE)
    cts = (dar, dai, dbr.reshape(S5_WIDTH, S5_STATE), dbi.reshape(S5_WIDTH, S5_STATE))
    d_are, d_aim, d_ldt, d_bre, d_bim = _s5_prep_bwd(*w["prep_in"], cts)
    out["s5_a_re"] = d_are.reshape(shape3).sum(axis=1)
    out["s5_a_im"] = d_aim.reshape(shape3).sum(axis=1)
    out["s5_log_dt"] = d_ldt.reshape(shape3).sum(axis=(1, 2))
    out["s5_b_re"] = jnp.transpose(d_bre.reshape(shape3), (0, 2, 1))
    out["s5_b_im"] = jnp.transpose(d_bim.reshape(shape3), (0, 2, 1))
    out["s5_d"] = g["s5_d"].reshape(-1)
    for k in ("g_pre", "g_post", "conv_b", "lru_b_a", "lru_b_x", "lru_lambda"):
        out[k] = g[k].reshape(-1)
    for k in ("lru_w_a", "lru_w_x", "conv_w", "w_in", "w_glu", "w_bs", "w_bl", "w_out", "w_ple", "w_ple_gate"):
        out[k] = g[k]
    return out


def _local_step(x, p, layers, target):
    rows = x.shape[0]
    ws = [_layer_weights(wl) for wl in layers]
    saved = []
    for i in range(DEPTH):
        x, sv = _layer_fwd(x, p[i], ws[i], rows)
        saved.append(sv)

    def f_loss(i, yb, tb):
        e = yb - tb
        return e * (1.0 / D_MODEL), jnp.sum(jnp.sum(e * e, axis=0, keepdims=True), axis=1, keepdims=True)

    dx, sq = _rows("f_loss", f_loss, [(x, "row"), (target, "row")],
                   [((rows, D_MODEL), F32, "row"), ((1, 1), F32, "acc")], rows=rows, tile=ROW_TILE)
    loss = sq[0, 0] * (0.5 / D_MODEL)
    grads = [None] * DEPTH
    for i in reversed(range(DEPTH)):
        dx, g = _layer_bwd(dx, saved[i], ws[i], rows)
        grads[i] = _layer_param_grads(g, ws[i], layers[i])
    return loss, dx, grads


def _place():
    return lax.axis_index("x"), lax.axis_index("y"), lax.axis_index("c")


def _other_chips(x, y):
    return [(1 - x, y), (x, 1 - y), (1 - x, 1 - y)]


def _any_spec():
    return pl.BlockSpec(memory_space=pl.ANY)


ICI_PIECES = 1
D2D_PIECES = 1
D2D_SOLO_PIECES = 1


def _pieces(rows, k):
    step = rows // k
    assert step * k == rows and step % 16 == 0, (rows, k)
    return [(q * step, step) for q in range(k)]


def _gather_chips(name, v, via_sibling):
    rows = v.shape[0]
    half = rows // 2

    n_sent = half if via_sibling else rows

    def body(v_ref, out_ref, send_sems, recv_sems):
        x, y, c = _place()
        me = 2 * x + y
        chips = _other_chips(x, y)
        slots = [2 * cx + cy for cx, cy in chips]

        def part(slot, hc, o=0, s=n_sent):
            return out_ref.at[slot, pl.ds(hc * half + o, s), :] if via_sibling else out_ref.at[slot, pl.ds(o, s), :]

        def own(o=0, s=n_sent):
            return v_ref.at[pl.ds(c * half + o, s), :] if via_sibling else v_ref.at[pl.ds(o, s), :]

        def copy(k, src, dst, to):
            return pltpu.make_async_remote_copy(src_ref=src, dst_ref=dst, send_sem=send_sems.at[k], recv_sem=recv_sems.at[k],
                                                device_id=to, device_id_type=MESH)

        for k in range(3):
            for o, s in _pieces(n_sent, ICI_PIECES):
                copy(k, own(o, s), part(me, c, o, s), (*chips[k], c)).start()
        for k in range(3):
            copy(k, own(), part(slots[k], c), (*chips[k], c)).wait_recv()
            if via_sibling:
                for o, s in _pieces(n_sent, D2D_PIECES):
                    copy(3 + k, part(slots[k], c, o, s), part(slots[k], c, o, s), (x, y, 1 - c)).start()
        if via_sibling:
            for k in range(3):
                copy(3 + k, own(), part(slots[k], 1 - c), (x, y, 1 - c)).wait_recv()
        for k in range(6 if via_sibling else 3):
            copy(k, own(), part(me, c), (x, y, 1 - c)).wait_send()

    n_sem = 6 if via_sibling else 3
    others = pl.pallas_call(
        body, name=name, out_shape=jax.ShapeDtypeStruct((4,) + v.shape, v.dtype),
        in_specs=[_any_spec()], out_specs=_any_spec(),
        scratch_shapes=[pltpu.SemaphoreType.DMA((n_sem,)), pltpu.SemaphoreType.DMA((n_sem,))],
    )(v)
    return lax.dynamic_update_slice(others, v[None], (2 * lax.axis_index("x") + lax.axis_index("y"), 0, 0))


def _rs_sibling(gr):
    half = gr.shape[1] // 2

    def body(g_ref, got_ref, send_sem, recv_sem):
        x, y, c = _place()

        def give(src, dst):
            return pltpu.make_async_remote_copy(src_ref=src, dst_ref=dst, send_sem=send_sem, recv_sem=recv_sem,
                                                device_id=(x, y, 1 - c), device_id_type=MESH)

        copy = give(g_ref.at[:, pl.ds((1 - c) * half, half), :], got_ref)
        copy.start()
        copy.wait()

    return pl.pallas_call(
        body, name="rs_sibling", out_shape=jax.ShapeDtypeStruct((4, half, LANES), F32),
        in_specs=[_any_spec()], out_specs=_any_spec(),
        scratch_shapes=[pltpu.SemaphoreType.DMA, pltpu.SemaphoreType.DMA],
    )(gr)


def _rs_chips(a16):
    half = a16.shape[1]

    def body(a16_ref, got_ref, send_sems, recv_sems):
        x, y, c = _place()
        chips = _other_chips(x, y)

        def copy(k, o=0, s=half):
            cx, cy = chips[k]
            return pltpu.make_async_remote_copy(
                src_ref=a16_ref.at[2 * cx + cy, pl.ds(o, s), :], dst_ref=got_ref.at[k, pl.ds(o, s), :],
                send_sem=send_sems.at[k], recv_sem=recv_sems.at[k], device_id=(cx, cy, c), device_id_type=MESH)

        for k in range(3):
            for o, s in _pieces(half, ICI_PIECES):
                copy(k, o, s).start()
        for k in range(3):
            copy(k).wait()

    return pl.pallas_call(
        body, name="rs_chips", out_shape=jax.ShapeDtypeStruct((3, half, LANES), BF16),
        in_specs=[_any_spec()], out_specs=_any_spec(),
        scratch_shapes=[pltpu.SemaphoreType.DMA((3,)), pltpu.SemaphoreType.DMA((3,))],
    )(a16)


def _swap_halves(v):
    def body(v_ref, out_ref, send_sem, recv_sem):
        x, y, c = _place()

        def give(hc):
            return pltpu.make_async_remote_copy(src_ref=v_ref, dst_ref=out_ref.at[hc], send_sem=send_sem, recv_sem=recv_sem,
                                                device_id=(x, y, 1 - c), device_id_type=MESH)

        give(c).start()
        give(c).wait_send()
        give(1 - c).wait_recv()

    other = pl.pallas_call(
        body, name="swap_halves", out_shape=jax.ShapeDtypeStruct((2,) + v.shape, v.dtype),
        in_specs=[_any_spec()], out_specs=_any_spec(),
        scratch_shapes=[pltpu.SemaphoreType.DMA, pltpu.SemaphoreType.DMA],
    )(v)
    both = lax.dynamic_update_slice(other, v[None], (lax.axis_index("c"), 0, 0))
    return both.reshape(2 * v.shape[0], v.shape[1])


WIDE = 1024
PACK_TILE = 3072
GRAD_ROWS_UNIT = 2 * PACK_TILE


def _pack(parts, rows_unit, dtype):
    flat = jnp.concatenate([q.reshape(-1).astype(dtype) for q in parts])
    unit = rows_unit * LANES
    total = -(-flat.shape[0] // unit) * unit
    return jnp.pad(flat, (0, total - flat.shape[0])).reshape(-1, LANES)


def _unpack(flat, shapes):
    out, off = [], 0
    for s in shapes:
        n = 1
        for q in s:
            n *= q
        out.append(flat[off:off + n].reshape(s))
        off += n
    return out


def _to_slots(name, full):
    dp, r, c = full.shape
    if BIG_SHARD_AXIS[name] == 1:
        return jnp.transpose(full.reshape(dp, r, 4, c // 4), (2, 0, 1, 3)).reshape(4, -1)
    return jnp.transpose(full.reshape(dp, 4, r // 4, c), (1, 0, 2, 3)).reshape(4, -1)


def _from_slots(name, slots, shard_shape):
    dp, r, c = shard_shape
    v = slots.reshape(4, dp, r, c)
    if BIG_SHARD_AXIS[name] == 1:
        return jnp.transpose(v, (1, 2, 0, 3)).reshape(dp, r, 4 * c)
    return jnp.transpose(v, (1, 0, 2, 3)).reshape(dp, 4 * r, c)


def _adamw(name, w, g, m, v, tile):
    def fn(i, wb, gb, mb, vb):
        m2 = ADAM_B1 * mb + (1.0 - ADAM_B1) * gb
        v2 = ADAM_B2 * vb + (1.0 - ADAM_B2) * (gb * gb)
        m_hat = m2 / (1.0 - ADAM_B1 ** ADAM_STEP)
        v_hat = v2 / (1.0 - ADAM_B2 ** ADAM_STEP)
        return -ADAM_LR * (m_hat / (jnp.sqrt(v_hat) + ADAM_EPS) + ADAM_WD * wb), m2, v2

    return _rows(name, fn, [(w, "row"), (g, "row"), (m, "row"), (v, "row")], [(w.shape, F32, "row")] * 3,
                 rows=w.shape[0], tile=tile)


def _as_2d(a):
    return a.reshape(-1, a.shape[-1])


def _adam_tile(rows):
    for t in (256, 184, 128, 64, 32, 16, 8):
        if rows % t == 0:
            return t
    return rows


def kernel(x, p, g_pre, w_in, s5_a_re, s5_a_im, s5_log_dt, s5_b_re, s5_b_im, s5_c_re, s5_c_im, s5_d, w_glu, w_bs, conv_w, conv_b, lru_w_a, lru_b_a, lru_w_x, lru_b_x, lru_lambda, w_bl, w_out, g_post, w_ple, w_ple_gate, loss_target, m_g_pre, m_w_in, m_s5_a_re, m_s5_a_im, m_s5_log_dt, m_s5_b_re, m_s5_b_im, m_s5_c_re, m_s5_c_im, m_s5_d, m_w_glu, m_w_bs, m_conv_w, m_conv_b, m_lru_w_a, m_lru_b_a, m_lru_w_x, m_lru_b_x, m_lru_lambda, m_w_bl, m_w_out, m_g_post, m_w_ple, m_w_ple_gate, v_g_pre, v_w_in, v_s5_a_re, v_s5_a_im, v_s5_log_dt, v_s5_b_re, v_s5_b_im, v_s5_c_re, v_s5_c_im, v_s5_d, v_w_glu, v_w_bs, v_conv_w, v_conv_b, v_lru_w_a, v_lru_b_a, v_lru_w_x, v_lru_b_x, v_lru_lambda, v_w_bl, v_w_out, v_g_post, v_w_ple, v_w_ple_gate):
    wts = dict(g_pre=g_pre, w_in=w_in, s5_a_re=s5_a_re, s5_a_im=s5_a_im, s5_log_dt=s5_log_dt, s5_b_re=s5_b_re,
               s5_b_im=s5_b_im, s5_c_re=s5_c_re, s5_c_im=s5_c_im, s5_d=s5_d, w_glu=w_glu, w_bs=w_bs, conv_w=conv_w,
               conv_b=conv_b, lru_w_a=lru_w_a, lru_b_a=lru_b_a, lru_w_x=lru_w_x, lru_b_x=lru_b_x, lru_lambda=lru_lambda,
               w_bl=w_bl, w_out=w_out, g_post=g_post, w_ple=w_ple, w_ple_gate=w_ple_gate)
    mom1 = dict(g_pre=m_g_pre, w_in=m_w_in, s5_a_re=m_s5_a_re, s5_a_im=m_s5_a_im, s5_log_dt=m_s5_log_dt, s5_b_re=m_s5_b_re,
                s5_b_im=m_s5_b_im, s5_c_re=m_s5_c_re, s5_c_im=m_s5_c_im, s5_d=m_s5_d, w_glu=m_w_glu, w_bs=m_w_bs,
                conv_w=m_conv_w, conv_b=m_conv_b, lru_w_a=m_lru_w_a, lru_b_a=m_lru_b_a, lru_w_x=m_lru_w_x, lru_b_x=m_lru_b_x,
                lru_lambda=m_lru_lambda, w_bl=m_w_bl, w_out=m_w_out, g_post=m_g_post, w_ple=m_w_ple, w_ple_gate=m_w_ple_gate)
    mom2 = dict(g_pre=v_g_pre, w_in=v_w_in, s5_a_re=v_s5_a_re, s5_a_im=v_s5_a_im, s5_log_dt=v_s5_log_dt, s5_b_re=v_s5_b_re,
                s5_b_im=v_s5_b_im, s5_c_re=v_s5_c_re, s5_c_im=v_s5_c_im, s5_d=v_s5_d, w_glu=v_w_glu, w_bs=v_w_bs,
                conv_w=v_conv_w, conv_b=v_conv_b, lru_w_a=v_lru_w_a, lru_b_a=v_lru_b_a, lru_w_x=v_lru_w_x, lru_b_x=v_lru_b_x,
                lru_lambda=v_lru_lambda, w_bl=v_w_bl, w_out=v_w_out, g_post=v_g_post, w_ple=v_w_ple, w_ple_gate=v_w_ple_gate)
    names = list(wts)

    def wire(name):
        return lax.bitcast_convert_type(wts[name], BF16) if name == "conv_w" else wts[name].astype(BF16)

    wire_shapes = [wire(k).shape for k in BIG]
    gathered = _gather_chips("gather_weights", _pack([wire(k) for k in BIG], 128, BF16), via_sibling=True)
    per_chip = [_unpack(gathered[j].reshape(-1), wire_shapes) for j in range(4)]
    whole = {}
    for idx, k in enumerate(BIG):
        v = jnp.stack([per_chip[j][idx] for j in range(4)])
        if k == "conv_w":
            v = lax.bitcast_convert_type(v, F32)
        whole[k] = _from_slots(k, v.reshape(4, -1), wts[k].shape)
    layers = []
    for i in range(DEPTH):
        wl = {k: whole[k][i] for k in BIG}
        wl.update({k: wts[k][i] for k in SMALL})
        layers.append(wl)

    loss, grad_x, grads = _local_step(x[0], p[:, 0], layers, loss_target[0])
    loss = lax.psum(loss, ("x", "y", "c"))

    big_slots = jnp.concatenate([_to_slots(k, jnp.stack([grads[i][k] for i in range(DEPTH)])) for k in BIG], axis=1)
    small_shapes = [wts[k].shape for k in SMALL]
    small_flat = jnp.concatenate([jnp.stack([grads[i][k] for i in range(DEPTH)]).reshape(-1) for k in SMALL])
    n_small = small_flat.shape[0]
    small_q = -(-n_small // (4 * 8 * WIDE)) * 8 * WIDE
    small_slots = jnp.pad(small_flat, (0, 4 * small_q - n_small)).reshape(4, small_q)
    n_big = big_slots.shape[1]
    n_big_pad = -(-n_big // (8 * WIDE)) * 8 * WIDE
    n_slot = n_big_pad + small_q
    unit = GRAD_ROWS_UNIT * LANES
    n_slot_pad = -(-n_slot // unit) * unit
    gr = jnp.concatenate([jnp.pad(big_slots, ((0, 0), (0, n_big_pad - n_big))), small_slots,
                          jnp.zeros((4, n_slot_pad - n_slot), F32)], axis=1).reshape(4, -1, LANES)
    got = _rs_sibling(gr)
    half = got.shape[1]
    mine = lax.dynamic_slice_in_dim(gr, lax.axis_index("c") * half, half, axis=1)
    rows2d = lambda a: a.reshape(-1, LANES)

    def f_add1(i, a, b):
        s = a + b
        return s, s

    a32, a16 = _rows("rs_add1", f_add1, [(rows2d(mine), "row"), (rows2d(got), "row")],
                     [((4 * half, LANES), F32, "row"), ((4 * half, LANES), BF16, "row")], rows=4 * half, tile=PACK_TILE)
    got3 = _rs_chips(a16.reshape(4, half, LANES))
    own = lax.dynamic_index_in_dim(a32.reshape(4, half, LANES), 2 * lax.axis_index("x") + lax.axis_index("y"), 0, keepdims=False)

    def f_add2(i, o, g0, g1, g2):
        return (((o + g0.astype(F32)) + g1.astype(F32)) + g2.astype(F32),)

    (red_half,) = _rows("rs_add2", f_add2, [(own, "row")] + [(got3[k], "row") for k in range(3)],
                        [((half, LANES), F32, "row")], rows=half, tile=PACK_TILE)
    red = _swap_halves(red_half).reshape(-1)
    small_red = _gather_chips("gather_small", red[n_big_pad:n_big_pad + small_q].reshape(-1, LANES), via_sibling=False)
    small_red = small_red.reshape(-1)[:n_small]

    big_shapes = [wts[k].shape for k in BIG]
    grad_out = dict(zip(BIG, _unpack(red[:n_big], big_shapes)))
    grad_out.update(zip(SMALL, _unpack(small_red, small_shapes)))
    delta, new_m, new_v = {}, {}, {}
    for k in BIG:
        w2 = _as_2d(wts[k])
        res = _adamw("adamw_" + k, w2, _as_2d(grad_out[k]), _as_2d(mom1[k]), _as_2d(mom2[k]), _adam_tile(w2.shape[0]))
        delta[k], new_m[k], new_v[k] = [r.reshape(wts[k].shape) for r in res]
    pack_small = lambda d: jnp.pad(jnp.concatenate([d[k].reshape(-1) for k in SMALL]), (0, 4 * small_q - n_small)).reshape(-1, WIDE)
    res = _adamw("adamw_small", pack_small(wts), pack_small(grad_out), pack_small(mom1), pack_small(mom2),
                 _adam_tile(4 * small_q // WIDE))
    for d, r in zip((delta, new_m, new_v), res):
        d.update(zip(SMALL, _unpack(r.reshape(-1), small_shapes)))
    return (loss, grad_x[None], *[grad_out[k] for k in names], *[delta[k] for k in names],
            *[new_m[k] for k in names], *[new_v[k] for k in names])
```

```python
import jax
import jax.numpy as jnp
from jax import lax
from jax.experimental import pallas as pl
from jax.experimental.pallas import tpu as pltpu

F32 = jnp.float32
BF16 = jnp.bfloat16
MESH = pl.DeviceIdType.MESH

DEPTH = 2
D_MODEL = 1024
NORM_EPS = 1e-6
S5_WIDTH = 512
S5_GROUPS = 32
S5_GROUP = 16
S5_STATE = 64
LRU_WIDTH = 1280
LRU_HEADS = 10
LRU_HEAD_DIM = 128
LRU_C = 8.0
CONV_WIDTH = 4
PLE_DIM = 256
IN_WIDTHS = (S5_WIDTH, S5_WIDTH, LRU_WIDTH, LRU_WIDTH, D_MODEL, D_MODEL)
ADAM_LR = 0.001
ADAM_B1 = 0.9
ADAM_B2 = 0.999
ADAM_EPS = 1e-08
ADAM_WD = 0.01
ADAM_STEP = 10

SUBLANES = 8
LANES = 128
S5_HALF_IN = S5_WIDTH // 2
S5_CPLX = S5_GROUPS * S5_STATE
S5_HALF_CPLX = S5_CPLX // 2
S5_LANES = 2 * S5_CPLX
VMEM_LIMIT = 48 * 2 ** 20
ROW_TILE = 256


def _sigmoid(x):
    return 1.0 / (1.0 + jnp.exp(-x))


def _gelu_parts(x):
    k = 0.7978845608028654
    t = jnp.tanh(k * (x + 0.044715 * x * x * x))
    val = 0.5 * x * (1.0 + t)
    grad = 0.5 * (1.0 + t) + 0.5 * x * (1.0 - t * t) * k * (1.0 + 3.0 * 0.044715 * x * x)
    return val, grad


def _nn(a, w):
    return jnp.dot(a.astype(BF16), w.astype(BF16), preferred_element_type=F32)


def _nt(a, w):
    return lax.dot_general(a.astype(BF16), w.astype(BF16), (((1,), (1,)), ((), ())), preferred_element_type=F32)


def _tn(a, b):
    return lax.dot_general(a.astype(BF16), b.astype(BF16), (((0,), (0,)), ((), ())), preferred_element_type=F32)


def _heads(op, a, w):
    d = LRU_HEAD_DIM
    return jnp.concatenate([op(a[:, h * d:(h + 1) * d], w[h]) for h in range(LRU_HEADS)], axis=1)


def _heads_tn(a, b):
    d = LRU_HEAD_DIM
    return jnp.stack([_tn(a[:, h * d:(h + 1) * d], b[:, h * d:(h + 1) * d]) for h in range(LRU_HEADS)], axis=0)


def _rows_before(x, halo, s):
    main = pltpu.roll(x, s, 0)
    head = pltpu.roll(jnp.concatenate([halo, x[0:SUBLANES]], axis=0), s, 0)[SUBLANES:2 * SUBLANES]
    return jnp.concatenate([head, main[SUBLANES:]], axis=0)


def _rows_after(x, halo, s):
    n = x.shape[0]
    main = pltpu.roll(x, n - s, 0)
    tail = pltpu.roll(jnp.concatenate([x[n - SUBLANES:], halo], axis=0), 2 * SUBLANES - s, 0)[0:SUBLANES]
    return jnp.concatenate([main[:n - SUBLANES], tail], axis=0)


def _rows(name, fn, ins, outs, *, rows, tile):
    tile = min(tile, rows)
    n = rows // tile
    assert n * tile == rows, (name, rows, tile)
    in_specs = []
    for arr, kind in ins:
        halo = SUBLANES * (4 // arr.dtype.itemsize)
        per, last = tile // halo, rows // halo - 1
        if kind in ("row", "raw"):
            in_specs.append(pl.BlockSpec((tile, arr.shape[1]), lambda i: (i, 0)))
        elif kind == "prev":
            in_specs.append(pl.BlockSpec((halo, arr.shape[1]), lambda i, per=per: (jnp.maximum(i * per - 1, 0), 0)))
        elif kind == "next":
            in_specs.append(pl.BlockSpec((halo, arr.shape[1]),
                                         lambda i, per=per, last=last: (jnp.minimum((i + 1) * per, last), 0)))
        else:
            in_specs.append(pl.BlockSpec(arr.shape, lambda i, nd=arr.ndim: (0,) * nd))
    out_shape, out_specs = [], []
    for shape, dtype, kind in outs:
        out_shape.append(jax.ShapeDtypeStruct(shape, dtype))
        if kind == "row":
            out_specs.append(pl.BlockSpec((tile, shape[1]), lambda i: (i, 0)))
        else:
            out_specs.append(pl.BlockSpec(shape, lambda i, nd=len(shape): (0,) * nd))
    n_in = len(ins)

    def load(ref, kind):
        v = ref[...]
        if kind in ("row", "prev", "next"):
            v = v.astype(F32)
        if kind == "prev":
            v = v[v.shape[0] - SUBLANES:]
        if kind == "next":
            v = v[:SUBLANES]
        return v

    def body(*refs):
        i = pl.program_id(0)
        vals = fn(i, *[load(r, kind) for r, (_, kind) in zip(refs[:n_in], ins)])
        assert len(vals) == len(outs), name
        for r, v, (_, _, kind) in zip(refs[n_in:], vals, outs):
            if kind == "row":
                r[...] = v.astype(r.dtype)
            else:
                @pl.when(i == 0)
                def _():
                    r[...] = jnp.zeros_like(r)

                r[...] += v.astype(r.dtype)

    return pl.pallas_call(
        body, name=name, grid=(n,), in_specs=in_specs, out_specs=out_specs, out_shape=out_shape,
        compiler_params=pltpu.CompilerParams(dimension_semantics=("arbitrary",), vmem_limit_bytes=VMEM_LIMIT),
    )(*[a for a, _ in ins])


def _s5_discretise(are, aim, ldt, bre, bim):
    dt = jnp.exp(ldt)
    er = jnp.exp(are * dt)
    abr = er * jnp.cos(aim * dt)
    abi = er * jnp.sin(aim * dt)
    den = are * are + aim * aim
    zr = ((abr - 1.0) * are + abi * aim) / den
    zi = (abi * are - (abr - 1.0) * aim) / den
    return abr, abi, zr * bre - zi * bim, zr * bim + zi * bre


def _s5_prep(are, aim, ldt, bre, bim):
    def body(a, b, c, d, e, o0, o1, o2, o3):
        r = _s5_discretise(a[...], b[...], c[...], d[...], e[...])
        o0[...], o1[...], o2[...], o3[...] = r

    sd = jax.ShapeDtypeStruct(are.shape, F32)
    return pl.pallas_call(body, name="s5_prep", out_shape=[sd] * 4)(are, aim, ldt, bre, bim)


def _s5_prep_bwd(are, aim, ldt, bre, bim, cts):
    def body(a, b, c, d, e, c0, c1, c2, c3, o0, o1, o2, o3, o4):
        _, vjp = jax.vjp(_s5_discretise, a[...], b[...], c[...], d[...], e[...])
        r = vjp((c0[...], c1[...], c2[...], c3[...]))
        o0[...], o1[...], o2[...], o3[...], o4[...] = r

    sd = jax.ShapeDtypeStruct(are.shape, F32)
    return pl.pallas_call(body, name="s5_prep_bwd", out_shape=[sd] * 5)(are, aim, ldt, bre, bim, *cts)


def _s5_consts(abr, abi):
    shape = (SUBLANES, S5_CPLX)

    def body(ar_ref, ai_ref, f_ref, b_ref):
        ar = jnp.broadcast_to(ar_ref[...], shape)
        ai = jnp.broadcast_to(ai_ref[...], shape)
        row = lax.broadcasted_iota(jnp.int32, shape, 0)

        def cmul(p, q):
            return (p[0] * q[0] - p[1] * q[1], p[0] * q[1] + p[1] * q[0])

        a1 = (ar, ai)
        a2 = cmul(a1, a1)
        a3 = cmul(a2, a1)
        a4 = cmul(a2, a2)
        pw = [a1, a2, a3, a4, cmul(a4, a1), cmul(a4, a2), cmul(a4, a3), cmul(a4, a4)]

        def by_row(vals):
            out = vals[7]
            for r in range(6, -1, -1):
                out = jnp.where(row == r, vals[r], out)
            return out

        fwd, rev = [], []
        for a, k in ((a1, 1), (a2, 2), (a4, 4)):
            fwd += [jnp.where(row >= k, a[0], 0.0), jnp.where(row >= k, a[1], 0.0)]
            rev += [jnp.where(row <= 7 - k, a[0], 0.0), jnp.where(row <= 7 - k, -a[1], 0.0)]
        fwd += [by_row([p[0] for p in pw]), by_row([p[1] for p in pw])]
        rev += [by_row([pw[7 - r][0] for r in range(8)]), by_row([-pw[7 - r][1] for r in range(8)])]
        f_ref[...] = jnp.concatenate(fwd, axis=0)
        b_ref[...] = jnp.concatenate(rev, axis=0)

    sd = jax.ShapeDtypeStruct((8 * SUBLANES, S5_CPLX), F32)
    return pl.pallas_call(body, name="s5_consts", out_shape=[sd, sd])(abr, abi)


def _s5_lane_offsets(q):
    re = (q // 8) * 2 * S5_HALF_CPLX + (q % 8) * LANES
    return re, re + S5_HALF_CPLX


def _s5_scan(s_ref, sc_ref, carry_ref, nblk, reverse):
    group = 2
    edge = 0 if reverse else SUBLANES - 1
    for q0 in range(0, S5_CPLX // LANES, group):
        offs = [_s5_lane_offsets(q) for q in range(q0, q0 + group)]
        consts = [[sc_ref[k * SUBLANES:(k + 1) * SUBLANES, q * LANES:(q + 1) * LANES] for k in range(8)]
                  for q in range(q0, q0 + group)]
        carry0 = tuple(carry_ref[:, o:o + LANES] for pair in offs for o in pair)

        def blk(t, carry, offs=offs, consts=consts):
            b = (nblk - 1 - t) if reverse else t
            r0 = pl.multiple_of(b * SUBLANES, SUBLANES)
            new = []
            for j, ((re, im), (a1r, a1i, a2r, a2i, a4r, a4i, pr, pi)) in enumerate(zip(offs, consts)):
                xr = s_ref[pl.ds(r0, SUBLANES), re:re + LANES]
                xi = s_ref[pl.ds(r0, SUBLANES), im:im + LANES]
                for ar, ai, sh in ((a1r, a1i, 1), (a2r, a2i, 2), (a4r, a4i, 4)):
                    shift = SUBLANES - sh if reverse else sh
                    sr = pltpu.roll(xr, shift, 0)
                    si = pltpu.roll(xi, shift, 0)
                    xr, xi = xr + ar * sr - ai * si, xi + ar * si + ai * sr
                cr, ci = carry[2 * j], carry[2 * j + 1]
                xr, xi = xr + pr * cr - pi * ci, xi + pr * ci + pi * cr
                s_ref[pl.ds(r0, SUBLANES), re:re + LANES] = xr
                s_ref[pl.ds(r0, SUBLANES), im:im + LANES] = xi
                new.append(jnp.broadcast_to(xr[edge:edge + 1, :], (SUBLANES, LANES)))
                new.append(jnp.broadcast_to(xi[edge:edge + 1, :], (SUBLANES, LANES)))
            return tuple(new)

        carry = lax.fori_loop(0, nblk, blk, carry0, unroll=2)
        for k, o in enumerate(o for pair in offs for o in pair):
            carry_ref[:, o:o + LANES] = carry[k]


def _s5_fwd(u, bd, cdt, dskip, sc, *, rows, tile):
    n = rows // tile
    nblk = tile // SUBLANES
    hc = 2 * S5_HALF_CPLX

    def body(u_ref, bd_ref, cdt_ref, d_ref, sc_ref, y_ref, s_ref, carry_ref):
        @pl.when(pl.program_id(0) == 0)
        def _():
            carry_ref[...] = jnp.zeros_like(carry_ref)

        ub = u_ref[...].astype(BF16)
        for h in range(2):
            s_ref[:, h * hc:(h + 1) * hc] = jnp.dot(ub[:, h * S5_HALF_IN:(h + 1) * S5_HALF_IN], bd_ref[h],
                                                    preferred_element_type=F32)
        _s5_scan(s_ref, sc_ref, carry_ref, nblk, reverse=False)
        ys = [_nt(s_ref[:, h * hc:(h + 1) * hc], cdt_ref[h]) for h in range(2)]
        y_ref[...] = jnp.concatenate(ys, axis=1) + d_ref[...] * u_ref[...]

    full = lambda a: pl.BlockSpec(a.shape, lambda i, nd=a.ndim: (0,) * nd)
    return pl.pallas_call(
        body, name="s5_fwd", grid=(n,),
        in_specs=[pl.BlockSpec((tile, S5_WIDTH), lambda i: (i, 0)), full(bd), full(cdt), full(dskip), full(sc)],
        out_specs=[pl.BlockSpec((tile, S5_WIDTH), lambda i: (i, 0)), pl.BlockSpec((tile, S5_LANES), lambda i: (i, 0))],
        out_shape=[jax.ShapeDtypeStruct((rows, S5_WIDTH), F32), jax.ShapeDtypeStruct((rows, S5_LANES), F32)],
        scratch_shapes=[pltpu.VMEM((SUBLANES, S5_LANES), F32)],
        compiler_params=pltpu.CompilerParams(dimension_semantics=("arbitrary",), vmem_limit_bytes=VMEM_LIMIT),
    )(u, bd, cdt, dskip, sc)


def _s5_bwd(dy, s, u, bd, cdt, dskip, sc, *, rows, tile):
    n = rows // tile
    nblk = tile // SUBLANES
    hc = 2 * S5_HALF_CPLX
    per8 = tile // SUBLANES

    def body(dy_ref, s_ref, sp_ref, u_ref, bd_ref, cdt_ref, d_ref, sc_ref,
             du_ref, dbd_ref, dcdt_ref, dd_ref, da_ref, g_ref, carry_ref):
        i = pl.program_id(0)

        @pl.when(i == 0)
        def _():
            carry_ref[...] = jnp.zeros_like(carry_ref)
            dbd_ref[...] = jnp.zeros_like(dbd_ref)
            dcdt_ref[...] = jnp.zeros_like(dcdt_ref)
            dd_ref[...] = jnp.zeros_like(dd_ref)
            da_ref[...] = jnp.zeros_like(da_ref)

        dy = dy_ref[...]
        dyb = dy.astype(BF16)
        u = u_ref[...]
        ub = u.astype(BF16)
        for h in range(2):
            g_ref[:, h * hc:(h + 1) * hc] = jnp.dot(dyb[:, h * S5_HALF_IN:(h + 1) * S5_HALF_IN], cdt_ref[h],
                                                    preferred_element_type=F32)
        _s5_scan(g_ref, sc_ref, carry_ref, nblk, reverse=True)
        dus = []
        for h in range(2):
            gb = g_ref[:, h * hc:(h + 1) * hc].astype(BF16)
            sb = s_ref[:, h * hc:(h + 1) * hc].astype(BF16)
            dus.append(_nt(gb, bd_ref[h]))
            dbd_ref[h] += _tn(ub[:, h * S5_HALF_IN:(h + 1) * S5_HALF_IN], gb)
            dcdt_ref[h] += _tn(dyb[:, h * S5_HALF_IN:(h + 1) * S5_HALF_IN], sb)
        du_ref[...] = (jnp.concatenate(dus, axis=1) + d_ref[...] * dy).astype(du_ref.dtype)
        dd_ref[...] += jnp.sum(dy * u, axis=0, keepdims=True)

        not_first = (i < n - 1).astype(F32)
        for h in range(2):
            re, im = h * hc, h * hc + S5_HALF_CPLX
            ssr = _rows_before(s_ref[:, re:re + S5_HALF_CPLX], sp_ref[:, re:re + S5_HALF_CPLX] * not_first, 1)
            ssi = _rows_before(s_ref[:, im:im + S5_HALF_CPLX], sp_ref[:, im:im + S5_HALF_CPLX] * not_first, 1)
            gr = g_ref[:, re:re + S5_HALF_CPLX]
            gi = g_ref[:, im:im + S5_HALF_CPLX]
            lanes = slice(h * S5_HALF_CPLX, (h + 1) * S5_HALF_CPLX)
            da_ref[0:1, lanes] += jnp.sum(ssr * gr + ssi * gi, axis=0, keepdims=True)
            da_ref[1:2, lanes] += jnp.sum(ssr * gi - ssi * gr, axis=0, keepdims=True)

    full = lambda a: pl.BlockSpec(a.shape, lambda i, nd=a.ndim: (0,) * nd)
    rev = lambda i: (n - 1 - i, 0)
    wshape = (2, S5_HALF_IN, hc)
    return pl.pallas_call(
        body, name="s5_bwd", grid=(n,),
        in_specs=[pl.BlockSpec((tile, S5_WIDTH), rev), pl.BlockSpec((tile, S5_LANES), rev),
                  pl.BlockSpec((SUBLANES, S5_LANES), lambda i: (jnp.maximum((n - 1 - i) * per8 - 1, 0), 0)),
                  pl.BlockSpec((tile, S5_WIDTH), rev), full(bd), full(cdt), full(dskip), full(sc)],
        out_specs=[pl.BlockSpec((tile, S5_WIDTH), rev),
                   pl.BlockSpec(wshape, lambda i: (0, 0, 0)), pl.BlockSpec(wshape, lambda i: (0, 0, 0)),
                   pl.BlockSpec((1, S5_WIDTH), lambda i: (0, 0)), pl.BlockSpec((SUBLANES, S5_CPLX), lambda i: (0, 0))],
        out_shape=[jax.ShapeDtypeStruct((rows, S5_WIDTH), BF16), jax.ShapeDtypeStruct(wshape, F32),
                   jax.ShapeDtypeStruct(wshape, F32), jax.ShapeDtypeStruct((1, S5_WIDTH), F32),
                   jax.ShapeDtypeStruct((SUBLANES, S5_CPLX), F32)],
        scratch_shapes=[pltpu.VMEM((tile, S5_LANES), F32), pltpu.VMEM((SUBLANES, S5_LANES), F32)],
        compiler_params=pltpu.CompilerParams(dimension_semantics=("arbitrary",), vmem_limit_bytes=VMEM_LIMIT),
    )(dy, s, s, u, bd, cdt, dskip, sc)


def _s5_block_diag(parts):
    v = jnp.stack(parts, axis=2).reshape(2, 16, S5_GROUP, 2, S5_STATE)
    eye = jnp.eye(16, dtype=v.dtype)
    return jnp.einsum("hgcpn,gk->hgcpkn", v, eye).reshape(2, S5_HALF_IN, 2 * S5_HALF_CPLX)


def _s5_block_diag_extract(m):
    v = m.reshape(2, 16, S5_GROUP, 2, 16, S5_STATE)
    d = jnp.diagonal(v, axis1=1, axis2=4)
    d = jnp.transpose(d, (2, 0, 4, 1, 3)).reshape(2, S5_GROUPS, S5_GROUP, S5_STATE)
    return d[0], d[1]


def _cplx_to_lanes(v):
    return v.reshape(1, S5_CPLX)


def _lru_scan_fwd(a, b, *, rows, tile):
    n = rows // tile
    nblk = tile // SUBLANES
    group = 2

    def body(a_ref, b_ref, h_ref, carry_ref):
        @pl.when(pl.program_id(0) == 0)
        def _():
            carry_ref[...] = jnp.zeros_like(carry_ref)

        row = lax.broadcasted_iota(jnp.int32, (SUBLANES, LANES), 0)
        for q0 in range(0, LRU_WIDTH // LANES, group):
            offs = [q * LANES for q in range(q0, q0 + group)]

            def blk(t, carry, offs=offs):
                r0 = pl.multiple_of(t * SUBLANES, SUBLANES)
                new = []
                for j, o in enumerate(offs):
                    av = a_ref[pl.ds(r0, SUBLANES), o:o + LANES]
                    xv = b_ref[pl.ds(r0, SUBLANES), o:o + LANES]
                    for sh in (1, 2, 4):
                        m = row >= sh
                        xs = pltpu.roll(xv, sh, 0)
                        asft = pltpu.roll(av, sh, 0)
                        xv = xv + jnp.where(m, av * xs, 0.0)
                        av = jnp.where(m, av * asft, av)
                    hv = xv + av * carry[j]
                    h_ref[pl.ds(r0, SUBLANES), o:o + LANES] = hv
                    new.append(jnp.broadcast_to(hv[SUBLANES - 1:SUBLANES, :], (SUBLANES, LANES)))
                return tuple(new)

            carry = lax.fori_loop(0, nblk, blk, tuple(carry_ref[:, o:o + LANES] for o in offs), unroll=2)
            for j, o in enumerate(offs):
                carry_ref[:, o:o + LANES] = carry[j]

    spec = pl.BlockSpec((tile, LRU_WIDTH), lambda i: (i, 0))
    return pl.pallas_call(
        body, name="lru_scan_fwd", grid=(n,), in_specs=[spec, spec], out_specs=spec,
        out_shape=jax.ShapeDtypeStruct((rows, LRU_WIDTH), F32),
        scratch_shapes=[pltpu.VMEM((SUBLANES, LRU_WIDTH), F32)],
        compiler_params=pltpu.CompilerParams(dimension_semantics=("arbitrary",), vmem_limit_bytes=VMEM_LIMIT),
    )(a, b)


def _lru_scan_bwd(dh, a, *, rows, tile):
    n = rows // tile
    nblk = tile // SUBLANES
    group = 2

    def body(dh_ref, a_ref, g_ref, cg_ref, ca_ref):
        @pl.when(pl.program_id(0) == 0)
        def _():
            cg_ref[...] = jnp.zeros_like(cg_ref)
            ca_ref[...] = jnp.zeros_like(ca_ref)

        row = lax.broadcasted_iota(jnp.int32, (SUBLANES, LANES), 0)
        for q0 in range(0, LRU_WIDTH // LANES, group):
            offs = [q * LANES for q in range(q0, q0 + group)]

            def blk(t, carry, offs=offs):
                r0 = pl.multiple_of((nblk - 1 - t) * SUBLANES, SUBLANES)
                new = []
                for j, o in enumerate(offs):
                    cg, ca = carry[2 * j], carry[2 * j + 1]
                    araw = a_ref[pl.ds(r0, SUBLANES), o:o + LANES]
                    xv = dh_ref[pl.ds(r0, SUBLANES), o:o + LANES]
                    av = jnp.where(row == SUBLANES - 1, ca, pltpu.roll(araw, SUBLANES - 1, 0))
                    for sh in (1, 2, 4):
                        m = row <= SUBLANES - 1 - sh
                        xs = pltpu.roll(xv, SUBLANES - sh, 0)
                        asft = pltpu.roll(av, SUBLANES - sh, 0)
                        xv = xv + jnp.where(m, av * xs, 0.0)
                        av = jnp.where(m, av * asft, av)
                    gv = xv + av * cg
                    g_ref[pl.ds(r0, SUBLANES), o:o + LANES] = gv
                    new.append(jnp.broadcast_to(gv[0:1, :], (SUBLANES, LANES)))
                    new.append(jnp.broadcast_to(araw[0:1, :], (SUBLANES, LANES)))
                return tuple(new)

            carry0 = tuple(r[:, o:o + LANES] for o in offs for r in (cg_ref, ca_ref))
            carry = lax.fori_loop(0, nblk, blk, carry0, unroll=2)
            for j, o in enumerate(offs):
                cg_ref[:, o:o + LANES] = carry[2 * j]
                ca_ref[:, o:o + LANES] = carry[2 * j + 1]

    spec = pl.BlockSpec((tile, LRU_WIDTH), lambda i: (n - 1 - i, 0))
    return pl.pallas_call(
        body, name="lru_scan_bwd", grid=(n,), in_specs=[spec, spec], out_specs=spec,
        out_shape=jax.ShapeDtypeStruct((rows, LRU_WIDTH), F32),
        scratch_shapes=[pltpu.VMEM((SUBLANES, LRU_WIDTH), F32), pltpu.VMEM((SUBLANES, LRU_WIDTH), F32)],
        compiler_params=pltpu.CompilerParams(dimension_semantics=("arbitrary",), vmem_limit_bytes=VMEM_LIMIT),
    )(dh, a)


def _conv_fwd(i, x, prev, cw, cb):
    prev = prev * (i > 0).astype(F32)
    y = x * cw[3:4, :] + cb
    for s in range(1, CONV_WIDTH):
        y = y + _rows_before(x, prev, s) * cw[3 - s:4 - s, :]
    return y


def _lru_gates(c, wa, ba, wx, bx, lam):
    r = _sigmoid(_heads(_nn, c, wa) + ba)
    ig = _sigmoid(_heads(_nn, c, wx) + bx)
    z = -lam
    sp = jnp.maximum(z, 0.0) + jnp.log(1.0 + jnp.exp(-jnp.abs(z)))
    log_a = -LRU_C * r * sp
    a = jnp.exp(log_a)
    z2 = 2.0 * log_a
    series = -z2 * (1.0 + z2 * (0.5 + z2 * (1.0 / 6.0 + z2 * (1.0 / 24.0 + z2 * (1.0 / 120.0 + z2 / 720.0)))))
    one_minus = jnp.where(z2 > -0.2, series, 1.0 - jnp.exp(z2))
    mult = jnp.sqrt(one_minus)
    return r, ig, sp, a, mult


def _layer_fwd(x, p, w, rows):
    tile = ROW_TILE
    d = D_MODEL

    def f_in(i, xb, g, *ws):
        rstd = lax.rsqrt(jnp.mean(xb * xb, axis=-1, keepdims=True) + NORM_EPS)
        hb = (xb * rstd * g).astype(BF16)
        return tuple(jnp.dot(hb, wc, preferred_element_type=F32) for wc in ws) + (hb,)

    s5x, s5g, lrux, lrug, gs, gl, h = _rows(
        "f_in", f_in, [(x, "row"), (w["g_pre"], "full")] + [(wc, "full") for wc in w["w_in"]],
        [((rows, wd), BF16, "row") for wd in IN_WIDTHS] + [((rows, d), BF16, "row")], rows=rows, tile=tile)

    ys, st = _s5_fwd(s5x, w["bd"], w["cdt"], w["s5_d"], w["scf"], rows=rows, tile=tile)

    def f_s5post(i, ysb, gb, wglu, wbs):
        glv, _ = _gelu_parts(ysb)
        glu = _nn(glv, wglu)
        y2 = glu[:, :S5_WIDTH] * _sigmoid(glu[:, S5_WIDTH:]) * (gb * _sigmoid(gb))
        return (_nn(y2, wbs),)

    (z_s,) = _rows("f_s5post", f_s5post, [(ys, "row"), (s5g, "row"), (w["w_glu"], "full"), (w["w_bs"], "full")],
                   [((rows, d), BF16, "row")], rows=rows, tile=tile)

    def f_gates(i, xb, prev, cw, cb, wa, ba, wx, bx, lam):
        c = _conv_fwd(i, xb, prev, cw, cb)
        _, ig, _, a, mult = _lru_gates(c, wa, ba, wx, bx, lam)
        return a, mult * (ig * c)

    a, b = _rows("f_gates", f_gates,
                 [(lrux, "row"), (lrux, "prev"), (w["conv_w"], "full"), (w["conv_b"], "full"), (w["lru_w_a"], "full"),
                  (w["lru_b_a"], "full"), (w["lru_w_x"], "full"), (w["lru_b_x"], "full"), (w["lru_lambda"], "full")],
                 [((rows, LRU_WIDTH), F32, "row")] * 2, rows=rows, tile=tile)
    hl = _lru_scan_fwd(a, b, rows=rows, tile=tile)

    def f_merge(i, hb, lg, zs, gsb, glb, xb, wbl, wout, gpost):
        z_l = _nn(hb * (lg * _sigmoid(lg)), wbl)
        merged = _sigmoid(gsb) * zs + _sigmoid(glb) * z_l
        mix = _nn(merged, wout)
        rstd = lax.rsqrt(jnp.mean(mix * mix, axis=-1, keepdims=True) + NORM_EPS)
        return xb + mix * rstd * gpost, mix, z_l

    x1, mix, z_l = _rows("f_merge", f_merge,
                         [(hl, "row"), (lrug, "row"), (z_s, "row"), (gs, "row"), (gl, "row"), (x, "row"),
                          (w["w_bl"], "full"), (w["w_out"], "full"), (w["g_post"], "full")],
                         [((rows, d), F32, "row"), ((rows, d), BF16, "row"), ((rows, d), BF16, "row")], rows=rows, tile=tile)

    def f_ple(i, x1b, pb, wple, wpg):
        return (x1b + _nn(pb, wple) * _sigmoid(_nn(x1b, wpg)),)

    (x2,) = _rows("f_ple", f_ple, [(x1, "row"), (p, "row"), (w["w_ple"], "full"), (w["w_ple_gate"], "full")],
                  [((rows, d), F32, "row")], rows=rows, tile=tile)
    saved = dict(x=x, h=h, s5x=s5x, s5g=s5g, lrux=lrux, lrug=lrug, gs=gs, gl=gl, ys=ys, st=st, a=a, hl=hl, z_s=z_s,
                 z_l=z_l, mix=mix, x1=x1, p=p)
    return x2, saved


def _layer_bwd(dx2, sv, w, rows):
    tile = ROW_TILE
    d = D_MODEL
    g = {}

    def b_ple(i, dxb, x1b, pb, wple, wpg):
        pe = _nn(pb, wple)
        sg = _sigmoid(_nn(x1b, wpg))
        dpe = dxb * sg
        dgt = dxb * pe * sg * (1.0 - sg)
        return dxb + _nt(dgt, wpg), _tn(pb, dpe), _tn(x1b, dgt)

    dx1, g["w_ple"], g["w_ple_gate"] = _rows(
        "b_ple", b_ple, [(dx2, "row"), (sv["x1"], "row"), (sv["p"], "row"), (w["w_ple"], "full"), (w["w_ple_gate"], "full")],
        [((rows, d), F32, "row"), ((PLE_DIM, d), F32, "acc"), ((d, d), F32, "acc")], rows=rows, tile=tile)

    def b_merge(i, dxb, mixb, zs, zl, gsb, glb, wout, gpost):
        rstd = lax.rsqrt(jnp.mean(mixb * mixb, axis=-1, keepdims=True) + NORM_EPS)
        nrm = mixb * rstd
        dn = dxb * gpost
        dmix = rstd * (dn - nrm * jnp.mean(dn * nrm, axis=-1, keepdims=True))
        ss, sl = _sigmoid(gsb), _sigmoid(glb)
        merged = ss * zs + sl * zl
        dm = _nt(dmix, wout)
        return (dm * ss, dm * sl, dm * zs * ss * (1.0 - ss), dm * zl * sl * (1.0 - sl),
                _tn(merged, dmix), jnp.sum(dxb * nrm, axis=0, keepdims=True))

    dz_s, dz_l, dgs, dgl, g["w_out"], g["g_post"] = _rows(
        "b_merge", b_merge,
        [(dx1, "row"), (sv["mix"], "row"), (sv["z_s"], "row"), (sv["z_l"], "row"), (sv["gs"], "row"), (sv["gl"], "row"),
         (w["w_out"], "full"), (w["g_post"], "full")],
        [((rows, d), BF16, "row")] * 4 + [((d, d), F32, "acc"), ((1, d), F32, "acc")], rows=rows, tile=tile)

    def b_bl(i, dzl, hb, lg, wbl):
        sl = _sigmoid(lg)
        silu = lg * sl
        dy3 = _nt(dzl, wbl)
        return dy3 * silu, dy3 * hb * sl * (1.0 + lg * (1.0 - sl)), _tn(hb * silu, dzl)

    dh, dlrug, g["w_bl"] = _rows(
        "b_bl", b_bl, [(dz_l, "row"), (sv["hl"], "row"), (sv["lrug"], "row"), (w["w_bl"], "full")],
        [((rows, LRU_WIDTH), F32, "row"), ((rows, LRU_WIDTH), BF16, "row"), ((LRU_WIDTH, d), F32, "acc")], rows=rows, tile=tile)

    gh = _lru_scan_bwd(dh, sv["a"], rows=rows, tile=tile)

    def b_gates(i, ghb, hb, hprev, xb, xprev, cw, cb, wa, ba, wx, bx, lam):
        c = _conv_fwd(i, xb, xprev, cw, cb)
        r, ig, sp, a, mult = _lru_gates(c, wa, ba, wx, bx, lam)
        h_before = _rows_before(hb, hprev * (i > 0).astype(F32), 1)
        da = ghb * h_before
        dmult = ghb * ig * c
        dlog_a = da * a - dmult * a * a / mult
        dpre_r = dlog_a * (-LRU_C) * sp * r * (1.0 - r)
        dpre_i = ghb * mult * c * ig * (1.0 - ig)
        dc = ghb * mult * ig + _heads(_nt, dpre_r, wa) + _heads(_nt, dpre_i, wx)
        dlam = jnp.sum(dlog_a * LRU_C * r, axis=0, keepdims=True) * _sigmoid(-lam)
        return (dc, _heads_tn(c, dpre_r), _heads_tn(c, dpre_i), jnp.sum(dpre_r, axis=0, keepdims=True),
                jnp.sum(dpre_i, axis=0, keepdims=True), dlam)

    hshape = (LRU_HEADS, LRU_HEAD_DIM, LRU_HEAD_DIM)
    dc, g["lru_w_a"], g["lru_w_x"], g["lru_b_a"], g["lru_b_x"], g["lru_lambda"] = _rows(
        "b_gates", b_gates,
        [(gh, "row"), (sv["hl"], "row"), (sv["hl"], "prev"), (sv["lrux"], "row"), (sv["lrux"], "prev"),
         (w["conv_w"], "full"), (w["conv_b"], "full"), (w["lru_w_a"], "full"), (w["lru_b_a"], "full"),
         (w["lru_w_x"], "full"), (w["lru_b_x"], "full"), (w["lru_lambda"], "full")],
        [((rows, LRU_WIDTH), BF16, "row"), (hshape, F32, "acc"), (hshape, F32, "acc")] + [((1, LRU_WIDTH), F32, "acc")] * 3,
        rows=rows, tile=tile)

    n_tiles = rows // min(tile, rows)

    def b_conv(i, dcb, dnext, xb, xprev, cw):
        dnext = dnext * (i < n_tiles - 1).astype(F32)
        xprev = xprev * (i > 0).astype(F32)
        dx = dcb * cw[3:4, :]
        dws = [jnp.sum(dcb * xb, axis=0, keepdims=True)]
        for s in range(1, CONV_WIDTH):
            dx = dx + _rows_after(dcb, dnext, s) * cw[3 - s:4 - s, :]
            dws.append(jnp.sum(dcb * _rows_before(xb, xprev, s), axis=0, keepdims=True))
        return dx, jnp.concatenate(dws[::-1], axis=0), jnp.sum(dcb, axis=0, keepdims=True)

    dlrux, g["conv_w"], g["conv_b"] = _rows(
        "b_conv", b_conv, [(dc, "row"), (dc, "next"), (sv["lrux"], "row"), (sv["lrux"], "prev"), (w["conv_w"], "full")],
        [((rows, LRU_WIDTH), BF16, "row"), ((CONV_WIDTH, LRU_WIDTH), F32, "acc"), ((1, LRU_WIDTH), F32, "acc")],
        rows=rows, tile=tile)

    def b_s5post(i, dzs, ysb, gb, wglu, wbs):
        glv, dgelu = _gelu_parts(ysb)
        glu = _nn(glv, wglu)
        ga, gb2 = glu[:, :S5_WIDTH], glu[:, S5_WIDTH:]
        sb = _sigmoid(gb2)
        sg = _sigmoid(gb)
        silu = gb * sg
        y2 = ga * sb * silu
        dy2 = _nt(dzs, wbs)
        dglu = jnp.concatenate([dy2 * sb * silu, dy2 * ga * silu * sb * (1.0 - sb)], axis=1)
        dg = dy2 * ga * sb * sg * (1.0 + gb * (1.0 - sg))
        return _nt(dglu, wglu) * dgelu, dg, _tn(y2, dzs), _tn(glv, dglu)

    dys, ds5g, g["w_bs"], g["w_glu"] = _rows(
        "b_s5post", b_s5post, [(dz_s, "row"), (sv["ys"], "row"), (sv["s5g"], "row"), (w["w_glu"], "full"), (w["w_bs"], "full")],
        [((rows, S5_WIDTH), F32, "row"), ((rows, S5_WIDTH), BF16, "row"), ((S5_WIDTH, d), F32, "acc"),
         ((S5_WIDTH, 2 * S5_WIDTH), F32, "acc")],
        rows=rows, tile=tile)

    ds5x, g["bd"], g["cdt"], g["s5_d"], g["abar"] = _s5_bwd(dys, sv["st"], sv["s5x"], w["bd"], w["cdt"], w["s5_d"],
                                                            w["scb"], rows=rows, tile=tile)

    dcomps = [ds5x, ds5g, dlrux, dlrug, dgs, dgl]

    def b_in(i, xb, dx1b, gpre, *rest):
        dcs, ws = rest[:6], rest[6:]
        dh = _nt(dcs[0], ws[0])
        for dcv, wc in zip(dcs[1:], ws[1:]):
            dh = dh + _nt(dcv, wc)
        rstd = lax.rsqrt(jnp.mean(xb * xb, axis=-1, keepdims=True) + NORM_EPS)
        nrm = xb * rstd
        dn = dh * gpre
        dx = rstd * (dn - nrm * jnp.mean(dn * nrm, axis=-1, keepdims=True))
        return dx1b + dx, jnp.sum(dh * nrm, axis=0, keepdims=True)

    dx, g["g_pre"] = _rows(
        "b_in", b_in, [(sv["x"], "row"), (dx1, "row"), (w["g_pre"], "full")] + [(dcv, "raw") for dcv in dcomps]
        + [(wc, "full") for wc in w["w_in"]],
        [((rows, d), F32, "row"), ((1, d), F32, "acc")], rows=rows, tile=tile)

    def b_win(i, hb, dcv):
        return (_tn(hb, dcv),)

    g["w_in"] = jnp.concatenate(
        [_rows("b_win", b_win, [(sv["h"], "raw"), (dcv, "raw")], [((d, dcv.shape[1]), F32, "acc")], rows=rows, tile=4 * tile)[0]
         for dcv in dcomps], axis=1)
    return dx, g


SMALL = ("g_pre", "s5_a_re", "s5_a_im", "s5_log_dt", "s5_b_re", "s5_b_im", "s5_c_re", "s5_c_im", "s5_d", "conv_b",
         "lru_w_a", "lru_b_a", "lru_w_x", "lru_b_x", "lru_lambda", "g_post")
BIG = ("w_in", "w_glu", "w_bs", "conv_w", "w_bl", "w_out", "w_ple", "w_ple_gate")
BIG_SHARD_AXIS = {"w_in": 1, "w_glu": 1, "w_bs": 1, "conv_w": 1, "w_bl": 0, "w_out": 0, "w_ple": 1, "w_ple_gate": 0}


def _bcast_groups(v):
    return jnp.broadcast_to(v[:, None, :], (S5_GROUPS, S5_GROUP, S5_STATE)).reshape(S5_WIDTH, S5_STATE)


def _s5_prep_inputs(wl):
    ldt = jnp.broadcast_to(wl["s5_log_dt"][:, None], (S5_GROUPS, S5_STATE))
    gcn = lambda b: jnp.transpose(b, (0, 2, 1)).reshape(S5_WIDTH, S5_STATE)
    return (_bcast_groups(wl["s5_a_re"]), _bcast_groups(wl["s5_a_im"]), _bcast_groups(ldt), gcn(wl["s5_b_re"]),
            gcn(wl["s5_b_im"]))


def _layer_weights(wl):
    w = {}
    offs = [0]
    for wd in IN_WIDTHS:
        offs.append(offs[-1] + wd)
    w["w_in"] = [wl["w_in"][:, offs[k]:offs[k + 1]] for k in range(6)]
    for k in ("w_glu", "w_bs", "w_bl", "w_out", "w_ple", "w_ple_gate"):
        w[k] = wl[k]
    w["conv_w"] = wl["conv_w"]
    for k in ("g_pre", "g_post", "s5_d", "conv_b", "lru_b_a", "lru_b_x", "lru_lambda"):
        w[k] = wl[k].reshape(1, -1)
    w["lru_w_a"] = wl["lru_w_a"].astype(BF16)
    w["lru_w_x"] = wl["lru_w_x"].astype(BF16)
    prep_in = _s5_prep_inputs(wl)
    abr, abi, bbr, bbi = _s5_prep(*prep_in)
    w["prep_in"] = prep_in
    shape3 = (S5_GROUPS, S5_GROUP, S5_STATE)
    w["bd"] = _s5_block_diag([bbr.reshape(shape3), bbi.reshape(shape3)]).astype(BF16)
    w["cdt"] = _s5_block_diag([wl["s5_c_re"], -wl["s5_c_im"]]).astype(BF16)
    abr_s = abr.reshape(shape3)[:, 0, :]
    abi_s = abi.reshape(shape3)[:, 0, :]
    w["scf"], w["scb"] = _s5_consts(_cplx_to_lanes(abr_s), _cplx_to_lanes(abi_s))
    return w


def _layer_param_grads(g, w, wl):
    out = {}
    shape3 = (S5_GROUPS, S5_GROUP, S5_STATE)
    dbr, dbi = _s5_block_diag_extract(g["bd"])
    dcr, dci = _s5_block_diag_extract(g["cdt"])
    out["s5_c_re"], out["s5_c_im"] = dcr, -dci
    zeros = jnp.zeros(shape3, F32)
    dar = zeros.at[:, 0, :].set(g["abar"][0].reshape(S5_GROUPS, S5_STATE)).reshape(S5_WIDTH, S5_STATE)
    dai = zeros.at[:, 0, :].set(g["abar"][1].reshape(S5_GROUPS, S5_STATE)).reshape(S5_WIDTH, S5_STATE)
    cts = (dar, dai, dbr.reshape(S5_WIDTH, S5_STATE), dbi.reshape(S5_WIDTH, S5_STATE))
    d_are, d_aim, d_ldt, d_bre, d_bim = _s5_prep_bwd(*w["prep_in"], cts)
    out["s5_a_re"] = d_are.reshape(shape3).sum(axis=1)
    out["s5_a_im"] = d_aim.reshape(shape3).sum(axis=1)
    out["s5_log_dt"] = d_ldt.reshape(shape3).sum(axis=(1, 2))
    out["s5_b_re"] = jnp.transpose(d_bre.reshape(shape3), (0, 2, 1))
    out["s5_b_im"] = jnp.transpose(d_bim.reshape(shape3), (0, 2, 1))
    out["s5_d"] = g["s5_d"].reshape(-1)
    for k in ("g_pre", "g_post", "conv_b", "lru_b_a", "lru_b_x", "lru_lambda"):
        out[k] = g[k].reshape(-1)
    for k in ("lru_w_a", "lru_w_x", "conv_w", "w_in", "w_glu", "w_bs", "w_bl", "w_out", "w_ple", "w_ple_gate"):
        out[k] = g[k]
    return out


def _local_step(x, p, layers, target):
    rows = x.shape[0]
    ws = [_layer_weights(wl) for wl in layers]
    saved = []
    for i in range(DEPTH):
        x, sv = _layer_fwd(x, p[i], ws[i], rows)
        saved.append(sv)

    def f_loss(i, yb, tb):
        e = yb - tb
        return e * (1.0 / D_MODEL), jnp.sum(jnp.sum(e * e, axis=0, keepdims=True), axis=1, keepdims=True)

    dx, sq = _rows("f_loss", f_loss, [(x, "row"), (target, "row")],
                   [((rows, D_MODEL), F32, "row"), ((1, 1), F32, "acc")], rows=rows, tile=ROW_TILE)
    loss = sq[0, 0] * (0.5 / D_MODEL)
    grads = [None] * DEPTH
    for i in reversed(range(DEPTH)):
        dx, g = _layer_bwd(dx, saved[i], ws[i], rows)
        grads[i] = _layer_param_grads(g, ws[i], layers[i])
    return loss, dx, grads


def _place():
    return lax.axis_index("x"), lax.axis_index("y"), lax.axis_index("c")


def _other_chips(x, y):
    return [(1 - x, y), (x, 1 - y), (1 - x, 1 - y)]


def _any_spec():
    return pl.BlockSpec(memory_space=pl.ANY)


ICI_PIECES = 1
D2D_PIECES = 1
D2D_SOLO_PIECES = 1


def _pieces(rows, k):
    step = rows // k
    assert step * k == rows and step % 16 == 0, (rows, k)
    return [(q * step, step) for q in range(k)]


def _gather_chips(name, v, via_sibling):
    rows = v.shape[0]
    half = rows // 2

    n_sent = half if via_sibling else rows

    def body(v_ref, out_ref, send_sems, recv_sems):
        x, y, c = _place()
        me = 2 * x + y
        chips = _other_chips(x, y)
        slots = [2 * cx + cy for cx, cy in chips]

        def part(slot, hc, o=0, s=n_sent):
            return out_ref.at[slot, pl.ds(hc * half + o, s), :] if via_sibling else out_ref.at[slot, pl.ds(o, s), :]

        def own(o=0, s=n_sent):
            return v_ref.at[pl.ds(c * half + o, s), :] if via_sibling else v_ref.at[pl.ds(o, s), :]

        def copy(k, src, dst, to):
            return pltpu.make_async_remote_copy(src_ref=src, dst_ref=dst, send_sem=send_sems.at[k], recv_sem=recv_sems.at[k],
                                                device_id=to, device_id_type=MESH)

        for k in range(3):
            for o, s in _pieces(n_sent, ICI_PIECES):
                copy(k, own(o, s), part(me, c, o, s), (*chips[k], c)).start()
        for k in range(3):
            copy(k, own(), part(slots[k], c), (*chips[k], c)).wait_recv()
            if via_sibling:
                for o, s in _pieces(n_sent, D2D_PIECES):
                    copy(3 + k, part(slots[k], c, o, s), part(slots[k], c, o, s), (x, y, 1 - c)).start()
        if via_sibling:
            for k in range(3):
                copy(3 + k, own(), part(slots[k], 1 - c), (x, y, 1 - c)).wait_recv()
        for k in range(6 if via_sibling else 3):
            copy(k, own(), part(me, c), (x, y, 1 - c)).wait_send()

    n_sem = 6 if via_sibling else 3
    others = pl.pallas_call(
        body, name=name, out_shape=jax.ShapeDtypeStruct((4,) + v.shape, v.dtype),
        in_specs=[_any_spec()], out_specs=_any_spec(),
        scratch_shapes=[pltpu.SemaphoreType.DMA((n_sem,)), pltpu.SemaphoreType.DMA((n_sem,))],
    )(v)
    return lax.dynamic_update_slice(others, v[None], (2 * lax.axis_index("x") + lax.axis_index("y"), 0, 0))


def _rs_sibling(gr):
    half = gr.shape[1] // 2

    def body(g_ref, got_ref, send_sem, recv_sem):
        x, y, c = _place()

        def give(src, dst):
            return pltpu.make_async_remote_copy(src_ref=src, dst_ref=dst, send_sem=send_sem, recv_sem=recv_sem,
                                                device_id=(x, y, 1 - c), device_id_type=MESH)

        copy = give(g_ref.at[:, pl.ds((1 - c) * half, half), :], got_ref)
        copy.start()
        copy.wait()

    return pl.pallas_call(
        body, name="rs_sibling", out_shape=jax.ShapeDtypeStruct((4, half, LANES), F32),
        in_specs=[_any_spec()], out_specs=_any_spec(),
        scratch_shapes=[pltpu.SemaphoreType.DMA, pltpu.SemaphoreType.DMA],
    )(gr)


def _rs_chips(a16):
    half = a16.shape[1]

    def body(a16_ref, got_ref, send_sems, recv_sems):
        x, y, c = _place()
        chips = _other_chips(x, y)

        def copy(k, o=0, s=half):
            cx, cy = chips[k]
            return pltpu.make_async_remote_copy(
                src_ref=a16_ref.at[2 * cx + cy, pl.ds(o, s), :], dst_ref=got_ref.at[k, pl.ds(o, s), :],
                send_sem=send_sems.at[k], recv_sem=recv_sems.at[k], device_id=(cx, cy, c), device_id_type=MESH)

        for k in range(3):
            for o, s in _pieces(half, ICI_PIECES):
                copy(k, o, s).start()
        for k in range(3):
            copy(k).wait()

    return pl.pallas_call(
        body, name="rs_chips", out_shape=jax.ShapeDtypeStruct((3, half, LANES), BF16),
        in_specs=[_any_spec()], out_specs=_any_spec(),
        scratch_shapes=[pltpu.SemaphoreType.DMA((3,)), pltpu.SemaphoreType.DMA((3,))],
    )(a16)


def _swap_halves(v):
    def body(v_ref, out_ref, send_sem, recv_sem):
        x, y, c = _place()

        def give(hc):
            return pltpu.make_async_remote_copy(src_ref=v_ref, dst_ref=out_ref.at[hc], send_sem=send_sem, recv_sem=recv_sem,
                                                device_id=(x, y, 1 - c), device_id_type=MESH)

        give(c).start()
        give(c).wait_send()
        give(1 - c).wait_recv()

    other = pl.pallas_call(
        body, name="swap_halves", out_shape=jax.ShapeDtypeStruct((2,) + v.shape, v.dtype),
        in_specs=[_any_spec()], out_specs=_any_spec(),
        scratch_shapes=[pltpu.SemaphoreType.DMA, pltpu.SemaphoreType.DMA],
    )(v)
    both = lax.dynamic_update_slice(other, v[None], (lax.axis_index("c"), 0, 0))
    return both.reshape(2 * v.shape[0], v.shape[1])


WIDE = 1024
PACK_TILE = 3072
GRAD_ROWS_UNIT = 2 * PACK_TILE


def _pack(parts, rows_unit, dtype):
    flat = jnp.concatenate([q.reshape(-1).astype(dtype) for q in parts])
    unit = rows_unit * LANES
    total = -(-flat.shape[0] // unit) * unit
    return jnp.pad(flat, (0, total - flat.shape[0])).reshape(-1, LANES)


def _unpack(flat, shapes):
    out, off = [], 0
    for s in shapes:
        n = 1
        for q in s:
            n *= q
        out.append(flat[off:off + n].reshape(s))
        off += n
    return out


def _to_slots(name, full):
    dp, r, c = full.shape
    if BIG_SHARD_AXIS[name] == 1:
        return jnp.transpose(full.reshape(dp, r, 4, c // 4), (2, 0, 1, 3)).reshape(4, -1)
    return jnp.transpose(full.reshape(dp, 4, r // 4, c), (1, 0, 2, 3)).reshape(4, -1)


def _from_slots(name, slots, shard_shape):
    dp, r, c = shard_shape
    v = slots.reshape(4, dp, r, c)
    if BIG_SHARD_AXIS[name] == 1:
        return jnp.transpose(v, (1, 2, 0, 3)).reshape(dp, r, 4 * c)
    return jnp.transpose(v, (1, 0, 2, 3)).reshape(dp, 4 * r, c)


def _adamw(name, w, g, m, v, tile):
    def fn(i, wb, gb, mb, vb):
        m2 = ADAM_B1 * mb + (1.0 - ADAM_B1) * gb
        v2 = ADAM_B2 * vb + (1.0 - ADAM_B2) * (gb * gb)
        m_hat = m2 / (1.0 - ADAM_B1 ** ADAM_STEP)
        v_hat = v2 / (1.0 - ADAM_B2 ** ADAM_STEP)
        return -ADAM_LR * (m_hat / (jnp.sqrt(v_hat) + ADAM_EPS) + ADAM_WD * wb), m2, v2

    return _rows(name, fn, [(w, "row"), (g, "row"), (m, "row"), (v, "row")], [(w.shape, F32, "row")] * 3,
                 rows=w.shape[0], tile=tile)


def _as_2d(a):
    return a.reshape(-1, a.shape[-1])


def _adam_tile(rows):
    for t in (256, 184, 128, 64, 32, 16, 8):
        if rows % t == 0:
            return t
    return rows


def kernel(x, p, g_pre, w_in, s5_a_re, s5_a_im, s5_log_dt, s5_b_re, s5_b_im, s5_c_re, s5_c_im, s5_d, w_glu, w_bs, conv_w, conv_b, lru_w_a, lru_b_a, lru_w_x, lru_b_x, lru_lambda, w_bl, w_out, g_post, w_ple, w_ple_gate, loss_target, m_g_pre, m_w_in, m_s5_a_re, m_s5_a_im, m_s5_log_dt, m_s5_b_re, m_s5_b_im, m_s5_c_re, m_s5_c_im, m_s5_d, m_w_glu, m_w_bs, m_conv_w, m_conv_b, m_lru_w_a, m_lru_b_a, m_lru_w_x, m_lru_b_x, m_lru_lambda, m_w_bl, m_w_out, m_g_post, m_w_ple, m_w_ple_gate, v_g_pre, v_w_in, v_s5_a_re, v_s5_a_im, v_s5_log_dt, v_s5_b_re, v_s5_b_im, v_s5_c_re, v_s5_c_im, v_s5_d, v_w_glu, v_w_bs, v_conv_w, v_conv_b, v_lru_w_a, v_lru_b_a, v_lru_w_x, v_lru_b_x, v_lru_lambda, v_w_bl, v_w_out, v_g_post, v_w_ple, v_w_ple_gate):
    wts = dict(g_pre=g_pre, w_in=w_in, s5_a_re=s5_a_re, s5_a_im=s5_a_im, s5_log_dt=s5_log_dt, s5_b_re=s5_b_re,
               s5_b_im=s5_b_im, s5_c_re=s5_c_re, s5_c_im=s5_c_im, s5_d=s5_d, w_glu=w_glu, w_bs=w_bs, conv_w=conv_w,
               conv_b=conv_b, lru_w_a=lru_w_a, lru_b_a=lru_b_a, lru_w_x=lru_w_x, lru_b_x=lru_b_x, lru_lambda=lru_lambda,
               w_bl=w_bl, w_out=w_out, g_post=g_post, w_ple=w_ple, w_ple_gate=w_ple_gate)
    mom1 = dict(g_pre=m_g_pre, w_in=m_w_in, s5_a_re=m_s5_a_re, s5_a_im=m_s5_a_im, s5_log_dt=m_s5_log_dt, s5_b_re=m_s5_b_re,
                s5_b_im=m_s5_b_im, s5_c_re=m_s5_c_re, s5_c_im=m_s5_c_im, s5_d=m_s5_d, w_glu=m_w_glu, w_bs=m_w_bs,
                conv_w=m_conv_w, conv_b=m_conv_b, lru_w_a=m_lru_w_a, lru_b_a=m_lru_b_a, lru_w_x=m_lru_w_x, lru_b_x=m_lru_b_x,
                lru_lambda=m_lru_lambda, w_bl=m_w_bl, w_out=m_w_out, g_post=m_g_post, w_ple=m_w_ple, w_ple_gate=m_w_ple_gate)
    mom2 = dict(g_pre=v_g_pre, w_in=v_w_in, s5_a_re=v_s5_a_re, s5_a_im=v_s5_a_im, s5_log_dt=v_s5_log_dt, s5_b_re=v_s5_b_re,
                s5_b_im=v_s5_b_im, s5_c_re=v_s5_c_re, s5_c_im=v_s5_c_im, s5_d=v_s5_d, w_glu=v_w_glu, w_bs=v_w_bs,
                conv_w=v_conv_w, conv_b=v_conv_b, lru_w_a=v_lru_w_a, lru_b_a=v_lru_b_a, lru_w_x=v_lru_w_x, lru_b_x=v_lru_b_x,
                lru_lambda=v_lru_lambda, w_bl=v_w_bl, w_out=v_w_out, g_post=v_g_post, w_ple=v_w_ple, w_ple_gate=v_w_ple_gate)
    names = list(wts)

    def wire(name):
        return lax.bitcast_convert_type(wts[name], BF16) if name == "conv_w" else wts[name].astype(BF16)

    wire_shapes = [wire(k).shape for k in BIG]
    gathered = _gather_chips("gather_weights", _pack([wire(k) for k in BIG], 128, BF16), via_sibling=True)
    per_chip = [_unpack(gathered[j].reshape(-1), wire_shapes) for j in range(4)]
    whole = {}
    for idx, k in enumerate(BIG):
        v = jnp.stack([per_chip[j][idx] for j in range(4)])
        if k == "conv_w":
            v = lax.bitcast_convert_type(v, F32)
        whole[k] = _from_slots(k, v.reshape(4, -1), wts[k].shape)
    layers = []
    for i in range(DEPTH):
        wl = {k: whole[k][i] for k in BIG}
        wl.update({k: wts[k][i] for k in SMALL})
        layers.append(wl)

    loss, grad_x, grads = _local_step(x[0], p[:, 0], layers, loss_target[0])
    loss = lax.psum(loss, ("x", "y", "c"))

    big_slots = jnp.concatenate([_to_slots(k, jnp.stack([grads[i][k] for i in range(DEPTH)])) for k in BIG], axis=1)
    small_shapes = [wts[k].shape for k in SMALL]
    small_flat = jnp.concatenate([jnp.stack([grads[i][k] for i in range(DEPTH)]).reshape(-1) for k in SMALL])
    n_small = small_flat.shape[0]
    small_q = -(-n_small // (4 * 8 * WIDE)) * 8 * WIDE
    small_slots = jnp.pad(small_flat, (0, 4 * small_q - n_small)).reshape(4, small_q)
    n_big = big_slots.shape[1]
    n_big_pad = -(-n_big // (8 * WIDE)) * 8 * WIDE
    n_slot = n_big_pad + small_q
    unit = GRAD_ROWS_UNIT * LANES
    n_slot_pad = -(-n_slot // unit) * unit
    gr = jnp.concatenate([jnp.pad(big_slots, ((0, 0), (0, n_big_pad - n_big))), small_slots,
                          jnp.zeros((4, n_slot_pad - n_slot), F32)], axis=1).reshape(4, -1, LANES)
    got = _rs_sibling(gr)
    half = got.shape[1]
    mine = lax.dynamic_slice_in_dim(gr, lax.axis_index("c") * half, half, axis=1)
    rows2d = lambda a: a.reshape(-1, LANES)

    def f_add1(i, a, b):
        s = a + b
        return s, s

    a32, a16 = _rows("rs_add1", f_add1, [(rows2d(mine), "row"), (rows2d(got), "row")],
                     [((4 * half, LANES), F32, "row"), ((4 * half, LANES), BF16, "row")], rows=4 * half, tile=PACK_TILE)
    got3 = _rs_chips(a16.reshape(4, half, LANES))
    own = lax.dynamic_index_in_dim(a32.reshape(4, half, LANES), 2 * lax.axis_index("x") + lax.axis_index("y"), 0, keepdims=False)

    def f_add2(i, o, g0, g1, g2):
        return (((o + g0.astype(F32)) + g1.astype(F32)) + g2.astype(F32),)

    (red_half,) = _rows("rs_add2", f_add2, [(own, "row")] + [(got3[k], "row") for k in range(3)],
                        [((half, LANES), F32, "row")], rows=half, tile=PACK_TILE)
    red = _swap_halves(red_half).reshape(-1)
    small_red = _gather_chips("gather_small", red[n_big_pad:n_big_pad + small_q].reshape(-1, LANES), via_sibling=False)
    small_red = small_red.reshape(-1)[:n_small]

    big_shapes = [wts[k].shape for k in BIG]
    grad_out = dict(zip(BIG, _unpack(red[:n_big], big_shapes)))
    grad_out.update(zip(SMALL, _unpack(small_red, small_shapes)))
    delta, new_m, new_v = {}, {}, {}
    for k in BIG:
        w2 = _as_2d(wts[k])
        res = _adamw("adamw_" + k, w2, _as_2d(grad_out[k]), _as_2d(mom1[k]), _as_2d(mom2[k]), _adam_tile(w2.shape[0]))
        delta[k], new_m[k], new_v[k] = [r.reshape(wts[k].shape) for r in res]
    pack_small = lambda d: jnp.pad(jnp.concatenate([d[k].reshape(-1) for k in SMALL]), (0, 4 * small_q - n_small)).reshape(-1, WIDE)
    res = _adamw("adamw_small", pack_small(wts), pack_small(grad_out), pack_small(mom1), pack_small(mom2),
                 _adam_tile(4 * small_q // WIDE))
    for d, r in zip((delta, new_m, new_v), res):
        d.update(zip(SMALL, _unpack(r.reshape(-1), small_shapes)))
    return (loss, grad_x[None], *[grad_out[k] for k in names], *[delta[k] for k in names],
            *[new_m[k] for k in names], *[new_v[k] for k in names])
```

```python
import jax
import jax.numpy as jnp
from jax import lax
from jax.experimental import pallas as pl
from jax.experimental.pallas import tpu as pltpu

F32 = jnp.float32
BF16 = jnp.bfloat16
MESH = pl.DeviceIdType.MESH

DEPTH = 2
D_MODEL = 1024
NORM_EPS = 1e-6
S5_WIDTH = 512
S5_GROUPS = 32
S5_GROUP = 16
S5_STATE = 64
LRU_WIDTH = 1280
LRU_HEADS = 10
LRU_HEAD_DIM = 128
LRU_C = 8.0
CONV_WIDTH = 4
PLE_DIM = 256
IN_WIDTHS = (S5_WIDTH, S5_WIDTH, LRU_WIDTH, LRU_WIDTH, D_MODEL, D_MODEL)
ADAM_LR = 0.001
ADAM_B1 = 0.9
ADAM_B2 = 0.999
ADAM_EPS = 1e-08
ADAM_WD = 0.01
ADAM_STEP = 10

SUBLANES = 8
LANES = 128
S5_HALF_IN = S5_WIDTH // 2
S5_CPLX = S5_GROUPS * S5_STATE
S5_HALF_CPLX = S5_CPLX // 2
S5_LANES = 2 * S5_CPLX
VMEM_LIMIT = 48 * 2 ** 20
ROW_TILE = 256


def _sigmoid(x):
    return 1.0 / (1.0 + jnp.exp(-x))


def _gelu_parts(x):
    k = 0.7978845608028654
    t = jnp.tanh(k * (x + 0.044715 * x * x * x))
    val = 0.5 * x * (1.0 + t)
    grad = 0.5 * (1.0 + t) + 0.5 * x * (1.0 - t * t) * k * (1.0 + 3.0 * 0.044715 * x * x)
    return val, grad


def _nn(a, w):
    return jnp.dot(a.astype(BF16), w.astype(BF16), preferred_element_type=F32)


def _nt(a, w):
    return lax.dot_general(a.astype(BF16), w.astype(BF16), (((1,), (1,)), ((), ())), preferred_element_type=F32)


def _tn(a, b):
    return lax.dot_general(a.astype(BF16), b.astype(BF16), (((0,), (0,)), ((), ())), preferred_element_type=F32)


def _heads(op, a, w):
    d = LRU_HEAD_DIM
    return jnp.concatenate([op(a[:, h * d:(h + 1) * d], w[h]) for h in range(LRU_HEADS)], axis=1)


def _heads_tn(a, b):
    d = LRU_HEAD_DIM
    return jnp.stack([_tn(a[:, h * d:(h + 1) * d], b[:, h * d:(h + 1) * d]) for h in range(LRU_HEADS)], axis=0)


def _rows_before(x, halo, s):
    main = pltpu.roll(x, s, 0)
    head = pltpu.roll(jnp.concatenate([halo, x[0:SUBLANES]], axis=0), s, 0)[SUBLANES:2 * SUBLANES]
    return jnp.concatenate([head, main[SUBLANES:]], axis=0)


def _rows_after(x, halo, s):
    n = x.shape[0]
    main = pltpu.roll(x, n - s, 0)
    tail = pltpu.roll(jnp.concatenate([x[n - SUBLANES:], halo], axis=0), 2 * SUBLANES - s, 0)[0:SUBLANES]
    return jnp.concatenate([main[:n - SUBLANES], tail], axis=0)


def _rows(name, fn, ins, outs, *, rows, tile):
    tile = min(tile, rows)
    n = rows // tile
    assert n * tile == rows, (name, rows, tile)
    in_specs = []
    for arr, kind in ins:
        halo = SUBLANES * (4 // arr.dtype.itemsize)
        per, last = tile // halo, rows // halo - 1
        if kind in ("row", "raw"):
            in_specs.append(pl.BlockSpec((tile, arr.shape[1]), lambda i: (i, 0)))
        elif kind == "prev":
            in_specs.append(pl.BlockSpec((halo, arr.shape[1]), lambda i, per=per: (jnp.maximum(i * per - 1, 0), 0)))
        elif kind == "next":
            in_specs.append(pl.BlockSpec((halo, arr.shape[1]),
                                         lambda i, per=per, last=last: (jnp.minimum((i + 1) * per, last), 0)))
        else:
            in_specs.append(pl.BlockSpec(arr.shape, lambda i, nd=arr.ndim: (0,) * nd))
    out_shape, out_specs = [], []
    for shape, dtype, kind in outs:
        out_shape.append(jax.ShapeDtypeStruct(shape, dtype))
        if kind == "row":
            out_specs.append(pl.BlockSpec((tile, shape[1]), lambda i: (i, 0)))
        else:
            out_specs.append(pl.BlockSpec(shape, lambda i, nd=len(shape): (0,) * nd))
    n_in = len(ins)

    def load(ref, kind):
        v = ref[...]
        if kind in ("row", "prev", "next"):
            v = v.astype(F32)
        if kind == "prev":
            v = v[v.shape[0] - SUBLANES:]
        if kind == "next":
            v = v[:SUBLANES]
        return v

    def body(*refs):
        i = pl.program_id(0)
        vals = fn(i, *[load(r, kind) for r, (_, kind) in zip(refs[:n_in], ins)])
        assert len(vals) == len(outs), name
        for r, v, (_, _, kind) in zip(refs[n_in:], vals, outs):
            if kind == "row":
                r[...] = v.astype(r.dtype)
            else:
                @pl.when(i == 0)
                def _():
                    r[...] = jnp.zeros_like(r)

                r[...] += v.astype(r.dtype)

    return pl.pallas_call(
        body, name=name, grid=(n,), in_specs=in_specs, out_specs=out_specs, out_shape=out_shape,
        compiler_params=pltpu.CompilerParams(dimension_semantics=("arbitrary",), vmem_limit_bytes=VMEM_LIMIT),
    )(*[a for a, _ in ins])


def _s5_discretise(are, aim, ldt, bre, bim):
    dt = jnp.exp(ldt)
    er = jnp.exp(are * dt)
    abr = er * jnp.cos(aim * dt)
    abi = er * jnp.sin(aim * dt)
    den = are * are + aim * aim
    zr = ((abr - 1.0) * are + abi * aim) / den
    zi = (abi * are - (abr - 1.0) * aim) / den
    return abr, abi, zr * bre - zi * bim, zr * bim + zi * bre


def _s5_prep(are, aim, ldt, bre, bim):
    def body(a, b, c, d, e, o0, o1, o2, o3):
        r = _s5_discretise(a[...], b[...], c[...], d[...], e[...])
        o0[...], o1[...], o2[...], o3[...] = r

    sd = jax.ShapeDtypeStruct(are.shape, F32)
    return pl.pallas_call(body, name="s5_prep", out_shape=[sd] * 4)(are, aim, ldt, bre, bim)


def _s5_prep_bwd(are, aim, ldt, bre, bim, cts):
    def body(a, b, c, d, e, c0, c1, c2, c3, o0, o1, o2, o3, o4):
        _, vjp = jax.vjp(_s5_discretise, a[...], b[...], c[...], d[...], e[...])
        r = vjp((c0[...], c1[...], c2[...], c3[...]))
        o0[...], o1[...], o2[...], o3[...], o4[...] = r

    sd = jax.ShapeDtypeStruct(are.shape, F32)
    return pl.pallas_call(body, name="s5_prep_bwd", out_shape=[sd] * 5)(are, aim, ldt, bre, bim, *cts)


def _s5_consts(abr, abi, seg):
    shape = (SUBLANES, S5_CPLX)
    assert seg & (seg - 1) == 0 and seg % SUBLANES == 0, seg

    def body(ar_ref, ai_ref, f_ref, b_ref):
        def cmul(p, q):
            return (p[0] * q[0] - p[1] * q[1], p[0] * q[1] + p[1] * q[0])

        row = lax.broadcasted_iota(jnp.int32, shape, 0)
        a1 = (jnp.broadcast_to(ar_ref[...], shape), jnp.broadcast_to(ai_ref[...], shape))
        squares = [a1]
        while 1 << (len(squares) - 1) < 4 * seg:
            squares.append(cmul(squares[-1], squares[-1]))
        nb = seg.bit_length() - 1
        fwd, rev = [], []
        for k, a in ((1, squares[nb]), (2, squares[nb + 1]), (4, squares[nb + 2])):
            fwd += [jnp.where(row >= k, a[0], 0.0), jnp.where(row >= k, a[1], 0.0)]
            rev += [jnp.where(row <= 7 - k, a[0], 0.0), jnp.where(row <= 7 - k, -a[1], 0.0)]
        fwd += [a1[0], a1[1]]
        rev += [a1[0], -a1[1]]
        e = lax.broadcasted_iota(jnp.int32, (seg, S5_CPLX), 0) + 1
        wide = lambda v: jnp.broadcast_to(v[0:1, :], (seg, S5_CPLX))
        pr, pi = jnp.ones((seg, S5_CPLX), F32), jnp.zeros((seg, S5_CPLX), F32)
        for b in range(nb + 1):
            sr, si = wide(squares[b][0]), wide(squares[b][1])
            bit = ((e >> b) & 1) == 1
            pr, pi = jnp.where(bit, pr * sr - pi * si, pr), jnp.where(bit, pr * si + pi * sr, pi)
        f_ref[...] = jnp.concatenate(fwd + [pr, pi], axis=0)
        b_ref[...] = jnp.concatenate(rev + [pr, -pi], axis=0)

    sd = jax.ShapeDtypeStruct((8 * SUBLANES + 2 * seg, S5_CPLX), F32)
    return pl.pallas_call(body, name="s5_consts", out_shape=[sd, sd])(abr, abi)


S5_TILES = S5_LANES // LANES
S5_HALF_TILES = S5_TILES // 2


def _s5_tile_index(q):
    re = (q // 8) * S5_HALF_TILES + (q % 8)
    return re, re + S5_HALF_TILES // 2


def _lanes_of(ref, first, count):
    return jnp.concatenate([ref[j] for j in range(first, first + count)], axis=1)


def _to_lane_tiles(ref, first, value):
    for j in range(value.shape[1] // LANES):
        ref[first + j] = value[:, j * LANES:(j + 1) * LANES]


def _time_perm(tile, transpose=False):
    seg = tile // SUBLANES
    rho = lax.broadcasted_iota(jnp.int32, (tile, tile), 1 if transpose else 0)
    t = lax.broadcasted_iota(jnp.int32, (tile, tile), 0 if transpose else 1)
    return (t == (rho & (SUBLANES - 1)) * seg + (rho >> 3)).astype(BF16)


def _reorder(perm, x):
    out = None
    for _ in range(1 if x.dtype == BF16 else 3):
        piece = x.astype(BF16)
        part = jnp.dot(perm, piece, preferred_element_type=F32)
        out = part if out is None else out + part
        x = x - piece.astype(x.dtype)
    return out


def _s5_scan(s_ref, sc_ref, carry_ref, tile, reverse):
    seg = tile // SUBLANES
    group = 4
    edge = 0 if reverse else SUBLANES - 1
    row = lax.broadcasted_iota(jnp.int32, (SUBLANES, LANES), 0)
    order = range(seg - 1, -1, -1) if reverse else range(seg)
    rows_of = lambda k: pl.ds(k * SUBLANES, SUBLANES)
    base = 8 * SUBLANES

    for q0 in range(0, S5_CPLX // LANES, group):
        qs = list(range(q0, q0 + group))
        tiles = [_s5_tile_index(q) for q in qs]
        cst = lambda k, q: sc_ref[k * SUBLANES:(k + 1) * SUBLANES, q * LANES:(q + 1) * LANES]
        state = [(jnp.zeros((SUBLANES, LANES), F32), jnp.zeros((SUBLANES, LANES), F32)) for _ in qs]
        mult = [(cst(6, q), cst(7, q)) for q in qs]
        for k in order:
            for j, (re, im) in enumerate(tiles):
                ar, ai = mult[j]
                xr, xi = state[j]
                nr = ar * xr - ai * xi + s_ref[re, rows_of(k), :]
                ni = ar * xi + ai * xr + s_ref[im, rows_of(k), :]
                s_ref[re, rows_of(k), :] = nr
                s_ref[im, rows_of(k), :] = ni
                state[j] = (nr, ni)
        start = []
        for j, (q, (re, im)) in enumerate(zip(qs, tiles)):
            er, ei = state[j]
            shift1 = SUBLANES - 1 if reverse else 1
            dr = jnp.where(row == SUBLANES - 1 - edge, carry_ref[re], pltpu.roll(er, shift1, 0))
            di = jnp.where(row == SUBLANES - 1 - edge, carry_ref[im], pltpu.roll(ei, shift1, 0))
            for c, sh in ((0, 1), (2, 2), (4, 4)):
                shift = SUBLANES - sh if reverse else sh
                ar, ai = cst(c, q), cst(c + 1, q)
                sr, si = pltpu.roll(dr, shift, 0), pltpu.roll(di, shift, 0)
                dr, di = dr + ar * sr - ai * si, di + ar * si + ai * sr
            start.append((dr, di))
        for k in order:
            t = seg - 1 - k if reverse else k
            for j, (q, (re, im)) in enumerate(zip(qs, tiles)):
                lanes = slice(q * LANES, (q + 1) * LANES)
                pr = jnp.broadcast_to(sc_ref[base + t:base + t + 1, lanes], (SUBLANES, LANES))
                pi = jnp.broadcast_to(sc_ref[base + seg + t:base + seg + t + 1, lanes], (SUBLANES, LANES))
                cr, ci = start[j]
                xr = s_ref[re, rows_of(k), :] + pr * cr - pi * ci
                xi = s_ref[im, rows_of(k), :] + pr * ci + pi * cr
                s_ref[re, rows_of(k), :] = xr
                s_ref[im, rows_of(k), :] = xi
                if k == order[-1]:
                    carry_ref[re] = jnp.broadcast_to(xr[edge:edge + 1, :], (SUBLANES, LANES))
                    carry_ref[im] = jnp.broadcast_to(xi[edge:edge + 1, :], (SUBLANES, LANES))


def _s5_fwd(u, bd, cdt, dskip, sc, *, rows, tile):
    n = rows // tile

    def body(u_ref, bd_ref, cdt_ref, d_ref, sc_ref, y_ref, s_ref, carry_ref):
        @pl.when(pl.program_id(0) == 0)
        def _():
            carry_ref[...] = jnp.zeros_like(carry_ref)

        ub = _reorder(_time_perm(tile), u_ref[...].astype(BF16)).astype(BF16)
        for h in range(2):
            _to_lane_tiles(s_ref, h * S5_HALF_TILES, jnp.dot(ub[:, h * S5_HALF_IN:(h + 1) * S5_HALF_IN], bd_ref[h],
                                                             preferred_element_type=F32))
        _s5_scan(s_ref, sc_ref, carry_ref, tile, reverse=False)
        ys = [_nt(_lanes_of(s_ref, h * S5_HALF_TILES, S5_HALF_TILES), cdt_ref[h]) for h in range(2)]
        y_ref[...] = _reorder(_time_perm(tile, transpose=True), jnp.concatenate(ys, axis=1)) + d_ref[...] * u_ref[...]

    full = lambda a: pl.BlockSpec(a.shape, lambda i, nd=a.ndim: (0,) * nd)
    return pl.pallas_call(
        body, name="s5_fwd", grid=(n,),
        in_specs=[pl.BlockSpec((tile, S5_WIDTH), lambda i: (i, 0)), full(bd), full(cdt), full(dskip), full(sc)],
        out_specs=[pl.BlockSpec((tile, S5_WIDTH), lambda i: (i, 0)),
                   pl.BlockSpec((S5_TILES, tile, LANES), lambda i: (0, i, 0))],
        out_shape=[jax.ShapeDtypeStruct((rows, S5_WIDTH), F32), jax.ShapeDtypeStruct((S5_TILES, rows, LANES), F32)],
        scratch_shapes=[pltpu.VMEM((S5_TILES, SUBLANES, LANES), F32)],
        compiler_params=pltpu.CompilerParams(dimension_semantics=("arbitrary",), vmem_limit_bytes=VMEM_LIMIT),
    )(u, bd, cdt, dskip, sc)


def _s5_bwd(dy, s, u, bd, cdt, dskip, sc, *, rows, tile):
    n = rows // tile
    hc = 2 * S5_HALF_CPLX
    per8 = tile // SUBLANES
    quarter = S5_HALF_TILES // 2

    def body(dy_ref, s_ref, sp_ref, u_ref, bd_ref, cdt_ref, d_ref, sc_ref,
             du_ref, dbd_ref, dcdt_ref, dd_ref, da_ref, g_ref, carry_ref):
        i = pl.program_id(0)

        @pl.when(i == 0)
        def _():
            carry_ref[...] = jnp.zeros_like(carry_ref)
            dbd_ref[...] = jnp.zeros_like(dbd_ref)
            dcdt_ref[...] = jnp.zeros_like(dcdt_ref)
            dd_ref[...] = jnp.zeros_like(dd_ref)
            da_ref[...] = jnp.zeros_like(da_ref)

        dy = dy_ref[...]
        u = u_ref[...]
        perm = _time_perm(tile)
        dyb = _reorder(perm, dy.astype(BF16)).astype(BF16)
        ub = _reorder(perm, u.astype(BF16)).astype(BF16)
        for h in range(2):
            _to_lane_tiles(g_ref, h * S5_HALF_TILES, jnp.dot(dyb[:, h * S5_HALF_IN:(h + 1) * S5_HALF_IN], cdt_ref[h],
                                                             preferred_element_type=F32))
        _s5_scan(g_ref, sc_ref, carry_ref, tile, reverse=True)
        dus = []
        for h in range(2):
            gb = _lanes_of(g_ref, h * S5_HALF_TILES, S5_HALF_TILES).astype(BF16)
            sb = _lanes_of(s_ref, h * S5_HALF_TILES, S5_HALF_TILES).astype(BF16)
            dus.append(_nt(gb, bd_ref[h]))
            dbd_ref[h] += _tn(ub[:, h * S5_HALF_IN:(h + 1) * S5_HALF_IN], gb)
            dcdt_ref[h] += _tn(dyb[:, h * S5_HALF_IN:(h + 1) * S5_HALF_IN], sb)
        du = _reorder(_time_perm(tile, transpose=True), jnp.concatenate(dus, axis=1))
        du_ref[...] = (du + d_ref[...] * dy).astype(du_ref.dtype)
        dd_ref[...] += jnp.sum(dy * u, axis=0, keepdims=True)

        not_first = (i < n - 1).astype(F32)
        row = lax.broadcasted_iota(jnp.int32, (SUBLANES, quarter * LANES), 0)

        def step_before(first):
            cur = _lanes_of(s_ref, first, quarter)
            before_tile = _lanes_of(sp_ref, first, quarter)[SUBLANES - 1:SUBLANES, :] * not_first
            head = jnp.where(row == 0, before_tile, pltpu.roll(cur[tile - SUBLANES:], 1, 0))
            return jnp.concatenate([head, cur[:tile - SUBLANES]], axis=0)

        for h in range(2):
            re, im = h * S5_HALF_TILES, h * S5_HALF_TILES + quarter
            ssr = step_before(re)
            ssi = step_before(im)
            gr = _lanes_of(g_ref, re, quarter)
            gi = _lanes_of(g_ref, im, quarter)
            lanes = slice(h * S5_HALF_CPLX, (h + 1) * S5_HALF_CPLX)
            da_ref[0:1, lanes] += jnp.sum(ssr * gr + ssi * gi, axis=0, keepdims=True)
            da_ref[1:2, lanes] += jnp.sum(ssr * gi - ssi * gr, axis=0, keepdims=True)

    full = lambda a: pl.BlockSpec(a.shape, lambda i, nd=a.ndim: (0,) * nd)
    rev = lambda i: (n - 1 - i, 0)
    wshape = (2, S5_HALF_IN, hc)
    return pl.pallas_call(
        body, name="s5_bwd", grid=(n,),
        in_specs=[pl.BlockSpec((tile, S5_WIDTH), rev), pl.BlockSpec((S5_TILES, tile, LANES), lambda i: (0, n - 1 - i, 0)),
                  pl.BlockSpec((S5_TILES, SUBLANES, LANES), lambda i: (0, jnp.maximum((n - 1 - i) * per8 - 1, 0), 0)),
                  pl.BlockSpec((tile, S5_WIDTH), rev), full(bd), full(cdt), full(dskip), full(sc)],
        out_specs=[pl.BlockSpec((tile, S5_WIDTH), rev),
                   pl.BlockSpec(wshape, lambda i: (0, 0, 0)), pl.BlockSpec(wshape, lambda i: (0, 0, 0)),
                   pl.BlockSpec((1, S5_WIDTH), lambda i: (0, 0)), pl.BlockSpec((SUBLANES, S5_CPLX), lambda i: (0, 0))],
        out_shape=[jax.ShapeDtypeStruct((rows, S5_WIDTH), BF16), jax.ShapeDtypeStruct(wshape, F32),
                   jax.ShapeDtypeStruct(wshape, F32), jax.ShapeDtypeStruct((1, S5_WIDTH), F32),
                   jax.ShapeDtypeStruct((SUBLANES, S5_CPLX), F32)],
        scratch_shapes=[pltpu.VMEM((S5_TILES, tile, LANES), F32), pltpu.VMEM((S5_TILES, SUBLANES, LANES), F32)],
        compiler_params=pltpu.CompilerParams(dimension_semantics=("arbitrary",), vmem_limit_bytes=VMEM_LIMIT),
    )(dy, s, s, u, bd, cdt, dskip, sc)


def _s5_block_diag(parts):
    v = jnp.stack(parts, axis=2).reshape(2, 16, S5_GROUP, 2, S5_STATE)
    eye = jnp.eye(16, dtype=v.dtype)
    return jnp.einsum("hgcpn,gk->hgcpkn", v, eye).reshape(2, S5_HALF_IN, 2 * S5_HALF_CPLX)


def _s5_block_diag_extract(m):
    v = m.reshape(2, 16, S5_GROUP, 2, 16, S5_STATE)
    d = jnp.diagonal(v, axis1=1, axis2=4)
    d = jnp.transpose(d, (2, 0, 4, 1, 3)).reshape(2, S5_GROUPS, S5_GROUP, S5_STATE)
    return d[0], d[1]


def _cplx_to_lanes(v):
    return v.reshape(1, S5_CPLX)


def _lru_scan_fwd(a, b, *, rows, tile):
    n = rows // tile
    nblk = tile // SUBLANES
    group = 2

    def body(a_ref, b_ref, h_ref, carry_ref):
        @pl.when(pl.program_id(0) == 0)
        def _():
            carry_ref[...] = jnp.zeros_like(carry_ref)

        row = lax.broadcasted_iota(jnp.int32, (SUBLANES, LANES), 0)
        for q0 in range(0, LRU_WIDTH // LANES, group):
            offs = [q * LANES for q in range(q0, q0 + group)]

            def blk(t, carry, offs=offs):
                r0 = pl.multiple_of(t * SUBLANES, SUBLANES)
                new = []
                for j, o in enumerate(offs):
                    av = a_ref[pl.ds(r0, SUBLANES), o:o + LANES]
                    xv = b_ref[pl.ds(r0, SUBLANES), o:o + LANES]
                    for sh in (1, 2, 4):
                        m = row >= sh
                        xs = pltpu.roll(xv, sh, 0)
                        asft = pltpu.roll(av, sh, 0)
                        xv = xv + jnp.where(m, av * xs, 0.0)
                        av = jnp.where(m, av * asft, av)
                    hv = xv + av * carry[j]
                    h_ref[pl.ds(r0, SUBLANES), o:o + LANES] = hv
                    new.append(jnp.broadcast_to(hv[SUBLANES - 1:SUBLANES, :], (SUBLANES, LANES)))
                return tuple(new)

            carry = lax.fori_loop(0, nblk, blk, tuple(carry_ref[:, o:o + LANES] for o in offs), unroll=2)
            for j, o in enumerate(offs):
                carry_ref[:, o:o + LANES] = carry[j]

    spec = pl.BlockSpec((tile, LRU_WIDTH), lambda i: (i, 0))
    return pl.pallas_call(
        body, name="lru_scan_fwd", grid=(n,), in_specs=[spec, spec], out_specs=spec,
        out_shape=jax.ShapeDtypeStruct((rows, LRU_WIDTH), F32),
        scratch_shapes=[pltpu.VMEM((SUBLANES, LRU_WIDTH), F32)],
        compiler_params=pltpu.CompilerParams(dimension_semantics=("arbitrary",), vmem_limit_bytes=VMEM_LIMIT),
    )(a, b)


def _lru_scan_bwd(dh, a, *, rows, tile):
    n = rows // tile
    nblk = tile // SUBLANES
    group = 2

    def body(dh_ref, a_ref, g_ref, cg_ref, ca_ref):
        @pl.when(pl.program_id(0) == 0)
        def _():
            cg_ref[...] = jnp.zeros_like(cg_ref)
            ca_ref[...] = jnp.zeros_like(ca_ref)

        row = lax.broadcasted_iota(jnp.int32, (SUBLANES, LANES), 0)
        for q0 in range(0, LRU_WIDTH // LANES, group):
            offs = [q * LANES for q in range(q0, q0 + group)]

            def blk(t, carry, offs=offs):
                r0 = pl.multiple_of((nblk - 1 - t) * SUBLANES, SUBLANES)
                new = []
                for j, o in enumerate(offs):
                    cg, ca = carry[2 * j], carry[2 * j + 1]
                    araw = a_ref[pl.ds(r0, SUBLANES), o:o + LANES]
                    xv = dh_ref[pl.ds(r0, SUBLANES), o:o + LANES]
                    av = jnp.where(row == SUBLANES - 1, ca, pltpu.roll(araw, SUBLANES - 1, 0))
                    for sh in (1, 2, 4):
                        m = row <= SUBLANES - 1 - sh
                        xs = pltpu.roll(xv, SUBLANES - sh, 0)
                        asft = pltpu.roll(av, SUBLANES - sh, 0)
                        xv = xv + jnp.where(m, av * xs, 0.0)
                        av = jnp.where(m, av * asft, av)
                    gv = xv + av * cg
                    g_ref[pl.ds(r0, SUBLANES), o:o + LANES] = gv
                    new.append(jnp.broadcast_to(gv[0:1, :], (SUBLANES, LANES)))
                    new.append(jnp.broadcast_to(araw[0:1, :], (SUBLANES, LANES)))
                return tuple(new)

            carry0 = tuple(r[:, o:o + LANES] for o in offs for r in (cg_ref, ca_ref))
            carry = lax.fori_loop(0, nblk, blk, carry0, unroll=2)
            for j, o in enumerate(offs):
                cg_ref[:, o:o + LANES] = carry[2 * j]
                ca_ref[:, o:o + LANES] = carry[2 * j + 1]

    spec = pl.BlockSpec((tile, LRU_WIDTH), lambda i: (n - 1 - i, 0))
    return pl.pallas_call(
        body, name="lru_scan_bwd", grid=(n,), in_specs=[spec, spec], out_specs=spec,
        out_shape=jax.ShapeDtypeStruct((rows, LRU_WIDTH), F32),
        scratch_shapes=[pltpu.VMEM((SUBLANES, LRU_WIDTH), F32), pltpu.VMEM((SUBLANES, LRU_WIDTH), F32)],
        compiler_params=pltpu.CompilerParams(dimension_semantics=("arbitrary",), vmem_limit_bytes=VMEM_LIMIT),
    )(dh, a)


def _conv_fwd(i, x, prev, cw, cb):
    prev = prev * (i > 0).astype(F32)
    y = x * cw[3:4, :] + cb
    for s in range(1, CONV_WIDTH):
        y = y + _rows_before(x, prev, s) * cw[3 - s:4 - s, :]
    return y


def _lru_gates(c, wa, ba, wx, bx, lam):
    r = _sigmoid(_heads(_nn, c, wa) + ba)
    ig = _sigmoid(_heads(_nn, c, wx) + bx)
    z = -lam
    sp = jnp.maximum(z, 0.0) + jnp.log(1.0 + jnp.exp(-jnp.abs(z)))
    log_a = -LRU_C * r * sp
    a = jnp.exp(log_a)
    z2 = 2.0 * log_a
    series = -z2 * (1.0 + z2 * (0.5 + z2 * (1.0 / 6.0 + z2 * (1.0 / 24.0 + z2 * (1.0 / 120.0 + z2 / 720.0)))))
    one_minus = jnp.where(z2 > -0.2, series, 1.0 - jnp.exp(z2))
    mult = jnp.sqrt(one_minus)
    return r, ig, sp, a, mult


def _layer_fwd(x, p, w, rows):
    tile = ROW_TILE
    d = D_MODEL

    def f_in(i, xb, g, *ws):
        rstd = lax.rsqrt(jnp.mean(xb * xb, axis=-1, keepdims=True) + NORM_EPS)
        hb = (xb * rstd * g).astype(BF16)
        return tuple(jnp.dot(hb, wc, preferred_element_type=F32) for wc in ws) + (hb,)

    s5x, s5g, lrux, lrug, gs, gl, h = _rows(
        "f_in", f_in, [(x, "row"), (w["g_pre"], "full")] + [(wc, "full") for wc in w["w_in"]],
        [((rows, wd), BF16, "row") for wd in IN_WIDTHS] + [((rows, d), BF16, "row")], rows=rows, tile=tile)

    ys, st = _s5_fwd(s5x, w["bd"], w["cdt"], w["s5_d"], w["scf"], rows=rows, tile=tile)

    def f_s5post(i, ysb, gb, wglu, wbs):
        glv, _ = _gelu_parts(ysb)
        glu = _nn(glv, wglu)
        y2 = glu[:, :S5_WIDTH] * _sigmoid(glu[:, S5_WIDTH:]) * (gb * _sigmoid(gb))
        return (_nn(y2, wbs),)

    (z_s,) = _rows("f_s5post", f_s5post, [(ys, "row"), (s5g, "row"), (w["w_glu"], "full"), (w["w_bs"], "full")],
                   [((rows, d), BF16, "row")], rows=rows, tile=tile)

    def f_gates(i, xb, prev, cw, cb, wa, ba, wx, bx, lam):
        c = _conv_fwd(i, xb, prev, cw, cb)
        _, ig, _, a, mult = _lru_gates(c, wa, ba, wx, bx, lam)
        return a, mult * (ig * c)

    a, b = _rows("f_gates", f_gates,
                 [(lrux, "row"), (lrux, "prev"), (w["conv_w"], "full"), (w["conv_b"], "full"), (w["lru_w_a"], "full"),
                  (w["lru_b_a"], "full"), (w["lru_w_x"], "full"), (w["lru_b_x"], "full"), (w["lru_lambda"], "full")],
                 [((rows, LRU_WIDTH), F32, "row")] * 2, rows=rows, tile=tile)
    hl = _lru_scan_fwd(a, b, rows=rows, tile=tile)

    def f_merge(i, hb, lg, zs, gsb, glb, xb, wbl, wout, gpost):
        z_l = _nn(hb * (lg * _sigmoid(lg)), wbl)
        merged = _sigmoid(gsb) * zs + _sigmoid(glb) * z_l
        mix = _nn(merged, wout)
        rstd = lax.rsqrt(jnp.mean(mix * mix, axis=-1, keepdims=True) + NORM_EPS)
        return xb + mix * rstd * gpost, mix, z_l

    x1, mix, z_l = _rows("f_merge", f_merge,
                         [(hl, "row"), (lrug, "row"), (z_s, "row"), (gs, "row"), (gl, "row"), (x, "row"),
                          (w["w_bl"], "full"), (w["w_out"], "full"), (w["g_post"], "full")],
                         [((rows, d), F32, "row"), ((rows, d), BF16, "row"), ((rows, d), BF16, "row")], rows=rows, tile=tile)

    def f_ple(i, x1b, pb, wple, wpg):
        return (x1b + _nn(pb, wple) * _sigmoid(_nn(x1b, wpg)),)

    (x2,) = _rows("f_ple", f_ple, [(x1, "row"), (p, "row"), (w["w_ple"], "full"), (w["w_ple_gate"], "full")],
                  [((rows, d), F32, "row")], rows=rows, tile=tile)
    saved = dict(x=x, h=h, s5x=s5x, s5g=s5g, lrux=lrux, lrug=lrug, gs=gs, gl=gl, ys=ys, st=st, a=a, hl=hl, z_s=z_s,
                 z_l=z_l, mix=mix, x1=x1, p=p)
    return x2, saved


def _layer_bwd(dx2, sv, w, rows):
    tile = ROW_TILE
    d = D_MODEL
    g = {}

    def b_ple(i, dxb, x1b, pb, wple, wpg):
        pe = _nn(pb, wple)
        sg = _sigmoid(_nn(x1b, wpg))
        dpe = dxb * sg
        dgt = dxb * pe * sg * (1.0 - sg)
        return dxb + _nt(dgt, wpg), _tn(pb, dpe), _tn(x1b, dgt)

    dx1, g["w_ple"], g["w_ple_gate"] = _rows(
        "b_ple", b_ple, [(dx2, "row"), (sv["x1"], "row"), (sv["p"], "row"), (w["w_ple"], "full"), (w["w_ple_gate"], "full")],
        [((rows, d), F32, "row"), ((PLE_DIM, d), F32, "acc"), ((d, d), F32, "acc")], rows=rows, tile=tile)

    def b_merge(i, dxb, mixb, zs, zl, gsb, glb, wout, gpost):
        rstd = lax.rsqrt(jnp.mean(mixb * mixb, axis=-1, keepdims=True) + NORM_EPS)
        nrm = mixb * rstd
        dn = dxb * gpost
        dmix = rstd * (dn - nrm * jnp.mean(dn * nrm, axis=-1, keepdims=True))
        ss, sl = _sigmoid(gsb), _sigmoid(glb)
        merged = ss * zs + sl * zl
        dm = _nt(dmix, wout)
        return (dm * ss, dm * sl, dm * zs * ss * (1.0 - ss), dm * zl * sl * (1.0 - sl),
                _tn(merged, dmix), jnp.sum(dxb * nrm, axis=0, keepdims=True))

    dz_s, dz_l, dgs, dgl, g["w_out"], g["g_post"] = _rows(
        "b_merge", b_merge,
        [(dx1, "row"), (sv["mix"], "row"), (sv["z_s"], "row"), (sv["z_l"], "row"), (sv["gs"], "row"), (sv["gl"], "row"),
         (w["w_out"], "full"), (w["g_post"], "full")],
        [((rows, d), BF16, "row")] * 4 + [((d, d), F32, "acc"), ((1, d), F32, "acc")], rows=rows, tile=tile)

    def b_bl(i, dzl, hb, lg, wbl):
        sl = _sigmoid(lg)
        silu = lg * sl
        dy3 = _nt(dzl, wbl)
        return dy3 * silu, dy3 * hb * sl * (1.0 + lg * (1.0 - sl)), _tn(hb * silu, dzl)

    dh, dlrug, g["w_bl"] = _rows(
        "b_bl", b_bl, [(dz_l, "row"), (sv["hl"], "row"), (sv["lrug"], "row"), (w["w_bl"], "full")],
        [((rows, LRU_WIDTH), F32, "row"), ((rows, LRU_WIDTH), BF16, "row"), ((LRU_WIDTH, d), F32, "acc")], rows=rows, tile=tile)

    gh = _lru_scan_bwd(dh, sv["a"], rows=rows, tile=tile)

    def b_gates(i, ghb, hb, hprev, xb, xprev, cw, cb, wa, ba, wx, bx, lam):
        c = _conv_fwd(i, xb, xprev, cw, cb)
        r, ig, sp, a, mult = _lru_gates(c, wa, ba, wx, bx, lam)
        h_before = _rows_before(hb, hprev * (i > 0).astype(F32), 1)
        da = ghb * h_before
        dmult = ghb * ig * c
        dlog_a = da * a - dmult * a * a / mult
        dpre_r = dlog_a * (-LRU_C) * sp * r * (1.0 - r)
        dpre_i = ghb * mult * c * ig * (1.0 - ig)
        dc = ghb * mult * ig + _heads(_nt, dpre_r, wa) + _heads(_nt, dpre_i, wx)
        dlam = jnp.sum(dlog_a * LRU_C * r, axis=0, keepdims=True) * _sigmoid(-lam)
        return (dc, _heads_tn(c, dpre_r), _heads_tn(c, dpre_i), jnp.sum(dpre_r, axis=0, keepdims=True),
                jnp.sum(dpre_i, axis=0, keepdims=True), dlam)

    hshape = (LRU_HEADS, LRU_HEAD_DIM, LRU_HEAD_DIM)
    dc, g["lru_w_a"], g["lru_w_x"], g["lru_b_a"], g["lru_b_x"], g["lru_lambda"] = _rows(
        "b_gates", b_gates,
        [(gh, "row"), (sv["hl"], "row"), (sv["hl"], "prev"), (sv["lrux"], "row"), (sv["lrux"], "prev"),
         (w["conv_w"], "full"), (w["conv_b"], "full"), (w["lru_w_a"], "full"), (w["lru_b_a"], "full"),
         (w["lru_w_x"], "full"), (w["lru_b_x"], "full"), (w["lru_lambda"], "full")],
        [((rows, LRU_WIDTH), BF16, "row"), (hshape, F32, "acc"), (hshape, F32, "acc")] + [((1, LRU_WIDTH), F32, "acc")] * 3,
        rows=rows, tile=tile)

    n_tiles = rows // min(tile, rows)

    def b_conv(i, dcb, dnext, xb, xprev, cw):
        dnext = dnext * (i < n_tiles - 1).astype(F32)
        xprev = xprev * (i > 0).astype(F32)
        dx = dcb * cw[3:4, :]
        dws = [jnp.sum(dcb * xb, axis=0, keepdims=True)]
        for s in range(1, CONV_WIDTH):
            dx = dx + _rows_after(dcb, dnext, s) * cw[3 - s:4 - s, :]
            dws.append(jnp.sum(dcb * _rows_before(xb, xprev, s), axis=0, keepdims=True))
        return dx, jnp.concatenate(dws[::-1], axis=0), jnp.sum(dcb, axis=0, keepdims=True)

    dlrux, g["conv_w"], g["conv_b"] = _rows(
        "b_conv", b_conv, [(dc, "row"), (dc, "next"), (sv["lrux"], "row"), (sv["lrux"], "prev"), (w["conv_w"], "full")],
        [((rows, LRU_WIDTH), BF16, "row"), ((CONV_WIDTH, LRU_WIDTH), F32, "acc"), ((1, LRU_WIDTH), F32, "acc")],
        rows=rows, tile=tile)

    def b_s5post(i, dzs, ysb, gb, wglu, wbs):
        glv, dgelu = _gelu_parts(ysb)
        glu = _nn(glv, wglu)
        ga, gb2 = glu[:, :S5_WIDTH], glu[:, S5_WIDTH:]
        sb = _sigmoid(gb2)
        sg = _sigmoid(gb)
        silu = gb * sg
        y2 = ga * sb * silu
        dy2 = _nt(dzs, wbs)
        dglu = jnp.concatenate([dy2 * sb * silu, dy2 * ga * silu * sb * (1.0 - sb)], axis=1)
        dg = dy2 * ga * sb * sg * (1.0 + gb * (1.0 - sg))
        return _nt(dglu, wglu) * dgelu, dg, _tn(y2, dzs), _tn(glv, dglu)

    dys, ds5g, g["w_bs"], g["w_glu"] = _rows(
        "b_s5post", b_s5post, [(dz_s, "row"), (sv["ys"], "row"), (sv["s5g"], "row"), (w["w_glu"], "full"), (w["w_bs"], "full")],
        [((rows, S5_WIDTH), F32, "row"), ((rows, S5_WIDTH), BF16, "row"), ((S5_WIDTH, d), F32, "acc"),
         ((S5_WIDTH, 2 * S5_WIDTH), F32, "acc")],
        rows=rows, tile=tile)

    ds5x, g["bd"], g["cdt"], g["s5_d"], g["abar"] = _s5_bwd(dys, sv["st"], sv["s5x"], w["bd"], w["cdt"], w["s5_d"],
                                                            w["scb"], rows=rows, tile=tile)

    dcomps = [ds5x, ds5g, dlrux, dlrug, dgs, dgl]

    def b_in(i, xb, dx1b, gpre, *rest):
        dcs, ws = rest[:6], rest[6:]
        dh = _nt(dcs[0], ws[0])
        for dcv, wc in zip(dcs[1:], ws[1:]):
            dh = dh + _nt(dcv, wc)
        rstd = lax.rsqrt(jnp.mean(xb * xb, axis=-1, keepdims=True) + NORM_EPS)
        nrm = xb * rstd
        dn = dh * gpre
        dx = rstd * (dn - nrm * jnp.mean(dn * nrm, axis=-1, keepdims=True))
        return dx1b + dx, jnp.sum(dh * nrm, axis=0, keepdims=True)

    dx, g["g_pre"] = _rows(
        "b_in", b_in, [(sv["x"], "row"), (dx1, "row"), (w["g_pre"], "full")] + [(dcv, "raw") for dcv in dcomps]
        + [(wc, "full") for wc in w["w_in"]],
        [((rows, d), F32, "row"), ((1, d), F32, "acc")], rows=rows, tile=tile)

    def b_win(i, hb, dcv):
        return (_tn(hb, dcv),)

    g["w_in"] = jnp.concatenate(
        [_rows("b_win", b_win, [(sv["h"], "raw"), (dcv, "raw")], [((d, dcv.shape[1]), F32, "acc")], rows=rows, tile=4 * tile)[0]
         for dcv in dcomps], axis=1)
    return dx, g


SMALL = ("g_pre", "s5_a_re", "s5_a_im", "s5_log_dt", "s5_b_re", "s5_b_im", "s5_c_re", "s5_c_im", "s5_d", "conv_b",
         "lru_w_a", "lru_b_a", "lru_w_x", "lru_b_x", "lru_lambda", "g_post")
BIG = ("w_in", "w_glu", "w_bs", "conv_w", "w_bl", "w_out", "w_ple", "w_ple_gate")
BIG_SHARD_AXIS = {"w_in": 1, "w_glu": 1, "w_bs": 1, "conv_w": 1, "w_bl": 0, "w_out": 0, "w_ple": 1, "w_ple_gate": 0}


def _bcast_groups(v):
    return jnp.broadcast_to(v[:, None, :], (S5_GROUPS, S5_GROUP, S5_STATE)).reshape(S5_WIDTH, S5_STATE)


def _s5_prep_inputs(wl):
    ldt = jnp.broadcast_to(wl["s5_log_dt"][:, None], (S5_GROUPS, S5_STATE))
    gcn = lambda b: jnp.transpose(b, (0, 2, 1)).reshape(S5_WIDTH, S5_STATE)
    return (_bcast_groups(wl["s5_a_re"]), _bcast_groups(wl["s5_a_im"]), _bcast_groups(ldt), gcn(wl["s5_b_re"]),
            gcn(wl["s5_b_im"]))


def _layer_weights(wl):
    w = {}
    offs = [0]
    for wd in IN_WIDTHS:
        offs.append(offs[-1] + wd)
    w["w_in"] = [wl["w_in"][:, offs[k]:offs[k + 1]] for k in range(6)]
    for k in ("w_glu", "w_bs", "w_bl", "w_out", "w_ple", "w_ple_gate"):
        w[k] = wl[k]
    w["conv_w"] = wl["conv_w"]
    for k in ("g_pre", "g_post", "s5_d", "conv_b", "lru_b_a", "lru_b_x", "lru_lambda"):
        w[k] = wl[k].reshape(1, -1)
    w["lru_w_a"] = wl["lru_w_a"].astype(BF16)
    w["lru_w_x"] = wl["lru_w_x"].astype(BF16)
    prep_in = _s5_prep_inputs(wl)
    abr, abi, bbr, bbi = _s5_prep(*prep_in)
    w["prep_in"] = prep_in
    shape3 = (S5_GROUPS, S5_GROUP, S5_STATE)
    w["bd"] = _s5_block_diag([bbr.reshape(shape3), bbi.reshape(shape3)]).astype(BF16)
    w["cdt"] = _s5_block_diag([wl["s5_c_re"], -wl["s5_c_im"]]).astype(BF16)
    abr_s = abr.reshape(shape3)[:, 0, :]
    abi_s = abi.reshape(shape3)[:, 0, :]
    w["scf"], w["scb"] = _s5_consts(_cplx_to_lanes(abr_s), _cplx_to_lanes(abi_s), ROW_TILE // SUBLANES)
    return w


def _layer_param_grads(g, w, wl):
    out = {}
    shape3 = (S5_GROUPS, S5_GROUP, S5_STATE)
    dbr, dbi = _s5_block_diag_extract(g["bd"])
    dcr, dci = _s5_block_diag_extract(g["cdt"])
    out["s5_c_re"], out["s5_c_im"] = dcr, -dci
    zeros = jnp.zeros(shape3, F32)
    dar = zeros.at[:, 0, :].set(g["abar"][0].reshape(S5_GROUPS, S5_STATE)).reshape(S5_WIDTH, S5_STATE)
    dai = zeros.at[:, 0, :].set(g["abar"][1].reshape(S5_GROUPS, S5_STATE)).reshape(S5_WIDTH, S5_STATE)
    cts = (dar, dai, dbr.reshape(S5_WIDTH, S5_STATE), dbi.reshape(S5_WIDTH, S5_STATE))
    d_are, d_aim, d_ldt, d_bre, d_bim = _s5_prep_bwd(*w["prep_in"], cts)
    out["s5_a_re"] = d_are.reshape(shape3).sum(axis=1)
    out["s5_a_im"] = d_aim.reshape(shape3).sum(axis=1)
    out["s5_log_dt"] = d_ldt.reshape(shape3).sum(axis=(1, 2))
    out["s5_b_re"] = jnp.transpose(d_bre.reshape(shape3), (0, 2, 1))
    out["s5_b_im"] = jnp.transpose(d_bim.reshape(shape3), (0, 2, 1))
    out["s5_d"] = g["s5_d"].reshape(-1)
    for k in ("g_pre", "g_post", "conv_b", "lru_b_a", "lru_b_x", "lru_lambda"):
        out[k] = g[k].reshape(-1)
    for k in ("lru_w_a", "lru_w_x", "conv_w", "w_in", "w_glu", "w_bs", "w_bl", "w_out", "w_ple", "w_ple_gate"):
        out[k] = g[k]
    return out


def _local_step(x, p, layers, target):
    rows = x.shape[0]
    ws = [_layer_weights(wl) for wl in layers]
    saved = []
    for i in range(DEPTH):
        x, sv = _layer_fwd(x, p[i], ws[i], rows)
        saved.append(sv)

    def f_loss(i, yb, tb):
        e = yb - tb
        return e * (1.0 / D_MODEL), jnp.sum(jnp.sum(e * e, axis=0, keepdims=True), axis=1, keepdims=True)

    dx, sq = _rows("f_loss", f_loss, [(x, "row"), (target, "row")],
                   [((rows, D_MODEL), F32, "row"), ((1, 1), F32, "acc")], rows=rows, tile=ROW_TILE)
    loss = sq[0, 0] * (0.5 / D_MODEL)
    grads = [None] * DEPTH
    for i in reversed(range(DEPTH)):
        dx, g = _layer_bwd(dx, saved[i], ws[i], rows)
        grads[i] = _layer_param_grads(g, ws[i], layers[i])
    return loss, dx, grads


def _place():
    return lax.axis_index("x"), lax.axis_index("y"), lax.axis_index("c")


def _other_chips(x, y):
    return [(1 - x, y), (x, 1 - y), (1 - x, 1 - y)]


def _any_spec():
    return pl.BlockSpec(memory_space=pl.ANY)


ICI_PIECES = 1
D2D_PIECES = 1
D2D_SOLO_PIECES = 1


def _pieces(rows, k):
    step = rows // k
    assert step * k == rows and step % 16 == 0, (rows, k)
    return [(q * step, step) for q in range(k)]


def _gather_chips(name, v, via_sibling):
    rows = v.shape[0]
    half = rows // 2

    n_sent = half if via_sibling else rows

    def body(v_ref, out_ref, send_sems, recv_sems):
        x, y, c = _place()
        me = 2 * x + y
        chips = _other_chips(x, y)
        slots = [2 * cx + cy for cx, cy in chips]

        def part(slot, hc, o=0, s=n_sent):
            return out_ref.at[slot, pl.ds(hc * half + o, s), :] if via_sibling else out_ref.at[slot, pl.ds(o, s), :]

        def own(o=0, s=n_sent):
            return v_ref.at[pl.ds(c * half + o, s), :] if via_sibling else v_ref.at[pl.ds(o, s), :]

        def copy(k, src, dst, to):
            return pltpu.make_async_remote_copy(src_ref=src, dst_ref=dst, send_sem=send_sems.at[k], recv_sem=recv_sems.at[k],
                                                device_id=to, device_id_type=MESH)

        for k in range(3):
            for o, s in _pieces(n_sent, ICI_PIECES):
                copy(k, own(o, s), part(me, c, o, s), (*chips[k], c)).start()
        for k in range(3):
            copy(k, own(), part(slots[k], c), (*chips[k], c)).wait_recv()
            if via_sibling:
                for o, s in _pieces(n_sent, D2D_PIECES):
                    copy(3 + k, part(slots[k], c, o, s), part(slots[k], c, o, s), (x, y, 1 - c)).start()
        if via_sibling:
            for k in range(3):
                copy(3 + k, own(), part(slots[k], 1 - c), (x, y, 1 - c)).wait_recv()
        for k in range(6 if via_sibling else 3):
            copy(k, own(), part(me, c), (x, y, 1 - c)).wait_send()

    n_sem = 6 if via_sibling else 3
    others = pl.pallas_call(
        body, name=name, out_shape=jax.ShapeDtypeStruct((4,) + v.shape, v.dtype),
        in_specs=[_any_spec()], out_specs=_any_spec(),
        scratch_shapes=[pltpu.SemaphoreType.DMA((n_sem,)), pltpu.SemaphoreType.DMA((n_sem,))],
    )(v)
    return lax.dynamic_update_slice(others, v[None], (2 * lax.axis_index("x") + lax.axis_index("y"), 0, 0))


def _rs_sibling(gr):
    half = gr.shape[1] // 2

    def body(g_ref, got_ref, send_sem, recv_sem):
        x, y, c = _place()

        def give(src, dst):
            return pltpu.make_async_remote_copy(src_ref=src, dst_ref=dst, send_sem=send_sem, recv_sem=recv_sem,
                                                device_id=(x, y, 1 - c), device_id_type=MESH)

        copy = give(g_ref.at[:, pl.ds((1 - c) * half, half), :], got_ref)
        copy.start()
        copy.wait()

    return pl.pallas_call(
        body, name="rs_sibling", out_shape=jax.ShapeDtypeStruct((4, half, LANES), F32),
        in_specs=[_any_spec()], out_specs=_any_spec(),
        scratch_shapes=[pltpu.SemaphoreType.DMA, pltpu.SemaphoreType.DMA],
    )(gr)


def _rs_chips(a16):
    half = a16.shape[1]

    def body(a16_ref, got_ref, send_sems, recv_sems):
        x, y, c = _place()
        chips = _other_chips(x, y)

        def copy(k, o=0, s=half):
            cx, cy = chips[k]
            return pltpu.make_async_remote_copy(
                src_ref=a16_ref.at[2 * cx + cy, pl.ds(o, s), :], dst_ref=got_ref.at[k, pl.ds(o, s), :],
                send_sem=send_sems.at[k], recv_sem=recv_sems.at[k], device_id=(cx, cy, c), device_id_type=MESH)

        for k in range(3):
            for o, s in _pieces(half, ICI_PIECES):
                copy(k, o, s).start()
        for k in range(3):
            copy(k).wait()

    return pl.pallas_call(
        body, name="rs_chips", out_shape=jax.ShapeDtypeStruct((3, half, LANES), BF16),
        in_specs=[_any_spec()], out_specs=_any_spec(),
        scratch_shapes=[pltpu.SemaphoreType.DMA((3,)), pltpu.SemaphoreType.DMA((3,))],
    )(a16)


def _swap_halves(v):
    def body(v_ref, out_ref, send_sem, recv_sem):
        x, y, c = _place()

        def give(hc):
            return pltpu.make_async_remote_copy(src_ref=v_ref, dst_ref=out_ref.at[hc], send_sem=send_sem, recv_sem=recv_sem,
                                                device_id=(x, y, 1 - c), device_id_type=MESH)

        give(c).start()
        give(c).wait_send()
        give(1 - c).wait_recv()

    other = pl.pallas_call(
        body, name="swap_halves", out_shape=jax.ShapeDtypeStruct((2,) + v.shape, v.dtype),
        in_specs=[_any_spec()], out_specs=_any_spec(),
        scratch_shapes=[pltpu.SemaphoreType.DMA, pltpu.SemaphoreType.DMA],
    )(v)
    both = lax.dynamic_update_slice(other, v[None], (lax.axis_index("c"), 0, 0))
    return both.reshape(2 * v.shape[0], v.shape[1])


WIDE = 1024
PACK_TILE = 3072
GRAD_ROWS_UNIT = 2 * PACK_TILE


def _pack(parts, rows_unit, dtype):
    flat = jnp.concatenate([q.reshape(-1).astype(dtype) for q in parts])
    unit = rows_unit * LANES
    total = -(-flat.shape[0] // unit) * unit
    return jnp.pad(flat, (0, total - flat.shape[0])).reshape(-1, LANES)


def _unpack(flat, shapes):
    out, off = [], 0
    for s in shapes:
        n = 1
        for q in s:
            n *= q
        out.append(flat[off:off + n].reshape(s))
        off += n
    return out


def _to_slots(name, full):
    dp, r, c = full.shape
    if BIG_SHARD_AXIS[name] == 1:
        return jnp.transpose(full.reshape(dp, r, 4, c // 4), (2, 0, 1, 3)).reshape(4, -1)
    return jnp.transpose(full.reshape(dp, 4, r // 4, c), (1, 0, 2, 3)).reshape(4, -1)


def _from_slots(name, slots, shard_shape):
    dp, r, c = shard_shape
    v = slots.reshape(4, dp, r, c)
    if BIG_SHARD_AXIS[name] == 1:
        return jnp.transpose(v, (1, 2, 0, 3)).reshape(dp, r, 4 * c)
    return jnp.transpose(v, (1, 0, 2, 3)).reshape(dp, 4 * r, c)


def _adamw(name, w, g, m, v, tile):
    def fn(i, wb, gb, mb, vb):
        m2 = ADAM_B1 * mb + (1.0 - ADAM_B1) * gb
        v2 = ADAM_B2 * vb + (1.0 - ADAM_B2) * (gb * gb)
        m_hat = m2 / (1.0 - ADAM_B1 ** ADAM_STEP)
        v_hat = v2 / (1.0 - ADAM_B2 ** ADAM_STEP)
        return -ADAM_LR * (m_hat / (jnp.sqrt(v_hat) + ADAM_EPS) + ADAM_WD * wb), m2, v2

    return _rows(name, fn, [(w, "row"), (g, "row"), (m, "row"), (v, "row")], [(w.shape, F32, "row")] * 3,
                 rows=w.shape[0], tile=tile)


def _as_2d(a):
    return a.reshape(-1, a.shape[-1])


def _adam_tile(rows):
    for t in (256, 184, 128, 64, 32, 16, 8):
        if rows % t == 0:
            return t
    return rows


def kernel(x, p, g_pre, w_in, s5_a_re, s5_a_im, s5_log_dt, s5_b_re, s5_b_im, s5_c_re, s5_c_im, s5_d, w_glu, w_bs, conv_w, conv_b, lru_w_a, lru_b_a, lru_w_x, lru_b_x, lru_lambda, w_bl, w_out, g_post, w_ple, w_ple_gate, loss_target, m_g_pre, m_w_in, m_s5_a_re, m_s5_a_im, m_s5_log_dt, m_s5_b_re, m_s5_b_im, m_s5_c_re, m_s5_c_im, m_s5_d, m_w_glu, m_w_bs, m_conv_w, m_conv_b, m_lru_w_a, m_lru_b_a, m_lru_w_x, m_lru_b_x, m_lru_lambda, m_w_bl, m_w_out, m_g_post, m_w_ple, m_w_ple_gate, v_g_pre, v_w_in, v_s5_a_re, v_s5_a_im, v_s5_log_dt, v_s5_b_re, v_s5_b_im, v_s5_c_re, v_s5_c_im, v_s5_d, v_w_glu, v_w_bs, v_conv_w, v_conv_b, v_lru_w_a, v_lru_b_a, v_lru_w_x, v_lru_b_x, v_lru_lambda, v_w_bl, v_w_out, v_g_post, v_w_ple, v_w_ple_gate):
    wts = dict(g_pre=g_pre, w_in=w_in, s5_a_re=s5_a_re, s5_a_im=s5_a_im, s5_log_dt=s5_log_dt, s5_b_re=s5_b_re,
               s5_b_im=s5_b_im, s5_c_re=s5_c_re, s5_c_im=s5_c_im, s5_d=s5_d, w_glu=w_glu, w_bs=w_bs, conv_w=conv_w,
               conv_b=conv_b, lru_w_a=lru_w_a, lru_b_a=lru_b_a, lru_w_x=lru_w_x, lru_b_x=lru_b_x, lru_lambda=lru_lambda,
               w_bl=w_bl, w_out=w_out, g_post=g_post, w_ple=w_ple, w_ple_gate=w_ple_gate)
    mom1 = dict(g_pre=m_g_pre, w_in=m_w_in, s5_a_re=m_s5_a_re, s5_a_im=m_s5_a_im, s5_log_dt=m_s5_log_dt, s5_b_re=m_s5_b_re,
                s5_b_im=m_s5_b_im, s5_c_re=m_s5_c_re, s5_c_im=m_s5_c_im, s5_d=m_s5_d, w_glu=m_w_glu, w_bs=m_w_bs,
                conv_w=m_conv_w, conv_b=m_conv_b, lru_w_a=m_lru_w_a, lru_b_a=m_lru_b_a, lru_w_x=m_lru_w_x, lru_b_x=m_lru_b_x,
                lru_lambda=m_lru_lambda, w_bl=m_w_bl, w_out=m_w_out, g_post=m_g_post, w_ple=m_w_ple, w_ple_gate=m_w_ple_gate)
    mom2 = dict(g_pre=v_g_pre, w_in=v_w_in, s5_a_re=v_s5_a_re, s5_a_im=v_s5_a_im, s5_log_dt=v_s5_log_dt, s5_b_re=v_s5_b_re,
                s5_b_im=v_s5_b_im, s5_c_re=v_s5_c_re, s5_c_im=v_s5_c_im, s5_d=v_s5_d, w_glu=v_w_glu, w_bs=v_w_bs,
                conv_w=v_conv_w, conv_b=v_conv_b, lru_w_a=v_lru_w_a, lru_b_a=v_lru_b_a, lru_w_x=v_lru_w_x, lru_b_x=v_lru_b_x,
                lru_lambda=v_lru_lambda, w_bl=v_w_bl, w_out=v_w_out, g_post=v_g_post, w_ple=v_w_ple, w_ple_gate=v_w_ple_gate)
    names = list(wts)

    def wire(name):
        return lax.bitcast_convert_type(wts[name], BF16) if name == "conv_w" else wts[name].astype(BF16)

    wire_shapes = [wire(k).shape for k in BIG]
    gathered = _gather_chips("gather_weights", _pack([wire(k) for k in BIG], 128, BF16), via_sibling=True)
    per_chip = [_unpack(gathered[j].reshape(-1), wire_shapes) for j in range(4)]
    whole = {}
    for idx, k in enumerate(BIG):
        v = jnp.stack([per_chip[j][idx] for j in range(4)])
        if k == "conv_w":
            v = lax.bitcast_convert_type(v, F32)
        whole[k] = _from_slots(k, v.reshape(4, -1), wts[k].shape)
    layers = []
    for i in range(DEPTH):
        wl = {k: whole[k][i] for k in BIG}
        wl.update({k: wts[k][i] for k in SMALL})
        layers.append(wl)

    loss, grad_x, grads = _local_step(x[0], p[:, 0], layers, loss_target[0])
    loss = lax.psum(loss, ("x", "y", "c"))

    big_slots = jnp.concatenate([_to_slots(k, jnp.stack([grads[i][k] for i in range(DEPTH)])) for k in BIG], axis=1)
    small_shapes = [wts[k].shape for k in SMALL]
    small_flat = jnp.concatenate([jnp.stack([grads[i][k] for i in range(DEPTH)]).reshape(-1) for k in SMALL])
    n_small = small_flat.shape[0]
    small_q = -(-n_small // (4 * 8 * WIDE)) * 8 * WIDE
    small_slots = jnp.pad(small_flat, (0, 4 * small_q - n_small)).reshape(4, small_q)
    n_big = big_slots.shape[1]
    n_big_pad = -(-n_big // (8 * WIDE)) * 8 * WIDE
    n_slot = n_big_pad + small_q
    unit = GRAD_ROWS_UNIT * LANES
    n_slot_pad = -(-n_slot // unit) * unit
    gr = jnp.concatenate([jnp.pad(big_slots, ((0, 0), (0, n_big_pad - n_big))), small_slots,
                          jnp.zeros((4, n_slot_pad - n_slot), F32)], axis=1).reshape(4, -1, LANES)
    got = _rs_sibling(gr)
    half = got.shape[1]
    mine = lax.dynamic_slice_in_dim(gr, lax.axis_index("c") * half, half, axis=1)
    rows2d = lambda a: a.reshape(-1, LANES)

    def f_add1(i, a, b):
        s = a + b
        return s, s

    a32, a16 = _rows("rs_add1", f_add1, [(rows2d(mine), "row"), (rows2d(got), "row")],
                     [((4 * half, LANES), F32, "row"), ((4 * half, LANES), BF16, "row")], rows=4 * half, tile=PACK_TILE)
    got3 = _rs_chips(a16.reshape(4, half, LANES))
    own = lax.dynamic_index_in_dim(a32.reshape(4, half, LANES), 2 * lax.axis_index("x") + lax.axis_index("y"), 0, keepdims=False)

    def f_add2(i, o, g0, g1, g2):
        return (((o + g0.astype(F32)) + g1.astype(F32)) + g2.astype(F32),)

    (red_half,) = _rows("rs_add2", f_add2, [(own, "row")] + [(got3[k], "row") for k in range(3)],
                        [((half, LANES), F32, "row")], rows=half, tile=PACK_TILE)
    red = _swap_halves(red_half).reshape(-1)
    small_red = _gather_chips("gather_small", red[n_big_pad:n_big_pad + small_q].reshape(-1, LANES), via_sibling=False)
    small_red = small_red.reshape(-1)[:n_small]

    big_shapes = [wts[k].shape for k in BIG]
    grad_out = dict(zip(BIG, _unpack(red[:n_big], big_shapes)))
    grad_out.update(zip(SMALL, _unpack(small_red, small_shapes)))
    delta, new_m, new_v = {}, {}, {}
    for k in BIG:
        w2 = _as_2d(wts[k])
        res = _adamw("adamw_" + k, w2, _as_2d(grad_out[k]), _as_2d(mom1[k]), _as_2d(mom2[k]), _adam_tile(w2.shape[0]))
        delta[k], new_m[k], new_v[k] = [r.reshape(wts[k].shape) for r in res]
    pack_small = lambda d: jnp.pad(jnp.concatenate([d[k].reshape(-1) for k in SMALL]), (0, 4 * small_q - n_small)).reshape(-1, WIDE)
    res = _adamw("adamw_small", pack_small(wts), pack_small(grad_out), pack_small(mom1), pack_small(mom2),
                 _adam_tile(4 * small_q // WIDE))
    for d, r in zip((delta, new_m, new_v), res):
        d.update(zip(SMALL, _unpack(r.reshape(-1), small_shapes)))
    return (loss, grad_x[None], *[grad_out[k] for k in names], *[delta[k] for k in names],
            *[new_m[k] for k in names], *[new_v[k] for k in names])
```

```python
import jax
import jax.numpy as jnp
from jax import lax
from jax.experimental import pallas as pl
from jax.experimental.pallas import tpu as pltpu

F32 = jnp.float32
BF16 = jnp.bfloat16
MESH = pl.DeviceIdType.MESH

DEPTH = 2
D_MODEL = 1024
NORM_EPS = 1e-6
S5_WIDTH = 512
S5_GROUPS = 32
S5_GROUP = 16
S5_STATE = 64
LRU_WIDTH = 1280
LRU_HEADS = 10
LRU_HEAD_DIM = 128
LRU_C = 8.0
CONV_WIDTH = 4
PLE_DIM = 256
IN_WIDTHS = (S5_WIDTH, S5_WIDTH, LRU_WIDTH, LRU_WIDTH, D_MODEL, D_MODEL)
IN_OFFSETS = (0, 512, 1024, 2304, 3584, 4608, 5632)
IN_SLOT = 5632 // 4
ADAM_LR = 0.001
ADAM_B1 = 0.9
ADAM_B2 = 0.999
ADAM_EPS = 1e-08
ADAM_WD = 0.01
ADAM_STEP = 10

SUBLANES = 8
LANES = 128
S5_HALF_IN = S5_WIDTH // 2
S5_CPLX = S5_GROUPS * S5_STATE
S5_HALF_CPLX = S5_CPLX // 2
S5_LANES = 2 * S5_CPLX
VMEM_LIMIT = 48 * 2 ** 20
ROW_TILE = 256


def _sigmoid(x):
    return 1.0 / (1.0 + jnp.exp(-x))


def _gelu_parts(x):
    k = 0.7978845608028654
    t = jnp.tanh(k * (x + 0.044715 * x * x * x))
    val = 0.5 * x * (1.0 + t)
    grad = 0.5 * (1.0 + t) + 0.5 * x * (1.0 - t * t) * k * (1.0 + 3.0 * 0.044715 * x * x)
    return val, grad


def _nn(a, w):
    return jnp.dot(a.astype(BF16), w.astype(BF16), preferred_element_type=F32)


def _nt(a, w):
    return lax.dot_general(a.astype(BF16), w.astype(BF16), (((1,), (1,)), ((), ())), preferred_element_type=F32)


def _tn(a, b):
    return lax.dot_general(a.astype(BF16), b.astype(BF16), (((0,), (0,)), ((), ())), preferred_element_type=F32)


def _heads(op, a, w):
    d = LRU_HEAD_DIM
    return jnp.concatenate([op(a[:, h * d:(h + 1) * d], w[h]) for h in range(LRU_HEADS)], axis=1)


def _heads_tn(a, b):
    d = LRU_HEAD_DIM
    return jnp.stack([_tn(a[:, h * d:(h + 1) * d], b[:, h * d:(h + 1) * d]) for h in range(LRU_HEADS)], axis=0)


def _rows_before(x, halo, s):
    main = pltpu.roll(x, s, 0)
    head = pltpu.roll(jnp.concatenate([halo, x[0:SUBLANES]], axis=0), s, 0)[SUBLANES:2 * SUBLANES]
    return jnp.concatenate([head, main[SUBLANES:]], axis=0)


def _rows_after(x, halo, s):
    n = x.shape[0]
    main = pltpu.roll(x, n - s, 0)
    tail = pltpu.roll(jnp.concatenate([x[n - SUBLANES:], halo], axis=0), 2 * SUBLANES - s, 0)[0:SUBLANES]
    return jnp.concatenate([main[:n - SUBLANES], tail], axis=0)


def _rows(name, fn, ins, outs, *, rows, tile):
    tile = min(tile, rows)
    n = rows // tile
    assert n * tile == rows, (name, rows, tile)
    in_specs = []
    for arr, kind in ins:
        halo = SUBLANES * (4 // arr.dtype.itemsize)
        per, last = tile // halo, rows // halo - 1
        if kind in ("row", "raw"):
            in_specs.append(pl.BlockSpec((tile, arr.shape[1]), lambda i: (i, 0)))
        elif kind == "prev":
            in_specs.append(pl.BlockSpec((halo, arr.shape[1]), lambda i, per=per: (jnp.maximum(i * per - 1, 0), 0)))
        elif kind == "next":
            in_specs.append(pl.BlockSpec((halo, arr.shape[1]),
                                         lambda i, per=per, last=last: (jnp.minimum((i + 1) * per, last), 0)))
        else:
            in_specs.append(pl.BlockSpec(arr.shape, lambda i, nd=arr.ndim: (0,) * nd))
    out_shape, out_specs = [], []
    for shape, dtype, kind in outs:
        out_shape.append(jax.ShapeDtypeStruct(shape, dtype))
        if kind == "row":
            out_specs.append(pl.BlockSpec((tile, shape[1]), lambda i: (i, 0)))
        else:
            out_specs.append(pl.BlockSpec(shape, lambda i, nd=len(shape): (0,) * nd))
    n_in = len(ins)

    def load(ref, kind):
        v = ref[...]
        if kind in ("row", "prev", "next"):
            v = v.astype(F32)
        if kind == "prev":
            v = v[v.shape[0] - SUBLANES:]
        if kind == "next":
            v = v[:SUBLANES]
        return v

    def body(*refs):
        i = pl.program_id(0)
        vals = fn(i, *[load(r, kind) for r, (_, kind) in zip(refs[:n_in], ins)])
        assert len(vals) == len(outs), name
        for r, v, (_, _, kind) in zip(refs[n_in:], vals, outs):
            if kind == "row":
                r[...] = v.astype(r.dtype)
            else:
                @pl.when(i == 0)
                def _():
                    r[...] = jnp.zeros_like(r)

                r[...] += v.astype(r.dtype)

    return pl.pallas_call(
        body, name=name, grid=(n,), in_specs=in_specs, out_specs=out_specs, out_shape=out_shape,
        compiler_params=pltpu.CompilerParams(dimension_semantics=("arbitrary",), vmem_limit_bytes=VMEM_LIMIT),
    )(*[a for a, _ in ins])


def _s5_discretise(are, aim, ldt, bre, bim):
    dt = jnp.exp(ldt)
    er = jnp.exp(are * dt)
    abr = er * jnp.cos(aim * dt)
    abi = er * jnp.sin(aim * dt)
    den = are * are + aim * aim
    zr = ((abr - 1.0) * are + abi * aim) / den
    zi = (abi * are - (abr - 1.0) * aim) / den
    return abr, abi, zr * bre - zi * bim, zr * bim + zi * bre


def _s5_prep(are, aim, ldt, bre, bim):
    def body(a, b, c, d, e, o0, o1, o2, o3):
        r = _s5_discretise(a[...], b[...], c[...], d[...], e[...])
        o0[...], o1[...], o2[...], o3[...] = r

    sd = jax.ShapeDtypeStruct(are.shape, F32)
    return pl.pallas_call(body, name="s5_prep", out_shape=[sd] * 4)(are, aim, ldt, bre, bim)


def _s5_prep_bwd(are, aim, ldt, bre, bim, cts):
    def body(a, b, c, d, e, c0, c1, c2, c3, o0, o1, o2, o3, o4):
        _, vjp = jax.vjp(_s5_discretise, a[...], b[...], c[...], d[...], e[...])
        r = vjp((c0[...], c1[...], c2[...], c3[...]))
        o0[...], o1[...], o2[...], o3[...], o4[...] = r

    sd = jax.ShapeDtypeStruct(are.shape, F32)
    return pl.pallas_call(body, name="s5_prep_bwd", out_shape=[sd] * 5)(are, aim, ldt, bre, bim, *cts)


def _s5_consts(abr, abi, seg):
    shape = (SUBLANES, S5_CPLX)
    assert seg & (seg - 1) == 0 and seg % SUBLANES == 0, seg

    def body(ar_ref, ai_ref, f_ref, b_ref):
        def cmul(p, q):
            return (p[0] * q[0] - p[1] * q[1], p[0] * q[1] + p[1] * q[0])

        row = lax.broadcasted_iota(jnp.int32, shape, 0)
        a1 = (jnp.broadcast_to(ar_ref[...], shape), jnp.broadcast_to(ai_ref[...], shape))
        squares = [a1]
        while 1 << (len(squares) - 1) < 4 * seg:
            squares.append(cmul(squares[-1], squares[-1]))
        nb = seg.bit_length() - 1
        fwd, rev = [], []
        for k, a in ((1, squares[nb]), (2, squares[nb + 1]), (4, squares[nb + 2])):
            fwd += [jnp.where(row >= k, a[0], 0.0), jnp.where(row >= k, a[1], 0.0)]
            rev += [jnp.where(row <= 7 - k, a[0], 0.0), jnp.where(row <= 7 - k, -a[1], 0.0)]
        fwd += [a1[0], a1[1]]
        rev += [a1[0], -a1[1]]
        e = lax.broadcasted_iota(jnp.int32, (seg, S5_CPLX), 0) + 1
        wide = lambda v: jnp.broadcast_to(v[0:1, :], (seg, S5_CPLX))
        pr, pi = jnp.ones((seg, S5_CPLX), F32), jnp.zeros((seg, S5_CPLX), F32)
        for b in range(nb + 1):
            sr, si = wide(squares[b][0]), wide(squares[b][1])
            bit = ((e >> b) & 1) == 1
            pr, pi = jnp.where(bit, pr * sr - pi * si, pr), jnp.where(bit, pr * si + pi * sr, pi)
        f_ref[...] = jnp.concatenate(fwd + [pr, pi], axis=0)
        b_ref[...] = jnp.concatenate(rev + [pr, -pi], axis=0)

    sd = jax.ShapeDtypeStruct((8 * SUBLANES + 2 * seg, S5_CPLX), F32)
    return pl.pallas_call(body, name="s5_consts", out_shape=[sd, sd])(abr, abi)


S5_TILES = S5_LANES // LANES
S5_HALF_TILES = S5_TILES // 2


def _s5_tile_index(q):
    re = (q // 8) * S5_HALF_TILES + (q % 8)
    return re, re + S5_HALF_TILES // 2


def _lanes_of(ref, first, count):
    return jnp.concatenate([ref[j] for j in range(first, first + count)], axis=1)


def _to_lane_tiles(ref, first, value):
    for j in range(value.shape[1] // LANES):
        ref[first + j] = value[:, j * LANES:(j + 1) * LANES]


def _time_perm(tile, transpose=False):
    seg = tile // SUBLANES
    rho = lax.broadcasted_iota(jnp.int32, (tile, tile), 1 if transpose else 0)
    t = lax.broadcasted_iota(jnp.int32, (tile, tile), 0 if transpose else 1)
    return (t == (rho & (SUBLANES - 1)) * seg + (rho >> 3)).astype(BF16)


def _reorder(perm, x):
    out = None
    for _ in range(1 if x.dtype == BF16 else 3):
        piece = x.astype(BF16)
        part = jnp.dot(perm, piece, preferred_element_type=F32)
        out = part if out is None else out + part
        x = x - piece.astype(x.dtype)
    return out


def _s5_scan(s_ref, sc_ref, carry_ref, tile, reverse):
    seg = tile // SUBLANES
    group = 4
    edge = 0 if reverse else SUBLANES - 1
    row = lax.broadcasted_iota(jnp.int32, (SUBLANES, LANES), 0)
    order = range(seg - 1, -1, -1) if reverse else range(seg)
    rows_of = lambda k: pl.ds(k * SUBLANES, SUBLANES)
    base = 8 * SUBLANES

    for q0 in range(0, S5_CPLX // LANES, group):
        qs = list(range(q0, q0 + group))
        tiles = [_s5_tile_index(q) for q in qs]
        cst = lambda k, q: sc_ref[k * SUBLANES:(k + 1) * SUBLANES, q * LANES:(q + 1) * LANES]
        state = [(jnp.zeros((SUBLANES, LANES), F32), jnp.zeros((SUBLANES, LANES), F32)) for _ in qs]
        mult = [(cst(6, q), cst(7, q)) for q in qs]
        for k in order:
            for j, (re, im) in enumerate(tiles):
                ar, ai = mult[j]
                xr, xi = state[j]
                nr = ar * xr - ai * xi + s_ref[re, rows_of(k), :]
                ni = ar * xi + ai * xr + s_ref[im, rows_of(k), :]
                s_ref[re, rows_of(k), :] = nr
                s_ref[im, rows_of(k), :] = ni
                state[j] = (nr, ni)
        start = []
        for j, (q, (re, im)) in enumerate(zip(qs, tiles)):
            er, ei = state[j]
            shift1 = SUBLANES - 1 if reverse else 1
            dr = jnp.where(row == SUBLANES - 1 - edge, carry_ref[re], pltpu.roll(er, shift1, 0))
            di = jnp.where(row == SUBLANES - 1 - edge, carry_ref[im], pltpu.roll(ei, shift1, 0))
            for c, sh in ((0, 1), (2, 2), (4, 4)):
                shift = SUBLANES - sh if reverse else sh
                ar, ai = cst(c, q), cst(c + 1, q)
                sr, si = pltpu.roll(dr, shift, 0), pltpu.roll(di, shift, 0)
                dr, di = dr + ar * sr - ai * si, di + ar * si + ai * sr
            start.append((dr, di))
        for k in order:
            t = seg - 1 - k if reverse else k
            for j, (q, (re, im)) in enumerate(zip(qs, tiles)):
                lanes = slice(q * LANES, (q + 1) * LANES)
                pr = jnp.broadcast_to(sc_ref[base + t:base + t + 1, lanes], (SUBLANES, LANES))
                pi = jnp.broadcast_to(sc_ref[base + seg + t:base + seg + t + 1, lanes], (SUBLANES, LANES))
                cr, ci = start[j]
                xr = s_ref[re, rows_of(k), :] + pr * cr - pi * ci
                xi = s_ref[im, rows_of(k), :] + pr * ci + pi * cr
                s_ref[re, rows_of(k), :] = xr
                s_ref[im, rows_of(k), :] = xi
                if k == order[-1]:
                    carry_ref[re] = jnp.broadcast_to(xr[edge:edge + 1, :], (SUBLANES, LANES))
                    carry_ref[im] = jnp.broadcast_to(xi[edge:edge + 1, :], (SUBLANES, LANES))


def _s5_fwd(u, bd, cdt, dskip, sc, *, rows, tile):
    n = rows // tile

    def body(u_ref, bd_ref, cdt_ref, d_ref, sc_ref, y_ref, s_ref, carry_ref):
        @pl.when(pl.program_id(0) == 0)
        def _():
            carry_ref[...] = jnp.zeros_like(carry_ref)

        ub = _reorder(_time_perm(tile), u_ref[...].astype(BF16)).astype(BF16)
        for h in range(2):
            _to_lane_tiles(s_ref, h * S5_HALF_TILES, jnp.dot(ub[:, h * S5_HALF_IN:(h + 1) * S5_HALF_IN], bd_ref[h],
                                                             preferred_element_type=F32))
        _s5_scan(s_ref, sc_ref, carry_ref, tile, reverse=False)
        ys = [_nt(_lanes_of(s_ref, h * S5_HALF_TILES, S5_HALF_TILES), cdt_ref[h]) for h in range(2)]
        y_ref[...] = _reorder(_time_perm(tile, transpose=True), jnp.concatenate(ys, axis=1)) + d_ref[...] * u_ref[...]

    full = lambda a: pl.BlockSpec(a.shape, lambda i, nd=a.ndim: (0,) * nd)
    return pl.pallas_call(
        body, name="s5_fwd", grid=(n,),
        in_specs=[pl.BlockSpec((tile, S5_WIDTH), lambda i: (i, 0)), full(bd), full(cdt), full(dskip), full(sc)],
        out_specs=[pl.BlockSpec((tile, S5_WIDTH), lambda i: (i, 0)),
                   pl.BlockSpec((S5_TILES, tile, LANES), lambda i: (0, i, 0))],
        out_shape=[jax.ShapeDtypeStruct((rows, S5_WIDTH), F32), jax.ShapeDtypeStruct((S5_TILES, rows, LANES), F32)],
        scratch_shapes=[pltpu.VMEM((S5_TILES, SUBLANES, LANES), F32)],
        compiler_params=pltpu.CompilerParams(dimension_semantics=("arbitrary",), vmem_limit_bytes=VMEM_LIMIT),
    )(u, bd, cdt, dskip, sc)


def _s5_bwd(dy, s, u, bd, cdt, dskip, sc, *, rows, tile):
    n = rows // tile
    hc = 2 * S5_HALF_CPLX
    per8 = tile // SUBLANES
    quarter = S5_HALF_TILES // 2

    def body(dy_ref, s_ref, sp_ref, u_ref, bd_ref, cdt_ref, d_ref, sc_ref,
             du_ref, dbd_ref, dcdt_ref, dd_ref, da_ref, g_ref, carry_ref):
        i = pl.program_id(0)

        @pl.when(i == 0)
        def _():
            carry_ref[...] = jnp.zeros_like(carry_ref)
            dbd_ref[...] = jnp.zeros_like(dbd_ref)
            dcdt_ref[...] = jnp.zeros_like(dcdt_ref)
            dd_ref[...] = jnp.zeros_like(dd_ref)
            da_ref[...] = jnp.zeros_like(da_ref)

        dy = dy_ref[...]
        u = u_ref[...]
        perm = _time_perm(tile)
        dyb = _reorder(perm, dy.astype(BF16)).astype(BF16)
        ub = _reorder(perm, u.astype(BF16)).astype(BF16)
        for h in range(2):
            _to_lane_tiles(g_ref, h * S5_HALF_TILES, jnp.dot(dyb[:, h * S5_HALF_IN:(h + 1) * S5_HALF_IN], cdt_ref[h],
                                                             preferred_element_type=F32))
        _s5_scan(g_ref, sc_ref, carry_ref, tile, reverse=True)
        dus = []
        for h in range(2):
            gb = _lanes_of(g_ref, h * S5_HALF_TILES, S5_HALF_TILES).astype(BF16)
            sb = _lanes_of(s_ref, h * S5_HALF_TILES, S5_HALF_TILES).astype(BF16)
            dus.append(_nt(gb, bd_ref[h]))
            dbd_ref[h] += _tn(ub[:, h * S5_HALF_IN:(h + 1) * S5_HALF_IN], gb)
            dcdt_ref[h] += _tn(dyb[:, h * S5_HALF_IN:(h + 1) * S5_HALF_IN], sb)
        du = _reorder(_time_perm(tile, transpose=True), jnp.concatenate(dus, axis=1))
        du_ref[...] = (du + d_ref[...] * dy).astype(du_ref.dtype)
        dd_ref[...] += jnp.sum(dy * u, axis=0, keepdims=True)

        not_first = (i < n - 1).astype(F32)
        row = lax.broadcasted_iota(jnp.int32, (SUBLANES, quarter * LANES), 0)

        def step_before(first):
            cur = _lanes_of(s_ref, first, quarter)
            before_tile = _lanes_of(sp_ref, first, quarter)[SUBLANES - 1:SUBLANES, :] * not_first
            head = jnp.where(row == 0, before_tile, pltpu.roll(cur[tile - SUBLANES:], 1, 0))
            return jnp.concatenate([head, cur[:tile - SUBLANES]], axis=0)

        for h in range(2):
            re, im = h * S5_HALF_TILES, h * S5_HALF_TILES + quarter
            ssr = step_before(re)
            ssi = step_before(im)
            gr = _lanes_of(g_ref, re, quarter)
            gi = _lanes_of(g_ref, im, quarter)
            lanes = slice(h * S5_HALF_CPLX, (h + 1) * S5_HALF_CPLX)
            da_ref[0:1, lanes] += jnp.sum(ssr * gr + ssi * gi, axis=0, keepdims=True)
            da_ref[1:2, lanes] += jnp.sum(ssr * gi - ssi * gr, axis=0, keepdims=True)

    full = lambda a: pl.BlockSpec(a.shape, lambda i, nd=a.ndim: (0,) * nd)
    rev = lambda i: (n - 1 - i, 0)
    wshape = (2, S5_HALF_IN, hc)
    return pl.pallas_call(
        body, name="s5_bwd", grid=(n,),
        in_specs=[pl.BlockSpec((tile, S5_WIDTH), rev), pl.BlockSpec((S5_TILES, tile, LANES), lambda i: (0, n - 1 - i, 0)),
                  pl.BlockSpec((S5_TILES, SUBLANES, LANES), lambda i: (0, jnp.maximum((n - 1 - i) * per8 - 1, 0), 0)),
                  pl.BlockSpec((tile, S5_WIDTH), rev), full(bd), full(cdt), full(dskip), full(sc)],
        out_specs=[pl.BlockSpec((tile, S5_WIDTH), rev),
                   pl.BlockSpec(wshape, lambda i: (0, 0, 0)), pl.BlockSpec(wshape, lambda i: (0, 0, 0)),
                   pl.BlockSpec((1, S5_WIDTH), lambda i: (0, 0)), pl.BlockSpec((SUBLANES, S5_CPLX), lambda i: (0, 0))],
        out_shape=[jax.ShapeDtypeStruct((rows, S5_WIDTH), BF16), jax.ShapeDtypeStruct(wshape, F32),
                   jax.ShapeDtypeStruct(wshape, F32), jax.ShapeDtypeStruct((1, S5_WIDTH), F32),
                   jax.ShapeDtypeStruct((SUBLANES, S5_CPLX), F32)],
        scratch_shapes=[pltpu.VMEM((S5_TILES, tile, LANES), F32), pltpu.VMEM((S5_TILES, SUBLANES, LANES), F32)],
        compiler_params=pltpu.CompilerParams(dimension_semantics=("arbitrary",), vmem_limit_bytes=VMEM_LIMIT),
    )(dy, s, s, u, bd, cdt, dskip, sc)


def _s5_block_diag(parts):
    v = jnp.stack(parts, axis=2).reshape(2, 16, S5_GROUP, 2, S5_STATE)
    eye = jnp.eye(16, dtype=v.dtype)
    return jnp.einsum("hgcpn,gk->hgcpkn", v, eye).reshape(2, S5_HALF_IN, 2 * S5_HALF_CPLX)


def _s5_block_diag_extract(m):
    v = m.reshape(2, 16, S5_GROUP, 2, 16, S5_STATE)
    d = jnp.diagonal(v, axis1=1, axis2=4)
    d = jnp.transpose(d, (2, 0, 4, 1, 3)).reshape(2, S5_GROUPS, S5_GROUP, S5_STATE)
    return d[0], d[1]


def _cplx_to_lanes(v):
    return v.reshape(1, S5_CPLX)


def _lru_scan_fwd(a, b, *, rows, tile):
    n = rows // tile
    nblk = tile // SUBLANES
    group = 5

    def body(a_ref, b_ref, h_ref, carry_ref):
        @pl.when(pl.program_id(0) == 0)
        def _():
            carry_ref[...] = jnp.zeros_like(carry_ref)

        row = lax.broadcasted_iota(jnp.int32, (SUBLANES, LANES), 0)
        for q0 in range(0, LRU_WIDTH // LANES, group):
            offs = [q * LANES for q in range(q0, q0 + group)]

            def blk(t, carry, offs=offs):
                r0 = pl.multiple_of(t * SUBLANES, SUBLANES)
                new = []
                for j, o in enumerate(offs):
                    av = a_ref[pl.ds(r0, SUBLANES), o:o + LANES]
                    xv = b_ref[pl.ds(r0, SUBLANES), o:o + LANES]
                    for sh in (1, 2, 4):
                        m = row >= sh
                        xs = pltpu.roll(xv, sh, 0)
                        asft = pltpu.roll(av, sh, 0)
                        xv = xv + jnp.where(m, av * xs, 0.0)
                        av = jnp.where(m, av * asft, av)
                    hv = xv + av * carry[j]
                    h_ref[pl.ds(r0, SUBLANES), o:o + LANES] = hv
                    new.append(jnp.broadcast_to(hv[SUBLANES - 1:SUBLANES, :], (SUBLANES, LANES)))
                return tuple(new)

            carry = lax.fori_loop(0, nblk, blk, tuple(carry_ref[:, o:o + LANES] for o in offs), unroll=2)
            for j, o in enumerate(offs):
                carry_ref[:, o:o + LANES] = carry[j]

    spec = pl.BlockSpec((tile, LRU_WIDTH), lambda i: (i, 0))
    return pl.pallas_call(
        body, name="lru_scan_fwd", grid=(n,), in_specs=[spec, spec], out_specs=spec,
        out_shape=jax.ShapeDtypeStruct((rows, LRU_WIDTH), F32),
        scratch_shapes=[pltpu.VMEM((SUBLANES, LRU_WIDTH), F32)],
        compiler_params=pltpu.CompilerParams(dimension_semantics=("arbitrary",), vmem_limit_bytes=VMEM_LIMIT),
    )(a, b)


def _lru_scan_bwd(dh, a, *, rows, tile):
    n = rows // tile
    nblk = tile // SUBLANES
    group = 5

    def body(dh_ref, a_ref, g_ref, cg_ref, ca_ref):
        @pl.when(pl.program_id(0) == 0)
        def _():
            cg_ref[...] = jnp.zeros_like(cg_ref)
            ca_ref[...] = jnp.zeros_like(ca_ref)

        row = lax.broadcasted_iota(jnp.int32, (SUBLANES, LANES), 0)
        for q0 in range(0, LRU_WIDTH // LANES, group):
            offs = [q * LANES for q in range(q0, q0 + group)]

            def blk(t, carry, offs=offs):
                r0 = pl.multiple_of((nblk - 1 - t) * SUBLANES, SUBLANES)
                new = []
                for j, o in enumerate(offs):
                    cg, ca = carry[2 * j], carry[2 * j + 1]
                    araw = a_ref[pl.ds(r0, SUBLANES), o:o + LANES]
                    xv = dh_ref[pl.ds(r0, SUBLANES), o:o + LANES]
                    av = jnp.where(row == SUBLANES - 1, ca, pltpu.roll(araw, SUBLANES - 1, 0))
                    for sh in (1, 2, 4):
                        m = row <= SUBLANES - 1 - sh
                        xs = pltpu.roll(xv, SUBLANES - sh, 0)
                        asft = pltpu.roll(av, SUBLANES - sh, 0)
                        xv = xv + jnp.where(m, av * xs, 0.0)
                        av = jnp.where(m, av * asft, av)
                    gv = xv + av * cg
                    g_ref[pl.ds(r0, SUBLANES), o:o + LANES] = gv
                    new.append(jnp.broadcast_to(gv[0:1, :], (SUBLANES, LANES)))
                    new.append(jnp.broadcast_to(araw[0:1, :], (SUBLANES, LANES)))
                return tuple(new)

            carry0 = tuple(r[:, o:o + LANES] for o in offs for r in (cg_ref, ca_ref))
            carry = lax.fori_loop(0, nblk, blk, carry0, unroll=2)
            for j, o in enumerate(offs):
                cg_ref[:, o:o + LANES] = carry[2 * j]
                ca_ref[:, o:o + LANES] = carry[2 * j + 1]

    spec = pl.BlockSpec((tile, LRU_WIDTH), lambda i: (n - 1 - i, 0))
    return pl.pallas_call(
        body, name="lru_scan_bwd", grid=(n,), in_specs=[spec, spec], out_specs=spec,
        out_shape=jax.ShapeDtypeStruct((rows, LRU_WIDTH), F32),
        scratch_shapes=[pltpu.VMEM((SUBLANES, LRU_WIDTH), F32), pltpu.VMEM((SUBLANES, LRU_WIDTH), F32)],
        compiler_params=pltpu.CompilerParams(dimension_semantics=("arbitrary",), vmem_limit_bytes=VMEM_LIMIT),
    )(dh, a)


def _conv_fwd(i, x, prev, cw, cb):
    prev = prev * (i > 0).astype(F32)
    y = x * cw[3:4, :] + cb
    for s in range(1, CONV_WIDTH):
        y = y + _rows_before(x, prev, s) * cw[3 - s:4 - s, :]
    return y


def _lru_gates(c, wa, ba, wx, bx, lam):
    r = _sigmoid(_heads(_nn, c, wa) + ba)
    ig = _sigmoid(_heads(_nn, c, wx) + bx)
    z = -lam
    sp = jnp.maximum(z, 0.0) + jnp.log(1.0 + jnp.exp(-jnp.abs(z)))
    log_a = -LRU_C * r * sp
    a = jnp.exp(log_a)
    z2 = 2.0 * log_a
    series = -z2 * (1.0 + z2 * (0.5 + z2 * (1.0 / 6.0 + z2 * (1.0 / 24.0 + z2 * (1.0 / 120.0 + z2 / 720.0)))))
    one_minus = jnp.where(z2 > -0.2, series, 1.0 - jnp.exp(z2))
    mult = jnp.sqrt(one_minus)
    return r, ig, sp, a, mult


def _layer_fwd(x, p, w, rows):
    tile = ROW_TILE
    d = D_MODEL

    def f_in(i, xb, g, *ws):
        rstd = lax.rsqrt(jnp.mean(xb * xb, axis=-1, keepdims=True) + NORM_EPS)
        hb = (xb * rstd * g).astype(BF16)
        proj = jnp.concatenate([jnp.dot(hb, wj, preferred_element_type=F32) for wj in ws], axis=1)
        return tuple(proj[:, IN_OFFSETS[k]:IN_OFFSETS[k + 1]] for k in range(6)) + (hb,)

    s5x, s5g, lrux, lrug, gs, gl, h = _rows(
        "f_in", f_in, [(x, "row"), (w["g_pre"], "full")] + [(wc, "full") for wc in w["w_in"]],
        [((rows, wd), BF16, "row") for wd in IN_WIDTHS] + [((rows, d), BF16, "row")], rows=rows, tile=tile)

    ys, st = _s5_fwd(s5x, w["bd"], w["cdt"], w["s5_d"], w["scf"], rows=rows, tile=tile)

    def f_s5post(i, ysb, gb, wglu, wbs):
        glv, _ = _gelu_parts(ysb)
        glu = _nn(glv, wglu)
        y2 = glu[:, :S5_WIDTH] * _sigmoid(glu[:, S5_WIDTH:]) * (gb * _sigmoid(gb))
        return (_nn(y2, wbs),)

    (z_s,) = _rows("f_s5post", f_s5post, [(ys, "row"), (s5g, "row"), (w["w_glu"], "full"), (w["w_bs"], "full")],
                   [((rows, d), BF16, "row")], rows=rows, tile=tile)

    def f_gates(i, xb, prev, cw, cb, wa, ba, wx, bx, lam):
        c = _conv_fwd(i, xb, prev, cw, cb)
        _, ig, _, a, mult = _lru_gates(c, wa, ba, wx, bx, lam)
        return a, mult * (ig * c)

    a, b = _rows("f_gates", f_gates,
                 [(lrux, "row"), (lrux, "prev"), (w["conv_w"], "full"), (w["conv_b"], "full"), (w["lru_w_a"], "full"),
                  (w["lru_b_a"], "full"), (w["lru_w_x"], "full"), (w["lru_b_x"], "full"), (w["lru_lambda"], "full")],
                 [((rows, LRU_WIDTH), F32, "row")] * 2, rows=rows, tile=tile)
    hl = _lru_scan_fwd(a, b, rows=rows, tile=tile)

    def f_merge(i, hb, lg, zs, gsb, glb, xb, wbl, wout, gpost):
        z_l = _nn(hb * (lg * _sigmoid(lg)), wbl)
        merged = _sigmoid(gsb) * zs + _sigmoid(glb) * z_l
        mix = _nn(merged, wout)
        rstd = lax.rsqrt(jnp.mean(mix * mix, axis=-1, keepdims=True) + NORM_EPS)
        return xb + mix * rstd * gpost, mix, z_l

    x1, mix, z_l = _rows("f_merge", f_merge,
                         [(hl, "row"), (lrug, "row"), (z_s, "row"), (gs, "row"), (gl, "row"), (x, "row"),
                          (w["w_bl"], "full"), (w["w_out"], "full"), (w["g_post"], "full")],
                         [((rows, d), F32, "row"), ((rows, d), BF16, "row"), ((rows, d), BF16, "row")], rows=rows, tile=tile)

    def f_ple(i, x1b, pb, wple, wpg):
        return (x1b + _nn(pb, wple) * _sigmoid(_nn(x1b, wpg)),)

    (x2,) = _rows("f_ple", f_ple, [(x1, "row"), (p, "row"), (w["w_ple"], "full"), (w["w_ple_gate"], "full")],
                  [((rows, d), F32, "row")], rows=rows, tile=tile)
    saved = dict(x=x, h=h, s5x=s5x, s5g=s5g, lrux=lrux, lrug=lrug, gs=gs, gl=gl, ys=ys, st=st, a=a, hl=hl, z_s=z_s,
                 z_l=z_l, mix=mix, x1=x1, p=p)
    return x2, saved


def _layer_bwd(dx2, sv, w, rows):
    tile = ROW_TILE
    d = D_MODEL
    g = {}

    def b_ple(i, dxb, x1b, pb, wple, wpg):
        pe = _nn(pb, wple)
        sg = _sigmoid(_nn(x1b, wpg))
        dpe = dxb * sg
        dgt = dxb * pe * sg * (1.0 - sg)
        return dxb + _nt(dgt, wpg), _tn(pb, dpe), _tn(x1b, dgt)

    dx1, g["w_ple"], g["w_ple_gate"] = _rows(
        "b_ple", b_ple, [(dx2, "row"), (sv["x1"], "row"), (sv["p"], "row"), (w["w_ple"], "full"), (w["w_ple_gate"], "full")],
        [((rows, d), F32, "row"), ((PLE_DIM, d), F32, "acc"), ((d, d), F32, "acc")], rows=rows, tile=tile)

    def b_merge(i, dxb, mixb, zs, zl, gsb, glb, wout, gpost):
        rstd = lax.rsqrt(jnp.mean(mixb * mixb, axis=-1, keepdims=True) + NORM_EPS)
        nrm = mixb * rstd
        dn = dxb * gpost
        dmix = rstd * (dn - nrm * jnp.mean(dn * nrm, axis=-1, keepdims=True))
        ss, sl = _sigmoid(gsb), _sigmoid(glb)
        merged = ss * zs + sl * zl
        dm = _nt(dmix, wout)
        return (dm * ss, dm * sl, dm * zs * ss * (1.0 - ss), dm * zl * sl * (1.0 - sl),
                _tn(merged, dmix), jnp.sum(dxb * nrm, axis=0, keepdims=True))

    dz_s, dz_l, dgs, dgl, g["w_out"], g["g_post"] = _rows(
        "b_merge", b_merge,
        [(dx1, "row"), (sv["mix"], "row"), (sv["z_s"], "row"), (sv["z_l"], "row"), (sv["gs"], "row"), (sv["gl"], "row"),
         (w["w_out"], "full"), (w["g_post"], "full")],
        [((rows, d), BF16, "row")] * 4 + [((d, d), F32, "acc"), ((1, d), F32, "acc")], rows=rows, tile=tile)

    def b_bl(i, dzl, hb, lg, wbl):
        sl = _sigmoid(lg)
        silu = lg * sl
        dy3 = _nt(dzl, wbl)
        return dy3 * silu, dy3 * hb * sl * (1.0 + lg * (1.0 - sl)), _tn(hb * silu, dzl)

    dh, dlrug, g["w_bl"] = _rows(
        "b_bl", b_bl, [(dz_l, "row"), (sv["hl"], "row"), (sv["lrug"], "row"), (w["w_bl"], "full")],
        [((rows, LRU_WIDTH), F32, "row"), ((rows, LRU_WIDTH), BF16, "row"), ((LRU_WIDTH, d), F32, "acc")], rows=rows, tile=tile)

    gh = _lru_scan_bwd(dh, sv["a"], rows=rows, tile=tile)

    def b_gates(i, ghb, hb, hprev, xb, xprev, cw, cb, wa, ba, wx, bx, lam):
        c = _conv_fwd(i, xb, xprev, cw, cb)
        r, ig, sp, a, mult = _lru_gates(c, wa, ba, wx, bx, lam)
        h_before = _rows_before(hb, hprev * (i > 0).astype(F32), 1)
        da = ghb * h_before
        dmult = ghb * ig * c
        dlog_a = da * a - dmult * a * a / mult
        dpre_r = dlog_a * (-LRU_C) * sp * r * (1.0 - r)
        dpre_i = ghb * mult * c * ig * (1.0 - ig)
        dc = ghb * mult * ig + _heads(_nt, dpre_r, wa) + _heads(_nt, dpre_i, wx)
        dlam = jnp.sum(dlog_a * LRU_C * r, axis=0, keepdims=True) * _sigmoid(-lam)
        return (dc, _heads_tn(c, dpre_r), _heads_tn(c, dpre_i), jnp.sum(dpre_r, axis=0, keepdims=True),
                jnp.sum(dpre_i, axis=0, keepdims=True), dlam)

    hshape = (LRU_HEADS, LRU_HEAD_DIM, LRU_HEAD_DIM)
    dc, g["lru_w_a"], g["lru_w_x"], g["lru_b_a"], g["lru_b_x"], g["lru_lambda"] = _rows(
        "b_gates", b_gates,
        [(gh, "row"), (sv["hl"], "row"), (sv["hl"], "prev"), (sv["lrux"], "row"), (sv["lrux"], "prev"),
         (w["conv_w"], "full"), (w["conv_b"], "full"), (w["lru_w_a"], "full"), (w["lru_b_a"], "full"),
         (w["lru_w_x"], "full"), (w["lru_b_x"], "full"), (w["lru_lambda"], "full")],
        [((rows, LRU_WIDTH), BF16, "row"), (hshape, F32, "acc"), (hshape, F32, "acc")] + [((1, LRU_WIDTH), F32, "acc")] * 3,
        rows=rows, tile=tile)

    n_tiles = rows // min(tile, rows)

    def b_conv(i, dcb, dnext, xb, xprev, cw):
        dnext = dnext * (i < n_tiles - 1).astype(F32)
        xprev = xprev * (i > 0).astype(F32)
        dx = dcb * cw[3:4, :]
        dws = [jnp.sum(dcb * xb, axis=0, keepdims=True)]
        for s in range(1, CONV_WIDTH):
            dx = dx + _rows_after(dcb, dnext, s) * cw[3 - s:4 - s, :]
            dws.append(jnp.sum(dcb * _rows_before(xb, xprev, s), axis=0, keepdims=True))
        return dx, jnp.concatenate(dws[::-1], axis=0), jnp.sum(dcb, axis=0, keepdims=True)

    dlrux, g["conv_w"], g["conv_b"] = _rows(
        "b_conv", b_conv, [(dc, "row"), (dc, "next"), (sv["lrux"], "row"), (sv["lrux"], "prev"), (w["conv_w"], "full")],
        [((rows, LRU_WIDTH), BF16, "row"), ((CONV_WIDTH, LRU_WIDTH), F32, "acc"), ((1, LRU_WIDTH), F32, "acc")],
        rows=rows, tile=tile)

    def b_s5post(i, dzs, ysb, gb, wglu, wbs):
        glv, dgelu = _gelu_parts(ysb)
        glu = _nn(glv, wglu)
        ga, gb2 = glu[:, :S5_WIDTH], glu[:, S5_WIDTH:]
        sb = _sigmoid(gb2)
        sg = _sigmoid(gb)
        silu = gb * sg
        y2 = ga * sb * silu
        dy2 = _nt(dzs, wbs)
        dglu = jnp.concatenate([dy2 * sb * silu, dy2 * ga * silu * sb * (1.0 - sb)], axis=1)
        dg = dy2 * ga * sb * sg * (1.0 + gb * (1.0 - sg))
        return _nt(dglu, wglu) * dgelu, dg, _tn(y2, dzs), _tn(glv, dglu)

    dys, ds5g, g["w_bs"], g["w_glu"] = _rows(
        "b_s5post", b_s5post, [(dz_s, "row"), (sv["ys"], "row"), (sv["s5g"], "row"), (w["w_glu"], "full"), (w["w_bs"], "full")],
        [((rows, S5_WIDTH), F32, "row"), ((rows, S5_WIDTH), BF16, "row"), ((S5_WIDTH, d), F32, "acc"),
         ((S5_WIDTH, 2 * S5_WIDTH), F32, "acc")],
        rows=rows, tile=tile)

    ds5x, g["bd"], g["cdt"], g["s5_d"], g["abar"] = _s5_bwd(dys, sv["st"], sv["s5x"], w["bd"], w["cdt"], w["s5_d"],
                                                            w["scb"], rows=rows, tile=tile)

    dcomps = [ds5x, ds5g, dlrux, dlrug, dgs, dgl]

    def b_in(i, xb, dx1b, gpre, *rest):
        dproj, ws = jnp.concatenate(rest[:6], axis=1), rest[6:]
        dh = _nt(dproj[:, :IN_SLOT], ws[0])
        for j in range(1, 4):
            dh = dh + _nt(dproj[:, j * IN_SLOT:(j + 1) * IN_SLOT], ws[j])
        rstd = lax.rsqrt(jnp.mean(xb * xb, axis=-1, keepdims=True) + NORM_EPS)
        nrm = xb * rstd
        dn = dh * gpre
        dx = rstd * (dn - nrm * jnp.mean(dn * nrm, axis=-1, keepdims=True))
        return dx1b + dx, jnp.sum(dh * nrm, axis=0, keepdims=True)

    dx, g["g_pre"] = _rows(
        "b_in", b_in, [(sv["x"], "row"), (dx1, "row"), (w["g_pre"], "full")] + [(dcv, "raw") for dcv in dcomps]
        + [(wc, "full") for wc in w["w_in"]],
        [((rows, d), F32, "row"), ((1, d), F32, "acc")], rows=rows, tile=tile)

    g["w_in"] = []
    for j in range(4):
        lo, hi = j * IN_SLOT, (j + 1) * IN_SLOT
        ks = [k for k in range(6) if IN_OFFSETS[k] < hi and IN_OFFSETS[k + 1] > lo]
        first = IN_OFFSETS[ks[0]]

        def b_win(i, hb, *parts, lo=lo, hi=hi, first=first):
            return (_tn(hb, jnp.concatenate(parts, axis=1)[:, lo - first:hi - first]),)

        g["w_in"].append(_rows("b_win", b_win, [(sv["h"], "raw")] + [(dcomps[k], "raw") for k in ks],
                               [((d, IN_SLOT), F32, "acc")], rows=rows, tile=4 * tile)[0])
    return dx, g


SMALL = ("g_pre", "s5_a_re", "s5_a_im", "s5_log_dt", "s5_b_re", "s5_b_im", "s5_c_re", "s5_c_im", "s5_d", "conv_b",
         "lru_w_a", "lru_b_a", "lru_w_x", "lru_b_x", "lru_lambda", "g_post")
BIG = ("w_in", "w_glu", "w_bs", "conv_w", "w_bl", "w_out", "w_ple", "w_ple_gate")
BIG_SHARD_AXIS = {"w_in": 1, "w_glu": 1, "w_bs": 1, "conv_w": 1, "w_bl": 0, "w_out": 0, "w_ple": 1, "w_ple_gate": 0}


def _bcast_groups(v):
    return jnp.broadcast_to(v[:, None, :], (S5_GROUPS, S5_GROUP, S5_STATE)).reshape(S5_WIDTH, S5_STATE)


def _s5_prep_inputs(wl):
    ldt = jnp.broadcast_to(wl["s5_log_dt"][:, None], (S5_GROUPS, S5_STATE))
    gcn = lambda b: jnp.transpose(b, (0, 2, 1)).reshape(S5_WIDTH, S5_STATE)
    return (_bcast_groups(wl["s5_a_re"]), _bcast_groups(wl["s5_a_im"]), _bcast_groups(ldt), gcn(wl["s5_b_re"]),
            gcn(wl["s5_b_im"]))


def _layer_weights(wl):
    w = {}
    for k in ("w_in", "w_glu", "w_bs", "w_bl", "w_out", "w_ple", "w_ple_gate"):
        w[k] = wl[k]
    w["conv_w"] = wl["conv_w"]
    for k in ("g_pre", "g_post", "s5_d", "conv_b", "lru_b_a", "lru_b_x", "lru_lambda"):
        w[k] = wl[k].reshape(1, -1)
    w["lru_w_a"] = wl["lru_w_a"].astype(BF16)
    w["lru_w_x"] = wl["lru_w_x"].astype(BF16)
    prep_in = _s5_prep_inputs(wl)
    abr, abi, bbr, bbi = _s5_prep(*prep_in)
    w["prep_in"] = prep_in
    shape3 = (S5_GROUPS, S5_GROUP, S5_STATE)
    w["bd"] = _s5_block_diag([bbr.reshape(shape3), bbi.reshape(shape3)]).astype(BF16)
    w["cdt"] = _s5_block_diag([wl["s5_c_re"], -wl["s5_c_im"]]).astype(BF16)
    abr_s = abr.reshape(shape3)[:, 0, :]
    abi_s = abi.reshape(shape3)[:, 0, :]
    w["scf"], w["scb"] = _s5_consts(_cplx_to_lanes(abr_s), _cplx_to_lanes(abi_s), ROW_TILE // SUBLANES)
    return w


def _layer_param_grads(g, w, wl):
    out = {}
    shape3 = (S5_GROUPS, S5_GROUP, S5_STATE)
    dbr, dbi = _s5_block_diag_extract(g["bd"])
    dcr, dci = _s5_block_diag_extract(g["cdt"])
    out["s5_c_re"], out["s5_c_im"] = dcr, -dci
    zeros = jnp.zeros(shape3, F32)
    dar = zeros.at[:, 0, :].set(g["abar"][0].reshape(S5_GROUPS, S5_STATE)).reshape(S5_WIDTH, S5_STATE)
    dai = zeros.at[:, 0, :].set(g["abar"][1].reshape(S5_GROUPS, S5_STATE)).reshape(S5_WIDTH, S5_STATE)
    cts = (dar, dai, dbr.reshape(S5_WIDTH, S5_STATE), dbi.reshape(S5_WIDTH, S5_STATE))
    d_are, d_aim, d_ldt, d_bre, d_bim = _s5_prep_bwd(*w["prep_in"], cts)
    out["s5_a_re"] = d_are.reshape(shape3).sum(axis=1)
    out["s5_a_im"] = d_aim.reshape(shape3).sum(axis=1)
    out["s5_log_dt"] = d_ldt.reshape(shape3).sum(axis=(1, 2))
    out["s5_b_re"] = jnp.transpose(d_bre.reshape(shape3), (0, 2, 1))
    out["s5_b_im"] = jnp.transpose(d_bim.reshape(shape3), (0, 2, 1))
    out["s5_d"] = g["s5_d"].reshape(-1)
    for k in ("g_pre", "g_post", "conv_b", "lru_b_a", "lru_b_x", "lru_lambda"):
        out[k] = g[k].reshape(-1)
    for k in ("lru_w_a", "lru_w_x", "conv_w", "w_in", "w_glu", "w_bs", "w_bl", "w_out", "w_ple", "w_ple_gate"):
        out[k] = g[k]
    return out


def _local_step(x, p, layers, target):
    rows = x.shape[0]
    ws = [_layer_weights(wl) for wl in layers]
    saved = []
    for i in range(DEPTH):
        x, sv = _layer_fwd(x, p[i], ws[i], rows)
        saved.append(sv)

    def f_loss(i, yb, tb):
        e = yb - tb
        return e * (1.0 / D_MODEL), jnp.sum(jnp.sum(e * e, axis=0, keepdims=True), axis=1, keepdims=True)

    dx, sq = _rows("f_loss", f_loss, [(x, "row"), (target, "row")],
                   [((rows, D_MODEL), F32, "row"), ((1, 1), F32, "acc")], rows=rows, tile=ROW_TILE)
    loss = sq[0, 0] * (0.5 / D_MODEL)
    grads = [None] * DEPTH
    for i in reversed(range(DEPTH)):
        dx, g = _layer_bwd(dx, saved[i], ws[i], rows)
        grads[i] = _layer_param_grads(g, ws[i], layers[i])
    return loss, dx, grads


def _place():
    return lax.axis_index("x"), lax.axis_index("y"), lax.axis_index("c")


def _other_chips(x, y):
    return [(1 - x, y), (x, 1 - y), (1 - x, 1 - y)]


def _any_spec():
    return pl.BlockSpec(memory_space=pl.ANY)


ICI_PIECES = 1
D2D_PIECES = 1
D2D_SOLO_PIECES = 1


def _pieces(rows, k):
    step = rows // k
    assert step * k == rows and step % 16 == 0, (rows, k)
    return [(q * step, step) for q in range(k)]


def _gather_chips(name, v, via_sibling):
    rows = v.shape[0]
    half = rows // 2

    n_sent = half if via_sibling else rows

    def body(v_ref, out_ref, send_sems, recv_sems):
        x, y, c = _place()
        me = 2 * x + y
        chips = _other_chips(x, y)
        slots = [2 * cx + cy for cx, cy in chips]

        def part(slot, hc, o=0, s=n_sent):
            return out_ref.at[slot, pl.ds(hc * half + o, s), :] if via_sibling else out_ref.at[slot, pl.ds(o, s), :]

        def own(o=0, s=n_sent):
            return v_ref.at[pl.ds(c * half + o, s), :] if via_sibling else v_ref.at[pl.ds(o, s), :]

        def copy(k, src, dst, to):
            return pltpu.make_async_remote_copy(src_ref=src, dst_ref=dst, send_sem=send_sems.at[k], recv_sem=recv_sems.at[k],
                                                device_id=to, device_id_type=MESH)

        for k in range(3):
            for o, s in _pieces(n_sent, ICI_PIECES):
                copy(k, own(o, s), part(me, c, o, s), (*chips[k], c)).start()
        for k in range(3):
            copy(k, own(), part(slots[k], c), (*chips[k], c)).wait_recv()
            if via_sibling:
                for o, s in _pieces(n_sent, D2D_PIECES):
                    copy(3 + k, part(slots[k], c, o, s), part(slots[k], c, o, s), (x, y, 1 - c)).start()
        if via_sibling:
            for k in range(3):
                copy(3 + k, own(), part(slots[k], 1 - c), (x, y, 1 - c)).wait_recv()
        for k in range(6 if via_sibling else 3):
            copy(k, own(), part(me, c), (x, y, 1 - c)).wait_send()

    n_sem = 6 if via_sibling else 3
    others = pl.pallas_call(
        body, name=name, out_shape=jax.ShapeDtypeStruct((4,) + v.shape, v.dtype),
        in_specs=[_any_spec()], out_specs=_any_spec(),
        scratch_shapes=[pltpu.SemaphoreType.DMA((n_sem,)), pltpu.SemaphoreType.DMA((n_sem,))],
    )(v)
    return lax.dynamic_update_slice(others, v[None], (2 * lax.axis_index("x") + lax.axis_index("y"), 0, 0))


def _rs_sibling(gr):
    half = gr.shape[1] // 2

    def body(g_ref, got_ref, send_sem, recv_sem):
        x, y, c = _place()

        def give(src, dst):
            return pltpu.make_async_remote_copy(src_ref=src, dst_ref=dst, send_sem=send_sem, recv_sem=recv_sem,
                                                device_id=(x, y, 1 - c), device_id_type=MESH)

        copy = give(g_ref.at[:, pl.ds((1 - c) * half, half), :], got_ref)
        copy.start()
        copy.wait()

    return pl.pallas_call(
        body, name="rs_sibling", out_shape=jax.ShapeDtypeStruct((4, half, LANES), F32),
        in_specs=[_any_spec()], out_specs=_any_spec(),
        scratch_shapes=[pltpu.SemaphoreType.DMA, pltpu.SemaphoreType.DMA],
    )(gr)


def _rs_chips(a16):
    half = a16.shape[1]

    def body(a16_ref, got_ref, send_sems, recv_sems):
        x, y, c = _place()
        chips = _other_chips(x, y)

        def copy(k, o=0, s=half):
            cx, cy = chips[k]
            return pltpu.make_async_remote_copy(
                src_ref=a16_ref.at[2 * cx + cy, pl.ds(o, s), :], dst_ref=got_ref.at[k, pl.ds(o, s), :],
                send_sem=send_sems.at[k], recv_sem=recv_sems.at[k], device_id=(cx, cy, c), device_id_type=MESH)

        for k in range(3):
            for o, s in _pieces(half, ICI_PIECES):
                copy(k, o, s).start()
        for k in range(3):
            copy(k).wait()

    return pl.pallas_call(
        body, name="rs_chips", out_shape=jax.ShapeDtypeStruct((3, half, LANES), BF16),
        in_specs=[_any_spec()], out_specs=_any_spec(),
        scratch_shapes=[pltpu.SemaphoreType.DMA((3,)), pltpu.SemaphoreType.DMA((3,))],
    )(a16)


def _swap_halves(v):
    def body(v_ref, out_ref, send_sem, recv_sem):
        x, y, c = _place()

        def give(hc):
            return pltpu.make_async_remote_copy(src_ref=v_ref, dst_ref=out_ref.at[hc], send_sem=send_sem, recv_sem=recv_sem,
                                                device_id=(x, y, 1 - c), device_id_type=MESH)

        give(c).start()
        give(c).wait_send()
        give(1 - c).wait_recv()

    other = pl.pallas_call(
        body, name="swap_halves", out_shape=jax.ShapeDtypeStruct((2,) + v.shape, v.dtype),
        in_specs=[_any_spec()], out_specs=_any_spec(),
        scratch_shapes=[pltpu.SemaphoreType.DMA, pltpu.SemaphoreType.DMA],
    )(v)
    both = lax.dynamic_update_slice(other, v[None], (lax.axis_index("c"), 0, 0))
    return both.reshape(2 * v.shape[0], v.shape[1])


WIDE = 1024
PACK_TILE = 3072
GRAD_ROWS_UNIT = 2 * PACK_TILE


def _pack(parts, rows_unit, dtype):
    flat = jnp.concatenate([q.reshape(-1).astype(dtype) for q in parts])
    unit = rows_unit * LANES
    total = -(-flat.shape[0] // unit) * unit
    return jnp.pad(flat, (0, total - flat.shape[0])).reshape(-1, LANES)


def _unpack(flat, shapes):
    out, off = [], 0
    for s in shapes:
        n = 1
        for q in s:
            n *= q
        out.append(flat[off:off + n].reshape(s))
        off += n
    return out


def _to_slots(name, full):
    dp, r, c = full.shape
    if BIG_SHARD_AXIS[name] == 1:
        return jnp.transpose(full.reshape(dp, r, 4, c // 4), (2, 0, 1, 3)).reshape(4, -1)
    return jnp.transpose(full.reshape(dp, 4, r // 4, c), (1, 0, 2, 3)).reshape(4, -1)


def _from_slots(name, slots, shard_shape):
    dp, r, c = shard_shape
    v = slots.reshape(4, dp, r, c)
    if BIG_SHARD_AXIS[name] == 1:
        return jnp.transpose(v, (1, 2, 0, 3)).reshape(dp, r, 4 * c)
    return jnp.transpose(v, (1, 0, 2, 3)).reshape(dp, 4 * r, c)


def _adamw(name, w, g, m, v, tile):
    def fn(i, wb, gb, mb, vb):
        m2 = ADAM_B1 * mb + (1.0 - ADAM_B1) * gb
        v2 = ADAM_B2 * vb + (1.0 - ADAM_B2) * (gb * gb)
        m_hat = m2 / (1.0 - ADAM_B1 ** ADAM_STEP)
        v_hat = v2 / (1.0 - ADAM_B2 ** ADAM_STEP)
        return -ADAM_LR * (m_hat / (jnp.sqrt(v_hat) + ADAM_EPS) + ADAM_WD * wb), m2, v2

    return _rows(name, fn, [(w, "row"), (g, "row"), (m, "row"), (v, "row")], [(w.shape, F32, "row")] * 3,
                 rows=w.shape[0], tile=tile)


def _as_2d(a):
    return a.reshape(-1, a.shape[-1])


def _adam_tile(rows):
    for t in (256, 184, 128, 64, 32, 16, 8):
        if rows % t == 0:
            return t
    return rows


def kernel(x, p, g_pre, w_in, s5_a_re, s5_a_im, s5_log_dt, s5_b_re, s5_b_im, s5_c_re, s5_c_im, s5_d, w_glu, w_bs, conv_w, conv_b, lru_w_a, lru_b_a, lru_w_x, lru_b_x, lru_lambda, w_bl, w_out, g_post, w_ple, w_ple_gate, loss_target, m_g_pre, m_w_in, m_s5_a_re, m_s5_a_im, m_s5_log_dt, m_s5_b_re, m_s5_b_im, m_s5_c_re, m_s5_c_im, m_s5_d, m_w_glu, m_w_bs, m_conv_w, m_conv_b, m_lru_w_a, m_lru_b_a, m_lru_w_x, m_lru_b_x, m_lru_lambda, m_w_bl, m_w_out, m_g_post, m_w_ple, m_w_ple_gate, v_g_pre, v_w_in, v_s5_a_re, v_s5_a_im, v_s5_log_dt, v_s5_b_re, v_s5_b_im, v_s5_c_re, v_s5_c_im, v_s5_d, v_w_glu, v_w_bs, v_conv_w, v_conv_b, v_lru_w_a, v_lru_b_a, v_lru_w_x, v_lru_b_x, v_lru_lambda, v_w_bl, v_w_out, v_g_post, v_w_ple, v_w_ple_gate):
    wts = dict(g_pre=g_pre, w_in=w_in, s5_a_re=s5_a_re, s5_a_im=s5_a_im, s5_log_dt=s5_log_dt, s5_b_re=s5_b_re,
               s5_b_im=s5_b_im, s5_c_re=s5_c_re, s5_c_im=s5_c_im, s5_d=s5_d, w_glu=w_glu, w_bs=w_bs, conv_w=conv_w,
               conv_b=conv_b, lru_w_a=lru_w_a, lru_b_a=lru_b_a, lru_w_x=lru_w_x, lru_b_x=lru_b_x, lru_lambda=lru_lambda,
               w_bl=w_bl, w_out=w_out, g_post=g_post, w_ple=w_ple, w_ple_gate=w_ple_gate)
    mom1 = dict(g_pre=m_g_pre, w_in=m_w_in, s5_a_re=m_s5_a_re, s5_a_im=m_s5_a_im, s5_log_dt=m_s5_log_dt, s5_b_re=m_s5_b_re,
                s5_b_im=m_s5_b_im, s5_c_re=m_s5_c_re, s5_c_im=m_s5_c_im, s5_d=m_s5_d, w_glu=m_w_glu, w_bs=m_w_bs,
                conv_w=m_conv_w, conv_b=m_conv_b, lru_w_a=m_lru_w_a, lru_b_a=m_lru_b_a, lru_w_x=m_lru_w_x, lru_b_x=m_lru_b_x,
                lru_lambda=m_lru_lambda, w_bl=m_w_bl, w_out=m_w_out, g_post=m_g_post, w_ple=m_w_ple, w_ple_gate=m_w_ple_gate)
    mom2 = dict(g_pre=v_g_pre, w_in=v_w_in, s5_a_re=v_s5_a_re, s5_a_im=v_s5_a_im, s5_log_dt=v_s5_log_dt, s5_b_re=v_s5_b_re,
                s5_b_im=v_s5_b_im, s5_c_re=v_s5_c_re, s5_c_im=v_s5_c_im, s5_d=v_s5_d, w_glu=v_w_glu, w_bs=v_w_bs,
                conv_w=v_conv_w, conv_b=v_conv_b, lru_w_a=v_lru_w_a, lru_b_a=v_lru_b_a, lru_w_x=v_lru_w_x, lru_b_x=v_lru_b_x,
                lru_lambda=v_lru_lambda, w_bl=v_w_bl, w_out=v_w_out, g_post=v_g_post, w_ple=v_w_ple, w_ple_gate=v_w_ple_gate)
    names = list(wts)

    def wire(name):
        return lax.bitcast_convert_type(wts[name], BF16) if name == "conv_w" else wts[name].astype(BF16)

    wire_shapes = [wire(k).shape for k in BIG]
    gathered = _gather_chips("gather_weights", _pack([wire(k) for k in BIG], 128, BF16), via_sibling=True)
    per_chip = [_unpack(gathered[j].reshape(-1), wire_shapes) for j in range(4)]
    whole = {}
    for idx, k in enumerate(BIG):
        if k == "w_in":
            continue
        v = jnp.stack([per_chip[j][idx] for j in range(4)])
        if k == "conv_w":
            v = lax.bitcast_convert_type(v, F32)
        whole[k] = _from_slots(k, v.reshape(4, -1), wts[k].shape)
    layers = []
    for i in range(DEPTH):
        wl = {k: whole[k][i] for k in BIG if k != "w_in"}
        wl["w_in"] = [per_chip[j][BIG.index("w_in")][i] for j in range(4)]
        wl.update({k: wts[k][i] for k in SMALL})
        layers.append(wl)

    loss, grad_x, grads = _local_step(x[0], p[:, 0], layers, loss_target[0])
    loss = lax.psum(loss, ("x", "y", "c"))

    def slots_of(k):
        if k == "w_in":
            return jnp.stack([jnp.stack([grads[i][k][j] for i in range(DEPTH)]).reshape(-1) for j in range(4)])
        return _to_slots(k, jnp.stack([grads[i][k] for i in range(DEPTH)]))

    big_slots = jnp.concatenate([slots_of(k) for k in BIG], axis=1)
    small_shapes = [wts[k].shape for k in SMALL]
    small_flat = jnp.concatenate([jnp.stack([grads[i][k] for i in range(DEPTH)]).reshape(-1) for k in SMALL])
    n_small = small_flat.shape[0]
    small_q = -(-n_small // (4 * 8 * WIDE)) * 8 * WIDE
    small_slots = jnp.pad(small_flat, (0, 4 * small_q - n_small)).reshape(4, small_q)
    n_big = big_slots.shape[1]
    n_big_pad = -(-n_big // (8 * WIDE)) * 8 * WIDE
    n_slot = n_big_pad + small_q
    unit = GRAD_ROWS_UNIT * LANES
    n_slot_pad = -(-n_slot // unit) * unit
    gr = jnp.concatenate([jnp.pad(big_slots, ((0, 0), (0, n_big_pad - n_big))), small_slots,
                          jnp.zeros((4, n_slot_pad - n_slot), F32)], axis=1).reshape(4, -1, LANES)
    got = _rs_sibling(gr)
    half = got.shape[1]
    mine = lax.dynamic_slice_in_dim(gr, lax.axis_index("c") * half, half, axis=1)
    rows2d = lambda a: a.reshape(-1, LANES)

    def f_add1(i, a, b):
        s = a + b
        return s, s

    a32, a16 = _rows("rs_add1", f_add1, [(rows2d(mine), "row"), (rows2d(got), "row")],
                     [((4 * half, LANES), F32, "row"), ((4 * half, LANES), BF16, "row")], rows=4 * half, tile=PACK_TILE)
    got3 = _rs_chips(a16.reshape(4, half, LANES))
    own = lax.dynamic_index_in_dim(a32.reshape(4, half, LANES), 2 * lax.axis_index("x") + lax.axis_index("y"), 0, keepdims=False)

    def f_add2(i, o, g0, g1, g2):
        return (((o + g0.astype(F32)) + g1.astype(F32)) + g2.astype(F32),)

    (red_half,) = _rows("rs_add2", f_add2, [(own, "row")] + [(got3[k], "row") for k in range(3)],
                        [((half, LANES), F32, "row")], rows=half, tile=PACK_TILE)
    red = _swap_halves(red_half).reshape(-1)
    small_red = _gather_chips("gather_small", red[n_big_pad:n_big_pad + small_q].reshape(-1, LANES), via_sibling=False)
    small_red = small_red.reshape(-1)[:n_small]

    big_shapes = [wts[k].shape for k in BIG]
    grad_out = dict(zip(BIG, _unpack(red[:n_big], big_shapes)))
    grad_out.update(zip(SMALL, _unpack(small_red, small_shapes)))
    delta, new_m, new_v = {}, {}, {}
    for k in BIG:
        w2 = _as_2d(wts[k])
        res = _adamw("adamw_" + k, w2, _as_2d(grad_out[k]), _as_2d(mom1[k]), _as_2d(mom2[k]), _adam_tile(w2.shape[0]))
        delta[k], new_m[k], new_v[k] = [r.reshape(wts[k].shape) for r in res]
    pack_small = lambda d: jnp.pad(jnp.concatenate([d[k].reshape(-1) for k in SMALL]), (0, 4 * small_q - n_small)).reshape(-1, WIDE)
    res = _adamw("adamw_small", pack_small(wts), pack_small(grad_out), pack_small(mom1), pack_small(mom2),
                 _adam_tile(4 * small_q // WIDE))
    for d, r in zip((delta, new_m, new_v), res):
        d.update(zip(SMALL, _unpack(r.reshape(-1), small_shapes)))
    return (loss, grad_x[None], *[grad_out[k] for k in names], *[delta[k] for k in names],
            *[new_m[k] for k in names], *[new_v[k] for k in names])
```

```python
import jax
import jax.numpy as jnp
from jax import lax
from jax.experimental import pallas as pl
from jax.experimental.pallas import tpu as pltpu

F32 = jnp.float32
BF16 = jnp.bfloat16
MESH = pl.DeviceIdType.MESH

DEPTH = 2
D_MODEL = 1024
NORM_EPS = 1e-6
S5_WIDTH = 512
S5_GROUPS = 32
S5_GROUP = 16
S5_STATE = 64
LRU_WIDTH = 1280
LRU_HEADS = 10
LRU_HEAD_DIM = 128
LRU_C = 8.0
CONV_WIDTH = 4
PLE_DIM = 256
IN_WIDTHS = (S5_WIDTH, S5_WIDTH, LRU_WIDTH, LRU_WIDTH, D_MODEL, D_MODEL)
IN_OFFSETS = (0, 512, 1024, 2304, 3584, 4608, 5632)
IN_SLOT = 5632 // 4
ADAM_LR = 0.001
ADAM_B1 = 0.9
ADAM_B2 = 0.999
ADAM_EPS = 1e-08
ADAM_WD = 0.01
ADAM_STEP = 10

SUBLANES = 8
LANES = 128
S5_HALF_IN = S5_WIDTH // 2
S5_CPLX = S5_GROUPS * S5_STATE
S5_HALF_CPLX = S5_CPLX // 2
S5_LANES = 2 * S5_CPLX
VMEM_LIMIT = 48 * 2 ** 20
ROW_TILE = 256


def _sigmoid(x):
    return 1.0 / (1.0 + jnp.exp(-x))


def _gelu_parts(x):
    k = 0.7978845608028654
    t = jnp.tanh(k * (x + 0.044715 * x * x * x))
    val = 0.5 * x * (1.0 + t)
    grad = 0.5 * (1.0 + t) + 0.5 * x * (1.0 - t * t) * k * (1.0 + 3.0 * 0.044715 * x * x)
    return val, grad


def _nn(a, w):
    return jnp.dot(a.astype(BF16), w.astype(BF16), preferred_element_type=F32)


def _nt(a, w):
    return lax.dot_general(a.astype(BF16), w.astype(BF16), (((1,), (1,)), ((), ())), preferred_element_type=F32)


def _tn(a, b):
    return lax.dot_general(a.astype(BF16), b.astype(BF16), (((0,), (0,)), ((), ())), preferred_element_type=F32)


def _heads(op, a, w):
    d = LRU_HEAD_DIM
    return jnp.concatenate([op(a[:, h * d:(h + 1) * d], w[h]) for h in range(LRU_HEADS)], axis=1)


def _heads_tn(a, b):
    d = LRU_HEAD_DIM
    return jnp.stack([_tn(a[:, h * d:(h + 1) * d], b[:, h * d:(h + 1) * d]) for h in range(LRU_HEADS)], axis=0)


def _rows_before(x, halo, s):
    main = pltpu.roll(x, s, 0)
    head = pltpu.roll(jnp.concatenate([halo, x[0:SUBLANES]], axis=0), s, 0)[SUBLANES:2 * SUBLANES]
    return jnp.concatenate([head, main[SUBLANES:]], axis=0)


def _rows_after(x, halo, s):
    n = x.shape[0]
    main = pltpu.roll(x, n - s, 0)
    tail = pltpu.roll(jnp.concatenate([x[n - SUBLANES:], halo], axis=0), 2 * SUBLANES - s, 0)[0:SUBLANES]
    return jnp.concatenate([main[:n - SUBLANES], tail], axis=0)


def _rows(name, fn, ins, outs, *, rows, tile):
    tile = min(tile, rows)
    n = rows // tile
    assert n * tile == rows, (name, rows, tile)
    in_specs = []
    for arr, kind in ins:
        halo = SUBLANES * (4 // arr.dtype.itemsize)
        per, last = tile // halo, rows // halo - 1
        if kind in ("row", "raw"):
            in_specs.append(pl.BlockSpec((tile, arr.shape[1]), lambda i: (i, 0)))
        elif kind == "prev":
            in_specs.append(pl.BlockSpec((halo, arr.shape[1]), lambda i, per=per: (jnp.maximum(i * per - 1, 0), 0)))
        elif kind == "next":
            in_specs.append(pl.BlockSpec((halo, arr.shape[1]),
                                         lambda i, per=per, last=last: (jnp.minimum((i + 1) * per, last), 0)))
        else:
            in_specs.append(pl.BlockSpec(arr.shape, lambda i, nd=arr.ndim: (0,) * nd))
    out_shape, out_specs = [], []
    for shape, dtype, kind in outs:
        out_shape.append(jax.ShapeDtypeStruct(shape, dtype))
        if kind == "row":
            out_specs.append(pl.BlockSpec((tile, shape[1]), lambda i: (i, 0)))
        else:
            out_specs.append(pl.BlockSpec(shape, lambda i, nd=len(shape): (0,) * nd))
    n_in = len(ins)

    def load(ref, kind):
        v = ref[...]
        if kind in ("row", "prev", "next"):
            v = v.astype(F32)
        if kind == "prev":
            v = v[v.shape[0] - SUBLANES:]
        if kind == "next":
            v = v[:SUBLANES]
        return v

    def body(*refs):
        i = pl.program_id(0)
        vals = fn(i, *[load(r, kind) for r, (_, kind) in zip(refs[:n_in], ins)])
        assert len(vals) == len(outs), name
        for r, v, (_, _, kind) in zip(refs[n_in:], vals, outs):
            if kind == "row":
                r[...] = v.astype(r.dtype)
            else:
                @pl.when(i == 0)
                def _():
                    r[...] = jnp.zeros_like(r)

                r[...] += v.astype(r.dtype)

    return pl.pallas_call(
        body, name=name, grid=(n,), in_specs=in_specs, out_specs=out_specs, out_shape=out_shape,
        compiler_params=pltpu.CompilerParams(dimension_semantics=("arbitrary",), vmem_limit_bytes=VMEM_LIMIT),
    )(*[a for a, _ in ins])


def _s5_discretise(are, aim, ldt, bre, bim):
    dt = jnp.exp(ldt)
    er = jnp.exp(are * dt)
    abr = er * jnp.cos(aim * dt)
    abi = er * jnp.sin(aim * dt)
    den = are * are + aim * aim
    zr = ((abr - 1.0) * are + abi * aim) / den
    zi = (abi * are - (abr - 1.0) * aim) / den
    return abr, abi, zr * bre - zi * bim, zr * bim + zi * bre


def _s5_prep(are, aim, ldt, bre, bim):
    def body(a, b, c, d, e, o0, o1, o2, o3):
        r = _s5_discretise(a[...], b[...], c[...], d[...], e[...])
        o0[...], o1[...], o2[...], o3[...] = r

    sd = jax.ShapeDtypeStruct(are.shape, F32)
    return pl.pallas_call(body, name="s5_prep", out_shape=[sd] * 4)(are, aim, ldt, bre, bim)


def _s5_prep_bwd(are, aim, ldt, bre, bim, cts):
    def body(a, b, c, d, e, c0, c1, c2, c3, o0, o1, o2, o3, o4):
        _, vjp = jax.vjp(_s5_discretise, a[...], b[...], c[...], d[...], e[...])
        r = vjp((c0[...], c1[...], c2[...], c3[...]))
        o0[...], o1[...], o2[...], o3[...], o4[...] = r

    sd = jax.ShapeDtypeStruct(are.shape, F32)
    return pl.pallas_call(body, name="s5_prep_bwd", out_shape=[sd] * 5)(are, aim, ldt, bre, bim, *cts)


def _s5_consts(abr, abi, seg):
    shape = (SUBLANES, S5_CPLX)
    assert seg & (seg - 1) == 0 and seg % SUBLANES == 0, seg

    def body(ar_ref, ai_ref, f_ref, b_ref):
        def cmul(p, q):
            return (p[0] * q[0] - p[1] * q[1], p[0] * q[1] + p[1] * q[0])

        row = lax.broadcasted_iota(jnp.int32, shape, 0)
        a1 = (jnp.broadcast_to(ar_ref[...], shape), jnp.broadcast_to(ai_ref[...], shape))
        squares = [a1]
        while 1 << (len(squares) - 1) < 4 * seg:
            squares.append(cmul(squares[-1], squares[-1]))
        nb = seg.bit_length() - 1
        fwd, rev = [], []
        for k, a in ((1, squares[nb]), (2, squares[nb + 1]), (4, squares[nb + 2])):
            fwd += [jnp.where(row >= k, a[0], 0.0), jnp.where(row >= k, a[1], 0.0)]
            rev += [jnp.where(row <= 7 - k, a[0], 0.0), jnp.where(row <= 7 - k, -a[1], 0.0)]
        fwd += [a1[0], a1[1]]
        rev += [a1[0], -a1[1]]
        e = lax.broadcasted_iota(jnp.int32, (seg, S5_CPLX), 0) + 1
        wide = lambda v: jnp.broadcast_to(v[0:1, :], (seg, S5_CPLX))
        pr, pi = jnp.ones((seg, S5_CPLX), F32), jnp.zeros((seg, S5_CPLX), F32)
        for b in range(nb + 1):
            sr, si = wide(squares[b][0]), wide(squares[b][1])
            bit = ((e >> b) & 1) == 1
            pr, pi = jnp.where(bit, pr * sr - pi * si, pr), jnp.where(bit, pr * si + pi * sr, pi)
        f_ref[...] = jnp.concatenate(fwd + [pr, pi], axis=0)
        b_ref[...] = jnp.concatenate(rev + [pr, -pi], axis=0)

    sd = jax.ShapeDtypeStruct((8 * SUBLANES + 2 * seg, S5_CPLX), F32)
    return pl.pallas_call(body, name="s5_consts", out_shape=[sd, sd])(abr, abi)


S5_TILES = S5_LANES // LANES
S5_HALF_TILES = S5_TILES // 2


def _s5_tile_index(q):
    re = (q // 8) * S5_HALF_TILES + (q % 8)
    return re, re + S5_HALF_TILES // 2


def _lanes_of(ref, first, count):
    return jnp.concatenate([ref[j] for j in range(first, first + count)], axis=1)


def _to_lane_tiles(ref, first, value):
    for j in range(value.shape[1] // LANES):
        ref[first + j] = value[:, j * LANES:(j + 1) * LANES]


def _time_perm(tile, transpose=False):
    seg = tile // SUBLANES
    rho = lax.broadcasted_iota(jnp.int32, (tile, tile), 1 if transpose else 0)
    t = lax.broadcasted_iota(jnp.int32, (tile, tile), 0 if transpose else 1)
    return (t == (rho & (SUBLANES - 1)) * seg + (rho >> 3)).astype(BF16)


def _reorder(perm, x):
    out = None
    for _ in range(1 if x.dtype == BF16 else 3):
        piece = x.astype(BF16)
        part = jnp.dot(perm, piece, preferred_element_type=F32)
        out = part if out is None else out + part
        x = x - piece.astype(x.dtype)
    return out


def _s5_scan(s_ref, sc_ref, carry_ref, tile, reverse):
    seg = tile // SUBLANES
    group = 4
    edge = 0 if reverse else SUBLANES - 1
    row = lax.broadcasted_iota(jnp.int32, (SUBLANES, LANES), 0)
    order = range(seg - 1, -1, -1) if reverse else range(seg)
    rows_of = lambda k: pl.ds(k * SUBLANES, SUBLANES)
    base = 8 * SUBLANES

    for q0 in range(0, S5_CPLX // LANES, group):
        qs = list(range(q0, q0 + group))
        tiles = [_s5_tile_index(q) for q in qs]
        cst = lambda k, q: sc_ref[k * SUBLANES:(k + 1) * SUBLANES, q * LANES:(q + 1) * LANES]
        state = [(jnp.zeros((SUBLANES, LANES), F32), jnp.zeros((SUBLANES, LANES), F32)) for _ in qs]
        mult = [(cst(6, q), cst(7, q)) for q in qs]
        for k in order:
            for j, (re, im) in enumerate(tiles):
                ar, ai = mult[j]
                xr, xi = state[j]
                nr = ar * xr - ai * xi + s_ref[re, rows_of(k), :]
                ni = ar * xi + ai * xr + s_ref[im, rows_of(k), :]
                s_ref[re, rows_of(k), :] = nr
                s_ref[im, rows_of(k), :] = ni
                state[j] = (nr, ni)
        start = []
        for j, (q, (re, im)) in enumerate(zip(qs, tiles)):
            er, ei = state[j]
            shift1 = SUBLANES - 1 if reverse else 1
            dr = jnp.where(row == SUBLANES - 1 - edge, carry_ref[re], pltpu.roll(er, shift1, 0))
            di = jnp.where(row == SUBLANES - 1 - edge, carry_ref[im], pltpu.roll(ei, shift1, 0))
            for c, sh in ((0, 1), (2, 2), (4, 4)):
                shift = SUBLANES - sh if reverse else sh
                ar, ai = cst(c, q), cst(c + 1, q)
                sr, si = pltpu.roll(dr, shift, 0), pltpu.roll(di, shift, 0)
                dr, di = dr + ar * sr - ai * si, di + ar * si + ai * sr
            start.append((dr, di))
        for k in order:
            t = seg - 1 - k if reverse else k
            for j, (q, (re, im)) in enumerate(zip(qs, tiles)):
                lanes = slice(q * LANES, (q + 1) * LANES)
                pr = jnp.broadcast_to(sc_ref[base + t:base + t + 1, lanes], (SUBLANES, LANES))
                pi = jnp.broadcast_to(sc_ref[base + seg + t:base + seg + t + 1, lanes], (SUBLANES, LANES))
                cr, ci = start[j]
                xr = s_ref[re, rows_of(k), :] + pr * cr - pi * ci
                xi = s_ref[im, rows_of(k), :] + pr * ci + pi * cr
                s_ref[re, rows_of(k), :] = xr
                s_ref[im, rows_of(k), :] = xi
                if k == order[-1]:
                    carry_ref[re] = jnp.broadcast_to(xr[edge:edge + 1, :], (SUBLANES, LANES))
                    carry_ref[im] = jnp.broadcast_to(xi[edge:edge + 1, :], (SUBLANES, LANES))


def _s5_fwd(u, bd, cdt, dskip, sc, *, rows, tile):
    n = rows // tile

    def body(u_ref, bd_ref, cdt_ref, d_ref, sc_ref, y_ref, s_ref, carry_ref):
        @pl.when(pl.program_id(0) == 0)
        def _():
            carry_ref[...] = jnp.zeros_like(carry_ref)

        ub = _reorder(_time_perm(tile), u_ref[...].astype(BF16)).astype(BF16)
        for h in range(2):
            _to_lane_tiles(s_ref, h * S5_HALF_TILES, jnp.dot(ub[:, h * S5_HALF_IN:(h + 1) * S5_HALF_IN], bd_ref[h],
                                                             preferred_element_type=F32))
        _s5_scan(s_ref, sc_ref, carry_ref, tile, reverse=False)
        ys = [_nt(_lanes_of(s_ref, h * S5_HALF_TILES, S5_HALF_TILES), cdt_ref[h]) for h in range(2)]
        y_ref[...] = _reorder(_time_perm(tile, transpose=True), jnp.concatenate(ys, axis=1)) + d_ref[...] * u_ref[...]

    full = lambda a: pl.BlockSpec(a.shape, lambda i, nd=a.ndim: (0,) * nd)
    return pl.pallas_call(
        body, name="s5_fwd", grid=(n,),
        in_specs=[pl.BlockSpec((tile, S5_WIDTH), lambda i: (i, 0)), full(bd), full(cdt), full(dskip), full(sc)],
        out_specs=[pl.BlockSpec((tile, S5_WIDTH), lambda i: (i, 0)),
                   pl.BlockSpec((S5_TILES, tile, LANES), lambda i: (0, i, 0))],
        out_shape=[jax.ShapeDtypeStruct((rows, S5_WIDTH), F32), jax.ShapeDtypeStruct((S5_TILES, rows, LANES), F32)],
        scratch_shapes=[pltpu.VMEM((S5_TILES, SUBLANES, LANES), F32)],
        compiler_params=pltpu.CompilerParams(dimension_semantics=("arbitrary",), vmem_limit_bytes=VMEM_LIMIT),
    )(u, bd, cdt, dskip, sc)


def _s5_bwd(dy, s, u, bd, cdt, dskip, sc, *, rows, tile):
    n = rows // tile
    hc = 2 * S5_HALF_CPLX
    per8 = tile // SUBLANES
    quarter = S5_HALF_TILES // 2

    def body(dy_ref, s_ref, sp_ref, u_ref, bd_ref, cdt_ref, d_ref, sc_ref,
             du_ref, dbd_ref, dcdt_ref, dd_ref, da_ref, g_ref, carry_ref):
        i = pl.program_id(0)

        @pl.when(i == 0)
        def _():
            carry_ref[...] = jnp.zeros_like(carry_ref)
            dbd_ref[...] = jnp.zeros_like(dbd_ref)
            dcdt_ref[...] = jnp.zeros_like(dcdt_ref)
            dd_ref[...] = jnp.zeros_like(dd_ref)
            da_ref[...] = jnp.zeros_like(da_ref)

        dy = dy_ref[...]
        u = u_ref[...]
        perm = _time_perm(tile)
        dyb = _reorder(perm, dy.astype(BF16)).astype(BF16)
        ub = _reorder(perm, u.astype(BF16)).astype(BF16)
        for h in range(2):
            _to_lane_tiles(g_ref, h * S5_HALF_TILES, jnp.dot(dyb[:, h * S5_HALF_IN:(h + 1) * S5_HALF_IN], cdt_ref[h],
                                                             preferred_element_type=F32))
        _s5_scan(g_ref, sc_ref, carry_ref, tile, reverse=True)
        dus = []
        for h in range(2):
            gb = _lanes_of(g_ref, h * S5_HALF_TILES, S5_HALF_TILES).astype(BF16)
            sb = _lanes_of(s_ref, h * S5_HALF_TILES, S5_HALF_TILES).astype(BF16)
            dus.append(_nt(gb, bd_ref[h]))
            dbd_ref[h] += _tn(ub[:, h * S5_HALF_IN:(h + 1) * S5_HALF_IN], gb)
            dcdt_ref[h] += _tn(dyb[:, h * S5_HALF_IN:(h + 1) * S5_HALF_IN], sb)
        du = _reorder(_time_perm(tile, transpose=True), jnp.concatenate(dus, axis=1))
        du_ref[...] = (du + d_ref[...] * dy).astype(du_ref.dtype)
        dd_ref[...] += jnp.sum(dy * u, axis=0, keepdims=True)

        not_first = (i < n - 1).astype(F32)
        row = lax.broadcasted_iota(jnp.int32, (SUBLANES, quarter * LANES), 0)

        def step_before(first):
            cur = _lanes_of(s_ref, first, quarter)
            before_tile = _lanes_of(sp_ref, first, quarter)[SUBLANES - 1:SUBLANES, :] * not_first
            head = jnp.where(row == 0, before_tile, pltpu.roll(cur[tile - SUBLANES:], 1, 0))
            return jnp.concatenate([head, cur[:tile - SUBLANES]], axis=0)

        for h in range(2):
            re, im = h * S5_HALF_TILES, h * S5_HALF_TILES + quarter
            ssr = step_before(re)
            ssi = step_before(im)
            gr = _lanes_of(g_ref, re, quarter)
            gi = _lanes_of(g_ref, im, quarter)
            lanes = slice(h * S5_HALF_CPLX, (h + 1) * S5_HALF_CPLX)
            da_ref[0:1, lanes] += jnp.sum(ssr * gr + ssi * gi, axis=0, keepdims=True)
            da_ref[1:2, lanes] += jnp.sum(ssr * gi - ssi * gr, axis=0, keepdims=True)

    full = lambda a: pl.BlockSpec(a.shape, lambda i, nd=a.ndim: (0,) * nd)
    rev = lambda i: (n - 1 - i, 0)
    wshape = (2, S5_HALF_IN, hc)
    return pl.pallas_call(
        body, name="s5_bwd", grid=(n,),
        in_specs=[pl.BlockSpec((tile, S5_WIDTH), rev), pl.BlockSpec((S5_TILES, tile, LANES), lambda i: (0, n - 1 - i, 0)),
                  pl.BlockSpec((S5_TILES, SUBLANES, LANES), lambda i: (0, jnp.maximum((n - 1 - i) * per8 - 1, 0), 0)),
                  pl.BlockSpec((tile, S5_WIDTH), rev), full(bd), full(cdt), full(dskip), full(sc)],
        out_specs=[pl.BlockSpec((tile, S5_WIDTH), rev),
                   pl.BlockSpec(wshape, lambda i: (0, 0, 0)), pl.BlockSpec(wshape, lambda i: (0, 0, 0)),
                   pl.BlockSpec((1, S5_WIDTH), lambda i: (0, 0)), pl.BlockSpec((SUBLANES, S5_CPLX), lambda i: (0, 0))],
        out_shape=[jax.ShapeDtypeStruct((rows, S5_WIDTH), BF16), jax.ShapeDtypeStruct(wshape, F32),
                   jax.ShapeDtypeStruct(wshape, F32), jax.ShapeDtypeStruct((1, S5_WIDTH), F32),
                   jax.ShapeDtypeStruct((SUBLANES, S5_CPLX), F32)],
        scratch_shapes=[pltpu.VMEM((S5_TILES, tile, LANES), F32), pltpu.VMEM((S5_TILES, SUBLANES, LANES), F32)],
        compiler_params=pltpu.CompilerParams(dimension_semantics=("arbitrary",), vmem_limit_bytes=VMEM_LIMIT),
    )(dy, s, s, u, bd, cdt, dskip, sc)


def _s5_block_diag(parts):
    v = jnp.stack(parts, axis=2).reshape(2, 16, S5_GROUP, 2, S5_STATE)
    eye = jnp.eye(16, dtype=v.dtype)
    return jnp.einsum("hgcpn,gk->hgcpkn", v, eye).reshape(2, S5_HALF_IN, 2 * S5_HALF_CPLX)


def _s5_block_diag_extract(m):
    v = m.reshape(2, 16, S5_GROUP, 2, 16, S5_STATE)
    d = jnp.diagonal(v, axis1=1, axis2=4)
    d = jnp.transpose(d, (2, 0, 4, 1, 3)).reshape(2, S5_GROUPS, S5_GROUP, S5_STATE)
    return d[0], d[1]


def _cplx_to_lanes(v):
    return v.reshape(1, S5_CPLX)


def _lru_scan_fwd(a, b, *, rows, tile):
    n = rows // tile
    nblk = tile // SUBLANES
    group = 5

    def body(a_ref, b_ref, h_ref, carry_ref):
        @pl.when(pl.program_id(0) == 0)
        def _():
            carry_ref[...] = jnp.zeros_like(carry_ref)

        row = lax.broadcasted_iota(jnp.int32, (SUBLANES, LANES), 0)
        for q0 in range(0, LRU_WIDTH // LANES, group):
            offs = [q * LANES for q in range(q0, q0 + group)]

            def blk(t, carry, offs=offs):
                r0 = pl.multiple_of(t * SUBLANES, SUBLANES)
                new = []
                for j, o in enumerate(offs):
                    av = a_ref[pl.ds(r0, SUBLANES), o:o + LANES]
                    xv = b_ref[pl.ds(r0, SUBLANES), o:o + LANES]
                    for sh in (1, 2, 4):
                        m = row >= sh
                        xs = pltpu.roll(xv, sh, 0)
                        asft = pltpu.roll(av, sh, 0)
                        xv = xv + jnp.where(m, av * xs, 0.0)
                        av = jnp.where(m, av * asft, av)
                    hv = xv + av * carry[j]
                    h_ref[pl.ds(r0, SUBLANES), o:o + LANES] = hv
                    new.append(jnp.broadcast_to(hv[SUBLANES - 1:SUBLANES, :], (SUBLANES, LANES)))
                return tuple(new)

            carry = lax.fori_loop(0, nblk, blk, tuple(carry_ref[:, o:o + LANES] for o in offs), unroll=2)
            for j, o in enumerate(offs):
                carry_ref[:, o:o + LANES] = carry[j]

    spec = pl.BlockSpec((tile, LRU_WIDTH), lambda i: (i, 0))
    return pl.pallas_call(
        body, name="lru_scan_fwd", grid=(n,), in_specs=[spec, spec], out_specs=spec,
        out_shape=jax.ShapeDtypeStruct((rows, LRU_WIDTH), F32),
        scratch_shapes=[pltpu.VMEM((SUBLANES, LRU_WIDTH), F32)],
        compiler_params=pltpu.CompilerParams(dimension_semantics=("arbitrary",), vmem_limit_bytes=VMEM_LIMIT),
    )(a, b)


def _lru_scan_bwd(dh, a, *, rows, tile):
    n = rows // tile
    nblk = tile // SUBLANES
    group = 5

    def body(dh_ref, a_ref, g_ref, cg_ref, ca_ref):
        @pl.when(pl.program_id(0) == 0)
        def _():
            cg_ref[...] = jnp.zeros_like(cg_ref)
            ca_ref[...] = jnp.zeros_like(ca_ref)

        row = lax.broadcasted_iota(jnp.int32, (SUBLANES, LANES), 0)
        for q0 in range(0, LRU_WIDTH // LANES, group):
            offs = [q * LANES for q in range(q0, q0 + group)]

            def blk(t, carry, offs=offs):
                r0 = pl.multiple_of((nblk - 1 - t) * SUBLANES, SUBLANES)
                new = []
                for j, o in enumerate(offs):
                    cg, ca = carry[2 * j], carry[2 * j + 1]
                    araw = a_ref[pl.ds(r0, SUBLANES), o:o + LANES]
                    xv = dh_ref[pl.ds(r0, SUBLANES), o:o + LANES]
                    av = jnp.where(row == SUBLANES - 1, ca, pltpu.roll(araw, SUBLANES - 1, 0))
                    for sh in (1, 2, 4):
                        m = row <= SUBLANES - 1 - sh
                        xs = pltpu.roll(xv, SUBLANES - sh, 0)
                        asft = pltpu.roll(av, SUBLANES - sh, 0)
                        xv = xv + jnp.where(m, av * xs, 0.0)
                        av = jnp.where(m, av * asft, av)
                    gv = xv + av * cg
                    g_ref[pl.ds(r0, SUBLANES), o:o + LANES] = gv
                    new.append(jnp.broadcast_to(gv[0:1, :], (SUBLANES, LANES)))
                    new.append(jnp.broadcast_to(araw[0:1, :], (SUBLANES, LANES)))
                return tuple(new)

            carry0 = tuple(r[:, o:o + LANES] for o in offs for r in (cg_ref, ca_ref))
            carry = lax.fori_loop(0, nblk, blk, carry0, unroll=2)
            for j, o in enumerate(offs):
                cg_ref[:, o:o + LANES] = carry[2 * j]
                ca_ref[:, o:o + LANES] = carry[2 * j + 1]

    spec = pl.BlockSpec((tile, LRU_WIDTH), lambda i: (n - 1 - i, 0))
    return pl.pallas_call(
        body, name="lru_scan_bwd", grid=(n,), in_specs=[spec, spec], out_specs=spec,
        out_shape=jax.ShapeDtypeStruct((rows, LRU_WIDTH), F32),
        scratch_shapes=[pltpu.VMEM((SUBLANES, LRU_WIDTH), F32), pltpu.VMEM((SUBLANES, LRU_WIDTH), F32)],
        compiler_params=pltpu.CompilerParams(dimension_semantics=("arbitrary",), vmem_limit_bytes=VMEM_LIMIT),
    )(dh, a)


def _conv_fwd(i, x, prev, cw, cb):
    prev = prev * (i > 0).astype(F32)
    y = x * cw[3:4, :] + cb
    for s in range(1, CONV_WIDTH):
        y = y + _rows_before(x, prev, s) * cw[3 - s:4 - s, :]
    return y


def _lru_gates(c, wa, ba, wx, bx, lam):
    r = _sigmoid(_heads(_nn, c, wa) + ba)
    ig = _sigmoid(_heads(_nn, c, wx) + bx)
    z = -lam
    sp = jnp.maximum(z, 0.0) + jnp.log(1.0 + jnp.exp(-jnp.abs(z)))
    log_a = -LRU_C * r * sp
    a = jnp.exp(log_a)
    z2 = 2.0 * log_a
    series = -z2 * (1.0 + z2 * (0.5 + z2 * (1.0 / 6.0 + z2 * (1.0 / 24.0 + z2 * (1.0 / 120.0 + z2 / 720.0)))))
    one_minus = jnp.where(z2 > -0.2, series, 1.0 - jnp.exp(z2))
    mult = jnp.sqrt(one_minus)
    return r, ig, sp, a, mult


def _layer_fwd(x, p, w, rows):
    tile = ROW_TILE
    d = D_MODEL

    def f_in(i, xb, g, *ws):
        rstd = lax.rsqrt(jnp.mean(xb * xb, axis=-1, keepdims=True) + NORM_EPS)
        hb = (xb * rstd * g).astype(BF16)
        proj = jnp.concatenate([jnp.dot(hb, wj, preferred_element_type=F32) for wj in ws], axis=1)
        return tuple(proj[:, IN_OFFSETS[k]:IN_OFFSETS[k + 1]] for k in range(6)) + (hb,)

    s5x, s5g, lrux, lrug, gs, gl, h = _rows(
        "f_in", f_in, [(x, "row"), (w["g_pre"], "full")] + [(wc, "full") for wc in w["w_in"]],
        [((rows, wd), BF16, "row") for wd in IN_WIDTHS] + [((rows, d), BF16, "row")], rows=rows, tile=tile)

    ys, st = _s5_fwd(s5x, w["bd"], w["cdt"], w["s5_d"], w["scf"], rows=rows, tile=tile)

    def f_s5post(i, ysb, gb, wglu, wbs):
        glv, _ = _gelu_parts(ysb)
        glu = _nn(glv, wglu)
        y2 = glu[:, :S5_WIDTH] * _sigmoid(glu[:, S5_WIDTH:]) * (gb * _sigmoid(gb))
        return (_nn(y2, wbs),)

    (z_s,) = _rows("f_s5post", f_s5post, [(ys, "row"), (s5g, "row"), (w["w_glu"], "full"), (w["w_bs"], "full")],
                   [((rows, d), BF16, "row")], rows=rows, tile=tile)

    def f_gates(i, xb, prev, cw, cb, wa, ba, wx, bx, lam):
        c = _conv_fwd(i, xb, prev, cw, cb)
        _, ig, _, a, mult = _lru_gates(c, wa, ba, wx, bx, lam)
        return a, mult * (ig * c)

    a, b = _rows("f_gates", f_gates,
                 [(lrux, "row"), (lrux, "prev"), (w["conv_w"], "full"), (w["conv_b"], "full"), (w["lru_w_a"], "full"),
                  (w["lru_b_a"], "full"), (w["lru_w_x"], "full"), (w["lru_b_x"], "full"), (w["lru_lambda"], "full")],
                 [((rows, LRU_WIDTH), F32, "row")] * 2, rows=rows, tile=tile)
    hl = _lru_scan_fwd(a, b, rows=rows, tile=tile)

    def f_merge(i, hb, lg, zs, gsb, glb, xb, wbl, wout, gpost):
        z_l = _nn(hb * (lg * _sigmoid(lg)), wbl)
        merged = _sigmoid(gsb) * zs + _sigmoid(glb) * z_l
        mix = _nn(merged, wout)
        rstd = lax.rsqrt(jnp.mean(mix * mix, axis=-1, keepdims=True) + NORM_EPS)
        return xb + mix * rstd * gpost, mix, z_l

    x1, mix, z_l = _rows("f_merge", f_merge,
                         [(hl, "row"), (lrug, "row"), (z_s, "row"), (gs, "row"), (gl, "row"), (x, "row"),
                          (w["w_bl"], "full"), (w["w_out"], "full"), (w["g_post"], "full")],
                         [((rows, d), F32, "row"), ((rows, d), BF16, "row"), ((rows, d), BF16, "row")], rows=rows, tile=tile)

    def f_ple(i, x1b, pb, wple, wpg):
        return (x1b + _nn(pb, wple) * _sigmoid(_nn(x1b, wpg)),)

    (x2,) = _rows("f_ple", f_ple, [(x1, "row"), (p, "row"), (w["w_ple"], "full"), (w["w_ple_gate"], "full")],
                  [((rows, d), F32, "row")], rows=rows, tile=tile)
    saved = dict(x=x, h=h, s5x=s5x, s5g=s5g, lrux=lrux, lrug=lrug, gs=gs, gl=gl, ys=ys, st=st, a=a, hl=hl, z_s=z_s,
                 z_l=z_l, mix=mix, x1=x1, p=p)
    return x2, saved


def _layer_bwd(dx2, sv, w, rows):
    tile = ROW_TILE
    d = D_MODEL
    g = {}

    def b_ple(i, dxb, x1b, pb, wple, wpg):
        pe = _nn(pb, wple)
        sg = _sigmoid(_nn(x1b, wpg))
        dpe = dxb * sg
        dgt = dxb * pe * sg * (1.0 - sg)
        return dxb + _nt(dgt, wpg), _tn(pb, dpe), _tn(x1b, dgt)

    dx1, g["w_ple"], g["w_ple_gate"] = _rows(
        "b_ple", b_ple, [(dx2, "row"), (sv["x1"], "row"), (sv["p"], "row"), (w["w_ple"], "full"), (w["w_ple_gate"], "full")],
        [((rows, d), F32, "row"), ((PLE_DIM, d), F32, "acc"), ((d, d), F32, "acc")], rows=rows, tile=tile)

    def b_merge(i, dxb, mixb, zs, zl, gsb, glb, wout, gpost):
        rstd = lax.rsqrt(jnp.mean(mixb * mixb, axis=-1, keepdims=True) + NORM_EPS)
        nrm = mixb * rstd
        dn = dxb * gpost
        dmix = rstd * (dn - nrm * jnp.mean(dn * nrm, axis=-1, keepdims=True))
        ss, sl = _sigmoid(gsb), _sigmoid(glb)
        merged = ss * zs + sl * zl
        dm = _nt(dmix, wout)
        return (dm * ss, dm * sl, dm * zs * ss * (1.0 - ss), dm * zl * sl * (1.0 - sl),
                _tn(merged, dmix), jnp.sum(dxb * nrm, axis=0, keepdims=True))

    dz_s, dz_l, dgs, dgl, g["w_out"], g["g_post"] = _rows(
        "b_merge", b_merge,
        [(dx1, "row"), (sv["mix"], "row"), (sv["z_s"], "row"), (sv["z_l"], "row"), (sv["gs"], "row"), (sv["gl"], "row"),
         (w["w_out"], "full"), (w["g_post"], "full")],
        [((rows, d), BF16, "row")] * 4 + [((d, d), F32, "acc"), ((1, d), F32, "acc")], rows=rows, tile=tile)

    def b_bl(i, dzl, hb, lg, wbl):
        sl = _sigmoid(lg)
        silu = lg * sl
        dy3 = _nt(dzl, wbl)
        return dy3 * silu, dy3 * hb * sl * (1.0 + lg * (1.0 - sl)), _tn(hb * silu, dzl)

    dh, dlrug, g["w_bl"] = _rows(
        "b_bl", b_bl, [(dz_l, "row"), (sv["hl"], "row"), (sv["lrug"], "row"), (w["w_bl"], "full")],
        [((rows, LRU_WIDTH), F32, "row"), ((rows, LRU_WIDTH), BF16, "row"), ((LRU_WIDTH, d), F32, "acc")], rows=rows, tile=tile)

    gh = _lru_scan_bwd(dh, sv["a"], rows=rows, tile=tile)

    def b_gates(i, ghb, hb, hprev, xb, xprev, cw, cb, wa, ba, wx, bx, lam):
        c = _conv_fwd(i, xb, xprev, cw, cb)
        r, ig, sp, a, mult = _lru_gates(c, wa, ba, wx, bx, lam)
        h_before = _rows_before(hb, hprev * (i > 0).astype(F32), 1)
        da = ghb * h_before
        dmult = ghb * ig * c
        dlog_a = da * a - dmult * a * a / mult
        dpre_r = dlog_a * (-LRU_C) * sp * r * (1.0 - r)
        dpre_i = ghb * mult * c * ig * (1.0 - ig)
        dc = ghb * mult * ig + _heads(_nt, dpre_r, wa) + _heads(_nt, dpre_i, wx)
        dlam = jnp.sum(dlog_a * LRU_C * r, axis=0, keepdims=True) * _sigmoid(-lam)
        return (dc, _heads_tn(c, dpre_r), _heads_tn(c, dpre_i), jnp.sum(dpre_r, axis=0, keepdims=True),
                jnp.sum(dpre_i, axis=0, keepdims=True), dlam)

    hshape = (LRU_HEADS, LRU_HEAD_DIM, LRU_HEAD_DIM)
    dc, g["lru_w_a"], g["lru_w_x"], g["lru_b_a"], g["lru_b_x"], g["lru_lambda"] = _rows(
        "b_gates", b_gates,
        [(gh, "row"), (sv["hl"], "row"), (sv["hl"], "prev"), (sv["lrux"], "row"), (sv["lrux"], "prev"),
         (w["conv_w"], "full"), (w["conv_b"], "full"), (w["lru_w_a"], "full"), (w["lru_b_a"], "full"),
         (w["lru_w_x"], "full"), (w["lru_b_x"], "full"), (w["lru_lambda"], "full")],
        [((rows, LRU_WIDTH), BF16, "row"), (hshape, F32, "acc"), (hshape, F32, "acc")] + [((1, LRU_WIDTH), F32, "acc")] * 3,
        rows=rows, tile=tile)

    n_tiles = rows // min(tile, rows)

    def b_conv(i, dcb, dnext, xb, xprev, cw):
        dnext = dnext * (i < n_tiles - 1).astype(F32)
        xprev = xprev * (i > 0).astype(F32)
        dx = dcb * cw[3:4, :]
        dws = [jnp.sum(dcb * xb, axis=0, keepdims=True)]
        for s in range(1, CONV_WIDTH):
            dx = dx + _rows_after(dcb, dnext, s) * cw[3 - s:4 - s, :]
            dws.append(jnp.sum(dcb * _rows_before(xb, xprev, s), axis=0, keepdims=True))
        return dx, jnp.concatenate(dws[::-1], axis=0), jnp.sum(dcb, axis=0, keepdims=True)

    dlrux, g["conv_w"], g["conv_b"] = _rows(
        "b_conv", b_conv, [(dc, "row"), (dc, "next"), (sv["lrux"], "row"), (sv["lrux"], "prev"), (w["conv_w"], "full")],
        [((rows, LRU_WIDTH), BF16, "row"), ((CONV_WIDTH, LRU_WIDTH), F32, "acc"), ((1, LRU_WIDTH), F32, "acc")],
        rows=rows, tile=tile)

    def b_s5post(i, dzs, ysb, gb, wglu, wbs):
        glv, dgelu = _gelu_parts(ysb)
        glu = _nn(glv, wglu)
        ga, gb2 = glu[:, :S5_WIDTH], glu[:, S5_WIDTH:]
        sb = _sigmoid(gb2)
        sg = _sigmoid(gb)
        silu = gb * sg
        y2 = ga * sb * silu
        dy2 = _nt(dzs, wbs)
        dglu = jnp.concatenate([dy2 * sb * silu, dy2 * ga * silu * sb * (1.0 - sb)], axis=1)
        dg = dy2 * ga * sb * sg * (1.0 + gb * (1.0 - sg))
        return _nt(dglu, wglu) * dgelu, dg, _tn(y2, dzs), _tn(glv, dglu)

    dys, ds5g, g["w_bs"], g["w_glu"] = _rows(
        "b_s5post", b_s5post, [(dz_s, "row"), (sv["ys"], "row"), (sv["s5g"], "row"), (w["w_glu"], "full"), (w["w_bs"], "full")],
        [((rows, S5_WIDTH), F32, "row"), ((rows, S5_WIDTH), BF16, "row"), ((S5_WIDTH, d), F32, "acc"),
         ((S5_WIDTH, 2 * S5_WIDTH), F32, "acc")],
        rows=rows, tile=tile)

    ds5x, g["bd"], g["cdt"], g["s5_d"], g["abar"] = _s5_bwd(dys, sv["st"], sv["s5x"], w["bd"], w["cdt"], w["s5_d"],
                                                            w["scb"], rows=rows, tile=tile)

    dcomps = [ds5x, ds5g, dlrux, dlrug, dgs, dgl]

    def b_in(i, xb, dx1b, gpre, *rest):
        dproj, ws = jnp.concatenate(rest[:6], axis=1), rest[6:]
        dh = _nt(dproj[:, :IN_SLOT], ws[0])
        for j in range(1, 4):
            dh = dh + _nt(dproj[:, j * IN_SLOT:(j + 1) * IN_SLOT], ws[j])
        rstd = lax.rsqrt(jnp.mean(xb * xb, axis=-1, keepdims=True) + NORM_EPS)
        nrm = xb * rstd
        dn = dh * gpre
        dx = rstd * (dn - nrm * jnp.mean(dn * nrm, axis=-1, keepdims=True))
        return dx1b + dx, jnp.sum(dh * nrm, axis=0, keepdims=True)

    dx, g["g_pre"] = _rows(
        "b_in", b_in, [(sv["x"], "row"), (dx1, "row"), (w["g_pre"], "full")] + [(dcv, "raw") for dcv in dcomps]
        + [(wc, "full") for wc in w["w_in"]],
        [((rows, d), F32, "row"), ((1, d), F32, "acc")], rows=rows, tile=tile)

    g["w_in"] = []
    for j in range(4):
        lo, hi = j * IN_SLOT, (j + 1) * IN_SLOT
        ks = [k for k in range(6) if IN_OFFSETS[k] < hi and IN_OFFSETS[k + 1] > lo]
        first = IN_OFFSETS[ks[0]]

        def b_win(i, hb, *parts, lo=lo, hi=hi, first=first):
            return (_tn(hb, jnp.concatenate(parts, axis=1)[:, lo - first:hi - first]),)

        g["w_in"].append(_rows("b_win", b_win, [(sv["h"], "raw")] + [(dcomps[k], "raw") for k in ks],
                               [((d, IN_SLOT), F32, "acc")], rows=rows, tile=4 * tile)[0])
    return dx, g


SMALL = ("g_pre", "s5_a_re", "s5_a_im", "s5_log_dt", "s5_b_re", "s5_b_im", "s5_c_re", "s5_c_im", "s5_d", "conv_b",
         "lru_w_a", "lru_b_a", "lru_w_x", "lru_b_x", "lru_lambda", "g_post")
BIG = ("w_in", "w_glu", "w_bs", "conv_w", "w_bl", "w_out", "w_ple", "w_ple_gate")
BIG_SHARD_AXIS = {"w_in": 1, "w_glu": 1, "w_bs": 1, "conv_w": 1, "w_bl": 0, "w_out": 0, "w_ple": 1, "w_ple_gate": 0}


def _bcast_groups(v):
    return jnp.broadcast_to(v[:, None, :], (S5_GROUPS, S5_GROUP, S5_STATE)).reshape(S5_WIDTH, S5_STATE)


def _s5_prep_inputs(wl):
    ldt = jnp.broadcast_to(wl["s5_log_dt"][:, None], (S5_GROUPS, S5_STATE))
    gcn = lambda b: jnp.transpose(b, (0, 2, 1)).reshape(S5_WIDTH, S5_STATE)
    return (_bcast_groups(wl["s5_a_re"]), _bcast_groups(wl["s5_a_im"]), _bcast_groups(ldt), gcn(wl["s5_b_re"]),
            gcn(wl["s5_b_im"]))


def _layer_weights(wl):
    w = {}
    for k in ("w_in", "w_glu", "w_bs", "w_bl", "w_out", "w_ple", "w_ple_gate"):
        w[k] = wl[k]
    w["conv_w"] = wl["conv_w"]
    for k in ("g_pre", "g_post", "s5_d", "conv_b", "lru_b_a", "lru_b_x", "lru_lambda"):
        w[k] = wl[k].reshape(1, -1)
    w["lru_w_a"] = wl["lru_w_a"].astype(BF16)
    w["lru_w_x"] = wl["lru_w_x"].astype(BF16)
    prep_in = _s5_prep_inputs(wl)
    abr, abi, bbr, bbi = _s5_prep(*prep_in)
    w["prep_in"] = prep_in
    shape3 = (S5_GROUPS, S5_GROUP, S5_STATE)
    w["bd"] = _s5_block_diag([bbr.reshape(shape3), bbi.reshape(shape3)]).astype(BF16)
    w["cdt"] = _s5_block_diag([wl["s5_c_re"], -wl["s5_c_im"]]).astype(BF16)
    abr_s = abr.reshape(shape3)[:, 0, :]
    abi_s = abi.reshape(shape3)[:, 0, :]
    w["scf"], w["scb"] = _s5_consts(_cplx_to_lanes(abr_s), _cplx_to_lanes(abi_s), ROW_TILE // SUBLANES)
    return w


def _layer_param_grads(g, w, wl):
    out = {}
    shape3 = (S5_GROUPS, S5_GROUP, S5_STATE)
    dbr, dbi = _s5_block_diag_extract(g["bd"])
    dcr, dci = _s5_block_diag_extract(g["cdt"])
    out["s5_c_re"], out["s5_c_im"] = dcr, -dci
    zeros = jnp.zeros(shape3, F32)
    dar = zeros.at[:, 0, :].set(g["abar"][0].reshape(S5_GROUPS, S5_STATE)).reshape(S5_WIDTH, S5_STATE)
    dai = zeros.at[:, 0, :].set(g["abar"][1].reshape(S5_GROUPS, S5_STATE)).reshape(S5_WIDTH, S5_STATE)
    cts = (dar, dai, dbr.reshape(S5_WIDTH, S5_STATE), dbi.reshape(S5_WIDTH, S5_STATE))
    d_are, d_aim, d_ldt, d_bre, d_bim = _s5_prep_bwd(*w["prep_in"], cts)
    out["s5_a_re"] = d_are.reshape(shape3).sum(axis=1)
    out["s5_a_im"] = d_aim.reshape(shape3).sum(axis=1)
    out["s5_log_dt"] = d_ldt.reshape(shape3).sum(axis=(1, 2))
    out["s5_b_re"] = jnp.transpose(d_bre.reshape(shape3), (0, 2, 1))
    out["s5_b_im"] = jnp.transpose(d_bim.reshape(shape3), (0, 2, 1))
    out["s5_d"] = g["s5_d"].reshape(-1)
    for k in ("g_pre", "g_post", "conv_b", "lru_b_a", "lru_b_x", "lru_lambda"):
        out[k] = g[k].reshape(-1)
    for k in ("lru_w_a", "lru_w_x", "conv_w", "w_in", "w_glu", "w_bs", "w_bl", "w_out", "w_ple", "w_ple_gate"):
        out[k] = g[k]
    return out


def _local_step(x, p, layers, target):
    rows = x.shape[0]
    ws = [_layer_weights(wl) for wl in layers]
    saved = []
    for i in range(DEPTH):
        x, sv = _layer_fwd(x, p[i], ws[i], rows)
        saved.append(sv)

    def f_loss(i, yb, tb):
        e = yb - tb
        return e * (1.0 / D_MODEL), jnp.sum(jnp.sum(e * e, axis=0, keepdims=True), axis=1, keepdims=True)

    dx, sq = _rows("f_loss", f_loss, [(x, "row"), (target, "row")],
                   [((rows, D_MODEL), F32, "row"), ((1, 1), F32, "acc")], rows=rows, tile=ROW_TILE)
    loss = sq[0, 0] * (0.5 / D_MODEL)
    grads = [None] * DEPTH
    for i in reversed(range(DEPTH)):
        dx, g = _layer_bwd(dx, saved[i], ws[i], rows)
        grads[i] = _layer_param_grads(g, ws[i], layers[i])
    return loss, dx, grads


def _place():
    return lax.axis_index("x"), lax.axis_index("y"), lax.axis_index("c")


def _other_chips(x, y):
    return [(1 - x, y), (x, 1 - y), (1 - x, 1 - y)]


def _any_spec():
    return pl.BlockSpec(memory_space=pl.ANY)


ICI_PIECES = 1
D2D_PIECES = 1
D2D_SOLO_PIECES = 1


def _pieces(rows, k):
    step = rows // k
    assert step * k == rows and step % 16 == 0, (rows, k)
    return [(q * step, step) for q in range(k)]


def _gather_chips(name, v, via_sibling):
    rows = v.shape[0]
    half = rows // 2

    n_sent = half if via_sibling else rows

    def body(v_ref, out_ref, send_sems, recv_sems):
        x, y, c = _place()
        me = 2 * x + y
        chips = _other_chips(x, y)
        slots = [2 * cx + cy for cx, cy in chips]

        def part(slot, hc, o=0, s=n_sent):
            return out_ref.at[slot, pl.ds(hc * half + o, s), :] if via_sibling else out_ref.at[slot, pl.ds(o, s), :]

        def own(o=0, s=n_sent):
            return v_ref.at[pl.ds(c * half + o, s), :] if via_sibling else v_ref.at[pl.ds(o, s), :]

        def copy(k, src, dst, to):
            return pltpu.make_async_remote_copy(src_ref=src, dst_ref=dst, send_sem=send_sems.at[k], recv_sem=recv_sems.at[k],
                                                device_id=to, device_id_type=MESH)

        for k in range(3):
            for o, s in _pieces(n_sent, ICI_PIECES):
                copy(k, own(o, s), part(me, c, o, s), (*chips[k], c)).start()
        for k in range(3):
            copy(k, own(), part(slots[k], c), (*chips[k], c)).wait_recv()
            if via_sibling:
                for o, s in _pieces(n_sent, D2D_PIECES):
                    copy(3 + k, part(slots[k], c, o, s), part(slots[k], c, o, s), (x, y, 1 - c)).start()
        if via_sibling:
            for k in range(3):
                copy(3 + k, own(), part(slots[k], 1 - c), (x, y, 1 - c)).wait_recv()
        for k in range(6 if via_sibling else 3):
            copy(k, own(), part(me, c), (x, y, 1 - c)).wait_send()

    n_sem = 6 if via_sibling else 3
    others = pl.pallas_call(
        body, name=name, out_shape=jax.ShapeDtypeStruct((4,) + v.shape, v.dtype),
        in_specs=[_any_spec()], out_specs=_any_spec(),
        scratch_shapes=[pltpu.SemaphoreType.DMA((n_sem,)), pltpu.SemaphoreType.DMA((n_sem,))],
    )(v)
    return lax.dynamic_update_slice(others, v[None], (2 * lax.axis_index("x") + lax.axis_index("y"), 0, 0))


def _rs_sibling(gr):
    half = gr.shape[1] // 2

    def body(g_ref, got_ref, send_sem, recv_sem):
        x, y, c = _place()

        def give(src, dst):
            return pltpu.make_async_remote_copy(src_ref=src, dst_ref=dst, send_sem=send_sem, recv_sem=recv_sem,
                                                device_id=(x, y, 1 - c), device_id_type=MESH)

        copy = give(g_ref.at[:, pl.ds((1 - c) * half, half), :], got_ref)
        copy.start()
        copy.wait()

    return pl.pallas_call(
        body, name="rs_sibling", out_shape=jax.ShapeDtypeStruct((4, half, LANES), F32),
        in_specs=[_any_spec()], out_specs=_any_spec(),
        scratch_shapes=[pltpu.SemaphoreType.DMA, pltpu.SemaphoreType.DMA],
    )(gr)


def _rs_chips(a16):
    half = a16.shape[1]

    def body(a16_ref, got_ref, send_sems, recv_sems):
        x, y, c = _place()
        chips = _other_chips(x, y)

        def copy(k, o=0, s=half):
            cx, cy = chips[k]
            return pltpu.make_async_remote_copy(
                src_ref=a16_ref.at[2 * cx + cy, pl.ds(o, s), :], dst_ref=got_ref.at[k, pl.ds(o, s), :],
                send_sem=send_sems.at[k], recv_sem=recv_sems.at[k], device_id=(cx, cy, c), device_id_type=MESH)

        for k in range(3):
            for o, s in _pieces(half, ICI_PIECES):
                copy(k, o, s).start()
        for k in range(3):
            copy(k).wait()

    return pl.pallas_call(
        body, name="rs_chips", out_shape=jax.ShapeDtypeStruct((3, half, LANES), BF16),
        in_specs=[_any_spec()], out_specs=_any_spec(),
        scratch_shapes=[pltpu.SemaphoreType.DMA((3,)), pltpu.SemaphoreType.DMA((3,))],
    )(a16)


def _swap_halves(v):
    def body(v_ref, out_ref, send_sem, recv_sem):
        x, y, c = _place()

        def give(hc):
            return pltpu.make_async_remote_copy(src_ref=v_ref, dst_ref=out_ref.at[hc], send_sem=send_sem, recv_sem=recv_sem,
                                                device_id=(x, y, 1 - c), device_id_type=MESH)

        give(c).start()
        give(c).wait_send()
        give(1 - c).wait_recv()

    other = pl.pallas_call(
        body, name="swap_halves", out_shape=jax.ShapeDtypeStruct((2,) + v.shape, v.dtype),
        in_specs=[_any_spec()], out_specs=_any_spec(),
        scratch_shapes=[pltpu.SemaphoreType.DMA, pltpu.SemaphoreType.DMA],
    )(v)
    both = lax.dynamic_update_slice(other, v[None], (lax.axis_index("c"), 0, 0))
    return both.reshape(2 * v.shape[0], v.shape[1])


WIDE = 1024
PACK_TILE = 3072
GRAD_ROWS_UNIT = 2 * PACK_TILE


def _pack(parts, rows_unit, dtype):
    flat = jnp.concatenate([q.reshape(-1).astype(dtype) for q in parts])
    unit = rows_unit * LANES
    total = -(-flat.shape[0] // unit) * unit
    return jnp.pad(flat, (0, total - flat.shape[0])).reshape(-1, LANES)


def _unpack(flat, shapes, align=1):
    out, off = [], 0
    for s in shapes:
        n = 1
        for q in s:
            n *= q
        out.append(flat[off:off + n].reshape(s))
        off += -(-n // align) * align
    return out


def _flat_aligned(v, align):
    v = v.reshape(-1)
    return jnp.pad(v, (0, -v.shape[0] % align))


def _to_slots(name, full):
    dp, r, c = full.shape
    if BIG_SHARD_AXIS[name] == 1:
        return jnp.transpose(full.reshape(dp, r, 4, c // 4), (2, 0, 1, 3)).reshape(4, -1)
    return jnp.transpose(full.reshape(dp, 4, r // 4, c), (1, 0, 2, 3)).reshape(4, -1)


def _from_slots(name, slots, shard_shape):
    dp, r, c = shard_shape
    v = slots.reshape(4, dp, r, c)
    if BIG_SHARD_AXIS[name] == 1:
        return jnp.transpose(v, (1, 2, 0, 3)).reshape(dp, r, 4 * c)
    return jnp.transpose(v, (1, 0, 2, 3)).reshape(dp, 4 * r, c)


def _adamw(name, w, g, m, v, tile):
    def fn(i, wb, gb, mb, vb):
        m2 = ADAM_B1 * mb + (1.0 - ADAM_B1) * gb
        v2 = ADAM_B2 * vb + (1.0 - ADAM_B2) * (gb * gb)
        m_hat = m2 / (1.0 - ADAM_B1 ** ADAM_STEP)
        v_hat = v2 / (1.0 - ADAM_B2 ** ADAM_STEP)
        return -ADAM_LR * (m_hat / (jnp.sqrt(v_hat) + ADAM_EPS) + ADAM_WD * wb), m2, v2

    return _rows(name, fn, [(w, "row"), (g, "row"), (m, "row"), (v, "row")], [(w.shape, F32, "row")] * 3,
                 rows=w.shape[0], tile=tile)


def _as_2d(a):
    return a.reshape(-1, a.shape[-1])


def _adam_tile(rows):
    for t in (256, 184, 128, 64, 32, 16, 8):
        if rows % t == 0:
            return t
    return rows


def kernel(x, p, g_pre, w_in, s5_a_re, s5_a_im, s5_log_dt, s5_b_re, s5_b_im, s5_c_re, s5_c_im, s5_d, w_glu, w_bs, conv_w, conv_b, lru_w_a, lru_b_a, lru_w_x, lru_b_x, lru_lambda, w_bl, w_out, g_post, w_ple, w_ple_gate, loss_target, m_g_pre, m_w_in, m_s5_a_re, m_s5_a_im, m_s5_log_dt, m_s5_b_re, m_s5_b_im, m_s5_c_re, m_s5_c_im, m_s5_d, m_w_glu, m_w_bs, m_conv_w, m_conv_b, m_lru_w_a, m_lru_b_a, m_lru_w_x, m_lru_b_x, m_lru_lambda, m_w_bl, m_w_out, m_g_post, m_w_ple, m_w_ple_gate, v_g_pre, v_w_in, v_s5_a_re, v_s5_a_im, v_s5_log_dt, v_s5_b_re, v_s5_b_im, v_s5_c_re, v_s5_c_im, v_s5_d, v_w_glu, v_w_bs, v_conv_w, v_conv_b, v_lru_w_a, v_lru_b_a, v_lru_w_x, v_lru_b_x, v_lru_lambda, v_w_bl, v_w_out, v_g_post, v_w_ple, v_w_ple_gate):
    wts = dict(g_pre=g_pre, w_in=w_in, s5_a_re=s5_a_re, s5_a_im=s5_a_im, s5_log_dt=s5_log_dt, s5_b_re=s5_b_re,
               s5_b_im=s5_b_im, s5_c_re=s5_c_re, s5_c_im=s5_c_im, s5_d=s5_d, w_glu=w_glu, w_bs=w_bs, conv_w=conv_w,
               conv_b=conv_b, lru_w_a=lru_w_a, lru_b_a=lru_b_a, lru_w_x=lru_w_x, lru_b_x=lru_b_x, lru_lambda=lru_lambda,
               w_bl=w_bl, w_out=w_out, g_post=g_post, w_ple=w_ple, w_ple_gate=w_ple_gate)
    mom1 = dict(g_pre=m_g_pre, w_in=m_w_in, s5_a_re=m_s5_a_re, s5_a_im=m_s5_a_im, s5_log_dt=m_s5_log_dt, s5_b_re=m_s5_b_re,
                s5_b_im=m_s5_b_im, s5_c_re=m_s5_c_re, s5_c_im=m_s5_c_im, s5_d=m_s5_d, w_glu=m_w_glu, w_bs=m_w_bs,
                conv_w=m_conv_w, conv_b=m_conv_b, lru_w_a=m_lru_w_a, lru_b_a=m_lru_b_a, lru_w_x=m_lru_w_x, lru_b_x=m_lru_b_x,
                lru_lambda=m_lru_lambda, w_bl=m_w_bl, w_out=m_w_out, g_post=m_g_post, w_ple=m_w_ple, w_ple_gate=m_w_ple_gate)
    mom2 = dict(g_pre=v_g_pre, w_in=v_w_in, s5_a_re=v_s5_a_re, s5_a_im=v_s5_a_im, s5_log_dt=v_s5_log_dt, s5_b_re=v_s5_b_re,
                s5_b_im=v_s5_b_im, s5_c_re=v_s5_c_re, s5_c_im=v_s5_c_im, s5_d=v_s5_d, w_glu=v_w_glu, w_bs=v_w_bs,
                conv_w=v_conv_w, conv_b=v_conv_b, lru_w_a=v_lru_w_a, lru_b_a=v_lru_b_a, lru_w_x=v_lru_w_x, lru_b_x=v_lru_b_x,
                lru_lambda=v_lru_lambda, w_bl=v_w_bl, w_out=v_w_out, g_post=v_g_post, w_ple=v_w_ple, w_ple_gate=v_w_ple_gate)
    names = list(wts)

    def wire(name):
        return lax.bitcast_convert_type(wts[name], BF16) if name == "conv_w" else wts[name].astype(BF16)

    wire_shapes = [wire(k).shape for k in BIG]
    gathered = _gather_chips("gather_weights", _pack([wire(k) for k in BIG], 128, BF16), via_sibling=True)
    per_chip = [_unpack(gathered[j].reshape(-1), wire_shapes) for j in range(4)]
    whole = {}
    for idx, k in enumerate(BIG):
        if k == "w_in":
            continue
        v = jnp.stack([per_chip[j][idx] for j in range(4)])
        if k == "conv_w":
            v = lax.bitcast_convert_type(v, F32)
        whole[k] = _from_slots(k, v.reshape(4, -1), wts[k].shape)
    layers = []
    for i in range(DEPTH):
        wl = {k: whole[k][i] for k in BIG if k != "w_in"}
        wl["w_in"] = [per_chip[j][BIG.index("w_in")][i] for j in range(4)]
        wl.update({k: wts[k][i] for k in SMALL})
        layers.append(wl)

    loss, grad_x, grads = _local_step(x[0], p[:, 0], layers, loss_target[0])
    loss = lax.psum(loss, ("x", "y", "c"))

    def slots_of(k):
        if k == "w_in":
            return jnp.stack([jnp.stack([grads[i][k][j] for i in range(DEPTH)]).reshape(-1) for j in range(4)])
        return _to_slots(k, jnp.stack([grads[i][k] for i in range(DEPTH)]))

    big_slots = jnp.concatenate([slots_of(k) for k in BIG], axis=1)
    small_shapes = [wts[k].shape for k in SMALL]
    small_flat = jnp.concatenate([_flat_aligned(jnp.stack([grads[i][k] for i in range(DEPTH)]), WIDE) for k in SMALL])
    n_small = small_flat.shape[0]
    small_q = -(-n_small // (4 * 8 * WIDE)) * 8 * WIDE
    small_slots = jnp.pad(small_flat, (0, 4 * small_q - n_small)).reshape(4, small_q)
    n_big = big_slots.shape[1]
    n_big_pad = -(-n_big // (8 * WIDE)) * 8 * WIDE
    n_slot = n_big_pad + small_q
    unit = GRAD_ROWS_UNIT * LANES
    n_slot_pad = -(-n_slot // unit) * unit
    gr = jnp.concatenate([jnp.pad(big_slots, ((0, 0), (0, n_big_pad - n_big))), small_slots,
                          jnp.zeros((4, n_slot_pad - n_slot), F32)], axis=1).reshape(4, -1, LANES)
    got = _rs_sibling(gr)
    half = got.shape[1]
    mine = lax.dynamic_slice_in_dim(gr, lax.axis_index("c") * half, half, axis=1)
    rows2d = lambda a: a.reshape(-1, LANES)

    def f_add1(i, a, b):
        s = a + b
        return s, s

    a32, a16 = _rows("rs_add1", f_add1, [(rows2d(mine), "row"), (rows2d(got), "row")],
                     [((4 * half, LANES), F32, "row"), ((4 * half, LANES), BF16, "row")], rows=4 * half, tile=PACK_TILE)
    got3 = _rs_chips(a16.reshape(4, half, LANES))
    own = lax.dynamic_index_in_dim(a32.reshape(4, half, LANES), 2 * lax.axis_index("x") + lax.axis_index("y"), 0, keepdims=False)

    def f_add2(i, o, g0, g1, g2):
        return (((o + g0.astype(F32)) + g1.astype(F32)) + g2.astype(F32),)

    (red_half,) = _rows("rs_add2", f_add2, [(own, "row")] + [(got3[k], "row") for k in range(3)],
                        [((half, LANES), F32, "row")], rows=half, tile=PACK_TILE)
    red = _swap_halves(red_half).reshape(-1)
    small_red = _gather_chips("gather_small", red[n_big_pad:n_big_pad + small_q].reshape(-1, LANES), via_sibling=False)
    small_red = small_red.reshape(-1)[:n_small]

    big_shapes = [wts[k].shape for k in BIG]
    grad_out = dict(zip(BIG, _unpack(red[:n_big], big_shapes)))
    grad_out.update(zip(SMALL, _unpack(small_red, small_shapes, align=WIDE)))
    delta, new_m, new_v = {}, {}, {}
    for k in BIG + SMALL:
        w2 = _as_2d(wts[k])
        res = _adamw("adamw_" + k, w2, _as_2d(grad_out[k]), _as_2d(mom1[k]), _as_2d(mom2[k]), _adam_tile(w2.shape[0]))
        delta[k], new_m[k], new_v[k] = [r.reshape(wts[k].shape) for r in res]
    return (loss, grad_x[None], *[grad_out[k] for k in names], *[delta[k] for k in names],
            *[new_m[k] for k in names], *[new_v[k] for k in names])
```

```python
import jax
import jax.numpy as jnp
from jax import lax
from jax.experimental import pallas as pl
from jax.experimental.pallas import tpu as pltpu

F32 = jnp.float32
BF16 = jnp.bfloat16
MESH = pl.DeviceIdType.MESH

DEPTH = 2
D_MODEL = 1024
NORM_EPS = 1e-6
S5_WIDTH = 512
S5_GROUPS = 32
S5_GROUP = 16
S5_STATE = 64
LRU_WIDTH = 1280
LRU_HEADS = 10
LRU_HEAD_DIM = 128
LRU_C = 8.0
CONV_WIDTH = 4
PLE_DIM = 256
IN_WIDTHS = (S5_WIDTH, S5_WIDTH, LRU_WIDTH, LRU_WIDTH, D_MODEL, D_MODEL)
IN_OFFSETS = (0, 512, 1024, 2304, 3584, 4608, 5632)
IN_SLOT = 5632 // 4
ADAM_LR = 0.001
ADAM_B1 = 0.9
ADAM_B2 = 0.999
ADAM_EPS = 1e-08
ADAM_WD = 0.01
ADAM_STEP = 10

SUBLANES = 8
LANES = 128
S5_HALF_IN = S5_WIDTH // 2
S5_CPLX = S5_GROUPS * S5_STATE
S5_HALF_CPLX = S5_CPLX // 2
S5_LANES = 2 * S5_CPLX
VMEM_LIMIT = 48 * 2 ** 20
ROW_TILE = 256


def _sigmoid(x):
    return 1.0 / (1.0 + jnp.exp(-x))


def _gelu_parts(x):
    k = 0.7978845608028654
    t = jnp.tanh(k * (x + 0.044715 * x * x * x))
    val = 0.5 * x * (1.0 + t)
    grad = 0.5 * (1.0 + t) + 0.5 * x * (1.0 - t * t) * k * (1.0 + 3.0 * 0.044715 * x * x)
    return val, grad


def _nn(a, w):
    return jnp.dot(a.astype(BF16), w.astype(BF16), preferred_element_type=F32)


def _nt(a, w):
    return lax.dot_general(a.astype(BF16), w.astype(BF16), (((1,), (1,)), ((), ())), preferred_element_type=F32)


def _tn(a, b):
    return lax.dot_general(a.astype(BF16), b.astype(BF16), (((0,), (0,)), ((), ())), preferred_element_type=F32)


def _heads(op, a, w):
    d = LRU_HEAD_DIM
    return jnp.concatenate([op(a[:, h * d:(h + 1) * d], w[h]) for h in range(LRU_HEADS)], axis=1)


def _heads_tn(a, b):
    d = LRU_HEAD_DIM
    return jnp.stack([_tn(a[:, h * d:(h + 1) * d], b[:, h * d:(h + 1) * d]) for h in range(LRU_HEADS)], axis=0)


def _rows_before(x, halo, s):
    main = pltpu.roll(x, s, 0)
    head = pltpu.roll(jnp.concatenate([halo, x[0:SUBLANES]], axis=0), s, 0)[SUBLANES:2 * SUBLANES]
    return jnp.concatenate([head, main[SUBLANES:]], axis=0)


def _rows_after(x, halo, s):
    n = x.shape[0]
    main = pltpu.roll(x, n - s, 0)
    tail = pltpu.roll(jnp.concatenate([x[n - SUBLANES:], halo], axis=0), 2 * SUBLANES - s, 0)[0:SUBLANES]
    return jnp.concatenate([main[:n - SUBLANES], tail], axis=0)


def _rows(name, fn, ins, outs, *, rows, tile):
    tile = min(tile, rows)
    n = rows // tile
    assert n * tile == rows, (name, rows, tile)
    in_specs = []
    for arr, kind in ins:
        halo = SUBLANES * (4 // arr.dtype.itemsize)
        per, last = tile // halo, rows // halo - 1
        if kind in ("row", "raw"):
            in_specs.append(pl.BlockSpec((tile, arr.shape[1]), lambda i: (i, 0)))
        elif kind == "prev":
            in_specs.append(pl.BlockSpec((halo, arr.shape[1]), lambda i, per=per: (jnp.maximum(i * per - 1, 0), 0)))
        elif kind == "next":
            in_specs.append(pl.BlockSpec((halo, arr.shape[1]),
                                         lambda i, per=per, last=last: (jnp.minimum((i + 1) * per, last), 0)))
        else:
            in_specs.append(pl.BlockSpec(arr.shape, lambda i, nd=arr.ndim: (0,) * nd))
    out_shape, out_specs = [], []
    for shape, dtype, kind in outs:
        out_shape.append(jax.ShapeDtypeStruct(shape, dtype))
        if kind == "row":
            out_specs.append(pl.BlockSpec((tile, shape[1]), lambda i: (i, 0)))
        else:
            out_specs.append(pl.BlockSpec(shape, lambda i, nd=len(shape): (0,) * nd))
    n_in = len(ins)

    def load(ref, kind):
        v = ref[...]
        if kind in ("row", "prev", "next"):
            v = v.astype(F32)
        if kind == "prev":
            v = v[v.shape[0] - SUBLANES:]
        if kind == "next":
            v = v[:SUBLANES]
        return v

    def body(*refs):
        i = pl.program_id(0)
        vals = fn(i, *[load(r, kind) for r, (_, kind) in zip(refs[:n_in], ins)])
        assert len(vals) == len(outs), name
        for r, v, (_, _, kind) in zip(refs[n_in:], vals, outs):
            if kind == "row":
                r[...] = v.astype(r.dtype)
            else:
                @pl.when(i == 0)
                def _():
                    r[...] = jnp.zeros_like(r)

                r[...] += v.astype(r.dtype)

    return pl.pallas_call(
        body, name=name, grid=(n,), in_specs=in_specs, out_specs=out_specs, out_shape=out_shape,
        compiler_params=pltpu.CompilerParams(dimension_semantics=("arbitrary",), vmem_limit_bytes=VMEM_LIMIT),
    )(*[a for a, _ in ins])


def _s5_discretise(are, aim, ldt, bre, bim):
    dt = jnp.exp(ldt)
    er = jnp.exp(are * dt)
    abr = er * jnp.cos(aim * dt)
    abi = er * jnp.sin(aim * dt)
    den = are * are + aim * aim
    zr = ((abr - 1.0) * are + abi * aim) / den
    zi = (abi * are - (abr - 1.0) * aim) / den
    return abr, abi, zr * bre - zi * bim, zr * bim + zi * bre


def _s5_prep(are, aim, ldt, bre, bim):
    def body(a, b, c, d, e, o0, o1, o2, o3):
        r = _s5_discretise(a[...], b[...], c[...], d[...], e[...])
        o0[...], o1[...], o2[...], o3[...] = r

    sd = jax.ShapeDtypeStruct(are.shape, F32)
    return pl.pallas_call(body, name="s5_prep", out_shape=[sd] * 4)(are, aim, ldt, bre, bim)


def _s5_prep_bwd(are, aim, ldt, bre, bim, cts):
    def body(a, b, c, d, e, c0, c1, c2, c3, o0, o1, o2, o3, o4):
        _, vjp = jax.vjp(_s5_discretise, a[...], b[...], c[...], d[...], e[...])
        r = vjp((c0[...], c1[...], c2[...], c3[...]))
        o0[...], o1[...], o2[...], o3[...], o4[...] = r

    sd = jax.ShapeDtypeStruct(are.shape, F32)
    return pl.pallas_call(body, name="s5_prep_bwd", out_shape=[sd] * 5)(are, aim, ldt, bre, bim, *cts)


def _s5_consts(abr, abi, seg):
    shape = (SUBLANES, S5_CPLX)
    assert seg & (seg - 1) == 0 and seg % SUBLANES == 0, seg

    def body(ar_ref, ai_ref, f_ref, b_ref):
        def cmul(p, q):
            return (p[0] * q[0] - p[1] * q[1], p[0] * q[1] + p[1] * q[0])

        row = lax.broadcasted_iota(jnp.int32, shape, 0)
        a1 = (jnp.broadcast_to(ar_ref[...], shape), jnp.broadcast_to(ai_ref[...], shape))
        squares = [a1]
        while 1 << (len(squares) - 1) < 4 * seg:
            squares.append(cmul(squares[-1], squares[-1]))
        nb = seg.bit_length() - 1
        fwd, rev = [], []
        for k, a in ((1, squares[nb]), (2, squares[nb + 1]), (4, squares[nb + 2])):
            fwd += [jnp.where(row >= k, a[0], 0.0), jnp.where(row >= k, a[1], 0.0)]
            rev += [jnp.where(row <= 7 - k, a[0], 0.0), jnp.where(row <= 7 - k, -a[1], 0.0)]
        fwd += [a1[0], a1[1]]
        rev += [a1[0], -a1[1]]
        e = lax.broadcasted_iota(jnp.int32, (seg, S5_CPLX), 0) + 1
        wide = lambda v: jnp.broadcast_to(v[0:1, :], (seg, S5_CPLX))
        pr, pi = jnp.ones((seg, S5_CPLX), F32), jnp.zeros((seg, S5_CPLX), F32)
        for b in range(nb + 1):
            sr, si = wide(squares[b][0]), wide(squares[b][1])
            bit = ((e >> b) & 1) == 1
            pr, pi = jnp.where(bit, pr * sr - pi * si, pr), jnp.where(bit, pr * si + pi * sr, pi)
        f_ref[...] = jnp.concatenate(fwd + [pr, pi], axis=0)
        b_ref[...] = jnp.concatenate(rev + [pr, -pi], axis=0)

    sd = jax.ShapeDtypeStruct((8 * SUBLANES + 2 * seg, S5_CPLX), F32)
    return pl.pallas_call(body, name="s5_consts", out_shape=[sd, sd])(abr, abi)


S5_TILES = S5_LANES // LANES
S5_HALF_TILES = S5_TILES // 2


def _s5_tile_index(q):
    re = (q // 8) * S5_HALF_TILES + (q % 8)
    return re, re + S5_HALF_TILES // 2


def _lanes_of(ref, first, count):
    return jnp.concatenate([ref[j] for j in range(first, first + count)], axis=1)


def _to_lane_tiles(ref, first, value):
    for j in range(value.shape[1] // LANES):
        ref[first + j] = value[:, j * LANES:(j + 1) * LANES]


def _time_perm(tile, transpose=False):
    seg = tile // SUBLANES
    rho = lax.broadcasted_iota(jnp.int32, (tile, tile), 1 if transpose else 0)
    t = lax.broadcasted_iota(jnp.int32, (tile, tile), 0 if transpose else 1)
    return (t == (rho & (SUBLANES - 1)) * seg + (rho >> 3)).astype(BF16)


def _reorder(perm, x):
    out = None
    for _ in range(1 if x.dtype == BF16 else 3):
        piece = x.astype(BF16)
        part = jnp.dot(perm, piece, preferred_element_type=F32)
        out = part if out is None else out + part
        x = x - piece.astype(x.dtype)
    return out


def _s5_scan(s_ref, sc_ref, carry_ref, tile, reverse):
    seg = tile // SUBLANES
    group = 4
    edge = 0 if reverse else SUBLANES - 1
    row = lax.broadcasted_iota(jnp.int32, (SUBLANES, LANES), 0)
    order = range(seg - 1, -1, -1) if reverse else range(seg)
    rows_of = lambda k: pl.ds(k * SUBLANES, SUBLANES)
    base = 8 * SUBLANES

    for q0 in range(0, S5_CPLX // LANES, group):
        qs = list(range(q0, q0 + group))
        tiles = [_s5_tile_index(q) for q in qs]
        cst = lambda k, q: sc_ref[k * SUBLANES:(k + 1) * SUBLANES, q * LANES:(q + 1) * LANES]
        state = [(jnp.zeros((SUBLANES, LANES), F32), jnp.zeros((SUBLANES, LANES), F32)) for _ in qs]
        mult = [(cst(6, q), cst(7, q)) for q in qs]
        for k in order:
            for j, (re, im) in enumerate(tiles):
                ar, ai = mult[j]
                xr, xi = state[j]
                nr = ar * xr - ai * xi + s_ref[re, rows_of(k), :]
                ni = ar * xi + ai * xr + s_ref[im, rows_of(k), :]
                s_ref[re, rows_of(k), :] = nr
                s_ref[im, rows_of(k), :] = ni
                state[j] = (nr, ni)
        start = []
        for j, (q, (re, im)) in enumerate(zip(qs, tiles)):
            er, ei = state[j]
            shift1 = SUBLANES - 1 if reverse else 1
            dr = jnp.where(row == SUBLANES - 1 - edge, carry_ref[re], pltpu.roll(er, shift1, 0))
            di = jnp.where(row == SUBLANES - 1 - edge, carry_ref[im], pltpu.roll(ei, shift1, 0))
            for c, sh in ((0, 1), (2, 2), (4, 4)):
                shift = SUBLANES - sh if reverse else sh
                ar, ai = cst(c, q), cst(c + 1, q)
                sr, si = pltpu.roll(dr, shift, 0), pltpu.roll(di, shift, 0)
                dr, di = dr + ar * sr - ai * si, di + ar * si + ai * sr
            start.append((dr, di))
        for k in order:
            t = seg - 1 - k if reverse else k
            for j, (q, (re, im)) in enumerate(zip(qs, tiles)):
                lanes = slice(q * LANES, (q + 1) * LANES)
                pr = jnp.broadcast_to(sc_ref[base + t:base + t + 1, lanes], (SUBLANES, LANES))
                pi = jnp.broadcast_to(sc_ref[base + seg + t:base + seg + t + 1, lanes], (SUBLANES, LANES))
                cr, ci = start[j]
                xr = s_ref[re, rows_of(k), :] + pr * cr - pi * ci
                xi = s_ref[im, rows_of(k), :] + pr * ci + pi * cr
                s_ref[re, rows_of(k), :] = xr
                s_ref[im, rows_of(k), :] = xi
                if k == order[-1]:
                    carry_ref[re] = jnp.broadcast_to(xr[edge:edge + 1, :], (SUBLANES, LANES))
                    carry_ref[im] = jnp.broadcast_to(xi[edge:edge + 1, :], (SUBLANES, LANES))


def _s5_fwd(u, bd, cdt, dskip, sc, *, rows, tile):
    n = rows // tile

    def body(u_ref, bd_ref, cdt_ref, d_ref, sc_ref, y_ref, s_ref, carry_ref):
        @pl.when(pl.program_id(0) == 0)
        def _():
            carry_ref[...] = jnp.zeros_like(carry_ref)

        ub = _reorder(_time_perm(tile), u_ref[...].astype(BF16)).astype(BF16)
        for h in range(2):
            _to_lane_tiles(s_ref, h * S5_HALF_TILES, jnp.dot(ub[:, h * S5_HALF_IN:(h + 1) * S5_HALF_IN], bd_ref[h],
                                                             preferred_element_type=F32))
        _s5_scan(s_ref, sc_ref, carry_ref, tile, reverse=False)
        ys = [_nt(_lanes_of(s_ref, h * S5_HALF_TILES, S5_HALF_TILES), cdt_ref[h]) for h in range(2)]
        y_ref[...] = _reorder(_time_perm(tile, transpose=True), jnp.concatenate(ys, axis=1)) + d_ref[...] * u_ref[...]

    full = lambda a: pl.BlockSpec(a.shape, lambda i, nd=a.ndim: (0,) * nd)
    return pl.pallas_call(
        body, name="s5_fwd", grid=(n,),
        in_specs=[pl.BlockSpec((tile, S5_WIDTH), lambda i: (i, 0)), full(bd), full(cdt), full(dskip), full(sc)],
        out_specs=[pl.BlockSpec((tile, S5_WIDTH), lambda i: (i, 0)),
                   pl.BlockSpec((S5_TILES, tile, LANES), lambda i: (0, i, 0))],
        out_shape=[jax.ShapeDtypeStruct((rows, S5_WIDTH), F32), jax.ShapeDtypeStruct((S5_TILES, rows, LANES), F32)],
        scratch_shapes=[pltpu.VMEM((S5_TILES, SUBLANES, LANES), F32)],
        compiler_params=pltpu.CompilerParams(dimension_semantics=("arbitrary",), vmem_limit_bytes=VMEM_LIMIT),
    )(u, bd, cdt, dskip, sc)


def _s5_bwd(dy, s, u, bd, cdt, dskip, sc, *, rows, tile):
    n = rows // tile
    hc = 2 * S5_HALF_CPLX
    per8 = tile // SUBLANES
    quarter = S5_HALF_TILES // 2

    def body(dy_ref, s_ref, sp_ref, u_ref, bd_ref, cdt_ref, d_ref, sc_ref,
             du_ref, dbd_ref, dcdt_ref, dd_ref, da_ref, g_ref, carry_ref):
        i = pl.program_id(0)

        @pl.when(i == 0)
        def _():
            carry_ref[...] = jnp.zeros_like(carry_ref)
            dbd_ref[...] = jnp.zeros_like(dbd_ref)
            dcdt_ref[...] = jnp.zeros_like(dcdt_ref)
            dd_ref[...] = jnp.zeros_like(dd_ref)
            da_ref[...] = jnp.zeros_like(da_ref)

        dy = dy_ref[...]
        u = u_ref[...]
        perm = _time_perm(tile)
        dyb = _reorder(perm, dy.astype(BF16)).astype(BF16)
        ub = _reorder(perm, u.astype(BF16)).astype(BF16)
        for h in range(2):
            _to_lane_tiles(g_ref, h * S5_HALF_TILES, jnp.dot(dyb[:, h * S5_HALF_IN:(h + 1) * S5_HALF_IN], cdt_ref[h],
                                                             preferred_element_type=F32))
        _s5_scan(g_ref, sc_ref, carry_ref, tile, reverse=True)
        dus = []
        for h in range(2):
            gb = _lanes_of(g_ref, h * S5_HALF_TILES, S5_HALF_TILES).astype(BF16)
            sb = _lanes_of(s_ref, h * S5_HALF_TILES, S5_HALF_TILES).astype(BF16)
            dus.append(_nt(gb, bd_ref[h]))
            dbd_ref[h] += _tn(ub[:, h * S5_HALF_IN:(h + 1) * S5_HALF_IN], gb)
            dcdt_ref[h] += _tn(dyb[:, h * S5_HALF_IN:(h + 1) * S5_HALF_IN], sb)
        du = _reorder(_time_perm(tile, transpose=True), jnp.concatenate(dus, axis=1))
        du_ref[...] = (du + d_ref[...] * dy).astype(du_ref.dtype)
        dd_ref[...] += jnp.sum(dy * u, axis=0, keepdims=True)

        not_first = (i < n - 1).astype(F32)
        row = lax.broadcasted_iota(jnp.int32, (SUBLANES, quarter * LANES), 0)

        def step_before(first):
            cur = _lanes_of(s_ref, first, quarter)
            before_tile = _lanes_of(sp_ref, first, quarter)[SUBLANES - 1:SUBLANES, :] * not_first
            head = jnp.where(row == 0, before_tile, pltpu.roll(cur[tile - SUBLANES:], 1, 0))
            return jnp.concatenate([head, cur[:tile - SUBLANES]], axis=0)

        for h in range(2):
            re, im = h * S5_HALF_TILES, h * S5_HALF_TILES + quarter
            ssr = step_before(re)
            ssi = step_before(im)
            gr = _lanes_of(g_ref, re, quarter)
            gi = _lanes_of(g_ref, im, quarter)
            lanes = slice(h * S5_HALF_CPLX, (h + 1) * S5_HALF_CPLX)
            da_ref[0:1, lanes] += jnp.sum(ssr * gr + ssi * gi, axis=0, keepdims=True)
            da_ref[1:2, lanes] += jnp.sum(ssr * gi - ssi * gr, axis=0, keepdims=True)

    full = lambda a: pl.BlockSpec(a.shape, lambda i, nd=a.ndim: (0,) * nd)
    rev = lambda i: (n - 1 - i, 0)
    wshape = (2, S5_HALF_IN, hc)
    return pl.pallas_call(
        body, name="s5_bwd", grid=(n,),
        in_specs=[pl.BlockSpec((tile, S5_WIDTH), rev), pl.BlockSpec((S5_TILES, tile, LANES), lambda i: (0, n - 1 - i, 0)),
                  pl.BlockSpec((S5_TILES, SUBLANES, LANES), lambda i: (0, jnp.maximum((n - 1 - i) * per8 - 1, 0), 0)),
                  pl.BlockSpec((tile, S5_WIDTH), rev), full(bd), full(cdt), full(dskip), full(sc)],
        out_specs=[pl.BlockSpec((tile, S5_WIDTH), rev),
                   pl.BlockSpec(wshape, lambda i: (0, 0, 0)), pl.BlockSpec(wshape, lambda i: (0, 0, 0)),
                   pl.BlockSpec((1, S5_WIDTH), lambda i: (0, 0)), pl.BlockSpec((SUBLANES, S5_CPLX), lambda i: (0, 0))],
        out_shape=[jax.ShapeDtypeStruct((rows, S5_WIDTH), BF16), jax.ShapeDtypeStruct(wshape, F32),
                   jax.ShapeDtypeStruct(wshape, F32), jax.ShapeDtypeStruct((1, S5_WIDTH), F32),
                   jax.ShapeDtypeStruct((SUBLANES, S5_CPLX), F32)],
        scratch_shapes=[pltpu.VMEM((S5_TILES, tile, LANES), F32), pltpu.VMEM((S5_TILES, SUBLANES, LANES), F32)],
        compiler_params=pltpu.CompilerParams(dimension_semantics=("arbitrary",), vmem_limit_bytes=VMEM_LIMIT),
    )(dy, s, s, u, bd, cdt, dskip, sc)


def _s5_block_diag(parts):
    v = jnp.stack(parts, axis=2).reshape(2, 16, S5_GROUP, 2, S5_STATE)
    eye = jnp.eye(16, dtype=v.dtype)
    return jnp.einsum("hgcpn,gk->hgcpkn", v, eye).reshape(2, S5_HALF_IN, 2 * S5_HALF_CPLX)


def _s5_block_diag_extract(m):
    v = m.reshape(2, 16, S5_GROUP, 2, 16, S5_STATE)
    d = jnp.diagonal(v, axis1=1, axis2=4)
    d = jnp.transpose(d, (2, 0, 4, 1, 3)).reshape(2, S5_GROUPS, S5_GROUP, S5_STATE)
    return d[0], d[1]


def _cplx_to_lanes(v):
    return v.reshape(1, S5_CPLX)


def _lru_scan_fwd(a, b, *, rows, tile):
    n = rows // tile
    nblk = tile // SUBLANES
    group = 5

    def body(a_ref, b_ref, h_ref, carry_ref):
        @pl.when(pl.program_id(0) == 0)
        def _():
            carry_ref[...] = jnp.zeros_like(carry_ref)

        row = lax.broadcasted_iota(jnp.int32, (SUBLANES, LANES), 0)
        for q0 in range(0, LRU_WIDTH // LANES, group):
            offs = [q * LANES for q in range(q0, q0 + group)]

            def blk(t, carry, offs=offs):
                r0 = pl.multiple_of(t * SUBLANES, SUBLANES)
                new = []
                for j, o in enumerate(offs):
                    av = a_ref[pl.ds(r0, SUBLANES), o:o + LANES]
                    xv = b_ref[pl.ds(r0, SUBLANES), o:o + LANES]
                    for sh in (1, 2, 4):
                        m = row >= sh
                        xs = pltpu.roll(xv, sh, 0)
                        asft = pltpu.roll(av, sh, 0)
                        xv = xv + jnp.where(m, av * xs, 0.0)
                        av = jnp.where(m, av * asft, av)
                    hv = xv + av * carry[j]
                    h_ref[pl.ds(r0, SUBLANES), o:o + LANES] = hv
                    new.append(jnp.broadcast_to(hv[SUBLANES - 1:SUBLANES, :], (SUBLANES, LANES)))
                return tuple(new)

            carry = lax.fori_loop(0, nblk, blk, tuple(carry_ref[:, o:o + LANES] for o in offs), unroll=2)
            for j, o in enumerate(offs):
                carry_ref[:, o:o + LANES] = carry[j]

    spec = pl.BlockSpec((tile, LRU_WIDTH), lambda i: (i, 0))
    return pl.pallas_call(
        body, name="lru_scan_fwd", grid=(n,), in_specs=[spec, spec], out_specs=spec,
        out_shape=jax.ShapeDtypeStruct((rows, LRU_WIDTH), F32),
        scratch_shapes=[pltpu.VMEM((SUBLANES, LRU_WIDTH), F32)],
        compiler_params=pltpu.CompilerParams(dimension_semantics=("arbitrary",), vmem_limit_bytes=VMEM_LIMIT),
    )(a, b)


def _lru_scan_bwd(dh, a, *, rows, tile):
    n = rows // tile
    nblk = tile // SUBLANES
    group = 5

    def body(dh_ref, a_ref, g_ref, cg_ref, ca_ref):
        @pl.when(pl.program_id(0) == 0)
        def _():
            cg_ref[...] = jnp.zeros_like(cg_ref)
            ca_ref[...] = jnp.zeros_like(ca_ref)

        row = lax.broadcasted_iota(jnp.int32, (SUBLANES, LANES), 0)
        for q0 in range(0, LRU_WIDTH // LANES, group):
            offs = [q * LANES for q in range(q0, q0 + group)]

            def blk(t, carry, offs=offs):
                r0 = pl.multiple_of((nblk - 1 - t) * SUBLANES, SUBLANES)
                new = []
                for j, o in enumerate(offs):
                    cg, ca = carry[2 * j], carry[2 * j + 1]
                    araw = a_ref[pl.ds(r0, SUBLANES), o:o + LANES]
                    xv = dh_ref[pl.ds(r0, SUBLANES), o:o + LANES]
                    av = jnp.where(row == SUBLANES - 1, ca, pltpu.roll(araw, SUBLANES - 1, 0))
                    for sh in (1, 2, 4):
                        m = row <= SUBLANES - 1 - sh
                        xs = pltpu.roll(xv, SUBLANES - sh, 0)
                        asft = pltpu.roll(av, SUBLANES - sh, 0)
                        xv = xv + jnp.where(m, av * xs, 0.0)
                        av = jnp.where(m, av * asft, av)
                    gv = xv + av * cg
                    g_ref[pl.ds(r0, SUBLANES), o:o + LANES] = gv
                    new.append(jnp.broadcast_to(gv[0:1, :], (SUBLANES, LANES)))
                    new.append(jnp.broadcast_to(araw[0:1, :], (SUBLANES, LANES)))
                return tuple(new)

            carry0 = tuple(r[:, o:o + LANES] for o in offs for r in (cg_ref, ca_ref))
            carry = lax.fori_loop(0, nblk, blk, carry0, unroll=2)
            for j, o in enumerate(offs):
                cg_ref[:, o:o + LANES] = carry[2 * j]
                ca_ref[:, o:o + LANES] = carry[2 * j + 1]

    spec = pl.BlockSpec((tile, LRU_WIDTH), lambda i: (n - 1 - i, 0))
    return pl.pallas_call(
        body, name="lru_scan_bwd", grid=(n,), in_specs=[spec, spec], out_specs=spec,
        out_shape=jax.ShapeDtypeStruct((rows, LRU_WIDTH), F32),
        scratch_shapes=[pltpu.VMEM((SUBLANES, LRU_WIDTH), F32), pltpu.VMEM((SUBLANES, LRU_WIDTH), F32)],
        compiler_params=pltpu.CompilerParams(dimension_semantics=("arbitrary",), vmem_limit_bytes=VMEM_LIMIT),
    )(dh, a)


def _conv_fwd(i, x, prev, cw, cb):
    prev = prev * (i > 0).astype(F32)
    y = x * cw[3:4, :] + cb
    for s in range(1, CONV_WIDTH):
        y = y + _rows_before(x, prev, s) * cw[3 - s:4 - s, :]
    return y


def _lru_gates(c, wa, ba, wx, bx, lam):
    r = _sigmoid(_heads(_nn, c, wa) + ba)
    ig = _sigmoid(_heads(_nn, c, wx) + bx)
    z = -lam
    sp = jnp.maximum(z, 0.0) + jnp.log(1.0 + jnp.exp(-jnp.abs(z)))
    log_a = -LRU_C * r * sp
    a = jnp.exp(log_a)
    z2 = 2.0 * log_a
    series = -z2 * (1.0 + z2 * (0.5 + z2 * (1.0 / 6.0 + z2 * (1.0 / 24.0 + z2 * (1.0 / 120.0 + z2 / 720.0)))))
    one_minus = jnp.where(z2 > -0.2, series, 1.0 - jnp.exp(z2))
    mult = jnp.sqrt(one_minus)
    return r, ig, sp, a, mult


def _layer_fwd(x, p, w, rows):
    tile = ROW_TILE
    d = D_MODEL

    def f_in(i, xb, g, *ws):
        rstd = lax.rsqrt(jnp.mean(xb * xb, axis=-1, keepdims=True) + NORM_EPS)
        hb = (xb * rstd * g).astype(BF16)
        proj = jnp.concatenate([jnp.dot(hb, wj, preferred_element_type=F32) for wj in ws], axis=1)
        return tuple(proj[:, IN_OFFSETS[k]:IN_OFFSETS[k + 1]] for k in range(6)) + (hb,)

    s5x, s5g, lrux, lrug, gs, gl, h = _rows(
        "f_in", f_in, [(x, "row"), (w["g_pre"], "full")] + [(wc, "full") for wc in w["w_in"]],
        [((rows, wd), BF16, "row") for wd in IN_WIDTHS] + [((rows, d), BF16, "row")], rows=rows, tile=tile)

    ys, st = _s5_fwd(s5x, w["bd"], w["cdt"], w["s5_d"], w["scf"], rows=rows, tile=tile)

    def f_s5post(i, ysb, gb, wglu, wbs):
        glv, _ = _gelu_parts(ysb)
        glu = _nn(glv, wglu)
        y2 = glu[:, :S5_WIDTH] * _sigmoid(glu[:, S5_WIDTH:]) * (gb * _sigmoid(gb))
        return (_nn(y2, wbs),)

    (z_s,) = _rows("f_s5post", f_s5post, [(ys, "row"), (s5g, "row"), (w["w_glu"], "full"), (w["w_bs"], "full")],
                   [((rows, d), BF16, "row")], rows=rows, tile=tile)

    def f_gates(i, xb, prev, cw, cb, wa, ba, wx, bx, lam):
        c = _conv_fwd(i, xb, prev, cw, cb)
        _, ig, _, a, mult = _lru_gates(c, wa, ba, wx, bx, lam)
        return a, mult * (ig * c)

    a, b = _rows("f_gates", f_gates,
                 [(lrux, "row"), (lrux, "prev"), (w["conv_w"], "full"), (w["conv_b"], "full"), (w["lru_w_a"], "full"),
                  (w["lru_b_a"], "full"), (w["lru_w_x"], "full"), (w["lru_b_x"], "full"), (w["lru_lambda"], "full")],
                 [((rows, LRU_WIDTH), F32, "row")] * 2, rows=rows, tile=tile)
    hl = _lru_scan_fwd(a, b, rows=rows, tile=tile)

    def f_merge(i, hb, lg, zs, gsb, glb, xb, wbl, wout, gpost):
        z_l = _nn(hb * (lg * _sigmoid(lg)), wbl)
        merged = _sigmoid(gsb) * zs + _sigmoid(glb) * z_l
        mix = _nn(merged, wout)
        rstd = lax.rsqrt(jnp.mean(mix * mix, axis=-1, keepdims=True) + NORM_EPS)
        return xb + mix * rstd * gpost, mix, z_l

    x1, mix, z_l = _rows("f_merge", f_merge,
                         [(hl, "row"), (lrug, "row"), (z_s, "row"), (gs, "row"), (gl, "row"), (x, "row"),
                          (w["w_bl"], "full"), (w["w_out"], "full"), (w["g_post"], "full")],
                         [((rows, d), F32, "row"), ((rows, d), BF16, "row"), ((rows, d), BF16, "row")], rows=rows, tile=tile)

    def f_ple(i, x1b, pb, wple, wpg):
        return (x1b + _nn(pb, wple) * _sigmoid(_nn(x1b, wpg)),)

    (x2,) = _rows("f_ple", f_ple, [(x1, "row"), (p, "row"), (w["w_ple"], "full"), (w["w_ple_gate"], "full")],
                  [((rows, d), F32, "row")], rows=rows, tile=tile)
    saved = dict(x=x, h=h, s5x=s5x, s5g=s5g, lrux=lrux, lrug=lrug, gs=gs, gl=gl, ys=ys, st=st, a=a, hl=hl, z_s=z_s,
                 z_l=z_l, mix=mix, x1=x1, p=p)
    return x2, saved


def _layer_bwd(dx2, sv, w, rows):
    tile = ROW_TILE
    d = D_MODEL
    g = {}

    def b_ple(i, dxb, x1b, pb, wple, wpg):
        pe = _nn(pb, wple)
        sg = _sigmoid(_nn(x1b, wpg))
        dpe = dxb * sg
        dgt = dxb * pe * sg * (1.0 - sg)
        return dxb + _nt(dgt, wpg), _tn(pb, dpe), _tn(x1b, dgt)

    dx1, g["w_ple"], g["w_ple_gate"] = _rows(
        "b_ple", b_ple, [(dx2, "row"), (sv["x1"], "row"), (sv["p"], "row"), (w["w_ple"], "full"), (w["w_ple_gate"], "full")],
        [((rows, d), F32, "row"), ((PLE_DIM, d), F32, "acc"), ((d, d), F32, "acc")], rows=rows, tile=tile)

    def b_merge(i, dxb, mixb, zs, zl, gsb, glb, wout, gpost):
        rstd = lax.rsqrt(jnp.mean(mixb * mixb, axis=-1, keepdims=True) + NORM_EPS)
        nrm = mixb * rstd
        dn = dxb * gpost
        dmix = rstd * (dn - nrm * jnp.mean(dn * nrm, axis=-1, keepdims=True))
        ss, sl = _sigmoid(gsb), _sigmoid(glb)
        merged = ss * zs + sl * zl
        dm = _nt(dmix, wout)
        return (dm * ss, dm * sl, dm * zs * ss * (1.0 - ss), dm * zl * sl * (1.0 - sl),
                _tn(merged, dmix), jnp.sum(dxb * nrm, axis=0, keepdims=True))

    dz_s, dz_l, dgs, dgl, g["w_out"], g["g_post"] = _rows(
        "b_merge", b_merge,
        [(dx1, "row"), (sv["mix"], "row"), (sv["z_s"], "row"), (sv["z_l"], "row"), (sv["gs"], "row"), (sv["gl"], "row"),
         (w["w_out"], "full"), (w["g_post"], "full")],
        [((rows, d), BF16, "row")] * 4 + [((d, d), F32, "acc"), ((1, d), F32, "acc")], rows=rows, tile=tile)

    def b_bl(i, dzl, hb, lg, wbl):
        sl = _sigmoid(lg)
        silu = lg * sl
        dy3 = _nt(dzl, wbl)
        return dy3 * silu, dy3 * hb * sl * (1.0 + lg * (1.0 - sl)), _tn(hb * silu, dzl)

    dh, dlrug, g["w_bl"] = _rows(
        "b_bl", b_bl, [(dz_l, "row"), (sv["hl"], "row"), (sv["lrug"], "row"), (w["w_bl"], "full")],
        [((rows, LRU_WIDTH), F32, "row"), ((rows, LRU_WIDTH), BF16, "row"), ((LRU_WIDTH, d), F32, "acc")], rows=rows, tile=tile)

    gh = _lru_scan_bwd(dh, sv["a"], rows=rows, tile=tile)

    def b_gates(i, ghb, hb, hprev, xb, xprev, cw, cb, wa, ba, wx, bx, lam):
        c = _conv_fwd(i, xb, xprev, cw, cb)
        r, ig, sp, a, mult = _lru_gates(c, wa, ba, wx, bx, lam)
        h_before = _rows_before(hb, hprev * (i > 0).astype(F32), 1)
        da = ghb * h_before
        dmult = ghb * ig * c
        dlog_a = da * a - dmult * a * a / mult
        dpre_r = dlog_a * (-LRU_C) * sp * r * (1.0 - r)
        dpre_i = ghb * mult * c * ig * (1.0 - ig)
        dc = ghb * mult * ig + _heads(_nt, dpre_r, wa) + _heads(_nt, dpre_i, wx)
        dlam = jnp.sum(dlog_a * LRU_C * r, axis=0, keepdims=True) * _sigmoid(-lam)
        return (dc, _heads_tn(c, dpre_r), _heads_tn(c, dpre_i), jnp.sum(dpre_r, axis=0, keepdims=True),
                jnp.sum(dpre_i, axis=0, keepdims=True), dlam)

    hshape = (LRU_HEADS, LRU_HEAD_DIM, LRU_HEAD_DIM)
    dc, g["lru_w_a"], g["lru_w_x"], g["lru_b_a"], g["lru_b_x"], g["lru_lambda"] = _rows(
        "b_gates", b_gates,
        [(gh, "row"), (sv["hl"], "row"), (sv["hl"], "prev"), (sv["lrux"], "row"), (sv["lrux"], "prev"),
         (w["conv_w"], "full"), (w["conv_b"], "full"), (w["lru_w_a"], "full"), (w["lru_b_a"], "full"),
         (w["lru_w_x"], "full"), (w["lru_b_x"], "full"), (w["lru_lambda"], "full")],
        [((rows, LRU_WIDTH), BF16, "row"), (hshape, F32, "acc"), (hshape, F32, "acc")] + [((1, LRU_WIDTH), F32, "acc")] * 3,
        rows=rows, tile=tile)

    n_tiles = rows // min(tile, rows)

    def b_conv(i, dcb, dnext, xb, xprev, cw):
        dnext = dnext * (i < n_tiles - 1).astype(F32)
        xprev = xprev * (i > 0).astype(F32)
        dx = dcb * cw[3:4, :]
        dws = [jnp.sum(dcb * xb, axis=0, keepdims=True)]
        for s in range(1, CONV_WIDTH):
            dx = dx + _rows_after(dcb, dnext, s) * cw[3 - s:4 - s, :]
            dws.append(jnp.sum(dcb * _rows_before(xb, xprev, s), axis=0, keepdims=True))
        return dx, jnp.concatenate(dws[::-1], axis=0), jnp.sum(dcb, axis=0, keepdims=True)

    dlrux, g["conv_w"], g["conv_b"] = _rows(
        "b_conv", b_conv, [(dc, "row"), (dc, "next"), (sv["lrux"], "row"), (sv["lrux"], "prev"), (w["conv_w"], "full")],
        [((rows, LRU_WIDTH), BF16, "row"), ((CONV_WIDTH, LRU_WIDTH), F32, "acc"), ((1, LRU_WIDTH), F32, "acc")],
        rows=rows, tile=tile)

    def b_s5post(i, dzs, ysb, gb, wglu, wbs):
        glv, dgelu = _gelu_parts(ysb)
        glu = _nn(glv, wglu)
        ga, gb2 = glu[:, :S5_WIDTH], glu[:, S5_WIDTH:]
        sb = _sigmoid(gb2)
        sg = _sigmoid(gb)
        silu = gb * sg
        y2 = ga * sb * silu
        dy2 = _nt(dzs, wbs)
        dglu = jnp.concatenate([dy2 * sb * silu, dy2 * ga * silu * sb * (1.0 - sb)], axis=1)
        dg = dy2 * ga * sb * sg * (1.0 + gb * (1.0 - sg))
        return _nt(dglu, wglu) * dgelu, dg, _tn(y2, dzs), _tn(glv, dglu)

    dys, ds5g, g["w_bs"], g["w_glu"] = _rows(
        "b_s5post", b_s5post, [(dz_s, "row"), (sv["ys"], "row"), (sv["s5g"], "row"), (w["w_glu"], "full"), (w["w_bs"], "full")],
        [((rows, S5_WIDTH), F32, "row"), ((rows, S5_WIDTH), BF16, "row"), ((S5_WIDTH, d), F32, "acc"),
         ((S5_WIDTH, 2 * S5_WIDTH), F32, "acc")],
        rows=rows, tile=tile)

    ds5x, g["bd"], g["cdt"], g["s5_d"], g["abar"] = _s5_bwd(dys, sv["st"], sv["s5x"], w["bd"], w["cdt"], w["s5_d"],
                                                            w["scb"], rows=rows, tile=tile)

    dcomps = [ds5x, ds5g, dlrux, dlrug, dgs, dgl]

    def b_in(i, xb, dx1b, gpre, *rest):
        dproj, ws = jnp.concatenate(rest[:6], axis=1), rest[6:]
        dh = _nt(dproj[:, :IN_SLOT], ws[0])
        for j in range(1, 4):
            dh = dh + _nt(dproj[:, j * IN_SLOT:(j + 1) * IN_SLOT], ws[j])
        rstd = lax.rsqrt(jnp.mean(xb * xb, axis=-1, keepdims=True) + NORM_EPS)
        nrm = xb * rstd
        dn = dh * gpre
        dx = rstd * (dn - nrm * jnp.mean(dn * nrm, axis=-1, keepdims=True))
        return dx1b + dx, jnp.sum(dh * nrm, axis=0, keepdims=True)

    dx, g["g_pre"] = _rows(
        "b_in", b_in, [(sv["x"], "row"), (dx1, "row"), (w["g_pre"], "full")] + [(dcv, "raw") for dcv in dcomps]
        + [(wc, "full") for wc in w["w_in"]],
        [((rows, d), F32, "row"), ((1, d), F32, "acc")], rows=rows, tile=tile)

    g["w_in"] = []
    for j in range(4):
        lo, hi = j * IN_SLOT, (j + 1) * IN_SLOT
        ks = [k for k in range(6) if IN_OFFSETS[k] < hi and IN_OFFSETS[k + 1] > lo]
        first = IN_OFFSETS[ks[0]]

        def b_win(i, hb, *parts, lo=lo, hi=hi, first=first):
            return (_tn(hb, jnp.concatenate(parts, axis=1)[:, lo - first:hi - first]),)

        g["w_in"].append(_rows("b_win", b_win, [(sv["h"], "raw")] + [(dcomps[k], "raw") for k in ks],
                               [((d, IN_SLOT), F32, "acc")], rows=rows, tile=4 * tile)[0])
    return dx, g


SMALL = ("g_pre", "s5_a_re", "s5_a_im", "s5_log_dt", "s5_b_re", "s5_b_im", "s5_c_re", "s5_c_im", "s5_d", "conv_b",
         "lru_w_a", "lru_b_a", "lru_w_x", "lru_b_x", "lru_lambda", "g_post")
BIG = ("w_in", "w_glu", "w_bs", "conv_w", "w_bl", "w_out", "w_ple", "w_ple_gate")
BIG_SHARD_AXIS = {"w_in": 1, "w_glu": 1, "w_bs": 1, "conv_w": 1, "w_bl": 0, "w_out": 0, "w_ple": 1, "w_ple_gate": 0}


def _bcast_groups(v):
    return jnp.broadcast_to(v[:, None, :], (S5_GROUPS, S5_GROUP, S5_STATE)).reshape(S5_WIDTH, S5_STATE)


def _s5_prep_inputs(wl):
    ldt = jnp.broadcast_to(wl["s5_log_dt"][:, None], (S5_GROUPS, S5_STATE))
    gcn = lambda b: jnp.transpose(b, (0, 2, 1)).reshape(S5_WIDTH, S5_STATE)
    return (_bcast_groups(wl["s5_a_re"]), _bcast_groups(wl["s5_a_im"]), _bcast_groups(ldt), gcn(wl["s5_b_re"]),
            gcn(wl["s5_b_im"]))


def _layer_weights(wl):
    w = {}
    for k in ("w_in", "w_glu", "w_bs", "w_bl", "w_out", "w_ple", "w_ple_gate"):
        w[k] = wl[k]
    w["conv_w"] = wl["conv_w"]
    for k in ("g_pre", "g_post", "s5_d", "conv_b", "lru_b_a", "lru_b_x", "lru_lambda"):
        w[k] = wl[k].reshape(1, -1)
    w["lru_w_a"] = wl["lru_w_a"].astype(BF16)
    w["lru_w_x"] = wl["lru_w_x"].astype(BF16)
    prep_in = _s5_prep_inputs(wl)
    abr, abi, bbr, bbi = _s5_prep(*prep_in)
    w["prep_in"] = prep_in
    shape3 = (S5_GROUPS, S5_GROUP, S5_STATE)
    w["bd"] = _s5_block_diag([bbr.reshape(shape3), bbi.reshape(shape3)]).astype(BF16)
    w["cdt"] = _s5_block_diag([wl["s5_c_re"], -wl["s5_c_im"]]).astype(BF16)
    abr_s = abr.reshape(shape3)[:, 0, :]
    abi_s = abi.reshape(shape3)[:, 0, :]
    w["scf"], w["scb"] = _s5_consts(_cplx_to_lanes(abr_s), _cplx_to_lanes(abi_s), ROW_TILE // SUBLANES)
    return w


def _layer_param_grads(g, w, wl):
    out = {}
    shape3 = (S5_GROUPS, S5_GROUP, S5_STATE)
    dbr, dbi = _s5_block_diag_extract(g["bd"])
    dcr, dci = _s5_block_diag_extract(g["cdt"])
    out["s5_c_re"], out["s5_c_im"] = dcr, -dci
    zeros = jnp.zeros(shape3, F32)
    dar = zeros.at[:, 0, :].set(g["abar"][0].reshape(S5_GROUPS, S5_STATE)).reshape(S5_WIDTH, S5_STATE)
    dai = zeros.at[:, 0, :].set(g["abar"][1].reshape(S5_GROUPS, S5_STATE)).reshape(S5_WIDTH, S5_STATE)
    cts = (dar, dai, dbr.reshape(S5_WIDTH, S5_STATE), dbi.reshape(S5_WIDTH, S5_STATE))
    d_are, d_aim, d_ldt, d_bre, d_bim = _s5_prep_bwd(*w["prep_in"], cts)
    out["s5_a_re"] = d_are.reshape(shape3).sum(axis=1)
    out["s5_a_im"] = d_aim.reshape(shape3).sum(axis=1)
    out["s5_log_dt"] = d_ldt.reshape(shape3).sum(axis=(1, 2))
    out["s5_b_re"] = jnp.transpose(d_bre.reshape(shape3), (0, 2, 1))
    out["s5_b_im"] = jnp.transpose(d_bim.reshape(shape3), (0, 2, 1))
    out["s5_d"] = g["s5_d"].reshape(-1)
    for k in ("g_pre", "g_post", "conv_b", "lru_b_a", "lru_b_x", "lru_lambda"):
        out[k] = g[k].reshape(-1)
    for k in ("lru_w_a", "lru_w_x", "conv_w", "w_in", "w_glu", "w_bs", "w_bl", "w_out", "w_ple", "w_ple_gate"):
        out[k] = g[k]
    return out


def _local_step(x, p, layers, target):
    rows = x.shape[0]
    ws = [_layer_weights(wl) for wl in layers]
    saved = []
    for i in range(DEPTH):
        x, sv = _layer_fwd(x, p[i], ws[i], rows)
        saved.append(sv)

    def f_loss(i, yb, tb):
        e = yb - tb
        return e * (1.0 / D_MODEL), jnp.sum(jnp.sum(e * e, axis=0, keepdims=True), axis=1, keepdims=True)

    dx, sq = _rows("f_loss", f_loss, [(x, "row"), (target, "row")],
                   [((rows, D_MODEL), F32, "row"), ((1, 1), F32, "acc")], rows=rows, tile=ROW_TILE)
    loss = sq[0, 0] * (0.5 / D_MODEL)
    grads = [None] * DEPTH
    for i in reversed(range(DEPTH)):
        dx, g = _layer_bwd(dx, saved[i], ws[i], rows)
        grads[i] = _layer_param_grads(g, ws[i], layers[i])
    return loss, dx, grads


def _place():
    return lax.axis_index("x"), lax.axis_index("y"), lax.axis_index("c")


def _other_chips(x, y):
    return [(1 - x, y), (x, 1 - y), (1 - x, 1 - y)]


def _any_spec():
    return pl.BlockSpec(memory_space=pl.ANY)


ICI_PIECES = 1
D2D_PIECES = 1
D2D_SOLO_PIECES = 1


def _pieces(rows, k):
    step = rows // k
    assert step * k == rows and step % 16 == 0, (rows, k)
    return [(q * step, step) for q in range(k)]


def _gather_chips(name, v, via_sibling):
    rows = v.shape[0]
    half = rows // 2

    n_sent = half if via_sibling else rows

    def body(v_ref, out_ref, send_sems, recv_sems):
        x, y, c = _place()
        me = 2 * x + y
        chips = _other_chips(x, y)
        slots = [2 * cx + cy for cx, cy in chips]

        def part(slot, hc, o=0, s=n_sent):
            return out_ref.at[slot, pl.ds(hc * half + o, s), :] if via_sibling else out_ref.at[slot, pl.ds(o, s), :]

        def own(o=0, s=n_sent):
            return v_ref.at[pl.ds(c * half + o, s), :] if via_sibling else v_ref.at[pl.ds(o, s), :]

        def copy(k, src, dst, to):
            return pltpu.make_async_remote_copy(src_ref=src, dst_ref=dst, send_sem=send_sems.at[k], recv_sem=recv_sems.at[k],
                                                device_id=to, device_id_type=MESH)

        for k in range(3):
            for o, s in _pieces(n_sent, ICI_PIECES):
                copy(k, own(o, s), part(me, c, o, s), (*chips[k], c)).start()
        for k in range(3):
            copy(k, own(), part(slots[k], c), (*chips[k], c)).wait_recv()
            if via_sibling:
                for o, s in _pieces(n_sent, D2D_PIECES):
                    copy(3 + k, part(slots[k], c, o, s), part(slots[k], c, o, s), (x, y, 1 - c)).start()
        if via_sibling:
            for k in range(3):
                copy(3 + k, own(), part(slots[k], 1 - c), (x, y, 1 - c)).wait_recv()
        for k in range(6 if via_sibling else 3):
            copy(k, own(), part(me, c), (x, y, 1 - c)).wait_send()

    n_sem = 6 if via_sibling else 3
    others = pl.pallas_call(
        body, name=name, out_shape=jax.ShapeDtypeStruct((4,) + v.shape, v.dtype),
        in_specs=[_any_spec()], out_specs=_any_spec(),
        scratch_shapes=[pltpu.SemaphoreType.DMA((n_sem,)), pltpu.SemaphoreType.DMA((n_sem,))],
    )(v)
    return lax.dynamic_update_slice(others, v[None], (2 * lax.axis_index("x") + lax.axis_index("y"), 0, 0))


def _rs_sibling(grs):
    n = len(grs)
    halves = [g.shape[1] // 2 for g in grs]

    def body(*refs):
        g_refs, got_refs, send_sems, recv_sems = refs[:n], refs[n:2 * n], refs[2 * n], refs[2 * n + 1]
        x, y, c = _place()
        copies = [pltpu.make_async_remote_copy(
            src_ref=g_refs[a].at[:, pl.ds((1 - c) * halves[a], halves[a]), :], dst_ref=got_refs[a],
            send_sem=send_sems.at[a], recv_sem=recv_sems.at[a], device_id=(x, y, 1 - c), device_id_type=MESH)
            for a in range(n)]
        for cp in copies:
            cp.start()
        for cp in copies:
            cp.wait()

    return pl.pallas_call(
        body, name="rs_sibling",
        out_shape=[jax.ShapeDtypeStruct((4, h, g.shape[2]), F32) for g, h in zip(grs, halves)],
        in_specs=[_any_spec()] * n, out_specs=[_any_spec()] * n,
        scratch_shapes=[pltpu.SemaphoreType.DMA((n,)), pltpu.SemaphoreType.DMA((n,))],
    )(*grs)


def _rs_chips(a16s):
    n = len(a16s)

    def body(*refs):
        a_refs, got_refs, send_sems, recv_sems = refs[:n], refs[n:2 * n], refs[2 * n], refs[2 * n + 1]
        x, y, c = _place()
        chips = _other_chips(x, y)
        copies = [pltpu.make_async_remote_copy(
            src_ref=a_refs[a].at[2 * cx + cy], dst_ref=got_refs[a].at[k], send_sem=send_sems.at[3 * a + k],
            recv_sem=recv_sems.at[3 * a + k], device_id=(cx, cy, c), device_id_type=MESH)
            for a in range(n) for k, (cx, cy) in enumerate(chips)]
        for cp in copies:
            cp.start()
        for cp in copies:
            cp.wait()

    return pl.pallas_call(
        body, name="rs_chips", out_shape=[jax.ShapeDtypeStruct((3,) + a.shape[1:], a.dtype) for a in a16s],
        in_specs=[_any_spec()] * n, out_specs=[_any_spec()] * n,
        scratch_shapes=[pltpu.SemaphoreType.DMA((3 * n,)), pltpu.SemaphoreType.DMA((3 * n,))],
    )(*a16s)


def _swap_halves(vs):
    n = len(vs)

    def body(*refs):
        v_refs, out_refs, send_sems, recv_sems = refs[:n], refs[n:2 * n], refs[2 * n], refs[2 * n + 1]
        x, y, c = _place()

        def give(a, hc):
            return pltpu.make_async_remote_copy(src_ref=v_refs[a], dst_ref=out_refs[a].at[hc], send_sem=send_sems.at[a],
                                                recv_sem=recv_sems.at[a], device_id=(x, y, 1 - c), device_id_type=MESH)

        for a in range(n):
            give(a, c).start()
        for a in range(n):
            give(a, c).wait_send()
            give(a, 1 - c).wait_recv()

    others = pl.pallas_call(
        body, name="swap_halves", out_shape=[jax.ShapeDtypeStruct((2,) + v.shape, v.dtype) for v in vs],
        in_specs=[_any_spec()] * n, out_specs=[_any_spec()] * n,
        scratch_shapes=[pltpu.SemaphoreType.DMA((n,)), pltpu.SemaphoreType.DMA((n,))],
    )(*vs)
    c = lax.axis_index("c")
    return [lax.dynamic_update_slice(o, v[None], (c, 0, 0)).reshape(2 * v.shape[0], v.shape[1]) for o, v in zip(others, vs)]


WIDE = 1024
PACK_TILE = 3072
GRAD_ROWS_UNIT = 2 * PACK_TILE


def _pack(parts, rows_unit, dtype):
    flat = jnp.concatenate([q.reshape(-1).astype(dtype) for q in parts])
    unit = rows_unit * LANES
    total = -(-flat.shape[0] // unit) * unit
    return jnp.pad(flat, (0, total - flat.shape[0])).reshape(-1, LANES)


def _unpack(flat, shapes, align=1):
    out, off = [], 0
    for s in shapes:
        n = 1
        for q in s:
            n *= q
        out.append(flat[off:off + n].reshape(s))
        off += -(-n // align) * align
    return out


def _pack_tile(rows, width):
    most = (2 ** 21) // (4 * width)
    if rows <= most:
        return rows
    return max(t for t in range(16, most + 1, 16) if rows % t == 0)


def _flat_aligned(v, align):
    v = v.reshape(-1)
    return jnp.pad(v, (0, -v.shape[0] % align))


def _to_slots(name, full):
    dp, r, c = full.shape
    if BIG_SHARD_AXIS[name] == 1:
        return jnp.transpose(full.reshape(dp, r, 4, c // 4), (2, 0, 1, 3)).reshape(4, -1)
    return jnp.transpose(full.reshape(dp, 4, r // 4, c), (1, 0, 2, 3)).reshape(4, -1)


def _from_slots(name, slots, shard_shape):
    dp, r, c = shard_shape
    v = slots.reshape(4, dp, r, c)
    if BIG_SHARD_AXIS[name] == 1:
        return jnp.transpose(v, (1, 2, 0, 3)).reshape(dp, r, 4 * c)
    return jnp.transpose(v, (1, 0, 2, 3)).reshape(dp, 4 * r, c)


def _adamw(name, w, g, m, v, tile):
    def fn(i, wb, gb, mb, vb):
        m2 = ADAM_B1 * mb + (1.0 - ADAM_B1) * gb
        v2 = ADAM_B2 * vb + (1.0 - ADAM_B2) * (gb * gb)
        m_hat = m2 / (1.0 - ADAM_B1 ** ADAM_STEP)
        v_hat = v2 / (1.0 - ADAM_B2 ** ADAM_STEP)
        return -ADAM_LR * (m_hat / (jnp.sqrt(v_hat) + ADAM_EPS) + ADAM_WD * wb), m2, v2

    return _rows(name, fn, [(w, "row"), (g, "row"), (m, "row"), (v, "row")], [(w.shape, F32, "row")] * 3,
                 rows=w.shape[0], tile=tile)


def _as_2d(a):
    return a.reshape(-1, a.shape[-1])


def _adam_tile(rows):
    for t in (256, 184, 128, 64, 32, 16, 8):
        if rows % t == 0:
            return t
    return rows


def kernel(x, p, g_pre, w_in, s5_a_re, s5_a_im, s5_log_dt, s5_b_re, s5_b_im, s5_c_re, s5_c_im, s5_d, w_glu, w_bs, conv_w, conv_b, lru_w_a, lru_b_a, lru_w_x, lru_b_x, lru_lambda, w_bl, w_out, g_post, w_ple, w_ple_gate, loss_target, m_g_pre, m_w_in, m_s5_a_re, m_s5_a_im, m_s5_log_dt, m_s5_b_re, m_s5_b_im, m_s5_c_re, m_s5_c_im, m_s5_d, m_w_glu, m_w_bs, m_conv_w, m_conv_b, m_lru_w_a, m_lru_b_a, m_lru_w_x, m_lru_b_x, m_lru_lambda, m_w_bl, m_w_out, m_g_post, m_w_ple, m_w_ple_gate, v_g_pre, v_w_in, v_s5_a_re, v_s5_a_im, v_s5_log_dt, v_s5_b_re, v_s5_b_im, v_s5_c_re, v_s5_c_im, v_s5_d, v_w_glu, v_w_bs, v_conv_w, v_conv_b, v_lru_w_a, v_lru_b_a, v_lru_w_x, v_lru_b_x, v_lru_lambda, v_w_bl, v_w_out, v_g_post, v_w_ple, v_w_ple_gate):
    wts = dict(g_pre=g_pre, w_in=w_in, s5_a_re=s5_a_re, s5_a_im=s5_a_im, s5_log_dt=s5_log_dt, s5_b_re=s5_b_re,
               s5_b_im=s5_b_im, s5_c_re=s5_c_re, s5_c_im=s5_c_im, s5_d=s5_d, w_glu=w_glu, w_bs=w_bs, conv_w=conv_w,
               conv_b=conv_b, lru_w_a=lru_w_a, lru_b_a=lru_b_a, lru_w_x=lru_w_x, lru_b_x=lru_b_x, lru_lambda=lru_lambda,
               w_bl=w_bl, w_out=w_out, g_post=g_post, w_ple=w_ple, w_ple_gate=w_ple_gate)
    mom1 = dict(g_pre=m_g_pre, w_in=m_w_in, s5_a_re=m_s5_a_re, s5_a_im=m_s5_a_im, s5_log_dt=m_s5_log_dt, s5_b_re=m_s5_b_re,
                s5_b_im=m_s5_b_im, s5_c_re=m_s5_c_re, s5_c_im=m_s5_c_im, s5_d=m_s5_d, w_glu=m_w_glu, w_bs=m_w_bs,
                conv_w=m_conv_w, conv_b=m_conv_b, lru_w_a=m_lru_w_a, lru_b_a=m_lru_b_a, lru_w_x=m_lru_w_x, lru_b_x=m_lru_b_x,
                lru_lambda=m_lru_lambda, w_bl=m_w_bl, w_out=m_w_out, g_post=m_g_post, w_ple=m_w_ple, w_ple_gate=m_w_ple_gate)
    mom2 = dict(g_pre=v_g_pre, w_in=v_w_in, s5_a_re=v_s5_a_re, s5_a_im=v_s5_a_im, s5_log_dt=v_s5_log_dt, s5_b_re=v_s5_b_re,
                s5_b_im=v_s5_b_im, s5_c_re=v_s5_c_re, s5_c_im=v_s5_c_im, s5_d=v_s5_d, w_glu=v_w_glu, w_bs=v_w_bs,
                conv_w=v_conv_w, conv_b=v_conv_b, lru_w_a=v_lru_w_a, lru_b_a=v_lru_b_a, lru_w_x=v_lru_w_x, lru_b_x=v_lru_b_x,
                lru_lambda=v_lru_lambda, w_bl=v_w_bl, w_out=v_w_out, g_post=v_g_post, w_ple=v_w_ple, w_ple_gate=v_w_ple_gate)
    names = list(wts)

    def wire(name):
        return lax.bitcast_convert_type(wts[name], BF16) if name == "conv_w" else wts[name].astype(BF16)

    wire_shapes = [wire(k).shape for k in BIG]
    gathered = _gather_chips("gather_weights", _pack([wire(k) for k in BIG], 128, BF16), via_sibling=True)
    per_chip = [_unpack(gathered[j].reshape(-1), wire_shapes) for j in range(4)]
    whole = {}
    for idx, k in enumerate(BIG):
        if k == "w_in":
            continue
        v = jnp.stack([per_chip[j][idx] for j in range(4)])
        if k == "conv_w":
            v = lax.bitcast_convert_type(v, F32)
        whole[k] = _from_slots(k, v.reshape(4, -1), wts[k].shape)
    layers = []
    for i in range(DEPTH):
        wl = {k: whole[k][i] for k in BIG if k != "w_in"}
        wl["w_in"] = [per_chip[j][BIG.index("w_in")][i] for j in range(4)]
        wl.update({k: wts[k][i] for k in SMALL})
        layers.append(wl)

    loss, grad_x, grads = _local_step(x[0], p[:, 0], layers, loss_target[0])
    loss = lax.psum(loss, ("x", "y", "c"))

    me = 2 * lax.axis_index("x") + lax.axis_index("y")
    c = lax.axis_index("c")
    by_rows, by_cols = ("w_bl", "w_out", "w_ple_gate"), ("w_glu", "w_bs", "w_ple")
    rows_of = lambda k, i, j: grads[i][k][j * (grads[i][k].shape[0] // 4):(j + 1) * (grads[i][k].shape[0] // 4)]
    cols_of = lambda k, i, j: grads[i][k][:, j * (grads[i][k].shape[1] // 4):(j + 1) * (grads[i][k].shape[1] // 4)]
    layer_range = range(DEPTH)
    packs = [
        jnp.stack([jnp.concatenate([grads[i]["w_in"][j] for i in layer_range]) for j in range(4)]),
        jnp.stack([jnp.concatenate([rows_of(k, i, j) for k in by_rows for i in layer_range]) for j in range(4)]),
        jnp.stack([jnp.concatenate([cols_of(k, i, j) for k in by_cols for i in layer_range]) for j in range(4)]),
    ]
    small_names = SMALL + ("conv_w",)
    small_shapes = [(DEPTH,) + grads[0][k].shape for k in small_names]
    small_flat = jnp.concatenate([_flat_aligned(jnp.stack([grads[i][k] for i in layer_range]), WIDE) for k in small_names])
    n_small = small_flat.shape[0]
    small_q = -(-n_small // (4 * 32 * LANES)) * 32 * LANES
    packs.append(jnp.pad(small_flat, (0, 4 * small_q - n_small)).reshape(4, small_q // LANES, LANES))

    gots = _rs_sibling(packs)
    a32s, a16s = [], []
    for pk, got in zip(packs, gots):
        half, width = got.shape[1], got.shape[2]
        mine = lax.dynamic_slice_in_dim(pk, c * half, half, axis=1)

        def f_add1(i, a, b):
            s = a + b
            return s, s

        a32, a16 = _rows("rs_add1", f_add1, [(mine.reshape(4 * half, width), "row"), (got.reshape(4 * half, width), "row")],
                         [((4 * half, width), F32, "row"), ((4 * half, width), BF16, "row")], rows=4 * half,
                         tile=_pack_tile(4 * half, width))
        a32s.append(a32.reshape(4, half, width))
        a16s.append(a16.reshape(4, half, width))
    got3s = _rs_chips(a16s)
    red_halves = []
    for a32, got3 in zip(a32s, got3s):
        half, width = a32.shape[1], a32.shape[2]
        own = lax.dynamic_index_in_dim(a32, me, 0, keepdims=False)

        def f_add2(i, o, g0, g1, g2):
            return (((o + g0) + g1) + g2,)

        red_halves.append(_rows("rs_add2", f_add2, [(own, "row")] + [(got3[k], "row") for k in range(3)],
                                [((half, width), F32, "row")], rows=half, tile=_pack_tile(half, width))[0])
    reds = _swap_halves(red_halves)
    small_red = _gather_chips("gather_small", reds[3], via_sibling=False).reshape(-1)[:n_small]

    grad_out = {"w_in": reds[0].reshape(wts["w_in"].shape)}
    for red, ks in ((reds[1], by_rows), (reds[2], by_cols)):
        off = 0
        for k in ks:
            n = wts[k].shape[0] * wts[k].shape[1]
            grad_out[k] = red[off:off + n].reshape(wts[k].shape)
            off += n
    small_out = dict(zip(small_names, _unpack(small_red, small_shapes, align=WIDE)))
    grad_out.update({k: small_out[k] for k in SMALL})
    grad_out["conv_w"] = lax.dynamic_slice_in_dim(small_out["conv_w"], me * wts["conv_w"].shape[2], wts["conv_w"].shape[2], axis=2)
    delta, new_m, new_v = {}, {}, {}
    for k in BIG + SMALL:
        w2 = _as_2d(wts[k])
        res = _adamw("adamw_" + k, w2, _as_2d(grad_out[k]), _as_2d(mom1[k]), _as_2d(mom2[k]), _adam_tile(w2.shape[0]))
        delta[k], new_m[k], new_v[k] = [r.reshape(wts[k].shape) for r in res]
    return (loss, grad_x[None], *[grad_out[k] for k in names], *[delta[k] for k in names],
            *[new_m[k] for k in names], *[new_v[k] for k in names])
```

```python
import jax
import jax.numpy as jnp
from jax import lax
from jax.experimental import pallas as pl
from jax.experimental.pallas import tpu as pltpu

F32 = jnp.float32
BF16 = jnp.bfloat16
MESH = pl.DeviceIdType.MESH

DEPTH = 2
D_MODEL = 1024
NORM_EPS = 1e-6
S5_WIDTH = 512
S5_GROUPS = 32
S5_GROUP = 16
S5_STATE = 64
LRU_WIDTH = 1280
LRU_HEADS = 10
LRU_HEAD_DIM = 128
LRU_C = 8.0
CONV_WIDTH = 4
PLE_DIM = 256
IN_WIDTHS = (S5_WIDTH, S5_WIDTH, LRU_WIDTH, LRU_WIDTH, D_MODEL, D_MODEL)
IN_OFFSETS = (0, 512, 1024, 2304, 3584, 4608, 5632)
IN_SLOT = 5632 // 4
ADAM_LR = 0.001
ADAM_B1 = 0.9
ADAM_B2 = 0.999
ADAM_EPS = 1e-08
ADAM_WD = 0.01
ADAM_STEP = 10

SUBLANES = 8
LANES = 128
S5_HALF_IN = S5_WIDTH // 2
S5_CPLX = S5_GROUPS * S5_STATE
S5_HALF_CPLX = S5_CPLX // 2
S5_LANES = 2 * S5_CPLX
VMEM_LIMIT = 48 * 2 ** 20
ROW_TILE = 256
WIDE_TILE = 512


def _sigmoid(x):
    return 0.5 * jnp.tanh(0.5 * x) + 0.5


def _gelu_parts(x):
    k = 0.7978845608028654
    t = jnp.tanh(k * (x + 0.044715 * x * x * x))
    val = 0.5 * x * (1.0 + t)
    grad = 0.5 * (1.0 + t) + 0.5 * x * (1.0 - t * t) * k * (1.0 + 3.0 * 0.044715 * x * x)
    return val, grad


def _nn(a, w):
    return jnp.dot(a.astype(BF16), w.astype(BF16), preferred_element_type=F32)


def _nt(a, w):
    return lax.dot_general(a.astype(BF16), w.astype(BF16), (((1,), (1,)), ((), ())), preferred_element_type=F32)


def _tn(a, b):
    return lax.dot_general(a.astype(BF16), b.astype(BF16), (((0,), (0,)), ((), ())), preferred_element_type=F32)


def _heads(op, a, w):
    d = LRU_HEAD_DIM
    return jnp.concatenate([op(a[:, h * d:(h + 1) * d], w[h]) for h in range(LRU_HEADS)], axis=1)


def _heads_tn(a, b):
    d = LRU_HEAD_DIM
    return jnp.stack([_tn(a[:, h * d:(h + 1) * d], b[:, h * d:(h + 1) * d]) for h in range(LRU_HEADS)], axis=0)


def _rows_before(x, halo, s):
    main = pltpu.roll(x, s, 0)
    head = pltpu.roll(jnp.concatenate([halo, x[0:SUBLANES]], axis=0), s, 0)[SUBLANES:2 * SUBLANES]
    return jnp.concatenate([head, main[SUBLANES:]], axis=0)


def _rows_after(x, halo, s):
    n = x.shape[0]
    main = pltpu.roll(x, n - s, 0)
    tail = pltpu.roll(jnp.concatenate([x[n - SUBLANES:], halo], axis=0), 2 * SUBLANES - s, 0)[0:SUBLANES]
    return jnp.concatenate([main[:n - SUBLANES], tail], axis=0)


def _rows(name, fn, ins, outs, *, rows, tile):
    tile = min(tile, rows)
    n = rows // tile
    assert n * tile == rows, (name, rows, tile)
    in_specs = []
    for arr, kind in ins:
        halo = SUBLANES * (4 // arr.dtype.itemsize)
        per, last = tile // halo, rows // halo - 1
        if kind in ("row", "raw"):
            in_specs.append(pl.BlockSpec((tile, arr.shape[1]), lambda i: (i, 0)))
        elif kind == "prev":
            in_specs.append(pl.BlockSpec((halo, arr.shape[1]), lambda i, per=per: (jnp.maximum(i * per - 1, 0), 0)))
        elif kind == "next":
            in_specs.append(pl.BlockSpec((halo, arr.shape[1]),
                                         lambda i, per=per, last=last: (jnp.minimum((i + 1) * per, last), 0)))
        else:
            in_specs.append(pl.BlockSpec(arr.shape, lambda i, nd=arr.ndim: (0,) * nd))
    out_shape, out_specs = [], []
    for shape, dtype, kind in outs:
        out_shape.append(jax.ShapeDtypeStruct(shape, dtype))
        if kind == "row":
            out_specs.append(pl.BlockSpec((tile, shape[1]), lambda i: (i, 0)))
        else:
            out_specs.append(pl.BlockSpec(shape, lambda i, nd=len(shape): (0,) * nd))
    n_in = len(ins)

    def load(ref, kind):
        v = ref[...]
        if kind in ("row", "prev", "next"):
            v = v.astype(F32)
        if kind == "prev":
            v = v[v.shape[0] - SUBLANES:]
        if kind == "next":
            v = v[:SUBLANES]
        return v

    def body(*refs):
        i = pl.program_id(0)
        vals = fn(i, *[load(r, kind) for r, (_, kind) in zip(refs[:n_in], ins)])
        assert len(vals) == len(outs), name
        for r, v, (_, _, kind) in zip(refs[n_in:], vals, outs):
            if kind == "row":
                r[...] = v.astype(r.dtype)
            else:
                @pl.when(i == 0)
                def _():
                    r[...] = jnp.zeros_like(r)

                r[...] += v.astype(r.dtype)

    return pl.pallas_call(
        body, name=name, grid=(n,), in_specs=in_specs, out_specs=out_specs, out_shape=out_shape,
        compiler_params=pltpu.CompilerParams(dimension_semantics=("arbitrary",), vmem_limit_bytes=VMEM_LIMIT),
    )(*[a for a, _ in ins])


def _s5_discretise(are, aim, ldt, bre, bim):
    dt = jnp.exp(ldt)
    er = jnp.exp(are * dt)
    abr = er * jnp.cos(aim * dt)
    abi = er * jnp.sin(aim * dt)
    den = are * are + aim * aim
    zr = ((abr - 1.0) * are + abi * aim) / den
    zi = (abi * are - (abr - 1.0) * aim) / den
    return abr, abi, zr * bre - zi * bim, zr * bim + zi * bre


def _s5_prep(are, aim, ldt, bre, bim):
    def body(a, b, c, d, e, o0, o1, o2, o3):
        r = _s5_discretise(a[...], b[...], c[...], d[...], e[...])
        o0[...], o1[...], o2[...], o3[...] = r

    sd = jax.ShapeDtypeStruct(are.shape, F32)
    return pl.pallas_call(body, name="s5_prep", out_shape=[sd] * 4)(are, aim, ldt, bre, bim)


def _s5_prep_bwd(are, aim, ldt, bre, bim, cts):
    def body(a, b, c, d, e, c0, c1, c2, c3, o0, o1, o2, o3, o4):
        _, vjp = jax.vjp(_s5_discretise, a[...], b[...], c[...], d[...], e[...])
        r = vjp((c0[...], c1[...], c2[...], c3[...]))
        o0[...], o1[...], o2[...], o3[...], o4[...] = r

    sd = jax.ShapeDtypeStruct(are.shape, F32)
    return pl.pallas_call(body, name="s5_prep_bwd", out_shape=[sd] * 5)(are, aim, ldt, bre, bim, *cts)


def _s5_consts(abr, abi, seg):
    shape = (SUBLANES, S5_CPLX)
    assert seg & (seg - 1) == 0 and seg % SUBLANES == 0, seg

    def body(ar_ref, ai_ref, f_ref, b_ref):
        def cmul(p, q):
            return (p[0] * q[0] - p[1] * q[1], p[0] * q[1] + p[1] * q[0])

        row = lax.broadcasted_iota(jnp.int32, shape, 0)
        a1 = (jnp.broadcast_to(ar_ref[...], shape), jnp.broadcast_to(ai_ref[...], shape))
        squares = [a1]
        while 1 << (len(squares) - 1) < 4 * seg:
            squares.append(cmul(squares[-1], squares[-1]))
        nb = seg.bit_length() - 1
        fwd, rev = [], []
        for k, a in ((1, squares[nb]), (2, squares[nb + 1]), (4, squares[nb + 2])):
            fwd += [jnp.where(row >= k, a[0], 0.0), jnp.where(row >= k, a[1], 0.0)]
            rev += [jnp.where(row <= 7 - k, a[0], 0.0), jnp.where(row <= 7 - k, -a[1], 0.0)]
        fwd += [a1[0], a1[1]]
        rev += [a1[0], -a1[1]]
        e = lax.broadcasted_iota(jnp.int32, (seg, S5_CPLX), 0) + 1
        wide = lambda v: jnp.broadcast_to(v[0:1, :], (seg, S5_CPLX))
        pr, pi = jnp.ones((seg, S5_CPLX), F32), jnp.zeros((seg, S5_CPLX), F32)
        for b in range(nb + 1):
            sr, si = wide(squares[b][0]), wide(squares[b][1])
            bit = ((e >> b) & 1) == 1
            pr, pi = jnp.where(bit, pr * sr - pi * si, pr), jnp.where(bit, pr * si + pi * sr, pi)
        f_ref[...] = jnp.concatenate(fwd + [pr, pi], axis=0)
        b_ref[...] = jnp.concatenate(rev + [pr, -pi], axis=0)

    sd = jax.ShapeDtypeStruct((8 * SUBLANES + 2 * seg, S5_CPLX), F32)
    return pl.pallas_call(body, name="s5_consts", out_shape=[sd, sd])(abr, abi)


S5_TILES = S5_LANES // LANES
S5_HALF_TILES = S5_TILES // 2


def _s5_tile_index(q):
    re = (q // 8) * S5_HALF_TILES + (q % 8)
    return re, re + S5_HALF_TILES // 2


def _lanes_of(ref, first, count):
    return jnp.concatenate([ref[j] for j in range(first, first + count)], axis=1)


def _to_lane_tiles(ref, first, value):
    for j in range(value.shape[1] // LANES):
        ref[first + j] = value[:, j * LANES:(j + 1) * LANES]


def _time_perm(tile, transpose=False):
    seg = tile // SUBLANES
    rho = lax.broadcasted_iota(jnp.int32, (tile, tile), 1 if transpose else 0)
    t = lax.broadcasted_iota(jnp.int32, (tile, tile), 0 if transpose else 1)
    return (t == (rho & (SUBLANES - 1)) * seg + (rho >> 3)).astype(BF16)


def _reorder(perm, x):
    out = None
    for _ in range(1 if x.dtype == BF16 else 3):
        piece = x.astype(BF16)
        part = jnp.dot(perm, piece, preferred_element_type=F32)
        out = part if out is None else out + part
        x = x - piece.astype(x.dtype)
    return out


def _s5_scan(s_ref, sc_ref, carry_ref, tile, reverse):
    seg = tile // SUBLANES
    group = 4
    edge = 0 if reverse else SUBLANES - 1
    row = lax.broadcasted_iota(jnp.int32, (SUBLANES, LANES), 0)
    order = range(seg - 1, -1, -1) if reverse else range(seg)
    rows_of = lambda k: pl.ds(k * SUBLANES, SUBLANES)
    base = 8 * SUBLANES

    for q0 in range(0, S5_CPLX // LANES, group):
        qs = list(range(q0, q0 + group))
        tiles = [_s5_tile_index(q) for q in qs]
        cst = lambda k, q: sc_ref[k * SUBLANES:(k + 1) * SUBLANES, q * LANES:(q + 1) * LANES]
        state = [(jnp.zeros((SUBLANES, LANES), F32), jnp.zeros((SUBLANES, LANES), F32)) for _ in qs]
        mult = [(cst(6, q), cst(7, q)) for q in qs]
        for k in order:
            for j, (re, im) in enumerate(tiles):
                ar, ai = mult[j]
                xr, xi = state[j]
                nr = ar * xr - ai * xi + s_ref[re, rows_of(k), :]
                ni = ar * xi + ai * xr + s_ref[im, rows_of(k), :]
                s_ref[re, rows_of(k), :] = nr
                s_ref[im, rows_of(k), :] = ni
                state[j] = (nr, ni)
        start = []
        for j, (q, (re, im)) in enumerate(zip(qs, tiles)):
            er, ei = state[j]
            shift1 = SUBLANES - 1 if reverse else 1
            dr = jnp.where(row == SUBLANES - 1 - edge, carry_ref[re], pltpu.roll(er, shift1, 0))
            di = jnp.where(row == SUBLANES - 1 - edge, carry_ref[im], pltpu.roll(ei, shift1, 0))
            for c, sh in ((0, 1), (2, 2), (4, 4)):
                shift = SUBLANES - sh if reverse else sh
                ar, ai = cst(c, q), cst(c + 1, q)
                sr, si = pltpu.roll(dr, shift, 0), pltpu.roll(di, shift, 0)
                dr, di = dr + ar * sr - ai * si, di + ar * si + ai * sr
            start.append((dr, di))
        for k in order:
            t = seg - 1 - k if reverse else k
            for j, (q, (re, im)) in enumerate(zip(qs, tiles)):
                lanes = slice(q * LANES, (q + 1) * LANES)
                pr = jnp.broadcast_to(sc_ref[base + t:base + t + 1, lanes], (SUBLANES, LANES))
                pi = jnp.broadcast_to(sc_ref[base + seg + t:base + seg + t + 1, lanes], (SUBLANES, LANES))
                cr, ci = start[j]
                xr = s_ref[re, rows_of(k), :] + pr * cr - pi * ci
                xi = s_ref[im, rows_of(k), :] + pr * ci + pi * cr
                s_ref[re, rows_of(k), :] = xr
                s_ref[im, rows_of(k), :] = xi
                if k == order[-1]:
                    carry_ref[re] = jnp.broadcast_to(xr[edge:edge + 1, :], (SUBLANES, LANES))
                    carry_ref[im] = jnp.broadcast_to(xi[edge:edge + 1, :], (SUBLANES, LANES))


def _s5_fwd(u, bd, cdt, dskip, sc, *, rows, tile):
    n = rows // tile

    def body(u_ref, bd_ref, cdt_ref, d_ref, sc_ref, y_ref, s_ref, carry_ref):
        @pl.when(pl.program_id(0) == 0)
        def _():
            carry_ref[...] = jnp.zeros_like(carry_ref)

        ub = _reorder(_time_perm(tile), u_ref[...].astype(BF16)).astype(BF16)
        for h in range(2):
            _to_lane_tiles(s_ref, h * S5_HALF_TILES, jnp.dot(ub[:, h * S5_HALF_IN:(h + 1) * S5_HALF_IN], bd_ref[h],
                                                             preferred_element_type=F32))
        _s5_scan(s_ref, sc_ref, carry_ref, tile, reverse=False)
        ys = [_nt(_lanes_of(s_ref, h * S5_HALF_TILES, S5_HALF_TILES), cdt_ref[h]) for h in range(2)]
        y_ref[...] = _reorder(_time_perm(tile, transpose=True), jnp.concatenate(ys, axis=1)) + d_ref[...] * u_ref[...]

    full = lambda a: pl.BlockSpec(a.shape, lambda i, nd=a.ndim: (0,) * nd)
    return pl.pallas_call(
        body, name="s5_fwd", grid=(n,),
        in_specs=[pl.BlockSpec((tile, S5_WIDTH), lambda i: (i, 0)), full(bd), full(cdt), full(dskip), full(sc)],
        out_specs=[pl.BlockSpec((tile, S5_WIDTH), lambda i: (i, 0)),
                   pl.BlockSpec((S5_TILES, tile, LANES), lambda i: (0, i, 0))],
        out_shape=[jax.ShapeDtypeStruct((rows, S5_WIDTH), F32), jax.ShapeDtypeStruct((S5_TILES, rows, LANES), F32)],
        scratch_shapes=[pltpu.VMEM((S5_TILES, SUBLANES, LANES), F32)],
        compiler_params=pltpu.CompilerParams(dimension_semantics=("arbitrary",), vmem_limit_bytes=VMEM_LIMIT),
    )(u, bd, cdt, dskip, sc)


def _s5_bwd(dy, s, u, bd, cdt, dskip, sc, *, rows, tile):
    n = rows // tile
    hc = 2 * S5_HALF_CPLX
    per8 = tile // SUBLANES
    quarter = S5_HALF_TILES // 2

    def body(dy_ref, s_ref, sp_ref, u_ref, bd_ref, cdt_ref, d_ref, sc_ref,
             du_ref, dbd_ref, dcdt_ref, dd_ref, da_ref, g_ref, carry_ref):
        i = pl.program_id(0)

        @pl.when(i == 0)
        def _():
            carry_ref[...] = jnp.zeros_like(carry_ref)
            dbd_ref[...] = jnp.zeros_like(dbd_ref)
            dcdt_ref[...] = jnp.zeros_like(dcdt_ref)
            dd_ref[...] = jnp.zeros_like(dd_ref)
            da_ref[...] = jnp.zeros_like(da_ref)

        dy = dy_ref[...]
        u = u_ref[...]
        perm = _time_perm(tile)
        dyb = _reorder(perm, dy.astype(BF16)).astype(BF16)
        ub = _reorder(perm, u.astype(BF16)).astype(BF16)
        for h in range(2):
            _to_lane_tiles(g_ref, h * S5_HALF_TILES, jnp.dot(dyb[:, h * S5_HALF_IN:(h + 1) * S5_HALF_IN], cdt_ref[h],
                                                             preferred_element_type=F32))
        _s5_scan(g_ref, sc_ref, carry_ref, tile, reverse=True)
        dus = []
        for h in range(2):
            gb = _lanes_of(g_ref, h * S5_HALF_TILES, S5_HALF_TILES).astype(BF16)
            sb = _lanes_of(s_ref, h * S5_HALF_TILES, S5_HALF_TILES).astype(BF16)
            dus.append(_nt(gb, bd_ref[h]))
            dbd_ref[h] += _tn(ub[:, h * S5_HALF_IN:(h + 1) * S5_HALF_IN], gb)
            dcdt_ref[h] += _tn(dyb[:, h * S5_HALF_IN:(h + 1) * S5_HALF_IN], sb)
        du = _reorder(_time_perm(tile, transpose=True), jnp.concatenate(dus, axis=1))
        du_ref[...] = (du + d_ref[...] * dy).astype(du_ref.dtype)
        dd_ref[...] += jnp.sum(dy * u, axis=0, keepdims=True)

        not_first = (i < n - 1).astype(F32)
        row = lax.broadcasted_iota(jnp.int32, (SUBLANES, quarter * LANES), 0)

        def step_before(first):
            cur = _lanes_of(s_ref, first, quarter)
            before_tile = _lanes_of(sp_ref, first, quarter)[SUBLANES - 1:SUBLANES, :] * not_first
            head = jnp.where(row == 0, before_tile, pltpu.roll(cur[tile - SUBLANES:], 1, 0))
            return jnp.concatenate([head, cur[:tile - SUBLANES]], axis=0)

        for h in range(2):
            re, im = h * S5_HALF_TILES, h * S5_HALF_TILES + quarter
            ssr = step_before(re)
            ssi = step_before(im)
            gr = _lanes_of(g_ref, re, quarter)
            gi = _lanes_of(g_ref, im, quarter)
            lanes = slice(h * S5_HALF_CPLX, (h + 1) * S5_HALF_CPLX)
            da_ref[0:1, lanes] += jnp.sum(ssr * gr + ssi * gi, axis=0, keepdims=True)
            da_ref[1:2, lanes] += jnp.sum(ssr * gi - ssi * gr, axis=0, keepdims=True)

    full = lambda a: pl.BlockSpec(a.shape, lambda i, nd=a.ndim: (0,) * nd)
    rev = lambda i: (n - 1 - i, 0)
    wshape = (2, S5_HALF_IN, hc)
    return pl.pallas_call(
        body, name="s5_bwd", grid=(n,),
        in_specs=[pl.BlockSpec((tile, S5_WIDTH), rev), pl.BlockSpec((S5_TILES, tile, LANES), lambda i: (0, n - 1 - i, 0)),
                  pl.BlockSpec((S5_TILES, SUBLANES, LANES), lambda i: (0, jnp.maximum((n - 1 - i) * per8 - 1, 0), 0)),
                  pl.BlockSpec((tile, S5_WIDTH), rev), full(bd), full(cdt), full(dskip), full(sc)],
        out_specs=[pl.BlockSpec((tile, S5_WIDTH), rev),
                   pl.BlockSpec(wshape, lambda i: (0, 0, 0)), pl.BlockSpec(wshape, lambda i: (0, 0, 0)),
                   pl.BlockSpec((1, S5_WIDTH), lambda i: (0, 0)), pl.BlockSpec((SUBLANES, S5_CPLX), lambda i: (0, 0))],
        out_shape=[jax.ShapeDtypeStruct((rows, S5_WIDTH), BF16), jax.ShapeDtypeStruct(wshape, F32),
                   jax.ShapeDtypeStruct(wshape, F32), jax.ShapeDtypeStruct((1, S5_WIDTH), F32),
                   jax.ShapeDtypeStruct((SUBLANES, S5_CPLX), F32)],
        scratch_shapes=[pltpu.VMEM((S5_TILES, tile, LANES), F32), pltpu.VMEM((S5_TILES, SUBLANES, LANES), F32)],
        compiler_params=pltpu.CompilerParams(dimension_semantics=("arbitrary",), vmem_limit_bytes=VMEM_LIMIT),
    )(dy, s, s, u, bd, cdt, dskip, sc)


def _s5_block_diag(parts):
    v = jnp.stack(parts, axis=2).reshape(2, 16, S5_GROUP, 2, S5_STATE)
    eye = jnp.eye(16, dtype=v.dtype)
    return jnp.einsum("hgcpn,gk->hgcpkn", v, eye).reshape(2, S5_HALF_IN, 2 * S5_HALF_CPLX)


def _s5_block_diag_extract(m):
    v = m.reshape(2, 16, S5_GROUP, 2, 16, S5_STATE)
    d = jnp.diagonal(v, axis1=1, axis2=4)
    d = jnp.transpose(d, (2, 0, 4, 1, 3)).reshape(2, S5_GROUPS, S5_GROUP, S5_STATE)
    return d[0], d[1]


def _cplx_to_lanes(v):
    return v.reshape(1, S5_CPLX)


def _lru_scan_fwd(a, b, *, rows, tile):
    n = rows // tile
    nblk = tile // SUBLANES
    group = 5

    def body(a_ref, b_ref, h_ref, carry_ref):
        @pl.when(pl.program_id(0) == 0)
        def _():
            carry_ref[...] = jnp.zeros_like(carry_ref)

        row = lax.broadcasted_iota(jnp.int32, (SUBLANES, LANES), 0)
        for q0 in range(0, LRU_WIDTH // LANES, group):
            offs = [q * LANES for q in range(q0, q0 + group)]

            def blk(t, carry, offs=offs):
                r0 = pl.multiple_of(t * SUBLANES, SUBLANES)
                new = []
                for j, o in enumerate(offs):
                    av = a_ref[pl.ds(r0, SUBLANES), o:o + LANES]
                    xv = b_ref[pl.ds(r0, SUBLANES), o:o + LANES]
                    for sh in (1, 2, 4):
                        m = row >= sh
                        xs = pltpu.roll(xv, sh, 0)
                        asft = pltpu.roll(av, sh, 0)
                        xv = xv + jnp.where(m, av * xs, 0.0)
                        av = jnp.where(m, av * asft, av)
                    hv = xv + av * carry[j]
                    h_ref[pl.ds(r0, SUBLANES), o:o + LANES] = hv
                    new.append(jnp.broadcast_to(hv[SUBLANES - 1:SUBLANES, :], (SUBLANES, LANES)))
                return tuple(new)

            carry = lax.fori_loop(0, nblk, blk, tuple(carry_ref[:, o:o + LANES] for o in offs), unroll=2)
            for j, o in enumerate(offs):
                carry_ref[:, o:o + LANES] = carry[j]

    spec = pl.BlockSpec((tile, LRU_WIDTH), lambda i: (i, 0))
    return pl.pallas_call(
        body, name="lru_scan_fwd", grid=(n,), in_specs=[spec, spec], out_specs=spec,
        out_shape=jax.ShapeDtypeStruct((rows, LRU_WIDTH), F32),
        scratch_shapes=[pltpu.VMEM((SUBLANES, LRU_WIDTH), F32)],
        compiler_params=pltpu.CompilerParams(dimension_semantics=("arbitrary",), vmem_limit_bytes=VMEM_LIMIT),
    )(a, b)


def _lru_scan_bwd(dh, a, *, rows, tile):
    n = rows // tile
    nblk = tile // SUBLANES
    group = 5

    def body(dh_ref, a_ref, g_ref, cg_ref, ca_ref):
        @pl.when(pl.program_id(0) == 0)
        def _():
            cg_ref[...] = jnp.zeros_like(cg_ref)
            ca_ref[...] = jnp.zeros_like(ca_ref)

        row = lax.broadcasted_iota(jnp.int32, (SUBLANES, LANES), 0)
        for q0 in range(0, LRU_WIDTH // LANES, group):
            offs = [q * LANES for q in range(q0, q0 + group)]

            def blk(t, carry, offs=offs):
                r0 = pl.multiple_of((nblk - 1 - t) * SUBLANES, SUBLANES)
                new = []
                for j, o in enumerate(offs):
                    cg, ca = carry[2 * j], carry[2 * j + 1]
                    araw = a_ref[pl.ds(r0, SUBLANES), o:o + LANES]
                    xv = dh_ref[pl.ds(r0, SUBLANES), o:o + LANES]
                    av = jnp.where(row == SUBLANES - 1, ca, pltpu.roll(araw, SUBLANES - 1, 0))
                    for sh in (1, 2, 4):
                        m = row <= SUBLANES - 1 - sh
                        xs = pltpu.roll(xv, SUBLANES - sh, 0)
                        asft = pltpu.roll(av, SUBLANES - sh, 0)
                        xv = xv + jnp.where(m, av * xs, 0.0)
                        av = jnp.where(m, av * asft, av)
                    gv = xv + av * cg
                    g_ref[pl.ds(r0, SUBLANES), o:o + LANES] = gv
                    new.append(jnp.broadcast_to(gv[0:1, :], (SUBLANES, LANES)))
                    new.append(jnp.broadcast_to(araw[0:1, :], (SUBLANES, LANES)))
                return tuple(new)

            carry0 = tuple(r[:, o:o + LANES] for o in offs for r in (cg_ref, ca_ref))
            carry = lax.fori_loop(0, nblk, blk, carry0, unroll=2)
            for j, o in enumerate(offs):
                cg_ref[:, o:o + LANES] = carry[2 * j]
                ca_ref[:, o:o + LANES] = carry[2 * j + 1]

    spec = pl.BlockSpec((tile, LRU_WIDTH), lambda i: (n - 1 - i, 0))
    return pl.pallas_call(
        body, name="lru_scan_bwd", grid=(n,), in_specs=[spec, spec], out_specs=spec,
        out_shape=jax.ShapeDtypeStruct((rows, LRU_WIDTH), F32),
        scratch_shapes=[pltpu.VMEM((SUBLANES, LRU_WIDTH), F32), pltpu.VMEM((SUBLANES, LRU_WIDTH), F32)],
        compiler_params=pltpu.CompilerParams(dimension_semantics=("arbitrary",), vmem_limit_bytes=VMEM_LIMIT),
    )(dh, a)


def _conv_fwd(i, x, prev, cw, cb):
    prev = prev * (i > 0).astype(F32)
    y = x * cw[3:4, :] + cb
    for s in range(1, CONV_WIDTH):
        y = y + _rows_before(x, prev, s) * cw[3 - s:4 - s, :]
    return y


def _lru_gates(c, wa, ba, wx, bx, lam):
    r = _sigmoid(_heads(_nn, c, wa) + ba)
    ig = _sigmoid(_heads(_nn, c, wx) + bx)
    z = -lam
    sp = jnp.maximum(z, 0.0) + jnp.log(1.0 + jnp.exp(-jnp.abs(z)))
    log_a = -LRU_C * r * sp
    a = jnp.exp(log_a)
    z2 = 2.0 * log_a
    series = -z2 * (1.0 + z2 * (0.5 + z2 * (1.0 / 6.0 + z2 * (1.0 / 24.0 + z2 * (1.0 / 120.0 + z2 / 720.0)))))
    one_minus = jnp.where(z2 > -0.2, series, 1.0 - jnp.exp(z2))
    mult = jnp.sqrt(one_minus)
    return r, ig, sp, a, mult


def _layer_fwd(x, p, w, rows):
    tile = WIDE_TILE
    d = D_MODEL

    def f_in(i, xb, g, *ws):
        rstd = lax.rsqrt(jnp.mean(xb * xb, axis=-1, keepdims=True) + NORM_EPS)
        hb = (xb * rstd * g).astype(BF16)
        proj = jnp.concatenate([jnp.dot(hb, wj, preferred_element_type=F32) for wj in ws], axis=1)
        return tuple(proj[:, IN_OFFSETS[k]:IN_OFFSETS[k + 1]] for k in range(6)) + (hb,)

    s5x, s5g, lrux, lrug, gs, gl, h = _rows(
        "f_in", f_in, [(x, "row"), (w["g_pre"], "full")] + [(wc, "full") for wc in w["w_in"]],
        [((rows, wd), BF16, "row") for wd in IN_WIDTHS] + [((rows, d), BF16, "row")], rows=rows, tile=ROW_TILE)

    ys, st = _s5_fwd(s5x, w["bd"], w["cdt"], w["s5_d"], w["scf"], rows=rows, tile=ROW_TILE)

    def f_s5post(i, ysb, gb, wglu, wbs):
        glv, _ = _gelu_parts(ysb)
        glu = _nn(glv, wglu)
        y2 = glu[:, :S5_WIDTH] * _sigmoid(glu[:, S5_WIDTH:]) * (gb * _sigmoid(gb))
        return (_nn(y2, wbs),)

    (z_s,) = _rows("f_s5post", f_s5post, [(ys, "row"), (s5g, "row"), (w["w_glu"], "full"), (w["w_bs"], "full")],
                   [((rows, d), BF16, "row")], rows=rows, tile=tile)

    def f_gates(i, xb, prev, cw, cb, wa, ba, wx, bx, lam):
        c = _conv_fwd(i, xb, prev, cw, cb)
        _, ig, _, a, mult = _lru_gates(c, wa, ba, wx, bx, lam)
        return a, mult * (ig * c)

    a, b = _rows("f_gates", f_gates,
                 [(lrux, "row"), (lrux, "prev"), (w["conv_w"], "full"), (w["conv_b"], "full"), (w["lru_w_a"], "full"),
                  (w["lru_b_a"], "full"), (w["lru_w_x"], "full"), (w["lru_b_x"], "full"), (w["lru_lambda"], "full")],
                 [((rows, LRU_WIDTH), F32, "row")] * 2, rows=rows, tile=tile)
    hl = _lru_scan_fwd(a, b, rows=rows, tile=tile)

    def f_merge(i, hb, lg, zs, gsb, glb, xb, wbl, wout, gpost):
        z_l = _nn(hb * (lg * _sigmoid(lg)), wbl)
        merged = _sigmoid(gsb) * zs + _sigmoid(glb) * z_l
        mix = _nn(merged, wout)
        rstd = lax.rsqrt(jnp.mean(mix * mix, axis=-1, keepdims=True) + NORM_EPS)
        return xb + mix * rstd * gpost, mix, z_l

    x1, mix, z_l = _rows("f_merge", f_merge,
                         [(hl, "row"), (lrug, "row"), (z_s, "row"), (gs, "row"), (gl, "row"), (x, "row"),
                          (w["w_bl"], "full"), (w["w_out"], "full"), (w["g_post"], "full")],
                         [((rows, d), F32, "row"), ((rows, d), BF16, "row"), ((rows, d), BF16, "row")], rows=rows, tile=tile)

    def f_ple(i, x1b, pb, wple, wpg):
        return (x1b + _nn(pb, wple) * _sigmoid(_nn(x1b, wpg)),)

    (x2,) = _rows("f_ple", f_ple, [(x1, "row"), (p, "row"), (w["w_ple"], "full"), (w["w_ple_gate"], "full")],
                  [((rows, d), F32, "row")], rows=rows, tile=tile)
    saved = dict(x=x, h=h, s5x=s5x, s5g=s5g, lrux=lrux, lrug=lrug, gs=gs, gl=gl, ys=ys, st=st, a=a, hl=hl, z_s=z_s,
                 z_l=z_l, mix=mix, x1=x1, p=p)
    return x2, saved


def _layer_bwd(dx2, sv, w, rows):
    tile = WIDE_TILE
    d = D_MODEL
    g = {}

    def b_ple(i, dxb, x1b, pb, wple, wpg):
        pe = _nn(pb, wple)
        sg = _sigmoid(_nn(x1b, wpg))
        dpe = dxb * sg
        dgt = dxb * pe * sg * (1.0 - sg)
        return dxb + _nt(dgt, wpg), _tn(pb, dpe), _tn(x1b, dgt)

    dx1, g["w_ple"], g["w_ple_gate"] = _rows(
        "b_ple", b_ple, [(dx2, "row"), (sv["x1"], "row"), (sv["p"], "row"), (w["w_ple"], "full"), (w["w_ple_gate"], "full")],
        [((rows, d), F32, "row"), ((PLE_DIM, d), F32, "acc"), ((d, d), F32, "acc")], rows=rows, tile=tile)

    def b_merge(i, dxb, mixb, zs, zl, gsb, glb, wout, gpost):
        rstd = lax.rsqrt(jnp.mean(mixb * mixb, axis=-1, keepdims=True) + NORM_EPS)
        nrm = mixb * rstd
        dn = dxb * gpost
        dmix = rstd * (dn - nrm * jnp.mean(dn * nrm, axis=-1, keepdims=True))
        ss, sl = _sigmoid(gsb), _sigmoid(glb)
        merged = ss * zs + sl * zl
        dm = _nt(dmix, wout)
        return (dm * ss, dm * sl, dm * zs * ss * (1.0 - ss), dm * zl * sl * (1.0 - sl),
                _tn(merged, dmix), jnp.sum(dxb * nrm, axis=0, keepdims=True))

    dz_s, dz_l, dgs, dgl, g["w_out"], g["g_post"] = _rows(
        "b_merge", b_merge,
        [(dx1, "row"), (sv["mix"], "row"), (sv["z_s"], "row"), (sv["z_l"], "row"), (sv["gs"], "row"), (sv["gl"], "row"),
         (w["w_out"], "full"), (w["g_post"], "full")],
        [((rows, d), BF16, "row")] * 4 + [((d, d), F32, "acc"), ((1, d), F32, "acc")], rows=rows, tile=tile)

    def b_bl(i, dzl, hb, lg, wbl):
        sl = _sigmoid(lg)
        silu = lg * sl
        dy3 = _nt(dzl, wbl)
        return dy3 * silu, dy3 * hb * sl * (1.0 + lg * (1.0 - sl)), _tn(hb * silu, dzl)

    dh, dlrug, g["w_bl"] = _rows(
        "b_bl", b_bl, [(dz_l, "row"), (sv["hl"], "row"), (sv["lrug"], "row"), (w["w_bl"], "full")],
        [((rows, LRU_WIDTH), F32, "row"), ((rows, LRU_WIDTH), BF16, "row"), ((LRU_WIDTH, d), F32, "acc")], rows=rows, tile=tile)

    gh = _lru_scan_bwd(dh, sv["a"], rows=rows, tile=tile)

    def b_gates(i, ghb, hb, hprev, xb, xprev, cw, cb, wa, ba, wx, bx, lam):
        c = _conv_fwd(i, xb, xprev, cw, cb)
        r, ig, sp, a, mult = _lru_gates(c, wa, ba, wx, bx, lam)
        h_before = _rows_before(hb, hprev * (i > 0).astype(F32), 1)
        da = ghb * h_before
        dmult = ghb * ig * c
        dlog_a = da * a - dmult * a * a / mult
        dpre_r = dlog_a * (-LRU_C) * sp * r * (1.0 - r)
        dpre_i = ghb * mult * c * ig * (1.0 - ig)
        dc = ghb * mult * ig + _heads(_nt, dpre_r, wa) + _heads(_nt, dpre_i, wx)
        dlam = jnp.sum(dlog_a * LRU_C * r, axis=0, keepdims=True) * _sigmoid(-lam)
        return (dc, _heads_tn(c, dpre_r), _heads_tn(c, dpre_i), jnp.sum(dpre_r, axis=0, keepdims=True),
                jnp.sum(dpre_i, axis=0, keepdims=True), dlam)

    hshape = (LRU_HEADS, LRU_HEAD_DIM, LRU_HEAD_DIM)
    dc, g["lru_w_a"], g["lru_w_x"], g["lru_b_a"], g["lru_b_x"], g["lru_lambda"] = _rows(
        "b_gates", b_gates,
        [(gh, "row"), (sv["hl"], "row"), (sv["hl"], "prev"), (sv["lrux"], "row"), (sv["lrux"], "prev"),
         (w["conv_w"], "full"), (w["conv_b"], "full"), (w["lru_w_a"], "full"), (w["lru_b_a"], "full"),
         (w["lru_w_x"], "full"), (w["lru_b_x"], "full"), (w["lru_lambda"], "full")],
        [((rows, LRU_WIDTH), BF16, "row"), (hshape, F32, "acc"), (hshape, F32, "acc")] + [((1, LRU_WIDTH), F32, "acc")] * 3,
        rows=rows, tile=tile)

    n_tiles = rows // min(tile, rows)

    def b_conv(i, dcb, dnext, xb, xprev, cw):
        dnext = dnext * (i < n_tiles - 1).astype(F32)
        xprev = xprev * (i > 0).astype(F32)
        dx = dcb * cw[3:4, :]
        dws = [jnp.sum(dcb * xb, axis=0, keepdims=True)]
        for s in range(1, CONV_WIDTH):
            dx = dx + _rows_after(dcb, dnext, s) * cw[3 - s:4 - s, :]
            dws.append(jnp.sum(dcb * _rows_before(xb, xprev, s), axis=0, keepdims=True))
        return dx, jnp.concatenate(dws[::-1], axis=0), jnp.sum(dcb, axis=0, keepdims=True)

    dlrux, g["conv_w"], g["conv_b"] = _rows(
        "b_conv", b_conv, [(dc, "row"), (dc, "next"), (sv["lrux"], "row"), (sv["lrux"], "prev"), (w["conv_w"], "full")],
        [((rows, LRU_WIDTH), BF16, "row"), ((CONV_WIDTH, LRU_WIDTH), F32, "acc"), ((1, LRU_WIDTH), F32, "acc")],
        rows=rows, tile=tile)

    def b_s5post(i, dzs, ysb, gb, wglu, wbs):
        glv, dgelu = _gelu_parts(ysb)
        glu = _nn(glv, wglu)
        ga, gb2 = glu[:, :S5_WIDTH], glu[:, S5_WIDTH:]
        sb = _sigmoid(gb2)
        sg = _sigmoid(gb)
        silu = gb * sg
        y2 = ga * sb * silu
        dy2 = _nt(dzs, wbs)
        dglu = jnp.concatenate([dy2 * sb * silu, dy2 * ga * silu * sb * (1.0 - sb)], axis=1)
        dg = dy2 * ga * sb * sg * (1.0 + gb * (1.0 - sg))
        return _nt(dglu, wglu) * dgelu, dg, _tn(y2, dzs), _tn(glv, dglu)

    dys, ds5g, g["w_bs"], g["w_glu"] = _rows(
        "b_s5post", b_s5post, [(dz_s, "row"), (sv["ys"], "row"), (sv["s5g"], "row"), (w["w_glu"], "full"), (w["w_bs"], "full")],
        [((rows, S5_WIDTH), F32, "row"), ((rows, S5_WIDTH), BF16, "row"), ((S5_WIDTH, d), F32, "acc"),
         ((S5_WIDTH, 2 * S5_WIDTH), F32, "acc")],
        rows=rows, tile=tile)

    ds5x, g["bd"], g["cdt"], g["s5_d"], g["abar"] = _s5_bwd(dys, sv["st"], sv["s5x"], w["bd"], w["cdt"], w["s5_d"],
                                                            w["scb"], rows=rows, tile=ROW_TILE)

    dcomps = [ds5x, ds5g, dlrux, dlrug, dgs, dgl]

    def b_in(i, xb, dx1b, gpre, *rest):
        dproj, ws = jnp.concatenate(rest[:6], axis=1), rest[6:]
        dh = _nt(dproj[:, :IN_SLOT], ws[0])
        for j in range(1, 4):
            dh = dh + _nt(dproj[:, j * IN_SLOT:(j + 1) * IN_SLOT], ws[j])
        rstd = lax.rsqrt(jnp.mean(xb * xb, axis=-1, keepdims=True) + NORM_EPS)
        nrm = xb * rstd
        dn = dh * gpre
        dx = rstd * (dn - nrm * jnp.mean(dn * nrm, axis=-1, keepdims=True))
        return dx1b + dx, jnp.sum(dh * nrm, axis=0, keepdims=True)

    dx, g["g_pre"] = _rows(
        "b_in", b_in, [(sv["x"], "row"), (dx1, "row"), (w["g_pre"], "full")] + [(dcv, "raw") for dcv in dcomps]
        + [(wc, "full") for wc in w["w_in"]],
        [((rows, d), F32, "row"), ((1, d), F32, "acc")], rows=rows, tile=ROW_TILE)

    g["w_in"] = []
    for j in range(4):
        lo, hi = j * IN_SLOT, (j + 1) * IN_SLOT
        ks = [k for k in range(6) if IN_OFFSETS[k] < hi and IN_OFFSETS[k + 1] > lo]
        first = IN_OFFSETS[ks[0]]

        def b_win(i, hb, *parts, lo=lo, hi=hi, first=first):
            return (_tn(hb, jnp.concatenate(parts, axis=1)[:, lo - first:hi - first]),)

        g["w_in"].append(_rows("b_win", b_win, [(sv["h"], "raw")] + [(dcomps[k], "raw") for k in ks],
                               [((d, IN_SLOT), F32, "acc")], rows=rows, tile=4 * ROW_TILE)[0])
    return dx, g


SMALL = ("g_pre", "s5_a_re", "s5_a_im", "s5_log_dt", "s5_b_re", "s5_b_im", "s5_c_re", "s5_c_im", "s5_d", "conv_b",
         "lru_w_a", "lru_b_a", "lru_w_x", "lru_b_x", "lru_lambda", "g_post")
BIG = ("w_in", "w_glu", "w_bs", "conv_w", "w_bl", "w_out", "w_ple", "w_ple_gate")
BIG_SHARD_AXIS = {"w_in": 1, "w_glu": 1, "w_bs": 1, "conv_w": 1, "w_bl": 0, "w_out": 0, "w_ple": 1, "w_ple_gate": 0}


def _bcast_groups(v):
    return jnp.broadcast_to(v[:, None, :], (S5_GROUPS, S5_GROUP, S5_STATE)).reshape(S5_WIDTH, S5_STATE)


def _s5_prep_inputs(wl):
    ldt = jnp.broadcast_to(wl["s5_log_dt"][:, None], (S5_GROUPS, S5_STATE))
    gcn = lambda b: jnp.transpose(b, (0, 2, 1)).reshape(S5_WIDTH, S5_STATE)
    return (_bcast_groups(wl["s5_a_re"]), _bcast_groups(wl["s5_a_im"]), _bcast_groups(ldt), gcn(wl["s5_b_re"]),
            gcn(wl["s5_b_im"]))


def _layer_weights(wl):
    w = {}
    for k in ("w_in", "w_glu", "w_bs", "w_bl", "w_out", "w_ple", "w_ple_gate"):
        w[k] = wl[k]
    w["conv_w"] = wl["conv_w"]
    for k in ("g_pre", "g_post", "s5_d", "conv_b", "lru_b_a", "lru_b_x", "lru_lambda"):
        w[k] = wl[k].reshape(1, -1)
    w["lru_w_a"] = wl["lru_w_a"].astype(BF16)
    w["lru_w_x"] = wl["lru_w_x"].astype(BF16)
    prep_in = _s5_prep_inputs(wl)
    abr, abi, bbr, bbi = _s5_prep(*prep_in)
    w["prep_in"] = prep_in
    shape3 = (S5_GROUPS, S5_GROUP, S5_STATE)
    w["bd"] = _s5_block_diag([bbr.reshape(shape3), bbi.reshape(shape3)]).astype(BF16)
    w["cdt"] = _s5_block_diag([wl["s5_c_re"], -wl["s5_c_im"]]).astype(BF16)
    abr_s = abr.reshape(shape3)[:, 0, :]
    abi_s = abi.reshape(shape3)[:, 0, :]
    w["scf"], w["scb"] = _s5_consts(_cplx_to_lanes(abr_s), _cplx_to_lanes(abi_s), ROW_TILE // SUBLANES)
    return w


def _layer_param_grads(g, w, wl):
    out = {}
    shape3 = (S5_GROUPS, S5_GROUP, S5_STATE)
    dbr, dbi = _s5_block_diag_extract(g["bd"])
    dcr, dci = _s5_block_diag_extract(g["cdt"])
    out["s5_c_re"], out["s5_c_im"] = dcr, -dci
    zeros = jnp.zeros(shape3, F32)
    dar = zeros.at[:, 0, :].set(g["abar"][0].reshape(S5_GROUPS, S5_STATE)).reshape(S5_WIDTH, S5_STATE)
    dai = zeros.at[:, 0, :].set(g["abar"][1].reshape(S5_GROUPS, S5_STATE)).reshape(S5_WIDTH, S5_STATE)
    cts = (dar, dai, dbr.reshape(S5_WIDTH, S5_STATE), dbi.reshape(S5_WIDTH, S5_STATE))
    d_are, d_aim, d_ldt, d_bre, d_bim = _s5_prep_bwd(*w["prep_in"], cts)
    out["s5_a_re"] = d_are.reshape(shape3).sum(axis=1)
    out["s5_a_im"] = d_aim.reshape(shape3).sum(axis=1)
    out["s5_log_dt"] = d_ldt.reshape(shape3).sum(axis=(1, 2))
    out["s5_b_re"] = jnp.transpose(d_bre.reshape(shape3), (0, 2, 1))
    out["s5_b_im"] = jnp.transpose(d_bim.reshape(shape3), (0, 2, 1))
    out["s5_d"] = g["s5_d"].reshape(-1)
    for k in ("g_pre", "g_post", "conv_b", "lru_b_a", "lru_b_x", "lru_lambda"):
        out[k] = g[k].reshape(-1)
    for k in ("lru_w_a", "lru_w_x", "conv_w", "w_in", "w_glu", "w_bs", "w_bl", "w_out", "w_ple", "w_ple_gate"):
        out[k] = g[k]
    return out


def _local_step(x, p, layers, target):
    rows = x.shape[0]
    ws = [_layer_weights(wl) for wl in layers]
    saved = []
    for i in range(DEPTH):
        x, sv = _layer_fwd(x, p[i], ws[i], rows)
        saved.append(sv)

    def f_loss(i, yb, tb):
        e = yb - tb
        return e * (1.0 / D_MODEL), jnp.sum(jnp.sum(e * e, axis=0, keepdims=True), axis=1, keepdims=True)

    dx, sq = _rows("f_loss", f_loss, [(x, "row"), (target, "row")],
                   [((rows, D_MODEL), F32, "row"), ((1, 1), F32, "acc")], rows=rows, tile=WIDE_TILE)
    loss = sq[0, 0] * (0.5 / D_MODEL)
    grads = [None] * DEPTH
    for i in reversed(range(DEPTH)):
        dx, g = _layer_bwd(dx, saved[i], ws[i], rows)
        grads[i] = _layer_param_grads(g, ws[i], layers[i])
    return loss, dx, grads


def _place():
    return lax.axis_index("x"), lax.axis_index("y"), lax.axis_index("c")


def _other_chips(x, y):
    return [(1 - x, y), (x, 1 - y), (1 - x, 1 - y)]


def _any_spec():
    return pl.BlockSpec(memory_space=pl.ANY)


ICI_PIECES = 1
D2D_PIECES = 1
D2D_SOLO_PIECES = 1


def _pieces(rows, k):
    step = rows // k
    assert step * k == rows and step % 16 == 0, (rows, k)
    return [(q * step, step) for q in range(k)]


def _gather_chips(name, v, via_sibling):
    rows = v.shape[0]
    half = rows // 2

    n_sent = half if via_sibling else rows

    def body(v_ref, out_ref, send_sems, recv_sems):
        x, y, c = _place()
        me = 2 * x + y
        chips = _other_chips(x, y)
        slots = [2 * cx + cy for cx, cy in chips]

        def part(slot, hc, o=0, s=n_sent):
            return out_ref.at[slot, pl.ds(hc * half + o, s), :] if via_sibling else out_ref.at[slot, pl.ds(o, s), :]

        def own(o=0, s=n_sent):
            return v_ref.at[pl.ds(c * half + o, s), :] if via_sibling else v_ref.at[pl.ds(o, s), :]

        def copy(k, src, dst, to):
            return pltpu.make_async_remote_copy(src_ref=src, dst_ref=dst, send_sem=send_sems.at[k], recv_sem=recv_sems.at[k],
                                                device_id=to, device_id_type=MESH)

        for k in range(3):
            for o, s in _pieces(n_sent, ICI_PIECES):
                copy(k, own(o, s), part(me, c, o, s), (*chips[k], c)).start()
        for k in range(3):
            copy(k, own(), part(slots[k], c), (*chips[k], c)).wait_recv()
            if via_sibling:
                for o, s in _pieces(n_sent, D2D_PIECES):
                    copy(3 + k, part(slots[k], c, o, s), part(slots[k], c, o, s), (x, y, 1 - c)).start()
        if via_sibling:
            for k in range(3):
                copy(3 + k, own(), part(slots[k], 1 - c), (x, y, 1 - c)).wait_recv()
        for k in range(6 if via_sibling else 3):
            copy(k, own(), part(me, c), (x, y, 1 - c)).wait_send()

    n_sem = 6 if via_sibling else 3
    others = pl.pallas_call(
        body, name=name, out_shape=jax.ShapeDtypeStruct((4,) + v.shape, v.dtype),
        in_specs=[_any_spec()], out_specs=_any_spec(),
        scratch_shapes=[pltpu.SemaphoreType.DMA((n_sem,)), pltpu.SemaphoreType.DMA((n_sem,))],
    )(v)
    return lax.dynamic_update_slice(others, v[None], (2 * lax.axis_index("x") + lax.axis_index("y"), 0, 0))


def _rs_sibling(grs):
    n = len(grs)
    halves = [g.shape[1] // 2 for g in grs]

    def body(*refs):
        g_refs, got_refs, send_sems, recv_sems = refs[:n], refs[n:2 * n], refs[2 * n], refs[2 * n + 1]
        x, y, c = _place()
        copies = [pltpu.make_async_remote_copy(
            src_ref=g_refs[a].at[:, pl.ds((1 - c) * halves[a], halves[a]), :], dst_ref=got_refs[a],
            send_sem=send_sems.at[a], recv_sem=recv_sems.at[a], device_id=(x, y, 1 - c), device_id_type=MESH)
            for a in range(n)]
        for cp in copies:
            cp.start()
        for cp in copies:
            cp.wait()

    return pl.pallas_call(
        body, name="rs_sibling",
        out_shape=[jax.ShapeDtypeStruct((4, h, g.shape[2]), F32) for g, h in zip(grs, halves)],
        in_specs=[_any_spec()] * n, out_specs=[_any_spec()] * n,
        scratch_shapes=[pltpu.SemaphoreType.DMA((n,)), pltpu.SemaphoreType.DMA((n,))],
    )(*grs)


def _rs_chips(a16s):
    n = len(a16s)

    def body(*refs):
        a_refs, got_refs, send_sems, recv_sems = refs[:n], refs[n:2 * n], refs[2 * n], refs[2 * n + 1]
        x, y, c = _place()
        chips = _other_chips(x, y)
        copies = [pltpu.make_async_remote_copy(
            src_ref=a_refs[a].at[2 * cx + cy], dst_ref=got_refs[a].at[k], send_sem=send_sems.at[3 * a + k],
            recv_sem=recv_sems.at[3 * a + k], device_id=(cx, cy, c), device_id_type=MESH)
            for a in range(n) for k, (cx, cy) in enumerate(chips)]
        for cp in copies:
            cp.start()
        for cp in copies:
            cp.wait()

    return pl.pallas_call(
        body, name="rs_chips", out_shape=[jax.ShapeDtypeStruct((3,) + a.shape[1:], a.dtype) for a in a16s],
        in_specs=[_any_spec()] * n, out_specs=[_any_spec()] * n,
        scratch_shapes=[pltpu.SemaphoreType.DMA((3 * n,)), pltpu.SemaphoreType.DMA((3 * n,))],
    )(*a16s)


def _swap_halves(vs):
    n = len(vs)

    def body(*refs):
        v_refs, out_refs, send_sems, recv_sems = refs[:n], refs[n:2 * n], refs[2 * n], refs[2 * n + 1]
        x, y, c = _place()

        def give(a, hc):
            return pltpu.make_async_remote_copy(src_ref=v_refs[a], dst_ref=out_refs[a].at[hc], send_sem=send_sems.at[a],
                                                recv_sem=recv_sems.at[a], device_id=(x, y, 1 - c), device_id_type=MESH)

        for a in range(n):
            give(a, c).start()
        for a in range(n):
            give(a, c).wait_send()
            give(a, 1 - c).wait_recv()

    others = pl.pallas_call(
        body, name="swap_halves", out_shape=[jax.ShapeDtypeStruct((2,) + v.shape, v.dtype) for v in vs],
        in_specs=[_any_spec()] * n, out_specs=[_any_spec()] * n,
        scratch_shapes=[pltpu.SemaphoreType.DMA((n,)), pltpu.SemaphoreType.DMA((n,))],
    )(*vs)
    c = lax.axis_index("c")
    return [lax.dynamic_update_slice(o, v[None], (c, 0, 0)).reshape(2 * v.shape[0], v.shape[1]) for o, v in zip(others, vs)]


WIDE = 1024
PACK_TILE = 3072
GRAD_ROWS_UNIT = 2 * PACK_TILE


def _pack(parts, rows_unit, dtype):
    flat = jnp.concatenate([q.reshape(-1).astype(dtype) for q in parts])
    unit = rows_unit * LANES
    total = -(-flat.shape[0] // unit) * unit
    return jnp.pad(flat, (0, total - flat.shape[0])).reshape(-1, LANES)


def _unpack(flat, shapes, align=1):
    out, off = [], 0
    for s in shapes:
        n = 1
        for q in s:
            n *= q
        out.append(flat[off:off + n].reshape(s))
        off += -(-n // align) * align
    return out


def _pack_tile(rows, width):
    most = (2 ** 21) // (4 * width)
    if rows <= most:
        return rows
    return max(t for t in range(16, most + 1, 16) if rows % t == 0)


def _flat_aligned(v, align):
    v = v.reshape(-1)
    return jnp.pad(v, (0, -v.shape[0] % align))


def _to_slots(name, full):
    dp, r, c = full.shape
    if BIG_SHARD_AXIS[name] == 1:
        return jnp.transpose(full.reshape(dp, r, 4, c // 4), (2, 0, 1, 3)).reshape(4, -1)
    return jnp.transpose(full.reshape(dp, 4, r // 4, c), (1, 0, 2, 3)).reshape(4, -1)


def _from_slots(name, slots, shard_shape):
    dp, r, c = shard_shape
    v = slots.reshape(4, dp, r, c)
    if BIG_SHARD_AXIS[name] == 1:
        return jnp.transpose(v, (1, 2, 0, 3)).reshape(dp, r, 4 * c)
    return jnp.transpose(v, (1, 0, 2, 3)).reshape(dp, 4 * r, c)


def _adamw(name, w, g, m, v, tile):
    def fn(i, wb, gb, mb, vb):
        m2 = ADAM_B1 * mb + (1.0 - ADAM_B1) * gb
        v2 = ADAM_B2 * vb + (1.0 - ADAM_B2) * (gb * gb)
        m_hat = m2 / (1.0 - ADAM_B1 ** ADAM_STEP)
        v_hat = v2 / (1.0 - ADAM_B2 ** ADAM_STEP)
        return -ADAM_LR * (m_hat / (jnp.sqrt(v_hat) + ADAM_EPS) + ADAM_WD * wb), m2, v2

    return _rows(name, fn, [(w, "row"), (g, "row"), (m, "row"), (v, "row")], [(w.shape, F32, "row")] * 3,
                 rows=w.shape[0], tile=tile)


def _as_2d(a):
    return a.reshape(-1, a.shape[-1])


def _adam_tile(rows):
    for t in (256, 184, 128, 64, 32, 16, 8):
        if rows % t == 0:
            return t
    return rows


def kernel(x, p, g_pre, w_in, s5_a_re, s5_a_im, s5_log_dt, s5_b_re, s5_b_im, s5_c_re, s5_c_im, s5_d, w_glu, w_bs, conv_w, conv_b, lru_w_a, lru_b_a, lru_w_x, lru_b_x, lru_lambda, w_bl, w_out, g_post, w_ple, w_ple_gate, loss_target, m_g_pre, m_w_in, m_s5_a_re, m_s5_a_im, m_s5_log_dt, m_s5_b_re, m_s5_b_im, m_s5_c_re, m_s5_c_im, m_s5_d, m_w_glu, m_w_bs, m_conv_w, m_conv_b, m_lru_w_a, m_lru_b_a, m_lru_w_x, m_lru_b_x, m_lru_lambda, m_w_bl, m_w_out, m_g_post, m_w_ple, m_w_ple_gate, v_g_pre, v_w_in, v_s5_a_re, v_s5_a_im, v_s5_log_dt, v_s5_b_re, v_s5_b_im, v_s5_c_re, v_s5_c_im, v_s5_d, v_w_glu, v_w_bs, v_conv_w, v_conv_b, v_lru_w_a, v_lru_b_a, v_lru_w_x, v_lru_b_x, v_lru_lambda, v_w_bl, v_w_out, v_g_post, v_w_ple, v_w_ple_gate):
    wts = dict(g_pre=g_pre, w_in=w_in, s5_a_re=s5_a_re, s5_a_im=s5_a_im, s5_log_dt=s5_log_dt, s5_b_re=s5_b_re,
               s5_b_im=s5_b_im, s5_c_re=s5_c_re, s5_c_im=s5_c_im, s5_d=s5_d, w_glu=w_glu, w_bs=w_bs, conv_w=conv_w,
               conv_b=conv_b, lru_w_a=lru_w_a, lru_b_a=lru_b_a, lru_w_x=lru_w_x, lru_b_x=lru_b_x, lru_lambda=lru_lambda,
               w_bl=w_bl, w_out=w_out, g_post=g_post, w_ple=w_ple, w_ple_gate=w_ple_gate)
    mom1 = dict(g_pre=m_g_pre, w_in=m_w_in, s5_a_re=m_s5_a_re, s5_a_im=m_s5_a_im, s5_log_dt=m_s5_log_dt, s5_b_re=m_s5_b_re,
                s5_b_im=m_s5_b_im, s5_c_re=m_s5_c_re, s5_c_im=m_s5_c_im, s5_d=m_s5_d, w_glu=m_w_glu, w_bs=m_w_bs,
                conv_w=m_conv_w, conv_b=m_conv_b, lru_w_a=m_lru_w_a, lru_b_a=m_lru_b_a, lru_w_x=m_lru_w_x, lru_b_x=m_lru_b_x,
                lru_lambda=m_lru_lambda, w_bl=m_w_bl, w_out=m_w_out, g_post=m_g_post, w_ple=m_w_ple, w_ple_gate=m_w_ple_gate)
    mom2 = dict(g_pre=v_g_pre, w_in=v_w_in, s5_a_re=v_s5_a_re, s5_a_im=v_s5_a_im, s5_log_dt=v_s5_log_dt, s5_b_re=v_s5_b_re,
                s5_b_im=v_s5_b_im, s5_c_re=v_s5_c_re, s5_c_im=v_s5_c_im, s5_d=v_s5_d, w_glu=v_w_glu, w_bs=v_w_bs,
                conv_w=v_conv_w, conv_b=v_conv_b, lru_w_a=v_lru_w_a, lru_b_a=v_lru_b_a, lru_w_x=v_lru_w_x, lru_b_x=v_lru_b_x,
                lru_lambda=v_lru_lambda, w_bl=v_w_bl, w_out=v_w_out, g_post=v_g_post, w_ple=v_w_ple, w_ple_gate=v_w_ple_gate)
    names = list(wts)

    def wire(name):
        return lax.bitcast_convert_type(wts[name], BF16) if name == "conv_w" else wts[name].astype(BF16)

    wire_shapes = [wire(k).shape for k in BIG]
    gathered = _gather_chips("gather_weights", _pack([wire(k) for k in BIG], 128, BF16), via_sibling=True)
    per_chip = [_unpack(gathered[j].reshape(-1), wire_shapes) for j in range(4)]
    whole = {}
    for idx, k in enumerate(BIG):
        if k == "w_in":
            continue
        v = jnp.stack([per_chip[j][idx] for j in range(4)])
        if k == "conv_w":
            v = lax.bitcast_convert_type(v, F32)
        whole[k] = _from_slots(k, v.reshape(4, -1), wts[k].shape)
    layers = []
    for i in range(DEPTH):
        wl = {k: whole[k][i] for k in BIG if k != "w_in"}
        wl["w_in"] = [per_chip[j][BIG.index("w_in")][i] for j in range(4)]
        wl.update({k: wts[k][i] for k in SMALL})
        layers.append(wl)

    loss, grad_x, grads = _local_step(x[0], p[:, 0], layers, loss_target[0])
    loss = lax.psum(loss, ("x", "y", "c"))

    me = 2 * lax.axis_index("x") + lax.axis_index("y")
    c = lax.axis_index("c")
    by_rows, by_cols = ("w_bl", "w_out", "w_ple_gate"), ("w_glu", "w_bs", "w_ple")
    rows_of = lambda k, i, j: grads[i][k][j * (grads[i][k].shape[0] // 4):(j + 1) * (grads[i][k].shape[0] // 4)]
    cols_of = lambda k, i, j: grads[i][k][:, j * (grads[i][k].shape[1] // 4):(j + 1) * (grads[i][k].shape[1] // 4)]
    layer_range = range(DEPTH)
    packs = [
        jnp.stack([jnp.concatenate([grads[i]["w_in"][j] for i in layer_range]) for j in range(4)]),
        jnp.stack([jnp.concatenate([rows_of(k, i, j) for k in by_rows for i in layer_range]) for j in range(4)]),
        jnp.stack([jnp.concatenate([cols_of(k, i, j) for k in by_cols for i in layer_range]) for j in range(4)]),
    ]
    small_names = SMALL + ("conv_w",)
    small_shapes = [(DEPTH,) + grads[0][k].shape for k in small_names]
    small_flat = jnp.concatenate([_flat_aligned(jnp.stack([grads[i][k] for i in layer_range]), WIDE) for k in small_names])
    n_small = small_flat.shape[0]
    small_q = -(-n_small // (4 * 32 * LANES)) * 32 * LANES
    packs.append(jnp.pad(small_flat, (0, 4 * small_q - n_small)).reshape(4, small_q // LANES, LANES))

    gots = _rs_sibling(packs)
    a32s, a16s = [], []
    for pk, got in zip(packs, gots):
        half, width = got.shape[1], got.shape[2]
        mine = lax.dynamic_slice_in_dim(pk, c * half, half, axis=1)

        def f_add1(i, a, b):
            s = a + b
            return s, s

        a32, a16 = _rows("rs_add1", f_add1, [(mine.reshape(4 * half, width), "row"), (got.reshape(4 * half, width), "row")],
                         [((4 * half, width), F32, "row"), ((4 * half, width), BF16, "row")], rows=4 * half,
                         tile=_pack_tile(4 * half, width))
        a32s.append(a32.reshape(4, half, width))
        a16s.append(a16.reshape(4, half, width))
    got3s = _rs_chips(a16s)
    red_halves = []
    for a32, got3 in zip(a32s, got3s):
        half, width = a32.shape[1], a32.shape[2]
        own = lax.dynamic_index_in_dim(a32, me, 0, keepdims=False)

        def f_add2(i, o, g0, g1, g2):
            return (((o + g0) + g1) + g2,)

        red_halves.append(_rows("rs_add2", f_add2, [(own, "row")] + [(got3[k], "row") for k in range(3)],
                                [((half, width), F32, "row")], rows=half, tile=_pack_tile(half, width))[0])
    reds = _swap_halves(red_halves)
    small_red = _gather_chips("gather_small", reds[3], via_sibling=False).reshape(-1)[:n_small]

    grad_out = {"w_in": reds[0].reshape(wts["w_in"].shape)}
    for red, ks in ((reds[1], by_rows), (reds[2], by_cols)):
        off = 0
        for k in ks:
            n = wts[k].shape[0] * wts[k].shape[1]
            grad_out[k] = red[off:off + n].reshape(wts[k].shape)
            off += n
    small_out = dict(zip(small_names, _unpack(small_red, small_shapes, align=WIDE)))
    grad_out.update({k: small_out[k] for k in SMALL})
    grad_out["conv_w"] = lax.dynamic_slice_in_dim(small_out["conv_w"], me * wts["conv_w"].shape[2], wts["conv_w"].shape[2], axis=2)
    delta, new_m, new_v = {}, {}, {}
    for k in BIG + SMALL:
        w2 = _as_2d(wts[k])
        res = _adamw("adamw_" + k, w2, _as_2d(grad_out[k]), _as_2d(mom1[k]), _as_2d(mom2[k]), _adam_tile(w2.shape[0]))
        delta[k], new_m[k], new_v[k] = [r.reshape(wts[k].shape) for r in res]
    return (loss, grad_x[None], *[grad_out[k] for k in names], *[delta[k] for k in names],
            *[new_m[k] for k in names], *[new_v[k] for k in names])
```

```python
import jax
import jax.numpy as jnp
from jax import lax
from jax.experimental import pallas as pl
from jax.experimental.pallas import tpu as pltpu

F32 = jnp.float32
BF16 = jnp.bfloat16
MESH = pl.DeviceIdType.MESH

DEPTH = 2
D_MODEL = 1024
NORM_EPS = 1e-6
S5_WIDTH = 512
S5_GROUPS = 32
S5_GROUP = 16
S5_STATE = 64
LRU_WIDTH = 1280
LRU_HEADS = 10
LRU_HEAD_DIM = 128
LRU_C = 8.0
CONV_WIDTH = 4
PLE_DIM = 256
IN_WIDTHS = (S5_WIDTH, S5_WIDTH, LRU_WIDTH, LRU_WIDTH, D_MODEL, D_MODEL)
IN_OFFSETS = (0, 512, 1024, 2304, 3584, 4608, 5632)
IN_SLOT = 5632 // 4
ADAM_LR = 0.001
ADAM_B1 = 0.9
ADAM_B2 = 0.999
ADAM_EPS = 1e-08
ADAM_WD = 0.01
ADAM_STEP = 10

SUBLANES = 8
LANES = 128
S5_HALF_IN = S5_WIDTH // 2
S5_CPLX = S5_GROUPS * S5_STATE
S5_HALF_CPLX = S5_CPLX // 2
S5_LANES = 2 * S5_CPLX
VMEM_LIMIT = 48 * 2 ** 20
ROW_TILE = 256
WIDE_TILE = 512


def _sigmoid(x):
    return 0.5 * jnp.tanh(0.5 * x) + 0.5


def _gelu_parts(x):
    k = 0.7978845608028654
    t = jnp.tanh(k * (x + 0.044715 * x * x * x))
    val = 0.5 * x * (1.0 + t)
    grad = 0.5 * (1.0 + t) + 0.5 * x * (1.0 - t * t) * k * (1.0 + 3.0 * 0.044715 * x * x)
    return val, grad


def _nn(a, w):
    return jnp.dot(a.astype(BF16), w.astype(BF16), preferred_element_type=F32)


def _nt(a, w):
    return lax.dot_general(a.astype(BF16), w.astype(BF16), (((1,), (1,)), ((), ())), preferred_element_type=F32)


def _tn(a, b):
    return lax.dot_general(a.astype(BF16), b.astype(BF16), (((0,), (0,)), ((), ())), preferred_element_type=F32)


def _heads(op, a, w):
    d = LRU_HEAD_DIM
    return jnp.concatenate([op(a[:, h * d:(h + 1) * d], w[h]) for h in range(LRU_HEADS)], axis=1)


def _heads_tn(a, b):
    d = LRU_HEAD_DIM
    return jnp.stack([_tn(a[:, h * d:(h + 1) * d], b[:, h * d:(h + 1) * d]) for h in range(LRU_HEADS)], axis=0)


def _rows_before(x, halo, s):
    main = pltpu.roll(x, s, 0)
    head = pltpu.roll(jnp.concatenate([halo, x[0:SUBLANES]], axis=0), s, 0)[SUBLANES:2 * SUBLANES]
    return jnp.concatenate([head, main[SUBLANES:]], axis=0)


def _rows_after(x, halo, s):
    n = x.shape[0]
    main = pltpu.roll(x, n - s, 0)
    tail = pltpu.roll(jnp.concatenate([x[n - SUBLANES:], halo], axis=0), 2 * SUBLANES - s, 0)[0:SUBLANES]
    return jnp.concatenate([main[:n - SUBLANES], tail], axis=0)


def _rows(name, fn, ins, outs, *, rows, tile):
    tile = min(tile, rows)
    n = rows // tile
    assert n * tile == rows, (name, rows, tile)
    in_specs = []
    for arr, kind in ins:
        halo = SUBLANES * (4 // arr.dtype.itemsize)
        per, last = tile // halo, rows // halo - 1
        if isinstance(kind, tuple):
            _, j, k, r = kind
            in_specs.append(pl.BlockSpec((None, r, arr.shape[2]), lambda i, j=j, k=k: (j, k, 0)))
        elif kind in ("row", "raw"):
            in_specs.append(pl.BlockSpec((tile, arr.shape[1]), lambda i: (i, 0)))
        elif kind == "prev":
            in_specs.append(pl.BlockSpec((halo, arr.shape[1]), lambda i, per=per: (jnp.maximum(i * per - 1, 0), 0)))
        elif kind == "next":
            in_specs.append(pl.BlockSpec((halo, arr.shape[1]),
                                         lambda i, per=per, last=last: (jnp.minimum((i + 1) * per, last), 0)))
        else:
            in_specs.append(pl.BlockSpec(arr.shape, lambda i, nd=arr.ndim: (0,) * nd))
    out_shape, out_specs = [], []
    for shape, dtype, kind in outs:
        out_shape.append(jax.ShapeDtypeStruct(shape, dtype))
        if kind == "row":
            out_specs.append(pl.BlockSpec((tile, shape[1]), lambda i: (i, 0)))
        else:
            out_specs.append(pl.BlockSpec(shape, lambda i, nd=len(shape): (0,) * nd))
    n_in = len(ins)

    def load(ref, kind):
        v = ref[...]
        if kind in ("row", "prev", "next"):
            v = v.astype(F32)
        if kind == "prev":
            v = v[v.shape[0] - SUBLANES:]
        if kind == "next":
            v = v[:SUBLANES]
        return v

    def body(*refs):
        i = pl.program_id(0)
        vals = fn(i, *[load(r, kind) for r, (_, kind) in zip(refs[:n_in], ins)])
        assert len(vals) == len(outs), name
        for r, v, (_, _, kind) in zip(refs[n_in:], vals, outs):
            if kind == "row":
                r[...] = v.astype(r.dtype)
            else:
                @pl.when(i == 0)
                def _():
                    r[...] = jnp.zeros_like(r)

                r[...] += v.astype(r.dtype)

    return pl.pallas_call(
        body, name=name, grid=(n,), in_specs=in_specs, out_specs=out_specs, out_shape=out_shape,
        compiler_params=pltpu.CompilerParams(dimension_semantics=("arbitrary",), vmem_limit_bytes=VMEM_LIMIT),
    )(*[a for a, _ in ins])


def _s5_discretise(are, aim, ldt, bre, bim):
    dt = jnp.exp(ldt)
    er = jnp.exp(are * dt)
    abr = er * jnp.cos(aim * dt)
    abi = er * jnp.sin(aim * dt)
    den = are * are + aim * aim
    zr = ((abr - 1.0) * are + abi * aim) / den
    zi = (abi * are - (abr - 1.0) * aim) / den
    return abr, abi, zr * bre - zi * bim, zr * bim + zi * bre


def _s5_prep(are, aim, ldt, bre, bim):
    def body(a, b, c, d, e, o0, o1, o2, o3):
        r = _s5_discretise(a[...], b[...], c[...], d[...], e[...])
        o0[...], o1[...], o2[...], o3[...] = r

    sd = jax.ShapeDtypeStruct(are.shape, F32)
    return pl.pallas_call(body, name="s5_prep", out_shape=[sd] * 4)(are, aim, ldt, bre, bim)


def _s5_prep_bwd(are, aim, ldt, bre, bim, cts):
    def body(a, b, c, d, e, c0, c1, c2, c3, o0, o1, o2, o3, o4):
        _, vjp = jax.vjp(_s5_discretise, a[...], b[...], c[...], d[...], e[...])
        r = vjp((c0[...], c1[...], c2[...], c3[...]))
        o0[...], o1[...], o2[...], o3[...], o4[...] = r

    sd = jax.ShapeDtypeStruct(are.shape, F32)
    return pl.pallas_call(body, name="s5_prep_bwd", out_shape=[sd] * 5)(are, aim, ldt, bre, bim, *cts)


def _s5_consts(abr, abi, seg):
    shape = (SUBLANES, S5_CPLX)
    assert seg & (seg - 1) == 0 and seg % SUBLANES == 0, seg

    def body(ar_ref, ai_ref, f_ref, b_ref):
        def cmul(p, q):
            return (p[0] * q[0] - p[1] * q[1], p[0] * q[1] + p[1] * q[0])

        row = lax.broadcasted_iota(jnp.int32, shape, 0)
        a1 = (jnp.broadcast_to(ar_ref[...], shape), jnp.broadcast_to(ai_ref[...], shape))
        squares = [a1]
        while 1 << (len(squares) - 1) < 4 * seg:
            squares.append(cmul(squares[-1], squares[-1]))
        nb = seg.bit_length() - 1
        fwd, rev = [], []
        for k, a in ((1, squares[nb]), (2, squares[nb + 1]), (4, squares[nb + 2])):
            fwd += [jnp.where(row >= k, a[0], 0.0), jnp.where(row >= k, a[1], 0.0)]
            rev += [jnp.where(row <= 7 - k, a[0], 0.0), jnp.where(row <= 7 - k, -a[1], 0.0)]
        fwd += [a1[0], a1[1]]
        rev += [a1[0], -a1[1]]
        e = lax.broadcasted_iota(jnp.int32, (seg, S5_CPLX), 0) + 1
        wide = lambda v: jnp.broadcast_to(v[0:1, :], (seg, S5_CPLX))
        pr, pi = jnp.ones((seg, S5_CPLX), F32), jnp.zeros((seg, S5_CPLX), F32)
        for b in range(nb + 1):
            sr, si = wide(squares[b][0]), wide(squares[b][1])
            bit = ((e >> b) & 1) == 1
            pr, pi = jnp.where(bit, pr * sr - pi * si, pr), jnp.where(bit, pr * si + pi * sr, pi)
        f_ref[...] = jnp.concatenate(fwd + [pr, pi], axis=0)
        b_ref[...] = jnp.concatenate(rev + [pr, -pi], axis=0)

    sd = jax.ShapeDtypeStruct((8 * SUBLANES + 2 * seg, S5_CPLX), F32)
    return pl.pallas_call(body, name="s5_consts", out_shape=[sd, sd])(abr, abi)


S5_TILES = S5_LANES // LANES
S5_HALF_TILES = S5_TILES // 2


def _s5_tile_index(q):
    re = (q // 8) * S5_HALF_TILES + (q % 8)
    return re, re + S5_HALF_TILES // 2


def _lanes_of(ref, first, count):
    return jnp.concatenate([ref[j] for j in range(first, first + count)], axis=1)


def _to_lane_tiles(ref, first, value):
    for j in range(value.shape[1] // LANES):
        ref[first + j] = value[:, j * LANES:(j + 1) * LANES]


def _time_perm(tile, transpose=False):
    seg = tile // SUBLANES
    rho = lax.broadcasted_iota(jnp.int32, (tile, tile), 1 if transpose else 0)
    t = lax.broadcasted_iota(jnp.int32, (tile, tile), 0 if transpose else 1)
    return (t == (rho & (SUBLANES - 1)) * seg + (rho >> 3)).astype(BF16)


def _reorder(perm, x):
    out = None
    for _ in range(1 if x.dtype == BF16 else 3):
        piece = x.astype(BF16)
        part = jnp.dot(perm, piece, preferred_element_type=F32)
        out = part if out is None else out + part
        x = x - piece.astype(x.dtype)
    return out


def _s5_scan(s_ref, sc_ref, carry_ref, tile, reverse):
    seg = tile // SUBLANES
    group = 4
    edge = 0 if reverse else SUBLANES - 1
    row = lax.broadcasted_iota(jnp.int32, (SUBLANES, LANES), 0)
    order = range(seg - 1, -1, -1) if reverse else range(seg)
    rows_of = lambda k: pl.ds(k * SUBLANES, SUBLANES)
    base = 8 * SUBLANES

    for q0 in range(0, S5_CPLX // LANES, group):
        qs = list(range(q0, q0 + group))
        tiles = [_s5_tile_index(q) for q in qs]
        cst = lambda k, q: sc_ref[k * SUBLANES:(k + 1) * SUBLANES, q * LANES:(q + 1) * LANES]
        state = [(jnp.zeros((SUBLANES, LANES), F32), jnp.zeros((SUBLANES, LANES), F32)) for _ in qs]
        mult = [(cst(6, q), cst(7, q)) for q in qs]
        for k in order:
            for j, (re, im) in enumerate(tiles):
                ar, ai = mult[j]
                xr, xi = state[j]
                nr = ar * xr - ai * xi + s_ref[re, rows_of(k), :]
                ni = ar * xi + ai * xr + s_ref[im, rows_of(k), :]
                s_ref[re, rows_of(k), :] = nr
                s_ref[im, rows_of(k), :] = ni
                state[j] = (nr, ni)
        start = []
        for j, (q, (re, im)) in enumerate(zip(qs, tiles)):
            er, ei = state[j]
            shift1 = SUBLANES - 1 if reverse else 1
            dr = jnp.where(row == SUBLANES - 1 - edge, carry_ref[re], pltpu.roll(er, shift1, 0))
            di = jnp.where(row == SUBLANES - 1 - edge, carry_ref[im], pltpu.roll(ei, shift1, 0))
            for c, sh in ((0, 1), (2, 2), (4, 4)):
                shift = SUBLANES - sh if reverse else sh
                ar, ai = cst(c, q), cst(c + 1, q)
                sr, si = pltpu.roll(dr, shift, 0), pltpu.roll(di, shift, 0)
                dr, di = dr + ar * sr - ai * si, di + ar * si + ai * sr
            start.append((dr, di))
        for k in order:
            t = seg - 1 - k if reverse else k
            for j, (q, (re, im)) in enumerate(zip(qs, tiles)):
                lanes = slice(q * LANES, (q + 1) * LANES)
                pr = jnp.broadcast_to(sc_ref[base + t:base + t + 1, lanes], (SUBLANES, LANES))
                pi = jnp.broadcast_to(sc_ref[base + seg + t:base + seg + t + 1, lanes], (SUBLANES, LANES))
                cr, ci = start[j]
                xr = s_ref[re, rows_of(k), :] + pr * cr - pi * ci
                xi = s_ref[im, rows_of(k), :] + pr * ci + pi * cr
                s_ref[re, rows_of(k), :] = xr
                s_ref[im, rows_of(k), :] = xi
                if k == order[-1]:
                    carry_ref[re] = jnp.broadcast_to(xr[edge:edge + 1, :], (SUBLANES, LANES))
                    carry_ref[im] = jnp.broadcast_to(xi[edge:edge + 1, :], (SUBLANES, LANES))


def _s5_fwd(u, bd, cdt, dskip, sc, *, rows, tile):
    n = rows // tile

    def body(u_ref, bd_ref, cdt_ref, d_ref, sc_ref, y_ref, s_ref, carry_ref):
        @pl.when(pl.program_id(0) == 0)
        def _():
            carry_ref[...] = jnp.zeros_like(carry_ref)

        ub = _reorder(_time_perm(tile), u_ref[...].astype(BF16)).astype(BF16)
        for h in range(2):
            _to_lane_tiles(s_ref, h * S5_HALF_TILES, jnp.dot(ub[:, h * S5_HALF_IN:(h + 1) * S5_HALF_IN], bd_ref[h],
                                                             preferred_element_type=F32))
        _s5_scan(s_ref, sc_ref, carry_ref, tile, reverse=False)
        ys = [_nt(_lanes_of(s_ref, h * S5_HALF_TILES, S5_HALF_TILES), cdt_ref[h]) for h in range(2)]
        y_ref[...] = _reorder(_time_perm(tile, transpose=True), jnp.concatenate(ys, axis=1)) + d_ref[...] * u_ref[...]

    full = lambda a: pl.BlockSpec(a.shape, lambda i, nd=a.ndim: (0,) * nd)
    return pl.pallas_call(
        body, name="s5_fwd", grid=(n,),
        in_specs=[pl.BlockSpec((tile, S5_WIDTH), lambda i: (i, 0)), full(bd), full(cdt), full(dskip), full(sc)],
        out_specs=[pl.BlockSpec((tile, S5_WIDTH), lambda i: (i, 0)),
                   pl.BlockSpec((S5_TILES, tile, LANES), lambda i: (0, i, 0))],
        out_shape=[jax.ShapeDtypeStruct((rows, S5_WIDTH), F32), jax.ShapeDtypeStruct((S5_TILES, rows, LANES), F32)],
        scratch_shapes=[pltpu.VMEM((S5_TILES, SUBLANES, LANES), F32)],
        compiler_params=pltpu.CompilerParams(dimension_semantics=("arbitrary",), vmem_limit_bytes=VMEM_LIMIT),
    )(u, bd, cdt, dskip, sc)


def _s5_bwd(dy, s, u, bd, cdt, dskip, sc, *, rows, tile):
    n = rows // tile
    hc = 2 * S5_HALF_CPLX
    per8 = tile // SUBLANES
    quarter = S5_HALF_TILES // 2

    def body(dy_ref, s_ref, sp_ref, u_ref, bd_ref, cdt_ref, d_ref, sc_ref,
             du_ref, dbd_ref, dcdt_ref, dd_ref, da_ref, g_ref, carry_ref):
        i = pl.program_id(0)

        @pl.when(i == 0)
        def _():
            carry_ref[...] = jnp.zeros_like(carry_ref)
            dbd_ref[...] = jnp.zeros_like(dbd_ref)
            dcdt_ref[...] = jnp.zeros_like(dcdt_ref)
            dd_ref[...] = jnp.zeros_like(dd_ref)
            da_ref[...] = jnp.zeros_like(da_ref)

        dy = dy_ref[...]
        u = u_ref[...]
        perm = _time_perm(tile)
        dyb = _reorder(perm, dy.astype(BF16)).astype(BF16)
        ub = _reorder(perm, u.astype(BF16)).astype(BF16)
        for h in range(2):
            _to_lane_tiles(g_ref, h * S5_HALF_TILES, jnp.dot(dyb[:, h * S5_HALF_IN:(h + 1) * S5_HALF_IN], cdt_ref[h],
                                                             preferred_element_type=F32))
        _s5_scan(g_ref, sc_ref, carry_ref, tile, reverse=True)
        dus = []
        for h in range(2):
            gb = _lanes_of(g_ref, h * S5_HALF_TILES, S5_HALF_TILES).astype(BF16)
            sb = _lanes_of(s_ref, h * S5_HALF_TILES, S5_HALF_TILES).astype(BF16)
            dus.append(_nt(gb, bd_ref[h]))
            dbd_ref[h] += _tn(ub[:, h * S5_HALF_IN:(h + 1) * S5_HALF_IN], gb)
            dcdt_ref[h] += _tn(dyb[:, h * S5_HALF_IN:(h + 1) * S5_HALF_IN], sb)
        du = _reorder(_time_perm(tile, transpose=True), jnp.concatenate(dus, axis=1))
        du_ref[...] = (du + d_ref[...] * dy).astype(du_ref.dtype)
        dd_ref[...] += jnp.sum(dy * u, axis=0, keepdims=True)

        not_first = (i < n - 1).astype(F32)
        row = lax.broadcasted_iota(jnp.int32, (SUBLANES, quarter * LANES), 0)

        def step_before(first):
            cur = _lanes_of(s_ref, first, quarter)
            before_tile = _lanes_of(sp_ref, first, quarter)[SUBLANES - 1:SUBLANES, :] * not_first
            head = jnp.where(row == 0, before_tile, pltpu.roll(cur[tile - SUBLANES:], 1, 0))
            return jnp.concatenate([head, cur[:tile - SUBLANES]], axis=0)

        for h in range(2):
            re, im = h * S5_HALF_TILES, h * S5_HALF_TILES + quarter
            ssr = step_before(re)
            ssi = step_before(im)
            gr = _lanes_of(g_ref, re, quarter)
            gi = _lanes_of(g_ref, im, quarter)
            lanes = slice(h * S5_HALF_CPLX, (h + 1) * S5_HALF_CPLX)
            da_ref[0:1, lanes] += jnp.sum(ssr * gr + ssi * gi, axis=0, keepdims=True)
            da_ref[1:2, lanes] += jnp.sum(ssr * gi - ssi * gr, axis=0, keepdims=True)

    full = lambda a: pl.BlockSpec(a.shape, lambda i, nd=a.ndim: (0,) * nd)
    rev = lambda i: (n - 1 - i, 0)
    wshape = (2, S5_HALF_IN, hc)
    return pl.pallas_call(
        body, name="s5_bwd", grid=(n,),
        in_specs=[pl.BlockSpec((tile, S5_WIDTH), rev), pl.BlockSpec((S5_TILES, tile, LANES), lambda i: (0, n - 1 - i, 0)),
                  pl.BlockSpec((S5_TILES, SUBLANES, LANES), lambda i: (0, jnp.maximum((n - 1 - i) * per8 - 1, 0), 0)),
                  pl.BlockSpec((tile, S5_WIDTH), rev), full(bd), full(cdt), full(dskip), full(sc)],
        out_specs=[pl.BlockSpec((tile, S5_WIDTH), rev),
                   pl.BlockSpec(wshape, lambda i: (0, 0, 0)), pl.BlockSpec(wshape, lambda i: (0, 0, 0)),
                   pl.BlockSpec((1, S5_WIDTH), lambda i: (0, 0)), pl.BlockSpec((SUBLANES, S5_CPLX), lambda i: (0, 0))],
        out_shape=[jax.ShapeDtypeStruct((rows, S5_WIDTH), BF16), jax.ShapeDtypeStruct(wshape, F32),
                   jax.ShapeDtypeStruct(wshape, F32), jax.ShapeDtypeStruct((1, S5_WIDTH), F32),
                   jax.ShapeDtypeStruct((SUBLANES, S5_CPLX), F32)],
        scratch_shapes=[pltpu.VMEM((S5_TILES, tile, LANES), F32), pltpu.VMEM((S5_TILES, SUBLANES, LANES), F32)],
        compiler_params=pltpu.CompilerParams(dimension_semantics=("arbitrary",), vmem_limit_bytes=VMEM_LIMIT),
    )(dy, s, s, u, bd, cdt, dskip, sc)


def _s5_block_diag(parts):
    v = jnp.stack(parts, axis=2).reshape(2, 16, S5_GROUP, 2, S5_STATE)
    eye = jnp.eye(16, dtype=v.dtype)
    return jnp.einsum("hgcpn,gk->hgcpkn", v, eye).reshape(2, S5_HALF_IN, 2 * S5_HALF_CPLX)


def _s5_block_diag_extract(m):
    v = m.reshape(2, 16, S5_GROUP, 2, 16, S5_STATE)
    d = jnp.diagonal(v, axis1=1, axis2=4)
    d = jnp.transpose(d, (2, 0, 4, 1, 3)).reshape(2, S5_GROUPS, S5_GROUP, S5_STATE)
    return d[0], d[1]


def _cplx_to_lanes(v):
    return v.reshape(1, S5_CPLX)


def _lru_scan_fwd(a, b, *, rows, tile):
    n = rows // tile
    nblk = tile // SUBLANES
    group = 5

    def body(a_ref, b_ref, h_ref, carry_ref):
        @pl.when(pl.program_id(0) == 0)
        def _():
            carry_ref[...] = jnp.zeros_like(carry_ref)

        row = lax.broadcasted_iota(jnp.int32, (SUBLANES, LANES), 0)
        for q0 in range(0, LRU_WIDTH // LANES, group):
            offs = [q * LANES for q in range(q0, q0 + group)]

            def blk(t, carry, offs=offs):
                r0 = pl.multiple_of(t * SUBLANES, SUBLANES)
                new = []
                for j, o in enumerate(offs):
                    av = a_ref[pl.ds(r0, SUBLANES), o:o + LANES]
                    xv = b_ref[pl.ds(r0, SUBLANES), o:o + LANES]
                    for sh in (1, 2, 4):
                        m = row >= sh
                        xs = pltpu.roll(xv, sh, 0)
                        asft = pltpu.roll(av, sh, 0)
                        xv = xv + jnp.where(m, av * xs, 0.0)
                        av = jnp.where(m, av * asft, av)
                    hv = xv + av * carry[j]
                    h_ref[pl.ds(r0, SUBLANES), o:o + LANES] = hv
                    new.append(jnp.broadcast_to(hv[SUBLANES - 1:SUBLANES, :], (SUBLANES, LANES)))
                return tuple(new)

            carry = lax.fori_loop(0, nblk, blk, tuple(carry_ref[:, o:o + LANES] for o in offs), unroll=2)
            for j, o in enumerate(offs):
                carry_ref[:, o:o + LANES] = carry[j]

    spec = pl.BlockSpec((tile, LRU_WIDTH), lambda i: (i, 0))
    return pl.pallas_call(
        body, name="lru_scan_fwd", grid=(n,), in_specs=[spec, spec], out_specs=spec,
        out_shape=jax.ShapeDtypeStruct((rows, LRU_WIDTH), F32),
        scratch_shapes=[pltpu.VMEM((SUBLANES, LRU_WIDTH), F32)],
        compiler_params=pltpu.CompilerParams(dimension_semantics=("arbitrary",), vmem_limit_bytes=VMEM_LIMIT),
    )(a, b)


def _lru_scan_bwd(dh, a, *, rows, tile):
    n = rows // tile
    nblk = tile // SUBLANES
    group = 5

    def body(dh_ref, a_ref, g_ref, cg_ref, ca_ref):
        @pl.when(pl.program_id(0) == 0)
        def _():
            cg_ref[...] = jnp.zeros_like(cg_ref)
            ca_ref[...] = jnp.zeros_like(ca_ref)

        row = lax.broadcasted_iota(jnp.int32, (SUBLANES, LANES), 0)
        for q0 in range(0, LRU_WIDTH // LANES, group):
            offs = [q * LANES for q in range(q0, q0 + group)]

            def blk(t, carry, offs=offs):
                r0 = pl.multiple_of((nblk - 1 - t) * SUBLANES, SUBLANES)
                new = []
                for j, o in enumerate(offs):
                    cg, ca = carry[2 * j], carry[2 * j + 1]
                    araw = a_ref[pl.ds(r0, SUBLANES), o:o + LANES]
                    xv = dh_ref[pl.ds(r0, SUBLANES), o:o + LANES]
                    av = jnp.where(row == SUBLANES - 1, ca, pltpu.roll(araw, SUBLANES - 1, 0))
                    for sh in (1, 2, 4):
                        m = row <= SUBLANES - 1 - sh
                        xs = pltpu.roll(xv, SUBLANES - sh, 0)
                        asft = pltpu.roll(av, SUBLANES - sh, 0)
                        xv = xv + jnp.where(m, av * xs, 0.0)
                        av = jnp.where(m, av * asft, av)
                    gv = xv + av * cg
                    g_ref[pl.ds(r0, SUBLANES), o:o + LANES] = gv
                    new.append(jnp.broadcast_to(gv[0:1, :], (SUBLANES, LANES)))
                    new.append(jnp.broadcast_to(araw[0:1, :], (SUBLANES, LANES)))
                return tuple(new)

            carry0 = tuple(r[:, o:o + LANES] for o in offs for r in (cg_ref, ca_ref))
            carry = lax.fori_loop(0, nblk, blk, carry0, unroll=2)
            for j, o in enumerate(offs):
                cg_ref[:, o:o + LANES] = carry[2 * j]
                ca_ref[:, o:o + LANES] = carry[2 * j + 1]

    spec = pl.BlockSpec((tile, LRU_WIDTH), lambda i: (n - 1 - i, 0))
    return pl.pallas_call(
        body, name="lru_scan_bwd", grid=(n,), in_specs=[spec, spec], out_specs=spec,
        out_shape=jax.ShapeDtypeStruct((rows, LRU_WIDTH), F32),
        scratch_shapes=[pltpu.VMEM((SUBLANES, LRU_WIDTH), F32), pltpu.VMEM((SUBLANES, LRU_WIDTH), F32)],
        compiler_params=pltpu.CompilerParams(dimension_semantics=("arbitrary",), vmem_limit_bytes=VMEM_LIMIT),
    )(dh, a)


def _conv_fwd(i, x, prev, cw, cb):
    prev = prev * (i > 0).astype(F32)
    y = x * cw[3:4, :] + cb
    for s in range(1, CONV_WIDTH):
        y = y + _rows_before(x, prev, s) * cw[3 - s:4 - s, :]
    return y


def _lru_gates(c, wa, ba, wx, bx, lam):
    r = _sigmoid(_heads(_nn, c, wa) + ba)
    ig = _sigmoid(_heads(_nn, c, wx) + bx)
    z = -lam
    sp = jnp.maximum(z, 0.0) + jnp.log(1.0 + jnp.exp(-jnp.abs(z)))
    log_a = -LRU_C * r * sp
    a = jnp.exp(log_a)
    z2 = 2.0 * log_a
    series = -z2 * (1.0 + z2 * (0.5 + z2 * (1.0 / 6.0 + z2 * (1.0 / 24.0 + z2 * (1.0 / 120.0 + z2 / 720.0)))))
    one_minus = jnp.where(z2 > -0.2, series, 1.0 - jnp.exp(z2))
    mult = jnp.sqrt(one_minus)
    return r, ig, sp, a, mult


def _layer_fwd(x, p, w, rows):
    tile = WIDE_TILE
    d = D_MODEL

    def f_in(i, xb, g, *ws):
        rstd = lax.rsqrt(jnp.mean(xb * xb, axis=-1, keepdims=True) + NORM_EPS)
        hb = (xb * rstd * g).astype(BF16)
        proj = jnp.concatenate([jnp.dot(hb, wj, preferred_element_type=F32) for wj in ws], axis=1)
        return tuple(proj[:, IN_OFFSETS[k]:IN_OFFSETS[k + 1]] for k in range(6)) + (hb,)

    s5x, s5g, lrux, lrug, gs, gl, h = _rows(
        "f_in", f_in, [(x, "row"), (w["g_pre"], "full")] + w["w_in"],
        [((rows, wd), BF16, "row") for wd in IN_WIDTHS] + [((rows, d), BF16, "row")], rows=rows, tile=ROW_TILE)

    ys, st = _s5_fwd(s5x, w["bd"], w["cdt"], w["s5_d"], w["scf"], rows=rows, tile=ROW_TILE)

    def f_s5post(i, ysb, gb, wglu, wbs):
        glv, _ = _gelu_parts(ysb)
        glu = _nn(glv, wglu)
        y2 = glu[:, :S5_WIDTH] * _sigmoid(glu[:, S5_WIDTH:]) * (gb * _sigmoid(gb))
        return (_nn(y2, wbs),)

    (z_s,) = _rows("f_s5post", f_s5post, [(ys, "row"), (s5g, "row"), (w["w_glu"], "full"), (w["w_bs"], "full")],
                   [((rows, d), BF16, "row")], rows=rows, tile=tile)

    def f_gates(i, xb, prev, cw, cb, wa, ba, wx, bx, lam):
        c = _conv_fwd(i, xb, prev, cw, cb)
        _, ig, _, a, mult = _lru_gates(c, wa, ba, wx, bx, lam)
        return a, mult * (ig * c)

    a, b = _rows("f_gates", f_gates,
                 [(lrux, "row"), (lrux, "prev"), (w["conv_w"], "full"), (w["conv_b"], "full"), (w["lru_w_a"], "full"),
                  (w["lru_b_a"], "full"), (w["lru_w_x"], "full"), (w["lru_b_x"], "full"), (w["lru_lambda"], "full")],
                 [((rows, LRU_WIDTH), F32, "row")] * 2, rows=rows, tile=tile)
    hl = _lru_scan_fwd(a, b, rows=rows, tile=tile)

    def f_merge(i, hb, lg, zs, gsb, glb, xb, wbl, wout, gpost):
        z_l = _nn(hb * (lg * _sigmoid(lg)), wbl)
        merged = _sigmoid(gsb) * zs + _sigmoid(glb) * z_l
        mix = _nn(merged, wout)
        rstd = lax.rsqrt(jnp.mean(mix * mix, axis=-1, keepdims=True) + NORM_EPS)
        return xb + mix * rstd * gpost, mix, z_l

    x1, mix, z_l = _rows("f_merge", f_merge,
                         [(hl, "row"), (lrug, "row"), (z_s, "row"), (gs, "row"), (gl, "row"), (x, "row"),
                          (w["w_bl"], "full"), (w["w_out"], "full"), (w["g_post"], "full")],
                         [((rows, d), F32, "row"), ((rows, d), BF16, "row"), ((rows, d), BF16, "row")], rows=rows, tile=tile)

    def f_ple(i, x1b, pb, wple, wpg):
        return (x1b + _nn(pb, wple) * _sigmoid(_nn(x1b, wpg)),)

    (x2,) = _rows("f_ple", f_ple, [(x1, "row"), (p, "row"), (w["w_ple"], "full"), (w["w_ple_gate"], "full")],
                  [((rows, d), F32, "row")], rows=rows, tile=tile)
    saved = dict(x=x, h=h, s5x=s5x, s5g=s5g, lrux=lrux, lrug=lrug, gs=gs, gl=gl, ys=ys, st=st, a=a, hl=hl, z_s=z_s,
                 z_l=z_l, mix=mix, x1=x1, p=p)
    return x2, saved


def _layer_bwd(dx2, sv, w, rows):
    tile = WIDE_TILE
    d = D_MODEL
    g = {}

    def b_ple(i, dxb, x1b, pb, wple, wpg):
        pe = _nn(pb, wple)
        sg = _sigmoid(_nn(x1b, wpg))
        dpe = dxb * sg
        dgt = dxb * pe * sg * (1.0 - sg)
        return dxb + _nt(dgt, wpg), _tn(pb, dpe), _tn(x1b, dgt)

    dx1, g["w_ple"], g["w_ple_gate"] = _rows(
        "b_ple", b_ple, [(dx2, "row"), (sv["x1"], "row"), (sv["p"], "row"), (w["w_ple"], "full"), (w["w_ple_gate"], "full")],
        [((rows, d), F32, "row"), ((PLE_DIM, d), F32, "acc"), ((d, d), F32, "acc")], rows=rows, tile=tile)

    def b_merge(i, dxb, mixb, zs, zl, gsb, glb, wout, gpost):
        rstd = lax.rsqrt(jnp.mean(mixb * mixb, axis=-1, keepdims=True) + NORM_EPS)
        nrm = mixb * rstd
        dn = dxb * gpost
        dmix = rstd * (dn - nrm * jnp.mean(dn * nrm, axis=-1, keepdims=True))
        ss, sl = _sigmoid(gsb), _sigmoid(glb)
        merged = ss * zs + sl * zl
        dm = _nt(dmix, wout)
        return (dm * ss, dm * sl, dm * zs * ss * (1.0 - ss), dm * zl * sl * (1.0 - sl),
                _tn(merged, dmix), jnp.sum(dxb * nrm, axis=0, keepdims=True))

    dz_s, dz_l, dgs, dgl, g["w_out"], g["g_post"] = _rows(
        "b_merge", b_merge,
        [(dx1, "row"), (sv["mix"], "row"), (sv["z_s"], "row"), (sv["z_l"], "row"), (sv["gs"], "row"), (sv["gl"], "row"),
         (w["w_out"], "full"), (w["g_post"], "full")],
        [((rows, d), BF16, "row")] * 4 + [((d, d), F32, "acc"), ((1, d), F32, "acc")], rows=rows, tile=tile)

    def b_bl(i, dzl, hb, lg, wbl):
        sl = _sigmoid(lg)
        silu = lg * sl
        dy3 = _nt(dzl, wbl)
        return dy3 * silu, dy3 * hb * sl * (1.0 + lg * (1.0 - sl)), _tn(hb * silu, dzl)

    dh, dlrug, g["w_bl"] = _rows(
        "b_bl", b_bl, [(dz_l, "row"), (sv["hl"], "row"), (sv["lrug"], "row"), (w["w_bl"], "full")],
        [((rows, LRU_WIDTH), F32, "row"), ((rows, LRU_WIDTH), BF16, "row"), ((LRU_WIDTH, d), F32, "acc")], rows=rows, tile=tile)

    gh = _lru_scan_bwd(dh, sv["a"], rows=rows, tile=tile)

    def b_gates(i, ghb, hb, hprev, xb, xprev, cw, cb, wa, ba, wx, bx, lam):
        c = _conv_fwd(i, xb, xprev, cw, cb)
        r, ig, sp, a, mult = _lru_gates(c, wa, ba, wx, bx, lam)
        h_before = _rows_before(hb, hprev * (i > 0).astype(F32), 1)
        da = ghb * h_before
        dmult = ghb * ig * c
        dlog_a = da * a - dmult * a * a / mult
        dpre_r = dlog_a * (-LRU_C) * sp * r * (1.0 - r)
        dpre_i = ghb * mult * c * ig * (1.0 - ig)
        dc = ghb * mult * ig + _heads(_nt, dpre_r, wa) + _heads(_nt, dpre_i, wx)
        dlam = jnp.sum(dlog_a * LRU_C * r, axis=0, keepdims=True) * _sigmoid(-lam)
        return (dc, _heads_tn(c, dpre_r), _heads_tn(c, dpre_i), jnp.sum(dpre_r, axis=0, keepdims=True),
                jnp.sum(dpre_i, axis=0, keepdims=True), dlam)

    hshape = (LRU_HEADS, LRU_HEAD_DIM, LRU_HEAD_DIM)
    dc, g["lru_w_a"], g["lru_w_x"], g["lru_b_a"], g["lru_b_x"], g["lru_lambda"] = _rows(
        "b_gates", b_gates,
        [(gh, "row"), (sv["hl"], "row"), (sv["hl"], "prev"), (sv["lrux"], "row"), (sv["lrux"], "prev"),
         (w["conv_w"], "full"), (w["conv_b"], "full"), (w["lru_w_a"], "full"), (w["lru_b_a"], "full"),
         (w["lru_w_x"], "full"), (w["lru_b_x"], "full"), (w["lru_lambda"], "full")],
        [((rows, LRU_WIDTH), BF16, "row"), (hshape, F32, "acc"), (hshape, F32, "acc")] + [((1, LRU_WIDTH), F32, "acc")] * 3,
        rows=rows, tile=tile)

    n_tiles = rows // min(tile, rows)

    def b_conv(i, dcb, dnext, xb, xprev, cw):
        dnext = dnext * (i < n_tiles - 1).astype(F32)
        xprev = xprev * (i > 0).astype(F32)
        dx = dcb * cw[3:4, :]
        dws = [jnp.sum(dcb * xb, axis=0, keepdims=True)]
        for s in range(1, CONV_WIDTH):
            dx = dx + _rows_after(dcb, dnext, s) * cw[3 - s:4 - s, :]
            dws.append(jnp.sum(dcb * _rows_before(xb, xprev, s), axis=0, keepdims=True))
        return dx, jnp.concatenate(dws[::-1], axis=0), jnp.sum(dcb, axis=0, keepdims=True)

    dlrux, g["conv_w"], g["conv_b"] = _rows(
        "b_conv", b_conv, [(dc, "row"), (dc, "next"), (sv["lrux"], "row"), (sv["lrux"], "prev"), (w["conv_w"], "full")],
        [((rows, LRU_WIDTH), BF16, "row"), ((CONV_WIDTH, LRU_WIDTH), F32, "acc"), ((1, LRU_WIDTH), F32, "acc")],
        rows=rows, tile=tile)

    def b_s5post(i, dzs, ysb, gb, wglu, wbs):
        glv, dgelu = _gelu_parts(ysb)
        glu = _nn(glv, wglu)
        ga, gb2 = glu[:, :S5_WIDTH], glu[:, S5_WIDTH:]
        sb = _sigmoid(gb2)
        sg = _sigmoid(gb)
        silu = gb * sg
        y2 = ga * sb * silu
        dy2 = _nt(dzs, wbs)
        dglu = jnp.concatenate([dy2 * sb * silu, dy2 * ga * silu * sb * (1.0 - sb)], axis=1)
        dg = dy2 * ga * sb * sg * (1.0 + gb * (1.0 - sg))
        return _nt(dglu, wglu) * dgelu, dg, _tn(y2, dzs), _tn(glv, dglu)

    dys, ds5g, g["w_bs"], g["w_glu"] = _rows(
        "b_s5post", b_s5post, [(dz_s, "row"), (sv["ys"], "row"), (sv["s5g"], "row"), (w["w_glu"], "full"), (w["w_bs"], "full")],
        [((rows, S5_WIDTH), F32, "row"), ((rows, S5_WIDTH), BF16, "row"), ((S5_WIDTH, d), F32, "acc"),
         ((S5_WIDTH, 2 * S5_WIDTH), F32, "acc")],
        rows=rows, tile=tile)

    ds5x, g["bd"], g["cdt"], g["s5_d"], g["abar"] = _s5_bwd(dys, sv["st"], sv["s5x"], w["bd"], w["cdt"], w["s5_d"],
                                                            w["scb"], rows=rows, tile=ROW_TILE)

    dcomps = [ds5x, ds5g, dlrux, dlrug, dgs, dgl]

    def b_in(i, xb, dx1b, gpre, *rest):
        dproj, ws = jnp.concatenate(rest[:6], axis=1), rest[6:]
        dh = _nt(dproj[:, :IN_SLOT], ws[0])
        for j in range(1, 4):
            dh = dh + _nt(dproj[:, j * IN_SLOT:(j + 1) * IN_SLOT], ws[j])
        rstd = lax.rsqrt(jnp.mean(xb * xb, axis=-1, keepdims=True) + NORM_EPS)
        nrm = xb * rstd
        dn = dh * gpre
        dx = rstd * (dn - nrm * jnp.mean(dn * nrm, axis=-1, keepdims=True))
        return dx1b + dx, jnp.sum(dh * nrm, axis=0, keepdims=True)

    dx, g["g_pre"] = _rows(
        "b_in", b_in, [(sv["x"], "row"), (dx1, "row"), (w["g_pre"], "full")] + [(dcv, "raw") for dcv in dcomps]
        + w["w_in"],
        [((rows, d), F32, "row"), ((1, d), F32, "acc")], rows=rows, tile=ROW_TILE)

    g["w_in"] = []
    for j in range(4):
        lo, hi = j * IN_SLOT, (j + 1) * IN_SLOT
        ks = [k for k in range(6) if IN_OFFSETS[k] < hi and IN_OFFSETS[k + 1] > lo]
        first = IN_OFFSETS[ks[0]]

        def b_win(i, hb, *parts, lo=lo, hi=hi, first=first):
            return (_tn(hb, jnp.concatenate(parts, axis=1)[:, lo - first:hi - first]),)

        g["w_in"].append(_rows("b_win", b_win, [(sv["h"], "raw")] + [(dcomps[k], "raw") for k in ks],
                               [((d, IN_SLOT), F32, "acc")], rows=rows, tile=4 * ROW_TILE)[0])
    return dx, g


SMALL = ("g_pre", "s5_a_re", "s5_a_im", "s5_log_dt", "s5_b_re", "s5_b_im", "s5_c_re", "s5_c_im", "s5_d", "conv_b",
         "lru_w_a", "lru_b_a", "lru_w_x", "lru_b_x", "lru_lambda", "g_post")
BIG = ("w_in", "w_glu", "w_bs", "conv_w", "w_bl", "w_out", "w_ple", "w_ple_gate")
BIG_SHARD_AXIS = {"w_in": 1, "w_glu": 1, "w_bs": 1, "conv_w": 1, "w_bl": 0, "w_out": 0, "w_ple": 1, "w_ple_gate": 0}


def _bcast_groups(v):
    return jnp.broadcast_to(v[:, None, :], (S5_GROUPS, S5_GROUP, S5_STATE)).reshape(S5_WIDTH, S5_STATE)


def _s5_prep_inputs(wl):
    ldt = jnp.broadcast_to(wl["s5_log_dt"][:, None], (S5_GROUPS, S5_STATE))
    gcn = lambda b: jnp.transpose(b, (0, 2, 1)).reshape(S5_WIDTH, S5_STATE)
    return (_bcast_groups(wl["s5_a_re"]), _bcast_groups(wl["s5_a_im"]), _bcast_groups(ldt), gcn(wl["s5_b_re"]),
            gcn(wl["s5_b_im"]))


def _layer_weights(wl):
    w = {}
    w["w_in"] = [wc if isinstance(wc, tuple) else (wc, "full") for wc in wl["w_in"]]
    for k in ("w_glu", "w_bs", "w_bl", "w_out", "w_ple", "w_ple_gate"):
        w[k] = wl[k]
    w["conv_w"] = wl["conv_w"]
    for k in ("g_pre", "g_post", "s5_d", "conv_b", "lru_b_a", "lru_b_x", "lru_lambda"):
        w[k] = wl[k].reshape(1, -1)
    w["lru_w_a"] = wl["lru_w_a"].astype(BF16)
    w["lru_w_x"] = wl["lru_w_x"].astype(BF16)
    prep_in = _s5_prep_inputs(wl)
    abr, abi, bbr, bbi = _s5_prep(*prep_in)
    w["prep_in"] = prep_in
    shape3 = (S5_GROUPS, S5_GROUP, S5_STATE)
    w["bd"] = _s5_block_diag([bbr.reshape(shape3), bbi.reshape(shape3)]).astype(BF16)
    w["cdt"] = _s5_block_diag([wl["s5_c_re"], -wl["s5_c_im"]]).astype(BF16)
    abr_s = abr.reshape(shape3)[:, 0, :]
    abi_s = abi.reshape(shape3)[:, 0, :]
    w["scf"], w["scb"] = _s5_consts(_cplx_to_lanes(abr_s), _cplx_to_lanes(abi_s), ROW_TILE // SUBLANES)
    return w


def _layer_param_grads(g, w, wl):
    out = {}
    shape3 = (S5_GROUPS, S5_GROUP, S5_STATE)
    dbr, dbi = _s5_block_diag_extract(g["bd"])
    dcr, dci = _s5_block_diag_extract(g["cdt"])
    out["s5_c_re"], out["s5_c_im"] = dcr, -dci
    zeros = jnp.zeros(shape3, F32)
    dar = zeros.at[:, 0, :].set(g["abar"][0].reshape(S5_GROUPS, S5_STATE)).reshape(S5_WIDTH, S5_STATE)
    dai = zeros.at[:, 0, :].set(g["abar"][1].reshape(S5_GROUPS, S5_STATE)).reshape(S5_WIDTH, S5_STATE)
    cts = (dar, dai, dbr.reshape(S5_WIDTH, S5_STATE), dbi.reshape(S5_WIDTH, S5_STATE))
    d_are, d_aim, d_ldt, d_bre, d_bim = _s5_prep_bwd(*w["prep_in"], cts)
    out["s5_a_re"] = d_are.reshape(shape3).sum(axis=1)
    out["s5_a_im"] = d_aim.reshape(shape3).sum(axis=1)
    out["s5_log_dt"] = d_ldt.reshape(shape3).sum(axis=(1, 2))
    out["s5_b_re"] = jnp.transpose(d_bre.reshape(shape3), (0, 2, 1))
    out["s5_b_im"] = jnp.transpose(d_bim.reshape(shape3), (0, 2, 1))
    out["s5_d"] = g["s5_d"].reshape(-1)
    for k in ("g_pre", "g_post", "conv_b", "lru_b_a", "lru_b_x", "lru_lambda"):
        out[k] = g[k].reshape(-1)
    for k in ("lru_w_a", "lru_w_x", "conv_w", "w_in", "w_glu", "w_bs", "w_bl", "w_out", "w_ple", "w_ple_gate"):
        out[k] = g[k]
    return out


def _local_step(x, p, layers, target):
    rows = x.shape[0]
    ws = [_layer_weights(wl) for wl in layers]
    saved = []
    for i in range(DEPTH):
        x, sv = _layer_fwd(x, p[i], ws[i], rows)
        saved.append(sv)

    def f_loss(i, yb, tb):
        e = yb - tb
        return e * (1.0 / D_MODEL), jnp.sum(jnp.sum(e * e, axis=0, keepdims=True), axis=1, keepdims=True)

    dx, sq = _rows("f_loss", f_loss, [(x, "row"), (target, "row")],
                   [((rows, D_MODEL), F32, "row"), ((1, 1), F32, "acc")], rows=rows, tile=WIDE_TILE)
    loss = sq[0, 0] * (0.5 / D_MODEL)
    grads = [None] * DEPTH
    for i in reversed(range(DEPTH)):
        dx, g = _layer_bwd(dx, saved[i], ws[i], rows)
        grads[i] = _layer_param_grads(g, ws[i], layers[i])
    return loss, dx, grads


def _place():
    return lax.axis_index("x"), lax.axis_index("y"), lax.axis_index("c")


def _other_chips(x, y):
    return [(1 - x, y), (x, 1 - y), (1 - x, 1 - y)]


def _any_spec():
    return pl.BlockSpec(memory_space=pl.ANY)


ICI_PIECES = 1
D2D_PIECES = 1
D2D_SOLO_PIECES = 1


def _pieces(rows, k):
    step = rows // k
    assert step * k == rows and step % 16 == 0, (rows, k)
    return [(q * step, step) for q in range(k)]


def _gather_chips(name, vs, via_sibling):
    n = len(vs)
    halved = [via_sibling and v.shape[0] % (16 * 4 // v.dtype.itemsize) == 0 for v in vs]

    def body(*refs):
        v_refs, out_refs, send_sems, recv_sems = refs[:n], refs[n:2 * n], refs[2 * n], refs[2 * n + 1]
        x, y, c = _place()
        me = 2 * x + y
        chips = _other_chips(x, y)
        slots = [2 * cx + cy for cx, cy in chips]

        def part(a, slot, hc):
            if not halved[a]:
                return out_refs[a].at[slot]
            half = vs[a].shape[0] // 2
            return out_refs[a].at[slot, pl.ds(hc * half, half), :]

        def own(a):
            if not halved[a]:
                return v_refs[a]
            half = vs[a].shape[0] // 2
            return v_refs[a].at[pl.ds(c * half, half), :]

        def copy(a, k, src, dst, to):
            return pltpu.make_async_remote_copy(src_ref=src, dst_ref=dst, send_sem=send_sems.at[6 * a + k],
                                                recv_sem=recv_sems.at[6 * a + k], device_id=to, device_id_type=MESH)

        for a in range(n):
            for k in range(3):
                copy(a, k, own(a), part(a, me, c), (*chips[k], c)).start()
        for a in range(n):
            for k in range(3):
                copy(a, k, own(a), part(a, slots[k], c), (*chips[k], c)).wait_recv()
                if halved[a]:
                    copy(a, 3 + k, part(a, slots[k], c), part(a, slots[k], c), (x, y, 1 - c)).start()
        for a in range(n):
            for k in range(3):
                if halved[a]:
                    copy(a, 3 + k, own(a), part(a, slots[k], 1 - c), (x, y, 1 - c)).wait_recv()
                    copy(a, 3 + k, own(a), part(a, me, c), (x, y, 1 - c)).wait_send()
                copy(a, k, own(a), part(a, me, c), (x, y, 1 - c)).wait_send()

    others = pl.pallas_call(
        body, name=name, out_shape=[jax.ShapeDtypeStruct((4,) + v.shape, v.dtype) for v in vs],
        in_specs=[_any_spec()] * n, out_specs=[_any_spec()] * n,
        scratch_shapes=[pltpu.SemaphoreType.DMA((6 * n,)), pltpu.SemaphoreType.DMA((6 * n,))],
    )(*vs)
    me = 2 * lax.axis_index("x") + lax.axis_index("y")
    return [lax.dynamic_update_slice(o, v[None], (me, 0, 0)) for o, v in zip(others, vs)]


def _rs_sibling(grs):
    n = len(grs)
    halves = [g.shape[1] // 2 for g in grs]

    def body(*refs):
        g_refs, got_refs, send_sems, recv_sems = refs[:n], refs[n:2 * n], refs[2 * n], refs[2 * n + 1]
        x, y, c = _place()
        copies = [pltpu.make_async_remote_copy(
            src_ref=g_refs[a].at[:, pl.ds((1 - c) * halves[a], halves[a]), :], dst_ref=got_refs[a],
            send_sem=send_sems.at[a], recv_sem=recv_sems.at[a], device_id=(x, y, 1 - c), device_id_type=MESH)
            for a in range(n)]
        for cp in copies:
            cp.start()
        for cp in copies:
            cp.wait()

    return pl.pallas_call(
        body, name="rs_sibling",
        out_shape=[jax.ShapeDtypeStruct((4, h, g.shape[2]), F32) for g, h in zip(grs, halves)],
        in_specs=[_any_spec()] * n, out_specs=[_any_spec()] * n,
        scratch_shapes=[pltpu.SemaphoreType.DMA((n,)), pltpu.SemaphoreType.DMA((n,))],
    )(*grs)


def _rs_chips(a16s):
    n = len(a16s)

    def body(*refs):
        a_refs, got_refs, send_sems, recv_sems = refs[:n], refs[n:2 * n], refs[2 * n], refs[2 * n + 1]
        x, y, c = _place()
        chips = _other_chips(x, y)
        copies = [pltpu.make_async_remote_copy(
            src_ref=a_refs[a].at[2 * cx + cy], dst_ref=got_refs[a].at[k], send_sem=send_sems.at[3 * a + k],
            recv_sem=recv_sems.at[3 * a + k], device_id=(cx, cy, c), device_id_type=MESH)
            for a in range(n) for k, (cx, cy) in enumerate(chips)]
        for cp in copies:
            cp.start()
        for cp in copies:
            cp.wait()

    return pl.pallas_call(
        body, name="rs_chips", out_shape=[jax.ShapeDtypeStruct((3,) + a.shape[1:], a.dtype) for a in a16s],
        in_specs=[_any_spec()] * n, out_specs=[_any_spec()] * n,
        scratch_shapes=[pltpu.SemaphoreType.DMA((3 * n,)), pltpu.SemaphoreType.DMA((3 * n,))],
    )(*a16s)


def _swap_halves(vs):
    n = len(vs)

    def body(*refs):
        v_refs, out_refs, send_sems, recv_sems = refs[:n], refs[n:2 * n], refs[2 * n], refs[2 * n + 1]
        x, y, c = _place()

        def give(a, hc):
            return pltpu.make_async_remote_copy(src_ref=v_refs[a], dst_ref=out_refs[a].at[hc], send_sem=send_sems.at[a],
                                                recv_sem=recv_sems.at[a], device_id=(x, y, 1 - c), device_id_type=MESH)

        for a in range(n):
            give(a, c).start()
        for a in range(n):
            give(a, c).wait_send()
            give(a, 1 - c).wait_recv()

    others = pl.pallas_call(
        body, name="swap_halves", out_shape=[jax.ShapeDtypeStruct((2,) + v.shape, v.dtype) for v in vs],
        in_specs=[_any_spec()] * n, out_specs=[_any_spec()] * n,
        scratch_shapes=[pltpu.SemaphoreType.DMA((n,)), pltpu.SemaphoreType.DMA((n,))],
    )(*vs)
    c = lax.axis_index("c")
    return [lax.dynamic_update_slice(o, v[None], (c, 0, 0)).reshape(2 * v.shape[0], v.shape[1]) for o, v in zip(others, vs)]


WIDE = 1024
PACK_TILE = 3072
GRAD_ROWS_UNIT = 2 * PACK_TILE


def _pack(parts, rows_unit, dtype):
    flat = jnp.concatenate([q.reshape(-1).astype(dtype) for q in parts])
    unit = rows_unit * LANES
    total = -(-flat.shape[0] // unit) * unit
    return jnp.pad(flat, (0, total - flat.shape[0])).reshape(-1, LANES)


def _unpack(flat, shapes, align=1):
    out, off = [], 0
    for s in shapes:
        n = 1
        for q in s:
            n *= q
        out.append(flat[off:off + n].reshape(s))
        off += -(-n // align) * align
    return out


def _pack_tile(rows, width):
    most = (2 ** 21) // (4 * width)
    if rows <= most:
        return rows
    return max(t for t in range(16, most + 1, 16) if rows % t == 0)


def _flat_aligned(v, align):
    v = v.reshape(-1)
    return jnp.pad(v, (0, -v.shape[0] % align))


def _to_slots(name, full):
    dp, r, c = full.shape
    if BIG_SHARD_AXIS[name] == 1:
        return jnp.transpose(full.reshape(dp, r, 4, c // 4), (2, 0, 1, 3)).reshape(4, -1)
    return jnp.transpose(full.reshape(dp, 4, r // 4, c), (1, 0, 2, 3)).reshape(4, -1)


def _from_slots(name, slots, shard_shape):
    dp, r, c = shard_shape
    v = slots.reshape(4, dp, r, c)
    if BIG_SHARD_AXIS[name] == 1:
        return jnp.transpose(v, (1, 2, 0, 3)).reshape(dp, r, 4 * c)
    return jnp.transpose(v, (1, 0, 2, 3)).reshape(dp, 4 * r, c)


def _adamw(name, w, g, m, v, tile):
    def fn(i, wb, gb, mb, vb):
        m2 = ADAM_B1 * mb + (1.0 - ADAM_B1) * gb
        v2 = ADAM_B2 * vb + (1.0 - ADAM_B2) * (gb * gb)
        m_hat = m2 / (1.0 - ADAM_B1 ** ADAM_STEP)
        v_hat = v2 / (1.0 - ADAM_B2 ** ADAM_STEP)
        return -ADAM_LR * (m_hat / (jnp.sqrt(v_hat) + ADAM_EPS) + ADAM_WD * wb), m2, v2

    return _rows(name, fn, [(w, "row"), (g, "row"), (m, "row"), (v, "row")], [(w.shape, F32, "row")] * 3,
                 rows=w.shape[0], tile=tile)


def _as_2d(a):
    return a.reshape(-1, a.shape[-1])


def _adam_tile(rows):
    for t in (256, 184, 128, 64, 32, 16, 8):
        if rows % t == 0:
            return t
    return rows


def kernel(x, p, g_pre, w_in, s5_a_re, s5_a_im, s5_log_dt, s5_b_re, s5_b_im, s5_c_re, s5_c_im, s5_d, w_glu, w_bs, conv_w, conv_b, lru_w_a, lru_b_a, lru_w_x, lru_b_x, lru_lambda, w_bl, w_out, g_post, w_ple, w_ple_gate, loss_target, m_g_pre, m_w_in, m_s5_a_re, m_s5_a_im, m_s5_log_dt, m_s5_b_re, m_s5_b_im, m_s5_c_re, m_s5_c_im, m_s5_d, m_w_glu, m_w_bs, m_conv_w, m_conv_b, m_lru_w_a, m_lru_b_a, m_lru_w_x, m_lru_b_x, m_lru_lambda, m_w_bl, m_w_out, m_g_post, m_w_ple, m_w_ple_gate, v_g_pre, v_w_in, v_s5_a_re, v_s5_a_im, v_s5_log_dt, v_s5_b_re, v_s5_b_im, v_s5_c_re, v_s5_c_im, v_s5_d, v_w_glu, v_w_bs, v_conv_w, v_conv_b, v_lru_w_a, v_lru_b_a, v_lru_w_x, v_lru_b_x, v_lru_lambda, v_w_bl, v_w_out, v_g_post, v_w_ple, v_w_ple_gate):
    wts = dict(g_pre=g_pre, w_in=w_in, s5_a_re=s5_a_re, s5_a_im=s5_a_im, s5_log_dt=s5_log_dt, s5_b_re=s5_b_re,
               s5_b_im=s5_b_im, s5_c_re=s5_c_re, s5_c_im=s5_c_im, s5_d=s5_d, w_glu=w_glu, w_bs=w_bs, conv_w=conv_w,
               conv_b=conv_b, lru_w_a=lru_w_a, lru_b_a=lru_b_a, lru_w_x=lru_w_x, lru_b_x=lru_b_x, lru_lambda=lru_lambda,
               w_bl=w_bl, w_out=w_out, g_post=g_post, w_ple=w_ple, w_ple_gate=w_ple_gate)
    mom1 = dict(g_pre=m_g_pre, w_in=m_w_in, s5_a_re=m_s5_a_re, s5_a_im=m_s5_a_im, s5_log_dt=m_s5_log_dt, s5_b_re=m_s5_b_re,
                s5_b_im=m_s5_b_im, s5_c_re=m_s5_c_re, s5_c_im=m_s5_c_im, s5_d=m_s5_d, w_glu=m_w_glu, w_bs=m_w_bs,
                conv_w=m_conv_w, conv_b=m_conv_b, lru_w_a=m_lru_w_a, lru_b_a=m_lru_b_a, lru_w_x=m_lru_w_x, lru_b_x=m_lru_b_x,
                lru_lambda=m_lru_lambda, w_bl=m_w_bl, w_out=m_w_out, g_post=m_g_post, w_ple=m_w_ple, w_ple_gate=m_w_ple_gate)
    mom2 = dict(g_pre=v_g_pre, w_in=v_w_in, s5_a_re=v_s5_a_re, s5_a_im=v_s5_a_im, s5_log_dt=v_s5_log_dt, s5_b_re=v_s5_b_re,
                s5_b_im=v_s5_b_im, s5_c_re=v_s5_c_re, s5_c_im=v_s5_c_im, s5_d=v_s5_d, w_glu=v_w_glu, w_bs=v_w_bs,
                conv_w=v_conv_w, conv_b=v_conv_b, lru_w_a=v_lru_w_a, lru_b_a=v_lru_b_a, lru_w_x=v_lru_w_x, lru_b_x=v_lru_b_x,
                lru_lambda=v_lru_lambda, w_bl=v_w_bl, w_out=v_w_out, g_post=v_g_post, w_ple=v_w_ple, w_ple_gate=v_w_ple_gate)
    names = list(wts)

    by_rows, by_cols = ("w_bl", "w_out", "w_ple_gate"), ("w_glu", "w_bs", "w_ple")
    two_d = lambda k: wts[k].astype(BF16).reshape(-1, wts[k].shape[2])
    sent = [two_d("w_in"), jnp.concatenate([two_d(k) for k in by_rows]), jnp.concatenate([two_d(k) for k in by_cols]),
            wts["conv_w"].reshape(-1, wts["conv_w"].shape[2])]
    g_in, g_rows, g_cols, g_conv = _gather_chips("gather_weights", sent, via_sibling=True)
    whole = {"conv_w": jnp.transpose(g_conv.reshape((4,) + wts["conv_w"].shape), (1, 2, 0, 3)).reshape(DEPTH, CONV_WIDTH, -1)}
    off = 0
    for k in by_rows:
        dp, r, _ = wts[k].shape
        piece = g_rows[:, off:off + dp * r].reshape(4, dp, r, -1)
        whole[k] = jnp.transpose(piece, (1, 0, 2, 3)).reshape(dp, 4 * r, -1)
        off += dp * r
    off = 0
    for k in by_cols:
        dp, r, cs = wts[k].shape
        piece = g_cols[:, off:off + dp * r].reshape(4, dp, r, cs)
        whole[k] = jnp.transpose(piece, (1, 2, 0, 3)).reshape(dp, r, 4 * cs)
        off += dp * r
    layers = []
    for i in range(DEPTH):
        wl = {k: whole[k][i] for k in BIG if k != "w_in"}
        wl["w_in"] = [(g_in, ("part", j, i, wts["w_in"].shape[1])) for j in range(4)]
        wl.update({k: wts[k][i] for k in SMALL})
        layers.append(wl)

    loss, grad_x, grads = _local_step(x[0], p[:, 0], layers, loss_target[0])
    loss = lax.psum(loss, ("x", "y", "c"))

    me = 2 * lax.axis_index("x") + lax.axis_index("y")
    c = lax.axis_index("c")
    by_rows, by_cols = ("w_bl", "w_out", "w_ple_gate"), ("w_glu", "w_bs", "w_ple")
    rows_of = lambda k, i, j: grads[i][k][j * (grads[i][k].shape[0] // 4):(j + 1) * (grads[i][k].shape[0] // 4)]
    cols_of = lambda k, i, j: grads[i][k][:, j * (grads[i][k].shape[1] // 4):(j + 1) * (grads[i][k].shape[1] // 4)]
    layer_range = range(DEPTH)
    packs = [
        jnp.stack([jnp.concatenate([grads[i]["w_in"][j] for i in layer_range]) for j in range(4)]),
        jnp.stack([jnp.concatenate([rows_of(k, i, j) for k in by_rows for i in layer_range]) for j in range(4)]),
        jnp.stack([jnp.concatenate([cols_of(k, i, j) for k in by_cols for i in layer_range]) for j in range(4)]),
    ]
    small_names = SMALL + ("conv_w",)
    small_shapes = [(DEPTH,) + grads[0][k].shape for k in small_names]
    small_flat = jnp.concatenate([_flat_aligned(jnp.stack([grads[i][k] for i in layer_range]), WIDE) for k in small_names])
    n_small = small_flat.shape[0]
    small_q = -(-n_small // (4 * 32 * LANES)) * 32 * LANES
    packs.append(jnp.pad(small_flat, (0, 4 * small_q - n_small)).reshape(4, small_q // LANES, LANES))

    gots = _rs_sibling(packs)
    a32s, a16s = [], []
    for pk, got in zip(packs, gots):
        half, width = got.shape[1], got.shape[2]
        mine = lax.dynamic_slice_in_dim(pk, c * half, half, axis=1)

        def f_add1(i, a, b):
            s = a + b
            return s, s

        a32, a16 = _rows("rs_add1", f_add1, [(mine.reshape(4 * half, width), "row"), (got.reshape(4 * half, width), "row")],
                         [((4 * half, width), F32, "row"), ((4 * half, width), BF16, "row")], rows=4 * half,
                         tile=_pack_tile(4 * half, width))
        a32s.append(a32.reshape(4, half, width))
        a16s.append(a16.reshape(4, half, width))
    got3s = _rs_chips(a16s)
    red_halves = []
    for a32, got3 in zip(a32s, got3s):
        half, width = a32.shape[1], a32.shape[2]
        own = lax.dynamic_index_in_dim(a32, me, 0, keepdims=False)

        def f_add2(i, o, g0, g1, g2):
            return (((o + g0) + g1) + g2,)

        red_halves.append(_rows("rs_add2", f_add2, [(own, "row")] + [(got3[k], "row") for k in range(3)],
                                [((half, width), F32, "row")], rows=half, tile=_pack_tile(half, width))[0])
    reds = _swap_halves(red_halves)
    small_red = _gather_chips("gather_small", [reds[3]], via_sibling=False)[0].reshape(-1)[:n_small]

    grad_out = {"w_in": reds[0].reshape(wts["w_in"].shape)}
    for red, ks in ((reds[1], by_rows), (reds[2], by_cols)):
        off = 0
        for k in ks:
            n = wts[k].shape[0] * wts[k].shape[1]
            grad_out[k] = red[off:off + n].reshape(wts[k].shape)
            off += n
    small_out = dict(zip(small_names, _unpack(small_red, small_shapes, align=WIDE)))
    grad_out.update({k: small_out[k] for k in SMALL})
    grad_out["conv_w"] = lax.dynamic_slice_in_dim(small_out["conv_w"], me * wts["conv_w"].shape[2], wts["conv_w"].shape[2], axis=2)
    delta, new_m, new_v = {}, {}, {}
    for k in BIG + SMALL:
        w2 = _as_2d(wts[k])
        res = _adamw("adamw_" + k, w2, _as_2d(grad_out[k]), _as_2d(mom1[k]), _as_2d(mom2[k]), _adam_tile(w2.shape[0]))
        delta[k], new_m[k], new_v[k] = [r.reshape(wts[k].shape) for r in res]
    return (loss, grad_x[None], *[grad_out[k] for k in names], *[delta[k] for k in names],
            *[new_m[k] for k in names], *[new_v[k] for k in names])
```

```python
import jax
import jax.numpy as jnp
from jax import lax
from jax.experimental import pallas as pl
from jax.experimental.pallas import tpu as pltpu

F32 = jnp.float32
BF16 = jnp.bfloat16
MESH = pl.DeviceIdType.MESH

DEPTH = 2
D_MODEL = 1024
NORM_EPS = 1e-6
S5_WIDTH = 512
S5_GROUPS = 32
S5_GROUP = 16
S5_STATE = 64
LRU_WIDTH = 1280
LRU_HEADS = 10
LRU_HEAD_DIM = 128
LRU_C = 8.0
CONV_WIDTH = 4
PLE_DIM = 256
IN_WIDTHS = (S5_WIDTH, S5_WIDTH, LRU_WIDTH, LRU_WIDTH, D_MODEL, D_MODEL)
IN_OFFSETS = (0, 512, 1024, 2304, 3584, 4608, 5632)
IN_SLOT = 5632 // 4
ADAM_LR = 0.001
ADAM_B1 = 0.9
ADAM_B2 = 0.999
ADAM_EPS = 1e-08
ADAM_WD = 0.01
ADAM_STEP = 10

SUBLANES = 8
LANES = 128
S5_HALF_IN = S5_WIDTH // 2
S5_CPLX = S5_GROUPS * S5_STATE
S5_HALF_CPLX = S5_CPLX // 2
S5_LANES = 2 * S5_CPLX
VMEM_LIMIT = 48 * 2 ** 20
ROW_TILE = 256
WIDE_TILE = 512


def _sigmoid(x):
    return 0.5 * jnp.tanh(0.5 * x) + 0.5


def _gelu_parts(x):
    k = 0.7978845608028654
    t = jnp.tanh(k * (x + 0.044715 * x * x * x))
    val = 0.5 * x * (1.0 + t)
    grad = 0.5 * (1.0 + t) + 0.5 * x * (1.0 - t * t) * k * (1.0 + 3.0 * 0.044715 * x * x)
    return val, grad


def _nn(a, w):
    return jnp.dot(a.astype(BF16), w.astype(BF16), preferred_element_type=F32)


def _nt(a, w):
    return lax.dot_general(a.astype(BF16), w.astype(BF16), (((1,), (1,)), ((), ())), preferred_element_type=F32)


def _tn(a, b):
    return lax.dot_general(a.astype(BF16), b.astype(BF16), (((0,), (0,)), ((), ())), preferred_element_type=F32)


def _heads(op, a, w):
    d = LRU_HEAD_DIM
    return jnp.concatenate([op(a[:, h * d:(h + 1) * d], w[h]) for h in range(LRU_HEADS)], axis=1)


def _heads_tn(a, b):
    d = LRU_HEAD_DIM
    return jnp.stack([_tn(a[:, h * d:(h + 1) * d], b[:, h * d:(h + 1) * d]) for h in range(LRU_HEADS)], axis=0)


def _rows_before(x, halo, s):
    main = pltpu.roll(x, s, 0)
    head = pltpu.roll(jnp.concatenate([halo, x[0:SUBLANES]], axis=0), s, 0)[SUBLANES:2 * SUBLANES]
    return jnp.concatenate([head, main[SUBLANES:]], axis=0)


def _rows_after(x, halo, s):
    n = x.shape[0]
    main = pltpu.roll(x, n - s, 0)
    tail = pltpu.roll(jnp.concatenate([x[n - SUBLANES:], halo], axis=0), 2 * SUBLANES - s, 0)[0:SUBLANES]
    return jnp.concatenate([main[:n - SUBLANES], tail], axis=0)


def _rows(name, fn, ins, outs, *, rows, tile):
    tile = min(tile, rows)
    n = rows // tile
    assert n * tile == rows, (name, rows, tile)
    in_specs = []
    for arr, kind in ins:
        halo = SUBLANES * (4 // arr.dtype.itemsize)
        per, last = tile // halo, rows // halo - 1
        if isinstance(kind, tuple):
            _, j, k, r = kind
            in_specs.append(pl.BlockSpec((None, r, arr.shape[2]), lambda i, j=j, k=k: (j, k, 0)))
        elif kind in ("row", "raw"):
            in_specs.append(pl.BlockSpec((tile, arr.shape[1]), lambda i: (i, 0)))
        elif kind == "prev":
            in_specs.append(pl.BlockSpec((halo, arr.shape[1]), lambda i, per=per: (jnp.maximum(i * per - 1, 0), 0)))
        elif kind == "next":
            in_specs.append(pl.BlockSpec((halo, arr.shape[1]),
                                         lambda i, per=per, last=last: (jnp.minimum((i + 1) * per, last), 0)))
        else:
            in_specs.append(pl.BlockSpec(arr.shape, lambda i, nd=arr.ndim: (0,) * nd))
    out_shape, out_specs = [], []
    for shape, dtype, kind in outs:
        out_shape.append(jax.ShapeDtypeStruct(shape, dtype))
        if kind == "row":
            out_specs.append(pl.BlockSpec((tile, shape[1]), lambda i: (i, 0)))
        else:
            out_specs.append(pl.BlockSpec(shape, lambda i, nd=len(shape): (0,) * nd))
    n_in = len(ins)

    def load(ref, kind):
        v = ref[...]
        if kind in ("row", "prev", "next"):
            v = v.astype(F32)
        if kind == "prev":
            v = v[v.shape[0] - SUBLANES:]
        if kind == "next":
            v = v[:SUBLANES]
        return v

    def body(*refs):
        i = pl.program_id(0)
        vals = fn(i, *[load(r, kind) for r, (_, kind) in zip(refs[:n_in], ins)])
        assert len(vals) == len(outs), name
        for r, v, (_, _, kind) in zip(refs[n_in:], vals, outs):
            if kind == "row":
                r[...] = v.astype(r.dtype)
            else:
                @pl.when(i == 0)
                def _():
                    r[...] = jnp.zeros_like(r)

                r[...] += v.astype(r.dtype)

    return pl.pallas_call(
        body, name=name, grid=(n,), in_specs=in_specs, out_specs=out_specs, out_shape=out_shape,
        compiler_params=pltpu.CompilerParams(dimension_semantics=("arbitrary",), vmem_limit_bytes=VMEM_LIMIT),
    )(*[a for a, _ in ins])


def _s5_discretise(are, aim, ldt, bre, bim):
    dt = jnp.exp(ldt)
    er = jnp.exp(are * dt)
    abr = er * jnp.cos(aim * dt)
    abi = er * jnp.sin(aim * dt)
    den = are * are + aim * aim
    zr = ((abr - 1.0) * are + abi * aim) / den
    zi = (abi * are - (abr - 1.0) * aim) / den
    return abr, abi, zr * bre - zi * bim, zr * bim + zi * bre


def _s5_prep(are, aim, ldt, bre, bim):
    def body(a, b, c, d, e, o0, o1, o2, o3):
        r = _s5_discretise(a[...], b[...], c[...], d[...], e[...])
        o0[...], o1[...], o2[...], o3[...] = r

    sd = jax.ShapeDtypeStruct(are.shape, F32)
    return pl.pallas_call(body, name="s5_prep", out_shape=[sd] * 4)(are, aim, ldt, bre, bim)


def _s5_prep_bwd(are, aim, ldt, bre, bim, cts):
    def body(a, b, c, d, e, c0, c1, c2, c3, o0, o1, o2, o3, o4):
        _, vjp = jax.vjp(_s5_discretise, a[...], b[...], c[...], d[...], e[...])
        r = vjp((c0[...], c1[...], c2[...], c3[...]))
        o0[...], o1[...], o2[...], o3[...], o4[...] = r

    sd = jax.ShapeDtypeStruct(are.shape, F32)
    return pl.pallas_call(body, name="s5_prep_bwd", out_shape=[sd] * 5)(are, aim, ldt, bre, bim, *cts)


def _s5_consts(abr, abi, seg):
    shape = (SUBLANES, S5_CPLX)
    assert seg & (seg - 1) == 0 and seg % SUBLANES == 0, seg

    def body(ar_ref, ai_ref, f_ref, b_ref):
        def cmul(p, q):
            return (p[0] * q[0] - p[1] * q[1], p[0] * q[1] + p[1] * q[0])

        row = lax.broadcasted_iota(jnp.int32, shape, 0)
        a1 = (jnp.broadcast_to(ar_ref[...], shape), jnp.broadcast_to(ai_ref[...], shape))
        squares = [a1]
        while 1 << (len(squares) - 1) < 4 * seg:
            squares.append(cmul(squares[-1], squares[-1]))
        nb = seg.bit_length() - 1
        fwd, rev = [], []
        for k, a in ((1, squares[nb]), (2, squares[nb + 1]), (4, squares[nb + 2])):
            fwd += [jnp.where(row >= k, a[0], 0.0), jnp.where(row >= k, a[1], 0.0)]
            rev += [jnp.where(row <= 7 - k, a[0], 0.0), jnp.where(row <= 7 - k, -a[1], 0.0)]
        fwd += [a1[0], a1[1]]
        rev += [a1[0], -a1[1]]
        e = lax.broadcasted_iota(jnp.int32, (seg, S5_CPLX), 0) + 1
        wide = lambda v: jnp.broadcast_to(v[0:1, :], (seg, S5_CPLX))
        pr, pi = jnp.ones((seg, S5_CPLX), F32), jnp.zeros((seg, S5_CPLX), F32)
        for b in range(nb + 1):
            sr, si = wide(squares[b][0]), wide(squares[b][1])
            bit = ((e >> b) & 1) == 1
            pr, pi = jnp.where(bit, pr * sr - pi * si, pr), jnp.where(bit, pr * si + pi * sr, pi)
        f_ref[...] = jnp.concatenate(fwd + [pr, pi], axis=0)
        b_ref[...] = jnp.concatenate(rev + [pr, -pi], axis=0)

    sd = jax.ShapeDtypeStruct((8 * SUBLANES + 2 * seg, S5_CPLX), F32)
    return pl.pallas_call(body, name="s5_consts", out_shape=[sd, sd])(abr, abi)


S5_TILES = S5_LANES // LANES
S5_HALF_TILES = S5_TILES // 2


def _s5_tile_index(q):
    re = (q // 8) * S5_HALF_TILES + (q % 8)
    return re, re + S5_HALF_TILES // 2


def _lanes_of(ref, first, count):
    return jnp.concatenate([ref[j] for j in range(first, first + count)], axis=1)


def _to_lane_tiles(ref, first, value):
    for j in range(value.shape[1] // LANES):
        ref[first + j] = value[:, j * LANES:(j + 1) * LANES]


def _time_perm(tile, transpose=False):
    seg = tile // SUBLANES
    rho = lax.broadcasted_iota(jnp.int32, (tile, tile), 1 if transpose else 0)
    t = lax.broadcasted_iota(jnp.int32, (tile, tile), 0 if transpose else 1)
    return (t == (rho & (SUBLANES - 1)) * seg + (rho >> 3)).astype(BF16)


def _reorder(perm, x):
    return jnp.dot(perm, x, preferred_element_type=F32)


def _s5_scan(s_ref, sc_ref, carry_ref, tile, reverse):
    seg = tile // SUBLANES
    group = 4
    edge = 0 if reverse else SUBLANES - 1
    row = lax.broadcasted_iota(jnp.int32, (SUBLANES, LANES), 0)
    order = range(seg - 1, -1, -1) if reverse else range(seg)
    rows_of = lambda k: pl.ds(k * SUBLANES, SUBLANES)
    base = 8 * SUBLANES

    for q0 in range(0, S5_CPLX // LANES, group):
        qs = list(range(q0, q0 + group))
        tiles = [_s5_tile_index(q) for q in qs]
        cst = lambda k, q: sc_ref[k * SUBLANES:(k + 1) * SUBLANES, q * LANES:(q + 1) * LANES]
        state = [(jnp.zeros((SUBLANES, LANES), F32), jnp.zeros((SUBLANES, LANES), F32)) for _ in qs]
        mult = [(cst(6, q), cst(7, q)) for q in qs]
        for k in order:
            for j, (re, im) in enumerate(tiles):
                ar, ai = mult[j]
                xr, xi = state[j]
                nr = ar * xr - ai * xi + s_ref[re, rows_of(k), :]
                ni = ar * xi + ai * xr + s_ref[im, rows_of(k), :]
                s_ref[re, rows_of(k), :] = nr
                s_ref[im, rows_of(k), :] = ni
                state[j] = (nr, ni)
        start = []
        for j, (q, (re, im)) in enumerate(zip(qs, tiles)):
            er, ei = state[j]
            shift1 = SUBLANES - 1 if reverse else 1
            dr = jnp.where(row == SUBLANES - 1 - edge, carry_ref[re], pltpu.roll(er, shift1, 0))
            di = jnp.where(row == SUBLANES - 1 - edge, carry_ref[im], pltpu.roll(ei, shift1, 0))
            for c, sh in ((0, 1), (2, 2), (4, 4)):
                shift = SUBLANES - sh if reverse else sh
                ar, ai = cst(c, q), cst(c + 1, q)
                sr, si = pltpu.roll(dr, shift, 0), pltpu.roll(di, shift, 0)
                dr, di = dr + ar * sr - ai * si, di + ar * si + ai * sr
            start.append((dr, di))
        for k in order:
            t = seg - 1 - k if reverse else k
            for j, (q, (re, im)) in enumerate(zip(qs, tiles)):
                lanes = slice(q * LANES, (q + 1) * LANES)
                pr = jnp.broadcast_to(sc_ref[base + t:base + t + 1, lanes], (SUBLANES, LANES))
                pi = jnp.broadcast_to(sc_ref[base + seg + t:base + seg + t + 1, lanes], (SUBLANES, LANES))
                cr, ci = start[j]
                xr = s_ref[re, rows_of(k), :] + pr * cr - pi * ci
                xi = s_ref[im, rows_of(k), :] + pr * ci + pi * cr
                s_ref[re, rows_of(k), :] = xr
                s_ref[im, rows_of(k), :] = xi
                if k == order[-1]:
                    carry_ref[re] = jnp.broadcast_to(xr[edge:edge + 1, :], (SUBLANES, LANES))
                    carry_ref[im] = jnp.broadcast_to(xi[edge:edge + 1, :], (SUBLANES, LANES))


def _s5_fwd(u, bd, cdt, dskip, sc, *, rows, tile):
    n = rows // tile

    def body(u_ref, bd_ref, cdt_ref, d_ref, sc_ref, y_ref, s_ref, carry_ref):
        @pl.when(pl.program_id(0) == 0)
        def _():
            carry_ref[...] = jnp.zeros_like(carry_ref)

        ub = _reorder(_time_perm(tile), u_ref[...].astype(BF16)).astype(BF16)
        for h in range(2):
            _to_lane_tiles(s_ref, h * S5_HALF_TILES, jnp.dot(ub[:, h * S5_HALF_IN:(h + 1) * S5_HALF_IN], bd_ref[h],
                                                             preferred_element_type=F32))
        _s5_scan(s_ref, sc_ref, carry_ref, tile, reverse=False)
        ys = [_nt(_lanes_of(s_ref, h * S5_HALF_TILES, S5_HALF_TILES), cdt_ref[h]) for h in range(2)]
        y = _reorder(_time_perm(tile, transpose=True), jnp.concatenate(ys, axis=1).astype(BF16))
        y_ref[...] = y + d_ref[...] * u_ref[...]

    full = lambda a: pl.BlockSpec(a.shape, lambda i, nd=a.ndim: (0,) * nd)
    return pl.pallas_call(
        body, name="s5_fwd", grid=(n,),
        in_specs=[pl.BlockSpec((tile, S5_WIDTH), lambda i: (i, 0)), full(bd), full(cdt), full(dskip), full(sc)],
        out_specs=[pl.BlockSpec((tile, S5_WIDTH), lambda i: (i, 0)),
                   pl.BlockSpec((S5_TILES, tile, LANES), lambda i: (0, i, 0))],
        out_shape=[jax.ShapeDtypeStruct((rows, S5_WIDTH), F32), jax.ShapeDtypeStruct((S5_TILES, rows, LANES), F32)],
        scratch_shapes=[pltpu.VMEM((S5_TILES, SUBLANES, LANES), F32)],
        compiler_params=pltpu.CompilerParams(dimension_semantics=("arbitrary",), vmem_limit_bytes=VMEM_LIMIT),
    )(u, bd, cdt, dskip, sc)


def _s5_bwd(dy, s, u, bd, cdt, dskip, sc, *, rows, tile):
    n = rows // tile
    hc = 2 * S5_HALF_CPLX
    per8 = tile // SUBLANES
    quarter = S5_HALF_TILES // 2

    def body(dy_ref, s_ref, sp_ref, u_ref, bd_ref, cdt_ref, d_ref, sc_ref,
             du_ref, dbd_ref, dcdt_ref, dd_ref, da_ref, g_ref, carry_ref):
        i = pl.program_id(0)

        @pl.when(i == 0)
        def _():
            carry_ref[...] = jnp.zeros_like(carry_ref)
            dbd_ref[...] = jnp.zeros_like(dbd_ref)
            dcdt_ref[...] = jnp.zeros_like(dcdt_ref)
            dd_ref[...] = jnp.zeros_like(dd_ref)
            da_ref[...] = jnp.zeros_like(da_ref)

        dy = dy_ref[...]
        u = u_ref[...]
        perm = _time_perm(tile)
        dyb = _reorder(perm, dy.astype(BF16)).astype(BF16)
        ub = _reorder(perm, u.astype(BF16)).astype(BF16)
        for h in range(2):
            _to_lane_tiles(g_ref, h * S5_HALF_TILES, jnp.dot(dyb[:, h * S5_HALF_IN:(h + 1) * S5_HALF_IN], cdt_ref[h],
                                                             preferred_element_type=F32))
        _s5_scan(g_ref, sc_ref, carry_ref, tile, reverse=True)
        dus = []
        for h in range(2):
            gb = _lanes_of(g_ref, h * S5_HALF_TILES, S5_HALF_TILES).astype(BF16)
            sb = _lanes_of(s_ref, h * S5_HALF_TILES, S5_HALF_TILES).astype(BF16)
            dus.append(_nt(gb, bd_ref[h]))
            dbd_ref[h] += _tn(ub[:, h * S5_HALF_IN:(h + 1) * S5_HALF_IN], gb)
            dcdt_ref[h] += _tn(dyb[:, h * S5_HALF_IN:(h + 1) * S5_HALF_IN], sb)
        du = _reorder(_time_perm(tile, transpose=True), jnp.concatenate(dus, axis=1).astype(BF16))
        du_ref[...] = (du + d_ref[...] * dy).astype(du_ref.dtype)
        dd_ref[...] += jnp.sum(dy * u, axis=0, keepdims=True)

        not_first = (i < n - 1).astype(F32)
        row = lax.broadcasted_iota(jnp.int32, (SUBLANES, quarter * LANES), 0)

        def step_before(first):
            cur = _lanes_of(s_ref, first, quarter)
            before_tile = _lanes_of(sp_ref, first, quarter)[SUBLANES - 1:SUBLANES, :] * not_first
            head = jnp.where(row == 0, before_tile, pltpu.roll(cur[tile - SUBLANES:], 1, 0))
            return jnp.concatenate([head, cur[:tile - SUBLANES]], axis=0)

        for h in range(2):
            re, im = h * S5_HALF_TILES, h * S5_HALF_TILES + quarter
            ssr = step_before(re)
            ssi = step_before(im)
            gr = _lanes_of(g_ref, re, quarter)
            gi = _lanes_of(g_ref, im, quarter)
            lanes = slice(h * S5_HALF_CPLX, (h + 1) * S5_HALF_CPLX)
            da_ref[0:1, lanes] += jnp.sum(ssr * gr + ssi * gi, axis=0, keepdims=True)
            da_ref[1:2, lanes] += jnp.sum(ssr * gi - ssi * gr, axis=0, keepdims=True)

    full = lambda a: pl.BlockSpec(a.shape, lambda i, nd=a.ndim: (0,) * nd)
    rev = lambda i: (n - 1 - i, 0)
    wshape = (2, S5_HALF_IN, hc)
    return pl.pallas_call(
        body, name="s5_bwd", grid=(n,),
        in_specs=[pl.BlockSpec((tile, S5_WIDTH), rev), pl.BlockSpec((S5_TILES, tile, LANES), lambda i: (0, n - 1 - i, 0)),
                  pl.BlockSpec((S5_TILES, SUBLANES, LANES), lambda i: (0, jnp.maximum((n - 1 - i) * per8 - 1, 0), 0)),
                  pl.BlockSpec((tile, S5_WIDTH), rev), full(bd), full(cdt), full(dskip), full(sc)],
        out_specs=[pl.BlockSpec((tile, S5_WIDTH), rev),
                   pl.BlockSpec(wshape, lambda i: (0, 0, 0)), pl.BlockSpec(wshape, lambda i: (0, 0, 0)),
                   pl.BlockSpec((1, S5_WIDTH), lambda i: (0, 0)), pl.BlockSpec((SUBLANES, S5_CPLX), lambda i: (0, 0))],
        out_shape=[jax.ShapeDtypeStruct((rows, S5_WIDTH), BF16), jax.ShapeDtypeStruct(wshape, F32),
                   jax.ShapeDtypeStruct(wshape, F32), jax.ShapeDtypeStruct((1, S5_WIDTH), F32),
                   jax.ShapeDtypeStruct((SUBLANES, S5_CPLX), F32)],
        scratch_shapes=[pltpu.VMEM((S5_TILES, tile, LANES), F32), pltpu.VMEM((S5_TILES, SUBLANES, LANES), F32)],
        compiler_params=pltpu.CompilerParams(dimension_semantics=("arbitrary",), vmem_limit_bytes=VMEM_LIMIT),
    )(dy, s, s, u, bd, cdt, dskip, sc)


def _s5_block_diag(parts):
    v = jnp.stack(parts, axis=2).reshape(2, 16, S5_GROUP, 2, S5_STATE)
    eye = jnp.eye(16, dtype=v.dtype)
    return jnp.einsum("hgcpn,gk->hgcpkn", v, eye).reshape(2, S5_HALF_IN, 2 * S5_HALF_CPLX)


def _s5_block_diag_extract(m):
    v = m.reshape(2, 16, S5_GROUP, 2, 16, S5_STATE)
    d = jnp.diagonal(v, axis1=1, axis2=4)
    d = jnp.transpose(d, (2, 0, 4, 1, 3)).reshape(2, S5_GROUPS, S5_GROUP, S5_STATE)
    return d[0], d[1]


def _cplx_to_lanes(v):
    return v.reshape(1, S5_CPLX)


def _lru_scan_fwd(a, b, *, rows, tile):
    n = rows // tile
    nblk = tile // SUBLANES
    group = 5

    def body(a_ref, b_ref, h_ref, carry_ref):
        @pl.when(pl.program_id(0) == 0)
        def _():
            carry_ref[...] = jnp.zeros_like(carry_ref)

        row = lax.broadcasted_iota(jnp.int32, (SUBLANES, LANES), 0)
        for q0 in range(0, LRU_WIDTH // LANES, group):
            offs = [q * LANES for q in range(q0, q0 + group)]

            def blk(t, carry, offs=offs):
                r0 = pl.multiple_of(t * SUBLANES, SUBLANES)
                new = []
                for j, o in enumerate(offs):
                    av = a_ref[pl.ds(r0, SUBLANES), o:o + LANES]
                    xv = b_ref[pl.ds(r0, SUBLANES), o:o + LANES]
                    for sh in (1, 2, 4):
                        m = row >= sh
                        xs = pltpu.roll(xv, sh, 0)
                        asft = pltpu.roll(av, sh, 0)
                        xv = xv + jnp.where(m, av * xs, 0.0)
                        av = jnp.where(m, av * asft, av)
                    hv = xv + av * carry[j]
                    h_ref[pl.ds(r0, SUBLANES), o:o + LANES] = hv
                    new.append(jnp.broadcast_to(hv[SUBLANES - 1:SUBLANES, :], (SUBLANES, LANES)))
                return tuple(new)

            carry = lax.fori_loop(0, nblk, blk, tuple(carry_ref[:, o:o + LANES] for o in offs), unroll=2)
            for j, o in enumerate(offs):
                carry_ref[:, o:o + LANES] = carry[j]

    spec = pl.BlockSpec((tile, LRU_WIDTH), lambda i: (i, 0))
    return pl.pallas_call(
        body, name="lru_scan_fwd", grid=(n,), in_specs=[spec, spec], out_specs=spec,
        out_shape=jax.ShapeDtypeStruct((rows, LRU_WIDTH), F32),
        scratch_shapes=[pltpu.VMEM((SUBLANES, LRU_WIDTH), F32)],
        compiler_params=pltpu.CompilerParams(dimension_semantics=("arbitrary",), vmem_limit_bytes=VMEM_LIMIT),
    )(a, b)


def _lru_scan_bwd(dh, a, *, rows, tile):
    n = rows // tile
    nblk = tile // SUBLANES
    group = 5

    def body(dh_ref, a_ref, g_ref, cg_ref, ca_ref):
        @pl.when(pl.program_id(0) == 0)
        def _():
            cg_ref[...] = jnp.zeros_like(cg_ref)
            ca_ref[...] = jnp.zeros_like(ca_ref)

        row = lax.broadcasted_iota(jnp.int32, (SUBLANES, LANES), 0)
        for q0 in range(0, LRU_WIDTH // LANES, group):
            offs = [q * LANES for q in range(q0, q0 + group)]

            def blk(t, carry, offs=offs):
                r0 = pl.multiple_of((nblk - 1 - t) * SUBLANES, SUBLANES)
                new = []
                for j, o in enumerate(offs):
                    cg, ca = carry[2 * j], carry[2 * j + 1]
                    araw = a_ref[pl.ds(r0, SUBLANES), o:o + LANES]
                    xv = dh_ref[pl.ds(r0, SUBLANES), o:o + LANES]
                    av = jnp.where(row == SUBLANES - 1, ca, pltpu.roll(araw, SUBLANES - 1, 0))
                    for sh in (1, 2, 4):
                        m = row <= SUBLANES - 1 - sh
                        xs = pltpu.roll(xv, SUBLANES - sh, 0)
                        asft = pltpu.roll(av, SUBLANES - sh, 0)
                        xv = xv + jnp.where(m, av * xs, 0.0)
                        av = jnp.where(m, av * asft, av)
                    gv = xv + av * cg
                    g_ref[pl.ds(r0, SUBLANES), o:o + LANES] = gv
                    new.append(jnp.broadcast_to(gv[0:1, :], (SUBLANES, LANES)))
                    new.append(jnp.broadcast_to(araw[0:1, :], (SUBLANES, LANES)))
                return tuple(new)

            carry0 = tuple(r[:, o:o + LANES] for o in offs for r in (cg_ref, ca_ref))
            carry = lax.fori_loop(0, nblk, blk, carry0, unroll=2)
            for j, o in enumerate(offs):
                cg_ref[:, o:o + LANES] = carry[2 * j]
                ca_ref[:, o:o + LANES] = carry[2 * j + 1]

    spec = pl.BlockSpec((tile, LRU_WIDTH), lambda i: (n - 1 - i, 0))
    return pl.pallas_call(
        body, name="lru_scan_bwd", grid=(n,), in_specs=[spec, spec], out_specs=spec,
        out_shape=jax.ShapeDtypeStruct((rows, LRU_WIDTH), F32),
        scratch_shapes=[pltpu.VMEM((SUBLANES, LRU_WIDTH), F32), pltpu.VMEM((SUBLANES, LRU_WIDTH), F32)],
        compiler_params=pltpu.CompilerParams(dimension_semantics=("arbitrary",), vmem_limit_bytes=VMEM_LIMIT),
    )(dh, a)


def _conv_fwd(i, x, prev, cw, cb):
    prev = prev * (i > 0).astype(F32)
    y = x * cw[3:4, :] + cb
    for s in range(1, CONV_WIDTH):
        y = y + _rows_before(x, prev, s) * cw[3 - s:4 - s, :]
    return y


def _lru_gates(c, wa, ba, wx, bx, lam):
    r = _sigmoid(_heads(_nn, c, wa) + ba)
    ig = _sigmoid(_heads(_nn, c, wx) + bx)
    z = -lam
    sp = jnp.maximum(z, 0.0) + jnp.log(1.0 + jnp.exp(-jnp.abs(z)))
    log_a = -LRU_C * r * sp
    a = jnp.exp(log_a)
    z2 = 2.0 * log_a
    series = -z2 * (1.0 + z2 * (0.5 + z2 * (1.0 / 6.0 + z2 * (1.0 / 24.0 + z2 * (1.0 / 120.0 + z2 / 720.0)))))
    one_minus = jnp.where(z2 > -0.2, series, 1.0 - jnp.exp(z2))
    mult = jnp.sqrt(one_minus)
    return r, ig, sp, a, mult


def _layer_fwd(x, p, w, rows):
    tile = WIDE_TILE
    d = D_MODEL

    def f_in(i, xb, g, *ws):
        rstd = lax.rsqrt(jnp.mean(xb * xb, axis=-1, keepdims=True) + NORM_EPS)
        hb = (xb * rstd * g).astype(BF16)
        proj = jnp.concatenate([jnp.dot(hb, wj, preferred_element_type=F32) for wj in ws], axis=1)
        return tuple(proj[:, IN_OFFSETS[k]:IN_OFFSETS[k + 1]] for k in range(6)) + (hb,)

    s5x, s5g, lrux, lrug, gs, gl, h = _rows(
        "f_in", f_in, [(x, "row"), (w["g_pre"], "full")] + w["w_in"],
        [((rows, wd), BF16, "row") for wd in IN_WIDTHS] + [((rows, d), BF16, "row")], rows=rows, tile=ROW_TILE)

    ys, st = _s5_fwd(s5x, w["bd"], w["cdt"], w["s5_d"], w["scf"], rows=rows, tile=ROW_TILE)

    def f_s5post(i, ysb, gb, wglu, wbs):
        glv, _ = _gelu_parts(ysb)
        glu = _nn(glv, wglu)
        y2 = glu[:, :S5_WIDTH] * _sigmoid(glu[:, S5_WIDTH:]) * (gb * _sigmoid(gb))
        return (_nn(y2, wbs),)

    (z_s,) = _rows("f_s5post", f_s5post, [(ys, "row"), (s5g, "row"), (w["w_glu"], "full"), (w["w_bs"], "full")],
                   [((rows, d), BF16, "row")], rows=rows, tile=tile)

    def f_gates(i, xb, prev, cw, cb, wa, ba, wx, bx, lam):
        c = _conv_fwd(i, xb, prev, cw, cb)
        _, ig, _, a, mult = _lru_gates(c, wa, ba, wx, bx, lam)
        return a, mult * (ig * c)

    a, b = _rows("f_gates", f_gates,
                 [(lrux, "row"), (lrux, "prev"), (w["conv_w"], "full"), (w["conv_b"], "full"), (w["lru_w_a"], "full"),
                  (w["lru_b_a"], "full"), (w["lru_w_x"], "full"), (w["lru_b_x"], "full"), (w["lru_lambda"], "full")],
                 [((rows, LRU_WIDTH), F32, "row")] * 2, rows=rows, tile=tile)
    hl = _lru_scan_fwd(a, b, rows=rows, tile=tile)

    def f_merge(i, hb, lg, zs, gsb, glb, xb, wbl, wout, gpost):
        z_l = _nn(hb * (lg * _sigmoid(lg)), wbl)
        merged = _sigmoid(gsb) * zs + _sigmoid(glb) * z_l
        mix = _nn(merged, wout)
        rstd = lax.rsqrt(jnp.mean(mix * mix, axis=-1, keepdims=True) + NORM_EPS)
        return xb + mix * rstd * gpost, mix, z_l

    x1, mix, z_l = _rows("f_merge", f_merge,
                         [(hl, "row"), (lrug, "row"), (z_s, "row"), (gs, "row"), (gl, "row"), (x, "row"),
                          (w["w_bl"], "full"), (w["w_out"], "full"), (w["g_post"], "full")],
                         [((rows, d), F32, "row"), ((rows, d), BF16, "row"), ((rows, d), BF16, "row")], rows=rows, tile=tile)

    def f_ple(i, x1b, pb, wple, wpg):
        return (x1b + _nn(pb, wple) * _sigmoid(_nn(x1b, wpg)),)

    (x2,) = _rows("f_ple", f_ple, [(x1, "row"), (p, "row"), (w["w_ple"], "full"), (w["w_ple_gate"], "full")],
                  [((rows, d), F32, "row")], rows=rows, tile=tile)
    saved = dict(x=x, h=h, s5x=s5x, s5g=s5g, lrux=lrux, lrug=lrug, gs=gs, gl=gl, ys=ys, st=st, a=a, hl=hl, z_s=z_s,
                 z_l=z_l, mix=mix, x1=x1, p=p)
    return x2, saved


def _layer_bwd(dx2, sv, w, rows):
    tile = WIDE_TILE
    d = D_MODEL
    g = {}

    def b_ple(i, dxb, x1b, pb, wple, wpg):
        pe = _nn(pb, wple)
        sg = _sigmoid(_nn(x1b, wpg))
        dpe = dxb * sg
        dgt = dxb * pe * sg * (1.0 - sg)
        return dxb + _nt(dgt, wpg), _tn(pb, dpe), _tn(x1b, dgt)

    dx1, g["w_ple"], g["w_ple_gate"] = _rows(
        "b_ple", b_ple, [(dx2, "row"), (sv["x1"], "row"), (sv["p"], "row"), (w["w_ple"], "full"), (w["w_ple_gate"], "full")],
        [((rows, d), F32, "row"), ((PLE_DIM, d), F32, "acc"), ((d, d), F32, "acc")], rows=rows, tile=tile)

    def b_merge(i, dxb, mixb, zs, zl, gsb, glb, wout, gpost):
        rstd = lax.rsqrt(jnp.mean(mixb * mixb, axis=-1, keepdims=True) + NORM_EPS)
        nrm = mixb * rstd
        dn = dxb * gpost
        dmix = rstd * (dn - nrm * jnp.mean(dn * nrm, axis=-1, keepdims=True))
        ss, sl = _sigmoid(gsb), _sigmoid(glb)
        merged = ss * zs + sl * zl
        dm = _nt(dmix, wout)
        return (dm * ss, dm * sl, dm * zs * ss * (1.0 - ss), dm * zl * sl * (1.0 - sl),
                _tn(merged, dmix), jnp.sum(dxb * nrm, axis=0, keepdims=True))

    dz_s, dz_l, dgs, dgl, g["w_out"], g["g_post"] = _rows(
        "b_merge", b_merge,
        [(dx1, "row"), (sv["mix"], "row"), (sv["z_s"], "row"), (sv["z_l"], "row"), (sv["gs"], "row"), (sv["gl"], "row"),
         (w["w_out"], "full"), (w["g_post"], "full")],
        [((rows, d), BF16, "row")] * 4 + [((d, d), F32, "acc"), ((1, d), F32, "acc")], rows=rows, tile=tile)

    def b_bl(i, dzl, hb, lg, wbl):
        sl = _sigmoid(lg)
        silu = lg * sl
        dy3 = _nt(dzl, wbl)
        return dy3 * silu, dy3 * hb * sl * (1.0 + lg * (1.0 - sl)), _tn(hb * silu, dzl)

    dh, dlrug, g["w_bl"] = _rows(
        "b_bl", b_bl, [(dz_l, "row"), (sv["hl"], "row"), (sv["lrug"], "row"), (w["w_bl"], "full")],
        [((rows, LRU_WIDTH), F32, "row"), ((rows, LRU_WIDTH), BF16, "row"), ((LRU_WIDTH, d), F32, "acc")], rows=rows, tile=tile)

    gh = _lru_scan_bwd(dh, sv["a"], rows=rows, tile=tile)

    def b_gates(i, ghb, hb, hprev, xb, xprev, cw, cb, wa, ba, wx, bx, lam):
        c = _conv_fwd(i, xb, xprev, cw, cb)
        r, ig, sp, a, mult = _lru_gates(c, wa, ba, wx, bx, lam)
        h_before = _rows_before(hb, hprev * (i > 0).astype(F32), 1)
        da = ghb * h_before
        dmult = ghb * ig * c
        dlog_a = da * a - dmult * a * a / mult
        dpre_r = dlog_a * (-LRU_C) * sp * r * (1.0 - r)
        dpre_i = ghb * mult * c * ig * (1.0 - ig)
        dc = ghb * mult * ig + _heads(_nt, dpre_r, wa) + _heads(_nt, dpre_i, wx)
        dlam = jnp.sum(dlog_a * LRU_C * r, axis=0, keepdims=True) * _sigmoid(-lam)
        return (dc, _heads_tn(c, dpre_r), _heads_tn(c, dpre_i), jnp.sum(dpre_r, axis=0, keepdims=True),
                jnp.sum(dpre_i, axis=0, keepdims=True), dlam)

    hshape = (LRU_HEADS, LRU_HEAD_DIM, LRU_HEAD_DIM)
    dc, g["lru_w_a"], g["lru_w_x"], g["lru_b_a"], g["lru_b_x"], g["lru_lambda"] = _rows(
        "b_gates", b_gates,
        [(gh, "row"), (sv["hl"], "row"), (sv["hl"], "prev"), (sv["lrux"], "row"), (sv["lrux"], "prev"),
         (w["conv_w"], "full"), (w["conv_b"], "full"), (w["lru_w_a"], "full"), (w["lru_b_a"], "full"),
         (w["lru_w_x"], "full"), (w["lru_b_x"], "full"), (w["lru_lambda"], "full")],
        [((rows, LRU_WIDTH), BF16, "row"), (hshape, F32, "acc"), (hshape, F32, "acc")] + [((1, LRU_WIDTH), F32, "acc")] * 3,
        rows=rows, tile=tile)

    n_tiles = rows // min(tile, rows)

    def b_conv(i, dcb, dnext, xb, xprev, cw):
        dnext = dnext * (i < n_tiles - 1).astype(F32)
        xprev = xprev * (i > 0).astype(F32)
        dx = dcb * cw[3:4, :]
        dws = [jnp.sum(dcb * xb, axis=0, keepdims=True)]
        for s in range(1, CONV_WIDTH):
            dx = dx + _rows_after(dcb, dnext, s) * cw[3 - s:4 - s, :]
            dws.append(jnp.sum(dcb * _rows_before(xb, xprev, s), axis=0, keepdims=True))
        return dx, jnp.concatenate(dws[::-1], axis=0), jnp.sum(dcb, axis=0, keepdims=True)

    dlrux, g["conv_w"], g["conv_b"] = _rows(
        "b_conv", b_conv, [(dc, "row"), (dc, "next"), (sv["lrux"], "row"), (sv["lrux"], "prev"), (w["conv_w"], "full")],
        [((rows, LRU_WIDTH), BF16, "row"), ((CONV_WIDTH, LRU_WIDTH), F32, "acc"), ((1, LRU_WIDTH), F32, "acc")],
        rows=rows, tile=tile)

    def b_s5post(i, dzs, ysb, gb, wglu, wbs):
        glv, dgelu = _gelu_parts(ysb)
        glu = _nn(glv, wglu)
        ga, gb2 = glu[:, :S5_WIDTH], glu[:, S5_WIDTH:]
        sb = _sigmoid(gb2)
        sg = _sigmoid(gb)
        silu = gb * sg
        y2 = ga * sb * silu
        dy2 = _nt(dzs, wbs)
        dglu = jnp.concatenate([dy2 * sb * silu, dy2 * ga * silu * sb * (1.0 - sb)], axis=1)
        dg = dy2 * ga * sb * sg * (1.0 + gb * (1.0 - sg))
        return _nt(dglu, wglu) * dgelu, dg, _tn(y2, dzs), _tn(glv, dglu)

    dys, ds5g, g["w_bs"], g["w_glu"] = _rows(
        "b_s5post", b_s5post, [(dz_s, "row"), (sv["ys"], "row"), (sv["s5g"], "row"), (w["w_glu"], "full"), (w["w_bs"], "full")],
        [((rows, S5_WIDTH), F32, "row"), ((rows, S5_WIDTH), BF16, "row"), ((S5_WIDTH, d), F32, "acc"),
         ((S5_WIDTH, 2 * S5_WIDTH), F32, "acc")],
        rows=rows, tile=tile)

    ds5x, g["bd"], g["cdt"], g["s5_d"], g["abar"] = _s5_bwd(dys, sv["st"], sv["s5x"], w["bd"], w["cdt"], w["s5_d"],
                                                            w["scb"], rows=rows, tile=ROW_TILE)

    dcomps = [ds5x, ds5g, dlrux, dlrug, dgs, dgl]

    def b_in(i, xb, dx1b, gpre, *rest):
        dproj, ws = jnp.concatenate(rest[:6], axis=1), rest[6:]
        dh = _nt(dproj[:, :IN_SLOT], ws[0])
        for j in range(1, 4):
            dh = dh + _nt(dproj[:, j * IN_SLOT:(j + 1) * IN_SLOT], ws[j])
        rstd = lax.rsqrt(jnp.mean(xb * xb, axis=-1, keepdims=True) + NORM_EPS)
        nrm = xb * rstd
        dn = dh * gpre
        dx = rstd * (dn - nrm * jnp.mean(dn * nrm, axis=-1, keepdims=True))
        return dx1b + dx, jnp.sum(dh * nrm, axis=0, keepdims=True)

    dx, g["g_pre"] = _rows(
        "b_in", b_in, [(sv["x"], "row"), (dx1, "row"), (w["g_pre"], "full")] + [(dcv, "raw") for dcv in dcomps]
        + w["w_in"],
        [((rows, d), F32, "row"), ((1, d), F32, "acc")], rows=rows, tile=ROW_TILE)

    g["w_in"] = []
    for j in range(4):
        lo, hi = j * IN_SLOT, (j + 1) * IN_SLOT
        ks = [k for k in range(6) if IN_OFFSETS[k] < hi and IN_OFFSETS[k + 1] > lo]
        first = IN_OFFSETS[ks[0]]

        def b_win(i, hb, *parts, lo=lo, hi=hi, first=first):
            return (_tn(hb, jnp.concatenate(parts, axis=1)[:, lo - first:hi - first]),)

        g["w_in"].append(_rows("b_win", b_win, [(sv["h"], "raw")] + [(dcomps[k], "raw") for k in ks],
                               [((d, IN_SLOT), F32, "acc")], rows=rows, tile=4 * ROW_TILE)[0])
    return dx, g


SMALL = ("g_pre", "s5_a_re", "s5_a_im", "s5_log_dt", "s5_b_re", "s5_b_im", "s5_c_re", "s5_c_im", "s5_d", "conv_b",
         "lru_w_a", "lru_b_a", "lru_w_x", "lru_b_x", "lru_lambda", "g_post")
BIG = ("w_in", "w_glu", "w_bs", "conv_w", "w_bl", "w_out", "w_ple", "w_ple_gate")


def _bcast_groups(v):
    return jnp.broadcast_to(v[:, None, :], (S5_GROUPS, S5_GROUP, S5_STATE)).reshape(S5_WIDTH, S5_STATE)


def _s5_prep_inputs(wl):
    ldt = jnp.broadcast_to(wl["s5_log_dt"][:, None], (S5_GROUPS, S5_STATE))
    gcn = lambda b: jnp.transpose(b, (0, 2, 1)).reshape(S5_WIDTH, S5_STATE)
    return (_bcast_groups(wl["s5_a_re"]), _bcast_groups(wl["s5_a_im"]), _bcast_groups(ldt), gcn(wl["s5_b_re"]),
            gcn(wl["s5_b_im"]))


def _layer_weights(wl):
    w = {}
    w["w_in"] = [wc if isinstance(wc, tuple) else (wc, "full") for wc in wl["w_in"]]
    for k in ("w_glu", "w_bs", "w_bl", "w_out", "w_ple", "w_ple_gate"):
        w[k] = wl[k]
    w["conv_w"] = wl["conv_w"]
    for k in ("g_pre", "g_post", "s5_d", "conv_b", "lru_b_a", "lru_b_x", "lru_lambda"):
        w[k] = wl[k].reshape(1, -1)
    w["lru_w_a"] = wl["lru_w_a"].astype(BF16)
    w["lru_w_x"] = wl["lru_w_x"].astype(BF16)
    prep_in = _s5_prep_inputs(wl)
    abr, abi, bbr, bbi = _s5_prep(*prep_in)
    w["prep_in"] = prep_in
    shape3 = (S5_GROUPS, S5_GROUP, S5_STATE)
    w["bd"] = _s5_block_diag([bbr.reshape(shape3), bbi.reshape(shape3)]).astype(BF16)
    w["cdt"] = _s5_block_diag([wl["s5_c_re"], -wl["s5_c_im"]]).astype(BF16)
    abr_s = abr.reshape(shape3)[:, 0, :]
    abi_s = abi.reshape(shape3)[:, 0, :]
    w["scf"], w["scb"] = _s5_consts(_cplx_to_lanes(abr_s), _cplx_to_lanes(abi_s), ROW_TILE // SUBLANES)
    return w


def _layer_param_grads(g, w, wl):
    out = {}
    shape3 = (S5_GROUPS, S5_GROUP, S5_STATE)
    dbr, dbi = _s5_block_diag_extract(g["bd"])
    dcr, dci = _s5_block_diag_extract(g["cdt"])
    out["s5_c_re"], out["s5_c_im"] = dcr, -dci
    zeros = jnp.zeros(shape3, F32)
    dar = zeros.at[:, 0, :].set(g["abar"][0].reshape(S5_GROUPS, S5_STATE)).reshape(S5_WIDTH, S5_STATE)
    dai = zeros.at[:, 0, :].set(g["abar"][1].reshape(S5_GROUPS, S5_STATE)).reshape(S5_WIDTH, S5_STATE)
    cts = (dar, dai, dbr.reshape(S5_WIDTH, S5_STATE), dbi.reshape(S5_WIDTH, S5_STATE))
    d_are, d_aim, d_ldt, d_bre, d_bim = _s5_prep_bwd(*w["prep_in"], cts)
    out["s5_a_re"] = d_are.reshape(shape3).sum(axis=1)
    out["s5_a_im"] = d_aim.reshape(shape3).sum(axis=1)
    out["s5_log_dt"] = d_ldt.reshape(shape3).sum(axis=(1, 2))
    out["s5_b_re"] = jnp.transpose(d_bre.reshape(shape3), (0, 2, 1))
    out["s5_b_im"] = jnp.transpose(d_bim.reshape(shape3), (0, 2, 1))
    out["s5_d"] = g["s5_d"].reshape(-1)
    for k in ("g_pre", "g_post", "conv_b", "lru_b_a", "lru_b_x", "lru_lambda"):
        out[k] = g[k].reshape(-1)
    for k in ("lru_w_a", "lru_w_x", "conv_w", "w_in", "w_glu", "w_bs", "w_bl", "w_out", "w_ple", "w_ple_gate"):
        out[k] = g[k]
    return out


def _local_step(x, p, layers, target):
    rows = x.shape[0]
    ws = [_layer_weights(wl) for wl in layers]
    saved = []
    for i in range(DEPTH):
        x, sv = _layer_fwd(x, p[i], ws[i], rows)
        saved.append(sv)

    def f_loss(i, yb, tb):
        e = yb - tb
        return e * (1.0 / D_MODEL), jnp.sum(jnp.sum(e * e, axis=0, keepdims=True), axis=1, keepdims=True)

    dx, sq = _rows("f_loss", f_loss, [(x, "row"), (target, "row")],
                   [((rows, D_MODEL), F32, "row"), ((1, 1), F32, "acc")], rows=rows, tile=WIDE_TILE)
    loss = sq[0, 0] * (0.5 / D_MODEL)
    grads = [None] * DEPTH
    for i in reversed(range(DEPTH)):
        dx, g = _layer_bwd(dx, saved[i], ws[i], rows)
        grads[i] = _layer_param_grads(g, ws[i], layers[i])
    return loss, dx, grads


def _place():
    return lax.axis_index("x"), lax.axis_index("y"), lax.axis_index("c")


def _other_chips(x, y):
    return [(1 - x, y), (x, 1 - y), (1 - x, 1 - y)]


def _any_spec():
    return pl.BlockSpec(memory_space=pl.ANY)


def _gather_chips(name, vs, via_sibling):
    n = len(vs)
    halved = [via_sibling and v.shape[0] % (16 * 4 // v.dtype.itemsize) == 0 for v in vs]

    def body(*refs):
        v_refs, out_refs, send_sems, recv_sems = refs[:n], refs[n:2 * n], refs[2 * n], refs[2 * n + 1]
        x, y, c = _place()
        me = 2 * x + y
        chips = _other_chips(x, y)
        slots = [2 * cx + cy for cx, cy in chips]

        def part(a, slot, hc):
            if not halved[a]:
                return out_refs[a].at[slot]
            half = vs[a].shape[0] // 2
            return out_refs[a].at[slot, pl.ds(hc * half, half), :]

        def own(a):
            if not halved[a]:
                return v_refs[a]
            half = vs[a].shape[0] // 2
            return v_refs[a].at[pl.ds(c * half, half), :]

        def copy(a, k, src, dst, to):
            return pltpu.make_async_remote_copy(src_ref=src, dst_ref=dst, send_sem=send_sems.at[6 * a + k],
                                                recv_sem=recv_sems.at[6 * a + k], device_id=to, device_id_type=MESH)

        for a in range(n):
            for k in range(3):
                copy(a, k, own(a), part(a, me, c), (*chips[k], c)).start()
        for a in range(n):
            for k in range(3):
                copy(a, k, own(a), part(a, slots[k], c), (*chips[k], c)).wait_recv()
                if halved[a]:
                    copy(a, 3 + k, part(a, slots[k], c), part(a, slots[k], c), (x, y, 1 - c)).start()
        for a in range(n):
            for k in range(3):
                if halved[a]:
                    copy(a, 3 + k, own(a), part(a, slots[k], 1 - c), (x, y, 1 - c)).wait_recv()
                    copy(a, 3 + k, own(a), part(a, me, c), (x, y, 1 - c)).wait_send()
                copy(a, k, own(a), part(a, me, c), (x, y, 1 - c)).wait_send()

    others = pl.pallas_call(
        body, name=name, out_shape=[jax.ShapeDtypeStruct((4,) + v.shape, v.dtype) for v in vs],
        in_specs=[_any_spec()] * n, out_specs=[_any_spec()] * n,
        scratch_shapes=[pltpu.SemaphoreType.DMA((6 * n,)), pltpu.SemaphoreType.DMA((6 * n,))],
    )(*vs)
    me = 2 * lax.axis_index("x") + lax.axis_index("y")
    return [lax.dynamic_update_slice(o, v[None], (me, 0, 0)) for o, v in zip(others, vs)]


def _rs_sibling(grs):
    n = len(grs)
    halves = [g.shape[1] // 2 for g in grs]

    def body(*refs):
        g_refs, got_refs, send_sems, recv_sems = refs[:n], refs[n:2 * n], refs[2 * n], refs[2 * n + 1]
        x, y, c = _place()
        copies = [pltpu.make_async_remote_copy(
            src_ref=g_refs[a].at[:, pl.ds((1 - c) * halves[a], halves[a]), :], dst_ref=got_refs[a],
            send_sem=send_sems.at[a], recv_sem=recv_sems.at[a], device_id=(x, y, 1 - c), device_id_type=MESH)
            for a in range(n)]
        for cp in copies:
            cp.start()
        for cp in copies:
            cp.wait()

    return pl.pallas_call(
        body, name="rs_sibling",
        out_shape=[jax.ShapeDtypeStruct((4, h, g.shape[2]), F32) for g, h in zip(grs, halves)],
        in_specs=[_any_spec()] * n, out_specs=[_any_spec()] * n,
        scratch_shapes=[pltpu.SemaphoreType.DMA((n,)), pltpu.SemaphoreType.DMA((n,))],
    )(*grs)


def _rs_chips(a16s):
    n = len(a16s)

    def body(*refs):
        a_refs, got_refs, send_sems, recv_sems = refs[:n], refs[n:2 * n], refs[2 * n], refs[2 * n + 1]
        x, y, c = _place()
        chips = _other_chips(x, y)
        copies = [pltpu.make_async_remote_copy(
            src_ref=a_refs[a].at[2 * cx + cy], dst_ref=got_refs[a].at[k], send_sem=send_sems.at[3 * a + k],
            recv_sem=recv_sems.at[3 * a + k], device_id=(cx, cy, c), device_id_type=MESH)
            for a in range(n) for k, (cx, cy) in enumerate(chips)]
        for cp in copies:
            cp.start()
        for cp in copies:
            cp.wait()

    return pl.pallas_call(
        body, name="rs_chips", out_shape=[jax.ShapeDtypeStruct((3,) + a.shape[1:], a.dtype) for a in a16s],
        in_specs=[_any_spec()] * n, out_specs=[_any_spec()] * n,
        scratch_shapes=[pltpu.SemaphoreType.DMA((3 * n,)), pltpu.SemaphoreType.DMA((3 * n,))],
    )(*a16s)


def _swap_halves(vs):
    n = len(vs)

    def body(*refs):
        v_refs, out_refs, send_sems, recv_sems = refs[:n], refs[n:2 * n], refs[2 * n], refs[2 * n + 1]
        x, y, c = _place()

        def give(a, hc):
            return pltpu.make_async_remote_copy(src_ref=v_refs[a], dst_ref=out_refs[a].at[hc], send_sem=send_sems.at[a],
                                                recv_sem=recv_sems.at[a], device_id=(x, y, 1 - c), device_id_type=MESH)

        for a in range(n):
            give(a, c).start()
        for a in range(n):
            give(a, c).wait_send()
            give(a, 1 - c).wait_recv()

    others = pl.pallas_call(
        body, name="swap_halves", out_shape=[jax.ShapeDtypeStruct((2,) + v.shape, v.dtype) for v in vs],
        in_specs=[_any_spec()] * n, out_specs=[_any_spec()] * n,
        scratch_shapes=[pltpu.SemaphoreType.DMA((n,)), pltpu.SemaphoreType.DMA((n,))],
    )(*vs)
    c = lax.axis_index("c")
    return [lax.dynamic_update_slice(o, v[None], (c, 0, 0)).reshape(2 * v.shape[0], v.shape[1]) for o, v in zip(others, vs)]


WIDE = 1024


def _unpack(flat, shapes, align=1):
    out, off = [], 0
    for s in shapes:
        n = 1
        for q in s:
            n *= q
        out.append(flat[off:off + n].reshape(s))
        off += -(-n // align) * align
    return out


def _pack_tile(rows, width):
    most = (2 ** 21) // (4 * width)
    if rows <= most:
        return rows
    return max(t for t in range(16, most + 1, 16) if rows % t == 0)


def _flat_aligned(v, align):
    v = v.reshape(-1)
    return jnp.pad(v, (0, -v.shape[0] % align))


def _adamw(name, w, g, m, v, tile):
    def fn(i, wb, gb, mb, vb):
        m2 = ADAM_B1 * mb + (1.0 - ADAM_B1) * gb
        v2 = ADAM_B2 * vb + (1.0 - ADAM_B2) * (gb * gb)
        m_hat = m2 / (1.0 - ADAM_B1 ** ADAM_STEP)
        v_hat = v2 / (1.0 - ADAM_B2 ** ADAM_STEP)
        return -ADAM_LR * (m_hat / (jnp.sqrt(v_hat) + ADAM_EPS) + ADAM_WD * wb), m2, v2

    return _rows(name, fn, [(w, "row"), (g, "row"), (m, "row"), (v, "row")], [(w.shape, F32, "row")] * 3,
                 rows=w.shape[0], tile=tile)


def _as_2d(a):
    return a.reshape(-1, a.shape[-1])


def _adam_tile(rows):
    for t in (256, 184, 128, 64, 32, 16, 8):
        if rows % t == 0:
            return t
    return rows


def kernel(x, p, g_pre, w_in, s5_a_re, s5_a_im, s5_log_dt, s5_b_re, s5_b_im, s5_c_re, s5_c_im, s5_d, w_glu, w_bs, conv_w, conv_b, lru_w_a, lru_b_a, lru_w_x, lru_b_x, lru_lambda, w_bl, w_out, g_post, w_ple, w_ple_gate, loss_target, m_g_pre, m_w_in, m_s5_a_re, m_s5_a_im, m_s5_log_dt, m_s5_b_re, m_s5_b_im, m_s5_c_re, m_s5_c_im, m_s5_d, m_w_glu, m_w_bs, m_conv_w, m_conv_b, m_lru_w_a, m_lru_b_a, m_lru_w_x, m_lru_b_x, m_lru_lambda, m_w_bl, m_w_out, m_g_post, m_w_ple, m_w_ple_gate, v_g_pre, v_w_in, v_s5_a_re, v_s5_a_im, v_s5_log_dt, v_s5_b_re, v_s5_b_im, v_s5_c_re, v_s5_c_im, v_s5_d, v_w_glu, v_w_bs, v_conv_w, v_conv_b, v_lru_w_a, v_lru_b_a, v_lru_w_x, v_lru_b_x, v_lru_lambda, v_w_bl, v_w_out, v_g_post, v_w_ple, v_w_ple_gate):
    wts = dict(g_pre=g_pre, w_in=w_in, s5_a_re=s5_a_re, s5_a_im=s5_a_im, s5_log_dt=s5_log_dt, s5_b_re=s5_b_re,
               s5_b_im=s5_b_im, s5_c_re=s5_c_re, s5_c_im=s5_c_im, s5_d=s5_d, w_glu=w_glu, w_bs=w_bs, conv_w=conv_w,
               conv_b=conv_b, lru_w_a=lru_w_a, lru_b_a=lru_b_a, lru_w_x=lru_w_x, lru_b_x=lru_b_x, lru_lambda=lru_lambda,
               w_bl=w_bl, w_out=w_out, g_post=g_post, w_ple=w_ple, w_ple_gate=w_ple_gate)
    mom1 = dict(g_pre=m_g_pre, w_in=m_w_in, s5_a_re=m_s5_a_re, s5_a_im=m_s5_a_im, s5_log_dt=m_s5_log_dt, s5_b_re=m_s5_b_re,
                s5_b_im=m_s5_b_im, s5_c_re=m_s5_c_re, s5_c_im=m_s5_c_im, s5_d=m_s5_d, w_glu=m_w_glu, w_bs=m_w_bs,
                conv_w=m_conv_w, conv_b=m_conv_b, lru_w_a=m_lru_w_a, lru_b_a=m_lru_b_a, lru_w_x=m_lru_w_x, lru_b_x=m_lru_b_x,
                lru_lambda=m_lru_lambda, w_bl=m_w_bl, w_out=m_w_out, g_post=m_g_post, w_ple=m_w_ple, w_ple_gate=m_w_ple_gate)
    mom2 = dict(g_pre=v_g_pre, w_in=v_w_in, s5_a_re=v_s5_a_re, s5_a_im=v_s5_a_im, s5_log_dt=v_s5_log_dt, s5_b_re=v_s5_b_re,
                s5_b_im=v_s5_b_im, s5_c_re=v_s5_c_re, s5_c_im=v_s5_c_im, s5_d=v_s5_d, w_glu=v_w_glu, w_bs=v_w_bs,
                conv_w=v_conv_w, conv_b=v_conv_b, lru_w_a=v_lru_w_a, lru_b_a=v_lru_b_a, lru_w_x=v_lru_w_x, lru_b_x=v_lru_b_x,
                lru_lambda=v_lru_lambda, w_bl=v_w_bl, w_out=v_w_out, g_post=v_g_post, w_ple=v_w_ple, w_ple_gate=v_w_ple_gate)
    names = list(wts)

    by_rows, by_cols = ("w_bl", "w_out", "w_ple_gate"), ("w_glu", "w_bs", "w_ple")
    two_d = lambda k: wts[k].astype(BF16).reshape(-1, wts[k].shape[2])
    sent = [two_d("w_in"), jnp.concatenate([two_d(k) for k in by_rows]), jnp.concatenate([two_d(k) for k in by_cols]),
            wts["conv_w"].reshape(-1, wts["conv_w"].shape[2])]
    g_in, g_rows, g_cols, g_conv = _gather_chips("gather_weights", sent, via_sibling=True)
    whole = {"conv_w": jnp.transpose(g_conv.reshape((4,) + wts["conv_w"].shape), (1, 2, 0, 3)).reshape(DEPTH, CONV_WIDTH, -1)}
    off = 0
    for k in by_rows:
        dp, r, _ = wts[k].shape
        piece = g_rows[:, off:off + dp * r].reshape(4, dp, r, -1)
        whole[k] = jnp.transpose(piece, (1, 0, 2, 3)).reshape(dp, 4 * r, -1)
        off += dp * r
    off = 0
    for k in by_cols:
        dp, r, cs = wts[k].shape
        piece = g_cols[:, off:off + dp * r].reshape(4, dp, r, cs)
        whole[k] = jnp.transpose(piece, (1, 2, 0, 3)).reshape(dp, r, 4 * cs)
        off += dp * r
    layers = []
    for i in range(DEPTH):
        wl = {k: whole[k][i] for k in BIG if k != "w_in"}
        wl["w_in"] = [(g_in, ("part", j, i, wts["w_in"].shape[1])) for j in range(4)]
        wl.update({k: wts[k][i] for k in SMALL})
        layers.append(wl)

    loss, grad_x, grads = _local_step(x[0], p[:, 0], layers, loss_target[0])
    loss = lax.psum(loss, ("x", "y", "c"))

    me = 2 * lax.axis_index("x") + lax.axis_index("y")
    c = lax.axis_index("c")
    by_rows, by_cols = ("w_bl", "w_out", "w_ple_gate"), ("w_glu", "w_bs", "w_ple")
    rows_of = lambda k, i, j: grads[i][k][j * (grads[i][k].shape[0] // 4):(j + 1) * (grads[i][k].shape[0] // 4)]
    cols_of = lambda k, i, j: grads[i][k][:, j * (grads[i][k].shape[1] // 4):(j + 1) * (grads[i][k].shape[1] // 4)]
    layer_range = range(DEPTH)
    packs = [
        jnp.stack([jnp.concatenate([grads[i]["w_in"][j] for i in layer_range]) for j in range(4)]),
        jnp.stack([jnp.concatenate([rows_of(k, i, j) for k in by_rows for i in layer_range]) for j in range(4)]),
        jnp.stack([jnp.concatenate([cols_of(k, i, j) for k in by_cols for i in layer_range]) for j in range(4)]),
    ]
    small_names = SMALL + ("conv_w",)
    small_shapes = [(DEPTH,) + grads[0][k].shape for k in small_names]
    small_flat = jnp.concatenate([_flat_aligned(jnp.stack([grads[i][k] for i in layer_range]), WIDE) for k in small_names])
    n_small = small_flat.shape[0]
    small_q = -(-n_small // (4 * 32 * LANES)) * 32 * LANES
    packs.append(jnp.pad(small_flat, (0, 4 * small_q - n_small)).reshape(4, small_q // LANES, LANES))

    gots = _rs_sibling(packs)
    a32s, a16s = [], []
    for pk, got in zip(packs, gots):
        half, width = got.shape[1], got.shape[2]
        mine = lax.dynamic_slice_in_dim(pk, c * half, half, axis=1)

        def f_add1(i, a, b):
            s = a + b
            return s, s

        a32, a16 = _rows("rs_add1", f_add1, [(mine.reshape(4 * half, width), "row"), (got.reshape(4 * half, width), "row")],
                         [((4 * half, width), F32, "row"), ((4 * half, width), BF16, "row")], rows=4 * half,
                         tile=_pack_tile(4 * half, width))
        a32s.append(a32.reshape(4, half, width))
        a16s.append(a16.reshape(4, half, width))
    got3s = _rs_chips(a16s)
    red_halves = []
    for a32, got3 in zip(a32s, got3s):
        half, width = a32.shape[1], a32.shape[2]
        own = lax.dynamic_index_in_dim(a32, me, 0, keepdims=False)

        def f_add2(i, o, g0, g1, g2):
            return (((o + g0) + g1) + g2,)

        red_halves.append(_rows("rs_add2", f_add2, [(own, "row")] + [(got3[k], "row") for k in range(3)],
                                [((half, width), F32, "row")], rows=half, tile=_pack_tile(half, width))[0])
    reds = _swap_halves(red_halves)
    small_red = _gather_chips("gather_small", [reds[3]], via_sibling=False)[0].reshape(-1)[:n_small]

    grad_out = {"w_in": reds[0].reshape(wts["w_in"].shape)}
    for red, ks in ((reds[1], by_rows), (reds[2], by_cols)):
        off = 0
        for k in ks:
            n = wts[k].shape[0] * wts[k].shape[1]
            grad_out[k] = red[off:off + n].reshape(wts[k].shape)
            off += n
    small_out = dict(zip(small_names, _unpack(small_red, small_shapes, align=WIDE)))
    grad_out.update({k: small_out[k] for k in SMALL})
    grad_out["conv_w"] = lax.dynamic_slice_in_dim(small_out["conv_w"], me * wts["conv_w"].shape[2], wts["conv_w"].shape[2], axis=2)
    delta, new_m, new_v = {}, {}, {}
    for k in BIG + SMALL:
        w2 = _as_2d(wts[k])
        res = _adamw("adamw_" + k, w2, _as_2d(grad_out[k]), _as_2d(mom1[k]), _as_2d(mom2[k]), _adam_tile(w2.shape[0]))
        delta[k], new_m[k], new_v[k] = [r.reshape(wts[k].shape) for r in res]
    return (loss, grad_x[None], *[grad_out[k] for k in names], *[delta[k] for k in names],
            *[new_m[k] for k in names], *[new_v[k] for k in names])
```

```python
import jax
import jax.numpy as jnp
from jax import lax
from jax.experimental import pallas as pl
from jax.experimental.pallas import tpu as pltpu

F32 = jnp.float32
BF16 = jnp.bfloat16
MESH = pl.DeviceIdType.MESH

DEPTH = 2
D_MODEL = 1024
NORM_EPS = 1e-6
S5_WIDTH = 512
S5_GROUPS = 32
S5_GROUP = 16
S5_STATE = 64
LRU_WIDTH = 1280
LRU_HEADS = 10
LRU_HEAD_DIM = 128
LRU_C = 8.0
CONV_WIDTH = 4
PLE_DIM = 256
IN_WIDTHS = (S5_WIDTH, S5_WIDTH, LRU_WIDTH, LRU_WIDTH, D_MODEL, D_MODEL)
IN_OFFSETS = (0, 512, 1024, 2304, 3584, 4608, 5632)
IN_SLOT = 5632 // 4
ADAM_LR = 0.001
ADAM_B1 = 0.9
ADAM_B2 = 0.999
ADAM_EPS = 1e-08
ADAM_WD = 0.01
ADAM_STEP = 10

SUBLANES = 8
LANES = 128
S5_HALF_IN = S5_WIDTH // 2
S5_CPLX = S5_GROUPS * S5_STATE
S5_HALF_CPLX = S5_CPLX // 2
S5_LANES = 2 * S5_CPLX
VMEM_LIMIT = 48 * 2 ** 20
ROW_TILE = 256
WIDE_TILE = 512


def _sigmoid(x):
    return 0.5 * jnp.tanh(0.5 * x) + 0.5


def _gelu_parts(x):
    k = 0.7978845608028654
    t = jnp.tanh(k * (x + 0.044715 * x * x * x))
    val = 0.5 * x * (1.0 + t)
    grad = 0.5 * (1.0 + t) + 0.5 * x * (1.0 - t * t) * k * (1.0 + 3.0 * 0.044715 * x * x)
    return val, grad


def _nn(a, w):
    return jnp.dot(a.astype(BF16), w.astype(BF16), preferred_element_type=F32)


def _nt(a, w):
    return lax.dot_general(a.astype(BF16), w.astype(BF16), (((1,), (1,)), ((), ())), preferred_element_type=F32)


def _tn(a, b):
    return lax.dot_general(a.astype(BF16), b.astype(BF16), (((0,), (0,)), ((), ())), preferred_element_type=F32)


def _heads(op, a, w):
    d = LRU_HEAD_DIM
    return jnp.concatenate([op(a[:, h * d:(h + 1) * d], w[h]) for h in range(LRU_HEADS)], axis=1)


def _heads_tn(a, b):
    d = LRU_HEAD_DIM
    return jnp.stack([_tn(a[:, h * d:(h + 1) * d], b[:, h * d:(h + 1) * d]) for h in range(LRU_HEADS)], axis=0)


def _rows_before(x, halo, s):
    main = pltpu.roll(x, s, 0)
    head = pltpu.roll(jnp.concatenate([halo, x[0:SUBLANES]], axis=0), s, 0)[SUBLANES:2 * SUBLANES]
    return jnp.concatenate([head, main[SUBLANES:]], axis=0)


def _rows_after(x, halo, s):
    n = x.shape[0]
    main = pltpu.roll(x, n - s, 0)
    tail = pltpu.roll(jnp.concatenate([x[n - SUBLANES:], halo], axis=0), 2 * SUBLANES - s, 0)[0:SUBLANES]
    return jnp.concatenate([main[:n - SUBLANES], tail], axis=0)


def _rows(name, fn, ins, outs, *, rows, tile):
    tile = min(tile, rows)
    n = rows // tile
    assert n * tile == rows, (name, rows, tile)
    in_specs = []
    for arr, kind in ins:
        halo = SUBLANES * (4 // arr.dtype.itemsize)
        per, last = tile // halo, rows // halo - 1
        if isinstance(kind, tuple):
            _, j, k, r = kind
            in_specs.append(pl.BlockSpec((None, r, arr.shape[2]), lambda i, j=j, k=k: (j, k, 0)))
        elif kind in ("row", "raw"):
            in_specs.append(pl.BlockSpec((tile, arr.shape[1]), lambda i: (i, 0)))
        elif kind == "prev":
            in_specs.append(pl.BlockSpec((halo, arr.shape[1]), lambda i, per=per: (jnp.maximum(i * per - 1, 0), 0)))
        elif kind == "next":
            in_specs.append(pl.BlockSpec((halo, arr.shape[1]),
                                         lambda i, per=per, last=last: (jnp.minimum((i + 1) * per, last), 0)))
        else:
            in_specs.append(pl.BlockSpec(arr.shape, lambda i, nd=arr.ndim: (0,) * nd))
    out_shape, out_specs = [], []
    for shape, dtype, kind in outs:
        out_shape.append(jax.ShapeDtypeStruct(shape, dtype))
        if kind == "row":
            out_specs.append(pl.BlockSpec((tile, shape[1]), lambda i: (i, 0)))
        else:
            out_specs.append(pl.BlockSpec(shape, lambda i, nd=len(shape): (0,) * nd))
    n_in = len(ins)

    def load(ref, kind):
        v = ref[...]
        if kind in ("row", "prev", "next"):
            v = v.astype(F32)
        if kind == "prev":
            v = v[v.shape[0] - SUBLANES:]
        if kind == "next":
            v = v[:SUBLANES]
        return v

    def body(*refs):
        i = pl.program_id(0)
        vals = fn(i, *[load(r, kind) for r, (_, kind) in zip(refs[:n_in], ins)])
        assert len(vals) == len(outs), name
        for r, v, (_, _, kind) in zip(refs[n_in:], vals, outs):
            if kind == "row":
                r[...] = v.astype(r.dtype)
            else:
                @pl.when(i == 0)
                def _():
                    r[...] = jnp.zeros_like(r)

                r[...] += v.astype(r.dtype)

    return pl.pallas_call(
        body, name=name, grid=(n,), in_specs=in_specs, out_specs=out_specs, out_shape=out_shape,
        compiler_params=pltpu.CompilerParams(dimension_semantics=("arbitrary",), vmem_limit_bytes=VMEM_LIMIT),
    )(*[a for a, _ in ins])


def _s5_discretise(are, aim, ldt, bre, bim):
    dt = jnp.exp(ldt)
    er = jnp.exp(are * dt)
    abr = er * jnp.cos(aim * dt)
    abi = er * jnp.sin(aim * dt)
    den = are * are + aim * aim
    zr = ((abr - 1.0) * are + abi * aim) / den
    zi = (abi * are - (abr - 1.0) * aim) / den
    return abr, abi, zr * bre - zi * bim, zr * bim + zi * bre


def _s5_prep(are, aim, ldt, bre, bim):
    def body(a, b, c, d, e, o0, o1, o2, o3):
        r = _s5_discretise(a[...], b[...], c[...], d[...], e[...])
        o0[...], o1[...], o2[...], o3[...] = r

    sd = jax.ShapeDtypeStruct(are.shape, F32)
    return pl.pallas_call(body, name="s5_prep", out_shape=[sd] * 4)(are, aim, ldt, bre, bim)


def _s5_prep_bwd(are, aim, ldt, bre, bim, cts):
    def body(a, b, c, d, e, c0, c1, c2, c3, o0, o1, o2, o3, o4):
        _, vjp = jax.vjp(_s5_discretise, a[...], b[...], c[...], d[...], e[...])
        r = vjp((c0[...], c1[...], c2[...], c3[...]))
        o0[...], o1[...], o2[...], o3[...], o4[...] = r

    sd = jax.ShapeDtypeStruct(are.shape, F32)
    return pl.pallas_call(body, name="s5_prep_bwd", out_shape=[sd] * 5)(are, aim, ldt, bre, bim, *cts)


def _s5_consts(abr, abi, seg):
    shape = (SUBLANES, S5_CPLX)
    assert seg & (seg - 1) == 0 and seg % SUBLANES == 0, seg

    def body(ar_ref, ai_ref, f_ref, b_ref):
        def cmul(p, q):
            return (p[0] * q[0] - p[1] * q[1], p[0] * q[1] + p[1] * q[0])

        row = lax.broadcasted_iota(jnp.int32, shape, 0)
        a1 = (jnp.broadcast_to(ar_ref[...], shape), jnp.broadcast_to(ai_ref[...], shape))
        squares = [a1]
        while 1 << (len(squares) - 1) < 4 * seg:
            squares.append(cmul(squares[-1], squares[-1]))
        nb = seg.bit_length() - 1
        fwd, rev = [], []
        for k, a in ((1, squares[nb]), (2, squares[nb + 1]), (4, squares[nb + 2])):
            fwd += [jnp.where(row >= k, a[0], 0.0), jnp.where(row >= k, a[1], 0.0)]
            rev += [jnp.where(row <= 7 - k, a[0], 0.0), jnp.where(row <= 7 - k, -a[1], 0.0)]
        fwd += [a1[0], a1[1]]
        rev += [a1[0], -a1[1]]
        e = lax.broadcasted_iota(jnp.int32, (seg, S5_CPLX), 0) + 1
        wide = lambda v: jnp.broadcast_to(v[0:1, :], (seg, S5_CPLX))
        pr, pi = jnp.ones((seg, S5_CPLX), F32), jnp.zeros((seg, S5_CPLX), F32)
        for b in range(nb + 1):
            sr, si = wide(squares[b][0]), wide(squares[b][1])
            bit = ((e >> b) & 1) == 1
            pr, pi = jnp.where(bit, pr * sr - pi * si, pr), jnp.where(bit, pr * si + pi * sr, pi)
        f_ref[...] = jnp.concatenate(fwd + [pr, pi], axis=0)
        b_ref[...] = jnp.concatenate(rev + [pr, -pi], axis=0)

    sd = jax.ShapeDtypeStruct((8 * SUBLANES + 2 * seg, S5_CPLX), F32)
    return pl.pallas_call(body, name="s5_consts", out_shape=[sd, sd])(abr, abi)


S5_TILES = S5_LANES // LANES
S5_HALF_TILES = S5_TILES // 2


def _s5_tile_index(q):
    re = (q // 8) * S5_HALF_TILES + (q % 8)
    return re, re + S5_HALF_TILES // 2


def _lanes_of(ref, first, count):
    return jnp.concatenate([ref[j] for j in range(first, first + count)], axis=1)


def _to_lane_tiles(ref, first, value):
    for j in range(value.shape[1] // LANES):
        ref[first + j] = value[:, j * LANES:(j + 1) * LANES]


def _time_perm(tile, transpose=False):
    seg = tile // SUBLANES
    rho = lax.broadcasted_iota(jnp.int32, (tile, tile), 1 if transpose else 0)
    t = lax.broadcasted_iota(jnp.int32, (tile, tile), 0 if transpose else 1)
    return (t == (rho & (SUBLANES - 1)) * seg + (rho >> 3)).astype(BF16)


def _reorder(perm, x):
    return jnp.dot(perm, x, preferred_element_type=F32)


def _s5_scan(s_ref, sc_ref, carry_ref, tile, reverse):
    seg = tile // SUBLANES
    group = 4
    edge = 0 if reverse else SUBLANES - 1
    row = lax.broadcasted_iota(jnp.int32, (SUBLANES, LANES), 0)
    order = range(seg - 1, -1, -1) if reverse else range(seg)
    rows_of = lambda k: pl.ds(k * SUBLANES, SUBLANES)
    base = 8 * SUBLANES

    for q0 in range(0, S5_CPLX // LANES, group):
        qs = list(range(q0, q0 + group))
        tiles = [_s5_tile_index(q) for q in qs]
        cst = lambda k, q: sc_ref[k * SUBLANES:(k + 1) * SUBLANES, q * LANES:(q + 1) * LANES]
        state = [(jnp.zeros((SUBLANES, LANES), F32), jnp.zeros((SUBLANES, LANES), F32)) for _ in qs]
        mult = [(cst(6, q), cst(7, q)) for q in qs]
        for k in order:
            for j, (re, im) in enumerate(tiles):
                ar, ai = mult[j]
                xr, xi = state[j]
                nr = ar * xr - ai * xi + s_ref[re, rows_of(k), :]
                ni = ar * xi + ai * xr + s_ref[im, rows_of(k), :]
                s_ref[re, rows_of(k), :] = nr
                s_ref[im, rows_of(k), :] = ni
                state[j] = (nr, ni)
        start = []
        for j, (q, (re, im)) in enumerate(zip(qs, tiles)):
            er, ei = state[j]
            shift1 = SUBLANES - 1 if reverse else 1
            dr = jnp.where(row == SUBLANES - 1 - edge, carry_ref[re], pltpu.roll(er, shift1, 0))
            di = jnp.where(row == SUBLANES - 1 - edge, carry_ref[im], pltpu.roll(ei, shift1, 0))
            for c, sh in ((0, 1), (2, 2), (4, 4)):
                shift = SUBLANES - sh if reverse else sh
                ar, ai = cst(c, q), cst(c + 1, q)
                sr, si = pltpu.roll(dr, shift, 0), pltpu.roll(di, shift, 0)
                dr, di = dr + ar * sr - ai * si, di + ar * si + ai * sr
            start.append((dr, di))
        for k in order:
            t = seg - 1 - k if reverse else k
            for j, (q, (re, im)) in enumerate(zip(qs, tiles)):
                lanes = slice(q * LANES, (q + 1) * LANES)
                pr = jnp.broadcast_to(sc_ref[base + t:base + t + 1, lanes], (SUBLANES, LANES))
                pi = jnp.broadcast_to(sc_ref[base + seg + t:base + seg + t + 1, lanes], (SUBLANES, LANES))
                cr, ci = start[j]
                xr = s_ref[re, rows_of(k), :] + pr * cr - pi * ci
                xi = s_ref[im, rows_of(k), :] + pr * ci + pi * cr
                s_ref[re, rows_of(k), :] = xr
                s_ref[im, rows_of(k), :] = xi
                if k == order[-1]:
                    carry_ref[re] = jnp.broadcast_to(xr[edge:edge + 1, :], (SUBLANES, LANES))
                    carry_ref[im] = jnp.broadcast_to(xi[edge:edge + 1, :], (SUBLANES, LANES))


def _s5_fwd(u, bd, cdt, dskip, sc, *, rows, tile):
    n = rows // tile

    def body(u_ref, bd_ref, cdt_ref, d_ref, sc_ref, y_ref, s_ref, carry_ref):
        @pl.when(pl.program_id(0) == 0)
        def _():
            carry_ref[...] = jnp.zeros_like(carry_ref)

        ub = _reorder(_time_perm(tile), u_ref[...].astype(BF16)).astype(BF16)
        for h in range(2):
            _to_lane_tiles(s_ref, h * S5_HALF_TILES, jnp.dot(ub[:, h * S5_HALF_IN:(h + 1) * S5_HALF_IN], bd_ref[h],
                                                             preferred_element_type=F32))
        _s5_scan(s_ref, sc_ref, carry_ref, tile, reverse=False)
        ys = [_nt(_lanes_of(s_ref, h * S5_HALF_TILES, S5_HALF_TILES), cdt_ref[h]) for h in range(2)]
        y = _reorder(_time_perm(tile, transpose=True), jnp.concatenate(ys, axis=1).astype(BF16))
        y_ref[...] = y + d_ref[...] * u_ref[...]

    full = lambda a: pl.BlockSpec(a.shape, lambda i, nd=a.ndim: (0,) * nd)
    return pl.pallas_call(
        body, name="s5_fwd", grid=(n,),
        in_specs=[pl.BlockSpec((tile, S5_WIDTH), lambda i: (i, 0)), full(bd), full(cdt), full(dskip), full(sc)],
        out_specs=[pl.BlockSpec((tile, S5_WIDTH), lambda i: (i, 0)),
                   pl.BlockSpec((S5_TILES, tile, LANES), lambda i: (0, i, 0))],
        out_shape=[jax.ShapeDtypeStruct((rows, S5_WIDTH), F32), jax.ShapeDtypeStruct((S5_TILES, rows, LANES), F32)],
        scratch_shapes=[pltpu.VMEM((S5_TILES, SUBLANES, LANES), F32)],
        compiler_params=pltpu.CompilerParams(dimension_semantics=("arbitrary",), vmem_limit_bytes=VMEM_LIMIT),
    )(u, bd, cdt, dskip, sc)


def _s5_bwd(dy, s, u, bd, cdt, dskip, sc, *, rows, tile):
    n = rows // tile
    hc = 2 * S5_HALF_CPLX
    per8 = tile // SUBLANES
    quarter = S5_HALF_TILES // 2

    def body(dy_ref, s_ref, sp_ref, u_ref, bd_ref, cdt_ref, d_ref, sc_ref,
             du_ref, dbd_ref, dcdt_ref, dd_ref, da_ref, g_ref, carry_ref):
        i = pl.program_id(0)

        @pl.when(i == 0)
        def _():
            carry_ref[...] = jnp.zeros_like(carry_ref)
            dbd_ref[...] = jnp.zeros_like(dbd_ref)
            dcdt_ref[...] = jnp.zeros_like(dcdt_ref)
            dd_ref[...] = jnp.zeros_like(dd_ref)
            da_ref[...] = jnp.zeros_like(da_ref)

        dy = dy_ref[...]
        u = u_ref[...]
        perm = _time_perm(tile)
        dyb = _reorder(perm, dy.astype(BF16)).astype(BF16)
        ub = _reorder(perm, u.astype(BF16)).astype(BF16)
        for h in range(2):
            _to_lane_tiles(g_ref, h * S5_HALF_TILES, jnp.dot(dyb[:, h * S5_HALF_IN:(h + 1) * S5_HALF_IN], cdt_ref[h],
                                                             preferred_element_type=F32))
        _s5_scan(g_ref, sc_ref, carry_ref, tile, reverse=True)
        dus = []
        for h in range(2):
            gb = _lanes_of(g_ref, h * S5_HALF_TILES, S5_HALF_TILES).astype(BF16)
            sb = _lanes_of(s_ref, h * S5_HALF_TILES, S5_HALF_TILES).astype(BF16)
            dus.append(_nt(gb, bd_ref[h]))
            dbd_ref[h] += _tn(ub[:, h * S5_HALF_IN:(h + 1) * S5_HALF_IN], gb)
            dcdt_ref[h] += _tn(dyb[:, h * S5_HALF_IN:(h + 1) * S5_HALF_IN], sb)
        du = _reorder(_time_perm(tile, transpose=True), jnp.concatenate(dus, axis=1).astype(BF16))
        du_ref[...] = (du + d_ref[...] * dy).astype(du_ref.dtype)
        dd_ref[...] += jnp.sum(dy * u, axis=0, keepdims=True)

        not_first = (i < n - 1).astype(F32)
        row = lax.broadcasted_iota(jnp.int32, (SUBLANES, quarter * LANES), 0)

        def step_before(first):
            cur = _lanes_of(s_ref, first, quarter)
            before_tile = _lanes_of(sp_ref, first, quarter)[SUBLANES - 1:SUBLANES, :] * not_first
            head = jnp.where(row == 0, before_tile, pltpu.roll(cur[tile - SUBLANES:], 1, 0))
            return jnp.concatenate([head, cur[:tile - SUBLANES]], axis=0)

        for h in range(2):
            re, im = h * S5_HALF_TILES, h * S5_HALF_TILES + quarter
            ssr = step_before(re)
            ssi = step_before(im)
            gr = _lanes_of(g_ref, re, quarter)
            gi = _lanes_of(g_ref, im, quarter)
            lanes = slice(h * S5_HALF_CPLX, (h + 1) * S5_HALF_CPLX)
            da_ref[0:1, lanes] += jnp.sum(ssr * gr + ssi * gi, axis=0, keepdims=True)
            da_ref[1:2, lanes] += jnp.sum(ssr * gi - ssi * gr, axis=0, keepdims=True)

    full = lambda a: pl.BlockSpec(a.shape, lambda i, nd=a.ndim: (0,) * nd)
    rev = lambda i: (n - 1 - i, 0)
    wshape = (2, S5_HALF_IN, hc)
    return pl.pallas_call(
        body, name="s5_bwd", grid=(n,),
        in_specs=[pl.BlockSpec((tile, S5_WIDTH), rev), pl.BlockSpec((S5_TILES, tile, LANES), lambda i: (0, n - 1 - i, 0)),
                  pl.BlockSpec((S5_TILES, SUBLANES, LANES), lambda i: (0, jnp.maximum((n - 1 - i) * per8 - 1, 0), 0)),
                  pl.BlockSpec((tile, S5_WIDTH), rev), full(bd), full(cdt), full(dskip), full(sc)],
        out_specs=[pl.BlockSpec((tile, S5_WIDTH), rev),
                   pl.BlockSpec(wshape, lambda i: (0, 0, 0)), pl.BlockSpec(wshape, lambda i: (0, 0, 0)),
                   pl.BlockSpec((1, S5_WIDTH), lambda i: (0, 0)), pl.BlockSpec((SUBLANES, S5_CPLX), lambda i: (0, 0))],
        out_shape=[jax.ShapeDtypeStruct((rows, S5_WIDTH), BF16), jax.ShapeDtypeStruct(wshape, F32),
                   jax.ShapeDtypeStruct(wshape, F32), jax.ShapeDtypeStruct((1, S5_WIDTH), F32),
                   jax.ShapeDtypeStruct((SUBLANES, S5_CPLX), F32)],
        scratch_shapes=[pltpu.VMEM((S5_TILES, tile, LANES), F32), pltpu.VMEM((S5_TILES, SUBLANES, LANES), F32)],
        compiler_params=pltpu.CompilerParams(dimension_semantics=("arbitrary",), vmem_limit_bytes=VMEM_LIMIT),
    )(dy, s, s, u, bd, cdt, dskip, sc)


def _s5_block_diag(parts):
    v = jnp.stack(parts, axis=2).reshape(2, 16, S5_GROUP, 2, S5_STATE)
    eye = jnp.eye(16, dtype=v.dtype)
    return jnp.einsum("hgcpn,gk->hgcpkn", v, eye).reshape(2, S5_HALF_IN, 2 * S5_HALF_CPLX)


def _s5_block_diag_extract(m):
    v = m.reshape(2, 16, S5_GROUP, 2, 16, S5_STATE)
    d = jnp.diagonal(v, axis1=1, axis2=4)
    d = jnp.transpose(d, (2, 0, 4, 1, 3)).reshape(2, S5_GROUPS, S5_GROUP, S5_STATE)
    return d[0], d[1]


def _cplx_to_lanes(v):
    return v.reshape(1, S5_CPLX)


def _lru_scan_fwd(a, b, *, rows, tile):
    n = rows // tile
    nblk = tile // SUBLANES
    group = 5

    def body(a_ref, b_ref, h_ref, carry_ref):
        @pl.when(pl.program_id(0) == 0)
        def _():
            carry_ref[...] = jnp.zeros_like(carry_ref)

        row = lax.broadcasted_iota(jnp.int32, (SUBLANES, LANES), 0)
        for q0 in range(0, LRU_WIDTH // LANES, group):
            offs = [q * LANES for q in range(q0, q0 + group)]

            def blk(t, carry, offs=offs):
                r0 = pl.multiple_of(t * SUBLANES, SUBLANES)
                new = []
                for j, o in enumerate(offs):
                    av = a_ref[pl.ds(r0, SUBLANES), o:o + LANES]
                    xv = b_ref[pl.ds(r0, SUBLANES), o:o + LANES]
                    for sh in (1, 2, 4):
                        m = row >= sh
                        xs = pltpu.roll(xv, sh, 0)
                        asft = pltpu.roll(av, sh, 0)
                        xv = xv + jnp.where(m, av * xs, 0.0)
                        av = jnp.where(m, av * asft, av)
                    hv = xv + av * carry[j]
                    h_ref[pl.ds(r0, SUBLANES), o:o + LANES] = hv
                    new.append(jnp.broadcast_to(hv[SUBLANES - 1:SUBLANES, :], (SUBLANES, LANES)))
                return tuple(new)

            carry = lax.fori_loop(0, nblk, blk, tuple(carry_ref[:, o:o + LANES] for o in offs), unroll=2)
            for j, o in enumerate(offs):
                carry_ref[:, o:o + LANES] = carry[j]

    spec = pl.BlockSpec((tile, LRU_WIDTH), lambda i: (i, 0))
    return pl.pallas_call(
        body, name="lru_scan_fwd", grid=(n,), in_specs=[spec, spec], out_specs=spec,
        out_shape=jax.ShapeDtypeStruct((rows, LRU_WIDTH), F32),
        scratch_shapes=[pltpu.VMEM((SUBLANES, LRU_WIDTH), F32)],
        compiler_params=pltpu.CompilerParams(dimension_semantics=("arbitrary",), vmem_limit_bytes=VMEM_LIMIT),
    )(a, b)


def _lru_scan_bwd(dh, a, *, rows, tile):
    n = rows // tile
    nblk = tile // SUBLANES
    group = 5

    def body(dh_ref, a_ref, g_ref, cg_ref, ca_ref):
        @pl.when(pl.program_id(0) == 0)
        def _():
            cg_ref[...] = jnp.zeros_like(cg_ref)
            ca_ref[...] = jnp.zeros_like(ca_ref)

        row = lax.broadcasted_iota(jnp.int32, (SUBLANES, LANES), 0)
        for q0 in range(0, LRU_WIDTH // LANES, group):
            offs = [q * LANES for q in range(q0, q0 + group)]

            def blk(t, carry, offs=offs):
                r0 = pl.multiple_of((nblk - 1 - t) * SUBLANES, SUBLANES)
                new = []
                for j, o in enumerate(offs):
                    cg, ca = carry[2 * j], carry[2 * j + 1]
                    araw = a_ref[pl.ds(r0, SUBLANES), o:o + LANES]
                    xv = dh_ref[pl.ds(r0, SUBLANES), o:o + LANES]
                    av = jnp.where(row == SUBLANES - 1, ca, pltpu.roll(araw, SUBLANES - 1, 0))
                    for sh in (1, 2, 4):
                        m = row <= SUBLANES - 1 - sh
                        xs = pltpu.roll(xv, SUBLANES - sh, 0)
                        asft = pltpu.roll(av, SUBLANES - sh, 0)
                        xv = xv + jnp.where(m, av * xs, 0.0)
                        av = jnp.where(m, av * asft, av)
                    gv = xv + av * cg
                    g_ref[pl.ds(r0, SUBLANES), o:o + LANES] = gv
                    new.append(jnp.broadcast_to(gv[0:1, :], (SUBLANES, LANES)))
                    new.append(jnp.broadcast_to(araw[0:1, :], (SUBLANES, LANES)))
                return tuple(new)

            carry0 = tuple(r[:, o:o + LANES] for o in offs for r in (cg_ref, ca_ref))
            carry = lax.fori_loop(0, nblk, blk, carry0, unroll=2)
            for j, o in enumerate(offs):
                cg_ref[:, o:o + LANES] = carry[2 * j]
                ca_ref[:, o:o + LANES] = carry[2 * j + 1]

    spec = pl.BlockSpec((tile, LRU_WIDTH), lambda i: (n - 1 - i, 0))
    return pl.pallas_call(
        body, name="lru_scan_bwd", grid=(n,), in_specs=[spec, spec], out_specs=spec,
        out_shape=jax.ShapeDtypeStruct((rows, LRU_WIDTH), F32),
        scratch_shapes=[pltpu.VMEM((SUBLANES, LRU_WIDTH), F32), pltpu.VMEM((SUBLANES, LRU_WIDTH), F32)],
        compiler_params=pltpu.CompilerParams(dimension_semantics=("arbitrary",), vmem_limit_bytes=VMEM_LIMIT),
    )(dh, a)


def _conv_fwd(i, x, prev, cw, cb):
    prev = prev * (i > 0).astype(F32)
    y = x * cw[3:4, :] + cb
    for s in range(1, CONV_WIDTH):
        y = y + _rows_before(x, prev, s) * cw[3 - s:4 - s, :]
    return y


def _lru_gates(c, wa, ba, wx, bx, lam):
    r = _sigmoid(_heads(_nn, c, wa) + ba)
    ig = _sigmoid(_heads(_nn, c, wx) + bx)
    z = -lam
    sp = jnp.maximum(z, 0.0) + jnp.log(1.0 + jnp.exp(-jnp.abs(z)))
    log_a = -LRU_C * r * sp
    a = jnp.exp(log_a)
    z2 = 2.0 * log_a
    series = -z2 * (1.0 + z2 * (0.5 + z2 * (1.0 / 6.0 + z2 * (1.0 / 24.0 + z2 * (1.0 / 120.0 + z2 / 720.0)))))
    one_minus = jnp.where(z2 > -0.2, series, 1.0 - jnp.exp(z2))
    mult = jnp.sqrt(one_minus)
    return r, ig, sp, a, mult


def _layer_fwd(x, p, w, rows, target=None):
    tile = WIDE_TILE
    d = D_MODEL

    def f_in(i, xb, g, *ws):
        rstd = lax.rsqrt(jnp.mean(xb * xb, axis=-1, keepdims=True) + NORM_EPS)
        hb = (xb * rstd * g).astype(BF16)
        proj = jnp.concatenate([jnp.dot(hb, wj, preferred_element_type=F32) for wj in ws], axis=1)
        return tuple(proj[:, IN_OFFSETS[k]:IN_OFFSETS[k + 1]] for k in range(6)) + (hb,)

    s5x, s5g, lrux, lrug, gs, gl, h = _rows(
        "f_in", f_in, [(x, "row"), (w["g_pre"], "full")] + w["w_in"],
        [((rows, wd), BF16, "row") for wd in IN_WIDTHS] + [((rows, d), BF16, "row")], rows=rows, tile=ROW_TILE)

    ys, st = _s5_fwd(s5x, w["bd"], w["cdt"], w["s5_d"], w["scf"], rows=rows, tile=ROW_TILE)

    def f_s5post(i, ysb, gb, wglu, wbs):
        glv, _ = _gelu_parts(ysb)
        glu = _nn(glv, wglu)
        y2 = glu[:, :S5_WIDTH] * _sigmoid(glu[:, S5_WIDTH:]) * (gb * _sigmoid(gb))
        return (_nn(y2, wbs),)

    (z_s,) = _rows("f_s5post", f_s5post, [(ys, "row"), (s5g, "row"), (w["w_glu"], "full"), (w["w_bs"], "full")],
                   [((rows, d), BF16, "row")], rows=rows, tile=tile)

    def f_gates(i, xb, prev, cw, cb, wa, ba, wx, bx, lam):
        c = _conv_fwd(i, xb, prev, cw, cb)
        _, ig, _, a, mult = _lru_gates(c, wa, ba, wx, bx, lam)
        return a, mult * (ig * c)

    a, b = _rows("f_gates", f_gates,
                 [(lrux, "row"), (lrux, "prev"), (w["conv_w"], "full"), (w["conv_b"], "full"), (w["lru_w_a"], "full"),
                  (w["lru_b_a"], "full"), (w["lru_w_x"], "full"), (w["lru_b_x"], "full"), (w["lru_lambda"], "full")],
                 [((rows, LRU_WIDTH), F32, "row")] * 2, rows=rows, tile=tile)
    hl = _lru_scan_fwd(a, b, rows=rows, tile=min(2 * tile, rows))

    def f_merge(i, hb, lg, zs, gsb, glb, xb, wbl, wout, gpost):
        z_l = _nn(hb * (lg * _sigmoid(lg)), wbl)
        merged = _sigmoid(gsb) * zs + _sigmoid(glb) * z_l
        mix = _nn(merged, wout)
        rstd = lax.rsqrt(jnp.mean(mix * mix, axis=-1, keepdims=True) + NORM_EPS)
        return xb + mix * rstd * gpost, mix, z_l

    x1, mix, z_l = _rows("f_merge", f_merge,
                         [(hl, "row"), (lrug, "row"), (z_s, "row"), (gs, "row"), (gl, "row"), (x, "row"),
                          (w["w_bl"], "full"), (w["w_out"], "full"), (w["g_post"], "full")],
                         [((rows, d), F32, "row"), ((rows, d), BF16, "row"), ((rows, d), BF16, "row")], rows=rows, tile=tile)

    saved = dict(x=x, h=h, s5x=s5x, s5g=s5g, lrux=lrux, lrug=lrug, gs=gs, gl=gl, ys=ys, st=st, a=a, hl=hl, z_s=z_s,
                 z_l=z_l, mix=mix, x1=x1, p=p)
    ple_ins = [(x1, "row"), (p, "row"), (w["w_ple"], "full"), (w["w_ple_gate"], "full")]
    if target is None:
        def f_ple(i, x1b, pb, wple, wpg):
            return (x1b + _nn(pb, wple) * _sigmoid(_nn(x1b, wpg)),)

        return _rows("f_ple", f_ple, ple_ins, [((rows, d), F32, "row")], rows=rows, tile=tile)[0], saved

    def f_ple_loss(i, x1b, pb, wple, wpg, tb):
        e = x1b + _nn(pb, wple) * _sigmoid(_nn(x1b, wpg)) - tb
        return e * (1.0 / D_MODEL), jnp.sum(jnp.sum(e * e, axis=0, keepdims=True), axis=1, keepdims=True)

    return _rows("f_ple_loss", f_ple_loss, ple_ins + [(target, "row")],
                 [((rows, d), F32, "row"), ((1, 1), F32, "acc")], rows=rows, tile=tile), saved


def _layer_bwd(dx2, sv, w, rows):
    tile = WIDE_TILE
    d = D_MODEL
    g = {}

    def b_ple(i, dxb, x1b, pb, wple, wpg):
        pe = _nn(pb, wple)
        sg = _sigmoid(_nn(x1b, wpg))
        dpe = dxb * sg
        dgt = dxb * pe * sg * (1.0 - sg)
        return dxb + _nt(dgt, wpg), _tn(pb, dpe), _tn(x1b, dgt)

    dx1, g["w_ple"], g["w_ple_gate"] = _rows(
        "b_ple", b_ple, [(dx2, "row"), (sv["x1"], "row"), (sv["p"], "row"), (w["w_ple"], "full"), (w["w_ple_gate"], "full")],
        [((rows, d), F32, "row"), ((PLE_DIM, d), F32, "acc"), ((d, d), F32, "acc")], rows=rows, tile=tile)

    def b_merge(i, dxb, mixb, zs, zl, gsb, glb, wout, gpost):
        rstd = lax.rsqrt(jnp.mean(mixb * mixb, axis=-1, keepdims=True) + NORM_EPS)
        nrm = mixb * rstd
        dn = dxb * gpost
        dmix = rstd * (dn - nrm * jnp.mean(dn * nrm, axis=-1, keepdims=True))
        ss, sl = _sigmoid(gsb), _sigmoid(glb)
        merged = ss * zs + sl * zl
        dm = _nt(dmix, wout)
        return (dm * ss, dm * sl, dm * zs * ss * (1.0 - ss), dm * zl * sl * (1.0 - sl),
                _tn(merged, dmix), jnp.sum(dxb * nrm, axis=0, keepdims=True))

    dz_s, dz_l, dgs, dgl, g["w_out"], g["g_post"] = _rows(
        "b_merge", b_merge,
        [(dx1, "row"), (sv["mix"], "row"), (sv["z_s"], "row"), (sv["z_l"], "row"), (sv["gs"], "row"), (sv["gl"], "row"),
         (w["w_out"], "full"), (w["g_post"], "full")],
        [((rows, d), BF16, "row")] * 4 + [((d, d), F32, "acc"), ((1, d), F32, "acc")], rows=rows, tile=tile)

    def b_bl(i, dzl, hb, lg, wbl):
        sl = _sigmoid(lg)
        silu = lg * sl
        dy3 = _nt(dzl, wbl)
        return dy3 * silu, dy3 * hb * sl * (1.0 + lg * (1.0 - sl)), _tn(hb * silu, dzl)

    dh, dlrug, g["w_bl"] = _rows(
        "b_bl", b_bl, [(dz_l, "row"), (sv["hl"], "row"), (sv["lrug"], "row"), (w["w_bl"], "full")],
        [((rows, LRU_WIDTH), F32, "row"), ((rows, LRU_WIDTH), BF16, "row"), ((LRU_WIDTH, d), F32, "acc")], rows=rows, tile=tile)

    gh = _lru_scan_bwd(dh, sv["a"], rows=rows, tile=min(2 * tile, rows))

    def b_gates(i, ghb, hb, hprev, xb, xprev, cw, cb, wa, ba, wx, bx, lam):
        c = _conv_fwd(i, xb, xprev, cw, cb)
        r, ig, sp, a, mult = _lru_gates(c, wa, ba, wx, bx, lam)
        h_before = _rows_before(hb, hprev * (i > 0).astype(F32), 1)
        da = ghb * h_before
        dmult = ghb * ig * c
        dlog_a = da * a - dmult * a * a / mult
        dpre_r = dlog_a * (-LRU_C) * sp * r * (1.0 - r)
        dpre_i = ghb * mult * c * ig * (1.0 - ig)
        dc = ghb * mult * ig + _heads(_nt, dpre_r, wa) + _heads(_nt, dpre_i, wx)
        dlam = jnp.sum(dlog_a * LRU_C * r, axis=0, keepdims=True) * _sigmoid(-lam)
        return (dc, _heads_tn(c, dpre_r), _heads_tn(c, dpre_i), jnp.sum(dpre_r, axis=0, keepdims=True),
                jnp.sum(dpre_i, axis=0, keepdims=True), dlam)

    hshape = (LRU_HEADS, LRU_HEAD_DIM, LRU_HEAD_DIM)
    dc, g["lru_w_a"], g["lru_w_x"], g["lru_b_a"], g["lru_b_x"], g["lru_lambda"] = _rows(
        "b_gates", b_gates,
        [(gh, "row"), (sv["hl"], "row"), (sv["hl"], "prev"), (sv["lrux"], "row"), (sv["lrux"], "prev"),
         (w["conv_w"], "full"), (w["conv_b"], "full"), (w["lru_w_a"], "full"), (w["lru_b_a"], "full"),
         (w["lru_w_x"], "full"), (w["lru_b_x"], "full"), (w["lru_lambda"], "full")],
        [((rows, LRU_WIDTH), BF16, "row"), (hshape, F32, "acc"), (hshape, F32, "acc")] + [((1, LRU_WIDTH), F32, "acc")] * 3,
        rows=rows, tile=tile)

    n_tiles = rows // min(tile, rows)

    def b_conv(i, dcb, dnext, xb, xprev, cw):
        dnext = dnext * (i < n_tiles - 1).astype(F32)
        xprev = xprev * (i > 0).astype(F32)
        dx = dcb * cw[3:4, :]
        dws = [jnp.sum(dcb * xb, axis=0, keepdims=True)]
        for s in range(1, CONV_WIDTH):
            dx = dx + _rows_after(dcb, dnext, s) * cw[3 - s:4 - s, :]
            dws.append(jnp.sum(dcb * _rows_before(xb, xprev, s), axis=0, keepdims=True))
        return dx, jnp.concatenate(dws[::-1], axis=0), jnp.sum(dcb, axis=0, keepdims=True)

    dlrux, g["conv_w"], g["conv_b"] = _rows(
        "b_conv", b_conv, [(dc, "row"), (dc, "next"), (sv["lrux"], "row"), (sv["lrux"], "prev"), (w["conv_w"], "full")],
        [((rows, LRU_WIDTH), BF16, "row"), ((CONV_WIDTH, LRU_WIDTH), F32, "acc"), ((1, LRU_WIDTH), F32, "acc")],
        rows=rows, tile=tile)

    def b_s5post(i, dzs, ysb, gb, wglu, wbs):
        glv, dgelu = _gelu_parts(ysb)
        glu = _nn(glv, wglu)
        ga, gb2 = glu[:, :S5_WIDTH], glu[:, S5_WIDTH:]
        sb = _sigmoid(gb2)
        sg = _sigmoid(gb)
        silu = gb * sg
        y2 = ga * sb * silu
        dy2 = _nt(dzs, wbs)
        dglu = jnp.concatenate([dy2 * sb * silu, dy2 * ga * silu * sb * (1.0 - sb)], axis=1)
        dg = dy2 * ga * sb * sg * (1.0 + gb * (1.0 - sg))
        return _nt(dglu, wglu) * dgelu, dg, _tn(y2, dzs), _tn(glv, dglu)

    dys, ds5g, g["w_bs"], g["w_glu"] = _rows(
        "b_s5post", b_s5post, [(dz_s, "row"), (sv["ys"], "row"), (sv["s5g"], "row"), (w["w_glu"], "full"), (w["w_bs"], "full")],
        [((rows, S5_WIDTH), F32, "row"), ((rows, S5_WIDTH), BF16, "row"), ((S5_WIDTH, d), F32, "acc"),
         ((S5_WIDTH, 2 * S5_WIDTH), F32, "acc")],
        rows=rows, tile=tile)

    ds5x, g["bd"], g["cdt"], g["s5_d"], g["abar"] = _s5_bwd(dys, sv["st"], sv["s5x"], w["bd"], w["cdt"], w["s5_d"],
                                                            w["scb"], rows=rows, tile=ROW_TILE)

    dcomps = [ds5x, ds5g, dlrux, dlrug, dgs, dgl]

    def b_in(i, xb, dx1b, gpre, *rest):
        dproj, ws = jnp.concatenate(rest[:6], axis=1), rest[6:]
        dh = _nt(dproj[:, :IN_SLOT], ws[0])
        for j in range(1, 4):
            dh = dh + _nt(dproj[:, j * IN_SLOT:(j + 1) * IN_SLOT], ws[j])
        rstd = lax.rsqrt(jnp.mean(xb * xb, axis=-1, keepdims=True) + NORM_EPS)
        nrm = xb * rstd
        dn = dh * gpre
        dx = rstd * (dn - nrm * jnp.mean(dn * nrm, axis=-1, keepdims=True))
        return dx1b + dx, jnp.sum(dh * nrm, axis=0, keepdims=True)

    dx, g["g_pre"] = _rows(
        "b_in", b_in, [(sv["x"], "row"), (dx1, "row"), (w["g_pre"], "full")] + [(dcv, "raw") for dcv in dcomps]
        + w["w_in"],
        [((rows, d), F32, "row"), ((1, d), F32, "acc")], rows=rows, tile=ROW_TILE)

    g["w_in"] = []
    for j in range(4):
        lo, hi = j * IN_SLOT, (j + 1) * IN_SLOT
        ks = [k for k in range(6) if IN_OFFSETS[k] < hi and IN_OFFSETS[k + 1] > lo]
        first = IN_OFFSETS[ks[0]]

        def b_win(i, hb, *parts, lo=lo, hi=hi, first=first):
            return (_tn(hb, jnp.concatenate(parts, axis=1)[:, lo - first:hi - first]),)

        g["w_in"].append(_rows("b_win", b_win, [(sv["h"], "raw")] + [(dcomps[k], "raw") for k in ks],
                               [((d, IN_SLOT), F32, "acc")], rows=rows, tile=4 * ROW_TILE)[0])
    return dx, g


SMALL = ("g_pre", "s5_a_re", "s5_a_im", "s5_log_dt", "s5_b_re", "s5_b_im", "s5_c_re", "s5_c_im", "s5_d", "conv_b",
         "lru_w_a", "lru_b_a", "lru_w_x", "lru_b_x", "lru_lambda", "g_post")
BIG = ("w_in", "w_glu", "w_bs", "conv_w", "w_bl", "w_out", "w_ple", "w_ple_gate")


def _bcast_groups(v):
    return jnp.broadcast_to(v[:, None, :], (S5_GROUPS, S5_GROUP, S5_STATE)).reshape(S5_WIDTH, S5_STATE)


def _s5_prep_inputs(wl):
    ldt = jnp.broadcast_to(wl["s5_log_dt"][:, None], (S5_GROUPS, S5_STATE))
    gcn = lambda b: jnp.transpose(b, (0, 2, 1)).reshape(S5_WIDTH, S5_STATE)
    return (_bcast_groups(wl["s5_a_re"]), _bcast_groups(wl["s5_a_im"]), _bcast_groups(ldt), gcn(wl["s5_b_re"]),
            gcn(wl["s5_b_im"]))


def _layer_weights(wl):
    w = {}
    w["w_in"] = [wc if isinstance(wc, tuple) else (wc, "full") for wc in wl["w_in"]]
    for k in ("w_glu", "w_bs", "w_bl", "w_out", "w_ple", "w_ple_gate"):
        w[k] = wl[k]
    w["conv_w"] = wl["conv_w"]
    for k in ("g_pre", "g_post", "s5_d", "conv_b", "lru_b_a", "lru_b_x", "lru_lambda"):
        w[k] = wl[k].reshape(1, -1)
    w["lru_w_a"] = wl["lru_w_a"].astype(BF16)
    w["lru_w_x"] = wl["lru_w_x"].astype(BF16)
    prep_in = _s5_prep_inputs(wl)
    abr, abi, bbr, bbi = _s5_prep(*prep_in)
    w["prep_in"] = prep_in
    shape3 = (S5_GROUPS, S5_GROUP, S5_STATE)
    w["bd"] = _s5_block_diag([bbr.reshape(shape3), bbi.reshape(shape3)]).astype(BF16)
    w["cdt"] = _s5_block_diag([wl["s5_c_re"], -wl["s5_c_im"]]).astype(BF16)
    abr_s = abr.reshape(shape3)[:, 0, :]
    abi_s = abi.reshape(shape3)[:, 0, :]
    w["scf"], w["scb"] = _s5_consts(_cplx_to_lanes(abr_s), _cplx_to_lanes(abi_s), ROW_TILE // SUBLANES)
    return w


def _layer_param_grads(g, w, wl):
    out = {}
    shape3 = (S5_GROUPS, S5_GROUP, S5_STATE)
    dbr, dbi = _s5_block_diag_extract(g["bd"])
    dcr, dci = _s5_block_diag_extract(g["cdt"])
    out["s5_c_re"], out["s5_c_im"] = dcr, -dci
    zeros = jnp.zeros(shape3, F32)
    dar = zeros.at[:, 0, :].set(g["abar"][0].reshape(S5_GROUPS, S5_STATE)).reshape(S5_WIDTH, S5_STATE)
    dai = zeros.at[:, 0, :].set(g["abar"][1].reshape(S5_GROUPS, S5_STATE)).reshape(S5_WIDTH, S5_STATE)
    cts = (dar, dai, dbr.reshape(S5_WIDTH, S5_STATE), dbi.reshape(S5_WIDTH, S5_STATE))
    d_are, d_aim, d_ldt, d_bre, d_bim = _s5_prep_bwd(*w["prep_in"], cts)
    out["s5_a_re"] = d_are.reshape(shape3).sum(axis=1)
    out["s5_a_im"] = d_aim.reshape(shape3).sum(axis=1)
    out["s5_log_dt"] = d_ldt.reshape(shape3).sum(axis=(1, 2))
    out["s5_b_re"] = jnp.transpose(d_bre.reshape(shape3), (0, 2, 1))
    out["s5_b_im"] = jnp.transpose(d_bim.reshape(shape3), (0, 2, 1))
    out["s5_d"] = g["s5_d"].reshape(-1)
    for k in ("g_pre", "g_post", "conv_b", "lru_b_a", "lru_b_x", "lru_lambda"):
        out[k] = g[k].reshape(-1)
    for k in ("lru_w_a", "lru_w_x", "conv_w", "w_in", "w_glu", "w_bs", "w_bl", "w_out", "w_ple", "w_ple_gate"):
        out[k] = g[k]
    return out


def _local_step(x, p, layers, target):
    rows = x.shape[0]
    ws = [_layer_weights(wl) for wl in layers]
    saved = []
    for i in range(DEPTH):
        x, sv = _layer_fwd(x, p[i], ws[i], rows, target if i == DEPTH - 1 else None)
        saved.append(sv)
    dx, sq = x
    loss = sq[0, 0] * (0.5 / D_MODEL)
    grads = [None] * DEPTH
    for i in reversed(range(DEPTH)):
        dx, g = _layer_bwd(dx, saved[i], ws[i], rows)
        grads[i] = _layer_param_grads(g, ws[i], layers[i])
    return loss, dx, grads


def _place():
    return lax.axis_index("x"), lax.axis_index("y"), lax.axis_index("c")


def _other_chips(x, y):
    return [(1 - x, y), (x, 1 - y), (1 - x, 1 - y)]


def _any_spec():
    return pl.BlockSpec(memory_space=pl.ANY)


def _gather_chips(name, vs, via_sibling):
    n = len(vs)
    halved = [via_sibling and v.shape[0] % (16 * 4 // v.dtype.itemsize) == 0 for v in vs]

    def body(*refs):
        v_refs, out_refs, send_sems, recv_sems = refs[:n], refs[n:2 * n], refs[2 * n], refs[2 * n + 1]
        x, y, c = _place()
        me = 2 * x + y
        chips = _other_chips(x, y)
        slots = [2 * cx + cy for cx, cy in chips]

        def part(a, slot, hc):
            if not halved[a]:
                return out_refs[a].at[slot]
            half = vs[a].shape[0] // 2
            return out_refs[a].at[slot, pl.ds(hc * half, half), :]

        def own(a):
            if not halved[a]:
                return v_refs[a]
            half = vs[a].shape[0] // 2
            return v_refs[a].at[pl.ds(c * half, half), :]

        def copy(a, k, src, dst, to):
            return pltpu.make_async_remote_copy(src_ref=src, dst_ref=dst, send_sem=send_sems.at[6 * a + k],
                                                recv_sem=recv_sems.at[6 * a + k], device_id=to, device_id_type=MESH)

        for a in range(n):
            for k in range(3):
                copy(a, k, own(a), part(a, me, c), (*chips[k], c)).start()
        for a in range(n):
            for k in range(3):
                copy(a, k, own(a), part(a, slots[k], c), (*chips[k], c)).wait_recv()
                if halved[a]:
                    copy(a, 3 + k, part(a, slots[k], c), part(a, slots[k], c), (x, y, 1 - c)).start()
        for a in range(n):
            for k in range(3):
                if halved[a]:
                    copy(a, 3 + k, own(a), part(a, slots[k], 1 - c), (x, y, 1 - c)).wait_recv()
                    copy(a, 3 + k, own(a), part(a, me, c), (x, y, 1 - c)).wait_send()
                copy(a, k, own(a), part(a, me, c), (x, y, 1 - c)).wait_send()

    others = pl.pallas_call(
        body, name=name, out_shape=[jax.ShapeDtypeStruct((4,) + v.shape, v.dtype) for v in vs],
        in_specs=[_any_spec()] * n, out_specs=[_any_spec()] * n,
        scratch_shapes=[pltpu.SemaphoreType.DMA((6 * n,)), pltpu.SemaphoreType.DMA((6 * n,))],
    )(*vs)
    me = 2 * lax.axis_index("x") + lax.axis_index("y")
    return [lax.dynamic_update_slice(o, v[None], (me, 0, 0)) for o, v in zip(others, vs)]


def _rs_sibling(grs):
    n = len(grs)
    halves = [g.shape[1] // 2 for g in grs]

    def body(*refs):
        g_refs, got_refs, send_sems, recv_sems = refs[:n], refs[n:2 * n], refs[2 * n], refs[2 * n + 1]
        x, y, c = _place()
        copies = [pltpu.make_async_remote_copy(
            src_ref=g_refs[a].at[:, pl.ds((1 - c) * halves[a], halves[a]), :], dst_ref=got_refs[a],
            send_sem=send_sems.at[a], recv_sem=recv_sems.at[a], device_id=(x, y, 1 - c), device_id_type=MESH)
            for a in range(n)]
        for cp in copies:
            cp.start()
        for cp in copies:
            cp.wait()

    return pl.pallas_call(
        body, name="rs_sibling",
        out_shape=[jax.ShapeDtypeStruct((4, h, g.shape[2]), F32) for g, h in zip(grs, halves)],
        in_specs=[_any_spec()] * n, out_specs=[_any_spec()] * n,
        scratch_shapes=[pltpu.SemaphoreType.DMA((n,)), pltpu.SemaphoreType.DMA((n,))],
    )(*grs)


def _rs_add_sibling(pk, got):
    _, half, width = got.shape
    tile = _pack_tile(half, width)
    nb = half // tile

    def body(lo_ref, hi_ref, got_ref, a32_ref, a16_ref):
        mine = jnp.where(lax.axis_index("c") == 0, lo_ref[...], hi_ref[...])
        s = mine + got_ref[...]
        a32_ref[...] = s
        a16_ref[...] = s.astype(BF16)

    blk = lambda first: pl.BlockSpec((None, tile, width), lambda s, i: (s, first + i, 0))
    return pl.pallas_call(
        body, name="rs_add_sibling", grid=(4, nb), in_specs=[blk(0), blk(nb), blk(0)], out_specs=[blk(0), blk(0)],
        out_shape=[jax.ShapeDtypeStruct(got.shape, F32), jax.ShapeDtypeStruct(got.shape, BF16)],
        compiler_params=pltpu.CompilerParams(dimension_semantics=("arbitrary", "arbitrary"), vmem_limit_bytes=VMEM_LIMIT),
    )(pk, pk, got)


def _rs_chips(a16s):
    n = len(a16s)

    def body(*refs):
        a_refs, got_refs, send_sems, recv_sems = refs[:n], refs[n:2 * n], refs[2 * n], refs[2 * n + 1]
        x, y, c = _place()
        chips = _other_chips(x, y)
        copies = [pltpu.make_async_remote_copy(
            src_ref=a_refs[a].at[2 * cx + cy], dst_ref=got_refs[a].at[k], send_sem=send_sems.at[3 * a + k],
            recv_sem=recv_sems.at[3 * a + k], device_id=(cx, cy, c), device_id_type=MESH)
            for a in range(n) for k, (cx, cy) in enumerate(chips)]
        for cp in copies:
            cp.start()
        for cp in copies:
            cp.wait()

    return pl.pallas_call(
        body, name="rs_chips", out_shape=[jax.ShapeDtypeStruct((3,) + a.shape[1:], a.dtype) for a in a16s],
        in_specs=[_any_spec()] * n, out_specs=[_any_spec()] * n,
        scratch_shapes=[pltpu.SemaphoreType.DMA((3 * n,)), pltpu.SemaphoreType.DMA((3 * n,))],
    )(*a16s)


def _swap_halves(vs):
    n = len(vs)

    def body(*refs):
        v_refs, out_refs, send_sems, recv_sems = refs[:n], refs[n:2 * n], refs[2 * n], refs[2 * n + 1]
        x, y, c = _place()

        def give(a, hc):
            return pltpu.make_async_remote_copy(src_ref=v_refs[a], dst_ref=out_refs[a].at[hc], send_sem=send_sems.at[a],
                                                recv_sem=recv_sems.at[a], device_id=(x, y, 1 - c), device_id_type=MESH)

        for a in range(n):
            give(a, c).start()
        for a in range(n):
            give(a, c).wait_send()
            give(a, 1 - c).wait_recv()

    others = pl.pallas_call(
        body, name="swap_halves", out_shape=[jax.ShapeDtypeStruct((2,) + v.shape, v.dtype) for v in vs],
        in_specs=[_any_spec()] * n, out_specs=[_any_spec()] * n,
        scratch_shapes=[pltpu.SemaphoreType.DMA((n,)), pltpu.SemaphoreType.DMA((n,))],
    )(*vs)
    c = lax.axis_index("c")
    return [lax.dynamic_update_slice(o, v[None], (c, 0, 0)).reshape(2 * v.shape[0], v.shape[1]) for o, v in zip(others, vs)]


WIDE = 1024


def _unpack(flat, shapes, align=1):
    out, off = [], 0
    for s in shapes:
        n = 1
        for q in s:
            n *= q
        out.append(flat[off:off + n].reshape(s))
        off += -(-n // align) * align
    return out


def _pack_tile(rows, width):
    most = (2 ** 21) // (4 * width)
    if rows <= most:
        return rows
    return max(t for t in range(16, most + 1, 16) if rows % t == 0)


def _flat_aligned(v, align):
    v = v.reshape(-1)
    return jnp.pad(v, (0, -v.shape[0] % align))


def _adamw(name, w, g, m, v, tile):
    def fn(i, wb, gb, mb, vb):
        m2 = ADAM_B1 * mb + (1.0 - ADAM_B1) * gb
        v2 = ADAM_B2 * vb + (1.0 - ADAM_B2) * (gb * gb)
        m_hat = m2 / (1.0 - ADAM_B1 ** ADAM_STEP)
        v_hat = v2 / (1.0 - ADAM_B2 ** ADAM_STEP)
        return -ADAM_LR * (m_hat / (jnp.sqrt(v_hat) + ADAM_EPS) + ADAM_WD * wb), m2, v2

    return _rows(name, fn, [(w, "row"), (g, "row"), (m, "row"), (v, "row")], [(w.shape, F32, "row")] * 3,
                 rows=w.shape[0], tile=tile)


def _as_2d(a):
    return a.reshape(-1, a.shape[-1])


def _adam_tile(rows):
    for t in (256, 184, 128, 64, 32, 16, 8):
        if rows % t == 0:
            return t
    return rows


def kernel(x, p, g_pre, w_in, s5_a_re, s5_a_im, s5_log_dt, s5_b_re, s5_b_im, s5_c_re, s5_c_im, s5_d, w_glu, w_bs, conv_w, conv_b, lru_w_a, lru_b_a, lru_w_x, lru_b_x, lru_lambda, w_bl, w_out, g_post, w_ple, w_ple_gate, loss_target, m_g_pre, m_w_in, m_s5_a_re, m_s5_a_im, m_s5_log_dt, m_s5_b_re, m_s5_b_im, m_s5_c_re, m_s5_c_im, m_s5_d, m_w_glu, m_w_bs, m_conv_w, m_conv_b, m_lru_w_a, m_lru_b_a, m_lru_w_x, m_lru_b_x, m_lru_lambda, m_w_bl, m_w_out, m_g_post, m_w_ple, m_w_ple_gate, v_g_pre, v_w_in, v_s5_a_re, v_s5_a_im, v_s5_log_dt, v_s5_b_re, v_s5_b_im, v_s5_c_re, v_s5_c_im, v_s5_d, v_w_glu, v_w_bs, v_conv_w, v_conv_b, v_lru_w_a, v_lru_b_a, v_lru_w_x, v_lru_b_x, v_lru_lambda, v_w_bl, v_w_out, v_g_post, v_w_ple, v_w_ple_gate):
    wts = dict(g_pre=g_pre, w_in=w_in, s5_a_re=s5_a_re, s5_a_im=s5_a_im, s5_log_dt=s5_log_dt, s5_b_re=s5_b_re,
               s5_b_im=s5_b_im, s5_c_re=s5_c_re, s5_c_im=s5_c_im, s5_d=s5_d, w_glu=w_glu, w_bs=w_bs, conv_w=conv_w,
               conv_b=conv_b, lru_w_a=lru_w_a, lru_b_a=lru_b_a, lru_w_x=lru_w_x, lru_b_x=lru_b_x, lru_lambda=lru_lambda,
               w_bl=w_bl, w_out=w_out, g_post=g_post, w_ple=w_ple, w_ple_gate=w_ple_gate)
    mom1 = dict(g_pre=m_g_pre, w_in=m_w_in, s5_a_re=m_s5_a_re, s5_a_im=m_s5_a_im, s5_log_dt=m_s5_log_dt, s5_b_re=m_s5_b_re,
                s5_b_im=m_s5_b_im, s5_c_re=m_s5_c_re, s5_c_im=m_s5_c_im, s5_d=m_s5_d, w_glu=m_w_glu, w_bs=m_w_bs,
                conv_w=m_conv_w, conv_b=m_conv_b, lru_w_a=m_lru_w_a, lru_b_a=m_lru_b_a, lru_w_x=m_lru_w_x, lru_b_x=m_lru_b_x,
                lru_lambda=m_lru_lambda, w_bl=m_w_bl, w_out=m_w_out, g_post=m_g_post, w_ple=m_w_ple, w_ple_gate=m_w_ple_gate)
    mom2 = dict(g_pre=v_g_pre, w_in=v_w_in, s5_a_re=v_s5_a_re, s5_a_im=v_s5_a_im, s5_log_dt=v_s5_log_dt, s5_b_re=v_s5_b_re,
                s5_b_im=v_s5_b_im, s5_c_re=v_s5_c_re, s5_c_im=v_s5_c_im, s5_d=v_s5_d, w_glu=v_w_glu, w_bs=v_w_bs,
                conv_w=v_conv_w, conv_b=v_conv_b, lru_w_a=v_lru_w_a, lru_b_a=v_lru_b_a, lru_w_x=v_lru_w_x, lru_b_x=v_lru_b_x,
                lru_lambda=v_lru_lambda, w_bl=v_w_bl, w_out=v_w_out, g_post=v_g_post, w_ple=v_w_ple, w_ple_gate=v_w_ple_gate)
    names = list(wts)

    by_rows, by_cols = ("w_bl", "w_out", "w_ple_gate"), ("w_glu", "w_bs", "w_ple")
    two_d = lambda k: wts[k].astype(BF16).reshape(-1, wts[k].shape[2])
    sent = [two_d("w_in"), jnp.concatenate([two_d(k) for k in by_rows]), jnp.concatenate([two_d(k) for k in by_cols]),
            wts["conv_w"].reshape(-1, wts["conv_w"].shape[2])]
    g_in, g_rows, g_cols, g_conv = _gather_chips("gather_weights", sent, via_sibling=True)
    whole = {"conv_w": jnp.transpose(g_conv.reshape((4,) + wts["conv_w"].shape), (1, 2, 0, 3)).reshape(DEPTH, CONV_WIDTH, -1)}
    off = 0
    for k in by_rows:
        dp, r, _ = wts[k].shape
        piece = g_rows[:, off:off + dp * r].reshape(4, dp, r, -1)
        whole[k] = jnp.transpose(piece, (1, 0, 2, 3)).reshape(dp, 4 * r, -1)
        off += dp * r
    off = 0
    for k in by_cols:
        dp, r, cs = wts[k].shape
        piece = g_cols[:, off:off + dp * r].reshape(4, dp, r, cs)
        whole[k] = jnp.transpose(piece, (1, 2, 0, 3)).reshape(dp, r, 4 * cs)
        off += dp * r
    layers = []
    for i in range(DEPTH):
        wl = {k: whole[k][i] for k in BIG if k != "w_in"}
        wl["w_in"] = [(g_in, ("part", j, i, wts["w_in"].shape[1])) for j in range(4)]
        wl.update({k: wts[k][i] for k in SMALL})
        layers.append(wl)

    loss, grad_x, grads = _local_step(x[0], p[:, 0], layers, loss_target[0])
    loss = lax.psum(loss, ("x", "y", "c"))

    me = 2 * lax.axis_index("x") + lax.axis_index("y")
    rows_of = lambda k, i, j: grads[i][k][j * (grads[i][k].shape[0] // 4):(j + 1) * (grads[i][k].shape[0] // 4)]
    cols_of = lambda k, i, j: grads[i][k][:, j * (grads[i][k].shape[1] // 4):(j + 1) * (grads[i][k].shape[1] // 4)]
    layer_range = range(DEPTH)
    packs = [
        jnp.stack([jnp.concatenate([grads[i]["w_in"][j] for i in layer_range]) for j in range(4)]),
        jnp.stack([jnp.concatenate([rows_of(k, i, j) for k in by_rows for i in layer_range]) for j in range(4)]),
        jnp.stack([jnp.concatenate([cols_of(k, i, j) for k in by_cols for i in layer_range]) for j in range(4)]),
    ]
    small_names = SMALL + ("conv_w",)
    small_shapes = [(DEPTH,) + grads[0][k].shape for k in small_names]
    small_flat = jnp.concatenate([_flat_aligned(jnp.stack([grads[i][k] for i in layer_range]), WIDE) for k in small_names])
    n_small = small_flat.shape[0]
    small_q = -(-n_small // (4 * 32 * LANES)) * 32 * LANES
    packs.append(jnp.pad(small_flat, (0, 4 * small_q - n_small)).reshape(4, small_q // LANES, LANES))

    gots = _rs_sibling(packs)
    a32s, a16s = [], []
    for pk, got in zip(packs, gots):
        a32, a16 = _rs_add_sibling(pk, got)
        a32s.append(a32)
        a16s.append(a16)
    got3s = _rs_chips(a16s)
    red_halves = []
    for a32, got3 in zip(a32s, got3s):
        half, width = a32.shape[1], a32.shape[2]
        own = lax.dynamic_index_in_dim(a32, me, 0, keepdims=False)

        def f_add2(i, o, g0, g1, g2):
            return (((o + g0) + g1) + g2,)

        red_halves.append(_rows("rs_add2", f_add2, [(own, "row")] + [(got3[k], "row") for k in range(3)],
                                [((half, width), F32, "row")], rows=half, tile=_pack_tile(half, width))[0])
    reds = _swap_halves(red_halves)
    small_red = _gather_chips("gather_small", [reds[3]], via_sibling=False)[0].reshape(-1)[:n_small]

    grad_out = {"w_in": reds[0].reshape(wts["w_in"].shape)}
    for red, ks in ((reds[1], by_rows), (reds[2], by_cols)):
        off = 0
        for k in ks:
            n = wts[k].shape[0] * wts[k].shape[1]
            grad_out[k] = red[off:off + n].reshape(wts[k].shape)
            off += n
    small_out = dict(zip(small_names, _unpack(small_red, small_shapes, align=WIDE)))
    grad_out.update({k: small_out[k] for k in SMALL})
    grad_out["conv_w"] = lax.dynamic_slice_in_dim(small_out["conv_w"], me * wts["conv_w"].shape[2], wts["conv_w"].shape[2], axis=2)
    delta, new_m, new_v = {}, {}, {}
    for k in BIG + SMALL:
        w2 = _as_2d(wts[k])
        res = _adamw("adamw_" + k, w2, _as_2d(grad_out[k]), _as_2d(mom1[k]), _as_2d(mom2[k]), _adam_tile(w2.shape[0]))
        delta[k], new_m[k], new_v[k] = [r.reshape(wts[k].shape) for r in res]
    return (loss, grad_x[None], *[grad_out[k] for k in names], *[delta[k] for k in names],
            *[new_m[k] for k in names], *[new_v[k] for k in names])
```

```python
import jax
import jax.numpy as jnp
from jax import lax
from jax.experimental import pallas as pl
from jax.experimental.pallas import tpu as pltpu

F32 = jnp.float32
BF16 = jnp.bfloat16
MESH = pl.DeviceIdType.MESH

DEPTH = 2
D_MODEL = 1024
NORM_EPS = 1e-6
S5_WIDTH = 512
S5_GROUPS = 32
S5_GROUP = 16
S5_STATE = 64
LRU_WIDTH = 1280
LRU_HEADS = 10
LRU_HEAD_DIM = 128
LRU_C = 8.0
CONV_WIDTH = 4
PLE_DIM = 256
IN_WIDTHS = (S5_WIDTH, S5_WIDTH, LRU_WIDTH, LRU_WIDTH, D_MODEL, D_MODEL)
IN_OFFSETS = (0, 512, 1024, 2304, 3584, 4608, 5632)
IN_SLOT = 5632 // 4
ADAM_LR = 0.001
ADAM_B1 = 0.9
ADAM_B2 = 0.999
ADAM_EPS = 1e-08
ADAM_WD = 0.01
ADAM_STEP = 10

SUBLANES = 8
LANES = 128
S5_HALF_IN = S5_WIDTH // 2
S5_CPLX = S5_GROUPS * S5_STATE
S5_HALF_CPLX = S5_CPLX // 2
S5_LANES = 2 * S5_CPLX
VMEM_LIMIT = 48 * 2 ** 20
ROW_TILE = 256
WIDE_TILE = 512


def _sigmoid(x):
    return 0.5 * jnp.tanh(0.5 * x) + 0.5


def _gelu_parts(x):
    k = 0.7978845608028654
    t = jnp.tanh(k * (x + 0.044715 * x * x * x))
    val = 0.5 * x * (1.0 + t)
    grad = 0.5 * (1.0 + t) + 0.5 * x * (1.0 - t * t) * k * (1.0 + 3.0 * 0.044715 * x * x)
    return val, grad


def _nn(a, w):
    return jnp.dot(a.astype(BF16), w.astype(BF16), preferred_element_type=F32)


def _nt(a, w):
    return lax.dot_general(a.astype(BF16), w.astype(BF16), (((1,), (1,)), ((), ())), preferred_element_type=F32)


def _tn(a, b):
    return lax.dot_general(a.astype(BF16), b.astype(BF16), (((0,), (0,)), ((), ())), preferred_element_type=F32)


def _heads(op, a, w):
    d = LRU_HEAD_DIM
    return jnp.concatenate([op(a[:, h * d:(h + 1) * d], w[h]) for h in range(LRU_HEADS)], axis=1)


def _heads_tn(a, b):
    d = LRU_HEAD_DIM
    return jnp.stack([_tn(a[:, h * d:(h + 1) * d], b[:, h * d:(h + 1) * d]) for h in range(LRU_HEADS)], axis=0)


def _rows_before(x, halo, s):
    main = pltpu.roll(x, s, 0)
    head = pltpu.roll(jnp.concatenate([halo, x[0:SUBLANES]], axis=0), s, 0)[SUBLANES:2 * SUBLANES]
    return jnp.concatenate([head, main[SUBLANES:]], axis=0)


def _rows_after(x, halo, s):
    n = x.shape[0]
    main = pltpu.roll(x, n - s, 0)
    tail = pltpu.roll(jnp.concatenate([x[n - SUBLANES:], halo], axis=0), 2 * SUBLANES - s, 0)[0:SUBLANES]
    return jnp.concatenate([main[:n - SUBLANES], tail], axis=0)


def _rows(name, fn, ins, outs, *, rows, tile):
    tile = min(tile, rows)
    n = rows // tile
    assert n * tile == rows, (name, rows, tile)
    in_specs = []
    for arr, kind in ins:
        halo = SUBLANES * (4 // arr.dtype.itemsize)
        per, last = tile // halo, rows // halo - 1
        if isinstance(kind, tuple):
            _, j, k, r = kind
            in_specs.append(pl.BlockSpec((None, r, arr.shape[2]), lambda i, j=j, k=k: (j, k, 0)))
        elif kind in ("row", "raw"):
            in_specs.append(pl.BlockSpec((tile, arr.shape[1]), lambda i: (i, 0)))
        elif kind == "prev":
            in_specs.append(pl.BlockSpec((halo, arr.shape[1]), lambda i, per=per: (jnp.maximum(i * per - 1, 0), 0)))
        elif kind == "next":
            in_specs.append(pl.BlockSpec((halo, arr.shape[1]),
                                         lambda i, per=per, last=last: (jnp.minimum((i + 1) * per, last), 0)))
        else:
            in_specs.append(pl.BlockSpec(arr.shape, lambda i, nd=arr.ndim: (0,) * nd))
    out_shape, out_specs = [], []
    for shape, dtype, kind in outs:
        out_shape.append(jax.ShapeDtypeStruct(shape, dtype))
        if kind == "row":
            out_specs.append(pl.BlockSpec((tile, shape[1]), lambda i: (i, 0)))
        else:
            out_specs.append(pl.BlockSpec(shape, lambda i, nd=len(shape): (0,) * nd))
    n_in = len(ins)

    def load(ref, kind):
        v = ref[...]
        if kind in ("row", "prev", "next"):
            v = v.astype(F32)
        if kind == "prev":
            v = v[v.shape[0] - SUBLANES:]
        if kind == "next":
            v = v[:SUBLANES]
        return v

    def body(*refs):
        i = pl.program_id(0)
        vals = fn(i, *[load(r, kind) for r, (_, kind) in zip(refs[:n_in], ins)])
        assert len(vals) == len(outs), name
        for r, v, (_, _, kind) in zip(refs[n_in:], vals, outs):
            if kind == "row":
                r[...] = v.astype(r.dtype)
            else:
                @pl.when(i == 0)
                def _():
                    r[...] = jnp.zeros_like(r)

                r[...] += v.astype(r.dtype)

    return pl.pallas_call(
        body, name=name, grid=(n,), in_specs=in_specs, out_specs=out_specs, out_shape=out_shape,
        compiler_params=pltpu.CompilerParams(dimension_semantics=("arbitrary",), vmem_limit_bytes=VMEM_LIMIT),
    )(*[a for a, _ in ins])


def _s5_discretise(are, aim, ldt, bre, bim):
    dt = jnp.exp(ldt)
    er = jnp.exp(are * dt)
    abr = er * jnp.cos(aim * dt)
    abi = er * jnp.sin(aim * dt)
    den = are * are + aim * aim
    zr = ((abr - 1.0) * are + abi * aim) / den
    zi = (abi * are - (abr - 1.0) * aim) / den
    return abr, abi, zr * bre - zi * bim, zr * bim + zi * bre


def _s5_prep(are, aim, ldt, bre, bim):
    def body(a, b, c, d, e, o0, o1, o2, o3):
        r = _s5_discretise(a[...], b[...], c[...], d[...], e[...])
        o0[...], o1[...], o2[...], o3[...] = r

    sd = jax.ShapeDtypeStruct(are.shape, F32)
    return pl.pallas_call(body, name="s5_prep", out_shape=[sd] * 4)(are, aim, ldt, bre, bim)


def _s5_prep_bwd(are, aim, ldt, bre, bim, cts):
    def body(a, b, c, d, e, c0, c1, c2, c3, o0, o1, o2, o3, o4):
        _, vjp = jax.vjp(_s5_discretise, a[...], b[...], c[...], d[...], e[...])
        r = vjp((c0[...], c1[...], c2[...], c3[...]))
        o0[...], o1[...], o2[...], o3[...], o4[...] = r

    sd = jax.ShapeDtypeStruct(are.shape, F32)
    return pl.pallas_call(body, name="s5_prep_bwd", out_shape=[sd] * 5)(are, aim, ldt, bre, bim, *cts)


def _s5_consts(abr, abi, seg):
    shape = (SUBLANES, S5_CPLX)
    assert seg & (seg - 1) == 0 and seg % SUBLANES == 0, seg

    def body(ar_ref, ai_ref, f_ref, b_ref):
        def cmul(p, q):
            return (p[0] * q[0] - p[1] * q[1], p[0] * q[1] + p[1] * q[0])

        row = lax.broadcasted_iota(jnp.int32, shape, 0)
        a1 = (jnp.broadcast_to(ar_ref[...], shape), jnp.broadcast_to(ai_ref[...], shape))
        squares = [a1]
        while 1 << (len(squares) - 1) < 4 * seg:
            squares.append(cmul(squares[-1], squares[-1]))
        nb = seg.bit_length() - 1
        fwd, rev = [], []
        for k, a in ((1, squares[nb]), (2, squares[nb + 1]), (4, squares[nb + 2])):
            fwd += [jnp.where(row >= k, a[0], 0.0), jnp.where(row >= k, a[1], 0.0)]
            rev += [jnp.where(row <= 7 - k, a[0], 0.0), jnp.where(row <= 7 - k, -a[1], 0.0)]
        fwd += [a1[0], a1[1]]
        rev += [a1[0], -a1[1]]
        e = lax.broadcasted_iota(jnp.int32, (seg, S5_CPLX), 0) + 1
        wide = lambda v: jnp.broadcast_to(v[0:1, :], (seg, S5_CPLX))
        pr, pi = jnp.ones((seg, S5_CPLX), F32), jnp.zeros((seg, S5_CPLX), F32)
        for b in range(nb + 1):
            sr, si = wide(squares[b][0]), wide(squares[b][1])
            bit = ((e >> b) & 1) == 1
            pr, pi = jnp.where(bit, pr * sr - pi * si, pr), jnp.where(bit, pr * si + pi * sr, pi)
        f_ref[...] = jnp.concatenate(fwd + [pr, pi], axis=0)
        b_ref[...] = jnp.concatenate(rev + [pr, -pi], axis=0)

    sd = jax.ShapeDtypeStruct((8 * SUBLANES + 2 * seg, S5_CPLX), F32)
    return pl.pallas_call(body, name="s5_consts", out_shape=[sd, sd])(abr, abi)


S5_TILES = S5_LANES // LANES
S5_HALF_TILES = S5_TILES // 2


def _s5_tile_index(q):
    re = (q // 8) * S5_HALF_TILES + (q % 8)
    return re, re + S5_HALF_TILES // 2


def _lanes_of(ref, first, count):
    return jnp.concatenate([ref[j] for j in range(first, first + count)], axis=1)


def _to_lane_tiles(ref, first, value):
    for j in range(value.shape[1] // LANES):
        ref[first + j] = value[:, j * LANES:(j + 1) * LANES]


def _time_perm(tile, transpose=False):
    seg = tile // SUBLANES
    rho = lax.broadcasted_iota(jnp.int32, (tile, tile), 1 if transpose else 0)
    t = lax.broadcasted_iota(jnp.int32, (tile, tile), 0 if transpose else 1)
    return (t == (rho & (SUBLANES - 1)) * seg + (rho >> 3)).astype(BF16)


def _reorder(perm, x):
    return jnp.dot(perm, x, preferred_element_type=F32)


def _s5_scan(s_ref, sc_ref, carry_ref, tile, reverse):
    seg = tile // SUBLANES
    group = 4
    edge = 0 if reverse else SUBLANES - 1
    row = lax.broadcasted_iota(jnp.int32, (SUBLANES, LANES), 0)
    order = range(seg - 1, -1, -1) if reverse else range(seg)
    rows_of = lambda k: pl.ds(k * SUBLANES, SUBLANES)
    base = 8 * SUBLANES

    for q0 in range(0, S5_CPLX // LANES, group):
        qs = list(range(q0, q0 + group))
        tiles = [_s5_tile_index(q) for q in qs]
        cst = lambda k, q: sc_ref[k * SUBLANES:(k + 1) * SUBLANES, q * LANES:(q + 1) * LANES]
        state = [(jnp.zeros((SUBLANES, LANES), F32), jnp.zeros((SUBLANES, LANES), F32)) for _ in qs]
        mult = [(cst(6, q), cst(7, q)) for q in qs]
        for k in order:
            for j, (re, im) in enumerate(tiles):
                ar, ai = mult[j]
                xr, xi = state[j]
                nr = ar * xr - ai * xi + s_ref[re, rows_of(k), :]
                ni = ar * xi + ai * xr + s_ref[im, rows_of(k), :]
                s_ref[re, rows_of(k), :] = nr
                s_ref[im, rows_of(k), :] = ni
                state[j] = (nr, ni)
        start = []
        for j, (q, (re, im)) in enumerate(zip(qs, tiles)):
            er, ei = state[j]
            shift1 = SUBLANES - 1 if reverse else 1
            dr = jnp.where(row == SUBLANES - 1 - edge, carry_ref[re], pltpu.roll(er, shift1, 0))
            di = jnp.where(row == SUBLANES - 1 - edge, carry_ref[im], pltpu.roll(ei, shift1, 0))
            for c, sh in ((0, 1), (2, 2), (4, 4)):
                shift = SUBLANES - sh if reverse else sh
                ar, ai = cst(c, q), cst(c + 1, q)
                sr, si = pltpu.roll(dr, shift, 0), pltpu.roll(di, shift, 0)
                dr, di = dr + ar * sr - ai * si, di + ar * si + ai * sr
            start.append((dr, di))
        for k in order:
            t = seg - 1 - k if reverse else k
            for j, (q, (re, im)) in enumerate(zip(qs, tiles)):
                lanes = slice(q * LANES, (q + 1) * LANES)
                pr = jnp.broadcast_to(sc_ref[base + t:base + t + 1, lanes], (SUBLANES, LANES))
                pi = jnp.broadcast_to(sc_ref[base + seg + t:base + seg + t + 1, lanes], (SUBLANES, LANES))
                cr, ci = start[j]
                xr = s_ref[re, rows_of(k), :] + pr * cr - pi * ci
                xi = s_ref[im, rows_of(k), :] + pr * ci + pi * cr
                s_ref[re, rows_of(k), :] = xr
                s_ref[im, rows_of(k), :] = xi
                if k == order[-1]:
                    carry_ref[re] = jnp.broadcast_to(xr[edge:edge + 1, :], (SUBLANES, LANES))
                    carry_ref[im] = jnp.broadcast_to(xi[edge:edge + 1, :], (SUBLANES, LANES))


def _s5_fwd(u, bd, cdt, dskip, sc, *, rows, tile):
    n = rows // tile

    def body(u_ref, bd_ref, cdt_ref, d_ref, sc_ref, y_ref, s_ref, carry_ref):
        @pl.when(pl.program_id(0) == 0)
        def _():
            carry_ref[...] = jnp.zeros_like(carry_ref)

        ub = _reorder(_time_perm(tile), u_ref[...].astype(BF16)).astype(BF16)
        for h in range(2):
            _to_lane_tiles(s_ref, h * S5_HALF_TILES, jnp.dot(ub[:, h * S5_HALF_IN:(h + 1) * S5_HALF_IN], bd_ref[h],
                                                             preferred_element_type=F32))
        _s5_scan(s_ref, sc_ref, carry_ref, tile, reverse=False)
        ys = [_nt(_lanes_of(s_ref, h * S5_HALF_TILES, S5_HALF_TILES), cdt_ref[h]) for h in range(2)]
        y = _reorder(_time_perm(tile, transpose=True), jnp.concatenate(ys, axis=1).astype(BF16))
        y_ref[...] = y + d_ref[...] * u_ref[...]

    full = lambda a: pl.BlockSpec(a.shape, lambda i, nd=a.ndim: (0,) * nd)
    return pl.pallas_call(
        body, name="s5_fwd", grid=(n,),
        in_specs=[pl.BlockSpec((tile, S5_WIDTH), lambda i: (i, 0)), full(bd), full(cdt), full(dskip), full(sc)],
        out_specs=[pl.BlockSpec((tile, S5_WIDTH), lambda i: (i, 0)),
                   pl.BlockSpec((S5_TILES, tile, LANES), lambda i: (0, i, 0))],
        out_shape=[jax.ShapeDtypeStruct((rows, S5_WIDTH), F32), jax.ShapeDtypeStruct((S5_TILES, rows, LANES), F32)],
        scratch_shapes=[pltpu.VMEM((S5_TILES, SUBLANES, LANES), F32)],
        compiler_params=pltpu.CompilerParams(dimension_semantics=("arbitrary",), vmem_limit_bytes=VMEM_LIMIT),
    )(u, bd, cdt, dskip, sc)


def _s5_bwd(dy, s, u, bd, cdt, dskip, sc, *, rows, tile):
    n = rows // tile
    hc = 2 * S5_HALF_CPLX
    per8 = tile // SUBLANES
    quarter = S5_HALF_TILES // 2

    def body(dy_ref, s_ref, sp_ref, u_ref, bd_ref, cdt_ref, d_ref, sc_ref,
             du_ref, dbd_ref, dcdt_ref, dd_ref, da_ref, g_ref, carry_ref):
        i = pl.program_id(0)

        @pl.when(i == 0)
        def _():
            carry_ref[...] = jnp.zeros_like(carry_ref)
            dbd_ref[...] = jnp.zeros_like(dbd_ref)
            dcdt_ref[...] = jnp.zeros_like(dcdt_ref)
            dd_ref[...] = jnp.zeros_like(dd_ref)
            da_ref[...] = jnp.zeros_like(da_ref)

        dy = dy_ref[...]
        u = u_ref[...]
        perm = _time_perm(tile)
        dyb = _reorder(perm, dy.astype(BF16)).astype(BF16)
        ub = _reorder(perm, u.astype(BF16)).astype(BF16)
        for h in range(2):
            _to_lane_tiles(g_ref, h * S5_HALF_TILES, jnp.dot(dyb[:, h * S5_HALF_IN:(h + 1) * S5_HALF_IN], cdt_ref[h],
                                                             preferred_element_type=F32))
        _s5_scan(g_ref, sc_ref, carry_ref, tile, reverse=True)
        dus = []
        for h in range(2):
            gb = _lanes_of(g_ref, h * S5_HALF_TILES, S5_HALF_TILES).astype(BF16)
            sb = _lanes_of(s_ref, h * S5_HALF_TILES, S5_HALF_TILES).astype(BF16)
            dus.append(_nt(gb, bd_ref[h]))
            dbd_ref[h] += _tn(ub[:, h * S5_HALF_IN:(h + 1) * S5_HALF_IN], gb)
            dcdt_ref[h] += _tn(dyb[:, h * S5_HALF_IN:(h + 1) * S5_HALF_IN], sb)
        du = _reorder(_time_perm(tile, transpose=True), jnp.concatenate(dus, axis=1).astype(BF16))
        du_ref[...] = (du + d_ref[...] * dy).astype(du_ref.dtype)
        dd_ref[...] += jnp.sum(dy * u, axis=0, keepdims=True)

        not_first = (i < n - 1).astype(F32)
        row = lax.broadcasted_iota(jnp.int32, (SUBLANES, quarter * LANES), 0)

        def step_before(first):
            cur = _lanes_of(s_ref, first, quarter)
            before_tile = _lanes_of(sp_ref, first, quarter)[SUBLANES - 1:SUBLANES, :] * not_first
            head = jnp.where(row == 0, before_tile, pltpu.roll(cur[tile - SUBLANES:], 1, 0))
            return jnp.concatenate([head, cur[:tile - SUBLANES]], axis=0)

        for h in range(2):
            re, im = h * S5_HALF_TILES, h * S5_HALF_TILES + quarter
            ssr = step_before(re)
            ssi = step_before(im)
            gr = _lanes_of(g_ref, re, quarter)
            gi = _lanes_of(g_ref, im, quarter)
            lanes = slice(h * S5_HALF_CPLX, (h + 1) * S5_HALF_CPLX)
            da_ref[0:1, lanes] += jnp.sum(ssr * gr + ssi * gi, axis=0, keepdims=True)
            da_ref[1:2, lanes] += jnp.sum(ssr * gi - ssi * gr, axis=0, keepdims=True)

    full = lambda a: pl.BlockSpec(a.shape, lambda i, nd=a.ndim: (0,) * nd)
    rev = lambda i: (n - 1 - i, 0)
    wshape = (2, S5_HALF_IN, hc)
    return pl.pallas_call(
        body, name="s5_bwd", grid=(n,),
        in_specs=[pl.BlockSpec((tile, S5_WIDTH), rev), pl.BlockSpec((S5_TILES, tile, LANES), lambda i: (0, n - 1 - i, 0)),
                  pl.BlockSpec((S5_TILES, SUBLANES, LANES), lambda i: (0, jnp.maximum((n - 1 - i) * per8 - 1, 0), 0)),
                  pl.BlockSpec((tile, S5_WIDTH), rev), full(bd), full(cdt), full(dskip), full(sc)],
        out_specs=[pl.BlockSpec((tile, S5_WIDTH), rev),
                   pl.BlockSpec(wshape, lambda i: (0, 0, 0)), pl.BlockSpec(wshape, lambda i: (0, 0, 0)),
                   pl.BlockSpec((1, S5_WIDTH), lambda i: (0, 0)), pl.BlockSpec((SUBLANES, S5_CPLX), lambda i: (0, 0))],
        out_shape=[jax.ShapeDtypeStruct((rows, S5_WIDTH), BF16), jax.ShapeDtypeStruct(wshape, F32),
                   jax.ShapeDtypeStruct(wshape, F32), jax.ShapeDtypeStruct((1, S5_WIDTH), F32),
                   jax.ShapeDtypeStruct((SUBLANES, S5_CPLX), F32)],
        scratch_shapes=[pltpu.VMEM((S5_TILES, tile, LANES), F32), pltpu.VMEM((S5_TILES, SUBLANES, LANES), F32)],
        compiler_params=pltpu.CompilerParams(dimension_semantics=("arbitrary",), vmem_limit_bytes=VMEM_LIMIT),
    )(dy, s, s, u, bd, cdt, dskip, sc)


def _s5_block_diag(parts):
    v = jnp.stack(parts, axis=2).reshape(2, 16, S5_GROUP, 2, S5_STATE)
    eye = jnp.eye(16, dtype=v.dtype)
    return jnp.einsum("hgcpn,gk->hgcpkn", v, eye).reshape(2, S5_HALF_IN, 2 * S5_HALF_CPLX)


def _s5_block_diag_extract(m):
    v = m.reshape(2, 16, S5_GROUP, 2, 16, S5_STATE)
    d = jnp.diagonal(v, axis1=1, axis2=4)
    d = jnp.transpose(d, (2, 0, 4, 1, 3)).reshape(2, S5_GROUPS, S5_GROUP, S5_STATE)
    return d[0], d[1]


def _cplx_to_lanes(v):
    return v.reshape(1, S5_CPLX)


def _lru_scan_fwd(a, b, *, rows, tile):
    n = rows // tile
    nblk = tile // SUBLANES
    group = 5

    def body(a_ref, b_ref, h_ref, carry_ref):
        @pl.when(pl.program_id(0) == 0)
        def _():
            carry_ref[...] = jnp.zeros_like(carry_ref)

        row = lax.broadcasted_iota(jnp.int32, (SUBLANES, LANES), 0)
        for q0 in range(0, LRU_WIDTH // LANES, group):
            offs = [q * LANES for q in range(q0, q0 + group)]

            def blk(t, carry, offs=offs):
                r0 = pl.multiple_of(t * SUBLANES, SUBLANES)
                new = []
                for j, o in enumerate(offs):
                    av = a_ref[pl.ds(r0, SUBLANES), o:o + LANES]
                    xv = b_ref[pl.ds(r0, SUBLANES), o:o + LANES]
                    for sh in (1, 2, 4):
                        m = row >= sh
                        xs = pltpu.roll(xv, sh, 0)
                        asft = pltpu.roll(av, sh, 0)
                        xv = xv + jnp.where(m, av * xs, 0.0)
                        av = jnp.where(m, av * asft, av)
                    hv = xv + av * carry[j]
                    h_ref[pl.ds(r0, SUBLANES), o:o + LANES] = hv
                    new.append(jnp.broadcast_to(hv[SUBLANES - 1:SUBLANES, :], (SUBLANES, LANES)))
                return tuple(new)

            carry = lax.fori_loop(0, nblk, blk, tuple(carry_ref[:, o:o + LANES] for o in offs), unroll=2)
            for j, o in enumerate(offs):
                carry_ref[:, o:o + LANES] = carry[j]

    spec = pl.BlockSpec((tile, LRU_WIDTH), lambda i: (i, 0))
    return pl.pallas_call(
        body, name="lru_scan_fwd", grid=(n,), in_specs=[spec, spec], out_specs=spec,
        out_shape=jax.ShapeDtypeStruct((rows, LRU_WIDTH), F32),
        scratch_shapes=[pltpu.VMEM((SUBLANES, LRU_WIDTH), F32)],
        compiler_params=pltpu.CompilerParams(dimension_semantics=("arbitrary",), vmem_limit_bytes=VMEM_LIMIT),
    )(a, b)


def _lru_scan_bwd(dh, a, *, rows, tile):
    n = rows // tile
    nblk = tile // SUBLANES
    group = 5

    def body(dh_ref, a_ref, g_ref, cg_ref, ca_ref):
        @pl.when(pl.program_id(0) == 0)
        def _():
            cg_ref[...] = jnp.zeros_like(cg_ref)
            ca_ref[...] = jnp.zeros_like(ca_ref)

        row = lax.broadcasted_iota(jnp.int32, (SUBLANES, LANES), 0)
        for q0 in range(0, LRU_WIDTH // LANES, group):
            offs = [q * LANES for q in range(q0, q0 + group)]

            def blk(t, carry, offs=offs):
                r0 = pl.multiple_of((nblk - 1 - t) * SUBLANES, SUBLANES)
                new = []
                for j, o in enumerate(offs):
                    cg, ca = carry[2 * j], carry[2 * j + 1]
                    araw = a_ref[pl.ds(r0, SUBLANES), o:o + LANES]
                    xv = dh_ref[pl.ds(r0, SUBLANES), o:o + LANES]
                    av = jnp.where(row == SUBLANES - 1, ca, pltpu.roll(araw, SUBLANES - 1, 0))
                    for sh in (1, 2, 4):
                        m = row <= SUBLANES - 1 - sh
                        xs = pltpu.roll(xv, SUBLANES - sh, 0)
                        asft = pltpu.roll(av, SUBLANES - sh, 0)
                        xv = xv + jnp.where(m, av * xs, 0.0)
                        av = jnp.where(m, av * asft, av)
                    gv = xv + av * cg
                    g_ref[pl.ds(r0, SUBLANES), o:o + LANES] = gv
                    new.append(jnp.broadcast_to(gv[0:1, :], (SUBLANES, LANES)))
                    new.append(jnp.broadcast_to(araw[0:1, :], (SUBLANES, LANES)))
                return tuple(new)

            carry0 = tuple(r[:, o:o + LANES] for o in offs for r in (cg_ref, ca_ref))
            carry = lax.fori_loop(0, nblk, blk, carry0, unroll=2)
            for j, o in enumerate(offs):
                cg_ref[:, o:o + LANES] = carry[2 * j]
                ca_ref[:, o:o + LANES] = carry[2 * j + 1]

    spec = pl.BlockSpec((tile, LRU_WIDTH), lambda i: (n - 1 - i, 0))
    return pl.pallas_call(
        body, name="lru_scan_bwd", grid=(n,), in_specs=[spec, spec], out_specs=spec,
        out_shape=jax.ShapeDtypeStruct((rows, LRU_WIDTH), F32),
        scratch_shapes=[pltpu.VMEM((SUBLANES, LRU_WIDTH), F32), pltpu.VMEM((SUBLANES, LRU_WIDTH), F32)],
        compiler_params=pltpu.CompilerParams(dimension_semantics=("arbitrary",), vmem_limit_bytes=VMEM_LIMIT),
    )(dh, a)


def _conv_fwd(i, x, prev, cw, cb):
    prev = prev * (i > 0).astype(F32)
    y = x * cw[3:4, :] + cb
    for s in range(1, CONV_WIDTH):
        y = y + _rows_before(x, prev, s) * cw[3 - s:4 - s, :]
    return y


def _lru_gates(c, wa, ba, wx, bx, lam):
    r = _sigmoid(_heads(_nn, c, wa) + ba)
    ig = _sigmoid(_heads(_nn, c, wx) + bx)
    z = -lam
    sp = jnp.maximum(z, 0.0) + jnp.log(1.0 + jnp.exp(-jnp.abs(z)))
    log_a = -LRU_C * r * sp
    a = jnp.exp(log_a)
    z2 = 2.0 * log_a
    series = -z2 * (1.0 + z2 * (0.5 + z2 * (1.0 / 6.0 + z2 * (1.0 / 24.0 + z2 * (1.0 / 120.0 + z2 / 720.0)))))
    one_minus = jnp.where(z2 > -0.2, series, 1.0 - jnp.exp(z2))
    mult = jnp.sqrt(one_minus)
    return r, ig, sp, a, mult


def _layer_fwd(x, p, w, rows, target=None):
    tile = WIDE_TILE
    d = D_MODEL

    def f_in(i, xb, g, *ws):
        rstd = lax.rsqrt(jnp.mean(xb * xb, axis=-1, keepdims=True) + NORM_EPS)
        hb = (xb * rstd * g).astype(BF16)
        proj = jnp.concatenate([jnp.dot(hb, wj, preferred_element_type=F32) for wj in ws], axis=1)
        return tuple(proj[:, IN_OFFSETS[k]:IN_OFFSETS[k + 1]] for k in range(6)) + (hb,)

    s5x, s5g, lrux, lrug, gs, gl, h = _rows(
        "f_in", f_in, [(x, "row"), (w["g_pre"], "full")] + w["w_in"],
        [((rows, wd), BF16, "row") for wd in IN_WIDTHS] + [((rows, d), BF16, "row")], rows=rows, tile=ROW_TILE)

    ys, st = _s5_fwd(s5x, w["bd"], w["cdt"], w["s5_d"], w["scf"], rows=rows, tile=ROW_TILE)

    def f_s5post(i, ysb, gb, wglu, wbs):
        glv, _ = _gelu_parts(ysb)
        glu = _nn(glv, wglu)
        y2 = glu[:, :S5_WIDTH] * _sigmoid(glu[:, S5_WIDTH:]) * (gb * _sigmoid(gb))
        return (_nn(y2, wbs),)

    (z_s,) = _rows("f_s5post", f_s5post, [(ys, "row"), (s5g, "row"), (w["w_glu"], "full"), (w["w_bs"], "full")],
                   [((rows, d), BF16, "row")], rows=rows, tile=tile)

    def f_gates(i, xb, prev, cw, cb, wa, ba, wx, bx, lam):
        c = _conv_fwd(i, xb, prev, cw, cb)
        _, ig, _, a, mult = _lru_gates(c, wa, ba, wx, bx, lam)
        return a, mult * (ig * c)

    a, b = _rows("f_gates", f_gates,
                 [(lrux, "row"), (lrux, "prev"), (w["conv_w"], "full"), (w["conv_b"], "full"), (w["lru_w_a"], "full"),
                  (w["lru_b_a"], "full"), (w["lru_w_x"], "full"), (w["lru_b_x"], "full"), (w["lru_lambda"], "full")],
                 [((rows, LRU_WIDTH), F32, "row")] * 2, rows=rows, tile=tile)
    hl = _lru_scan_fwd(a, b, rows=rows, tile=min(2 * tile, rows))

    def f_merge(i, hb, lg, zs, gsb, glb, xb, wbl, wout, gpost):
        z_l = _nn(hb * (lg * _sigmoid(lg)), wbl)
        merged = _sigmoid(gsb) * zs + _sigmoid(glb) * z_l
        mix = _nn(merged, wout)
        rstd = lax.rsqrt(jnp.mean(mix * mix, axis=-1, keepdims=True) + NORM_EPS)
        return xb + mix * rstd * gpost, mix, z_l

    x1, mix, z_l = _rows("f_merge", f_merge,
                         [(hl, "row"), (lrug, "row"), (z_s, "row"), (gs, "row"), (gl, "row"), (x, "row"),
                          (w["w_bl"], "full"), (w["w_out"], "full"), (w["g_post"], "full")],
                         [((rows, d), F32, "row"), ((rows, d), BF16, "row"), ((rows, d), BF16, "row")], rows=rows, tile=tile)

    saved = dict(x=x, h=h, s5x=s5x, s5g=s5g, lrux=lrux, lrug=lrug, gs=gs, gl=gl, ys=ys, st=st, a=a, hl=hl, z_s=z_s,
                 z_l=z_l, mix=mix, x1=x1, p=p)
    ple_ins = [(x1, "row"), (p, "row"), (w["w_ple"], "full"), (w["w_ple_gate"], "full")]
    if target is None:
        def f_ple(i, x1b, pb, wple, wpg):
            return (x1b + _nn(pb, wple) * _sigmoid(_nn(x1b, wpg)),)

        return _rows("f_ple", f_ple, ple_ins, [((rows, d), F32, "row")], rows=rows, tile=tile)[0], saved

    def f_ple_loss(i, x1b, pb, wple, wpg, tb):
        e = x1b + _nn(pb, wple) * _sigmoid(_nn(x1b, wpg)) - tb
        return e * (1.0 / D_MODEL), jnp.sum(jnp.sum(e * e, axis=0, keepdims=True), axis=1, keepdims=True)

    return _rows("f_ple_loss", f_ple_loss, ple_ins + [(target, "row")],
                 [((rows, d), F32, "row"), ((1, 1), F32, "acc")], rows=rows, tile=tile), saved


def _layer_bwd(dx2, sv, w, rows):
    tile = WIDE_TILE
    d = D_MODEL
    g = {}

    def b_ple(i, dxb, x1b, pb, wple, wpg):
        pe = _nn(pb, wple)
        sg = _sigmoid(_nn(x1b, wpg))
        dpe = dxb * sg
        dgt = dxb * pe * sg * (1.0 - sg)
        return dxb + _nt(dgt, wpg), _tn(pb, dpe), _tn(x1b, dgt)

    dx1, g["w_ple"], g["w_ple_gate"] = _rows(
        "b_ple", b_ple, [(dx2, "row"), (sv["x1"], "row"), (sv["p"], "row"), (w["w_ple"], "full"), (w["w_ple_gate"], "full")],
        [((rows, d), F32, "row"), ((PLE_DIM, d), F32, "acc"), ((d, d), F32, "acc")], rows=rows, tile=tile)

    def b_merge(i, dxb, mixb, zs, zl, gsb, glb, wout, gpost):
        rstd = lax.rsqrt(jnp.mean(mixb * mixb, axis=-1, keepdims=True) + NORM_EPS)
        nrm = mixb * rstd
        dn = dxb * gpost
        dmix = rstd * (dn - nrm * jnp.mean(dn * nrm, axis=-1, keepdims=True))
        ss, sl = _sigmoid(gsb), _sigmoid(glb)
        merged = ss * zs + sl * zl
        dm = _nt(dmix, wout)
        return (dm * ss, dm * sl, dm * zs * ss * (1.0 - ss), dm * zl * sl * (1.0 - sl),
                _tn(merged, dmix), jnp.sum(dxb * nrm, axis=0, keepdims=True))

    dz_s, dz_l, dgs, dgl, g["w_out"], g["g_post"] = _rows(
        "b_merge", b_merge,
        [(dx1, "row"), (sv["mix"], "row"), (sv["z_s"], "row"), (sv["z_l"], "row"), (sv["gs"], "row"), (sv["gl"], "row"),
         (w["w_out"], "full"), (w["g_post"], "full")],
        [((rows, d), BF16, "row")] * 4 + [((d, d), F32, "acc"), ((1, d), F32, "acc")], rows=rows, tile=tile)

    def b_bl(i, dzl, hb, lg, wbl):
        sl = _sigmoid(lg)
        silu = lg * sl
        dy3 = _nt(dzl, wbl)
        return dy3 * silu, dy3 * hb * sl * (1.0 + lg * (1.0 - sl)), _tn(hb * silu, dzl)

    dh, dlrug, g["w_bl"] = _rows(
        "b_bl", b_bl, [(dz_l, "row"), (sv["hl"], "row"), (sv["lrug"], "row"), (w["w_bl"], "full")],
        [((rows, LRU_WIDTH), F32, "row"), ((rows, LRU_WIDTH), BF16, "row"), ((LRU_WIDTH, d), F32, "acc")], rows=rows, tile=tile)

    gh = _lru_scan_bwd(dh, sv["a"], rows=rows, tile=min(2 * tile, rows))

    def b_gates(i, ghb, hb, hprev, xb, xprev, cw, cb, wa, ba, wx, bx, lam):
        c = _conv_fwd(i, xb, xprev, cw, cb)
        r, ig, sp, a, mult = _lru_gates(c, wa, ba, wx, bx, lam)
        h_before = _rows_before(hb, hprev * (i > 0).astype(F32), 1)
        da = ghb * h_before
        dmult = ghb * ig * c
        dlog_a = da * a - dmult * a * a / mult
        dpre_r = dlog_a * (-LRU_C) * sp * r * (1.0 - r)
        dpre_i = ghb * mult * c * ig * (1.0 - ig)
        dc = ghb * mult * ig + _heads(_nt, dpre_r, wa) + _heads(_nt, dpre_i, wx)
        dlam = jnp.sum(dlog_a * LRU_C * r, axis=0, keepdims=True) * _sigmoid(-lam)
        return (dc, _heads_tn(c, dpre_r), _heads_tn(c, dpre_i), jnp.sum(dpre_r, axis=0, keepdims=True),
                jnp.sum(dpre_i, axis=0, keepdims=True), dlam)

    hshape = (LRU_HEADS, LRU_HEAD_DIM, LRU_HEAD_DIM)
    dc, g["lru_w_a"], g["lru_w_x"], g["lru_b_a"], g["lru_b_x"], g["lru_lambda"] = _rows(
        "b_gates", b_gates,
        [(gh, "row"), (sv["hl"], "row"), (sv["hl"], "prev"), (sv["lrux"], "row"), (sv["lrux"], "prev"),
         (w["conv_w"], "full"), (w["conv_b"], "full"), (w["lru_w_a"], "full"), (w["lru_b_a"], "full"),
         (w["lru_w_x"], "full"), (w["lru_b_x"], "full"), (w["lru_lambda"], "full")],
        [((rows, LRU_WIDTH), BF16, "row"), (hshape, F32, "acc"), (hshape, F32, "acc")] + [((1, LRU_WIDTH), F32, "acc")] * 3,
        rows=rows, tile=tile)

    n_tiles = rows // min(tile, rows)

    def b_conv(i, dcb, dnext, xb, xprev, cw):
        dnext = dnext * (i < n_tiles - 1).astype(F32)
        xprev = xprev * (i > 0).astype(F32)
        dx = dcb * cw[3:4, :]
        dws = [jnp.sum(dcb * xb, axis=0, keepdims=True)]
        for s in range(1, CONV_WIDTH):
            dx = dx + _rows_after(dcb, dnext, s) * cw[3 - s:4 - s, :]
            dws.append(jnp.sum(dcb * _rows_before(xb, xprev, s), axis=0, keepdims=True))
        return dx, jnp.concatenate(dws[::-1], axis=0), jnp.sum(dcb, axis=0, keepdims=True)

    dlrux, g["conv_w"], g["conv_b"] = _rows(
        "b_conv", b_conv, [(dc, "row"), (dc, "next"), (sv["lrux"], "row"), (sv["lrux"], "prev"), (w["conv_w"], "full")],
        [((rows, LRU_WIDTH), BF16, "row"), ((CONV_WIDTH, LRU_WIDTH), F32, "acc"), ((1, LRU_WIDTH), F32, "acc")],
        rows=rows, tile=tile)

    def b_s5post(i, dzs, ysb, gb, wglu, wbs):
        glv, dgelu = _gelu_parts(ysb)
        glu = _nn(glv, wglu)
        ga, gb2 = glu[:, :S5_WIDTH], glu[:, S5_WIDTH:]
        sb = _sigmoid(gb2)
        sg = _sigmoid(gb)
        silu = gb * sg
        y2 = ga * sb * silu
        dy2 = _nt(dzs, wbs)
        dglu = jnp.concatenate([dy2 * sb * silu, dy2 * ga * silu * sb * (1.0 - sb)], axis=1)
        dg = dy2 * ga * sb * sg * (1.0 + gb * (1.0 - sg))
        return _nt(dglu, wglu) * dgelu, dg, _tn(y2, dzs), _tn(glv, dglu)

    dys, ds5g, g["w_bs"], g["w_glu"] = _rows(
        "b_s5post", b_s5post, [(dz_s, "row"), (sv["ys"], "row"), (sv["s5g"], "row"), (w["w_glu"], "full"), (w["w_bs"], "full")],
        [((rows, S5_WIDTH), F32, "row"), ((rows, S5_WIDTH), BF16, "row"), ((S5_WIDTH, d), F32, "acc"),
         ((S5_WIDTH, 2 * S5_WIDTH), F32, "acc")],
        rows=rows, tile=tile)

    ds5x, g["bd"], g["cdt"], g["s5_d"], g["abar"] = _s5_bwd(dys, sv["st"], sv["s5x"], w["bd"], w["cdt"], w["s5_d"],
                                                            w["scb"], rows=rows, tile=ROW_TILE)

    dcomps = [ds5x, ds5g, dlrux, dlrug, dgs, dgl]

    def b_in(i, xb, dx1b, gpre, *rest):
        dproj, ws = jnp.concatenate(rest[:6], axis=1), rest[6:]
        dh = _nt(dproj[:, :IN_SLOT], ws[0])
        for j in range(1, 4):
            dh = dh + _nt(dproj[:, j * IN_SLOT:(j + 1) * IN_SLOT], ws[j])
        rstd = lax.rsqrt(jnp.mean(xb * xb, axis=-1, keepdims=True) + NORM_EPS)
        nrm = xb * rstd
        dn = dh * gpre
        dx = rstd * (dn - nrm * jnp.mean(dn * nrm, axis=-1, keepdims=True))
        return dx1b + dx, jnp.sum(dh * nrm, axis=0, keepdims=True)

    dx, g["g_pre"] = _rows(
        "b_in", b_in, [(sv["x"], "row"), (dx1, "row"), (w["g_pre"], "full")] + [(dcv, "raw") for dcv in dcomps]
        + w["w_in"],
        [((rows, d), F32, "row"), ((1, d), F32, "acc")], rows=rows, tile=ROW_TILE)

    g["w_in"] = []
    for j in range(4):
        lo, hi = j * IN_SLOT, (j + 1) * IN_SLOT
        ks = [k for k in range(6) if IN_OFFSETS[k] < hi and IN_OFFSETS[k + 1] > lo]
        first = IN_OFFSETS[ks[0]]

        def b_win(i, hb, *parts, lo=lo, hi=hi, first=first):
            return (_tn(hb, jnp.concatenate(parts, axis=1)[:, lo - first:hi - first]),)

        g["w_in"].append(_rows("b_win", b_win, [(sv["h"], "raw")] + [(dcomps[k], "raw") for k in ks],
                               [((d, IN_SLOT), F32, "acc")], rows=rows, tile=4 * ROW_TILE)[0])
    return dx, g


SMALL = ("g_pre", "s5_a_re", "s5_a_im", "s5_log_dt", "s5_b_re", "s5_b_im", "s5_c_re", "s5_c_im", "s5_d", "conv_b",
         "lru_w_a", "lru_b_a", "lru_w_x", "lru_b_x", "lru_lambda", "g_post")
BIG = ("w_in", "w_glu", "w_bs", "conv_w", "w_bl", "w_out", "w_ple", "w_ple_gate")


def _bcast_groups(v):
    return jnp.broadcast_to(v[:, None, :], (S5_GROUPS, S5_GROUP, S5_STATE)).reshape(S5_WIDTH, S5_STATE)


def _s5_prep_inputs(wl):
    ldt = jnp.broadcast_to(wl["s5_log_dt"][:, None], (S5_GROUPS, S5_STATE))
    gcn = lambda b: jnp.transpose(b, (0, 2, 1)).reshape(S5_WIDTH, S5_STATE)
    return (_bcast_groups(wl["s5_a_re"]), _bcast_groups(wl["s5_a_im"]), _bcast_groups(ldt), gcn(wl["s5_b_re"]),
            gcn(wl["s5_b_im"]))


def _layer_weights(wl):
    w = {}
    w["w_in"] = [wc if isinstance(wc, tuple) else (wc, "full") for wc in wl["w_in"]]
    for k in ("w_glu", "w_bs", "w_bl", "w_out", "w_ple", "w_ple_gate"):
        w[k] = wl[k]
    w["conv_w"] = wl["conv_w"]
    for k in ("g_pre", "g_post", "s5_d", "conv_b", "lru_b_a", "lru_b_x", "lru_lambda"):
        w[k] = wl[k].reshape(1, -1)
    w["lru_w_a"] = wl["lru_w_a"].astype(BF16)
    w["lru_w_x"] = wl["lru_w_x"].astype(BF16)
    prep_in = _s5_prep_inputs(wl)
    abr, abi, bbr, bbi = _s5_prep(*prep_in)
    w["prep_in"] = prep_in
    shape3 = (S5_GROUPS, S5_GROUP, S5_STATE)
    w["bd"] = _s5_block_diag([bbr.reshape(shape3), bbi.reshape(shape3)]).astype(BF16)
    w["cdt"] = _s5_block_diag([wl["s5_c_re"], -wl["s5_c_im"]]).astype(BF16)
    abr_s = abr.reshape(shape3)[:, 0, :]
    abi_s = abi.reshape(shape3)[:, 0, :]
    w["scf"], w["scb"] = _s5_consts(_cplx_to_lanes(abr_s), _cplx_to_lanes(abi_s), ROW_TILE // SUBLANES)
    return w


def _layer_param_grads(g, w, wl):
    out = {}
    shape3 = (S5_GROUPS, S5_GROUP, S5_STATE)
    dbr, dbi = _s5_block_diag_extract(g["bd"])
    dcr, dci = _s5_block_diag_extract(g["cdt"])
    out["s5_c_re"], out["s5_c_im"] = dcr, -dci
    zeros = jnp.zeros(shape3, F32)
    dar = zeros.at[:, 0, :].set(g["abar"][0].reshape(S5_GROUPS, S5_STATE)).reshape(S5_WIDTH, S5_STATE)
    dai = zeros.at[:, 0, :].set(g["abar"][1].reshape(S5_GROUPS, S5_STATE)).reshape(S5_WIDTH, S5_STATE)
    cts = (dar, dai, dbr.reshape(S5_WIDTH, S5_STATE), dbi.reshape(S5_WIDTH, S5_STATE))
    d_are, d_aim, d_ldt, d_bre, d_bim = _s5_prep_bwd(*w["prep_in"], cts)
    out["s5_a_re"] = d_are.reshape(shape3).sum(axis=1)
    out["s5_a_im"] = d_aim.reshape(shape3).sum(axis=1)
    out["s5_log_dt"] = d_ldt.reshape(shape3).sum(axis=(1, 2))
    out["s5_b_re"] = jnp.transpose(d_bre.reshape(shape3), (0, 2, 1))
    out["s5_b_im"] = jnp.transpose(d_bim.reshape(shape3), (0, 2, 1))
    out["s5_d"] = g["s5_d"].reshape(-1)
    for k in ("g_pre", "g_post", "conv_b", "lru_b_a", "lru_b_x", "lru_lambda"):
        out[k] = g[k].reshape(-1)
    for k in ("lru_w_a", "lru_w_x", "conv_w", "w_in", "w_glu", "w_bs", "w_bl", "w_out", "w_ple", "w_ple_gate"):
        out[k] = g[k]
    return out


def _local_step(x, p, layers, target):
    rows = x.shape[0]
    ws = [_layer_weights(wl) for wl in layers]
    saved = []
    for i in range(DEPTH):
        x, sv = _layer_fwd(x, p[i], ws[i], rows, target if i == DEPTH - 1 else None)
        saved.append(sv)
    dx, sq = x
    loss = sq[0, 0] * (0.5 / D_MODEL)
    grads = [None] * DEPTH
    for i in reversed(range(DEPTH)):
        dx, g = _layer_bwd(dx, saved[i], ws[i], rows)
        grads[i] = _layer_param_grads(g, ws[i], layers[i])
    return loss, dx, grads


def _place():
    return lax.axis_index("x"), lax.axis_index("y"), lax.axis_index("c")


def _other_chips(x, y):
    return [(1 - x, y), (x, 1 - y), (1 - x, 1 - y)]


def _any_spec():
    return pl.BlockSpec(memory_space=pl.ANY)


def _gather_chips(name, vs, via_sibling):
    n = len(vs)
    halved = [via_sibling and v.shape[0] % (16 * 4 // v.dtype.itemsize) == 0 for v in vs]
    relayed = [h and v.shape[0] % (32 * 4 // v.dtype.itemsize) == 0 for h, v in zip(halved, vs)]
    SEMS = 7

    def body(*refs):
        v_refs, out_refs, send_sems, recv_sems = refs[:n], refs[n:2 * n], refs[2 * n], refs[2 * n + 1]
        x, y, c = _place()
        me = 2 * x + y
        chips = _other_chips(x, y)
        slots = [2 * cx + cy for cx, cy in chips]
        sibling = (x, y, 1 - c)

        def part(a, slot, hc, quarter=None):
            if not halved[a]:
                return out_refs[a].at[slot]
            half = vs[a].shape[0] // 2
            if quarter is None:
                return out_refs[a].at[slot, pl.ds(hc * half, half), :]
            return out_refs[a].at[slot, pl.ds(hc * half + quarter * (half // 2), half // 2), :]

        def own(a):
            if not halved[a]:
                return v_refs[a]
            half = vs[a].shape[0] // 2
            return v_refs[a].at[pl.ds(c * half, half), :]

        def copy(a, k, src, dst, to):
            return pltpu.make_async_remote_copy(src_ref=src, dst_ref=dst, send_sem=send_sems.at[SEMS * a + k],
                                                recv_sem=recv_sems.at[SEMS * a + k], device_id=to, device_id_type=MESH)

        def relay(a, k):
            src = part(a, slots[k], c, quarter=k)
            return copy(a, 2 + k, src, src, (*chips[1 - k], c))

        for a in range(n):
            for k in range(2 if relayed[a] else 3):
                copy(a, k, own(a), part(a, me, c), (*chips[k], c)).start()
        for a in range(n):
            for k in range(2 if relayed[a] else 3):
                copy(a, k, own(a), part(a, slots[k], c), (*chips[k], c)).wait_recv()
                if relayed[a]:
                    relay(a, k).start()
                if halved[a]:
                    copy(a, 4 + k, part(a, slots[k], c), part(a, slots[k], c), sibling).start()
        for a in range(n):
            if relayed[a]:
                for k in range(2):
                    copy(a, 2 + k, part(a, slots[2], c, quarter=k), part(a, slots[2], c, quarter=k), sibling).wait_recv()
                copy(a, 6, part(a, slots[2], c), part(a, slots[2], c), sibling).start()
        for a in range(n):
            for k in range(3):
                if halved[a]:
                    copy(a, 4 + k, own(a), part(a, slots[k], 1 - c), sibling).wait_recv()
                    copy(a, 4 + k, own(a), part(a, me, c), sibling).wait_send()
                if relayed[a] and k == 2:
                    for q in range(2):
                        relay(a, q).wait_send()
                else:
                    copy(a, k, own(a), part(a, me, c), sibling).wait_send()

    others = pl.pallas_call(
        body, name=name, out_shape=[jax.ShapeDtypeStruct((4,) + v.shape, v.dtype) for v in vs],
        in_specs=[_any_spec()] * n, out_specs=[_any_spec()] * n,
        scratch_shapes=[pltpu.SemaphoreType.DMA((SEMS * n,)), pltpu.SemaphoreType.DMA((SEMS * n,))],
    )(*vs)
    me = 2 * lax.axis_index("x") + lax.axis_index("y")
    return [lax.dynamic_update_slice(o, v[None], (me, 0, 0)) for o, v in zip(others, vs)]


def _rs_sibling(grs):
    n = len(grs)
    halves = [g.shape[1] // 2 for g in grs]

    def body(*refs):
        g_refs, got_refs, send_sems, recv_sems = refs[:n], refs[n:2 * n], refs[2 * n], refs[2 * n + 1]
        x, y, c = _place()
        copies = [pltpu.make_async_remote_copy(
            src_ref=g_refs[a].at[:, pl.ds((1 - c) * halves[a], halves[a]), :], dst_ref=got_refs[a],
            send_sem=send_sems.at[a], recv_sem=recv_sems.at[a], device_id=(x, y, 1 - c), device_id_type=MESH)
            for a in range(n)]
        for cp in copies:
            cp.start()
        for cp in copies:
            cp.wait()

    return pl.pallas_call(
        body, name="rs_sibling",
        out_shape=[jax.ShapeDtypeStruct((4, h, g.shape[2]), F32) for g, h in zip(grs, halves)],
        in_specs=[_any_spec()] * n, out_specs=[_any_spec()] * n,
        scratch_shapes=[pltpu.SemaphoreType.DMA((n,)), pltpu.SemaphoreType.DMA((n,))],
    )(*grs)


def _rs_add_sibling(pk, got):
    _, half, width = got.shape
    tile = _pack_tile(half, width)
    nb = half // tile

    def body(lo_ref, hi_ref, got_ref, a32_ref, a16_ref):
        mine = jnp.where(lax.axis_index("c") == 0, lo_ref[...], hi_ref[...])
        s = mine + got_ref[...]
        a32_ref[...] = s
        a16_ref[...] = s.astype(BF16)

    blk = lambda first: pl.BlockSpec((None, tile, width), lambda s, i: (s, first + i, 0))
    return pl.pallas_call(
        body, name="rs_add_sibling", grid=(4, nb), in_specs=[blk(0), blk(nb), blk(0)], out_specs=[blk(0), blk(0)],
        out_shape=[jax.ShapeDtypeStruct(got.shape, F32), jax.ShapeDtypeStruct(got.shape, BF16)],
        compiler_params=pltpu.CompilerParams(dimension_semantics=("arbitrary", "arbitrary"), vmem_limit_bytes=VMEM_LIMIT),
    )(pk, pk, got)


def _rs_chips(a16s):
    n = len(a16s)

    def body(*refs):
        a_refs, got_refs, send_sems, recv_sems = refs[:n], refs[n:2 * n], refs[2 * n], refs[2 * n + 1]
        x, y, c = _place()
        chips = _other_chips(x, y)
        copies = [pltpu.make_async_remote_copy(
            src_ref=a_refs[a].at[2 * cx + cy], dst_ref=got_refs[a].at[k], send_sem=send_sems.at[3 * a + k],
            recv_sem=recv_sems.at[3 * a + k], device_id=(cx, cy, c), device_id_type=MESH)
            for a in range(n) for k, (cx, cy) in enumerate(chips)]
        for cp in copies:
            cp.start()
        for cp in copies:
            cp.wait()

    return pl.pallas_call(
        body, name="rs_chips", out_shape=[jax.ShapeDtypeStruct((3,) + a.shape[1:], a.dtype) for a in a16s],
        in_specs=[_any_spec()] * n, out_specs=[_any_spec()] * n,
        scratch_shapes=[pltpu.SemaphoreType.DMA((3 * n,)), pltpu.SemaphoreType.DMA((3 * n,))],
    )(*a16s)


def _swap_halves(vs):
    n = len(vs)

    def body(*refs):
        v_refs, out_refs, send_sems, recv_sems = refs[:n], refs[n:2 * n], refs[2 * n], refs[2 * n + 1]
        x, y, c = _place()

        def give(a, hc):
            return pltpu.make_async_remote_copy(src_ref=v_refs[a], dst_ref=out_refs[a].at[hc], send_sem=send_sems.at[a],
                                                recv_sem=recv_sems.at[a], device_id=(x, y, 1 - c), device_id_type=MESH)

        for a in range(n):
            give(a, c).start()
        for a in range(n):
            give(a, c).wait_send()
            give(a, 1 - c).wait_recv()

    others = pl.pallas_call(
        body, name="swap_halves", out_shape=[jax.ShapeDtypeStruct((2,) + v.shape, v.dtype) for v in vs],
        in_specs=[_any_spec()] * n, out_specs=[_any_spec()] * n,
        scratch_shapes=[pltpu.SemaphoreType.DMA((n,)), pltpu.SemaphoreType.DMA((n,))],
    )(*vs)
    c = lax.axis_index("c")
    return [lax.dynamic_update_slice(o, v[None], (c, 0, 0)).reshape(2 * v.shape[0], v.shape[1]) for o, v in zip(others, vs)]


WIDE = 1024


def _unpack(flat, shapes, align=1):
    out, off = [], 0
    for s in shapes:
        n = 1
        for q in s:
            n *= q
        out.append(flat[off:off + n].reshape(s))
        off += -(-n // align) * align
    return out


def _pack_tile(rows, width):
    most = (2 ** 21) // (4 * width)
    if rows <= most:
        return rows
    return max(t for t in range(16, most + 1, 16) if rows % t == 0)


def _flat_aligned(v, align):
    v = v.reshape(-1)
    return jnp.pad(v, (0, -v.shape[0] % align))


def _adamw(name, w, g, m, v, tile):
    def fn(i, wb, gb, mb, vb):
        m2 = ADAM_B1 * mb + (1.0 - ADAM_B1) * gb
        v2 = ADAM_B2 * vb + (1.0 - ADAM_B2) * (gb * gb)
        m_hat = m2 / (1.0 - ADAM_B1 ** ADAM_STEP)
        v_hat = v2 / (1.0 - ADAM_B2 ** ADAM_STEP)
        return -ADAM_LR * (m_hat / (jnp.sqrt(v_hat) + ADAM_EPS) + ADAM_WD * wb), m2, v2

    return _rows(name, fn, [(w, "row"), (g, "row"), (m, "row"), (v, "row")], [(w.shape, F32, "row")] * 3,
                 rows=w.shape[0], tile=tile)


def _as_2d(a):
    return a.reshape(-1, a.shape[-1])


def _adam_tile(rows):
    for t in (256, 184, 128, 64, 32, 16, 8):
        if rows % t == 0:
            return t
    return rows


def kernel(x, p, g_pre, w_in, s5_a_re, s5_a_im, s5_log_dt, s5_b_re, s5_b_im, s5_c_re, s5_c_im, s5_d, w_glu, w_bs, conv_w, conv_b, lru_w_a, lru_b_a, lru_w_x, lru_b_x, lru_lambda, w_bl, w_out, g_post, w_ple, w_ple_gate, loss_target, m_g_pre, m_w_in, m_s5_a_re, m_s5_a_im, m_s5_log_dt, m_s5_b_re, m_s5_b_im, m_s5_c_re, m_s5_c_im, m_s5_d, m_w_glu, m_w_bs, m_conv_w, m_conv_b, m_lru_w_a, m_lru_b_a, m_lru_w_x, m_lru_b_x, m_lru_lambda, m_w_bl, m_w_out, m_g_post, m_w_ple, m_w_ple_gate, v_g_pre, v_w_in, v_s5_a_re, v_s5_a_im, v_s5_log_dt, v_s5_b_re, v_s5_b_im, v_s5_c_re, v_s5_c_im, v_s5_d, v_w_glu, v_w_bs, v_conv_w, v_conv_b, v_lru_w_a, v_lru_b_a, v_lru_w_x, v_lru_b_x, v_lru_lambda, v_w_bl, v_w_out, v_g_post, v_w_ple, v_w_ple_gate):
    wts = dict(g_pre=g_pre, w_in=w_in, s5_a_re=s5_a_re, s5_a_im=s5_a_im, s5_log_dt=s5_log_dt, s5_b_re=s5_b_re,
               s5_b_im=s5_b_im, s5_c_re=s5_c_re, s5_c_im=s5_c_im, s5_d=s5_d, w_glu=w_glu, w_bs=w_bs, conv_w=conv_w,
               conv_b=conv_b, lru_w_a=lru_w_a, lru_b_a=lru_b_a, lru_w_x=lru_w_x, lru_b_x=lru_b_x, lru_lambda=lru_lambda,
               w_bl=w_bl, w_out=w_out, g_post=g_post, w_ple=w_ple, w_ple_gate=w_ple_gate)
    mom1 = dict(g_pre=m_g_pre, w_in=m_w_in, s5_a_re=m_s5_a_re, s5_a_im=m_s5_a_im, s5_log_dt=m_s5_log_dt, s5_b_re=m_s5_b_re,
                s5_b_im=m_s5_b_im, s5_c_re=m_s5_c_re, s5_c_im=m_s5_c_im, s5_d=m_s5_d, w_glu=m_w_glu, w_bs=m_w_bs,
                conv_w=m_conv_w, conv_b=m_conv_b, lru_w_a=m_lru_w_a, lru_b_a=m_lru_b_a, lru_w_x=m_lru_w_x, lru_b_x=m_lru_b_x,
                lru_lambda=m_lru_lambda, w_bl=m_w_bl, w_out=m_w_out, g_post=m_g_post, w_ple=m_w_ple, w_ple_gate=m_w_ple_gate)
    mom2 = dict(g_pre=v_g_pre, w_in=v_w_in, s5_a_re=v_s5_a_re, s5_a_im=v_s5_a_im, s5_log_dt=v_s5_log_dt, s5_b_re=v_s5_b_re,
                s5_b_im=v_s5_b_im, s5_c_re=v_s5_c_re, s5_c_im=v_s5_c_im, s5_d=v_s5_d, w_glu=v_w_glu, w_bs=v_w_bs,
                conv_w=v_conv_w, conv_b=v_conv_b, lru_w_a=v_lru_w_a, lru_b_a=v_lru_b_a, lru_w_x=v_lru_w_x, lru_b_x=v_lru_b_x,
                lru_lambda=v_lru_lambda, w_bl=v_w_bl, w_out=v_w_out, g_post=v_g_post, w_ple=v_w_ple, w_ple_gate=v_w_ple_gate)
    names = list(wts)

    by_rows, by_cols = ("w_bl", "w_out", "w_ple_gate"), ("w_glu", "w_bs", "w_ple")
    two_d = lambda k: wts[k].astype(BF16).reshape(-1, wts[k].shape[2])
    sent = [two_d("w_in"), jnp.concatenate([two_d(k) for k in by_rows]), jnp.concatenate([two_d(k) for k in by_cols]),
            wts["conv_w"].reshape(-1, wts["conv_w"].shape[2])]
    g_in, g_rows, g_cols, g_conv = _gather_chips("gather_weights", sent, via_sibling=True)
    whole = {"conv_w": jnp.transpose(g_conv.reshape((4,) + wts["conv_w"].shape), (1, 2, 0, 3)).reshape(DEPTH, CONV_WIDTH, -1)}
    off = 0
    for k in by_rows:
        dp, r, _ = wts[k].shape
        piece = g_rows[:, off:off + dp * r].reshape(4, dp, r, -1)
        whole[k] = jnp.transpose(piece, (1, 0, 2, 3)).reshape(dp, 4 * r, -1)
        off += dp * r
    off = 0
    for k in by_cols:
        dp, r, cs = wts[k].shape
        piece = g_cols[:, off:off + dp * r].reshape(4, dp, r, cs)
        whole[k] = jnp.transpose(piece, (1, 2, 0, 3)).reshape(dp, r, 4 * cs)
        off += dp * r
    layers = []
    for i in range(DEPTH):
        wl = {k: whole[k][i] for k in BIG if k != "w_in"}
        wl["w_in"] = [(g_in, ("part", j, i, wts["w_in"].shape[1])) for j in range(4)]
        wl.update({k: wts[k][i] for k in SMALL})
        layers.append(wl)

    loss, grad_x, grads = _local_step(x[0], p[:, 0], layers, loss_target[0])
    loss = lax.psum(loss, ("x", "y", "c"))

    me = 2 * lax.axis_index("x") + lax.axis_index("y")
    rows_of = lambda k, i, j: grads[i][k][j * (grads[i][k].shape[0] // 4):(j + 1) * (grads[i][k].shape[0] // 4)]
    cols_of = lambda k, i, j: grads[i][k][:, j * (grads[i][k].shape[1] // 4):(j + 1) * (grads[i][k].shape[1] // 4)]
    layer_range = range(DEPTH)
    packs = [
        jnp.stack([jnp.concatenate([grads[i]["w_in"][j] for i in layer_range]) for j in range(4)]),
        jnp.stack([jnp.concatenate([rows_of(k, i, j) for k in by_rows for i in layer_range]) for j in range(4)]),
        jnp.stack([jnp.concatenate([cols_of(k, i, j) for k in by_cols for i in layer_range]) for j in range(4)]),
    ]
    small_names = SMALL + ("conv_w",)
    small_shapes = [(DEPTH,) + grads[0][k].shape for k in small_names]
    small_flat = jnp.concatenate([_flat_aligned(jnp.stack([grads[i][k] for i in layer_range]), WIDE) for k in small_names])
    n_small = small_flat.shape[0]
    small_q = -(-n_small // (4 * 32 * LANES)) * 32 * LANES
    packs.append(jnp.pad(small_flat, (0, 4 * small_q - n_small)).reshape(4, small_q // LANES, LANES))

    gots = _rs_sibling(packs)
    a32s, a16s = [], []
    for pk, got in zip(packs, gots):
        a32, a16 = _rs_add_sibling(pk, got)
        a32s.append(a32)
        a16s.append(a16)
    got3s = _rs_chips(a16s)
    red_halves = []
    for a32, got3 in zip(a32s, got3s):
        half, width = a32.shape[1], a32.shape[2]
        own = lax.dynamic_index_in_dim(a32, me, 0, keepdims=False)

        def f_add2(i, o, g0, g1, g2):
            return (((o + g0) + g1) + g2,)

        red_halves.append(_rows("rs_add2", f_add2, [(own, "row")] + [(got3[k], "row") for k in range(3)],
                                [((half, width), F32, "row")], rows=half, tile=_pack_tile(half, width))[0])
    reds = _swap_halves(red_halves)
    small_red = _gather_chips("gather_small", [reds[3]], via_sibling=False)[0].reshape(-1)[:n_small]

    grad_out = {"w_in": reds[0].reshape(wts["w_in"].shape)}
    for red, ks in ((reds[1], by_rows), (reds[2], by_cols)):
        off = 0
        for k in ks:
            n = wts[k].shape[0] * wts[k].shape[1]
            grad_out[k] = red[off:off + n].reshape(wts[k].shape)
            off += n
    small_out = dict(zip(small_names, _unpack(small_red, small_shapes, align=WIDE)))
    grad_out.update({k: small_out[k] for k in SMALL})
    grad_out["conv_w"] = lax.dynamic_slice_in_dim(small_out["conv_w"], me * wts["conv_w"].shape[2], wts["conv_w"].shape[2], axis=2)
    delta, new_m, new_v = {}, {}, {}
    for k in BIG + SMALL:
        w2 = _as_2d(wts[k])
        res = _adamw("adamw_" + k, w2, _as_2d(grad_out[k]), _as_2d(mom1[k]), _as_2d(mom2[k]), _adam_tile(w2.shape[0]))
        delta[k], new_m[k], new_v[k] = [r.reshape(wts[k].shape) for r in res]
    return (loss, grad_x[None], *[grad_out[k] for k in names], *[delta[k] for k in names],
            *[new_m[k] for k in names], *[new_v[k] for k in names])
```

```python
import jax
import jax.numpy as jnp
from jax import lax
from jax.experimental import pallas as pl
from jax.experimental.pallas import tpu as pltpu

F32 = jnp.float32
BF16 = jnp.bfloat16
MESH = pl.DeviceIdType.MESH

DEPTH = 2
D_MODEL = 1024
NORM_EPS = 1e-6
S5_WIDTH = 512
S5_GROUPS = 32
S5_GROUP = 16
S5_STATE = 64
LRU_WIDTH = 1280
LRU_HEADS = 10
LRU_HEAD_DIM = 128
LRU_C = 8.0
CONV_WIDTH = 4
PLE_DIM = 256
IN_WIDTHS = (S5_WIDTH, S5_WIDTH, LRU_WIDTH, LRU_WIDTH, D_MODEL, D_MODEL)
IN_OFFSETS = (0, 512, 1024, 2304, 3584, 4608, 5632)
IN_SLOT = 5632 // 4
ADAM_LR = 0.001
ADAM_B1 = 0.9
ADAM_B2 = 0.999
ADAM_EPS = 1e-08
ADAM_WD = 0.01
ADAM_STEP = 10

SUBLANES = 8
LANES = 128
S5_HALF_IN = S5_WIDTH // 2
S5_CPLX = S5_GROUPS * S5_STATE
S5_HALF_CPLX = S5_CPLX // 2
S5_LANES = 2 * S5_CPLX
VMEM_LIMIT = 48 * 2 ** 20
ROW_TILE = 256
WIDE_TILE = 512


def _sigmoid(x):
    return 0.5 * jnp.tanh(0.5 * x) + 0.5


def _gelu_parts(x):
    k = 0.7978845608028654
    t = jnp.tanh(k * (x + 0.044715 * x * x * x))
    val = 0.5 * x * (1.0 + t)
    grad = 0.5 * (1.0 + t) + 0.5 * x * (1.0 - t * t) * k * (1.0 + 3.0 * 0.044715 * x * x)
    return val, grad


def _nn(a, w):
    return jnp.dot(a.astype(BF16), w.astype(BF16), preferred_element_type=F32)


def _nt(a, w):
    return lax.dot_general(a.astype(BF16), w.astype(BF16), (((1,), (1,)), ((), ())), preferred_element_type=F32)


def _tn(a, b):
    return lax.dot_general(a.astype(BF16), b.astype(BF16), (((0,), (0,)), ((), ())), preferred_element_type=F32)


def _heads(op, a, w):
    d = LRU_HEAD_DIM
    return jnp.concatenate([op(a[:, h * d:(h + 1) * d], w[h]) for h in range(LRU_HEADS)], axis=1)


def _heads_tn(a, b):
    d = LRU_HEAD_DIM
    return jnp.stack([_tn(a[:, h * d:(h + 1) * d], b[:, h * d:(h + 1) * d]) for h in range(LRU_HEADS)], axis=0)


def _rows_before(x, halo, s):
    main = pltpu.roll(x, s, 0)
    head = pltpu.roll(jnp.concatenate([halo, x[0:SUBLANES]], axis=0), s, 0)[SUBLANES:2 * SUBLANES]
    return jnp.concatenate([head, main[SUBLANES:]], axis=0)


def _rows_after(x, halo, s):
    n = x.shape[0]
    main = pltpu.roll(x, n - s, 0)
    tail = pltpu.roll(jnp.concatenate([x[n - SUBLANES:], halo], axis=0), 2 * SUBLANES - s, 0)[0:SUBLANES]
    return jnp.concatenate([main[:n - SUBLANES], tail], axis=0)


def _rows(name, fn, ins, outs, *, rows, tile):
    tile = min(tile, rows)
    n = rows // tile
    assert n * tile == rows, (name, rows, tile)
    in_specs = []
    for arr, kind in ins:
        halo = SUBLANES * (4 // arr.dtype.itemsize)
        per, last = tile // halo, rows // halo - 1
        if isinstance(kind, tuple):
            _, j, k, r = kind
            in_specs.append(pl.BlockSpec((None, r, arr.shape[2]), lambda i, j=j, k=k: (j, k, 0)))
        elif kind in ("row", "raw"):
            in_specs.append(pl.BlockSpec((tile, arr.shape[1]), lambda i: (i, 0)))
        elif kind == "prev":
            in_specs.append(pl.BlockSpec((halo, arr.shape[1]), lambda i, per=per: (jnp.maximum(i * per - 1, 0), 0)))
        elif kind == "next":
            in_specs.append(pl.BlockSpec((halo, arr.shape[1]),
                                         lambda i, per=per, last=last: (jnp.minimum((i + 1) * per, last), 0)))
        else:
            in_specs.append(pl.BlockSpec(arr.shape, lambda i, nd=arr.ndim: (0,) * nd))
    out_shape, out_specs = [], []
    for shape, dtype, kind in outs:
        out_shape.append(jax.ShapeDtypeStruct(shape, dtype))
        if kind == "row":
            out_specs.append(pl.BlockSpec((tile, shape[1]), lambda i: (i, 0)))
        else:
            out_specs.append(pl.BlockSpec(shape, lambda i, nd=len(shape): (0,) * nd))
    n_in = len(ins)

    def load(ref, kind):
        v = ref[...]
        if kind in ("row", "prev", "next"):
            v = v.astype(F32)
        if kind == "prev":
            v = v[v.shape[0] - SUBLANES:]
        if kind == "next":
            v = v[:SUBLANES]
        return v

    def body(*refs):
        i = pl.program_id(0)
        vals = fn(i, *[load(r, kind) for r, (_, kind) in zip(refs[:n_in], ins)])
        assert len(vals) == len(outs), name
        for r, v, (_, _, kind) in zip(refs[n_in:], vals, outs):
            if kind == "row":
                r[...] = v.astype(r.dtype)
            else:
                @pl.when(i == 0)
                def _():
                    r[...] = jnp.zeros_like(r)

                r[...] += v.astype(r.dtype)

    return pl.pallas_call(
        body, name=name, grid=(n,), in_specs=in_specs, out_specs=out_specs, out_shape=out_shape,
        compiler_params=pltpu.CompilerParams(dimension_semantics=("arbitrary",), vmem_limit_bytes=VMEM_LIMIT),
    )(*[a for a, _ in ins])


def _s5_discretise(are, aim, ldt, bre, bim):
    dt = jnp.exp(ldt)
    er = jnp.exp(are * dt)
    abr = er * jnp.cos(aim * dt)
    abi = er * jnp.sin(aim * dt)
    den = are * are + aim * aim
    zr = ((abr - 1.0) * are + abi * aim) / den
    zi = (abi * are - (abr - 1.0) * aim) / den
    return abr, abi, zr * bre - zi * bim, zr * bim + zi * bre


def _s5_prep(are, aim, ldt, bre, bim):
    def body(a, b, c, d, e, o0, o1, o2, o3):
        r = _s5_discretise(a[...], b[...], c[...], d[...], e[...])
        o0[...], o1[...], o2[...], o3[...] = r

    sd = jax.ShapeDtypeStruct(are.shape, F32)
    return pl.pallas_call(body, name="s5_prep", out_shape=[sd] * 4)(are, aim, ldt, bre, bim)


def _s5_prep_bwd(are, aim, ldt, bre, bim, cts):
    def body(a, b, c, d, e, c0, c1, c2, c3, o0, o1, o2, o3, o4):
        _, vjp = jax.vjp(_s5_discretise, a[...], b[...], c[...], d[...], e[...])
        r = vjp((c0[...], c1[...], c2[...], c3[...]))
        o0[...], o1[...], o2[...], o3[...], o4[...] = r

    sd = jax.ShapeDtypeStruct(are.shape, F32)
    return pl.pallas_call(body, name="s5_prep_bwd", out_shape=[sd] * 5)(are, aim, ldt, bre, bim, *cts)


def _s5_consts(abr, abi, seg):
    shape = (SUBLANES, S5_CPLX)
    assert seg & (seg - 1) == 0 and seg % SUBLANES == 0, seg

    def body(ar_ref, ai_ref, f_ref, b_ref):
        def cmul(p, q):
            return (p[0] * q[0] - p[1] * q[1], p[0] * q[1] + p[1] * q[0])

        row = lax.broadcasted_iota(jnp.int32, shape, 0)
        a1 = (jnp.broadcast_to(ar_ref[...], shape), jnp.broadcast_to(ai_ref[...], shape))
        squares = [a1]
        while 1 << (len(squares) - 1) < 4 * seg:
            squares.append(cmul(squares[-1], squares[-1]))
        nb = seg.bit_length() - 1
        fwd, rev = [], []
        for k, a in ((1, squares[nb]), (2, squares[nb + 1]), (4, squares[nb + 2])):
            fwd += [jnp.where(row >= k, a[0], 0.0), jnp.where(row >= k, a[1], 0.0)]
            rev += [jnp.where(row <= 7 - k, a[0], 0.0), jnp.where(row <= 7 - k, -a[1], 0.0)]
        fwd += [a1[0], a1[1]]
        rev += [a1[0], -a1[1]]
        e = lax.broadcasted_iota(jnp.int32, (seg, S5_CPLX), 0) + 1
        wide = lambda v: jnp.broadcast_to(v[0:1, :], (seg, S5_CPLX))
        pr, pi = jnp.ones((seg, S5_CPLX), F32), jnp.zeros((seg, S5_CPLX), F32)
        for b in range(nb + 1):
            sr, si = wide(squares[b][0]), wide(squares[b][1])
            bit = ((e >> b) & 1) == 1
            pr, pi = jnp.where(bit, pr * sr - pi * si, pr), jnp.where(bit, pr * si + pi * sr, pi)
        f_ref[...] = jnp.concatenate(fwd + [pr, pi], axis=0)
        b_ref[...] = jnp.concatenate(rev + [pr, -pi], axis=0)

    sd = jax.ShapeDtypeStruct((8 * SUBLANES + 2 * seg, S5_CPLX), F32)
    return pl.pallas_call(body, name="s5_consts", out_shape=[sd, sd])(abr, abi)


S5_TILES = S5_LANES // LANES
S5_HALF_TILES = S5_TILES // 2


def _s5_tile_index(q):
    re = (q // 8) * S5_HALF_TILES + (q % 8)
    return re, re + S5_HALF_TILES // 2


def _lanes_of(ref, first, count):
    return jnp.concatenate([ref[j] for j in range(first, first + count)], axis=1)


def _to_lane_tiles(ref, first, value):
    for j in range(value.shape[1] // LANES):
        ref[first + j] = value[:, j * LANES:(j + 1) * LANES]


def _time_perm(tile, transpose=False):
    seg = tile // SUBLANES
    rho = lax.broadcasted_iota(jnp.int32, (tile, tile), 1 if transpose else 0)
    t = lax.broadcasted_iota(jnp.int32, (tile, tile), 0 if transpose else 1)
    return (t == (rho & (SUBLANES - 1)) * seg + (rho >> 3)).astype(BF16)


def _reorder(perm, x):
    return jnp.dot(perm, x, preferred_element_type=F32)


def _s5_scan(s_ref, sc_ref, carry_ref, tile, reverse):
    seg = tile // SUBLANES
    group = 4
    edge = 0 if reverse else SUBLANES - 1
    row = lax.broadcasted_iota(jnp.int32, (SUBLANES, LANES), 0)
    order = range(seg - 1, -1, -1) if reverse else range(seg)
    rows_of = lambda k: pl.ds(k * SUBLANES, SUBLANES)
    base = 8 * SUBLANES

    for q0 in range(0, S5_CPLX // LANES, group):
        qs = list(range(q0, q0 + group))
        tiles = [_s5_tile_index(q) for q in qs]
        cst = lambda k, q: sc_ref[k * SUBLANES:(k + 1) * SUBLANES, q * LANES:(q + 1) * LANES]
        state = [(jnp.zeros((SUBLANES, LANES), F32), jnp.zeros((SUBLANES, LANES), F32)) for _ in qs]
        mult = [(cst(6, q), cst(7, q)) for q in qs]
        for k in order:
            for j, (re, im) in enumerate(tiles):
                ar, ai = mult[j]
                xr, xi = state[j]
                nr = ar * xr - ai * xi + s_ref[re, rows_of(k), :]
                ni = ar * xi + ai * xr + s_ref[im, rows_of(k), :]
                s_ref[re, rows_of(k), :] = nr
                s_ref[im, rows_of(k), :] = ni
                state[j] = (nr, ni)
        start = []
        for j, (q, (re, im)) in enumerate(zip(qs, tiles)):
            er, ei = state[j]
            shift1 = SUBLANES - 1 if reverse else 1
            dr = jnp.where(row == SUBLANES - 1 - edge, carry_ref[re], pltpu.roll(er, shift1, 0))
            di = jnp.where(row == SUBLANES - 1 - edge, carry_ref[im], pltpu.roll(ei, shift1, 0))
            for c, sh in ((0, 1), (2, 2), (4, 4)):
                shift = SUBLANES - sh if reverse else sh
                ar, ai = cst(c, q), cst(c + 1, q)
                sr, si = pltpu.roll(dr, shift, 0), pltpu.roll(di, shift, 0)
                dr, di = dr + ar * sr - ai * si, di + ar * si + ai * sr
            start.append((dr, di))
        for k in order:
            t = seg - 1 - k if reverse else k
            for j, (q, (re, im)) in enumerate(zip(qs, tiles)):
                lanes = slice(q * LANES, (q + 1) * LANES)
                pr = jnp.broadcast_to(sc_ref[base + t:base + t + 1, lanes], (SUBLANES, LANES))
                pi = jnp.broadcast_to(sc_ref[base + seg + t:base + seg + t + 1, lanes], (SUBLANES, LANES))
                cr, ci = start[j]
                xr = s_ref[re, rows_of(k), :] + pr * cr - pi * ci
                xi = s_ref[im, rows_of(k), :] + pr * ci + pi * cr
                s_ref[re, rows_of(k), :] = xr
                s_ref[im, rows_of(k), :] = xi
                if k == order[-1]:
                    carry_ref[re] = jnp.broadcast_to(xr[edge:edge + 1, :], (SUBLANES, LANES))
                    carry_ref[im] = jnp.broadcast_to(xi[edge:edge + 1, :], (SUBLANES, LANES))


def _s5_fwd(u, bd, cdt, dskip, sc, *, rows, tile):
    n = rows // tile

    def body(u_ref, bd_ref, cdt_ref, d_ref, sc_ref, y_ref, s_ref, carry_ref):
        @pl.when(pl.program_id(0) == 0)
        def _():
            carry_ref[...] = jnp.zeros_like(carry_ref)

        ub = _reorder(_time_perm(tile), u_ref[...].astype(BF16)).astype(BF16)
        for h in range(2):
            _to_lane_tiles(s_ref, h * S5_HALF_TILES, jnp.dot(ub[:, h * S5_HALF_IN:(h + 1) * S5_HALF_IN], bd_ref[h],
                                                             preferred_element_type=F32))
        _s5_scan(s_ref, sc_ref, carry_ref, tile, reverse=False)
        ys = [_nt(_lanes_of(s_ref, h * S5_HALF_TILES, S5_HALF_TILES), cdt_ref[h]) for h in range(2)]
        y = _reorder(_time_perm(tile, transpose=True), jnp.concatenate(ys, axis=1).astype(BF16))
        y_ref[...] = y + d_ref[...] * u_ref[...]

    full = lambda a: pl.BlockSpec(a.shape, lambda i, nd=a.ndim: (0,) * nd)
    return pl.pallas_call(
        body, name="s5_fwd", grid=(n,),
        in_specs=[pl.BlockSpec((tile, S5_WIDTH), lambda i: (i, 0)), full(bd), full(cdt), full(dskip), full(sc)],
        out_specs=[pl.BlockSpec((tile, S5_WIDTH), lambda i: (i, 0)),
                   pl.BlockSpec((S5_TILES, tile, LANES), lambda i: (0, i, 0))],
        out_shape=[jax.ShapeDtypeStruct((rows, S5_WIDTH), F32), jax.ShapeDtypeStruct((S5_TILES, rows, LANES), F32)],
        scratch_shapes=[pltpu.VMEM((S5_TILES, SUBLANES, LANES), F32)],
        compiler_params=pltpu.CompilerParams(dimension_semantics=("arbitrary",), vmem_limit_bytes=VMEM_LIMIT),
    )(u, bd, cdt, dskip, sc)


def _s5_bwd(dy, s, u, bd, cdt, dskip, sc, *, rows, tile):
    n = rows // tile
    hc = 2 * S5_HALF_CPLX
    per8 = tile // SUBLANES
    quarter = S5_HALF_TILES // 2

    def body(dy_ref, s_ref, sp_ref, u_ref, bd_ref, cdt_ref, d_ref, sc_ref,
             du_ref, dbd_ref, dcdt_ref, dd_ref, da_ref, g_ref, carry_ref):
        i = pl.program_id(0)

        @pl.when(i == 0)
        def _():
            carry_ref[...] = jnp.zeros_like(carry_ref)
            dbd_ref[...] = jnp.zeros_like(dbd_ref)
            dcdt_ref[...] = jnp.zeros_like(dcdt_ref)
            dd_ref[...] = jnp.zeros_like(dd_ref)
            da_ref[...] = jnp.zeros_like(da_ref)

        dy = dy_ref[...]
        u = u_ref[...]
        perm = _time_perm(tile)
        dyb = _reorder(perm, dy.astype(BF16)).astype(BF16)
        ub = _reorder(perm, u.astype(BF16)).astype(BF16)
        for h in range(2):
            _to_lane_tiles(g_ref, h * S5_HALF_TILES, jnp.dot(dyb[:, h * S5_HALF_IN:(h + 1) * S5_HALF_IN], cdt_ref[h],
                                                             preferred_element_type=F32))
        _s5_scan(g_ref, sc_ref, carry_ref, tile, reverse=True)
        dus = []
        for h in range(2):
            gb = _lanes_of(g_ref, h * S5_HALF_TILES, S5_HALF_TILES).astype(BF16)
            sb = _lanes_of(s_ref, h * S5_HALF_TILES, S5_HALF_TILES).astype(BF16)
            dus.append(_nt(gb, bd_ref[h]))
            dbd_ref[h] += _tn(ub[:, h * S5_HALF_IN:(h + 1) * S5_HALF_IN], gb)
            dcdt_ref[h] += _tn(dyb[:, h * S5_HALF_IN:(h + 1) * S5_HALF_IN], sb)
        du = _reorder(_time_perm(tile, transpose=True), jnp.concatenate(dus, axis=1).astype(BF16))
        du_ref[...] = (du + d_ref[...] * dy).astype(du_ref.dtype)
        dd_ref[...] += jnp.sum(dy * u, axis=0, keepdims=True)

        not_first = (i < n - 1).astype(F32)
        row = lax.broadcasted_iota(jnp.int32, (SUBLANES, quarter * LANES), 0)

        def step_before(first):
            cur = _lanes_of(s_ref, first, quarter)
            before_tile = _lanes_of(sp_ref, first, quarter)[SUBLANES - 1:SUBLANES, :] * not_first
            head = jnp.where(row == 0, before_tile, pltpu.roll(cur[tile - SUBLANES:], 1, 0))
            return jnp.concatenate([head, cur[:tile - SUBLANES]], axis=0)

        for h in range(2):
            re, im = h * S5_HALF_TILES, h * S5_HALF_TILES + quarter
            ssr = step_before(re)
            ssi = step_before(im)
            gr = _lanes_of(g_ref, re, quarter)
            gi = _lanes_of(g_ref, im, quarter)
            lanes = slice(h * S5_HALF_CPLX, (h + 1) * S5_HALF_CPLX)
            da_ref[0:1, lanes] += jnp.sum(ssr * gr + ssi * gi, axis=0, keepdims=True)
            da_ref[1:2, lanes] += jnp.sum(ssr * gi - ssi * gr, axis=0, keepdims=True)

    full = lambda a: pl.BlockSpec(a.shape, lambda i, nd=a.ndim: (0,) * nd)
    rev = lambda i: (n - 1 - i, 0)
    wshape = (2, S5_HALF_IN, hc)
    return pl.pallas_call(
        body, name="s5_bwd", grid=(n,),
        in_specs=[pl.BlockSpec((tile, S5_WIDTH), rev), pl.BlockSpec((S5_TILES, tile, LANES), lambda i: (0, n - 1 - i, 0)),
                  pl.BlockSpec((S5_TILES, SUBLANES, LANES), lambda i: (0, jnp.maximum((n - 1 - i) * per8 - 1, 0), 0)),
                  pl.BlockSpec((tile, S5_WIDTH), rev), full(bd), full(cdt), full(dskip), full(sc)],
        out_specs=[pl.BlockSpec((tile, S5_WIDTH), rev),
                   pl.BlockSpec(wshape, lambda i: (0, 0, 0)), pl.BlockSpec(wshape, lambda i: (0, 0, 0)),
                   pl.BlockSpec((1, S5_WIDTH), lambda i: (0, 0)), pl.BlockSpec((SUBLANES, S5_CPLX), lambda i: (0, 0))],
        out_shape=[jax.ShapeDtypeStruct((rows, S5_WIDTH), BF16), jax.ShapeDtypeStruct(wshape, F32),
                   jax.ShapeDtypeStruct(wshape, F32), jax.ShapeDtypeStruct((1, S5_WIDTH), F32),
                   jax.ShapeDtypeStruct((SUBLANES, S5_CPLX), F32)],
        scratch_shapes=[pltpu.VMEM((S5_TILES, tile, LANES), F32), pltpu.VMEM((S5_TILES, SUBLANES, LANES), F32)],
        compiler_params=pltpu.CompilerParams(dimension_semantics=("arbitrary",), vmem_limit_bytes=VMEM_LIMIT),
    )(dy, s, s, u, bd, cdt, dskip, sc)


def _s5_block_diag(parts):
    v = jnp.stack(parts, axis=2).reshape(2, 16, S5_GROUP, 2, S5_STATE)
    eye = jnp.eye(16, dtype=v.dtype)
    return jnp.einsum("hgcpn,gk->hgcpkn", v, eye).reshape(2, S5_HALF_IN, 2 * S5_HALF_CPLX)


def _s5_block_diag_extract(m):
    v = m.reshape(2, 16, S5_GROUP, 2, 16, S5_STATE)
    d = jnp.diagonal(v, axis1=1, axis2=4)
    d = jnp.transpose(d, (2, 0, 4, 1, 3)).reshape(2, S5_GROUPS, S5_GROUP, S5_STATE)
    return d[0], d[1]


def _cplx_to_lanes(v):
    return v.reshape(1, S5_CPLX)


def _lru_scan_fwd(a, b, *, rows, tile):
    n = rows // tile
    nblk = tile // SUBLANES
    group = 5

    def body(a_ref, b_ref, h_ref, carry_ref):
        @pl.when(pl.program_id(0) == 0)
        def _():
            carry_ref[...] = jnp.zeros_like(carry_ref)

        row = lax.broadcasted_iota(jnp.int32, (SUBLANES, LANES), 0)
        for q0 in range(0, LRU_WIDTH // LANES, group):
            offs = [q * LANES for q in range(q0, q0 + group)]

            def blk(t, carry, offs=offs):
                r0 = pl.multiple_of(t * SUBLANES, SUBLANES)
                new = []
                for j, o in enumerate(offs):
                    av = a_ref[pl.ds(r0, SUBLANES), o:o + LANES]
                    xv = b_ref[pl.ds(r0, SUBLANES), o:o + LANES]
                    for sh in (1, 2, 4):
                        m = row >= sh
                        xs = pltpu.roll(xv, sh, 0)
                        asft = pltpu.roll(av, sh, 0)
                        xv = xv + jnp.where(m, av * xs, 0.0)
                        av = jnp.where(m, av * asft, av)
                    hv = xv + av * carry[j]
                    h_ref[pl.ds(r0, SUBLANES), o:o + LANES] = hv
                    new.append(jnp.broadcast_to(hv[SUBLANES - 1:SUBLANES, :], (SUBLANES, LANES)))
                return tuple(new)

            carry = lax.fori_loop(0, nblk, blk, tuple(carry_ref[:, o:o + LANES] for o in offs), unroll=2)
            for j, o in enumerate(offs):
                carry_ref[:, o:o + LANES] = carry[j]

    spec = pl.BlockSpec((tile, LRU_WIDTH), lambda i: (i, 0))
    return pl.pallas_call(
        body, name="lru_scan_fwd", grid=(n,), in_specs=[spec, spec], out_specs=spec,
        out_shape=jax.ShapeDtypeStruct((rows, LRU_WIDTH), F32),
        scratch_shapes=[pltpu.VMEM((SUBLANES, LRU_WIDTH), F32)],
        compiler_params=pltpu.CompilerParams(dimension_semantics=("arbitrary",), vmem_limit_bytes=VMEM_LIMIT),
    )(a, b)


def _lru_scan_bwd(dh, a, *, rows, tile):
    n = rows // tile
    nblk = tile // SUBLANES
    group = 5

    def body(dh_ref, a_ref, g_ref, cg_ref, ca_ref):
        @pl.when(pl.program_id(0) == 0)
        def _():
            cg_ref[...] = jnp.zeros_like(cg_ref)
            ca_ref[...] = jnp.zeros_like(ca_ref)

        row = lax.broadcasted_iota(jnp.int32, (SUBLANES, LANES), 0)
        for q0 in range(0, LRU_WIDTH // LANES, group):
            offs = [q * LANES for q in range(q0, q0 + group)]

            def blk(t, carry, offs=offs):
                r0 = pl.multiple_of((nblk - 1 - t) * SUBLANES, SUBLANES)
                new = []
                for j, o in enumerate(offs):
                    cg, ca = carry[2 * j], carry[2 * j + 1]
                    araw = a_ref[pl.ds(r0, SUBLANES), o:o + LANES]
                    xv = dh_ref[pl.ds(r0, SUBLANES), o:o + LANES]
                    av = jnp.where(row == SUBLANES - 1, ca, pltpu.roll(araw, SUBLANES - 1, 0))
                    for sh in (1, 2, 4):
                        m = row <= SUBLANES - 1 - sh
                        xs = pltpu.roll(xv, SUBLANES - sh, 0)
                        asft = pltpu.roll(av, SUBLANES - sh, 0)
                        xv = xv + jnp.where(m, av * xs, 0.0)
                        av = jnp.where(m, av * asft, av)
                    gv = xv + av * cg
                    g_ref[pl.ds(r0, SUBLANES), o:o + LANES] = gv
                    new.append(jnp.broadcast_to(gv[0:1, :], (SUBLANES, LANES)))
                    new.append(jnp.broadcast_to(araw[0:1, :], (SUBLANES, LANES)))
                return tuple(new)

            carry0 = tuple(r[:, o:o + LANES] for o in offs for r in (cg_ref, ca_ref))
            carry = lax.fori_loop(0, nblk, blk, carry0, unroll=2)
            for j, o in enumerate(offs):
                cg_ref[:, o:o + LANES] = carry[2 * j]
                ca_ref[:, o:o + LANES] = carry[2 * j + 1]

    spec = pl.BlockSpec((tile, LRU_WIDTH), lambda i: (n - 1 - i, 0))
    return pl.pallas_call(
        body, name="lru_scan_bwd", grid=(n,), in_specs=[spec, spec], out_specs=spec,
        out_shape=jax.ShapeDtypeStruct((rows, LRU_WIDTH), F32),
        scratch_shapes=[pltpu.VMEM((SUBLANES, LRU_WIDTH), F32), pltpu.VMEM((SUBLANES, LRU_WIDTH), F32)],
        compiler_params=pltpu.CompilerParams(dimension_semantics=("arbitrary",), vmem_limit_bytes=VMEM_LIMIT),
    )(dh, a)


def _conv_fwd(i, x, prev, cw, cb):
    prev = prev * (i > 0).astype(F32)
    y = x * cw[3:4, :] + cb
    for s in range(1, CONV_WIDTH):
        y = y + _rows_before(x, prev, s) * cw[3 - s:4 - s, :]
    return y


def _lru_gates(c, wa, ba, wx, bx, lam):
    r = _sigmoid(_heads(_nn, c, wa) + ba)
    ig = _sigmoid(_heads(_nn, c, wx) + bx)
    z = -lam
    sp = jnp.maximum(z, 0.0) + jnp.log(1.0 + jnp.exp(-jnp.abs(z)))
    log_a = -LRU_C * r * sp
    a = jnp.exp(log_a)
    z2 = 2.0 * log_a
    series = -z2 * (1.0 + z2 * (0.5 + z2 * (1.0 / 6.0 + z2 * (1.0 / 24.0 + z2 * (1.0 / 120.0 + z2 / 720.0)))))
    one_minus = jnp.where(z2 > -0.2, series, 1.0 - jnp.exp(z2))
    mult = jnp.sqrt(one_minus)
    return r, ig, sp, a, mult


def _layer_fwd(x, p, w, rows, target=None):
    tile = WIDE_TILE
    d = D_MODEL

    def f_in(i, xb, g, *ws):
        rstd = lax.rsqrt(jnp.mean(xb * xb, axis=-1, keepdims=True) + NORM_EPS)
        hb = (xb * rstd * g).astype(BF16)
        proj = jnp.concatenate([jnp.dot(hb, wj, preferred_element_type=F32) for wj in ws], axis=1)
        return tuple(proj[:, IN_OFFSETS[k]:IN_OFFSETS[k + 1]] for k in range(6)) + (hb,)

    s5x, s5g, lrux, lrug, gs, gl, h = _rows(
        "f_in", f_in, [(x, "row"), (w["g_pre"], "full")] + w["w_in"],
        [((rows, wd), BF16, "row") for wd in IN_WIDTHS] + [((rows, d), BF16, "row")], rows=rows, tile=ROW_TILE)

    ys, st = _s5_fwd(s5x, w["bd"], w["cdt"], w["s5_d"], w["scf"], rows=rows, tile=ROW_TILE)

    def f_s5post(i, ysb, gb, wglu, wbs):
        glv, _ = _gelu_parts(ysb)
        glu = _nn(glv, wglu)
        y2 = glu[:, :S5_WIDTH] * _sigmoid(glu[:, S5_WIDTH:]) * (gb * _sigmoid(gb))
        return (_nn(y2, wbs),)

    (z_s,) = _rows("f_s5post", f_s5post, [(ys, "row"), (s5g, "row"), (w["w_glu"], "full"), (w["w_bs"], "full")],
                   [((rows, d), BF16, "row")], rows=rows, tile=tile)

    def f_gates(i, xb, prev, cw, cb, wa, ba, wx, bx, lam):
        c = _conv_fwd(i, xb, prev, cw, cb)
        _, ig, _, a, mult = _lru_gates(c, wa, ba, wx, bx, lam)
        return a, mult * (ig * c)

    a, b = _rows("f_gates", f_gates,
                 [(lrux, "row"), (lrux, "prev"), (w["conv_w"], "full"), (w["conv_b"], "full"), (w["lru_w_a"], "full"),
                  (w["lru_b_a"], "full"), (w["lru_w_x"], "full"), (w["lru_b_x"], "full"), (w["lru_lambda"], "full")],
                 [((rows, LRU_WIDTH), F32, "row")] * 2, rows=rows, tile=tile)
    hl = _lru_scan_fwd(a, b, rows=rows, tile=min(2 * tile, rows))

    def f_merge(i, hb, lg, zs, gsb, glb, xb, wbl, wout, gpost):
        z_l = _nn(hb * (lg * _sigmoid(lg)), wbl)
        merged = _sigmoid(gsb) * zs + _sigmoid(glb) * z_l
        mix = _nn(merged, wout)
        rstd = lax.rsqrt(jnp.mean(mix * mix, axis=-1, keepdims=True) + NORM_EPS)
        return xb + mix * rstd * gpost, mix, z_l

    x1, mix, z_l = _rows("f_merge", f_merge,
                         [(hl, "row"), (lrug, "row"), (z_s, "row"), (gs, "row"), (gl, "row"), (x, "row"),
                          (w["w_bl"], "full"), (w["w_out"], "full"), (w["g_post"], "full")],
                         [((rows, d), F32, "row"), ((rows, d), BF16, "row"), ((rows, d), BF16, "row")], rows=rows, tile=tile)

    saved = dict(x=x, h=h, s5x=s5x, s5g=s5g, lrux=lrux, lrug=lrug, gs=gs, gl=gl, ys=ys, st=st, a=a, hl=hl, z_s=z_s,
                 z_l=z_l, mix=mix, x1=x1, p=p)
    ple_ins = [(x1, "row"), (p, "row"), (w["w_ple"], "full"), (w["w_ple_gate"], "full")]
    if target is None:
        def f_ple(i, x1b, pb, wple, wpg):
            return (x1b + _nn(pb, wple) * _sigmoid(_nn(x1b, wpg)),)

        return _rows("f_ple", f_ple, ple_ins, [((rows, d), F32, "row")], rows=rows, tile=tile)[0], saved

    def f_ple_loss(i, x1b, pb, wple, wpg, tb):
        e = x1b + _nn(pb, wple) * _sigmoid(_nn(x1b, wpg)) - tb
        return e * (1.0 / D_MODEL), jnp.sum(jnp.sum(e * e, axis=0, keepdims=True), axis=1, keepdims=True)

    return _rows("f_ple_loss", f_ple_loss, ple_ins + [(target, "row")],
                 [((rows, d), F32, "row"), ((1, 1), F32, "acc")], rows=rows, tile=tile), saved


def _layer_bwd(dx2, sv, w, rows):
    tile = WIDE_TILE
    d = D_MODEL
    g = {}

    def b_ple(i, dxb, x1b, pb, wple, wpg):
        pe = _nn(pb, wple)
        sg = _sigmoid(_nn(x1b, wpg))
        dpe = dxb * sg
        dgt = dxb * pe * sg * (1.0 - sg)
        return dxb + _nt(dgt, wpg), _tn(pb, dpe), _tn(x1b, dgt)

    dx1, g["w_ple"], g["w_ple_gate"] = _rows(
        "b_ple", b_ple, [(dx2, "row"), (sv["x1"], "row"), (sv["p"], "row"), (w["w_ple"], "full"), (w["w_ple_gate"], "full")],
        [((rows, d), F32, "row"), ((PLE_DIM, d), F32, "acc"), ((d, d), F32, "acc")], rows=rows, tile=tile)

    def b_merge(i, dxb, mixb, zs, zl, gsb, glb, wout, gpost):
        rstd = lax.rsqrt(jnp.mean(mixb * mixb, axis=-1, keepdims=True) + NORM_EPS)
        nrm = mixb * rstd
        dn = dxb * gpost
        dmix = rstd * (dn - nrm * jnp.mean(dn * nrm, axis=-1, keepdims=True))
        ss, sl = _sigmoid(gsb), _sigmoid(glb)
        merged = ss * zs + sl * zl
        dm = _nt(dmix, wout)
        return (dm * ss, dm * sl, dm * zs * ss * (1.0 - ss), dm * zl * sl * (1.0 - sl),
                _tn(merged, dmix), jnp.sum(dxb * nrm, axis=0, keepdims=True))

    dz_s, dz_l, dgs, dgl, g["w_out"], g["g_post"] = _rows(
        "b_merge", b_merge,
        [(dx1, "row"), (sv["mix"], "row"), (sv["z_s"], "row"), (sv["z_l"], "row"), (sv["gs"], "row"), (sv["gl"], "row"),
         (w["w_out"], "full"), (w["g_post"], "full")],
        [((rows, d), BF16, "row")] * 4 + [((d, d), F32, "acc"), ((1, d), F32, "acc")], rows=rows, tile=tile)

    def b_bl(i, dzl, hb, lg, wbl):
        sl = _sigmoid(lg)
        silu = lg * sl
        dy3 = _nt(dzl, wbl)
        return dy3 * silu, dy3 * hb * sl * (1.0 + lg * (1.0 - sl)), _tn(hb * silu, dzl)

    dh, dlrug, g["w_bl"] = _rows(
        "b_bl", b_bl, [(dz_l, "row"), (sv["hl"], "row"), (sv["lrug"], "row"), (w["w_bl"], "full")],
        [((rows, LRU_WIDTH), F32, "row"), ((rows, LRU_WIDTH), BF16, "row"), ((LRU_WIDTH, d), F32, "acc")], rows=rows, tile=tile)

    gh = _lru_scan_bwd(dh, sv["a"], rows=rows, tile=min(2 * tile, rows))

    def b_gates(i, ghb, hb, hprev, xb, xprev, cw, cb, wa, ba, wx, bx, lam):
        c = _conv_fwd(i, xb, xprev, cw, cb)
        r, ig, sp, a, mult = _lru_gates(c, wa, ba, wx, bx, lam)
        h_before = _rows_before(hb, hprev * (i > 0).astype(F32), 1)
        da = ghb * h_before
        dmult = ghb * ig * c
        dlog_a = da * a - dmult * a * a / mult
        dpre_r = dlog_a * (-LRU_C) * sp * r * (1.0 - r)
        dpre_i = ghb * mult * c * ig * (1.0 - ig)
        dc = ghb * mult * ig + _heads(_nt, dpre_r, wa) + _heads(_nt, dpre_i, wx)
        dlam = jnp.sum(dlog_a * LRU_C * r, axis=0, keepdims=True) * _sigmoid(-lam)
        return (dc, _heads_tn(c, dpre_r), _heads_tn(c, dpre_i), jnp.sum(dpre_r, axis=0, keepdims=True),
                jnp.sum(dpre_i, axis=0, keepdims=True), dlam)

    hshape = (LRU_HEADS, LRU_HEAD_DIM, LRU_HEAD_DIM)
    dc, g["lru_w_a"], g["lru_w_x"], g["lru_b_a"], g["lru_b_x"], g["lru_lambda"] = _rows(
        "b_gates", b_gates,
        [(gh, "row"), (sv["hl"], "row"), (sv["hl"], "prev"), (sv["lrux"], "row"), (sv["lrux"], "prev"),
         (w["conv_w"], "full"), (w["conv_b"], "full"), (w["lru_w_a"], "full"), (w["lru_b_a"], "full"),
         (w["lru_w_x"], "full"), (w["lru_b_x"], "full"), (w["lru_lambda"], "full")],
        [((rows, LRU_WIDTH), BF16, "row"), (hshape, F32, "acc"), (hshape, F32, "acc")] + [((1, LRU_WIDTH), F32, "acc")] * 3,
        rows=rows, tile=tile)

    n_tiles = rows // min(tile, rows)

    def b_conv(i, dcb, dnext, xb, xprev, cw):
        dnext = dnext * (i < n_tiles - 1).astype(F32)
        xprev = xprev * (i > 0).astype(F32)
        dx = dcb * cw[3:4, :]
        dws = [jnp.sum(dcb * xb, axis=0, keepdims=True)]
        for s in range(1, CONV_WIDTH):
            dx = dx + _rows_after(dcb, dnext, s) * cw[3 - s:4 - s, :]
            dws.append(jnp.sum(dcb * _rows_before(xb, xprev, s), axis=0, keepdims=True))
        return dx, jnp.concatenate(dws[::-1], axis=0), jnp.sum(dcb, axis=0, keepdims=True)

    dlrux, g["conv_w"], g["conv_b"] = _rows(
        "b_conv", b_conv, [(dc, "row"), (dc, "next"), (sv["lrux"], "row"), (sv["lrux"], "prev"), (w["conv_w"], "full")],
        [((rows, LRU_WIDTH), BF16, "row"), ((CONV_WIDTH, LRU_WIDTH), F32, "acc"), ((1, LRU_WIDTH), F32, "acc")],
        rows=rows, tile=tile)

    def b_s5post(i, dzs, ysb, gb, wglu, wbs):
        glv, dgelu = _gelu_parts(ysb)
        glu = _nn(glv, wglu)
        ga, gb2 = glu[:, :S5_WIDTH], glu[:, S5_WIDTH:]
        sb = _sigmoid(gb2)
        sg = _sigmoid(gb)
        silu = gb * sg
        y2 = ga * sb * silu
        dy2 = _nt(dzs, wbs)
        dglu = jnp.concatenate([dy2 * sb * silu, dy2 * ga * silu * sb * (1.0 - sb)], axis=1)
        dg = dy2 * ga * sb * sg * (1.0 + gb * (1.0 - sg))
        return _nt(dglu, wglu) * dgelu, dg, _tn(y2, dzs), _tn(glv, dglu)

    dys, ds5g, g["w_bs"], g["w_glu"] = _rows(
        "b_s5post", b_s5post, [(dz_s, "row"), (sv["ys"], "row"), (sv["s5g"], "row"), (w["w_glu"], "full"), (w["w_bs"], "full")],
        [((rows, S5_WIDTH), F32, "row"), ((rows, S5_WIDTH), BF16, "row"), ((S5_WIDTH, d), F32, "acc"),
         ((S5_WIDTH, 2 * S5_WIDTH), F32, "acc")],
        rows=rows, tile=tile)

    ds5x, g["bd"], g["cdt"], g["s5_d"], g["abar"] = _s5_bwd(dys, sv["st"], sv["s5x"], w["bd"], w["cdt"], w["s5_d"],
                                                            w["scb"], rows=rows, tile=ROW_TILE)

    dcomps = [ds5x, ds5g, dlrux, dlrug, dgs, dgl]

    def b_in(i, xb, dx1b, gpre, *rest):
        dproj, ws = jnp.concatenate(rest[:6], axis=1), rest[6:]
        dh = _nt(dproj[:, :IN_SLOT], ws[0])
        for j in range(1, 4):
            dh = dh + _nt(dproj[:, j * IN_SLOT:(j + 1) * IN_SLOT], ws[j])
        rstd = lax.rsqrt(jnp.mean(xb * xb, axis=-1, keepdims=True) + NORM_EPS)
        nrm = xb * rstd
        dn = dh * gpre
        dx = rstd * (dn - nrm * jnp.mean(dn * nrm, axis=-1, keepdims=True))
        return dx1b + dx, jnp.sum(dh * nrm, axis=0, keepdims=True)

    dx, g["g_pre"] = _rows(
        "b_in", b_in, [(sv["x"], "row"), (dx1, "row"), (w["g_pre"], "full")] + [(dcv, "raw") for dcv in dcomps]
        + w["w_in"],
        [((rows, d), F32, "row"), ((1, d), F32, "acc")], rows=rows, tile=ROW_TILE)

    g["w_in"] = []
    for j in range(4):
        lo, hi = j * IN_SLOT, (j + 1) * IN_SLOT
        ks = [k for k in range(6) if IN_OFFSETS[k] < hi and IN_OFFSETS[k + 1] > lo]
        first = IN_OFFSETS[ks[0]]

        def b_win(i, hb, *parts, lo=lo, hi=hi, first=first):
            return (_tn(hb, jnp.concatenate(parts, axis=1)[:, lo - first:hi - first]),)

        g["w_in"].append(_rows("b_win", b_win, [(sv["h"], "raw")] + [(dcomps[k], "raw") for k in ks],
                               [((d, IN_SLOT), F32, "acc")], rows=rows, tile=4 * ROW_TILE)[0])
    return dx, g


SMALL = ("g_pre", "s5_a_re", "s5_a_im", "s5_log_dt", "s5_b_re", "s5_b_im", "s5_c_re", "s5_c_im", "s5_d", "conv_b",
         "lru_w_a", "lru_b_a", "lru_w_x", "lru_b_x", "lru_lambda", "g_post")
BIG = ("w_in", "w_glu", "w_bs", "conv_w", "w_bl", "w_out", "w_ple", "w_ple_gate")


def _bcast_groups(v):
    return jnp.broadcast_to(v[:, None, :], (S5_GROUPS, S5_GROUP, S5_STATE)).reshape(S5_WIDTH, S5_STATE)


def _s5_prep_inputs(wl):
    ldt = jnp.broadcast_to(wl["s5_log_dt"][:, None], (S5_GROUPS, S5_STATE))
    gcn = lambda b: jnp.transpose(b, (0, 2, 1)).reshape(S5_WIDTH, S5_STATE)
    return (_bcast_groups(wl["s5_a_re"]), _bcast_groups(wl["s5_a_im"]), _bcast_groups(ldt), gcn(wl["s5_b_re"]),
            gcn(wl["s5_b_im"]))


def _layer_weights(wl):
    w = {}
    w["w_in"] = [wc if isinstance(wc, tuple) else (wc, "full") for wc in wl["w_in"]]
    for k in ("w_glu", "w_bs", "w_bl", "w_out", "w_ple", "w_ple_gate"):
        w[k] = wl[k]
    w["conv_w"] = wl["conv_w"]
    for k in ("g_pre", "g_post", "s5_d", "conv_b", "lru_b_a", "lru_b_x", "lru_lambda"):
        w[k] = wl[k].reshape(1, -1)
    w["lru_w_a"] = wl["lru_w_a"].astype(BF16)
    w["lru_w_x"] = wl["lru_w_x"].astype(BF16)
    prep_in = _s5_prep_inputs(wl)
    abr, abi, bbr, bbi = _s5_prep(*prep_in)
    w["prep_in"] = prep_in
    shape3 = (S5_GROUPS, S5_GROUP, S5_STATE)
    w["bd"] = _s5_block_diag([bbr.reshape(shape3), bbi.reshape(shape3)]).astype(BF16)
    w["cdt"] = _s5_block_diag([wl["s5_c_re"], -wl["s5_c_im"]]).astype(BF16)
    abr_s = abr.reshape(shape3)[:, 0, :]
    abi_s = abi.reshape(shape3)[:, 0, :]
    w["scf"], w["scb"] = _s5_consts(_cplx_to_lanes(abr_s), _cplx_to_lanes(abi_s), ROW_TILE // SUBLANES)
    return w


def _layer_param_grads(g, w, wl):
    out = {}
    shape3 = (S5_GROUPS, S5_GROUP, S5_STATE)
    dbr, dbi = _s5_block_diag_extract(g["bd"])
    dcr, dci = _s5_block_diag_extract(g["cdt"])
    out["s5_c_re"], out["s5_c_im"] = dcr, -dci
    zeros = jnp.zeros(shape3, F32)
    dar = zeros.at[:, 0, :].set(g["abar"][0].reshape(S5_GROUPS, S5_STATE)).reshape(S5_WIDTH, S5_STATE)
    dai = zeros.at[:, 0, :].set(g["abar"][1].reshape(S5_GROUPS, S5_STATE)).reshape(S5_WIDTH, S5_STATE)
    cts = (dar, dai, dbr.reshape(S5_WIDTH, S5_STATE), dbi.reshape(S5_WIDTH, S5_STATE))
    d_are, d_aim, d_ldt, d_bre, d_bim = _s5_prep_bwd(*w["prep_in"], cts)
    out["s5_a_re"] = d_are.reshape(shape3).sum(axis=1)
    out["s5_a_im"] = d_aim.reshape(shape3).sum(axis=1)
    out["s5_log_dt"] = d_ldt.reshape(shape3).sum(axis=(1, 2))
    out["s5_b_re"] = jnp.transpose(d_bre.reshape(shape3), (0, 2, 1))
    out["s5_b_im"] = jnp.transpose(d_bim.reshape(shape3), (0, 2, 1))
    out["s5_d"] = g["s5_d"].reshape(-1)
    for k in ("g_pre", "g_post", "conv_b", "lru_b_a", "lru_b_x", "lru_lambda"):
        out[k] = g[k].reshape(-1)
    for k in ("lru_w_a", "lru_w_x", "conv_w", "w_in", "w_glu", "w_bs", "w_bl", "w_out", "w_ple", "w_ple_gate"):
        out[k] = g[k]
    return out


def _local_step(x, p, layers, target):
    rows = x.shape[0]
    ws = [_layer_weights(wl) for wl in layers]
    saved = []
    for i in range(DEPTH):
        x, sv = _layer_fwd(x, p[i], ws[i], rows, target if i == DEPTH - 1 else None)
        saved.append(sv)
    dx, sq = x
    loss = sq[0, 0] * (0.5 / D_MODEL)
    grads = [None] * DEPTH
    for i in reversed(range(DEPTH)):
        dx, g = _layer_bwd(dx, saved[i], ws[i], rows)
        grads[i] = _layer_param_grads(g, ws[i], layers[i])
    return loss, dx, grads


def _place():
    return lax.axis_index("x"), lax.axis_index("y"), lax.axis_index("c")


def _other_chips(x, y):
    return [(1 - x, y), (x, 1 - y), (1 - x, 1 - y)]


def _any_spec():
    return pl.BlockSpec(memory_space=pl.ANY)


def _gather_chips(name, vs, via_sibling):
    n = len(vs)
    halved = [via_sibling and v.shape[0] % (16 * 4 // v.dtype.itemsize) == 0 for v in vs]
    relayed = [h and v.shape[0] % (32 * 4 // v.dtype.itemsize) == 0 for h, v in zip(halved, vs)]
    SEMS = 7

    def body(*refs):
        v_refs, out_refs, send_sems, recv_sems = refs[:n], refs[n:2 * n], refs[2 * n], refs[2 * n + 1]
        x, y, c = _place()
        me = 2 * x + y
        chips = _other_chips(x, y)
        slots = [2 * cx + cy for cx, cy in chips]
        sibling = (x, y, 1 - c)

        def part(a, slot, hc, quarter=None):
            if not halved[a]:
                return out_refs[a].at[slot]
            half = vs[a].shape[0] // 2
            if quarter is None:
                return out_refs[a].at[slot, pl.ds(hc * half, half), :]
            return out_refs[a].at[slot, pl.ds(hc * half + quarter * (half // 2), half // 2), :]

        def own(a):
            if not halved[a]:
                return v_refs[a]
            half = vs[a].shape[0] // 2
            return v_refs[a].at[pl.ds(c * half, half), :]

        def copy(a, k, src, dst, to):
            return pltpu.make_async_remote_copy(src_ref=src, dst_ref=dst, send_sem=send_sems.at[SEMS * a + k],
                                                recv_sem=recv_sems.at[SEMS * a + k], device_id=to, device_id_type=MESH)

        def relay(a, k):
            src = part(a, slots[k], c, quarter=k)
            return copy(a, 2 + k, src, src, (*chips[1 - k], c))

        for a in range(n):
            for k in range(2 if relayed[a] else 3):
                copy(a, k, own(a), part(a, me, c), (*chips[k], c)).start()
        for a in range(n):
            for k in range(2 if relayed[a] else 3):
                copy(a, k, own(a), part(a, slots[k], c), (*chips[k], c)).wait_recv()
                if relayed[a]:
                    relay(a, k).start()
                if halved[a]:
                    copy(a, 4 + k, part(a, slots[k], c), part(a, slots[k], c), sibling).start()
        for a in range(n):
            if relayed[a]:
                for k in range(2):
                    copy(a, 2 + k, part(a, slots[2], c, quarter=k), part(a, slots[2], c, quarter=k), sibling).wait_recv()
                copy(a, 6, part(a, slots[2], c), part(a, slots[2], c), sibling).start()
        for a in range(n):
            for k in range(3):
                if halved[a]:
                    copy(a, 4 + k, own(a), part(a, slots[k], 1 - c), sibling).wait_recv()
                    copy(a, 4 + k, own(a), part(a, me, c), sibling).wait_send()
                if relayed[a] and k == 2:
                    for q in range(2):
                        relay(a, q).wait_send()
                else:
                    copy(a, k, own(a), part(a, me, c), sibling).wait_send()

    others = pl.pallas_call(
        body, name=name, out_shape=[jax.ShapeDtypeStruct((4,) + v.shape, v.dtype) for v in vs],
        in_specs=[_any_spec()] * n, out_specs=[_any_spec()] * n,
        scratch_shapes=[pltpu.SemaphoreType.DMA((SEMS * n,)), pltpu.SemaphoreType.DMA((SEMS * n,))],
    )(*vs)
    me = 2 * lax.axis_index("x") + lax.axis_index("y")
    return [lax.dynamic_update_slice(o, v[None], (me, 0, 0)) for o, v in zip(others, vs)]


def _rs_sibling(grs):
    n = len(grs)
    halves = [g.shape[1] // 2 for g in grs]

    def body(*refs):
        g_refs, got_refs, send_sems, recv_sems = refs[:n], refs[n:2 * n], refs[2 * n], refs[2 * n + 1]
        x, y, c = _place()
        copies = [pltpu.make_async_remote_copy(
            src_ref=g_refs[a].at[:, pl.ds((1 - c) * halves[a], halves[a]), :], dst_ref=got_refs[a],
            send_sem=send_sems.at[a], recv_sem=recv_sems.at[a], device_id=(x, y, 1 - c), device_id_type=MESH)
            for a in range(n)]
        for cp in copies:
            cp.start()
        for cp in copies:
            cp.wait()

    return pl.pallas_call(
        body, name="rs_sibling",
        out_shape=[jax.ShapeDtypeStruct((4, h, g.shape[2]), F32) for g, h in zip(grs, halves)],
        in_specs=[_any_spec()] * n, out_specs=[_any_spec()] * n,
        scratch_shapes=[pltpu.SemaphoreType.DMA((n,)), pltpu.SemaphoreType.DMA((n,))],
    )(*grs)


def _rs_add_sibling(pk, got):
    _, half, width = got.shape
    tile = _pack_tile(half, width)
    nb = half // tile

    def body(lo_ref, hi_ref, got_ref, a32_ref, a16_ref):
        mine = jnp.where(lax.axis_index("c") == 0, lo_ref[...], hi_ref[...])
        s = mine + got_ref[...]
        a32_ref[...] = s
        a16_ref[...] = s.astype(BF16)

    blk = lambda first: pl.BlockSpec((None, tile, width), lambda s, i: (s, first + i, 0))
    return pl.pallas_call(
        body, name="rs_add_sibling", grid=(4, nb), in_specs=[blk(0), blk(nb), blk(0)], out_specs=[blk(0), blk(0)],
        out_shape=[jax.ShapeDtypeStruct(got.shape, F32), jax.ShapeDtypeStruct(got.shape, BF16)],
        compiler_params=pltpu.CompilerParams(dimension_semantics=("arbitrary", "arbitrary"), vmem_limit_bytes=VMEM_LIMIT),
    )(pk, pk, got)


def _rs_chips(a16s):
    n = len(a16s)
    relayed = [a.shape[1] % (32 * 4 // a.dtype.itemsize) == 0 for a in a16s]
    SEMS = 6

    def body(*refs):
        a_refs, got_refs, stage_refs = refs[:n], refs[n:2 * n], refs[2 * n:3 * n]
        send_sems, recv_sems = refs[3 * n], refs[3 * n + 1]
        x, y, c = _place()
        chips = _other_chips(x, y)
        slots = [2 * cx + cy for cx, cy in chips]

        def copy(a, k, src, dst, to):
            return pltpu.make_async_remote_copy(src_ref=src, dst_ref=dst, send_sem=send_sems.at[SEMS * a + k],
                                                recv_sem=recv_sems.at[SEMS * a + k], device_id=(*to, c), device_id_type=MESH)

        def straight(a, k):
            return copy(a, k, a_refs[a].at[slots[k]], got_refs[a].at[k], chips[k])

        def staged(a, k):
            q = a16s[a].shape[1] // 2
            return copy(a, 2 + k, a_refs[a].at[slots[2], pl.ds(k * q, q), :], stage_refs[a].at[k], chips[k])

        def passed(a, k):
            q = a16s[a].shape[1] // 2
            return copy(a, 4 + k, stage_refs[a].at[k], got_refs[a].at[2, pl.ds(k * q, q), :], chips[1 - k])

        for a in range(n):
            for k in range(2):
                straight(a, k).start()
            if relayed[a]:
                for k in range(2):
                    staged(a, k).start()
            else:
                straight(a, 2).start()
        for a in range(n):
            if relayed[a]:
                for k in range(2):
                    staged(a, k).wait_recv()
                    passed(a, k).start()
        for a in range(n):
            for k in range(2):
                straight(a, k).wait()
            if relayed[a]:
                for k in range(2):
                    staged(a, k).wait_send()
                    passed(a, k).wait()
            else:
                straight(a, 2).wait()

    outs = pl.pallas_call(
        body, name="rs_chips",
        out_shape=[jax.ShapeDtypeStruct((3,) + a.shape[1:], a.dtype) for a in a16s]
        + [jax.ShapeDtypeStruct((2, a.shape[1] // 2, a.shape[2]), a.dtype) for a in a16s],
        in_specs=[_any_spec()] * n, out_specs=[_any_spec()] * (2 * n),
        scratch_shapes=[pltpu.SemaphoreType.DMA((SEMS * n,)), pltpu.SemaphoreType.DMA((SEMS * n,))],
    )(*a16s)
    return outs[:n]


def _swap_halves(vs):
    n = len(vs)

    def body(*refs):
        v_refs, out_refs, send_sems, recv_sems = refs[:n], refs[n:2 * n], refs[2 * n], refs[2 * n + 1]
        x, y, c = _place()

        def give(a, hc):
            return pltpu.make_async_remote_copy(src_ref=v_refs[a], dst_ref=out_refs[a].at[hc], send_sem=send_sems.at[a],
                                                recv_sem=recv_sems.at[a], device_id=(x, y, 1 - c), device_id_type=MESH)

        for a in range(n):
            give(a, c).start()
        for a in range(n):
            give(a, c).wait_send()
            give(a, 1 - c).wait_recv()

    others = pl.pallas_call(
        body, name="swap_halves", out_shape=[jax.ShapeDtypeStruct((2,) + v.shape, v.dtype) for v in vs],
        in_specs=[_any_spec()] * n, out_specs=[_any_spec()] * n,
        scratch_shapes=[pltpu.SemaphoreType.DMA((n,)), pltpu.SemaphoreType.DMA((n,))],
    )(*vs)
    c = lax.axis_index("c")
    return [lax.dynamic_update_slice(o, v[None], (c, 0, 0)).reshape(2 * v.shape[0], v.shape[1]) for o, v in zip(others, vs)]


WIDE = 1024


def _unpack(flat, shapes, align=1):
    out, off = [], 0
    for s in shapes:
        n = 1
        for q in s:
            n *= q
        out.append(flat[off:off + n].reshape(s))
        off += -(-n // align) * align
    return out


def _pack_tile(rows, width):
    most = (2 ** 21) // (4 * width)
    if rows <= most:
        return rows
    return max(t for t in range(16, most + 1, 16) if rows % t == 0)


def _flat_aligned(v, align):
    v = v.reshape(-1)
    return jnp.pad(v, (0, -v.shape[0] % align))


def _adamw(name, w, g, m, v, tile):
    def fn(i, wb, gb, mb, vb):
        m2 = ADAM_B1 * mb + (1.0 - ADAM_B1) * gb
        v2 = ADAM_B2 * vb + (1.0 - ADAM_B2) * (gb * gb)
        m_hat = m2 / (1.0 - ADAM_B1 ** ADAM_STEP)
        v_hat = v2 / (1.0 - ADAM_B2 ** ADAM_STEP)
        return -ADAM_LR * (m_hat / (jnp.sqrt(v_hat) + ADAM_EPS) + ADAM_WD * wb), m2, v2

    return _rows(name, fn, [(w, "row"), (g, "row"), (m, "row"), (v, "row")], [(w.shape, F32, "row")] * 3,
                 rows=w.shape[0], tile=tile)


def _as_2d(a):
    return a.reshape(-1, a.shape[-1])


def _adam_tile(rows):
    for t in (256, 184, 128, 64, 32, 16, 8):
        if rows % t == 0:
            return t
    return rows


def kernel(x, p, g_pre, w_in, s5_a_re, s5_a_im, s5_log_dt, s5_b_re, s5_b_im, s5_c_re, s5_c_im, s5_d, w_glu, w_bs, conv_w, conv_b, lru_w_a, lru_b_a, lru_w_x, lru_b_x, lru_lambda, w_bl, w_out, g_post, w_ple, w_ple_gate, loss_target, m_g_pre, m_w_in, m_s5_a_re, m_s5_a_im, m_s5_log_dt, m_s5_b_re, m_s5_b_im, m_s5_c_re, m_s5_c_im, m_s5_d, m_w_glu, m_w_bs, m_conv_w, m_conv_b, m_lru_w_a, m_lru_b_a, m_lru_w_x, m_lru_b_x, m_lru_lambda, m_w_bl, m_w_out, m_g_post, m_w_ple, m_w_ple_gate, v_g_pre, v_w_in, v_s5_a_re, v_s5_a_im, v_s5_log_dt, v_s5_b_re, v_s5_b_im, v_s5_c_re, v_s5_c_im, v_s5_d, v_w_glu, v_w_bs, v_conv_w, v_conv_b, v_lru_w_a, v_lru_b_a, v_lru_w_x, v_lru_b_x, v_lru_lambda, v_w_bl, v_w_out, v_g_post, v_w_ple, v_w_ple_gate):
    wts = dict(g_pre=g_pre, w_in=w_in, s5_a_re=s5_a_re, s5_a_im=s5_a_im, s5_log_dt=s5_log_dt, s5_b_re=s5_b_re,
               s5_b_im=s5_b_im, s5_c_re=s5_c_re, s5_c_im=s5_c_im, s5_d=s5_d, w_glu=w_glu, w_bs=w_bs, conv_w=conv_w,
               conv_b=conv_b, lru_w_a=lru_w_a, lru_b_a=lru_b_a, lru_w_x=lru_w_x, lru_b_x=lru_b_x, lru_lambda=lru_lambda,
               w_bl=w_bl, w_out=w_out, g_post=g_post, w_ple=w_ple, w_ple_gate=w_ple_gate)
    mom1 = dict(g_pre=m_g_pre, w_in=m_w_in, s5_a_re=m_s5_a_re, s5_a_im=m_s5_a_im, s5_log_dt=m_s5_log_dt, s5_b_re=m_s5_b_re,
                s5_b_im=m_s5_b_im, s5_c_re=m_s5_c_re, s5_c_im=m_s5_c_im, s5_d=m_s5_d, w_glu=m_w_glu, w_bs=m_w_bs,
                conv_w=m_conv_w, conv_b=m_conv_b, lru_w_a=m_lru_w_a, lru_b_a=m_lru_b_a, lru_w_x=m_lru_w_x, lru_b_x=m_lru_b_x,
                lru_lambda=m_lru_lambda, w_bl=m_w_bl, w_out=m_w_out, g_post=m_g_post, w_ple=m_w_ple, w_ple_gate=m_w_ple_gate)
    mom2 = dict(g_pre=v_g_pre, w_in=v_w_in, s5_a_re=v_s5_a_re, s5_a_im=v_s5_a_im, s5_log_dt=v_s5_log_dt, s5_b_re=v_s5_b_re,
                s5_b_im=v_s5_b_im, s5_c_re=v_s5_c_re, s5_c_im=v_s5_c_im, s5_d=v_s5_d, w_glu=v_w_glu, w_bs=v_w_bs,
                conv_w=v_conv_w, conv_b=v_conv_b, lru_w_a=v_lru_w_a, lru_b_a=v_lru_b_a, lru_w_x=v_lru_w_x, lru_b_x=v_lru_b_x,
                lru_lambda=v_lru_lambda, w_bl=v_w_bl, w_out=v_w_out, g_post=v_g_post, w_ple=v_w_ple, w_ple_gate=v_w_ple_gate)
    names = list(wts)

    by_rows, by_cols = ("w_bl", "w_out", "w_ple_gate"), ("w_glu", "w_bs", "w_ple")
    two_d = lambda k: wts[k].astype(BF16).reshape(-1, wts[k].shape[2])
    sent = [two_d("w_in"), jnp.concatenate([two_d(k) for k in by_rows]), jnp.concatenate([two_d(k) for k in by_cols]),
            wts["conv_w"].reshape(-1, wts["conv_w"].shape[2])]
    g_in, g_rows, g_cols, g_conv = _gather_chips("gather_weights", sent, via_sibling=True)
    whole = {"conv_w": jnp.transpose(g_conv.reshape((4,) + wts["conv_w"].shape), (1, 2, 0, 3)).reshape(DEPTH, CONV_WIDTH, -1)}
    off = 0
    for k in by_rows:
        dp, r, _ = wts[k].shape
        piece = g_rows[:, off:off + dp * r].reshape(4, dp, r, -1)
        whole[k] = jnp.transpose(piece, (1, 0, 2, 3)).reshape(dp, 4 * r, -1)
        off += dp * r
    off = 0
    for k in by_cols:
        dp, r, cs = wts[k].shape
        piece = g_cols[:, off:off + dp * r].reshape(4, dp, r, cs)
        whole[k] = jnp.transpose(piece, (1, 2, 0, 3)).reshape(dp, r, 4 * cs)
        off += dp * r
    layers = []
    for i in range(DEPTH):
        wl = {k: whole[k][i] for k in BIG if k != "w_in"}
        wl["w_in"] = [(g_in, ("part", j, i, wts["w_in"].shape[1])) for j in range(4)]
        wl.update({k: wts[k][i] for k in SMALL})
        layers.append(wl)

    loss, grad_x, grads = _local_step(x[0], p[:, 0], layers, loss_target[0])
    loss = lax.psum(loss, ("x", "y", "c"))

    me = 2 * lax.axis_index("x") + lax.axis_index("y")
    rows_of = lambda k, i, j: grads[i][k][j * (grads[i][k].shape[0] // 4):(j + 1) * (grads[i][k].shape[0] // 4)]
    cols_of = lambda k, i, j: grads[i][k][:, j * (grads[i][k].shape[1] // 4):(j + 1) * (grads[i][k].shape[1] // 4)]
    layer_range = range(DEPTH)
    packs = [
        jnp.stack([jnp.concatenate([grads[i]["w_in"][j] for i in layer_range]) for j in range(4)]),
        jnp.stack([jnp.concatenate([rows_of(k, i, j) for k in by_rows for i in layer_range]) for j in range(4)]),
        jnp.stack([jnp.concatenate([cols_of(k, i, j) for k in by_cols for i in layer_range]) for j in range(4)]),
    ]
    small_names = SMALL + ("conv_w",)
    small_shapes = [(DEPTH,) + grads[0][k].shape for k in small_names]
    small_flat = jnp.concatenate([_flat_aligned(jnp.stack([grads[i][k] for i in layer_range]), WIDE) for k in small_names])
    n_small = small_flat.shape[0]
    small_q = -(-n_small // (4 * 32 * LANES)) * 32 * LANES
    packs.append(jnp.pad(small_flat, (0, 4 * small_q - n_small)).reshape(4, small_q // LANES, LANES))

    gots = _rs_sibling(packs)
    a32s, a16s = [], []
    for pk, got in zip(packs, gots):
        a32, a16 = _rs_add_sibling(pk, got)
        a32s.append(a32)
        a16s.append(a16)
    got3s = _rs_chips(a16s)
    red_halves = []
    for a32, got3 in zip(a32s, got3s):
        half, width = a32.shape[1], a32.shape[2]
        own = lax.dynamic_index_in_dim(a32, me, 0, keepdims=False)

        def f_add2(i, o, g0, g1, g2):
            return (((o + g0) + g1) + g2,)

        red_halves.append(_rows("rs_add2", f_add2, [(own, "row")] + [(got3[k], "row") for k in range(3)],
                                [((half, width), F32, "row")], rows=half, tile=_pack_tile(half, width))[0])
    reds = _swap_halves(red_halves)
    small_red = _gather_chips("gather_small", [reds[3]], via_sibling=False)[0].reshape(-1)[:n_small]

    grad_out = {"w_in": reds[0].reshape(wts["w_in"].shape)}
    for red, ks in ((reds[1], by_rows), (reds[2], by_cols)):
        off = 0
        for k in ks:
            n = wts[k].shape[0] * wts[k].shape[1]
            grad_out[k] = red[off:off + n].reshape(wts[k].shape)
            off += n
    small_out = dict(zip(small_names, _unpack(small_red, small_shapes, align=WIDE)))
    grad_out.update({k: small_out[k] for k in SMALL})
    grad_out["conv_w"] = lax.dynamic_slice_in_dim(small_out["conv_w"], me * wts["conv_w"].shape[2], wts["conv_w"].shape[2], axis=2)
    delta, new_m, new_v = {}, {}, {}
    for k in BIG + SMALL:
        w2 = _as_2d(wts[k])
        res = _adamw("adamw_" + k, w2, _as_2d(grad_out[k]), _as_2d(mom1[k]), _as_2d(mom2[k]), _adam_tile(w2.shape[0]))
        delta[k], new_m[k], new_v[k] = [r.reshape(wts[k].shape) for r in res]
    return (loss, grad_x[None], *[grad_out[k] for k in names], *[delta[k] for k in names],
            *[new_m[k] for k in names], *[new_v[k] for k in names])
```

```python
import jax
import jax.numpy as jnp
from jax import lax
from jax.experimental import pallas as pl
from jax.experimental.pallas import tpu as pltpu

F32 = jnp.float32
BF16 = jnp.bfloat16
MESH = pl.DeviceIdType.MESH

DEPTH = 2
D_MODEL = 1024
NORM_EPS = 1e-6
S5_WIDTH = 512
S5_GROUPS = 32
S5_GROUP = 16
S5_STATE = 64
LRU_WIDTH = 1280
LRU_HEADS = 10
LRU_HEAD_DIM = 128
LRU_C = 8.0
CONV_WIDTH = 4
PLE_DIM = 256
IN_WIDTHS = (S5_WIDTH, S5_WIDTH, LRU_WIDTH, LRU_WIDTH, D_MODEL, D_MODEL)
IN_OFFSETS = (0, 512, 1024, 2304, 3584, 4608, 5632)
IN_SLOT = 5632 // 4
ADAM_LR = 0.001
ADAM_B1 = 0.9
ADAM_B2 = 0.999
ADAM_EPS = 1e-08
ADAM_WD = 0.01
ADAM_STEP = 10

SUBLANES = 8
LANES = 128
S5_HALF_IN = S5_WIDTH // 2
S5_CPLX = S5_GROUPS * S5_STATE
S5_HALF_CPLX = S5_CPLX // 2
S5_LANES = 2 * S5_CPLX
VMEM_LIMIT = 48 * 2 ** 20
ROW_TILE = 256
WIDE_TILE = 512


def _sigmoid(x):
    return 0.5 * jnp.tanh(0.5 * x) + 0.5


def _gelu_parts(x):
    k = 0.7978845608028654
    t = jnp.tanh(k * (x + 0.044715 * x * x * x))
    val = 0.5 * x * (1.0 + t)
    grad = 0.5 * (1.0 + t) + 0.5 * x * (1.0 - t * t) * k * (1.0 + 3.0 * 0.044715 * x * x)
    return val, grad


def _nn(a, w):
    return jnp.dot(a.astype(BF16), w.astype(BF16), preferred_element_type=F32)


def _nt(a, w):
    return lax.dot_general(a.astype(BF16), w.astype(BF16), (((1,), (1,)), ((), ())), preferred_element_type=F32)


def _tn(a, b):
    return lax.dot_general(a.astype(BF16), b.astype(BF16), (((0,), (0,)), ((), ())), preferred_element_type=F32)


def _heads(op, a, w):
    d = LRU_HEAD_DIM
    return jnp.concatenate([op(a[:, h * d:(h + 1) * d], w[h]) for h in range(LRU_HEADS)], axis=1)


def _heads_tn(a, b):
    d = LRU_HEAD_DIM
    return jnp.stack([_tn(a[:, h * d:(h + 1) * d], b[:, h * d:(h + 1) * d]) for h in range(LRU_HEADS)], axis=0)


def _rows_before(x, halo, s):
    main = pltpu.roll(x, s, 0)
    head = pltpu.roll(jnp.concatenate([halo, x[0:SUBLANES]], axis=0), s, 0)[SUBLANES:2 * SUBLANES]
    return jnp.concatenate([head, main[SUBLANES:]], axis=0)


def _rows_after(x, halo, s):
    n = x.shape[0]
    main = pltpu.roll(x, n - s, 0)
    tail = pltpu.roll(jnp.concatenate([x[n - SUBLANES:], halo], axis=0), 2 * SUBLANES - s, 0)[0:SUBLANES]
    return jnp.concatenate([main[:n - SUBLANES], tail], axis=0)


def _rows(name, fn, ins, outs, *, rows, tile):
    tile = min(tile, rows)
    n = rows // tile
    assert n * tile == rows, (name, rows, tile)
    in_specs = []
    for arr, kind in ins:
        halo = SUBLANES * (4 // arr.dtype.itemsize)
        per, last = tile // halo, rows // halo - 1
        if isinstance(kind, tuple):
            _, j, k, r = kind
            in_specs.append(pl.BlockSpec((None, r, arr.shape[2]), lambda i, j=j, k=k: (j, k, 0)))
        elif kind in ("row", "raw"):
            in_specs.append(pl.BlockSpec((tile, arr.shape[1]), lambda i: (i, 0)))
        elif kind == "prev":
            in_specs.append(pl.BlockSpec((halo, arr.shape[1]), lambda i, per=per: (jnp.maximum(i * per - 1, 0), 0)))
        elif kind == "next":
            in_specs.append(pl.BlockSpec((halo, arr.shape[1]),
                                         lambda i, per=per, last=last: (jnp.minimum((i + 1) * per, last), 0)))
        else:
            in_specs.append(pl.BlockSpec(arr.shape, lambda i, nd=arr.ndim: (0,) * nd))
    out_shape, out_specs = [], []
    for shape, dtype, kind in outs:
        out_shape.append(jax.ShapeDtypeStruct(shape, dtype))
        if kind == "row":
            out_specs.append(pl.BlockSpec((tile, shape[1]), lambda i: (i, 0)))
        else:
            out_specs.append(pl.BlockSpec(shape, lambda i, nd=len(shape): (0,) * nd))
    n_in = len(ins)

    def load(ref, kind):
        v = ref[...]
        if kind in ("row", "prev", "next"):
            v = v.astype(F32)
        if kind == "prev":
            v = v[v.shape[0] - SUBLANES:]
        if kind == "next":
            v = v[:SUBLANES]
        return v

    def body(*refs):
        i = pl.program_id(0)
        vals = fn(i, *[load(r, kind) for r, (_, kind) in zip(refs[:n_in], ins)])
        assert len(vals) == len(outs), name
        for r, v, (_, _, kind) in zip(refs[n_in:], vals, outs):
            if kind == "row":
                r[...] = v.astype(r.dtype)
            else:
                @pl.when(i == 0)
                def _():
                    r[...] = jnp.zeros_like(r)

                r[...] += v.astype(r.dtype)

    return pl.pallas_call(
        body, name=name, grid=(n,), in_specs=in_specs, out_specs=out_specs, out_shape=out_shape,
        compiler_params=pltpu.CompilerParams(dimension_semantics=("arbitrary",), vmem_limit_bytes=VMEM_LIMIT),
    )(*[a for a, _ in ins])


def _s5_discretise(are, aim, ldt, bre, bim):
    dt = jnp.exp(ldt)
    er = jnp.exp(are * dt)
    abr = er * jnp.cos(aim * dt)
    abi = er * jnp.sin(aim * dt)
    den = are * are + aim * aim
    zr = ((abr - 1.0) * are + abi * aim) / den
    zi = (abi * are - (abr - 1.0) * aim) / den
    return abr, abi, zr * bre - zi * bim, zr * bim + zi * bre


def _s5_prep(are, aim, ldt, bre, bim):
    def body(a, b, c, d, e, o0, o1, o2, o3):
        r = _s5_discretise(a[...], b[...], c[...], d[...], e[...])
        o0[...], o1[...], o2[...], o3[...] = r

    sd = jax.ShapeDtypeStruct(are.shape, F32)
    return pl.pallas_call(body, name="s5_prep", out_shape=[sd] * 4)(are, aim, ldt, bre, bim)


def _s5_prep_bwd(are, aim, ldt, bre, bim, cts):
    def body(a, b, c, d, e, c0, c1, c2, c3, o0, o1, o2, o3, o4):
        _, vjp = jax.vjp(_s5_discretise, a[...], b[...], c[...], d[...], e[...])
        r = vjp((c0[...], c1[...], c2[...], c3[...]))
        o0[...], o1[...], o2[...], o3[...], o4[...] = r

    sd = jax.ShapeDtypeStruct(are.shape, F32)
    return pl.pallas_call(body, name="s5_prep_bwd", out_shape=[sd] * 5)(are, aim, ldt, bre, bim, *cts)


def _s5_consts(abr, abi, seg):
    shape = (SUBLANES, S5_CPLX)
    assert seg & (seg - 1) == 0 and seg % SUBLANES == 0, seg

    def body(ar_ref, ai_ref, f_ref, b_ref):
        def cmul(p, q):
            return (p[0] * q[0] - p[1] * q[1], p[0] * q[1] + p[1] * q[0])

        row = lax.broadcasted_iota(jnp.int32, shape, 0)
        a1 = (jnp.broadcast_to(ar_ref[...], shape), jnp.broadcast_to(ai_ref[...], shape))
        squares = [a1]
        while 1 << (len(squares) - 1) < 4 * seg:
            squares.append(cmul(squares[-1], squares[-1]))
        nb = seg.bit_length() - 1
        fwd, rev = [], []
        for k, a in ((1, squares[nb]), (2, squares[nb + 1]), (4, squares[nb + 2])):
            fwd += [jnp.where(row >= k, a[0], 0.0), jnp.where(row >= k, a[1], 0.0)]
            rev += [jnp.where(row <= 7 - k, a[0], 0.0), jnp.where(row <= 7 - k, -a[1], 0.0)]
        fwd += [a1[0], a1[1]]
        rev += [a1[0], -a1[1]]
        e = lax.broadcasted_iota(jnp.int32, (seg, S5_CPLX), 0) + 1
        wide = lambda v: jnp.broadcast_to(v[0:1, :], (seg, S5_CPLX))
        pr, pi = jnp.ones((seg, S5_CPLX), F32), jnp.zeros((seg, S5_CPLX), F32)
        for b in range(nb + 1):
            sr, si = wide(squares[b][0]), wide(squares[b][1])
            bit = ((e >> b) & 1) == 1
            pr, pi = jnp.where(bit, pr * sr - pi * si, pr), jnp.where(bit, pr * si + pi * sr, pi)
        f_ref[...] = jnp.concatenate(fwd + [pr, pi], axis=0)
        b_ref[...] = jnp.concatenate(rev + [pr, -pi], axis=0)

    sd = jax.ShapeDtypeStruct((8 * SUBLANES + 2 * seg, S5_CPLX), F32)
    return pl.pallas_call(body, name="s5_consts", out_shape=[sd, sd])(abr, abi)


S5_TILES = S5_LANES // LANES
S5_HALF_TILES = S5_TILES // 2


def _s5_tile_index(q):
    re = (q // 8) * S5_HALF_TILES + (q % 8)
    return re, re + S5_HALF_TILES // 2


def _lanes_of(ref, first, count):
    return jnp.concatenate([ref[j] for j in range(first, first + count)], axis=1)


def _to_lane_tiles(ref, first, value):
    for j in range(value.shape[1] // LANES):
        ref[first + j] = value[:, j * LANES:(j + 1) * LANES]


def _time_perm(tile, transpose=False):
    seg = tile // SUBLANES
    rho = lax.broadcasted_iota(jnp.int32, (tile, tile), 1 if transpose else 0)
    t = lax.broadcasted_iota(jnp.int32, (tile, tile), 0 if transpose else 1)
    return (t == (rho & (SUBLANES - 1)) * seg + (rho >> 3)).astype(BF16)


def _reorder(perm, x):
    return jnp.dot(perm, x, preferred_element_type=F32)


def _s5_scan(s_ref, sc_ref, carry_ref, tile, reverse):
    seg = tile // SUBLANES
    group = 4
    edge = 0 if reverse else SUBLANES - 1
    row = lax.broadcasted_iota(jnp.int32, (SUBLANES, LANES), 0)
    order = range(seg - 1, -1, -1) if reverse else range(seg)
    rows_of = lambda k: pl.ds(k * SUBLANES, SUBLANES)
    base = 8 * SUBLANES

    for q0 in range(0, S5_CPLX // LANES, group):
        qs = list(range(q0, q0 + group))
        tiles = [_s5_tile_index(q) for q in qs]
        cst = lambda k, q: sc_ref[k * SUBLANES:(k + 1) * SUBLANES, q * LANES:(q + 1) * LANES]
        state = [(jnp.zeros((SUBLANES, LANES), F32), jnp.zeros((SUBLANES, LANES), F32)) for _ in qs]
        mult = [(cst(6, q), cst(7, q)) for q in qs]
        for k in order:
            for j, (re, im) in enumerate(tiles):
                ar, ai = mult[j]
                xr, xi = state[j]
                nr = ar * xr - ai * xi + s_ref[re, rows_of(k), :]
                ni = ar * xi + ai * xr + s_ref[im, rows_of(k), :]
                s_ref[re, rows_of(k), :] = nr
                s_ref[im, rows_of(k), :] = ni
                state[j] = (nr, ni)
        start = []
        for j, (q, (re, im)) in enumerate(zip(qs, tiles)):
            er, ei = state[j]
            shift1 = SUBLANES - 1 if reverse else 1
            dr = jnp.where(row == SUBLANES - 1 - edge, carry_ref[re], pltpu.roll(er, shift1, 0))
            di = jnp.where(row == SUBLANES - 1 - edge, carry_ref[im], pltpu.roll(ei, shift1, 0))
            for c, sh in ((0, 1), (2, 2), (4, 4)):
                shift = SUBLANES - sh if reverse else sh
                ar, ai = cst(c, q), cst(c + 1, q)
                sr, si = pltpu.roll(dr, shift, 0), pltpu.roll(di, shift, 0)
                dr, di = dr + ar * sr - ai * si, di + ar * si + ai * sr
            start.append((dr, di))
        for k in order:
            t = seg - 1 - k if reverse else k
            for j, (q, (re, im)) in enumerate(zip(qs, tiles)):
                lanes = slice(q * LANES, (q + 1) * LANES)
                pr = jnp.broadcast_to(sc_ref[base + t:base + t + 1, lanes], (SUBLANES, LANES))
                pi = jnp.broadcast_to(sc_ref[base + seg + t:base + seg + t + 1, lanes], (SUBLANES, LANES))
                cr, ci = start[j]
                xr = s_ref[re, rows_of(k), :] + pr * cr - pi * ci
                xi = s_ref[im, rows_of(k), :] + pr * ci + pi * cr
                s_ref[re, rows_of(k), :] = xr
                s_ref[im, rows_of(k), :] = xi
                if k == order[-1]:
                    carry_ref[re] = jnp.broadcast_to(xr[edge:edge + 1, :], (SUBLANES, LANES))
                    carry_ref[im] = jnp.broadcast_to(xi[edge:edge + 1, :], (SUBLANES, LANES))


def _s5_fwd(u, bd, cdt, dskip, sc, *, rows, tile):
    n = rows // tile

    def body(u_ref, bd_ref, cdt_ref, d_ref, sc_ref, y_ref, s_ref, carry_ref):
        @pl.when(pl.program_id(0) == 0)
        def _():
            carry_ref[...] = jnp.zeros_like(carry_ref)

        ub = _reorder(_time_perm(tile), u_ref[...].astype(BF16)).astype(BF16)
        for h in range(2):
            _to_lane_tiles(s_ref, h * S5_HALF_TILES, jnp.dot(ub[:, h * S5_HALF_IN:(h + 1) * S5_HALF_IN], bd_ref[h],
                                                             preferred_element_type=F32))
        _s5_scan(s_ref, sc_ref, carry_ref, tile, reverse=False)
        ys = [_nt(_lanes_of(s_ref, h * S5_HALF_TILES, S5_HALF_TILES), cdt_ref[h]) for h in range(2)]
        y = _reorder(_time_perm(tile, transpose=True), jnp.concatenate(ys, axis=1).astype(BF16))
        y_ref[...] = y + d_ref[...] * u_ref[...]

    full = lambda a: pl.BlockSpec(a.shape, lambda i, nd=a.ndim: (0,) * nd)
    return pl.pallas_call(
        body, name="s5_fwd", grid=(n,),
        in_specs=[pl.BlockSpec((tile, S5_WIDTH), lambda i: (i, 0)), full(bd), full(cdt), full(dskip), full(sc)],
        out_specs=[pl.BlockSpec((tile, S5_WIDTH), lambda i: (i, 0)),
                   pl.BlockSpec((S5_TILES, tile, LANES), lambda i: (0, i, 0))],
        out_shape=[jax.ShapeDtypeStruct((rows, S5_WIDTH), F32), jax.ShapeDtypeStruct((S5_TILES, rows, LANES), F32)],
        scratch_shapes=[pltpu.VMEM((S5_TILES, SUBLANES, LANES), F32)],
        compiler_params=pltpu.CompilerParams(dimension_semantics=("arbitrary",), vmem_limit_bytes=VMEM_LIMIT),
    )(u, bd, cdt, dskip, sc)


def _s5_bwd(dy, s, u, bd, cdt, dskip, sc, *, rows, tile):
    n = rows // tile
    hc = 2 * S5_HALF_CPLX
    per8 = tile // SUBLANES
    quarter = S5_HALF_TILES // 2

    def body(dy_ref, s_ref, sp_ref, u_ref, bd_ref, cdt_ref, d_ref, sc_ref,
             du_ref, dbd_ref, dcdt_ref, dd_ref, da_ref, g_ref, carry_ref):
        i = pl.program_id(0)

        @pl.when(i == 0)
        def _():
            carry_ref[...] = jnp.zeros_like(carry_ref)
            dbd_ref[...] = jnp.zeros_like(dbd_ref)
            dcdt_ref[...] = jnp.zeros_like(dcdt_ref)
            dd_ref[...] = jnp.zeros_like(dd_ref)
            da_ref[...] = jnp.zeros_like(da_ref)

        dy = dy_ref[...]
        u = u_ref[...]
        perm = _time_perm(tile)
        dyb = _reorder(perm, dy.astype(BF16)).astype(BF16)
        ub = _reorder(perm, u.astype(BF16)).astype(BF16)
        for h in range(2):
            _to_lane_tiles(g_ref, h * S5_HALF_TILES, jnp.dot(dyb[:, h * S5_HALF_IN:(h + 1) * S5_HALF_IN], cdt_ref[h],
                                                             preferred_element_type=F32))
        _s5_scan(g_ref, sc_ref, carry_ref, tile, reverse=True)
        dus = []
        for h in range(2):
            gb = _lanes_of(g_ref, h * S5_HALF_TILES, S5_HALF_TILES).astype(BF16)
            sb = _lanes_of(s_ref, h * S5_HALF_TILES, S5_HALF_TILES).astype(BF16)
            dus.append(_nt(gb, bd_ref[h]))
            dbd_ref[h] += _tn(ub[:, h * S5_HALF_IN:(h + 1) * S5_HALF_IN], gb)
            dcdt_ref[h] += _tn(dyb[:, h * S5_HALF_IN:(h + 1) * S5_HALF_IN], sb)
        du = _reorder(_time_perm(tile, transpose=True), jnp.concatenate(dus, axis=1).astype(BF16))
        du_ref[...] = (du + d_ref[...] * dy).astype(du_ref.dtype)
        dd_ref[...] += jnp.sum(dy * u, axis=0, keepdims=True)

        not_first = (i < n - 1).astype(F32)
        row = lax.broadcasted_iota(jnp.int32, (SUBLANES, quarter * LANES), 0)

        def step_before(first):
            cur = _lanes_of(s_ref, first, quarter)
            before_tile = _lanes_of(sp_ref, first, quarter)[SUBLANES - 1:SUBLANES, :] * not_first
            head = jnp.where(row == 0, before_tile, pltpu.roll(cur[tile - SUBLANES:], 1, 0))
            return jnp.concatenate([head, cur[:tile - SUBLANES]], axis=0)

        for h in range(2):
            re, im = h * S5_HALF_TILES, h * S5_HALF_TILES + quarter
            ssr = step_before(re)
            ssi = step_before(im)
            gr = _lanes_of(g_ref, re, quarter)
            gi = _lanes_of(g_ref, im, quarter)
            lanes = slice(h * S5_HALF_CPLX, (h + 1) * S5_HALF_CPLX)
            da_ref[0:1, lanes] += jnp.sum(ssr * gr + ssi * gi, axis=0, keepdims=True)
            da_ref[1:2, lanes] += jnp.sum(ssr * gi - ssi * gr, axis=0, keepdims=True)

    full = lambda a: pl.BlockSpec(a.shape, lambda i, nd=a.ndim: (0,) * nd)
    rev = lambda i: (n - 1 - i, 0)
    wshape = (2, S5_HALF_IN, hc)
    return pl.pallas_call(
        body, name="s5_bwd", grid=(n,),
        in_specs=[pl.BlockSpec((tile, S5_WIDTH), rev), pl.BlockSpec((S5_TILES, tile, LANES), lambda i: (0, n - 1 - i, 0)),
                  pl.BlockSpec((S5_TILES, SUBLANES, LANES), lambda i: (0, jnp.maximum((n - 1 - i) * per8 - 1, 0), 0)),
                  pl.BlockSpec((tile, S5_WIDTH), rev), full(bd), full(cdt), full(dskip), full(sc)],
        out_specs=[pl.BlockSpec((tile, S5_WIDTH), rev),
                   pl.BlockSpec(wshape, lambda i: (0, 0, 0)), pl.BlockSpec(wshape, lambda i: (0, 0, 0)),
                   pl.BlockSpec((1, S5_WIDTH), lambda i: (0, 0)), pl.BlockSpec((SUBLANES, S5_CPLX), lambda i: (0, 0))],
        out_shape=[jax.ShapeDtypeStruct((rows, S5_WIDTH), BF16), jax.ShapeDtypeStruct(wshape, F32),
                   jax.ShapeDtypeStruct(wshape, F32), jax.ShapeDtypeStruct((1, S5_WIDTH), F32),
                   jax.ShapeDtypeStruct((SUBLANES, S5_CPLX), F32)],
        scratch_shapes=[pltpu.VMEM((S5_TILES, tile, LANES), F32), pltpu.VMEM((S5_TILES, SUBLANES, LANES), F32)],
        compiler_params=pltpu.CompilerParams(dimension_semantics=("arbitrary",), vmem_limit_bytes=VMEM_LIMIT),
    )(dy, s, s, u, bd, cdt, dskip, sc)


def _s5_block_diag(parts):
    v = jnp.stack(parts, axis=2).reshape(2, 16, S5_GROUP, 2, S5_STATE)
    eye = jnp.eye(16, dtype=v.dtype)
    return jnp.einsum("hgcpn,gk->hgcpkn", v, eye).reshape(2, S5_HALF_IN, 2 * S5_HALF_CPLX)


def _s5_block_diag_extract(m):
    v = m.reshape(2, 16, S5_GROUP, 2, 16, S5_STATE)
    d = jnp.diagonal(v, axis1=1, axis2=4)
    d = jnp.transpose(d, (2, 0, 4, 1, 3)).reshape(2, S5_GROUPS, S5_GROUP, S5_STATE)
    return d[0], d[1]


def _cplx_to_lanes(v):
    return v.reshape(1, S5_CPLX)


def _lru_scan_fwd(a, b, *, rows, tile):
    n = rows // tile
    nblk = tile // SUBLANES
    group = 5

    def body(a_ref, b_ref, h_ref, carry_ref):
        @pl.when(pl.program_id(0) == 0)
        def _():
            carry_ref[...] = jnp.zeros_like(carry_ref)

        row = lax.broadcasted_iota(jnp.int32, (SUBLANES, LANES), 0)
        for q0 in range(0, LRU_WIDTH // LANES, group):
            offs = [q * LANES for q in range(q0, q0 + group)]

            def blk(t, carry, offs=offs):
                r0 = pl.multiple_of(t * SUBLANES, SUBLANES)
                new = []
                for j, o in enumerate(offs):
                    av = a_ref[pl.ds(r0, SUBLANES), o:o + LANES]
                    xv = b_ref[pl.ds(r0, SUBLANES), o:o + LANES]
                    for sh in (1, 2, 4):
                        m = row >= sh
                        xs = pltpu.roll(xv, sh, 0)
                        asft = pltpu.roll(av, sh, 0)
                        xv = xv + jnp.where(m, av * xs, 0.0)
                        av = jnp.where(m, av * asft, av)
                    hv = xv + av * carry[j]
                    h_ref[pl.ds(r0, SUBLANES), o:o + LANES] = hv
                    new.append(jnp.broadcast_to(hv[SUBLANES - 1:SUBLANES, :], (SUBLANES, LANES)))
                return tuple(new)

            carry = lax.fori_loop(0, nblk, blk, tuple(carry_ref[:, o:o + LANES] for o in offs), unroll=2)
            for j, o in enumerate(offs):
                carry_ref[:, o:o + LANES] = carry[j]

    spec = pl.BlockSpec((tile, LRU_WIDTH), lambda i: (i, 0))
    return pl.pallas_call(
        body, name="lru_scan_fwd", grid=(n,), in_specs=[spec, spec], out_specs=spec,
        out_shape=jax.ShapeDtypeStruct((rows, LRU_WIDTH), F32),
        scratch_shapes=[pltpu.VMEM((SUBLANES, LRU_WIDTH), F32)],
        compiler_params=pltpu.CompilerParams(dimension_semantics=("arbitrary",), vmem_limit_bytes=VMEM_LIMIT),
    )(a, b)


def _lru_scan_bwd(dh, a, *, rows, tile):
    n = rows // tile
    nblk = tile // SUBLANES
    group = 5

    def body(dh_ref, a_ref, g_ref, cg_ref, ca_ref):
        @pl.when(pl.program_id(0) == 0)
        def _():
            cg_ref[...] = jnp.zeros_like(cg_ref)
            ca_ref[...] = jnp.zeros_like(ca_ref)

        row = lax.broadcasted_iota(jnp.int32, (SUBLANES, LANES), 0)
        for q0 in range(0, LRU_WIDTH // LANES, group):
            offs = [q * LANES for q in range(q0, q0 + group)]

            def blk(t, carry, offs=offs):
                r0 = pl.multiple_of((nblk - 1 - t) * SUBLANES, SUBLANES)
                new = []
                for j, o in enumerate(offs):
                    cg, ca = carry[2 * j], carry[2 * j + 1]
                    araw = a_ref[pl.ds(r0, SUBLANES), o:o + LANES]
                    xv = dh_ref[pl.ds(r0, SUBLANES), o:o + LANES]
                    av = jnp.where(row == SUBLANES - 1, ca, pltpu.roll(araw, SUBLANES - 1, 0))
                    for sh in (1, 2, 4):
                        m = row <= SUBLANES - 1 - sh
                        xs = pltpu.roll(xv, SUBLANES - sh, 0)
                        asft = pltpu.roll(av, SUBLANES - sh, 0)
                        xv = xv + jnp.where(m, av * xs, 0.0)
                        av = jnp.where(m, av * asft, av)
                    gv = xv + av * cg
                    g_ref[pl.ds(r0, SUBLANES), o:o + LANES] = gv
                    new.append(jnp.broadcast_to(gv[0:1, :], (SUBLANES, LANES)))
                    new.append(jnp.broadcast_to(araw[0:1, :], (SUBLANES, LANES)))
                return tuple(new)

            carry0 = tuple(r[:, o:o + LANES] for o in offs for r in (cg_ref, ca_ref))
            carry = lax.fori_loop(0, nblk, blk, carry0, unroll=2)
            for j, o in enumerate(offs):
                cg_ref[:, o:o + LANES] = carry[2 * j]
                ca_ref[:, o:o + LANES] = carry[2 * j + 1]

    spec = pl.BlockSpec((tile, LRU_WIDTH), lambda i: (n - 1 - i, 0))
    return pl.pallas_call(
        body, name="lru_scan_bwd", grid=(n,), in_specs=[spec, spec], out_specs=spec,
        out_shape=jax.ShapeDtypeStruct((rows, LRU_WIDTH), F32),
        scratch_shapes=[pltpu.VMEM((SUBLANES, LRU_WIDTH), F32), pltpu.VMEM((SUBLANES, LRU_WIDTH), F32)],
        compiler_params=pltpu.CompilerParams(dimension_semantics=("arbitrary",), vmem_limit_bytes=VMEM_LIMIT),
    )(dh, a)


def _conv_fwd(i, x, prev, cw, cb):
    prev = prev * (i > 0).astype(F32)
    y = x * cw[3:4, :] + cb
    for s in range(1, CONV_WIDTH):
        y = y + _rows_before(x, prev, s) * cw[3 - s:4 - s, :]
    return y


def _lru_gates(c, wa, ba, wx, bx, lam):
    r = _sigmoid(_heads(_nn, c, wa) + ba)
    ig = _sigmoid(_heads(_nn, c, wx) + bx)
    z = -lam
    sp = jnp.maximum(z, 0.0) + jnp.log(1.0 + jnp.exp(-jnp.abs(z)))
    log_a = -LRU_C * r * sp
    a = jnp.exp(log_a)
    z2 = 2.0 * log_a
    series = -z2 * (1.0 + z2 * (0.5 + z2 * (1.0 / 6.0 + z2 * (1.0 / 24.0 + z2 * (1.0 / 120.0 + z2 / 720.0)))))
    one_minus = jnp.where(z2 > -0.2, series, 1.0 - jnp.exp(z2))
    mult = jnp.sqrt(one_minus)
    return r, ig, sp, a, mult


def _layer_fwd(x, p, w, rows, target=None):
    tile = WIDE_TILE
    d = D_MODEL

    def f_in(i, xb, g, *ws):
        rstd = lax.rsqrt(jnp.mean(xb * xb, axis=-1, keepdims=True) + NORM_EPS)
        hb = (xb * rstd * g).astype(BF16)
        proj = jnp.concatenate([jnp.dot(hb, wj, preferred_element_type=F32) for wj in ws], axis=1)
        return tuple(proj[:, IN_OFFSETS[k]:IN_OFFSETS[k + 1]] for k in range(6)) + (hb,)

    s5x, s5g, lrux, lrug, gs, gl, h = _rows(
        "f_in", f_in, [(x, "row"), (w["g_pre"], "full")] + w["w_in"],
        [((rows, wd), BF16, "row") for wd in IN_WIDTHS] + [((rows, d), BF16, "row")], rows=rows, tile=ROW_TILE)

    ys, st = _s5_fwd(s5x, w["bd"], w["cdt"], w["s5_d"], w["scf"], rows=rows, tile=ROW_TILE)

    def f_s5post(i, ysb, gb, wglu, wbs):
        glv, _ = _gelu_parts(ysb)
        glu = _nn(glv, wglu)
        y2 = glu[:, :S5_WIDTH] * _sigmoid(glu[:, S5_WIDTH:]) * (gb * _sigmoid(gb))
        return (_nn(y2, wbs),)

    (z_s,) = _rows("f_s5post", f_s5post, [(ys, "row"), (s5g, "row"), (w["w_glu"], "full"), (w["w_bs"], "full")],
                   [((rows, d), BF16, "row")], rows=rows, tile=tile)

    def f_gates(i, xb, prev, cw, cb, wa, ba, wx, bx, lam):
        c = _conv_fwd(i, xb, prev, cw, cb)
        _, ig, _, a, mult = _lru_gates(c, wa, ba, wx, bx, lam)
        return a, mult * (ig * c)

    a, b = _rows("f_gates", f_gates,
                 [(lrux, "row"), (lrux, "prev"), (w["conv_w"], "full"), (w["conv_b"], "full"), (w["lru_w_a"], "full"),
                  (w["lru_b_a"], "full"), (w["lru_w_x"], "full"), (w["lru_b_x"], "full"), (w["lru_lambda"], "full")],
                 [((rows, LRU_WIDTH), F32, "row")] * 2, rows=rows, tile=tile)
    hl = _lru_scan_fwd(a, b, rows=rows, tile=min(2 * tile, rows))

    def f_merge(i, hb, lg, zs, gsb, glb, xb, wbl, wout, gpost):
        z_l = _nn(hb * (lg * _sigmoid(lg)), wbl)
        merged = _sigmoid(gsb) * zs + _sigmoid(glb) * z_l
        mix = _nn(merged, wout)
        rstd = lax.rsqrt(jnp.mean(mix * mix, axis=-1, keepdims=True) + NORM_EPS)
        return xb + mix * rstd * gpost, mix, z_l

    x1, mix, z_l = _rows("f_merge", f_merge,
                         [(hl, "row"), (lrug, "row"), (z_s, "row"), (gs, "row"), (gl, "row"), (x, "row"),
                          (w["w_bl"], "full"), (w["w_out"], "full"), (w["g_post"], "full")],
                         [((rows, d), F32, "row"), ((rows, d), BF16, "row"), ((rows, d), BF16, "row")], rows=rows, tile=tile)

    saved = dict(x=x, h=h, s5x=s5x, s5g=s5g, lrux=lrux, lrug=lrug, gs=gs, gl=gl, ys=ys, st=st, a=a, hl=hl, z_s=z_s,
                 z_l=z_l, mix=mix, x1=x1, p=p)
    ple_ins = [(x1, "row"), (p, "row"), (w["w_ple"], "full"), (w["w_ple_gate"], "full")]
    if target is None:
        def f_ple(i, x1b, pb, wple, wpg):
            return (x1b + _nn(pb, wple) * _sigmoid(_nn(x1b, wpg)),)

        return _rows("f_ple", f_ple, ple_ins, [((rows, d), F32, "row")], rows=rows, tile=tile)[0], saved

    def f_ple_loss(i, x1b, pb, wple, wpg, tb):
        e = x1b + _nn(pb, wple) * _sigmoid(_nn(x1b, wpg)) - tb
        return e * (1.0 / D_MODEL), jnp.sum(jnp.sum(e * e, axis=0, keepdims=True), axis=1, keepdims=True)

    return _rows("f_ple_loss", f_ple_loss, ple_ins + [(target, "row")],
                 [((rows, d), F32, "row"), ((1, 1), F32, "acc")], rows=rows, tile=tile), saved


def _layer_bwd(dx2, sv, w, rows):
    tile = WIDE_TILE
    d = D_MODEL
    g = {}

    def b_ple(i, dxb, x1b, pb, wple, wpg):
        pe = _nn(pb, wple)
        sg = _sigmoid(_nn(x1b, wpg))
        dpe = dxb * sg
        dgt = dxb * pe * sg * (1.0 - sg)
        return dxb + _nt(dgt, wpg), _tn(pb, dpe), _tn(x1b, dgt)

    dx1, g["w_ple"], g["w_ple_gate"] = _rows(
        "b_ple", b_ple, [(dx2, "row"), (sv["x1"], "row"), (sv["p"], "row"), (w["w_ple"], "full"), (w["w_ple_gate"], "full")],
        [((rows, d), F32, "row"), ((PLE_DIM, d), F32, "acc"), ((d, d), F32, "acc")], rows=rows, tile=tile)

    def b_merge(i, dxb, mixb, zs, zl, gsb, glb, wout, gpost):
        rstd = lax.rsqrt(jnp.mean(mixb * mixb, axis=-1, keepdims=True) + NORM_EPS)
        nrm = mixb * rstd
        dn = dxb * gpost
        dmix = rstd * (dn - nrm * jnp.mean(dn * nrm, axis=-1, keepdims=True))
        ss, sl = _sigmoid(gsb), _sigmoid(glb)
        merged = ss * zs + sl * zl
        dm = _nt(dmix, wout)
        return (dm * ss, dm * sl, dm * zs * ss * (1.0 - ss), dm * zl * sl * (1.0 - sl),
                _tn(merged, dmix), jnp.sum(dxb * nrm, axis=0, keepdims=True))

    dz_s, dz_l, dgs, dgl, g["w_out"], g["g_post"] = _rows(
        "b_merge", b_merge,
        [(dx1, "row"), (sv["mix"], "row"), (sv["z_s"], "row"), (sv["z_l"], "row"), (sv["gs"], "row"), (sv["gl"], "row"),
         (w["w_out"], "full"), (w["g_post"], "full")],
        [((rows, d), BF16, "row")] * 4 + [((d, d), F32, "acc"), ((1, d), F32, "acc")], rows=rows, tile=tile)

    def b_bl(i, dzl, hb, lg, wbl):
        sl = _sigmoid(lg)
        silu = lg * sl
        dy3 = _nt(dzl, wbl)
        return dy3 * silu, dy3 * hb * sl * (1.0 + lg * (1.0 - sl)), _tn(hb * silu, dzl)

    dh, dlrug, g["w_bl"] = _rows(
        "b_bl", b_bl, [(dz_l, "row"), (sv["hl"], "row"), (sv["lrug"], "row"), (w["w_bl"], "full")],
        [((rows, LRU_WIDTH), F32, "row"), ((rows, LRU_WIDTH), BF16, "row"), ((LRU_WIDTH, d), F32, "acc")], rows=rows, tile=tile)

    gh = _lru_scan_bwd(dh, sv["a"], rows=rows, tile=min(2 * tile, rows))

    def b_gates(i, ghb, hb, hprev, xb, xprev, cw, cb, wa, ba, wx, bx, lam):
        c = _conv_fwd(i, xb, xprev, cw, cb)
        r, ig, sp, a, mult = _lru_gates(c, wa, ba, wx, bx, lam)
        h_before = _rows_before(hb, hprev * (i > 0).astype(F32), 1)
        da = ghb * h_before
        dmult = ghb * ig * c
        dlog_a = da * a - dmult * a * a / mult
        dpre_r = dlog_a * (-LRU_C) * sp * r * (1.0 - r)
        dpre_i = ghb * mult * c * ig * (1.0 - ig)
        dc = ghb * mult * ig + _heads(_nt, dpre_r, wa) + _heads(_nt, dpre_i, wx)
        dlam = jnp.sum(dlog_a * LRU_C * r, axis=0, keepdims=True) * _sigmoid(-lam)
        return (dc, _heads_tn(c, dpre_r), _heads_tn(c, dpre_i), jnp.sum(dpre_r, axis=0, keepdims=True),
                jnp.sum(dpre_i, axis=0, keepdims=True), dlam)

    hshape = (LRU_HEADS, LRU_HEAD_DIM, LRU_HEAD_DIM)
    dc, g["lru_w_a"], g["lru_w_x"], g["lru_b_a"], g["lru_b_x"], g["lru_lambda"] = _rows(
        "b_gates", b_gates,
        [(gh, "row"), (sv["hl"], "row"), (sv["hl"], "prev"), (sv["lrux"], "row"), (sv["lrux"], "prev"),
         (w["conv_w"], "full"), (w["conv_b"], "full"), (w["lru_w_a"], "full"), (w["lru_b_a"], "full"),
         (w["lru_w_x"], "full"), (w["lru_b_x"], "full"), (w["lru_lambda"], "full")],
        [((rows, LRU_WIDTH), BF16, "row"), (hshape, F32, "acc"), (hshape, F32, "acc")] + [((1, LRU_WIDTH), F32, "acc")] * 3,
        rows=rows, tile=tile)

    n_tiles = rows // min(tile, rows)

    def b_conv(i, dcb, dnext, xb, xprev, cw):
        dnext = dnext * (i < n_tiles - 1).astype(F32)
        xprev = xprev * (i > 0).astype(F32)
        dx = dcb * cw[3:4, :]
        dws = [jnp.sum(dcb * xb, axis=0, keepdims=True)]
        for s in range(1, CONV_WIDTH):
            dx = dx + _rows_after(dcb, dnext, s) * cw[3 - s:4 - s, :]
            dws.append(jnp.sum(dcb * _rows_before(xb, xprev, s), axis=0, keepdims=True))
        return dx, jnp.concatenate(dws[::-1], axis=0), jnp.sum(dcb, axis=0, keepdims=True)

    dlrux, g["conv_w"], g["conv_b"] = _rows(
        "b_conv", b_conv, [(dc, "row"), (dc, "next"), (sv["lrux"], "row"), (sv["lrux"], "prev"), (w["conv_w"], "full")],
        [((rows, LRU_WIDTH), BF16, "row"), ((CONV_WIDTH, LRU_WIDTH), F32, "acc"), ((1, LRU_WIDTH), F32, "acc")],
        rows=rows, tile=tile)

    def b_s5post(i, dzs, ysb, gb, wglu, wbs):
        glv, dgelu = _gelu_parts(ysb)
        glu = _nn(glv, wglu)
        ga, gb2 = glu[:, :S5_WIDTH], glu[:, S5_WIDTH:]
        sb = _sigmoid(gb2)
        sg = _sigmoid(gb)
        silu = gb * sg
        y2 = ga * sb * silu
        dy2 = _nt(dzs, wbs)
        dglu = jnp.concatenate([dy2 * sb * silu, dy2 * ga * silu * sb * (1.0 - sb)], axis=1)
        dg = dy2 * ga * sb * sg * (1.0 + gb * (1.0 - sg))
        return _nt(dglu, wglu) * dgelu, dg, _tn(y2, dzs), _tn(glv, dglu)

    dys, ds5g, g["w_bs"], g["w_glu"] = _rows(
        "b_s5post", b_s5post, [(dz_s, "row"), (sv["ys"], "row"), (sv["s5g"], "row"), (w["w_glu"], "full"), (w["w_bs"], "full")],
        [((rows, S5_WIDTH), F32, "row"), ((rows, S5_WIDTH), BF16, "row"), ((S5_WIDTH, d), F32, "acc"),
         ((S5_WIDTH, 2 * S5_WIDTH), F32, "acc")],
        rows=rows, tile=tile)

    ds5x, g["bd"], g["cdt"], g["s5_d"], g["abar"] = _s5_bwd(dys, sv["st"], sv["s5x"], w["bd"], w["cdt"], w["s5_d"],
                                                            w["scb"], rows=rows, tile=ROW_TILE)

    dcomps = [ds5x, ds5g, dlrux, dlrug, dgs, dgl]

    def b_in(i, xb, dx1b, gpre, *rest):
        dproj, ws = jnp.concatenate(rest[:6], axis=1), rest[6:]
        dh = _nt(dproj[:, :IN_SLOT], ws[0])
        for j in range(1, 4):
            dh = dh + _nt(dproj[:, j * IN_SLOT:(j + 1) * IN_SLOT], ws[j])
        rstd = lax.rsqrt(jnp.mean(xb * xb, axis=-1, keepdims=True) + NORM_EPS)
        nrm = xb * rstd
        dn = dh * gpre
        dx = rstd * (dn - nrm * jnp.mean(dn * nrm, axis=-1, keepdims=True))
        return dx1b + dx, jnp.sum(dh * nrm, axis=0, keepdims=True)

    dx, g["g_pre"] = _rows(
        "b_in", b_in, [(sv["x"], "row"), (dx1, "row"), (w["g_pre"], "full")] + [(dcv, "raw") for dcv in dcomps]
        + w["w_in"],
        [((rows, d), F32, "row"), ((1, d), F32, "acc")], rows=rows, tile=ROW_TILE)

    g["w_in"] = []
    for j in range(4):
        lo, hi = j * IN_SLOT, (j + 1) * IN_SLOT
        ks = [k for k in range(6) if IN_OFFSETS[k] < hi and IN_OFFSETS[k + 1] > lo]
        first = IN_OFFSETS[ks[0]]

        def b_win(i, hb, *parts, lo=lo, hi=hi, first=first):
            return (_tn(hb, jnp.concatenate(parts, axis=1)[:, lo - first:hi - first]),)

        g["w_in"].append(_rows("b_win", b_win, [(sv["h"], "raw")] + [(dcomps[k], "raw") for k in ks],
                               [((d, IN_SLOT), F32, "acc")], rows=rows, tile=4 * ROW_TILE)[0])
    return dx, g


SMALL = ("g_pre", "s5_a_re", "s5_a_im", "s5_log_dt", "s5_b_re", "s5_b_im", "s5_c_re", "s5_c_im", "s5_d", "conv_b",
         "lru_w_a", "lru_b_a", "lru_w_x", "lru_b_x", "lru_lambda", "g_post")
BIG = ("w_in", "w_glu", "w_bs", "conv_w", "w_bl", "w_out", "w_ple", "w_ple_gate")


def _bcast_groups(v):
    return jnp.broadcast_to(v[:, None, :], (S5_GROUPS, S5_GROUP, S5_STATE)).reshape(S5_WIDTH, S5_STATE)


def _s5_prep_inputs(wl):
    ldt = jnp.broadcast_to(wl["s5_log_dt"][:, None], (S5_GROUPS, S5_STATE))
    gcn = lambda b: jnp.transpose(b, (0, 2, 1)).reshape(S5_WIDTH, S5_STATE)
    return (_bcast_groups(wl["s5_a_re"]), _bcast_groups(wl["s5_a_im"]), _bcast_groups(ldt), gcn(wl["s5_b_re"]),
            gcn(wl["s5_b_im"]))


def _layer_weights(wl):
    w = {}
    w["w_in"] = [wc if isinstance(wc, tuple) else (wc, "full") for wc in wl["w_in"]]
    for k in ("w_glu", "w_bs", "w_bl", "w_out", "w_ple", "w_ple_gate"):
        w[k] = wl[k]
    w["conv_w"] = wl["conv_w"]
    for k in ("g_pre", "g_post", "s5_d", "conv_b", "lru_b_a", "lru_b_x", "lru_lambda"):
        w[k] = wl[k].reshape(1, -1)
    w["lru_w_a"] = wl["lru_w_a"].astype(BF16)
    w["lru_w_x"] = wl["lru_w_x"].astype(BF16)
    prep_in = _s5_prep_inputs(wl)
    abr, abi, bbr, bbi = _s5_prep(*prep_in)
    w["prep_in"] = prep_in
    shape3 = (S5_GROUPS, S5_GROUP, S5_STATE)
    w["bd"] = _s5_block_diag([bbr.reshape(shape3), bbi.reshape(shape3)]).astype(BF16)
    w["cdt"] = _s5_block_diag([wl["s5_c_re"], -wl["s5_c_im"]]).astype(BF16)
    abr_s = abr.reshape(shape3)[:, 0, :]
    abi_s = abi.reshape(shape3)[:, 0, :]
    w["scf"], w["scb"] = _s5_consts(_cplx_to_lanes(abr_s), _cplx_to_lanes(abi_s), ROW_TILE // SUBLANES)
    return w


def _layer_param_grads(g, w, wl):
    out = {}
    shape3 = (S5_GROUPS, S5_GROUP, S5_STATE)
    dbr, dbi = _s5_block_diag_extract(g["bd"])
    dcr, dci = _s5_block_diag_extract(g["cdt"])
    out["s5_c_re"], out["s5_c_im"] = dcr, -dci
    zeros = jnp.zeros(shape3, F32)
    dar = zeros.at[:, 0, :].set(g["abar"][0].reshape(S5_GROUPS, S5_STATE)).reshape(S5_WIDTH, S5_STATE)
    dai = zeros.at[:, 0, :].set(g["abar"][1].reshape(S5_GROUPS, S5_STATE)).reshape(S5_WIDTH, S5_STATE)
    cts = (dar, dai, dbr.reshape(S5_WIDTH, S5_STATE), dbi.reshape(S5_WIDTH, S5_STATE))
    d_are, d_aim, d_ldt, d_bre, d_bim = _s5_prep_bwd(*w["prep_in"], cts)
    out["s5_a_re"] = d_are.reshape(shape3).sum(axis=1)
    out["s5_a_im"] = d_aim.reshape(shape3).sum(axis=1)
    out["s5_log_dt"] = d_ldt.reshape(shape3).sum(axis=(1, 2))
    out["s5_b_re"] = jnp.transpose(d_bre.reshape(shape3), (0, 2, 1))
    out["s5_b_im"] = jnp.transpose(d_bim.reshape(shape3), (0, 2, 1))
    out["s5_d"] = g["s5_d"].reshape(-1)
    for k in ("g_pre", "g_post", "conv_b", "lru_b_a", "lru_b_x", "lru_lambda"):
        out[k] = g[k].reshape(-1)
    for k in ("lru_w_a", "lru_w_x", "conv_w", "w_in", "w_glu", "w_bs", "w_bl", "w_out", "w_ple", "w_ple_gate"):
        out[k] = g[k]
    return out


def _local_step(x, p, layers, target):
    rows = x.shape[0]
    ws = [_layer_weights(wl) for wl in layers]
    saved = []
    for i in range(DEPTH):
        x, sv = _layer_fwd(x, p[i], ws[i], rows, target if i == DEPTH - 1 else None)
        saved.append(sv)
    dx, sq = x
    loss = sq[0, 0] * (0.5 / D_MODEL)
    grads = [None] * DEPTH
    for i in reversed(range(DEPTH)):
        dx, g = _layer_bwd(dx, saved[i], ws[i], rows)
        grads[i] = _layer_param_grads(g, ws[i], layers[i])
    return loss, dx, grads


def _place():
    return lax.axis_index("x"), lax.axis_index("y"), lax.axis_index("c")


def _other_chips(x, y):
    return [(1 - x, y), (x, 1 - y), (1 - x, 1 - y)]


def _any_spec():
    return pl.BlockSpec(memory_space=pl.ANY)


def _gather_chips(name, vs, via_sibling):
    n = len(vs)
    halved = [via_sibling and v.shape[0] % (16 * 4 // v.dtype.itemsize) == 0 for v in vs]
    relayed = [h and v.shape[0] % (32 * 4 // v.dtype.itemsize) == 0 for h, v in zip(halved, vs)]
    SEMS = 7

    def body(*refs):
        v_refs, out_refs, send_sems, recv_sems = refs[:n], refs[n:2 * n], refs[2 * n], refs[2 * n + 1]
        x, y, c = _place()
        me = 2 * x + y
        chips = _other_chips(x, y)
        slots = [2 * cx + cy for cx, cy in chips]
        sibling = (x, y, 1 - c)

        def part(a, slot, hc, quarter=None):
            if not halved[a]:
                return out_refs[a].at[slot]
            half = vs[a].shape[0] // 2
            if quarter is None:
                return out_refs[a].at[slot, pl.ds(hc * half, half), :]
            return out_refs[a].at[slot, pl.ds(hc * half + quarter * (half // 2), half // 2), :]

        def own(a):
            if not halved[a]:
                return v_refs[a]
            half = vs[a].shape[0] // 2
            return v_refs[a].at[pl.ds(c * half, half), :]

        def copy(a, k, src, dst, to):
            return pltpu.make_async_remote_copy(src_ref=src, dst_ref=dst, send_sem=send_sems.at[SEMS * a + k],
                                                recv_sem=recv_sems.at[SEMS * a + k], device_id=to, device_id_type=MESH)

        def relay(a, k):
            src = part(a, slots[k], c, quarter=k)
            return copy(a, 2 + k, src, src, (*chips[1 - k], c))

        for a in range(n):
            for k in range(2 if relayed[a] else 3):
                copy(a, k, own(a), part(a, me, c), (*chips[k], c)).start()
        for a in range(n):
            for k in range(2 if relayed[a] else 3):
                copy(a, k, own(a), part(a, slots[k], c), (*chips[k], c)).wait_recv()
                if relayed[a]:
                    relay(a, k).start()
                if halved[a]:
                    copy(a, 4 + k, part(a, slots[k], c), part(a, slots[k], c), sibling).start()
        for a in range(n):
            if relayed[a]:
                for k in range(2):
                    copy(a, 2 + k, part(a, slots[2], c, quarter=k), part(a, slots[2], c, quarter=k), sibling).wait_recv()
                copy(a, 6, part(a, slots[2], c), part(a, slots[2], c), sibling).start()
        for a in range(n):
            for k in range(3):
                if halved[a]:
                    copy(a, 4 + k, own(a), part(a, slots[k], 1 - c), sibling).wait_recv()
                    copy(a, 4 + k, own(a), part(a, me, c), sibling).wait_send()
                if relayed[a] and k == 2:
                    for q in range(2):
                        relay(a, q).wait_send()
                else:
                    copy(a, k, own(a), part(a, me, c), sibling).wait_send()

    others = pl.pallas_call(
        body, name=name, out_shape=[jax.ShapeDtypeStruct((4,) + v.shape, v.dtype) for v in vs],
        in_specs=[_any_spec()] * n, out_specs=[_any_spec()] * n,
        scratch_shapes=[pltpu.SemaphoreType.DMA((SEMS * n,)), pltpu.SemaphoreType.DMA((SEMS * n,))],
    )(*vs)
    me = 2 * lax.axis_index("x") + lax.axis_index("y")
    return [lax.dynamic_update_slice(o, v[None], (me, 0, 0)) for o, v in zip(others, vs)]


def _rs_sibling(grs):
    n = len(grs)
    halves = [g.shape[1] // 2 for g in grs]

    def body(*refs):
        g_refs, got_refs, send_sems, recv_sems = refs[:n], refs[n:2 * n], refs[2 * n], refs[2 * n + 1]
        x, y, c = _place()
        copies = [pltpu.make_async_remote_copy(
            src_ref=g_refs[a].at[:, pl.ds((1 - c) * halves[a], halves[a]), :], dst_ref=got_refs[a],
            send_sem=send_sems.at[a], recv_sem=recv_sems.at[a], device_id=(x, y, 1 - c), device_id_type=MESH)
            for a in range(n)]
        for cp in copies:
            cp.start()
        for cp in copies:
            cp.wait()

    return pl.pallas_call(
        body, name="rs_sibling",
        out_shape=[jax.ShapeDtypeStruct((4, h, g.shape[2]), F32) for g, h in zip(grs, halves)],
        in_specs=[_any_spec()] * n, out_specs=[_any_spec()] * n,
        scratch_shapes=[pltpu.SemaphoreType.DMA((n,)), pltpu.SemaphoreType.DMA((n,))],
    )(*grs)


def _rs_add_sibling(pk, got):
    _, half, width = got.shape
    tile = _pack_tile(half, width)
    nb = half // tile

    def body(lo_ref, hi_ref, got_ref, a32_ref, a16_ref):
        mine = jnp.where(lax.axis_index("c") == 0, lo_ref[...], hi_ref[...])
        s = mine + got_ref[...]
        a32_ref[...] = s
        a16_ref[...] = s.astype(BF16)

    blk = lambda first: pl.BlockSpec((None, tile, width), lambda s, i: (s, first + i, 0))
    return pl.pallas_call(
        body, name="rs_add_sibling", grid=(4, nb), in_specs=[blk(0), blk(nb), blk(0)], out_specs=[blk(0), blk(0)],
        out_shape=[jax.ShapeDtypeStruct(got.shape, F32), jax.ShapeDtypeStruct(got.shape, BF16)],
        compiler_params=pltpu.CompilerParams(dimension_semantics=("arbitrary", "arbitrary"), vmem_limit_bytes=VMEM_LIMIT),
    )(pk, pk, got)


def _rs_chips(a16s):
    n = len(a16s)
    relayed = [a.shape[1] % (32 * 4 // a.dtype.itemsize) == 0 for a in a16s]
    SEMS = 6

    def body(*refs):
        a_refs, got_refs, stage_refs = refs[:n], refs[n:2 * n], refs[2 * n:3 * n]
        send_sems, recv_sems = refs[3 * n], refs[3 * n + 1]
        x, y, c = _place()
        chips = _other_chips(x, y)
        slots = [2 * cx + cy for cx, cy in chips]

        def copy(a, k, src, dst, to):
            return pltpu.make_async_remote_copy(src_ref=src, dst_ref=dst, send_sem=send_sems.at[SEMS * a + k],
                                                recv_sem=recv_sems.at[SEMS * a + k], device_id=(*to, c), device_id_type=MESH)

        def straight(a, k):
            return copy(a, k, a_refs[a].at[slots[k]], got_refs[a].at[k], chips[k])

        def staged(a, k):
            q = a16s[a].shape[1] // 2
            return copy(a, 2 + k, a_refs[a].at[slots[2], pl.ds(k * q, q), :], stage_refs[a].at[k], chips[k])

        def passed(a, k):
            q = a16s[a].shape[1] // 2
            return copy(a, 4 + k, stage_refs[a].at[k], got_refs[a].at[2, pl.ds(k * q, q), :], chips[1 - k])

        for a in range(n):
            for k in range(2):
                straight(a, k).start()
            if relayed[a]:
                for k in range(2):
                    staged(a, k).start()
            else:
                straight(a, 2).start()
        for a in range(n):
            if relayed[a]:
                for k in range(2):
                    staged(a, k).wait_recv()
                    passed(a, k).start()
        for a in range(n):
            for k in range(2):
                straight(a, k).wait()
            if relayed[a]:
                for k in range(2):
                    staged(a, k).wait_send()
                    passed(a, k).wait()
            else:
                straight(a, 2).wait()

    outs = pl.pallas_call(
        body, name="rs_chips",
        out_shape=[jax.ShapeDtypeStruct((3,) + a.shape[1:], a.dtype) for a in a16s]
        + [jax.ShapeDtypeStruct((2, a.shape[1] // 2, a.shape[2]), a.dtype) for a in a16s],
        in_specs=[_any_spec()] * n, out_specs=[_any_spec()] * (2 * n),
        scratch_shapes=[pltpu.SemaphoreType.DMA((SEMS * n,)), pltpu.SemaphoreType.DMA((SEMS * n,))],
    )(*a16s)
    return outs[:n]


def _swap_halves(vs):
    n = len(vs)

    def body(*refs):
        v_refs, out_refs, send_sems, recv_sems = refs[:n], refs[n:2 * n], refs[2 * n], refs[2 * n + 1]
        x, y, c = _place()

        def give(a, hc):
            return pltpu.make_async_remote_copy(src_ref=v_refs[a], dst_ref=out_refs[a].at[hc], send_sem=send_sems.at[a],
                                                recv_sem=recv_sems.at[a], device_id=(x, y, 1 - c), device_id_type=MESH)

        for a in range(n):
            give(a, c).start()
        for a in range(n):
            give(a, c).wait_send()
            give(a, 1 - c).wait_recv()

    others = pl.pallas_call(
        body, name="swap_halves", out_shape=[jax.ShapeDtypeStruct((2,) + v.shape, v.dtype) for v in vs],
        in_specs=[_any_spec()] * n, out_specs=[_any_spec()] * n,
        scratch_shapes=[pltpu.SemaphoreType.DMA((n,)), pltpu.SemaphoreType.DMA((n,))],
    )(*vs)
    c = lax.axis_index("c")
    return [lax.dynamic_update_slice(o, v[None], (c, 0, 0)).reshape(2 * v.shape[0], v.shape[1]) for o, v in zip(others, vs)]


WIDE = 1024


def _unpack(flat, shapes, align=1):
    out, off = [], 0
    for s in shapes:
        n = 1
        for q in s:
            n *= q
        out.append(flat[off:off + n].reshape(s))
        off += -(-n // align) * align
    return out


def _pack_tile(rows, width):
    most = (2 ** 21) // (4 * width)
    if rows <= most:
        return rows
    return max(t for t in range(16, most + 1, 16) if rows % t == 0)


def _flat_aligned(v, align):
    v = v.reshape(-1)
    return jnp.pad(v, (0, -v.shape[0] % align))


def _adamw(name, w, g, m, v, tile):
    def fn(i, wb, gb, mb, vb):
        m2 = ADAM_B1 * mb + (1.0 - ADAM_B1) * gb
        v2 = ADAM_B2 * vb + (1.0 - ADAM_B2) * (gb * gb)
        m_hat = m2 / (1.0 - ADAM_B1 ** ADAM_STEP)
        v_hat = v2 / (1.0 - ADAM_B2 ** ADAM_STEP)
        return -ADAM_LR * (m_hat / (jnp.sqrt(v_hat) + ADAM_EPS) + ADAM_WD * wb), m2, v2

    return _rows(name, fn, [(w, "row"), (g, "row"), (m, "row"), (v, "row")], [(w.shape, F32, "row")] * 3,
                 rows=w.shape[0], tile=tile)


def _as_2d(a):
    return a.reshape(-1, a.shape[-1])


def _adam_tile(rows):
    for t in (256, 184, 128, 64, 32, 16, 8):
        if rows % t == 0:
            return t
    return rows


def kernel(x, p, g_pre, w_in, s5_a_re, s5_a_im, s5_log_dt, s5_b_re, s5_b_im, s5_c_re, s5_c_im, s5_d, w_glu, w_bs, conv_w, conv_b, lru_w_a, lru_b_a, lru_w_x, lru_b_x, lru_lambda, w_bl, w_out, g_post, w_ple, w_ple_gate, loss_target, m_g_pre, m_w_in, m_s5_a_re, m_s5_a_im, m_s5_log_dt, m_s5_b_re, m_s5_b_im, m_s5_c_re, m_s5_c_im, m_s5_d, m_w_glu, m_w_bs, m_conv_w, m_conv_b, m_lru_w_a, m_lru_b_a, m_lru_w_x, m_lru_b_x, m_lru_lambda, m_w_bl, m_w_out, m_g_post, m_w_ple, m_w_ple_gate, v_g_pre, v_w_in, v_s5_a_re, v_s5_a_im, v_s5_log_dt, v_s5_b_re, v_s5_b_im, v_s5_c_re, v_s5_c_im, v_s5_d, v_w_glu, v_w_bs, v_conv_w, v_conv_b, v_lru_w_a, v_lru_b_a, v_lru_w_x, v_lru_b_x, v_lru_lambda, v_w_bl, v_w_out, v_g_post, v_w_ple, v_w_ple_gate):
    wts = dict(g_pre=g_pre, w_in=w_in, s5_a_re=s5_a_re, s5_a_im=s5_a_im, s5_log_dt=s5_log_dt, s5_b_re=s5_b_re,
               s5_b_im=s5_b_im, s5_c_re=s5_c_re, s5_c_im=s5_c_im, s5_d=s5_d, w_glu=w_glu, w_bs=w_bs, conv_w=conv_w,
               conv_b=conv_b, lru_w_a=lru_w_a, lru_b_a=lru_b_a, lru_w_x=lru_w_x, lru_b_x=lru_b_x, lru_lambda=lru_lambda,
               w_bl=w_bl, w_out=w_out, g_post=g_post, w_ple=w_ple, w_ple_gate=w_ple_gate)
    mom1 = dict(g_pre=m_g_pre, w_in=m_w_in, s5_a_re=m_s5_a_re, s5_a_im=m_s5_a_im, s5_log_dt=m_s5_log_dt, s5_b_re=m_s5_b_re,
                s5_b_im=m_s5_b_im, s5_c_re=m_s5_c_re, s5_c_im=m_s5_c_im, s5_d=m_s5_d, w_glu=m_w_glu, w_bs=m_w_bs,
                conv_w=m_conv_w, conv_b=m_conv_b, lru_w_a=m_lru_w_a, lru_b_a=m_lru_b_a, lru_w_x=m_lru_w_x, lru_b_x=m_lru_b_x,
                lru_lambda=m_lru_lambda, w_bl=m_w_bl, w_out=m_w_out, g_post=m_g_post, w_ple=m_w_ple, w_ple_gate=m_w_ple_gate)
    mom2 = dict(g_pre=v_g_pre, w_in=v_w_in, s5_a_re=v_s5_a_re, s5_a_im=v_s5_a_im, s5_log_dt=v_s5_log_dt, s5_b_re=v_s5_b_re,
                s5_b_im=v_s5_b_im, s5_c_re=v_s5_c_re, s5_c_im=v_s5_c_im, s5_d=v_s5_d, w_glu=v_w_glu, w_bs=v_w_bs,
                conv_w=v_conv_w, conv_b=v_conv_b, lru_w_a=v_lru_w_a, lru_b_a=v_lru_b_a, lru_w_x=v_lru_w_x, lru_b_x=v_lru_b_x,
                lru_lambda=v_lru_lambda, w_bl=v_w_bl, w_out=v_w_out, g_post=v_g_post, w_ple=v_w_ple, w_ple_gate=v_w_ple_gate)
    names = list(wts)

    by_rows, by_cols = ("w_bl", "w_out", "w_ple_gate"), ("w_glu", "w_bs", "w_ple")
    two_d = lambda k: wts[k].astype(BF16).reshape(-1, wts[k].shape[2])
    sent = [two_d("w_in"), jnp.concatenate([two_d(k) for k in by_rows]), jnp.concatenate([two_d(k) for k in by_cols]),
            wts["conv_w"].reshape(-1, wts["conv_w"].shape[2])]
    g_in, g_rows, g_cols, g_conv = _gather_chips("gather_weights", sent, via_sibling=True)
    whole = {"conv_w": jnp.transpose(g_conv.reshape((4,) + wts["conv_w"].shape), (1, 2, 0, 3)).reshape(DEPTH, CONV_WIDTH, -1)}
    off = 0
    for k in by_rows:
        dp, r, _ = wts[k].shape
        piece = g_rows[:, off:off + dp * r].reshape(4, dp, r, -1)
        whole[k] = jnp.transpose(piece, (1, 0, 2, 3)).reshape(dp, 4 * r, -1)
        off += dp * r
    off = 0
    for k in by_cols:
        dp, r, cs = wts[k].shape
        piece = g_cols[:, off:off + dp * r].reshape(4, dp, r, cs)
        whole[k] = jnp.transpose(piece, (1, 2, 0, 3)).reshape(dp, r, 4 * cs)
        off += dp * r
    layers = []
    for i in range(DEPTH):
        wl = {k: whole[k][i] for k in BIG if k != "w_in"}
        wl["w_in"] = [(g_in, ("part", j, i, wts["w_in"].shape[1])) for j in range(4)]
        wl.update({k: wts[k][i] for k in SMALL})
        layers.append(wl)

    loss, grad_x, grads = _local_step(x[0], p[:, 0], layers, loss_target[0])
    loss = lax.psum(loss, ("x", "y", "c"))

    me = 2 * lax.axis_index("x") + lax.axis_index("y")
    rows_of = lambda k, i, j: grads[i][k][j * (grads[i][k].shape[0] // 4):(j + 1) * (grads[i][k].shape[0] // 4)]
    cols_of = lambda k, i, j: grads[i][k][:, j * (grads[i][k].shape[1] // 4):(j + 1) * (grads[i][k].shape[1] // 4)]
    layer_range = range(DEPTH)
    packs = [
        jnp.stack([jnp.concatenate([grads[i]["w_in"][j] for i in layer_range]) for j in range(4)]),
        jnp.stack([jnp.concatenate([rows_of(k, i, j) for k in by_rows for i in layer_range]) for j in range(4)]),
        jnp.stack([jnp.concatenate([cols_of(k, i, j) for k in by_cols for i in layer_range]) for j in range(4)]),
    ]
    small_names = SMALL + ("conv_w",)
    small_shapes = [(DEPTH,) + grads[0][k].shape for k in small_names]
    small_flat = jnp.concatenate([_flat_aligned(jnp.stack([grads[i][k] for i in layer_range]), WIDE) for k in small_names])
    n_small = small_flat.shape[0]
    small_q = -(-n_small // (4 * 32 * LANES)) * 32 * LANES
    packs.append(jnp.pad(small_flat, (0, 4 * small_q - n_small)).reshape(4, small_q // LANES, LANES))

    gots = _rs_sibling(packs)
    a32s, a16s = [], []
    for pk, got in zip(packs, gots):
        a32, a16 = _rs_add_sibling(pk, got)
        a32s.append(a32)
        a16s.append(a16)
    got3s = _rs_chips(a16s)
    red_halves = []
    for a32, got3 in zip(a32s, got3s):
        half, width = a32.shape[1], a32.shape[2]
        own = lax.dynamic_index_in_dim(a32, me, 0, keepdims=False)

        def f_add2(i, o, g0, g1, g2):
            return (((o + g0) + g1) + g2,)

        red_halves.append(_rows("rs_add2", f_add2, [(own, "row")] + [(got3[k], "row") for k in range(3)],
                                [((half, width), F32, "row")], rows=half, tile=_pack_tile(half, width))[0])
    reds = _swap_halves(red_halves)
    small_red = _gather_chips("gather_small", [reds[3]], via_sibling=True)[0].reshape(-1)[:n_small]

    grad_out = {"w_in": reds[0].reshape(wts["w_in"].shape)}
    for red, ks in ((reds[1], by_rows), (reds[2], by_cols)):
        off = 0
        for k in ks:
            n = wts[k].shape[0] * wts[k].shape[1]
            grad_out[k] = red[off:off + n].reshape(wts[k].shape)
            off += n
    small_out = dict(zip(small_names, _unpack(small_red, small_shapes, align=WIDE)))
    grad_out.update({k: small_out[k] for k in SMALL})
    grad_out["conv_w"] = lax.dynamic_slice_in_dim(small_out["conv_w"], me * wts["conv_w"].shape[2], wts["conv_w"].shape[2], axis=2)
    delta, new_m, new_v = {}, {}, {}
    for k in BIG + SMALL:
        w2 = _as_2d(wts[k])
        res = _adamw("adamw_" + k, w2, _as_2d(grad_out[k]), _as_2d(mom1[k]), _as_2d(mom2[k]), _adam_tile(w2.shape[0]))
        delta[k], new_m[k], new_v[k] = [r.reshape(wts[k].shape) for r in res]
    return (loss, grad_x[None], *[grad_out[k] for k in names], *[delta[k] for k in names],
            *[new_m[k] for k in names], *[new_v[k] for k in names])
```
